```python
import math
import jax, jax.numpy as jnp
from jax import lax
import numpy as np

D_MODEL = 1024
BATCH = 2
SEQ = 8192
DEPTH = 2

CHUNK = 64
N_META = 16
META_PAD = (-N_META) % CHUNK
N_MIXERS = 2
N_A_LAYERS = (DEPTH + N_MIXERS - 1) // N_MIXERS
N_B_LAYERS = DEPTH // N_MIXERS
RMS_EPS = 1e-6
L2_EPS = 1e-6

GDN_HEADS = 8
GDN_DK = 128
GDN_DV = 128
GDN_CONV = 4
GDN_QK_DIM = GDN_HEADS * GDN_DK
GDN_V_DIM = GDN_HEADS * GDN_DV
GDN_CONV_DIM = 2 * GDN_QK_DIM + GDN_V_DIM
GDN_IN_DIM = GDN_CONV_DIM + GDN_V_DIM + 2 * GDN_HEADS

ATT_HEADS = 8
ATT_KV_HEADS = 2
ATT_GROUP = ATT_HEADS // ATT_KV_HEADS
ATT_HD = 128
ATT_Q_DIM = ATT_HEADS * ATT_HD
ATT_KV_DIM = ATT_KV_HEADS * ATT_HD
IDX_HEADS = 8
IDX_HD = 64
IDX_Q_DIM = IDX_HEADS * IDX_HD
TOPK_MAX = 256
Q_BLOCK = 128
DSA_IN_DIM = ATT_Q_DIM + 2 * ATT_KV_DIM + IDX_Q_DIM + IDX_HD + IDX_HEADS

REL_BUCKETS = 32
REL_MAX_DIST = 128

N_GROUPS = 4
EXPERTS_PER_GROUP = 8
N_EXPERTS = N_GROUPS * EXPERTS_PER_GROUP
EXPERT_FF = 256
TOPK_IN_GROUP = 2

kernel_name = 'hybrid_gdn_dsa_hmoe_stream'


def rms_norm(x, gain):
    xf = x.astype(jnp.float32)
    y = xf * lax.rsqrt(jnp.mean(xf * xf, axis=-1, keepdims=True) + RMS_EPS)
    return (y * gain.astype(jnp.float32)).astype(x.dtype)


def layer_norm(x, gain, bias):
    xf = x.astype(jnp.float32)
    mu = jnp.mean(xf, axis=-1, keepdims=True)
    xc = xf - mu
    y = xc * lax.rsqrt(jnp.mean(xc * xc, axis=-1, keepdims=True) + RMS_EPS)
    return (y * gain.astype(jnp.float32) + bias.astype(jnp.float32)).astype(x.dtype)


def l2_normalize(x):
    return x * lax.rsqrt(jnp.sum(x * x, axis=-1, keepdims=True) + L2_EPS)


def causal_conv(u, w):
    K = w.shape[0]
    L = u.shape[1]
    up = jnp.pad(u, ((0, 0), (K - 1, 0), (0, 0)))
    out = up[:, 0:L] * w[0]
    for j in range(1, K):
        out = out + up[:, j:j + L] * w[j]
    return out


def gated_delta_rule(q, k, v, beta, g):
    B, L, H, dk = q.shape
    n = L // CHUNK

    def chunks(t):
        t = t.reshape((B, n, CHUNK, H) + t.shape[3:])
        return jnp.moveaxis(t, 3, 1)

    q = chunks(q) * (dk ** -0.5)
    k = chunks(k)
    v = chunks(v)
    beta = chunks(beta)
    g = jnp.cumsum(chunks(g), axis=-1)
    causal = jnp.tril(jnp.ones((CHUNK, CHUNK), dtype=bool))
    strict = jnp.tril(jnp.ones((CHUNK, CHUNK), dtype=bool), k=-1)
    decay = jnp.exp(jnp.where(causal, g[..., :, None] - g[..., None, :], -jnp.inf))
    k_beta = k * beta[..., None]
    v_beta = v * beta[..., None]
    lower = jnp.where(strict, jnp.einsum('bhncd,bhnsd->bhncs', k_beta, k) * decay, 0.0)
    m = lower + jnp.eye(CHUNK, dtype=q.dtype)
    u = lax.linalg.triangular_solve(m, v_beta, left_side=True, lower=True, unit_diagonal=True)
    w = lax.linalg.triangular_solve(m, k_beta * jnp.exp(g)[..., None], left_side=True, lower=True,
                                    unit_diagonal=True)
    qk = jnp.where(causal, jnp.einsum('bhncd,bhnsd->bhncs', q, k) * decay, 0.0)
    q_decay = q * jnp.exp(g)[..., None]
    k_decay = k * jnp.exp(g[..., -1:] - g)[..., None]
    chunk_decay = jnp.exp(g[..., -1])

    def step(S, inp):
        qk_i, qd_i, kd_i, u_i, w_i, cd_i = inp
        v_new = u_i - jnp.einsum('bhck,bhkv->bhcv', w_i, S)
        o = jnp.einsum('bhck,bhkv->bhcv', qd_i, S) + jnp.einsum('bhcs,bhsv->bhcv', qk_i, v_new)
        S = S * cd_i[..., None, None] + jnp.einsum('bhck,bhcv->bhkv', kd_i, v_new)
        return S, o

    xs = tuple(jnp.moveaxis(t, 2, 0) for t in (qk, q_decay, k_decay, u, w, chunk_decay))
    S0 = jnp.zeros((B, H, dk, v.shape[-1]), q.dtype)
    _, o = lax.scan(step, S0, xs)
    return jnp.transpose(o, (1, 0, 3, 2, 4)).reshape(B, L, H, v.shape[-1])


def gated_deltanet(hn, w_in, conv_w, a_log, dt_bias, o_gain, w_out):
    B, L, _ = hn.shape
    Lp = L + META_PAD
    f32 = jnp.float32
    proj = jnp.pad(hn @ w_in, ((0, 0), (META_PAD, 0), (0, 0)))
    qkv = jax.nn.silu(causal_conv(proj[..., :GDN_CONV_DIM], conv_w)).astype(f32)
    q = l2_normalize(qkv[..., :GDN_QK_DIM].reshape(B, Lp, GDN_HEADS, GDN_DK))
    k = l2_normalize(qkv[..., GDN_QK_DIM:2 * GDN_QK_DIM].reshape(B, Lp, GDN_HEADS, GDN_DK))
    v = qkv[..., 2 * GDN_QK_DIM:].reshape(B, Lp, GDN_HEADS, GDN_DV)
    z = proj[:, META_PAD:, GDN_CONV_DIM:GDN_CONV_DIM + GDN_V_DIM].reshape(B, L, GDN_HEADS, GDN_DV)
    b = proj[..., GDN_CONV_DIM + GDN_V_DIM:GDN_CONV_DIM + GDN_V_DIM + GDN_HEADS].astype(f32)
    a = proj[..., GDN_CONV_DIM + GDN_V_DIM + GDN_HEADS:].astype(f32)
    beta = jax.nn.sigmoid(b)
    g = -jnp.exp(a_log.astype(f32)) * jax.nn.softplus(a + dt_bias.astype(f32))
    o = gated_delta_rule(q, k, v, beta, g)[:, META_PAD:]
    o = rms_norm(o, o_gain) * jax.nn.silu(z.astype(f32))
    return o.reshape(B, L, GDN_V_DIM).astype(hn.dtype) @ w_out


def t5_bucket(rel):
    nb = REL_BUCKETS // 2
    max_exact = nb // 2
    ret = jnp.where(rel > 0, nb, 0)
    n = jnp.abs(rel)
    nf = jnp.maximum(n, 1).astype(jnp.float32)
    large = max_exact + (jnp.log(nf / max_exact) / math.log(REL_MAX_DIST / max_exact)
                         * (nb - max_exact)).astype(jnp.int32)
    large = jnp.minimum(large, nb - 1)
    return ret + jnp.where(n < max_exact, n, large)


def dsa_attention(hn, w_in, q_gain, k_gain, idx_gain, idx_bias, rel_bias, w_out, n_select):
    B, L, _ = hn.shape
    f32 = jnp.float32
    proj = hn @ w_in
    cuts = [ATT_Q_DIM, ATT_Q_DIM + ATT_KV_DIM, ATT_Q_DIM + 2 * ATT_KV_DIM,
            ATT_Q_DIM + 2 * ATT_KV_DIM + IDX_Q_DIM, ATT_Q_DIM + 2 * ATT_KV_DIM + IDX_Q_DIM + IDX_HD]
    q, k, v, qi, ki, wi = jnp.split(proj, cuts, axis=-1)
    q = rms_norm(q.reshape(B, L, ATT_HEADS, ATT_HD), q_gain)
    k = rms_norm(k.reshape(B, L, ATT_KV_HEADS, ATT_HD), k_gain)
    v = v.reshape(B, L, ATT_KV_HEADS, ATT_HD)
    qi = qi.reshape(B, L, IDX_HEADS, IDX_HD).astype(f32)
    ki = layer_norm(ki, idx_gain, idx_bias).astype(f32)
    wi = wi.astype(f32) * (IDX_HEADS ** -0.5 * IDX_HD ** -0.5)
    key_chunk = (jnp.arange(L) + META_PAD) // CHUNK
    n_blk = -(-L // Q_BLOCK)
    Lq = n_blk * Q_BLOCK

    def to_blocks(t):
        t = jnp.pad(t, [(0, 0), (0, Lq - L)] + [(0, 0)] * (t.ndim - 2))
        return jnp.moveaxis(t.reshape((B, n_blk, Q_BLOCK) + t.shape[2:]), 1, 0)

    qpos = jnp.arange(Lq).reshape(n_blk, Q_BLOCK)
    gather = jax.vmap(lambda t, i: t[i])

    def block(args):
        qb, qib, wib, pb = args
        qchunk = (pb + META_PAD) // CHUNK
        admissible = key_chunk[None, :] <= qchunk[:, None]
        scores = jnp.einsum('bqh,bqhs->bqs', wib,
                            jax.nn.relu(jnp.einsum('bqhd,bsd->bqhs', qib, ki)))
        scores = jnp.where(admissible[None], scores, -jnp.inf)
        top_val, top_idx = lax.top_k(scores, n_select)
        valid = top_val > -jnp.inf
        k_sel = gather(k, top_idx).astype(f32)
        v_sel = gather(v, top_idx).astype(f32)
        qg = qb.astype(f32).reshape(B, Q_BLOCK, ATT_KV_HEADS, ATT_GROUP, ATT_HD)
        logits = jnp.einsum('bqngd,bqsnd->bqngs', qg, k_sel) * (ATT_HD ** -0.5)
        bias = rel_bias.astype(f32)[t5_bucket(top_idx - pb[None, :, None])]
        bias = jnp.transpose(bias.reshape(B, Q_BLOCK, n_select, ATT_KV_HEADS, ATT_GROUP), (0, 1, 3, 4, 2))
        logits = jnp.where(valid[:, :, None, None, :], logits + bias, -jnp.inf)
        p = jax.nn.softmax(logits, axis=-1)
        o = jnp.einsum('bqngs,bqsnd->bqngd', p, v_sel)
        return o.reshape(B, Q_BLOCK, ATT_Q_DIM).astype(hn.dtype)

    out = lax.map(block, (to_blocks(q), to_blocks(qi), to_blocks(wi), qpos))
    out = jnp.moveaxis(out, 0, 1).reshape(B, Lq, ATT_Q_DIM)[:, :L]
    return out @ w_out


def hier_moe(hn, w_group, b_group, w_expert, b_expert, w_gate_up, w_down):
    B, L, D = hn.shape
    f32 = jnp.float32
    t = hn.reshape(B * L, D)
    g_logits = (t @ w_group).astype(f32) + b_group.astype(f32)
    g_prob = jax.nn.softmax(g_logits, axis=-1)
    g_sel = jnp.argmax(g_logits, axis=-1)
    g_w = jnp.take_along_axis(g_prob, g_sel[:, None], axis=-1)
    e_logits = ((t @ w_expert).astype(f32) + b_expert.astype(f32)).reshape(B * L, N_GROUPS, EXPERTS_PER_GROUP)
    e_logits = jnp.take_along_axis(e_logits, g_sel[:, None, None], axis=1)[:, 0]
    top_p, top_i = lax.top_k(jax.nn.softmax(e_logits, axis=-1), TOPK_IN_GROUP)
    top_p = top_p / jnp.sum(top_p, axis=-1, keepdims=True)
    in_group = jnp.sum(jax.nn.one_hot(top_i, EXPERTS_PER_GROUP, dtype=f32) * top_p[..., None], axis=1)
    combine = (jax.nn.one_hot(g_sel, N_GROUPS, dtype=f32)[:, :, None]
               * (g_w * in_group)[:, None, :]).astype(t.dtype)
    y = jnp.zeros_like(t)
    for gi in range(N_GROUPS):
        h = jnp.einsum('td,edf->tef', t, w_gate_up[gi])
        act = jax.nn.silu(h[..., :EXPERT_FF]) * h[..., EXPERT_FF:] * combine[:, gi, :, None]
        y = y + jnp.einsum('tef,efd->td', act, w_down[gi])
    return y.reshape(B, L, D)


def setup_inputs(seed: int = 0) -> dict:
    key = jax.random.key(seed)
    ks = jax.random.split(key, 24)
    f32 = jnp.float32
    D = D_MODEL

    def nrm(k, shape, scale):
        return jax.random.normal(k, shape, f32) * scale

    dt = jnp.exp(jax.random.uniform(ks[8], (N_A_LAYERS, GDN_HEADS), f32, math.log(1e-3), math.log(1e-1)))
    return {
        'x': nrm(ks[0], (BATCH, SEQ, D), 1.0),
        'meta_tokens': nrm(ks[1], (N_META, D), 1.0),
        'norm_mix': 1.0 + nrm(ks[2], (DEPTH, D), 0.02),
        'norm_ffn': 1.0 + nrm(ks[3], (DEPTH, D), 0.02),
        'rel_bias': nrm(ks[4], (REL_BUCKETS, ATT_HEADS), 0.5),
        'gdn_w_in': nrm(ks[5], (N_A_LAYERS, D, GDN_IN_DIM), D ** -0.5),
        'gdn_conv': nrm(ks[6], (N_A_LAYERS, GDN_CONV, GDN_CONV_DIM), GDN_CONV ** -0.5),
        'gdn_a_log': jnp.log(jax.random.uniform(ks[7], (N_A_LAYERS, GDN_HEADS), f32, 1.0, 16.0)),
        'gdn_dt_bias': dt + jnp.log(-jnp.expm1(-dt)),
        'gdn_o_norm': 1.0 + nrm(ks[9], (N_A_LAYERS, GDN_DV), 0.02),
        'gdn_w_out': nrm(ks[10], (N_A_LAYERS, GDN_V_DIM, D), GDN_V_DIM ** -0.5),
        'dsa_w_in': nrm(ks[11], (N_B_LAYERS, D, DSA_IN_DIM), D ** -0.5),
        'dsa_q_norm': 1.0 + nrm(ks[12], (N_B_LAYERS, ATT_HD), 0.02),
        'dsa_k_norm': 1.0 + nrm(ks[13], (N_B_LAYERS, ATT_HD), 0.02),
        'dsa_idx_ln_g': 1.0 + nrm(ks[14], (N_B_LAYERS, IDX_HD), 0.02),
        'dsa_idx_ln_b': nrm(ks[15], (N_B_LAYERS, IDX_HD), 0.02),
        'dsa_w_out': nrm(ks[16], (N_B_LAYERS, ATT_Q_DIM, D), ATT_Q_DIM ** -0.5),
        'moe_w_group': nrm(ks[17], (DEPTH, D, N_GROUPS), D ** -0.5),
        'moe_b_group': nrm(ks[18], (DEPTH, N_GROUPS), 0.01),
        'moe_w_expert': nrm(ks[19], (DEPTH, D, N_EXPERTS), D ** -0.5),
        'moe_b_expert': nrm(ks[20], (DEPTH, N_EXPERTS), 0.01),
        'moe_w_gate_up': nrm(ks[21], (DEPTH, N_GROUPS, EXPERTS_PER_GROUP, D, 2 * EXPERT_FF), D ** -0.5),
        'moe_w_down': nrm(ks[22], (DEPTH, N_GROUPS, EXPERTS_PER_GROUP, EXPERT_FF, D), EXPERT_FF ** -0.5),
    }


def reference(x, meta_tokens, norm_mix, norm_ffn, rel_bias,
              gdn_w_in, gdn_conv, gdn_a_log, gdn_dt_bias, gdn_o_norm, gdn_w_out,
              dsa_w_in, dsa_q_norm, dsa_k_norm, dsa_idx_ln_g, dsa_idx_ln_b, dsa_w_out,
              moe_w_group, moe_b_group, moe_w_expert, moe_b_expert, moe_w_gate_up, moe_w_down):
    B, S, D = x.shape
    n_select = min(TOPK_MAX, S // 4)
    meta = jnp.broadcast_to(meta_tokens.astype(x.dtype)[None], (B, N_META, D))
    h = jnp.concatenate([meta, x], axis=1)
    for i in range(DEPTH):
        j = i // N_MIXERS
        hn = rms_norm(h, norm_mix[i])
        if i % N_MIXERS == 0:
            mix = gated_deltanet(hn, gdn_w_in[j], gdn_conv[j], gdn_a_log[j], gdn_dt_bias[j],
                                 gdn_o_norm[j], gdn_w_out[j])
        else:
            mix = dsa_attention(hn, dsa_w_in[j], dsa_q_norm[j], dsa_k_norm[j], dsa_idx_ln_g[j],
                                dsa_idx_ln_b[j], rel_bias, dsa_w_out[j], n_select)
        h = h + mix
        h = h + hier_moe(rms_norm(h, norm_ffn[i]), moe_w_group[i], moe_b_group[i], moe_w_expert[i],
                         moe_b_expert[i], moe_w_gate_up[i], moe_w_down[i])
    return h[:, N_META:]
```

```python
import functools
import math

import jax
import jax.numpy as jnp
from jax import lax
from jax.experimental import pallas as pl
from jax.experimental.pallas import tpu as pltpu

F32 = jnp.float32
BF16 = jnp.bfloat16

LANES = 128
SUBLANES = 8
VMEM_LIMIT_BYTES = 56 * 1024 * 1024

CHUNK = 64
N_META = 16
META_PAD = (-N_META) % CHUNK
RMS_EPS = 1e-6
L2_EPS = 1e-6

GDN_HEADS = 8
GDN_DK = 128
GDN_DV = 128
GDN_CONV = 4
GDN_QK_DIM = GDN_HEADS * GDN_DK
GDN_V_DIM = GDN_HEADS * GDN_DV
GDN_CONV_DIM = 2 * GDN_QK_DIM + GDN_V_DIM
GDN_GATE_OFF = GDN_CONV_DIM + GDN_V_DIM
GDN_PROJ_W = GDN_GATE_OFF + LANES

ATT_HEADS = 8
ATT_KV_HEADS = 2
ATT_GROUP = ATT_HEADS // ATT_KV_HEADS
ATT_HD = 128
ATT_Q_DIM = ATT_HEADS * ATT_HD
ATT_KV_DIM = ATT_KV_HEADS * ATT_HD
IDX_HEADS = 8
IDX_HD = 64
TOPK_MAX = 256
REL_BUCKETS = 32
REL_MAX_DIST = 128

DSA_Q_OFF = 0
DSA_K_OFF = ATT_Q_DIM
DSA_V_OFF = DSA_K_OFF + ATT_KV_DIM
DSA_QI_OFF = DSA_V_OFF + ATT_KV_DIM
DSA_KI_OFF = DSA_QI_OFF + IDX_HEADS * LANES
DSA_WI_OFF = DSA_KI_OFF + LANES
DSA_PROJ_W = DSA_WI_OFF + LANES

N_GROUPS = 4
EXPERTS_PER_GROUP = 8
N_EXPERTS = N_GROUPS * EXPERTS_PER_GROUP
EXPERT_FF = 256

ATT_TILE = 128
ROW_ALIGN = 128
NEG_INF = float("-inf")


def _pick_tile(n, candidates):
    for c in candidates:
        if n % c == 0:
            return c
    raise ValueError(f"no tile for {n}")


def _compiler_params(semantics):
    return pltpu.CompilerParams(dimension_semantics=semantics, vmem_limit_bytes=VMEM_LIMIT_BYTES)


def _silu(x):
    return x * jax.nn.sigmoid(x)


def _norm_matmul_kernel(x_ref, g_ref, w_ref, o_ref):
    x = x_ref[...]
    y = x * lax.rsqrt(jnp.mean(x * x, axis=-1, keepdims=True) + RMS_EPS) * g_ref[...]
    o_ref[...] = jnp.dot(y.astype(BF16), w_ref[...], preferred_element_type=F32)


def _norm_matmul(h2d, gain, w_bf16):
    T, D = h2d.shape
    N = w_bf16.shape[1]
    tm = _pick_tile(T, (256, 128))
    return pl.pallas_call(
        _norm_matmul_kernel,
        grid=(T // tm,),
        in_specs=[pl.BlockSpec((tm, D), lambda i: (i, 0)),
                  pl.BlockSpec((1, D), lambda i: (0, 0)),
                  pl.BlockSpec((D, N), lambda i: (0, 0))],
        out_specs=pl.BlockSpec((tm, N), lambda i: (i, 0)),
        out_shape=jax.ShapeDtypeStruct((T, N), F32),
        compiler_params=_compiler_params(("parallel",)),
        name="norm_matmul",
    )(h2d, gain.reshape(1, D), w_bf16)


def _matmul_residual_kernel(a_ref, w_ref, r_ref, o_ref):
    o_ref[...] = r_ref[...] + jnp.dot(a_ref[...], w_ref[...], preferred_element_type=F32)


def _matmul_residual(a_bf16, w_bf16, res):
    T, K = a_bf16.shape
    N = w_bf16.shape[1]
    tm = _pick_tile(T, (512, 256, 128))
    return pl.pallas_call(
        _matmul_residual_kernel,
        grid=(T // tm,),
        in_specs=[pl.BlockSpec((tm, K), lambda i: (i, 0)),
                  pl.BlockSpec((K, N), lambda i: (0, 0)),
                  pl.BlockSpec((tm, N), lambda i: (i, 0))],
        out_specs=pl.BlockSpec((tm, N), lambda i: (i, 0)),
        out_shape=jax.ShapeDtypeStruct((T, N), F32),
        compiler_params=_compiler_params(("parallel",)),
        name="matmul_residual",
    )(a_bf16, w_bf16, res)


def _nt_dot(a, b):
    return lax.dot_general(a, b, (((1,), (1,)), ((), ())), preferred_element_type=F32)


def _tn_dot(a, b):
    return lax.dot_general(a, b, (((0,), (0,)), ((), ())), preferred_element_type=F32)


def _gdn_kernel(proj_ref, conv_ref, gate_ref, ogain_ref, o_ref, state_ref, xbuf_ref):
    c = pl.program_id(1)
    C = CHUNK

    @pl.when(c == 0)
    def _():
        state_ref[...] = jnp.zeros_like(state_ref)
        xbuf_ref[0:SUBLANES, :] = jnp.zeros((SUBLANES, GDN_CONV_DIM), F32)

    row = c * C + lax.broadcasted_iota(jnp.int32, (C, 1), 0)
    live = row >= META_PAD
    u = jnp.where(live, proj_ref[0, :, 0:GDN_CONV_DIM], 0.0)
    xbuf_ref[SUBLANES:SUBLANES + C, :] = u
    conv = conv_ref[0:1, :] * xbuf_ref[SUBLANES - 3:SUBLANES - 3 + C, :]
    for j in range(1, GDN_CONV):
        conv = conv + conv_ref[j:j + 1, :] * xbuf_ref[SUBLANES - 3 + j:SUBLANES - 3 + j + C, :]
    xbuf_ref[0:SUBLANES, :] = u[C - SUBLANES:C, :]
    qkv = _silu(conv)

    gates = jnp.where(live, proj_ref[0, :, GDN_GATE_OFF:GDN_GATE_OFF + LANES], 0.0)
    beta = jax.nn.sigmoid(gates)
    neg_rate = -jnp.exp(gate_ref[0:1, :])
    g = neg_rate * jax.nn.softplus(gates + gate_ref[1:2, :])
    r_i = lax.broadcasted_iota(jnp.int32, (C, C), 0)
    c_i = lax.broadcasted_iota(jnp.int32, (C, C), 1)
    causal = c_i <= r_i
    strict = c_i < r_i
    gc = jnp.dot(causal.astype(F32), g, preferred_element_type=F32,
                 precision=lax.Precision.HIGHEST)
    gc_t = gc.T
    eg = jnp.exp(gc)
    g_last = gc[C - 1:C, :]
    eg_last = jnp.exp(g_last)
    ek = jnp.exp(g_last - gc)

    for h in range(GDN_HEADS):
        lo = h * GDN_DK
        q = qkv[:, lo:lo + GDN_DK]
        k = qkv[:, GDN_QK_DIM + lo:GDN_QK_DIM + lo + GDN_DK]
        v = qkv[:, 2 * GDN_QK_DIM + h * GDN_DV:2 * GDN_QK_DIM + (h + 1) * GDN_DV]
        q = q * lax.rsqrt(jnp.sum(q * q, axis=-1, keepdims=True) + L2_EPS) * (GDN_DK ** -0.5)
        k = k * lax.rsqrt(jnp.sum(k * k, axis=-1, keepdims=True) + L2_EPS)
        gh = GDN_HEADS + h
        beta_h = beta[:, h:h + 1]
        decay = jnp.exp(jnp.where(causal, gc[:, gh:gh + 1] - gc_t[gh:gh + 1, :], NEG_INF))
        kb = k * beta_h
        vb = v * beta_h
        k16 = k.astype(BF16)
        lower = jnp.where(strict, _nt_dot(kb.astype(BF16), k16) * decay, 0.0)
        x = jnp.concatenate([vb, kb * eg[:, gh:gh + 1]], axis=1)
        n = -lower
        for it in range(6):
            x = x + jnp.dot(n.astype(BF16), x.astype(BF16), preferred_element_type=F32)
            if it < 5:
                n = jnp.dot(n.astype(BF16), n.astype(BF16), preferred_element_type=F32)
        u_h = x[:, 0:GDN_DV]
        w_h = x[:, GDN_DV:2 * GDN_DV]
        qk = jnp.where(causal, _nt_dot(q.astype(BF16), k16) * decay, 0.0)
        s = state_ref[h]
        s16 = s.astype(BF16)
        v_new = u_h - jnp.dot(w_h.astype(BF16), s16, preferred_element_type=F32)
        q_dec = q * eg[:, gh:gh + 1]
        o = (jnp.dot(q_dec.astype(BF16), s16, preferred_element_type=F32)
             + jnp.dot(qk.astype(BF16), v_new.astype(BF16), preferred_element_type=F32))
        k_dec = k * ek[:, gh:gh + 1]
        state_ref[h] = s * eg_last[:, gh:gh + 1] + _tn_dot(k_dec.astype(BF16), v_new.astype(BF16))
        z = proj_ref[0, :, GDN_CONV_DIM + h * GDN_DV:GDN_CONV_DIM + (h + 1) * GDN_DV]
        on = o * lax.rsqrt(jnp.mean(o * o, axis=-1, keepdims=True) + RMS_EPS) * ogain_ref[...]
        o_ref[0, :, h * GDN_DV:(h + 1) * GDN_DV] = (on * _silu(z)).astype(o_ref.dtype)


def _gdn_core(proj, conv_w, a_log, dt_bias, o_gain):
    B, LP, W = proj.shape
    nc = LP // CHUNK
    gate = jnp.zeros((2, LANES), F32)
    gate = gate.at[0, GDN_HEADS:2 * GDN_HEADS].set(a_log.astype(F32))
    gate = gate.at[1, GDN_HEADS:2 * GDN_HEADS].set(dt_bias.astype(F32))
    return pl.pallas_call(
        _gdn_kernel,
        grid=(B, nc),
        in_specs=[pl.BlockSpec((1, CHUNK, W), lambda b, c: (b, c, 0)),
                  pl.BlockSpec((GDN_CONV, GDN_CONV_DIM), lambda b, c: (0, 0)),
                  pl.BlockSpec((2, LANES), lambda b, c: (0, 0)),
                  pl.BlockSpec((1, GDN_DV), lambda b, c: (0, 0))],
        out_specs=pl.BlockSpec((1, CHUNK, GDN_V_DIM), lambda b, c: (b, c, 0)),
        out_shape=jax.ShapeDtypeStruct((B, LP, GDN_V_DIM), BF16),
        scratch_shapes=[pltpu.VMEM((GDN_HEADS, GDN_DK, GDN_DV), F32),
                        pltpu.VMEM((SUBLANES + CHUNK, GDN_CONV_DIM), F32)],
        compiler_params=_compiler_params(("parallel", "arbitrary")),
        name="gdn_core",
    )(proj, conv_w, gate, o_gain.reshape(1, GDN_DV))


def _moe_kernel(h_ref, gain_ref, wr_ref, br_ref, wgu_ref, wd_ref, o_ref, xn_ref, comb_ref, acc_ref):
    e = pl.program_id(1)
    lane = lax.broadcasted_iota(jnp.int32, (1, LANES), 1)

    @pl.when(e == 0)
    def _():
        x = h_ref[...]
        xn = x * lax.rsqrt(jnp.mean(x * x, axis=-1, keepdims=True) + RMS_EPS) * gain_ref[...]
        xn_ref[...] = xn.astype(BF16)
        logits = jnp.dot(xn, wr_ref[...], preferred_element_type=F32,
                         precision=lax.Precision.HIGHEST) + br_ref[...]
        e_log = logits[:, 0:LANES]
        g_log = jnp.where(lane < N_GROUPS, logits[:, LANES:2 * LANES], NEG_INF)
        g_max = jnp.max(g_log, axis=-1, keepdims=True)
        g_sel = jnp.min(jnp.where(g_log == g_max, lane, LANES), axis=-1, keepdims=True)
        g_w = 1.0 / jnp.sum(jnp.exp(g_log - g_max), axis=-1, keepdims=True)
        in_grp = (lane >= g_sel * EXPERTS_PER_GROUP) & (lane < (g_sel + 1) * EXPERTS_PER_GROUP)
        l0 = jnp.where(in_grp, e_log, NEG_INF)
        m1 = jnp.max(l0, axis=-1, keepdims=True)
        i1 = jnp.min(jnp.where(l0 == m1, lane, LANES), axis=-1, keepdims=True)
        l1 = jnp.where(lane == i1, NEG_INF, l0)
        m2 = jnp.max(l1, axis=-1, keepdims=True)
        i2 = jnp.min(jnp.where(l1 == m2, lane, LANES), axis=-1, keepdims=True)
        p2 = jnp.exp(m2 - m1)
        w1 = g_w / (1.0 + p2)
        w2 = g_w * p2 / (1.0 + p2)
        comb_ref[...] = jnp.where(lane == i1, w1, 0.0) + jnp.where(lane == i2, w2, 0.0)
        acc_ref[...] = jnp.zeros_like(acc_ref)

    xn = xn_ref[...]
    hh = jnp.dot(xn, wgu_ref[0], preferred_element_type=F32)
    cw = jnp.sum(jnp.where(lane == e, comb_ref[...], 0.0), axis=-1, keepdims=True)
    act = _silu(hh[:, 0:EXPERT_FF]) * hh[:, EXPERT_FF:2 * EXPERT_FF] * cw
    acc_ref[...] += jnp.dot(act.astype(BF16), wd_ref[0], preferred_element_type=F32)

    @pl.when(e == N_EXPERTS - 1)
    def _():
        o_ref[...] = h_ref[...] + acc_ref[...]


def _moe(h2d, gain, w_group, b_group, w_expert, b_expert, w_gate_up, w_down):
    T, D = h2d.shape
    tm = _pick_tile(T, (1280, 640, 512, 256, 128))
    wr = jnp.zeros((D, 2 * LANES), F32)
    wr = wr.at[:, 0:N_EXPERTS].set(w_expert.astype(F32))
    wr = wr.at[:, LANES:LANES + N_GROUPS].set(w_group.astype(F32))
    br = jnp.zeros((1, 2 * LANES), F32)
    br = br.at[0, 0:N_EXPERTS].set(b_expert.astype(F32))
    br = br.at[0, LANES:LANES + N_GROUPS].set(b_group.astype(F32))
    wgu = w_gate_up.reshape(N_EXPERTS, D, 2 * EXPERT_FF).astype(BF16)
    wd = w_down.reshape(N_EXPERTS, EXPERT_FF, D).astype(BF16)
    return pl.pallas_call(
        _moe_kernel,
        grid=(T // tm, N_EXPERTS),
        in_specs=[pl.BlockSpec((tm, D), lambda i, e: (i, 0)),
                  pl.BlockSpec((1, D), lambda i, e: (0, 0)),
                  pl.BlockSpec((D, 2 * LANES), lambda i, e: (0, 0)),
                  pl.BlockSpec((1, 2 * LANES), lambda i, e: (0, 0)),
                  pl.BlockSpec((1, D, 2 * EXPERT_FF), lambda i, e: (e, 0, 0)),
                  pl.BlockSpec((1, EXPERT_FF, D), lambda i, e: (e, 0, 0))],
        out_specs=pl.BlockSpec((tm, D), lambda i, e: (i, 0)),
        out_shape=jax.ShapeDtypeStruct((T, D), F32),
        scratch_shapes=[pltpu.VMEM((tm, D), BF16),
                        pltpu.VMEM((tm, LANES), F32),
                        pltpu.VMEM((tm, D), F32)],
        compiler_params=_compiler_params(("parallel", "arbitrary")),
        name="moe",
    )(h2d, gain.reshape(1, D), wr, br, wgu, wd)


def _dsa_prep_kernel(p_ref, qg_ref, kg_ref, lng_ref, lnb_ref, q_ref, k_ref, v_ref, qi_ref, ki_ref, wi_ref):
    lane = lax.broadcasted_iota(jnp.int32, (1, LANES), 1)
    for h in range(ATT_HEADS):
        x = p_ref[:, DSA_Q_OFF + h * ATT_HD:DSA_Q_OFF + (h + 1) * ATT_HD]
        y = x * lax.rsqrt(jnp.mean(x * x, axis=-1, keepdims=True) + RMS_EPS) * qg_ref[...]
        q_ref[:, h * ATT_HD:(h + 1) * ATT_HD] = (y * (ATT_HD ** -0.5)).astype(BF16)
    for n in range(ATT_KV_HEADS):
        x = p_ref[:, DSA_K_OFF + n * ATT_HD:DSA_K_OFF + (n + 1) * ATT_HD]
        y = x * lax.rsqrt(jnp.mean(x * x, axis=-1, keepdims=True) + RMS_EPS) * kg_ref[...]
        k_ref[:, n * ATT_HD:(n + 1) * ATT_HD] = y.astype(BF16)
    v_ref[...] = p_ref[:, DSA_V_OFF:DSA_V_OFF + ATT_KV_DIM].astype(BF16)
    qi_ref[...] = p_ref[:, DSA_QI_OFF:DSA_QI_OFF + IDX_HEADS * LANES].astype(BF16)
    x = p_ref[:, DSA_KI_OFF:DSA_KI_OFF + LANES]
    live = lane < IDX_HD
    mu = jnp.sum(x, axis=-1, keepdims=True) * (1.0 / IDX_HD)
    xc = jnp.where(live, x - mu, 0.0)
    var = jnp.sum(xc * xc, axis=-1, keepdims=True) * (1.0 / IDX_HD)
    ki = xc * lax.rsqrt(var + RMS_EPS) * lng_ref[...] + lnb_ref[...]
    ki_ref[...] = jnp.where(live, ki, 0.0).astype(BF16)
    wi_ref[...] = p_ref[:, DSA_WI_OFF:DSA_WI_OFF + LANES] * (IDX_HEADS ** -0.5 * IDX_HD ** -0.5)


def _dsa_prep(proj2d, q_gain, k_gain, ln_g, ln_b):
    T, W = proj2d.shape
    tm = _pick_tile(T, (256, 128))
    lng = jnp.zeros((1, LANES), F32).at[0, 0:IDX_HD].set(ln_g.astype(F32))
    lnb = jnp.zeros((1, LANES), F32).at[0, 0:IDX_HD].set(ln_b.astype(F32))
    row = lambda i: (i, 0)
    fix = lambda i: (0, 0)
    return pl.pallas_call(
        _dsa_prep_kernel,
        grid=(T // tm,),
        in_specs=[pl.BlockSpec((tm, W), row),
                  pl.BlockSpec((1, ATT_HD), fix), pl.BlockSpec((1, ATT_HD), fix),
                  pl.BlockSpec((1, LANES), fix), pl.BlockSpec((1, LANES), fix)],
        out_specs=[pl.BlockSpec((tm, ATT_Q_DIM), row), pl.BlockSpec((tm, ATT_KV_DIM), row),
                   pl.BlockSpec((tm, ATT_KV_DIM), row), pl.BlockSpec((tm, IDX_HEADS * LANES), row),
                   pl.BlockSpec((tm, LANES), row), pl.BlockSpec((tm, LANES), row)],
        out_shape=[jax.ShapeDtypeStruct((T, ATT_Q_DIM), BF16), jax.ShapeDtypeStruct((T, ATT_KV_DIM), BF16),
                   jax.ShapeDtypeStruct((T, ATT_KV_DIM), BF16), jax.ShapeDtypeStruct((T, IDX_HEADS * LANES), BF16),
                   jax.ShapeDtypeStruct((T, LANES), BF16), jax.ShapeDtypeStruct((T, LANES), F32)],
        compiler_params=_compiler_params(("parallel",)),
        name="dsa_prep",
    )(proj2d, q_gain.reshape(1, ATT_HD), k_gain.reshape(1, ATT_HD), lng, lnb)


def _dsa_kernel(q_ref, qi_ref, wi_ref, k_ref, v_ref, ki_ref, bias_ref, o_ref,
                sc_ref, qg_ref, m_ref, l_ref, acc_ref, *, n_select, max_iters):
    i = pl.program_id(1)
    TQ = ATT_TILE
    TK = ATT_TILE
    nkt = i + 1
    fmin = float(jnp.finfo(F32).min)

    q_row = i * TQ + lax.broadcasted_iota(jnp.int32, (TQ, 1), 0)
    q_chunk = q_row // CHUNK
    k_lane = lax.broadcasted_iota(jnp.int32, (1, TK), 1)

    def admissible(kt):
        k_pos = kt * TK + k_lane
        return (k_pos // CHUNK <= q_chunk) & (k_pos >= META_PAD)

    wi = wi_ref[0]

    def score_tile(kt, carry):
        ki = ki_ref[0, pl.ds(pl.multiple_of(kt * TK, TK), TK), :]
        acc = jnp.zeros((TQ, TK), F32)
        for h in range(IDX_HEADS):
            d = _nt_dot(qi_ref[0, :, h * LANES:(h + 1) * LANES], ki)
            acc = acc + wi[:, h:h + 1] * jnp.maximum(d, 0.0)
        sc_ref[kt] = jnp.where(admissible(kt), acc, NEG_INF)
        return carry

    lax.fori_loop(0, nkt, score_tile, 0)

    n_adm = CHUNK * (q_chunk + 1) - META_PAD
    need = n_adm > n_select
    kf = float(n_select)

    def count_where(pred_fn):
        def body(kt, cnt):
            return cnt + jnp.where(pred_fn(sc_ref[kt], kt), 1.0, 0.0)
        lanes = lax.fori_loop(0, nkt, body, jnp.zeros((TQ, TK), F32))
        return jnp.sum(lanes, axis=-1, keepdims=True)

    def minmax_body(kt, carry):
        mn, mx = carry
        s = sc_ref[kt]
        return (jnp.minimum(mn, jnp.where(s == NEG_INF, jnp.inf, s)), jnp.maximum(mx, s))

    mn_l, mx_l = lax.fori_loop(0, nkt, minmax_body,
                               (jnp.full((TQ, TK), jnp.inf, F32), jnp.full((TQ, TK), NEG_INF, F32)))
    s_min = jnp.min(mn_l, axis=-1, keepdims=True)
    s_max = jnp.max(mx_l, axis=-1, keepdims=True)
    lo0 = s_min
    hi0 = s_max + (s_max - s_min) + 1.0
    cnt0 = n_adm.astype(F32)

    def bis_cond(st):
        it, lo, hi, cnt, active = st
        return jnp.logical_and(it < max_iters, jnp.max(active) > 0.0)

    def bis_body(st):
        it, lo, hi, cnt, active = st
        mid = lo + (hi - lo) * 0.5
        c = count_where(lambda s, kt: s >= mid)
        ge = c >= kf
        moving = (mid > lo) & (mid < hi) & (active > 0.0)
        lo_n = jnp.where(moving & ge, mid, lo)
        hi_n = jnp.where(moving & (~ge), mid, hi)
        cnt_n = jnp.where(moving & ge, c, cnt)
        active_n = jnp.where(moving & (cnt_n != kf), 1.0, 0.0)
        return it + 1, lo_n, hi_n, cnt_n, active_n

    active0 = jnp.where(need, 1.0, 0.0)
    _, lo, hi, cnt, _ = lax.while_loop(bis_cond, bis_body, (jnp.int32(0), lo0, hi0, cnt0, active0))
    thr = jnp.where(need, lo, fmin)
    tied = need & (cnt > kf)
    big_pos = jnp.int32(2 ** 30)

    def tie_break():
        c_gt = count_where(lambda s, kt: s > thr)
        want = kf - c_gt

        def body(_, st):
            jl, jh = st
            jm = (jl + jh) // 2
            c = count_where(lambda s, kt: (s == thr) & (kt * TK + k_lane <= jm))
            ok = c >= want
            return jnp.where(ok, jl, jm), jnp.where(ok, jm, jh)

        n_keys = sc_ref.shape[0] * TK
        steps = max(1, int(math.ceil(math.log2(n_keys))) + 1)
        jl0 = jnp.full((TQ, 1), -1, jnp.int32)
        jh0 = jnp.full((TQ, 1), n_keys, jnp.int32)
        _, jh = lax.fori_loop(0, steps, body, (jl0, jh0))
        return jnp.where(tied, jh, big_pos)

    pos_cap = lax.cond(jnp.max(jnp.where(tied, 1.0, 0.0)) > 0.0, tie_break,
                       lambda: jnp.full((TQ, 1), big_pos, jnp.int32))

    def selected(kt):
        s = sc_ref[kt]
        return (s > thr) | ((s == thr) & (kt * TK + k_lane <= pos_cap))

    for n in range(ATT_KV_HEADS):
        for g in range(ATT_GROUP):
            h = n * ATT_GROUP + g
            qg_ref[n, g * TQ:(g + 1) * TQ, :] = q_ref[0, :, h * ATT_HD:(h + 1) * ATT_HD]
    m_ref[...] = jnp.full(m_ref.shape, -1e30, F32)
    l_ref[...] = jnp.zeros_like(l_ref)
    acc_ref[...] = jnp.zeros_like(acc_ref)

    def attend(kt, bias_slot):
        sel = selected(kt)
        start = pl.multiple_of(kt * TK, TK)
        for n in range(ATT_KV_HEADS):
            kk = k_ref[0, pl.ds(start, TK), n * ATT_HD:(n + 1) * ATT_HD]
            vv = v_ref[0, pl.ds(start, TK), n * ATT_HD:(n + 1) * ATT_HD]
            for g in range(ATT_GROUP):
                h = n * ATT_GROUP + g
                rows = slice(g * TQ, (g + 1) * TQ)
                s = _nt_dot(qg_ref[n, rows, :], kk)
                if bias_slot is not None:
                    s = s + bias_ref[h, :, bias_slot * TK:(bias_slot + 1) * TK]
                s = jnp.where(sel, s, NEG_INF)
                m_old = m_ref[n, rows, :]
                m_new = jnp.maximum(m_old, jnp.max(s, axis=-1, keepdims=True))
                alpha = jnp.exp(m_old - m_new)
                p = jnp.exp(s - m_new)
                l_ref[n, rows, :] = alpha * l_ref[n, rows, :] + jnp.sum(p, axis=-1, keepdims=True)
                acc_ref[n, rows, :] = alpha * acc_ref[n, rows, :] + jnp.dot(
                    p.astype(BF16), vv, preferred_element_type=F32)
                m_ref[n, rows, :] = m_new

    def far_body(kt, carry):
        attend(kt, None)
        return carry

    lax.fori_loop(0, jnp.maximum(nkt - 2, 0), far_body, 0)

    @pl.when(i >= 1)
    def _():
        attend(i - 1, 0)

    attend(i, 1)

    for n in range(ATT_KV_HEADS):
        for g in range(ATT_GROUP):
            h = n * ATT_GROUP + g
            rows = slice(g * TQ, (g + 1) * TQ)
            o_ref[0, :, h * ATT_HD:(h + 1) * ATT_HD] = (acc_ref[n, rows, :] / l_ref[n, rows, :]).astype(o_ref.dtype)


def _t5_bucket(rel):
    nb = REL_BUCKETS // 2
    max_exact = nb // 2
    ret = jnp.where(rel > 0, nb, 0)
    n = jnp.abs(rel)
    nf = jnp.maximum(n, 1).astype(jnp.float32)
    large = max_exact + (jnp.log(nf / max_exact) / math.log(REL_MAX_DIST / max_exact)
                         * (nb - max_exact)).astype(jnp.int32)
    large = jnp.minimum(large, nb - 1)
    return ret + jnp.where(n < max_exact, n, large)


def _dsa_core(q, k, v, qi, ki, wi, rel_bias, n_select):
    B, LP, _ = q.shape
    nq = LP // ATT_TILE
    a = jnp.arange(ATT_TILE)[:, None]
    cc = jnp.arange(2 * ATT_TILE)[None, :]
    rel = (cc - ATT_TILE) - a
    far = rel_bias.astype(F32)[REL_BUCKETS // 2 - 1]
    bias_near = jnp.transpose(rel_bias.astype(F32)[_t5_bucket(rel)] - far, (2, 0, 1))
    kern = functools.partial(_dsa_kernel, n_select=n_select, max_iters=200)
    tile = lambda b, i: (b, i, 0)
    whole = lambda b, i: (b, 0, 0)
    return pl.pallas_call(
        kern,
        grid=(B, nq),
        in_specs=[pl.BlockSpec((1, ATT_TILE, ATT_Q_DIM), tile),
                  pl.BlockSpec((1, ATT_TILE, IDX_HEADS * LANES), tile),
                  pl.BlockSpec((1, ATT_TILE, LANES), tile),
                  pl.BlockSpec((1, LP, ATT_KV_DIM), whole),
                  pl.BlockSpec((1, LP, ATT_KV_DIM), whole),
                  pl.BlockSpec((1, LP, LANES), whole),
                  pl.BlockSpec((ATT_HEADS, ATT_TILE, 2 * ATT_TILE), lambda b, i: (0, 0, 0))],
        out_specs=pl.BlockSpec((1, ATT_TILE, ATT_Q_DIM), tile),
        out_shape=jax.ShapeDtypeStruct((B, LP, ATT_Q_DIM), BF16),
        scratch_shapes=[pltpu.VMEM((nq, ATT_TILE, ATT_TILE), F32),
                        pltpu.VMEM((ATT_KV_HEADS, ATT_GROUP * ATT_TILE, ATT_HD), BF16),
                        pltpu.VMEM((ATT_KV_HEADS, ATT_GROUP * ATT_TILE, 1), F32),
                        pltpu.VMEM((ATT_KV_HEADS, ATT_GROUP * ATT_TILE, 1), F32),
                        pltpu.VMEM((ATT_KV_HEADS, ATT_GROUP * ATT_TILE, ATT_HD), F32)],
        compiler_params=_compiler_params(("parallel", "arbitrary")),
        name="dsa_core",
    )(q, qi, wi, k, v, ki, bias_near)


def _gdn_in_weight(w_in):
    D = w_in.shape[0]
    w = jnp.zeros((D, GDN_PROJ_W), F32).at[:, 0:w_in.shape[1]].set(w_in.astype(F32))
    return w.astype(BF16)


def _dsa_in_weight(w_in):
    D = w_in.shape[0]
    w = jnp.zeros((D, DSA_PROJ_W), F32)
    w = w.at[:, 0:DSA_QI_OFF].set(w_in[:, 0:DSA_QI_OFF].astype(F32))
    src = ATT_Q_DIM + 2 * ATT_KV_DIM
    for h in range(IDX_HEADS):
        w = w.at[:, DSA_QI_OFF + h * LANES:DSA_QI_OFF + h * LANES + IDX_HD].set(
            w_in[:, src + h * IDX_HD:src + (h + 1) * IDX_HD].astype(F32))
    src += IDX_HEADS * IDX_HD
    w = w.at[:, DSA_KI_OFF:DSA_KI_OFF + IDX_HD].set(w_in[:, src:src + IDX_HD].astype(F32))
    src += IDX_HD
    w = w.at[:, DSA_WI_OFF:DSA_WI_OFF + IDX_HEADS].set(w_in[:, src:src + IDX_HEADS].astype(F32))
    return w.astype(BF16)


def kernel(x, meta_tokens, norm_mix, norm_ffn, rel_bias, gdn_w_in, gdn_conv, gdn_a_log, gdn_dt_bias,
           gdn_o_norm, gdn_w_out, dsa_w_in, dsa_q_norm, dsa_k_norm, dsa_idx_ln_g, dsa_idx_ln_b, dsa_w_out,
           moe_w_group, moe_b_group, moe_w_expert, moe_b_expert, moe_w_gate_up, moe_w_down):
    B, S, D = x.shape
    depth = norm_mix.shape[0]
    n_select = min(TOPK_MAX, S // 4)
    off = META_PAD + N_META
    LP = -(-(off + S) // ROW_ALIGN) * ROW_ALIGN
    meta = jnp.broadcast_to(meta_tokens.astype(x.dtype)[None], (B, N_META, D))
    h = jnp.concatenate([jnp.zeros((B, META_PAD, D), x.dtype), meta, x,
                         jnp.zeros((B, LP - off - S, D), x.dtype)], axis=1)
    h = h.reshape(B * LP, D)
    for i in range(depth):
        j = i // 2
        if i % 2 == 0:
            proj = _norm_matmul(h, norm_mix[i], _gdn_in_weight(gdn_w_in[j]))
            o = _gdn_core(proj.reshape(B, LP, GDN_PROJ_W), gdn_conv[j].astype(F32), gdn_a_log[j],
                          gdn_dt_bias[j], gdn_o_norm[j])
            h = _matmul_residual(o.reshape(B * LP, GDN_V_DIM), gdn_w_out[j].astype(BF16), h)
        else:
            proj = _norm_matmul(h, norm_mix[i], _dsa_in_weight(dsa_w_in[j]))
            q, k, v, qi, ki, wi = _dsa_prep(proj, dsa_q_norm[j], dsa_k_norm[j], dsa_idx_ln_g[j], dsa_idx_ln_b[j])
            r3 = lambda t: t.reshape(B, LP, t.shape[-1])
            o = _dsa_core(r3(q), r3(k), r3(v), r3(qi), r3(ki), r3(wi), rel_bias, n_select)
            h = _matmul_residual(o.reshape(B * LP, ATT_Q_DIM), dsa_w_out[j].astype(BF16), h)
        h = _moe(h, norm_ffn[i], moe_w_group[i], moe_b_group[i], moe_w_expert[i], moe_b_expert[i],
                 moe_w_gate_up[i], moe_w_down[i])
    return h.reshape(B, LP, D)[:, off:off + S]
```

```python
import functools
import math

import jax
import jax.numpy as jnp
from jax import lax
from jax.experimental import pallas as pl
from jax.experimental.pallas import tpu as pltpu

F32 = jnp.float32
BF16 = jnp.bfloat16

LANES = 128
SUBLANES = 8
VMEM_LIMIT_BYTES = 56 * 1024 * 1024

CHUNK = 64
N_META = 16
META_PAD = (-N_META) % CHUNK
RMS_EPS = 1e-6
L2_EPS = 1e-6

GDN_HEADS = 8
GDN_DK = 128
GDN_DV = 128
GDN_CONV = 4
GDN_QK_DIM = GDN_HEADS * GDN_DK
GDN_V_DIM = GDN_HEADS * GDN_DV
GDN_CONV_DIM = 2 * GDN_QK_DIM + GDN_V_DIM
GDN_GATE_OFF = GDN_CONV_DIM + GDN_V_DIM
GDN_PROJ_W = GDN_GATE_OFF + LANES

ATT_HEADS = 8
ATT_KV_HEADS = 2
ATT_GROUP = ATT_HEADS // ATT_KV_HEADS
ATT_HD = 128
ATT_Q_DIM = ATT_HEADS * ATT_HD
ATT_KV_DIM = ATT_KV_HEADS * ATT_HD
IDX_HEADS = 8
IDX_HD = 64
TOPK_MAX = 256
REL_BUCKETS = 32
REL_MAX_DIST = 128

DSA_Q_OFF = 0
DSA_K_OFF = ATT_Q_DIM
DSA_V_OFF = DSA_K_OFF + ATT_KV_DIM
DSA_QI_OFF = DSA_V_OFF + ATT_KV_DIM
DSA_KI_OFF = DSA_QI_OFF + IDX_HEADS * LANES
DSA_WI_OFF = DSA_KI_OFF + LANES
DSA_PROJ_W = DSA_WI_OFF + LANES

N_GROUPS = 4
EXPERTS_PER_GROUP = 8
N_EXPERTS = N_GROUPS * EXPERTS_PER_GROUP
EXPERT_FF = 256

ATT_TILE = 128
KEY_TILE = 256
ROW_ALIGN = KEY_TILE
BISECT_STEPS = 26
NEG_INF = float("-inf")


def _pick_tile(n, candidates):
    for c in candidates:
        if n % c == 0:
            return c
    raise ValueError(f"no tile for {n}")


def _compiler_params(semantics):
    return pltpu.CompilerParams(dimension_semantics=semantics, vmem_limit_bytes=VMEM_LIMIT_BYTES)


def _silu(x):
    return x * jax.nn.sigmoid(x)


def _norm_matmul_kernel(x_ref, g_ref, w_ref, o_ref):
    x = x_ref[...]
    y = x * lax.rsqrt(jnp.mean(x * x, axis=-1, keepdims=True) + RMS_EPS) * g_ref[...]
    o_ref[...] = jnp.dot(y.astype(BF16), w_ref[...], preferred_element_type=F32)


def _norm_matmul(h2d, gain, w_bf16):
    T, D = h2d.shape
    N = w_bf16.shape[1]
    tm = _pick_tile(T, (256, 128))
    return pl.pallas_call(
        _norm_matmul_kernel,
        grid=(T // tm,),
        in_specs=[pl.BlockSpec((tm, D), lambda i: (i, 0)),
                  pl.BlockSpec((1, D), lambda i: (0, 0)),
                  pl.BlockSpec((D, N), lambda i: (0, 0))],
        out_specs=pl.BlockSpec((tm, N), lambda i: (i, 0)),
        out_shape=jax.ShapeDtypeStruct((T, N), F32),
        compiler_params=_compiler_params(("parallel",)),
        name="norm_matmul",
    )(h2d, gain.reshape(1, D), w_bf16)


def _matmul_residual_kernel(a_ref, w_ref, r_ref, o_ref):
    o_ref[...] = r_ref[...] + jnp.dot(a_ref[...], w_ref[...], preferred_element_type=F32)


def _matmul_residual(a_bf16, w_bf16, res):
    T, K = a_bf16.shape
    N = w_bf16.shape[1]
    tm = _pick_tile(T, (512, 256, 128))
    return pl.pallas_call(
        _matmul_residual_kernel,
        grid=(T // tm,),
        in_specs=[pl.BlockSpec((tm, K), lambda i: (i, 0)),
                  pl.BlockSpec((K, N), lambda i: (0, 0)),
                  pl.BlockSpec((tm, N), lambda i: (i, 0))],
        out_specs=pl.BlockSpec((tm, N), lambda i: (i, 0)),
        out_shape=jax.ShapeDtypeStruct((T, N), F32),
        compiler_params=_compiler_params(("parallel",)),
        name="matmul_residual",
    )(a_bf16, w_bf16, res)


def _nt_dot(a, b):
    return lax.dot_general(a, b, (((1,), (1,)), ((), ())), preferred_element_type=F32)


def _tn_dot(a, b):
    return lax.dot_general(a, b, (((0,), (0,)), ((), ())), preferred_element_type=F32)


def _gdn_kernel(proj_ref, conv_ref, gate_ref, ogain_ref, o_ref, state_ref, xbuf_ref):
    c = pl.program_id(1)
    C = CHUNK

    @pl.when(c == 0)
    def _():
        state_ref[...] = jnp.zeros_like(state_ref)
        xbuf_ref[0:SUBLANES, :] = jnp.zeros((SUBLANES, GDN_CONV_DIM), F32)

    row = c * C + lax.broadcasted_iota(jnp.int32, (C, 1), 0)
    live = row >= META_PAD
    u = jnp.where(live, proj_ref[0, :, 0:GDN_CONV_DIM], 0.0)
    xbuf_ref[SUBLANES:SUBLANES + C, :] = u
    conv = conv_ref[0:1, :] * xbuf_ref[SUBLANES - 3:SUBLANES - 3 + C, :]
    for j in range(1, GDN_CONV):
        conv = conv + conv_ref[j:j + 1, :] * xbuf_ref[SUBLANES - 3 + j:SUBLANES - 3 + j + C, :]
    xbuf_ref[0:SUBLANES, :] = u[C - SUBLANES:C, :]
    qkv = _silu(conv)

    gates = jnp.where(live, proj_ref[0, :, GDN_GATE_OFF:GDN_GATE_OFF + LANES], 0.0)
    beta = jax.nn.sigmoid(gates)
    neg_rate = -jnp.exp(gate_ref[0:1, :])
    g = neg_rate * jax.nn.softplus(gates + gate_ref[1:2, :])
    r_i = lax.broadcasted_iota(jnp.int32, (C, C), 0)
    c_i = lax.broadcasted_iota(jnp.int32, (C, C), 1)
    causal = c_i <= r_i
    strict = c_i < r_i
    gc = jnp.dot(causal.astype(F32), g, preferred_element_type=F32,
                 precision=lax.Precision.HIGHEST)
    gc_t = gc.T
    eg = jnp.exp(gc)
    g_last = gc[C - 1:C, :]
    eg_last = jnp.exp(g_last)
    ek = jnp.exp(g_last - gc)

    for h in range(GDN_HEADS):
        lo = h * GDN_DK
        q = qkv[:, lo:lo + GDN_DK]
        k = qkv[:, GDN_QK_DIM + lo:GDN_QK_DIM + lo + GDN_DK]
        v = qkv[:, 2 * GDN_QK_DIM + h * GDN_DV:2 * GDN_QK_DIM + (h + 1) * GDN_DV]
        q = q * lax.rsqrt(jnp.sum(q * q, axis=-1, keepdims=True) + L2_EPS) * (GDN_DK ** -0.5)
        k = k * lax.rsqrt(jnp.sum(k * k, axis=-1, keepdims=True) + L2_EPS)
        gh = GDN_HEADS + h
        beta_h = beta[:, h:h + 1]
        decay = jnp.exp(jnp.where(causal, gc[:, gh:gh + 1] - gc_t[gh:gh + 1, :], NEG_INF))
        kb = k * beta_h
        vb = v * beta_h
        k16 = k.astype(BF16)
        lower = jnp.where(strict, _nt_dot(kb.astype(BF16), k16) * decay, 0.0)
        x = jnp.concatenate([vb, kb * eg[:, gh:gh + 1]], axis=1)
        n = -lower
        for it in range(6):
            x = x + jnp.dot(n.astype(BF16), x.astype(BF16), preferred_element_type=F32)
            if it < 5:
                n = jnp.dot(n.astype(BF16), n.astype(BF16), preferred_element_type=F32)
        u_h = x[:, 0:GDN_DV]
        w_h = x[:, GDN_DV:2 * GDN_DV]
        qk = jnp.where(causal, _nt_dot(q.astype(BF16), k16) * decay, 0.0)
        s = state_ref[h]
        s16 = s.astype(BF16)
        v_new = u_h - jnp.dot(w_h.astype(BF16), s16, preferred_element_type=F32)
        q_dec = q * eg[:, gh:gh + 1]
        o = (jnp.dot(q_dec.astype(BF16), s16, preferred_element_type=F32)
             + jnp.dot(qk.astype(BF16), v_new.astype(BF16), preferred_element_type=F32))
        k_dec = k * ek[:, gh:gh + 1]
        state_ref[h] = s * eg_last[:, gh:gh + 1] + _tn_dot(k_dec.astype(BF16), v_new.astype(BF16))
        z = proj_ref[0, :, GDN_CONV_DIM + h * GDN_DV:GDN_CONV_DIM + (h + 1) * GDN_DV]
        on = o * lax.rsqrt(jnp.mean(o * o, axis=-1, keepdims=True) + RMS_EPS) * ogain_ref[...]
        o_ref[0, :, h * GDN_DV:(h + 1) * GDN_DV] = (on * _silu(z)).astype(o_ref.dtype)


def _gdn_core(proj, conv_w, a_log, dt_bias, o_gain):
    B, LP, W = proj.shape
    nc = LP // CHUNK
    gate = jnp.zeros((2, LANES), F32)
    gate = gate.at[0, GDN_HEADS:2 * GDN_HEADS].set(a_log.astype(F32))
    gate = gate.at[1, GDN_HEADS:2 * GDN_HEADS].set(dt_bias.astype(F32))
    return pl.pallas_call(
        _gdn_kernel,
        grid=(B, nc),
        in_specs=[pl.BlockSpec((1, CHUNK, W), lambda b, c: (b, c, 0)),
                  pl.BlockSpec((GDN_CONV, GDN_CONV_DIM), lambda b, c: (0, 0)),
                  pl.BlockSpec((2, LANES), lambda b, c: (0, 0)),
                  pl.BlockSpec((1, GDN_DV), lambda b, c: (0, 0))],
        out_specs=pl.BlockSpec((1, CHUNK, GDN_V_DIM), lambda b, c: (b, c, 0)),
        out_shape=jax.ShapeDtypeStruct((B, LP, GDN_V_DIM), BF16),
        scratch_shapes=[pltpu.VMEM((GDN_HEADS, GDN_DK, GDN_DV), F32),
                        pltpu.VMEM((SUBLANES + CHUNK, GDN_CONV_DIM), F32)],
        compiler_params=_compiler_params(("parallel", "arbitrary")),
        name="gdn_core",
    )(proj, conv_w, gate, o_gain.reshape(1, GDN_DV))


def _moe_kernel(h_ref, gain_ref, wr_ref, br_ref, wgu_ref, wd_ref, o_ref, xn_ref, comb_ref, acc_ref):
    e = pl.program_id(1)
    lane = lax.broadcasted_iota(jnp.int32, (1, LANES), 1)

    @pl.when(e == 0)
    def _():
        x = h_ref[...]
        xn = x * lax.rsqrt(jnp.mean(x * x, axis=-1, keepdims=True) + RMS_EPS) * gain_ref[...]
        xn_ref[...] = xn.astype(BF16)
        logits = jnp.dot(xn, wr_ref[...], preferred_element_type=F32,
                         precision=lax.Precision.HIGHEST) + br_ref[...]
        e_log = logits[:, 0:LANES]
        g_log = jnp.where(lane < N_GROUPS, logits[:, LANES:2 * LANES], NEG_INF)
        g_max = jnp.max(g_log, axis=-1, keepdims=True)
        g_sel = jnp.min(jnp.where(g_log == g_max, lane, LANES), axis=-1, keepdims=True)
        g_w = 1.0 / jnp.sum(jnp.exp(g_log - g_max), axis=-1, keepdims=True)
        in_grp = (lane >= g_sel * EXPERTS_PER_GROUP) & (lane < (g_sel + 1) * EXPERTS_PER_GROUP)
        l0 = jnp.where(in_grp, e_log, NEG_INF)
        m1 = jnp.max(l0, axis=-1, keepdims=True)
        i1 = jnp.min(jnp.where(l0 == m1, lane, LANES), axis=-1, keepdims=True)
        l1 = jnp.where(lane == i1, NEG_INF, l0)
        m2 = jnp.max(l1, axis=-1, keepdims=True)
        i2 = jnp.min(jnp.where(l1 == m2, lane, LANES), axis=-1, keepdims=True)
        p2 = jnp.exp(m2 - m1)
        w1 = g_w / (1.0 + p2)
        w2 = g_w * p2 / (1.0 + p2)
        comb_ref[...] = jnp.where(lane == i1, w1, 0.0) + jnp.where(lane == i2, w2, 0.0)
        acc_ref[...] = jnp.zeros_like(acc_ref)

    xn = xn_ref[...]
    hh = jnp.dot(xn, wgu_ref[0], preferred_element_type=F32)
    cw = jnp.sum(jnp.where(lane == e, comb_ref[...], 0.0), axis=-1, keepdims=True)
    act = _silu(hh[:, 0:EXPERT_FF]) * hh[:, EXPERT_FF:2 * EXPERT_FF] * cw
    acc_ref[...] += jnp.dot(act.astype(BF16), wd_ref[0], preferred_element_type=F32)

    @pl.when(e == N_EXPERTS - 1)
    def _():
        o_ref[...] = h_ref[...] + acc_ref[...]


def _moe(h2d, gain, w_group, b_group, w_expert, b_expert, w_gate_up, w_down):
    T, D = h2d.shape
    tm = _pick_tile(T, (1408, 1280, 1024, 768, 640, 512, 256, 128))
    wr = jnp.zeros((D, 2 * LANES), F32)
    wr = wr.at[:, 0:N_EXPERTS].set(w_expert.astype(F32))
    wr = wr.at[:, LANES:LANES + N_GROUPS].set(w_group.astype(F32))
    br = jnp.zeros((1, 2 * LANES), F32)
    br = br.at[0, 0:N_EXPERTS].set(b_expert.astype(F32))
    br = br.at[0, LANES:LANES + N_GROUPS].set(b_group.astype(F32))
    wgu = w_gate_up.reshape(N_EXPERTS, D, 2 * EXPERT_FF).astype(BF16)
    wd = w_down.reshape(N_EXPERTS, EXPERT_FF, D).astype(BF16)
    return pl.pallas_call(
        _moe_kernel,
        grid=(T // tm, N_EXPERTS),
        in_specs=[pl.BlockSpec((tm, D), lambda i, e: (i, 0)),
                  pl.BlockSpec((1, D), lambda i, e: (0, 0)),
                  pl.BlockSpec((D, 2 * LANES), lambda i, e: (0, 0)),
                  pl.BlockSpec((1, 2 * LANES), lambda i, e: (0, 0)),
                  pl.BlockSpec((1, D, 2 * EXPERT_FF), lambda i, e: (e, 0, 0)),
                  pl.BlockSpec((1, EXPERT_FF, D), lambda i, e: (e, 0, 0))],
        out_specs=pl.BlockSpec((tm, D), lambda i, e: (i, 0)),
        out_shape=jax.ShapeDtypeStruct((T, D), F32),
        scratch_shapes=[pltpu.VMEM((tm, D), BF16),
                        pltpu.VMEM((tm, LANES), F32),
                        pltpu.VMEM((tm, D), F32)],
        compiler_params=_compiler_params(("parallel", "arbitrary")),
        name="moe",
    )(h2d, gain.reshape(1, D), wr, br, wgu, wd)


def _dsa_prep_kernel(p_ref, qg_ref, kg_ref, lng_ref, lnb_ref, q_ref, kt_ref, v_ref, qi_ref, kit_ref, wi_ref):
    lane = lax.broadcasted_iota(jnp.int32, (1, LANES), 1)
    for h in range(ATT_HEADS):
        x = p_ref[:, DSA_Q_OFF + h * ATT_HD:DSA_Q_OFF + (h + 1) * ATT_HD]
        y = x * lax.rsqrt(jnp.mean(x * x, axis=-1, keepdims=True) + RMS_EPS) * qg_ref[...]
        q_ref[:, h * ATT_HD:(h + 1) * ATT_HD] = (y * (ATT_HD ** -0.5)).astype(BF16)
    for n in range(ATT_KV_HEADS):
        x = p_ref[:, DSA_K_OFF + n * ATT_HD:DSA_K_OFF + (n + 1) * ATT_HD]
        y = x * lax.rsqrt(jnp.mean(x * x, axis=-1, keepdims=True) + RMS_EPS) * kg_ref[...]
        kt_ref[0, n * ATT_HD:(n + 1) * ATT_HD, :] = y.T.astype(BF16)
    v_ref[...] = p_ref[:, DSA_V_OFF:DSA_V_OFF + ATT_KV_DIM].astype(BF16)
    qi_ref[...] = p_ref[:, DSA_QI_OFF:DSA_QI_OFF + IDX_HEADS * LANES].astype(BF16)
    x = p_ref[:, DSA_KI_OFF:DSA_KI_OFF + LANES]
    live = lane < IDX_HD
    mu = jnp.sum(x, axis=-1, keepdims=True) * (1.0 / IDX_HD)
    xc = jnp.where(live, x - mu, 0.0)
    var = jnp.sum(xc * xc, axis=-1, keepdims=True) * (1.0 / IDX_HD)
    ki = xc * lax.rsqrt(var + RMS_EPS) * lng_ref[...] + lnb_ref[...]
    kit_ref[0] = jnp.where(live, ki, 0.0).T.astype(BF16)
    wi_ref[...] = p_ref[:, DSA_WI_OFF:DSA_WI_OFF + LANES] * (IDX_HEADS ** -0.5 * IDX_HD ** -0.5)


def _dsa_prep(proj2d, q_gain, k_gain, ln_g, ln_b):
    T, W = proj2d.shape
    tm = KEY_TILE
    assert T % tm == 0
    lng = jnp.zeros((1, LANES), F32).at[0, 0:IDX_HD].set(ln_g.astype(F32))
    lnb = jnp.zeros((1, LANES), F32).at[0, 0:IDX_HD].set(ln_b.astype(F32))
    row = lambda i: (i, 0)
    fix = lambda i: (0, 0)
    blk = lambda i: (i, 0, 0)
    return pl.pallas_call(
        _dsa_prep_kernel,
        grid=(T // tm,),
        in_specs=[pl.BlockSpec((tm, W), row),
                  pl.BlockSpec((1, ATT_HD), fix), pl.BlockSpec((1, ATT_HD), fix),
                  pl.BlockSpec((1, LANES), fix), pl.BlockSpec((1, LANES), fix)],
        out_specs=[pl.BlockSpec((tm, ATT_Q_DIM), row), pl.BlockSpec((1, ATT_KV_DIM, tm), blk),
                   pl.BlockSpec((tm, ATT_KV_DIM), row), pl.BlockSpec((tm, IDX_HEADS * LANES), row),
                   pl.BlockSpec((1, LANES, tm), blk), pl.BlockSpec((tm, LANES), row)],
        out_shape=[jax.ShapeDtypeStruct((T, ATT_Q_DIM), BF16), jax.ShapeDtypeStruct((T // tm, ATT_KV_DIM, tm), BF16),
                   jax.ShapeDtypeStruct((T, ATT_KV_DIM), BF16), jax.ShapeDtypeStruct((T, IDX_HEADS * LANES), BF16),
                   jax.ShapeDtypeStruct((T // tm, LANES, tm), BF16), jax.ShapeDtypeStruct((T, LANES), F32)],
        compiler_params=_compiler_params(("parallel",)),
        name="dsa_prep",
    )(proj2d, q_gain.reshape(1, ATT_HD), k_gain.reshape(1, ATT_HD), lng, lnb)


def _dsa_kernel(q_ref, qi_ref, wi_ref, kt_ref, v_ref, kit_ref, bias_ref, o_ref,
                sc_ref, qg_ref, wib_ref, m_ref, l_ref, acc_ref, *, n_select, live_rows):
    i = pl.program_id(1)
    TQ = ATT_TILE
    TK = KEY_TILE
    HALVES = TK // LANES

    @pl.when(i * TQ >= live_rows)
    def _():
        o_ref[...] = jnp.zeros_like(o_ref)

    @pl.when(i * TQ < live_rows)
    def _():
        _dsa_tile(i, q_ref, qi_ref, wi_ref, kt_ref, v_ref, kit_ref, bias_ref, o_ref,
                  sc_ref, qg_ref, wib_ref, m_ref, l_ref, acc_ref, n_select, TQ, TK, HALVES)


def _dsa_tile(i, q_ref, qi_ref, wi_ref, kt_ref, v_ref, kit_ref, bias_ref, o_ref,
              sc_ref, qg_ref, wib_ref, m_ref, l_ref, acc_ref, n_select, TQ, TK, HALVES):
    nkt = lax.shift_right_logical(i, 1) + 1
    fmin = float(jnp.finfo(F32).min)
    kf = float(n_select)

    q_row = i * TQ + lax.broadcasted_iota(jnp.int32, (TQ, 1), 0)
    q_chunk = lax.shift_right_logical(q_row, int(math.log2(CHUNK)))
    k_lane = lax.broadcasted_iota(jnp.int32, (1, TK), 1)

    def halves(x):
        return [x[:, j * LANES:(j + 1) * LANES] for j in range(HALVES)]

    def lane_wide(col):
        return jnp.broadcast_to(col, (TQ, LANES))

    wi = wi_ref[0]
    for h in range(IDX_HEADS):
        wib_ref[h] = jnp.broadcast_to(wi[:, h:h + 1], (TQ, TK))

    def score_tile(kt, carry):
        ki = kit_ref[0, kt]
        acc = jnp.zeros((TQ, TK), F32)
        for h in range(IDX_HEADS):
            d = jnp.dot(qi_ref[0, :, h * LANES:(h + 1) * LANES], ki, preferred_element_type=F32)
            acc = acc + wib_ref[h] * jnp.maximum(d, 0.0)
        sc_ref[kt] = acc
        return carry

    lax.fori_loop(0, nkt, score_tile, 0)

    def mask_inadmissible(kt):
        k_pos = kt * TK + k_lane
        adm = (lax.shift_right_logical(k_pos, int(math.log2(CHUNK))) <= q_chunk) & (k_pos >= META_PAD)
        sc_ref[kt] = jnp.where(adm, sc_ref[kt], NEG_INF)

    mask_inadmissible(0)

    @pl.when(nkt > 1)
    def _():
        mask_inadmissible(nkt - 1)

    n_adm = CHUNK * (q_chunk + 1) - META_PAD
    need = n_adm > n_select

    def row_sum(x):
        return jnp.sum(x, axis=-1, keepdims=True)

    def count_ge(t):
        tb = lane_wide(t)

        def body(kt, acc):
            for s in halves(sc_ref[kt]):
                acc = acc + jnp.where(s >= tb, 1.0, 0.0)
            return acc

        return row_sum(lax.fori_loop(0, nkt, body, jnp.zeros((TQ, LANES), F32)))

    def minmax_body(kt, carry):
        mn, mx = carry
        for s in halves(sc_ref[kt]):
            mn = jnp.minimum(mn, jnp.where(s == NEG_INF, jnp.inf, s))
            mx = jnp.maximum(mx, s)
        return mn, mx

    mn_l, mx_l = lax.fori_loop(0, nkt, minmax_body,
                               (jnp.full((TQ, LANES), jnp.inf, F32), jnp.full((TQ, LANES), NEG_INF, F32)))
    s_min = jnp.min(mn_l, axis=-1, keepdims=True)
    s_max = jnp.max(mx_l, axis=-1, keepdims=True)
    lo0 = s_min
    hi0 = s_max + (s_max - s_min) + 1.0
    cnt0 = n_adm.astype(F32)
    active0 = jnp.where(need, 1.0, 0.0)

    def any_active(active):
        return jnp.max(active) > 0.0

    def bis_body(st):
        it, lo, hi, cnt, active = st
        mid = lo + (hi - lo) * 0.5
        c = count_ge(mid)
        ge = c >= kf
        moving = (mid > lo) & (mid < hi) & (active > 0.0)
        lo_n = jnp.where(moving & ge, mid, lo)
        hi_n = jnp.where(moving & (~ge), mid, hi)
        cnt_n = jnp.where(moving & ge, c, cnt)
        active_n = jnp.where(moving & (cnt_n != kf), 1.0, 0.0)
        return it + 1, lo_n, hi_n, cnt_n, active_n

    _, lo, hi, cnt, _ = lax.while_loop(
        lambda st: jnp.logical_and(st[0] < BISECT_STEPS, any_active(st[4])), bis_body,
        (jnp.int32(0), lo0, hi0, cnt0, active0))

    def snap_body(st):
        lo, hi, cnt, active = st
        hb = lane_wide(hi)

        def body(kt, mx):
            for s in halves(sc_ref[kt]):
                mx = jnp.maximum(mx, jnp.where(s < hb, s, NEG_INF))
            return mx

        v1 = jnp.max(lax.fori_loop(0, nkt, body, jnp.full((TQ, LANES), NEG_INF, F32)),
                     axis=-1, keepdims=True)
        c1 = count_ge(v1)
        on = active > 0.0
        hit = on & (c1 >= kf)
        lo_n = jnp.where(hit, v1, lo)
        cnt_n = jnp.where(hit, c1, cnt)
        hi_n = jnp.where(on & (~hit), v1, hi)
        return lo_n, hi_n, cnt_n, jnp.where(on & (~hit), 1.0, 0.0)

    unsettled = jnp.where(need & (cnt != kf), 1.0, 0.0)
    lo, hi, cnt, _ = lax.while_loop(lambda st: any_active(st[3]), snap_body, (lo, hi, cnt, unsettled))

    thr = jnp.where(need, lo, fmin)
    tied = need & (cnt > kf)
    big_pos = jnp.int32(2 ** 30)

    def tie_break():
        tb = lane_wide(thr)

        def count_pred(pred):
            def body(kt, acc):
                pos = kt * TK + k_lane
                for j, s in enumerate(halves(sc_ref[kt])):
                    acc = acc + jnp.where(pred(s, pos[:, j * LANES:(j + 1) * LANES]), 1.0, 0.0)
                return acc
            return row_sum(lax.fori_loop(0, nkt, body, jnp.zeros((TQ, LANES), F32)))

        want = kf - count_pred(lambda s, pos: s > tb)

        def body(_, st):
            jl, jh = st
            jm = lax.shift_right_arithmetic(jl + jh, 1)
            c = count_pred(lambda s, pos: (s == tb) & (pos <= jm))
            ok = c >= want
            return jnp.where(ok, jl, jm), jnp.where(ok, jm, jh)

        n_keys = sc_ref.shape[0] * TK
        steps = int(math.ceil(math.log2(n_keys + 1))) + 1
        jl0 = jnp.full((TQ, 1), -1, jnp.int32)
        jh0 = jnp.full((TQ, 1), n_keys, jnp.int32)
        _, jh = lax.fori_loop(0, steps, body, (jl0, jh0))
        return jnp.where(tied, jh, big_pos)

    pos_cap = lax.cond(jnp.max(jnp.where(tied, 1.0, 0.0)) > 0.0, tie_break,
                       lambda: jnp.full((TQ, 1), big_pos, jnp.int32))

    def to_mask(kt, carry):
        s = sc_ref[kt]
        pos = kt * TK + k_lane
        sel = (s > thr) | ((s == thr) & (pos <= pos_cap))
        sc_ref[kt] = jnp.where(sel, 0.0, NEG_INF)
        return carry

    lax.fori_loop(0, nkt, to_mask, 0)

    for n in range(ATT_KV_HEADS):
        for g in range(ATT_GROUP):
            h = n * ATT_GROUP + g
            qg_ref[n, g * TQ:(g + 1) * TQ, :] = q_ref[0, :, h * ATT_HD:(h + 1) * ATT_HD]
    m_ref[...] = jnp.full(m_ref.shape, -1e30, F32)
    l_ref[...] = jnp.zeros_like(l_ref)
    acc_ref[...] = jnp.zeros_like(acc_ref)

    def logits(kt, n, b0):
        kk = kt_ref[0, kt, n * ATT_HD:(n + 1) * ATT_HD, :]
        s = jnp.dot(qg_ref[n], kk, preferred_element_type=F32)
        mask = sc_ref[kt]
        out = []
        for g in range(ATT_GROUP):
            sg = s[g * TQ:(g + 1) * TQ, :] + mask
            if b0 is not None:
                h = n * ATT_GROUP + g
                sg = sg + jnp.concatenate([bias_ref[h, b0 + j] for j in range(HALVES)], axis=1)
            out.append(sg)
        return out

    def max_pass(kt, b0):
        for n in range(ATT_KV_HEADS):
            for g, sg in enumerate(logits(kt, n, b0)):
                rows = slice(g * TQ, (g + 1) * TQ)
                m = m_ref[n, rows, :]
                for part in halves(sg):
                    m = jnp.maximum(m, part)
                m_ref[n, rows, :] = m

    def exp_pass(kt, b0):
        start = pl.multiple_of(kt * TK, TK)
        for n in range(ATT_KV_HEADS):
            ps = []
            for g, sg in enumerate(logits(kt, n, b0)):
                rows = slice(g * TQ, (g + 1) * TQ)
                m = m_ref[n, rows, :]
                parts = [jnp.exp(part - m) for part in halves(sg)]
                l_ref[n, rows, :] += functools.reduce(lambda a, b: a + b, parts)
                ps.append(jnp.concatenate(parts, axis=1).astype(BF16))
            p = jnp.concatenate(ps, axis=0)
            vv = v_ref[0, pl.ds(start, TK), n * ATT_HD:(n + 1) * ATT_HD]
            acc_ref[n] += jnp.dot(p, vv, preferred_element_type=F32)

    n_far = lax.shift_right_arithmetic(i - 1, 1)
    parity = lax.bitwise_and(i, 1)

    def run_pass(tile_fn):
        def far_body(kt, carry):
            tile_fn(kt, None)
            return carry

        lax.fori_loop(0, n_far, far_body, 0)

        @pl.when(i >= 1)
        def _():
            tile_fn(n_far, parity)

        @pl.when(parity == 0)
        def _():
            tile_fn(lax.shift_right_logical(i, 1), 2)

    run_pass(max_pass)
    for n in range(ATT_KV_HEADS):
        m_ref[n] = jnp.broadcast_to(jnp.max(m_ref[n], axis=-1, keepdims=True), m_ref.shape[1:])
    run_pass(exp_pass)

    for n in range(ATT_KV_HEADS):
        inv_l = 1.0 / jnp.sum(l_ref[n], axis=-1, keepdims=True)
        for g in range(ATT_GROUP):
            h = n * ATT_GROUP + g
            rows = slice(g * TQ, (g + 1) * TQ)
            o_ref[0, :, h * ATT_HD:(h + 1) * ATT_HD] = (acc_ref[n, rows, :] * inv_l[rows, :]).astype(o_ref.dtype)


def _t5_bucket(rel):
    nb = REL_BUCKETS // 2
    max_exact = nb // 2
    ret = jnp.where(rel > 0, nb, 0)
    n = jnp.abs(rel)
    nf = jnp.maximum(n, 1).astype(jnp.float32)
    large = max_exact + (jnp.log(nf / max_exact) / math.log(REL_MAX_DIST / max_exact)
                         * (nb - max_exact)).astype(jnp.int32)
    large = jnp.minimum(large, nb - 1)
    return ret + jnp.where(n < max_exact, n, large)


def _dsa_core(q, kt, v, qi, kit, wi, rel_bias, n_select, live_rows):
    B, LP, _ = q.shape
    nq = LP // ATT_TILE
    nkt = LP // KEY_TILE
    a = jnp.arange(ATT_TILE)[:, None]
    xx = jnp.arange(2 * KEY_TILE)[None, :]
    rel = (xx - KEY_TILE) - a
    far = rel_bias.astype(F32)[REL_BUCKETS // 2 - 1]
    table = rel_bias.astype(F32)[_t5_bucket(rel)] - far
    bias_near = jnp.transpose(table.reshape(ATT_TILE, 2 * KEY_TILE // LANES, LANES, ATT_HEADS), (3, 1, 0, 2))
    kern = functools.partial(_dsa_kernel, n_select=n_select, live_rows=live_rows)
    tile = lambda b, i: (b, i, 0)
    whole3 = lambda b, i: (b, 0, 0)
    whole4 = lambda b, i: (b, 0, 0, 0)
    return pl.pallas_call(
        kern,
        grid=(B, nq),
        in_specs=[pl.BlockSpec((1, ATT_TILE, ATT_Q_DIM), tile),
                  pl.BlockSpec((1, ATT_TILE, IDX_HEADS * LANES), tile),
                  pl.BlockSpec((1, ATT_TILE, LANES), tile),
                  pl.BlockSpec((1, nkt, ATT_KV_DIM, KEY_TILE), whole4),
                  pl.BlockSpec((1, LP, ATT_KV_DIM), whole3),
                  pl.BlockSpec((1, nkt, LANES, KEY_TILE), whole4),
                  pl.BlockSpec((ATT_HEADS, 2 * KEY_TILE // LANES, ATT_TILE, LANES), lambda b, i: (0, 0, 0, 0))],
        out_specs=pl.BlockSpec((1, ATT_TILE, ATT_Q_DIM), tile),
        out_shape=jax.ShapeDtypeStruct((B, LP, ATT_Q_DIM), BF16),
        scratch_shapes=[pltpu.VMEM((nkt, ATT_TILE, KEY_TILE), F32),
                        pltpu.VMEM((ATT_KV_HEADS, ATT_GROUP * ATT_TILE, ATT_HD), BF16),
                        pltpu.VMEM((IDX_HEADS, ATT_TILE, KEY_TILE), F32),
                        pltpu.VMEM((ATT_KV_HEADS, ATT_GROUP * ATT_TILE, LANES), F32),
                        pltpu.VMEM((ATT_KV_HEADS, ATT_GROUP * ATT_TILE, LANES), F32),
                        pltpu.VMEM((ATT_KV_HEADS, ATT_GROUP * ATT_TILE, ATT_HD), F32)],
        compiler_params=_compiler_params(("parallel", "arbitrary")),
        name="dsa_core",
    )(q, qi, wi, kt, v, kit, bias_near)


def _gdn_in_weight(w_in):
    D = w_in.shape[0]
    w = jnp.zeros((D, GDN_PROJ_W), F32).at[:, 0:w_in.shape[1]].set(w_in.astype(F32))
    return w.astype(BF16)


def _dsa_in_weight(w_in):
    D = w_in.shape[0]
    w = jnp.zeros((D, DSA_PROJ_W), F32)
    w = w.at[:, 0:DSA_QI_OFF].set(w_in[:, 0:DSA_QI_OFF].astype(F32))
    src = ATT_Q_DIM + 2 * ATT_KV_DIM
    for h in range(IDX_HEADS):
        w = w.at[:, DSA_QI_OFF + h * LANES:DSA_QI_OFF + h * LANES + IDX_HD].set(
            w_in[:, src + h * IDX_HD:src + (h + 1) * IDX_HD].astype(F32))
    src += IDX_HEADS * IDX_HD
    w = w.at[:, DSA_KI_OFF:DSA_KI_OFF + IDX_HD].set(w_in[:, src:src + IDX_HD].astype(F32))
    src += IDX_HD
    w = w.at[:, DSA_WI_OFF:DSA_WI_OFF + IDX_HEADS].set(w_in[:, src:src + IDX_HEADS].astype(F32))
    return w.astype(BF16)


def kernel(x, meta_tokens, norm_mix, norm_ffn, rel_bias, gdn_w_in, gdn_conv, gdn_a_log, gdn_dt_bias,
           gdn_o_norm, gdn_w_out, dsa_w_in, dsa_q_norm, dsa_k_norm, dsa_idx_ln_g, dsa_idx_ln_b, dsa_w_out,
           moe_w_group, moe_b_group, moe_w_expert, moe_b_expert, moe_w_gate_up, moe_w_down):
    B, S, D = x.shape
    depth = norm_mix.shape[0]
    n_select = min(TOPK_MAX, S // 4)
    off = META_PAD + N_META
    LP = -(-(off + S) // ROW_ALIGN) * ROW_ALIGN
    meta = jnp.broadcast_to(meta_tokens.astype(x.dtype)[None], (B, N_META, D))
    h = jnp.concatenate([jnp.zeros((B, META_PAD, D), x.dtype), meta, x,
                         jnp.zeros((B, LP - off - S, D), x.dtype)], axis=1)
    h = h.reshape(B * LP, D)
    for i in range(depth):
        j = i // 2
        if i % 2 == 0:
            proj = _norm_matmul(h, norm_mix[i], _gdn_in_weight(gdn_w_in[j]))
            o = _gdn_core(proj.reshape(B, LP, GDN_PROJ_W), gdn_conv[j].astype(F32), gdn_a_log[j],
                          gdn_dt_bias[j], gdn_o_norm[j])
            h = _matmul_residual(o.reshape(B * LP, GDN_V_DIM), gdn_w_out[j].astype(BF16), h)
        else:
            proj = _norm_matmul(h, norm_mix[i], _dsa_in_weight(dsa_w_in[j]))
            q, kt, v, qi, kit, wi = _dsa_prep(proj, dsa_q_norm[j], dsa_k_norm[j], dsa_idx_ln_g[j], dsa_idx_ln_b[j])
            r3 = lambda t: t.reshape(B, LP, t.shape[-1])
            r4 = lambda t: t.reshape(B, LP // KEY_TILE, t.shape[-2], KEY_TILE)
            o = _dsa_core(r3(q), r4(kt), r3(v), r3(qi), r4(kit), r3(wi), rel_bias, n_select, off + S)
            h = _matmul_residual(o.reshape(B * LP, ATT_Q_DIM), dsa_w_out[j].astype(BF16), h)
        h = _moe(h, norm_ffn[i], moe_w_group[i], moe_b_group[i], moe_w_expert[i], moe_b_expert[i],
                 moe_w_gate_up[i], moe_w_down[i])
    return h.reshape(B, LP, D)[:, off:off + S]
```

```python
import functools
import math

import jax
import jax.numpy as jnp
from jax import lax
from jax.experimental import pallas as pl
from jax.experimental.pallas import tpu as pltpu

F32 = jnp.float32
BF16 = jnp.bfloat16

LANES = 128
SUBLANES = 8
VMEM_LIMIT_BYTES = 56 * 1024 * 1024

CHUNK = 64
N_META = 16
META_PAD = (-N_META) % CHUNK
RMS_EPS = 1e-6
L2_EPS = 1e-6

GDN_HEADS = 8
GDN_DK = 128
GDN_DV = 128
GDN_CONV = 4
GDN_QK_DIM = GDN_HEADS * GDN_DK
GDN_V_DIM = GDN_HEADS * GDN_DV
GDN_CONV_DIM = 2 * GDN_QK_DIM + GDN_V_DIM
GDN_GATE_OFF = GDN_CONV_DIM + GDN_V_DIM
GDN_PROJ_W = GDN_GATE_OFF + LANES

ATT_HEADS = 8
ATT_KV_HEADS = 2
ATT_GROUP = ATT_HEADS // ATT_KV_HEADS
ATT_HD = 128
ATT_Q_DIM = ATT_HEADS * ATT_HD
ATT_KV_DIM = ATT_KV_HEADS * ATT_HD
IDX_HEADS = 8
IDX_HD = 64
TOPK_MAX = 256
REL_BUCKETS = 32
REL_MAX_DIST = 128

DSA_Q_OFF = 0
DSA_K_OFF = ATT_Q_DIM
DSA_V_OFF = DSA_K_OFF + ATT_KV_DIM
DSA_QI_OFF = DSA_V_OFF + ATT_KV_DIM
DSA_KI_OFF = DSA_QI_OFF + IDX_HEADS * LANES
DSA_WI_OFF = DSA_KI_OFF + LANES
DSA_PROJ_W = DSA_WI_OFF + LANES

N_GROUPS = 4
EXPERTS_PER_GROUP = 8
N_EXPERTS = N_GROUPS * EXPERTS_PER_GROUP
EXPERT_FF = 256

ATT_TILE = 128
KEY_TILE = 256
ROW_ALIGN = KEY_TILE
BISECT_UNROLL = 4
BISECT_STEPS = 24
NEG_INF = float("-inf")


def _pick_tile(n, candidates):
    for c in candidates:
        if n % c == 0:
            return c
    raise ValueError(f"no tile for {n}")


def _compiler_params(semantics):
    return pltpu.CompilerParams(dimension_semantics=semantics, vmem_limit_bytes=VMEM_LIMIT_BYTES)


def _silu(x):
    return x * jax.nn.sigmoid(x)


def _norm_matmul_kernel(x_ref, g_ref, w_ref, o_ref):
    x = x_ref[...]
    y = x * lax.rsqrt(jnp.mean(x * x, axis=-1, keepdims=True) + RMS_EPS) * g_ref[...]
    o_ref[...] = jnp.dot(y.astype(BF16), w_ref[...], preferred_element_type=F32)


def _norm_matmul(h2d, gain, w_bf16):
    T, D = h2d.shape
    N = w_bf16.shape[1]
    tm = _pick_tile(T, (256, 128))
    return pl.pallas_call(
        _norm_matmul_kernel,
        grid=(T // tm,),
        in_specs=[pl.BlockSpec((tm, D), lambda i: (i, 0)),
                  pl.BlockSpec((1, D), lambda i: (0, 0)),
                  pl.BlockSpec((D, N), lambda i: (0, 0))],
        out_specs=pl.BlockSpec((tm, N), lambda i: (i, 0)),
        out_shape=jax.ShapeDtypeStruct((T, N), F32),
        compiler_params=_compiler_params(("parallel",)),
        name="norm_matmul",
    )(h2d, gain.reshape(1, D), w_bf16)


def _matmul_residual_kernel(a_ref, w_ref, r_ref, o_ref):
    o_ref[...] = r_ref[...] + jnp.dot(a_ref[...], w_ref[...], preferred_element_type=F32)


def _matmul_residual(a_bf16, w_bf16, res):
    T, K = a_bf16.shape
    N = w_bf16.shape[1]
    tm = _pick_tile(T, (512, 256, 128))
    return pl.pallas_call(
        _matmul_residual_kernel,
        grid=(T // tm,),
        in_specs=[pl.BlockSpec((tm, K), lambda i: (i, 0)),
                  pl.BlockSpec((K, N), lambda i: (0, 0)),
                  pl.BlockSpec((tm, N), lambda i: (i, 0))],
        out_specs=pl.BlockSpec((tm, N), lambda i: (i, 0)),
        out_shape=jax.ShapeDtypeStruct((T, N), F32),
        compiler_params=_compiler_params(("parallel",)),
        name="matmul_residual",
    )(a_bf16, w_bf16, res)


def _nt_dot(a, b):
    return lax.dot_general(a, b, (((1,), (1,)), ((), ())), preferred_element_type=F32)


def _tn_dot(a, b):
    return lax.dot_general(a, b, (((0,), (0,)), ((), ())), preferred_element_type=F32)


def _gdn_kernel(proj_ref, conv_ref, gate_ref, ogain_ref, o_ref, state_ref, xbuf_ref):
    c = pl.program_id(1)
    C = CHUNK

    @pl.when(c == 0)
    def _():
        state_ref[...] = jnp.zeros_like(state_ref)
        xbuf_ref[0:SUBLANES, :] = jnp.zeros((SUBLANES, GDN_CONV_DIM), F32)

    row = c * C + lax.broadcasted_iota(jnp.int32, (C, 1), 0)
    live = row >= META_PAD
    u = jnp.where(live, proj_ref[0, :, 0:GDN_CONV_DIM], 0.0)
    xbuf_ref[SUBLANES:SUBLANES + C, :] = u
    conv = conv_ref[0:1, :] * xbuf_ref[SUBLANES - 3:SUBLANES - 3 + C, :]
    for j in range(1, GDN_CONV):
        conv = conv + conv_ref[j:j + 1, :] * xbuf_ref[SUBLANES - 3 + j:SUBLANES - 3 + j + C, :]
    xbuf_ref[0:SUBLANES, :] = u[C - SUBLANES:C, :]
    qkv = _silu(conv)

    gates = jnp.where(live, proj_ref[0, :, GDN_GATE_OFF:GDN_GATE_OFF + LANES], 0.0)
    beta = jax.nn.sigmoid(gates)
    neg_rate = -jnp.exp(gate_ref[0:1, :])
    g = neg_rate * jax.nn.softplus(gates + gate_ref[1:2, :])
    r_i = lax.broadcasted_iota(jnp.int32, (C, C), 0)
    c_i = lax.broadcasted_iota(jnp.int32, (C, C), 1)
    causal = c_i <= r_i
    strict = c_i < r_i
    gc = jnp.dot(causal.astype(F32), g, preferred_element_type=F32,
                 precision=lax.Precision.HIGHEST)
    gc_t = gc.T
    eg = jnp.exp(gc)
    g_last = gc[C - 1:C, :]
    eg_last = jnp.exp(g_last)
    ek = jnp.exp(g_last - gc)

    for h in range(GDN_HEADS):
        lo = h * GDN_DK
        q = qkv[:, lo:lo + GDN_DK]
        k = qkv[:, GDN_QK_DIM + lo:GDN_QK_DIM + lo + GDN_DK]
        v = qkv[:, 2 * GDN_QK_DIM + h * GDN_DV:2 * GDN_QK_DIM + (h + 1) * GDN_DV]
        q = q * lax.rsqrt(jnp.sum(q * q, axis=-1, keepdims=True) + L2_EPS) * (GDN_DK ** -0.5)
        k = k * lax.rsqrt(jnp.sum(k * k, axis=-1, keepdims=True) + L2_EPS)
        gh = GDN_HEADS + h
        beta_h = beta[:, h:h + 1]
        decay = jnp.exp(jnp.where(causal, gc[:, gh:gh + 1] - gc_t[gh:gh + 1, :], NEG_INF))
        kb = k * beta_h
        vb = v * beta_h
        k16 = k.astype(BF16)
        lower = jnp.where(strict, _nt_dot(kb.astype(BF16), k16) * decay, 0.0)
        x = jnp.concatenate([vb, kb * eg[:, gh:gh + 1]], axis=1)
        n = -lower
        for it in range(6):
            x = x + jnp.dot(n.astype(BF16), x.astype(BF16), preferred_element_type=F32)
            if it < 5:
                n = jnp.dot(n.astype(BF16), n.astype(BF16), preferred_element_type=F32)
        u_h = x[:, 0:GDN_DV]
        w_h = x[:, GDN_DV:2 * GDN_DV]
        qk = jnp.where(causal, _nt_dot(q.astype(BF16), k16) * decay, 0.0)
        s = state_ref[h]
        s16 = s.astype(BF16)
        v_new = u_h - jnp.dot(w_h.astype(BF16), s16, preferred_element_type=F32)
        q_dec = q * eg[:, gh:gh + 1]
        o = (jnp.dot(q_dec.astype(BF16), s16, preferred_element_type=F32)
             + jnp.dot(qk.astype(BF16), v_new.astype(BF16), preferred_element_type=F32))
        k_dec = k * ek[:, gh:gh + 1]
        state_ref[h] = s * eg_last[:, gh:gh + 1] + _tn_dot(k_dec.astype(BF16), v_new.astype(BF16))
        z = proj_ref[0, :, GDN_CONV_DIM + h * GDN_DV:GDN_CONV_DIM + (h + 1) * GDN_DV]
        on = o * lax.rsqrt(jnp.mean(o * o, axis=-1, keepdims=True) + RMS_EPS) * ogain_ref[...]
        o_ref[0, :, h * GDN_DV:(h + 1) * GDN_DV] = (on * _silu(z)).astype(o_ref.dtype)


def _gdn_core(proj, conv_w, a_log, dt_bias, o_gain):
    B, LP, W = proj.shape
    nc = LP // CHUNK
    gate = jnp.zeros((2, LANES), F32)
    gate = gate.at[0, GDN_HEADS:2 * GDN_HEADS].set(a_log.astype(F32))
    gate = gate.at[1, GDN_HEADS:2 * GDN_HEADS].set(dt_bias.astype(F32))
    return pl.pallas_call(
        _gdn_kernel,
        grid=(B, nc),
        in_specs=[pl.BlockSpec((1, CHUNK, W), lambda b, c: (b, c, 0)),
                  pl.BlockSpec((GDN_CONV, GDN_CONV_DIM), lambda b, c: (0, 0)),
                  pl.BlockSpec((2, LANES), lambda b, c: (0, 0)),
                  pl.BlockSpec((1, GDN_DV), lambda b, c: (0, 0))],
        out_specs=pl.BlockSpec((1, CHUNK, GDN_V_DIM), lambda b, c: (b, c, 0)),
        out_shape=jax.ShapeDtypeStruct((B, LP, GDN_V_DIM), BF16),
        scratch_shapes=[pltpu.VMEM((GDN_HEADS, GDN_DK, GDN_DV), F32),
                        pltpu.VMEM((SUBLANES + CHUNK, GDN_CONV_DIM), F32)],
        compiler_params=_compiler_params(("parallel", "arbitrary")),
        name="gdn_core",
    )(proj, conv_w, gate, o_gain.reshape(1, GDN_DV))


def _moe_kernel(h_ref, gain_ref, wr_ref, br_ref, wgu_ref, wd_ref, o_ref, xn_ref, comb_ref, acc_ref):
    e = pl.program_id(1)
    lane = lax.broadcasted_iota(jnp.int32, (1, LANES), 1)

    @pl.when(e == 0)
    def _():
        x = h_ref[...]
        xn = x * lax.rsqrt(jnp.mean(x * x, axis=-1, keepdims=True) + RMS_EPS) * gain_ref[...]
        xn_ref[...] = xn.astype(BF16)
        logits = jnp.dot(xn, wr_ref[...], preferred_element_type=F32,
                         precision=lax.Precision.HIGHEST) + br_ref[...]
        e_log = logits[:, 0:LANES]
        g_log = jnp.where(lane < N_GROUPS, logits[:, LANES:2 * LANES], NEG_INF)
        g_max = jnp.max(g_log, axis=-1, keepdims=True)
        g_sel = jnp.min(jnp.where(g_log == g_max, lane, LANES), axis=-1, keepdims=True)
        g_w = 1.0 / jnp.sum(jnp.exp(g_log - g_max), axis=-1, keepdims=True)
        in_grp = (lane >= g_sel * EXPERTS_PER_GROUP) & (lane < (g_sel + 1) * EXPERTS_PER_GROUP)
        l0 = jnp.where(in_grp, e_log, NEG_INF)
        m1 = jnp.max(l0, axis=-1, keepdims=True)
        i1 = jnp.min(jnp.where(l0 == m1, lane, LANES), axis=-1, keepdims=True)
        l1 = jnp.where(lane == i1, NEG_INF, l0)
        m2 = jnp.max(l1, axis=-1, keepdims=True)
        i2 = jnp.min(jnp.where(l1 == m2, lane, LANES), axis=-1, keepdims=True)
        p2 = jnp.exp(m2 - m1)
        w1 = g_w / (1.0 + p2)
        w2 = g_w * p2 / (1.0 + p2)
        comb_ref[...] = jnp.where(lane == i1, w1, 0.0) + jnp.where(lane == i2, w2, 0.0)
        acc_ref[...] = jnp.zeros_like(acc_ref)

    xn = xn_ref[...]
    hh = jnp.dot(xn, wgu_ref[0], preferred_element_type=F32)
    cw = jnp.sum(jnp.where(lane == e, comb_ref[...], 0.0), axis=-1, keepdims=True)
    act = _silu(hh[:, 0:EXPERT_FF]) * hh[:, EXPERT_FF:2 * EXPERT_FF] * cw
    acc_ref[...] += jnp.dot(act.astype(BF16), wd_ref[0], preferred_element_type=F32)

    @pl.when(e == N_EXPERTS - 1)
    def _():
        o_ref[...] = h_ref[...] + acc_ref[...]


def _moe(h2d, gain, w_group, b_group, w_expert, b_expert, w_gate_up, w_down):
    T, D = h2d.shape
    tm = _pick_tile(T, (1408, 1280, 1024, 768, 640, 512, 256, 128))
    wr = jnp.zeros((D, 2 * LANES), F32)
    wr = wr.at[:, 0:N_EXPERTS].set(w_expert.astype(F32))
    wr = wr.at[:, LANES:LANES + N_GROUPS].set(w_group.astype(F32))
    br = jnp.zeros((1, 2 * LANES), F32)
    br = br.at[0, 0:N_EXPERTS].set(b_expert.astype(F32))
    br = br.at[0, LANES:LANES + N_GROUPS].set(b_group.astype(F32))
    wgu = w_gate_up.reshape(N_EXPERTS, D, 2 * EXPERT_FF).astype(BF16)
    wd = w_down.reshape(N_EXPERTS, EXPERT_FF, D).astype(BF16)
    return pl.pallas_call(
        _moe_kernel,
        grid=(T // tm, N_EXPERTS),
        in_specs=[pl.BlockSpec((tm, D), lambda i, e: (i, 0)),
                  pl.BlockSpec((1, D), lambda i, e: (0, 0)),
                  pl.BlockSpec((D, 2 * LANES), lambda i, e: (0, 0)),
                  pl.BlockSpec((1, 2 * LANES), lambda i, e: (0, 0)),
                  pl.BlockSpec((1, D, 2 * EXPERT_FF), lambda i, e: (e, 0, 0)),
                  pl.BlockSpec((1, EXPERT_FF, D), lambda i, e: (e, 0, 0))],
        out_specs=pl.BlockSpec((tm, D), lambda i, e: (i, 0)),
        out_shape=jax.ShapeDtypeStruct((T, D), F32),
        scratch_shapes=[pltpu.VMEM((tm, D), BF16),
                        pltpu.VMEM((tm, LANES), F32),
                        pltpu.VMEM((tm, D), F32)],
        compiler_params=_compiler_params(("parallel", "arbitrary")),
        name="moe",
    )(h2d, gain.reshape(1, D), wr, br, wgu, wd)


def _dsa_prep_kernel(p_ref, qg_ref, kg_ref, lng_ref, lnb_ref, q_ref, kt_ref, v_ref, qi_ref, kit_ref, wi_ref):
    lane = lax.broadcasted_iota(jnp.int32, (1, LANES), 1)
    for h in range(ATT_HEADS):
        x = p_ref[:, DSA_Q_OFF + h * ATT_HD:DSA_Q_OFF + (h + 1) * ATT_HD]
        y = x * lax.rsqrt(jnp.mean(x * x, axis=-1, keepdims=True) + RMS_EPS) * qg_ref[...]
        q_ref[:, h * ATT_HD:(h + 1) * ATT_HD] = (y * (ATT_HD ** -0.5)).astype(BF16)
    for n in range(ATT_KV_HEADS):
        x = p_ref[:, DSA_K_OFF + n * ATT_HD:DSA_K_OFF + (n + 1) * ATT_HD]
        y = x * lax.rsqrt(jnp.mean(x * x, axis=-1, keepdims=True) + RMS_EPS) * kg_ref[...]
        kt_ref[0, n * ATT_HD:(n + 1) * ATT_HD, :] = y.T.astype(BF16)
    v_ref[...] = p_ref[:, DSA_V_OFF:DSA_V_OFF + ATT_KV_DIM].astype(BF16)
    qi_ref[...] = p_ref[:, DSA_QI_OFF:DSA_QI_OFF + IDX_HEADS * LANES].astype(BF16)
    x = p_ref[:, DSA_KI_OFF:DSA_KI_OFF + LANES]
    live = lane < IDX_HD
    mu = jnp.sum(x, axis=-1, keepdims=True) * (1.0 / IDX_HD)
    xc = jnp.where(live, x - mu, 0.0)
    var = jnp.sum(xc * xc, axis=-1, keepdims=True) * (1.0 / IDX_HD)
    ki = xc * lax.rsqrt(var + RMS_EPS) * lng_ref[...] + lnb_ref[...]
    kit_ref[0] = jnp.where(live, ki, 0.0).T.astype(BF16)
    wi_ref[...] = p_ref[:, DSA_WI_OFF:DSA_WI_OFF + LANES] * (IDX_HEADS ** -0.5 * IDX_HD ** -0.5)


def _dsa_prep(proj2d, q_gain, k_gain, ln_g, ln_b):
    T, W = proj2d.shape
    tm = KEY_TILE
    assert T % tm == 0
    lng = jnp.zeros((1, LANES), F32).at[0, 0:IDX_HD].set(ln_g.astype(F32))
    lnb = jnp.zeros((1, LANES), F32).at[0, 0:IDX_HD].set(ln_b.astype(F32))
    row = lambda i: (i, 0)
    fix = lambda i: (0, 0)
    blk = lambda i: (i, 0, 0)
    return pl.pallas_call(
        _dsa_prep_kernel,
        grid=(T // tm,),
        in_specs=[pl.BlockSpec((tm, W), row),
                  pl.BlockSpec((1, ATT_HD), fix), pl.BlockSpec((1, ATT_HD), fix),
                  pl.BlockSpec((1, LANES), fix), pl.BlockSpec((1, LANES), fix)],
        out_specs=[pl.BlockSpec((tm, ATT_Q_DIM), row), pl.BlockSpec((1, ATT_KV_DIM, tm), blk),
                   pl.BlockSpec((tm, ATT_KV_DIM), row), pl.BlockSpec((tm, IDX_HEADS * LANES), row),
                   pl.BlockSpec((1, LANES, tm), blk), pl.BlockSpec((tm, LANES), row)],
        out_shape=[jax.ShapeDtypeStruct((T, ATT_Q_DIM), BF16), jax.ShapeDtypeStruct((T // tm, ATT_KV_DIM, tm), BF16),
                   jax.ShapeDtypeStruct((T, ATT_KV_DIM), BF16), jax.ShapeDtypeStruct((T, IDX_HEADS * LANES), BF16),
                   jax.ShapeDtypeStruct((T // tm, LANES, tm), BF16), jax.ShapeDtypeStruct((T, LANES), F32)],
        compiler_params=_compiler_params(("parallel",)),
        name="dsa_prep",
    )(proj2d, q_gain.reshape(1, ATT_HD), k_gain.reshape(1, ATT_HD), lng, lnb)


def _dsa_kernel(q_ref, qi_ref, wi_ref, kt_ref, v_ref, kit_ref, bias_ref, o_ref,
                sc_ref, qg_ref, wib_ref, m_ref, l_ref, acc_ref, *, n_select, live_rows):
    i = pl.program_id(1)
    TQ = ATT_TILE
    TK = KEY_TILE
    HALVES = TK // LANES

    @pl.when(i * TQ >= live_rows)
    def _():
        o_ref[...] = jnp.zeros_like(o_ref)

    @pl.when(i * TQ < live_rows)
    def _():
        _dsa_tile(i, q_ref, qi_ref, wi_ref, kt_ref, v_ref, kit_ref, bias_ref, o_ref,
                  sc_ref, qg_ref, wib_ref, m_ref, l_ref, acc_ref, n_select, TQ, TK, HALVES)


def _dsa_tile(i, q_ref, qi_ref, wi_ref, kt_ref, v_ref, kit_ref, bias_ref, o_ref,
              sc_ref, qg_ref, wib_ref, m_ref, l_ref, acc_ref, n_select, TQ, TK, HALVES):
    nkt = lax.shift_right_logical(i, 1) + 1
    fmin = float(jnp.finfo(F32).min)
    kf = float(n_select)

    q_row = i * TQ + lax.broadcasted_iota(jnp.int32, (TQ, 1), 0)
    q_chunk = lax.shift_right_logical(q_row, int(math.log2(CHUNK)))
    k_lane = lax.broadcasted_iota(jnp.int32, (1, TK), 1)

    def halves(x):
        return [x[:, j * LANES:(j + 1) * LANES] for j in range(HALVES)]

    def lane_wide(col):
        return jnp.broadcast_to(col, (TQ, LANES))

    wi = wi_ref[0]
    for h in range(IDX_HEADS):
        wib_ref[h] = jnp.broadcast_to(wi[:, h:h + 1], (TQ, TK))

    def score_tile(kt, carry):
        ki = kit_ref[0, kt]
        acc = jnp.zeros((TQ, TK), F32)
        for h in range(IDX_HEADS):
            d = jnp.dot(qi_ref[0, :, h * LANES:(h + 1) * LANES], ki, preferred_element_type=F32)
            acc = acc + wib_ref[h] * jnp.maximum(d, 0.0)
        sc_ref[kt] = acc
        return carry

    def paired_loop(n, fn):
        def body(j, carry):
            fn(2 * j, carry)
            fn(2 * j + 1, carry)
            return carry

        lax.fori_loop(0, lax.shift_right_arithmetic(n, 1), body, 0)

        @pl.when(lax.bitwise_and(n, 1) == 1)
        def _():
            fn(n - 1, 0)

    paired_loop(nkt, score_tile)

    def mask_inadmissible(kt):
        k_pos = kt * TK + k_lane
        adm = (lax.shift_right_logical(k_pos, int(math.log2(CHUNK))) <= q_chunk) & (k_pos >= META_PAD)
        sc_ref[kt] = jnp.where(adm, sc_ref[kt], NEG_INF)

    mask_inadmissible(0)

    @pl.when(nkt > 1)
    def _():
        mask_inadmissible(nkt - 1)

    n_adm = CHUNK * (q_chunk + 1) - META_PAD
    need = n_adm > n_select

    def row_sum(x):
        return jnp.sum(x, axis=-1, keepdims=True)

    def count_ge(t):
        tb = lane_wide(t)

        def body(kt, acc):
            for s in halves(sc_ref[kt]):
                acc = acc + jnp.where(s >= tb, 1.0, 0.0)
            return acc

        return row_sum(lax.fori_loop(0, nkt, body, jnp.zeros((TQ, LANES), F32)))

    def minmax_body(kt, carry):
        mn, mx = carry
        for s in halves(sc_ref[kt]):
            mn = jnp.minimum(mn, jnp.where(s == NEG_INF, jnp.inf, s))
            mx = jnp.maximum(mx, s)
        return mn, mx

    mn_l, mx_l = lax.fori_loop(0, nkt, minmax_body,
                               (jnp.full((TQ, LANES), jnp.inf, F32), jnp.full((TQ, LANES), NEG_INF, F32)))
    s_min = jnp.min(mn_l, axis=-1, keepdims=True)
    s_max = jnp.max(mx_l, axis=-1, keepdims=True)
    lo0 = s_min
    hi0 = s_max + (s_max - s_min) + 1.0
    cnt0 = n_adm.astype(F32)
    active0 = jnp.where(need, 1.0, 0.0)

    def any_active(active):
        return jnp.max(active) > 0.0

    def bis_step(lo, hi, cnt, active):
        mid = lo + (hi - lo) * 0.5
        c = count_ge(mid)
        ge = c >= kf
        moving = (mid > lo) & (mid < hi) & (active > 0.0)
        lo_n = jnp.where(moving & ge, mid, lo)
        hi_n = jnp.where(moving & (~ge), mid, hi)
        cnt_n = jnp.where(moving & ge, c, cnt)
        active_n = jnp.where(moving & (cnt_n != kf), 1.0, 0.0)
        return lo_n, hi_n, cnt_n, active_n

    def bis_body(st):
        it, rest = st[0], st[1:]
        for _ in range(BISECT_UNROLL):
            rest = bis_step(*rest)
        return (it + BISECT_UNROLL,) + rest

    _, lo, hi, cnt, _ = lax.while_loop(
        lambda st: jnp.logical_and(st[0] < BISECT_STEPS, any_active(st[4])), bis_body,
        (jnp.int32(0), lo0, hi0, cnt0, active0))

    def snap_body(st):
        lo, hi, cnt, active = st
        hb = lane_wide(hi)

        def body(kt, mx):
            for s in halves(sc_ref[kt]):
                mx = jnp.maximum(mx, jnp.where(s < hb, s, NEG_INF))
            return mx

        v1 = jnp.max(lax.fori_loop(0, nkt, body, jnp.full((TQ, LANES), NEG_INF, F32)),
                     axis=-1, keepdims=True)
        c1 = count_ge(v1)
        on = active > 0.0
        hit = on & (c1 >= kf)
        lo_n = jnp.where(hit, v1, lo)
        cnt_n = jnp.where(hit, c1, cnt)
        hi_n = jnp.where(on & (~hit), v1, hi)
        return lo_n, hi_n, cnt_n, jnp.where(on & (~hit), 1.0, 0.0)

    unsettled = jnp.where(need & (cnt != kf), 1.0, 0.0)
    lo, hi, cnt, _ = lax.while_loop(lambda st: any_active(st[3]), snap_body, (lo, hi, cnt, unsettled))

    thr = jnp.where(need, lo, fmin)
    tied = need & (cnt > kf)
    big_pos = jnp.int32(2 ** 30)

    def tie_break():
        tb = lane_wide(thr)

        def count_pred(pred):
            def body(kt, acc):
                pos = kt * TK + k_lane
                for j, s in enumerate(halves(sc_ref[kt])):
                    acc = acc + jnp.where(pred(s, pos[:, j * LANES:(j + 1) * LANES]), 1.0, 0.0)
                return acc
            return row_sum(lax.fori_loop(0, nkt, body, jnp.zeros((TQ, LANES), F32)))

        want = kf - count_pred(lambda s, pos: s > tb)

        def body(_, st):
            jl, jh = st
            jm = lax.shift_right_arithmetic(jl + jh, 1)
            c = count_pred(lambda s, pos: (s == tb) & (pos <= jm))
            ok = c >= want
            return jnp.where(ok, jl, jm), jnp.where(ok, jm, jh)

        n_keys = sc_ref.shape[0] * TK
        steps = int(math.ceil(math.log2(n_keys + 1))) + 1
        jl0 = jnp.full((TQ, 1), -1, jnp.int32)
        jh0 = jnp.full((TQ, 1), n_keys, jnp.int32)
        _, jh = lax.fori_loop(0, steps, body, (jl0, jh0))
        return jnp.where(tied, jh, big_pos)

    pos_cap = lax.cond(jnp.max(jnp.where(tied, 1.0, 0.0)) > 0.0, tie_break,
                       lambda: jnp.full((TQ, 1), big_pos, jnp.int32))

    def to_mask(kt, carry):
        s = sc_ref[kt]
        pos = kt * TK + k_lane
        sel = (s > thr) | ((s == thr) & (pos <= pos_cap))
        sc_ref[kt] = jnp.where(sel, 0.0, NEG_INF)
        return carry

    lax.fori_loop(0, nkt, to_mask, 0)

    for n in range(ATT_KV_HEADS):
        for g in range(ATT_GROUP):
            h = n * ATT_GROUP + g
            qg_ref[n, g * TQ:(g + 1) * TQ, :] = q_ref[0, :, h * ATT_HD:(h + 1) * ATT_HD]
    m_ref[...] = jnp.full(m_ref.shape, -1e30, F32)
    l_ref[...] = jnp.zeros_like(l_ref)
    acc_ref[...] = jnp.zeros_like(acc_ref)

    def logits(kt, n, b0):
        kk = kt_ref[0, kt, n * ATT_HD:(n + 1) * ATT_HD, :]
        s = jnp.dot(qg_ref[n], kk, preferred_element_type=F32)
        mask = sc_ref[kt]
        out = []
        for g in range(ATT_GROUP):
            sg = s[g * TQ:(g + 1) * TQ, :] + mask
            if b0 is not None:
                h = n * ATT_GROUP + g
                sg = sg + jnp.concatenate([bias_ref[h, b0 + j] for j in range(HALVES)], axis=1)
            out.append(sg)
        return out

    def max_pass(kt, b0):
        for n in range(ATT_KV_HEADS):
            for g, sg in enumerate(logits(kt, n, b0)):
                rows = slice(g * TQ, (g + 1) * TQ)
                m = m_ref[n, rows, :]
                for part in halves(sg):
                    m = jnp.maximum(m, part)
                m_ref[n, rows, :] = m

    def exp_pass(kt, b0):
        start = pl.multiple_of(kt * TK, TK)
        for n in range(ATT_KV_HEADS):
            ps = []
            for g, sg in enumerate(logits(kt, n, b0)):
                rows = slice(g * TQ, (g + 1) * TQ)
                m = m_ref[n, rows, :]
                parts = [jnp.exp(part - m) for part in halves(sg)]
                l_ref[n, rows, :] += functools.reduce(lambda a, b: a + b, parts)
                ps.append(jnp.concatenate(parts, axis=1).astype(BF16))
            p = jnp.concatenate(ps, axis=0)
            vv = v_ref[0, pl.ds(start, TK), n * ATT_HD:(n + 1) * ATT_HD]
            acc_ref[n] += jnp.dot(p, vv, preferred_element_type=F32)

    n_far = lax.shift_right_arithmetic(i - 1, 1)
    parity = lax.bitwise_and(i, 1)

    def run_pass(tile_fn):
        paired_loop(jnp.maximum(n_far, 0), lambda kt, carry: tile_fn(kt, None))

        @pl.when(i >= 1)
        def _():
            tile_fn(n_far, parity)

        @pl.when(parity == 0)
        def _():
            tile_fn(lax.shift_right_logical(i, 1), 2)

    run_pass(max_pass)
    for n in range(ATT_KV_HEADS):
        m_ref[n] = jnp.broadcast_to(jnp.max(m_ref[n], axis=-1, keepdims=True), m_ref.shape[1:])
    run_pass(exp_pass)

    for n in range(ATT_KV_HEADS):
        inv_l = 1.0 / jnp.sum(l_ref[n], axis=-1, keepdims=True)
        for g in range(ATT_GROUP):
            h = n * ATT_GROUP + g
            rows = slice(g * TQ, (g + 1) * TQ)
            o_ref[0, :, h * ATT_HD:(h + 1) * ATT_HD] = (acc_ref[n, rows, :] * inv_l[rows, :]).astype(o_ref.dtype)


def _t5_bucket(rel):
    nb = REL_BUCKETS // 2
    max_exact = nb // 2
    ret = jnp.where(rel > 0, nb, 0)
    n = jnp.abs(rel)
    nf = jnp.maximum(n, 1).astype(jnp.float32)
    large = max_exact + (jnp.log(nf / max_exact) / math.log(REL_MAX_DIST / max_exact)
                         * (nb - max_exact)).astype(jnp.int32)
    large = jnp.minimum(large, nb - 1)
    return ret + jnp.where(n < max_exact, n, large)


def _dsa_core(q, kt, v, qi, kit, wi, rel_bias, n_select, live_rows):
    B, LP, _ = q.shape
    nq = LP // ATT_TILE
    nkt = LP // KEY_TILE
    a = jnp.arange(ATT_TILE)[:, None]
    xx = jnp.arange(2 * KEY_TILE)[None, :]
    rel = (xx - KEY_TILE) - a
    far = rel_bias.astype(F32)[REL_BUCKETS // 2 - 1]
    onehot = jax.nn.one_hot(_t5_bucket(rel), REL_BUCKETS, dtype=F32)
    table = jnp.einsum("abk,kh->abh", onehot, rel_bias.astype(F32),
                       precision=lax.Precision.HIGHEST) - far
    bias_near = jnp.transpose(table.reshape(ATT_TILE, 2 * KEY_TILE // LANES, LANES, ATT_HEADS), (3, 1, 0, 2))
    kern = functools.partial(_dsa_kernel, n_select=n_select, live_rows=live_rows)
    tile = lambda b, i: (b, i, 0)
    whole3 = lambda b, i: (b, 0, 0)
    whole4 = lambda b, i: (b, 0, 0, 0)
    return pl.pallas_call(
        kern,
        grid=(B, nq),
        in_specs=[pl.BlockSpec((1, ATT_TILE, ATT_Q_DIM), tile),
                  pl.BlockSpec((1, ATT_TILE, IDX_HEADS * LANES), tile),
                  pl.BlockSpec((1, ATT_TILE, LANES), tile),
                  pl.BlockSpec((1, nkt, ATT_KV_DIM, KEY_TILE), whole4),
                  pl.BlockSpec((1, LP, ATT_KV_DIM), whole3),
                  pl.BlockSpec((1, nkt, LANES, KEY_TILE), whole4),
                  pl.BlockSpec((ATT_HEADS, 2 * KEY_TILE // LANES, ATT_TILE, LANES), lambda b, i: (0, 0, 0, 0))],
        out_specs=pl.BlockSpec((1, ATT_TILE, ATT_Q_DIM), tile),
        out_shape=jax.ShapeDtypeStruct((B, LP, ATT_Q_DIM), BF16),
        scratch_shapes=[pltpu.VMEM((nkt, ATT_TILE, KEY_TILE), F32),
                        pltpu.VMEM((ATT_KV_HEADS, ATT_GROUP * ATT_TILE, ATT_HD), BF16),
                        pltpu.VMEM((IDX_HEADS, ATT_TILE, KEY_TILE), F32),
                        pltpu.VMEM((ATT_KV_HEADS, ATT_GROUP * ATT_TILE, LANES), F32),
                        pltpu.VMEM((ATT_KV_HEADS, ATT_GROUP * ATT_TILE, LANES), F32),
                        pltpu.VMEM((ATT_KV_HEADS, ATT_GROUP * ATT_TILE, ATT_HD), F32)],
        compiler_params=_compiler_params(("parallel", "arbitrary")),
        name="dsa_core",
    )(q, qi, wi, kt, v, kit, bias_near)


def _gdn_in_weight(w_in):
    D = w_in.shape[0]
    w = jnp.zeros((D, GDN_PROJ_W), F32).at[:, 0:w_in.shape[1]].set(w_in.astype(F32))
    return w.astype(BF16)


def _dsa_in_weight(w_in):
    D = w_in.shape[0]
    w = jnp.zeros((D, DSA_PROJ_W), F32)
    w = w.at[:, 0:DSA_QI_OFF].set(w_in[:, 0:DSA_QI_OFF].astype(F32))
    src = ATT_Q_DIM + 2 * ATT_KV_DIM
    for h in range(IDX_HEADS):
        w = w.at[:, DSA_QI_OFF + h * LANES:DSA_QI_OFF + h * LANES + IDX_HD].set(
            w_in[:, src + h * IDX_HD:src + (h + 1) * IDX_HD].astype(F32))
    src += IDX_HEADS * IDX_HD
    w = w.at[:, DSA_KI_OFF:DSA_KI_OFF + IDX_HD].set(w_in[:, src:src + IDX_HD].astype(F32))
    src += IDX_HD
    w = w.at[:, DSA_WI_OFF:DSA_WI_OFF + IDX_HEADS].set(w_in[:, src:src + IDX_HEADS].astype(F32))
    return w.astype(BF16)


def kernel(x, meta_tokens, norm_mix, norm_ffn, rel_bias, gdn_w_in, gdn_conv, gdn_a_log, gdn_dt_bias,
           gdn_o_norm, gdn_w_out, dsa_w_in, dsa_q_norm, dsa_k_norm, dsa_idx_ln_g, dsa_idx_ln_b, dsa_w_out,
           moe_w_group, moe_b_group, moe_w_expert, moe_b_expert, moe_w_gate_up, moe_w_down):
    B, S, D = x.shape
    depth = norm_mix.shape[0]
    n_select = min(TOPK_MAX, S // 4)
    off = META_PAD + N_META
    LP = -(-(off + S) // ROW_ALIGN) * ROW_ALIGN
    meta = jnp.broadcast_to(meta_tokens.astype(x.dtype)[None], (B, N_META, D))
    h = jnp.concatenate([jnp.zeros((B, META_PAD, D), x.dtype), meta, x,
                         jnp.zeros((B, LP - off - S, D), x.dtype)], axis=1)
    h = h.reshape(B * LP, D)
    for i in range(depth):
        j = i // 2
        if i % 2 == 0:
            proj = _norm_matmul(h, norm_mix[i], _gdn_in_weight(gdn_w_in[j]))
            o = _gdn_core(proj.reshape(B, LP, GDN_PROJ_W), gdn_conv[j].astype(F32), gdn_a_log[j],
                          gdn_dt_bias[j], gdn_o_norm[j])
            h = _matmul_residual(o.reshape(B * LP, GDN_V_DIM), gdn_w_out[j].astype(BF16), h)
        else:
            proj = _norm_matmul(h, norm_mix[i], _dsa_in_weight(dsa_w_in[j]))
            q, kt, v, qi, kit, wi = _dsa_prep(proj, dsa_q_norm[j], dsa_k_norm[j], dsa_idx_ln_g[j], dsa_idx_ln_b[j])
            r3 = lambda t: t.reshape(B, LP, t.shape[-1])
            r4 = lambda t: t.reshape(B, LP // KEY_TILE, t.shape[-2], KEY_TILE)
            o = _dsa_core(r3(q), r4(kt), r3(v), r3(qi), r4(kit), r3(wi), rel_bias, n_select, off + S)
            h = _matmul_residual(o.reshape(B * LP, ATT_Q_DIM), dsa_w_out[j].astype(BF16), h)
        h = _moe(h, norm_ffn[i], moe_w_group[i], moe_b_group[i], moe_w_expert[i], moe_b_expert[i],
                 moe_w_gate_up[i], moe_w_down[i])
    return h.reshape(B, LP, D)[:, off:off + S]
```

```python
import functools
import math

import jax
import jax.numpy as jnp
from jax import lax
from jax.experimental import pallas as pl
from jax.experimental.pallas import tpu as pltpu

F32 = jnp.float32
BF16 = jnp.bfloat16

LANES = 128
SUBLANES = 8
VMEM_LIMIT_BYTES = 56 * 1024 * 1024

CHUNK = 64
N_META = 16
META_PAD = (-N_META) % CHUNK
RMS_EPS = 1e-6
L2_EPS = 1e-6

GDN_HEADS = 8
GDN_DK = 128
GDN_DV = 128
GDN_CONV = 4
GDN_QK_DIM = GDN_HEADS * GDN_DK
GDN_V_DIM = GDN_HEADS * GDN_DV
GDN_CONV_DIM = 2 * GDN_QK_DIM + GDN_V_DIM
GDN_GATE_OFF = GDN_CONV_DIM + GDN_V_DIM
GDN_PROJ_W = GDN_GATE_OFF + LANES

ATT_HEADS = 8
ATT_KV_HEADS = 2
ATT_GROUP = ATT_HEADS // ATT_KV_HEADS
ATT_HD = 128
ATT_Q_DIM = ATT_HEADS * ATT_HD
ATT_KV_DIM = ATT_KV_HEADS * ATT_HD
IDX_HEADS = 8
IDX_HD = 64
TOPK_MAX = 256
REL_BUCKETS = 32
REL_MAX_DIST = 128

DSA_Q_OFF = 0
DSA_K_OFF = ATT_Q_DIM
DSA_V_OFF = DSA_K_OFF + ATT_KV_DIM
DSA_QI_OFF = DSA_V_OFF + ATT_KV_DIM
DSA_KI_OFF = DSA_QI_OFF + IDX_HEADS * LANES
DSA_WI_OFF = DSA_KI_OFF + LANES
DSA_PROJ_W = DSA_WI_OFF + LANES

N_GROUPS = 4
EXPERTS_PER_GROUP = 8
N_EXPERTS = N_GROUPS * EXPERTS_PER_GROUP
EXPERT_FF = 256

ATT_TILE = 128
KEY_TILE = 256
ROW_ALIGN = KEY_TILE
BISECT_UNROLL = 4
BISECT_STEPS = 24
NEG_INF = float("-inf")


def _pick_tile(n, candidates):
    for c in candidates:
        if n % c == 0:
            return c
    raise ValueError(f"no tile for {n}")


def _compiler_params(semantics):
    return pltpu.CompilerParams(dimension_semantics=semantics, vmem_limit_bytes=VMEM_LIMIT_BYTES)


def _silu(x):
    return x * jax.nn.sigmoid(x)


def _norm_matmul_kernel(x_ref, g_ref, w_ref, o_ref):
    x = x_ref[...]
    y = x * lax.rsqrt(jnp.mean(x * x, axis=-1, keepdims=True) + RMS_EPS) * g_ref[...]
    o_ref[...] = jnp.dot(y.astype(BF16), w_ref[...], preferred_element_type=F32)


def _norm_matmul(h2d, gain, w_bf16):
    T, D = h2d.shape
    N = w_bf16.shape[1]
    tm = _pick_tile(T, (256, 128))
    return pl.pallas_call(
        _norm_matmul_kernel,
        grid=(T // tm,),
        in_specs=[pl.BlockSpec((tm, D), lambda i: (i, 0)),
                  pl.BlockSpec((1, D), lambda i: (0, 0)),
                  pl.BlockSpec((D, N), lambda i: (0, 0))],
        out_specs=pl.BlockSpec((tm, N), lambda i: (i, 0)),
        out_shape=jax.ShapeDtypeStruct((T, N), F32),
        compiler_params=_compiler_params(("parallel",)),
        name="norm_matmul",
    )(h2d, gain.reshape(1, D), w_bf16)


def _matmul_residual_kernel(a_ref, w_ref, r_ref, o_ref):
    o_ref[...] = r_ref[...] + jnp.dot(a_ref[...], w_ref[...], preferred_element_type=F32)


def _matmul_residual(a_bf16, w_bf16, res):
    T, K = a_bf16.shape
    N = w_bf16.shape[1]
    tm = _pick_tile(T, (512, 256, 128))
    return pl.pallas_call(
        _matmul_residual_kernel,
        grid=(T // tm,),
        in_specs=[pl.BlockSpec((tm, K), lambda i: (i, 0)),
                  pl.BlockSpec((K, N), lambda i: (0, 0)),
                  pl.BlockSpec((tm, N), lambda i: (i, 0))],
        out_specs=pl.BlockSpec((tm, N), lambda i: (i, 0)),
        out_shape=jax.ShapeDtypeStruct((T, N), F32),
        compiler_params=_compiler_params(("parallel",)),
        name="matmul_residual",
    )(a_bf16, w_bf16, res)


def _nt_dot(a, b):
    return lax.dot_general(a, b, (((1,), (1,)), ((), ())), preferred_element_type=F32)


def _tn_dot(a, b):
    return lax.dot_general(a, b, (((0,), (0,)), ((), ())), preferred_element_type=F32)


def _gdn_kernel(proj_ref, conv_ref, gate_ref, ogain_ref, o_ref, state_ref, xbuf_ref, qkv_ref):
    c = pl.program_id(1)
    C = CHUNK

    @pl.when(c == 0)
    def _():
        state_ref[...] = jnp.zeros_like(state_ref)
        xbuf_ref[0:SUBLANES, :] = jnp.zeros((SUBLANES, GDN_CONV_DIM), F32)

    row = c * C + lax.broadcasted_iota(jnp.int32, (C, 1), 0)
    live = row >= META_PAD
    u = jnp.where(live, proj_ref[0, :, 0:GDN_CONV_DIM], 0.0)
    xbuf_ref[SUBLANES:SUBLANES + C, :] = u
    conv = conv_ref[0:1, :] * xbuf_ref[SUBLANES - 3:SUBLANES - 3 + C, :]
    for j in range(1, GDN_CONV):
        conv = conv + conv_ref[j:j + 1, :] * xbuf_ref[SUBLANES - 3 + j:SUBLANES - 3 + j + C, :]
    xbuf_ref[0:SUBLANES, :] = u[C - SUBLANES:C, :]
    qkv_ref[...] = _silu(conv)
    qkv = qkv_ref

    gates = jnp.where(live, proj_ref[0, :, GDN_GATE_OFF:GDN_GATE_OFF + LANES], 0.0)
    beta = jax.nn.sigmoid(gates)
    neg_rate = -jnp.exp(gate_ref[0:1, :])
    g = neg_rate * jax.nn.softplus(gates + gate_ref[1:2, :])
    r_i = lax.broadcasted_iota(jnp.int32, (C, C), 0)
    c_i = lax.broadcasted_iota(jnp.int32, (C, C), 1)
    causal = c_i <= r_i
    strict = c_i < r_i
    gc = jnp.dot(causal.astype(F32), g, preferred_element_type=F32,
                 precision=lax.Precision.HIGHEST)
    gc_t = gc.T
    eg = jnp.exp(gc)
    g_last = gc[C - 1:C, :]
    eg_last = jnp.exp(g_last)
    ek = jnp.exp(g_last - gc)

    heads = range(GDN_HEADS)
    dot = functools.partial(jnp.dot, preferred_element_type=F32)
    col = lambda a, h: a[:, GDN_HEADS + h:GDN_HEADS + h + 1]
    qs, ks, k16s, xs, decays = [], [], [], [], []
    for h in heads:
        lo = h * GDN_DK
        q = qkv[:, lo:lo + GDN_DK]
        k = qkv[:, GDN_QK_DIM + lo:GDN_QK_DIM + lo + GDN_DK]
        v = qkv[:, 2 * GDN_QK_DIM + h * GDN_DV:2 * GDN_QK_DIM + (h + 1) * GDN_DV]
        q = q * lax.rsqrt(jnp.sum(q * q, axis=-1, keepdims=True) + L2_EPS) * (GDN_DK ** -0.5)
        k = k * lax.rsqrt(jnp.sum(k * k, axis=-1, keepdims=True) + L2_EPS)
        kb = k * beta[:, h:h + 1]
        vb = v * beta[:, h:h + 1]
        qs.append(q)
        ks.append(k)
        k16s.append(k.astype(BF16))
        xs.append((kb, jnp.concatenate([vb, kb * col(eg, h)], axis=1)))
        decays.append(jnp.exp(jnp.where(causal, col(gc, h) - gc_t[GDN_HEADS + h:GDN_HEADS + h + 1, :],
                                        NEG_INF)))
    kk = [_nt_dot(xs[h][0].astype(BF16), k16s[h]) for h in heads]
    qk = [_nt_dot(qs[h].astype(BF16), k16s[h]) for h in heads]
    ns = [-jnp.where(strict, kk[h] * decays[h], 0.0) for h in heads]
    xs = [x for _, x in xs]
    for it in range(6):
        n16 = [n.astype(BF16) for n in ns]
        xs = [xs[h] + dot(n16[h], xs[h].astype(BF16)) for h in heads]
        if it < 5:
            ns = [dot(n16[h], n16[h]) for h in heads]
    qk = [jnp.where(causal, qk[h] * decays[h], 0.0).astype(BF16) for h in heads]
    states = [state_ref[h] for h in heads]
    s16 = [s.astype(BF16) for s in states]
    ws = [dot(xs[h][:, GDN_DV:2 * GDN_DV].astype(BF16), s16[h]) for h in heads]
    qs_s = [dot((qs[h] * col(eg, h)).astype(BF16), s16[h]) for h in heads]
    v_new = [(xs[h][:, 0:GDN_DV] - ws[h]).astype(BF16) for h in heads]
    os_ = [qs_s[h] + dot(qk[h], v_new[h]) for h in heads]
    kv = [_tn_dot((ks[h] * col(ek, h)).astype(BF16), v_new[h]) for h in heads]
    for h in heads:
        state_ref[h] = states[h] * col(eg_last, h) + kv[h]
        z = proj_ref[0, :, GDN_CONV_DIM + h * GDN_DV:GDN_CONV_DIM + (h + 1) * GDN_DV]
        o = os_[h]
        on = o * lax.rsqrt(jnp.mean(o * o, axis=-1, keepdims=True) + RMS_EPS) * ogain_ref[...]
        o_ref[0, :, h * GDN_DV:(h + 1) * GDN_DV] = (on * _silu(z)).astype(o_ref.dtype)


def _gdn_core(proj, conv_w, a_log, dt_bias, o_gain):
    B, LP, W = proj.shape
    nc = LP // CHUNK
    gate = jnp.zeros((2, LANES), F32)
    gate = gate.at[0, GDN_HEADS:2 * GDN_HEADS].set(a_log.astype(F32))
    gate = gate.at[1, GDN_HEADS:2 * GDN_HEADS].set(dt_bias.astype(F32))
    return pl.pallas_call(
        _gdn_kernel,
        grid=(B, nc),
        in_specs=[pl.BlockSpec((1, CHUNK, W), lambda b, c: (b, c, 0)),
                  pl.BlockSpec((GDN_CONV, GDN_CONV_DIM), lambda b, c: (0, 0)),
                  pl.BlockSpec((2, LANES), lambda b, c: (0, 0)),
                  pl.BlockSpec((1, GDN_DV), lambda b, c: (0, 0))],
        out_specs=pl.BlockSpec((1, CHUNK, GDN_V_DIM), lambda b, c: (b, c, 0)),
        out_shape=jax.ShapeDtypeStruct((B, LP, GDN_V_DIM), BF16),
        scratch_shapes=[pltpu.VMEM((GDN_HEADS, GDN_DK, GDN_DV), F32),
                        pltpu.VMEM((SUBLANES + CHUNK, GDN_CONV_DIM), F32),
                        pltpu.VMEM((CHUNK, GDN_CONV_DIM), F32)],
        compiler_params=_compiler_params(("parallel", "arbitrary")),
        name="gdn_core",
    )(proj, conv_w, gate, o_gain.reshape(1, GDN_DV))


def _moe_kernel(h_ref, gain_ref, wr_ref, br_ref, wgu_ref, wd_ref, o_ref, xn_ref, comb_ref, acc_ref):
    e = pl.program_id(1)
    lane = lax.broadcasted_iota(jnp.int32, (1, LANES), 1)

    @pl.when(e == 0)
    def _():
        x = h_ref[...]
        xn = x * lax.rsqrt(jnp.mean(x * x, axis=-1, keepdims=True) + RMS_EPS) * gain_ref[...]
        xn_ref[...] = xn.astype(BF16)
        logits = jnp.dot(xn, wr_ref[...], preferred_element_type=F32,
                         precision=lax.Precision.HIGHEST) + br_ref[...]
        e_log = logits[:, 0:LANES]
        g_log = jnp.where(lane < N_GROUPS, logits[:, LANES:2 * LANES], NEG_INF)
        g_max = jnp.max(g_log, axis=-1, keepdims=True)
        g_sel = jnp.min(jnp.where(g_log == g_max, lane, LANES), axis=-1, keepdims=True)
        g_w = 1.0 / jnp.sum(jnp.exp(g_log - g_max), axis=-1, keepdims=True)
        in_grp = (lane >= g_sel * EXPERTS_PER_GROUP) & (lane < (g_sel + 1) * EXPERTS_PER_GROUP)
        l0 = jnp.where(in_grp, e_log, NEG_INF)
        m1 = jnp.max(l0, axis=-1, keepdims=True)
        i1 = jnp.min(jnp.where(l0 == m1, lane, LANES), axis=-1, keepdims=True)
        l1 = jnp.where(lane == i1, NEG_INF, l0)
        m2 = jnp.max(l1, axis=-1, keepdims=True)
        i2 = jnp.min(jnp.where(l1 == m2, lane, LANES), axis=-1, keepdims=True)
        p2 = jnp.exp(m2 - m1)
        w1 = g_w / (1.0 + p2)
        w2 = g_w * p2 / (1.0 + p2)
        comb_ref[...] = jnp.where(lane == i1, w1, 0.0) + jnp.where(lane == i2, w2, 0.0)
        acc_ref[...] = jnp.zeros_like(acc_ref)

    xn = xn_ref[...]
    hh = jnp.dot(xn, wgu_ref[0], preferred_element_type=F32)
    cw = jnp.sum(jnp.where(lane == e, comb_ref[...], 0.0), axis=-1, keepdims=True)
    act = _silu(hh[:, 0:EXPERT_FF]) * hh[:, EXPERT_FF:2 * EXPERT_FF] * cw
    acc_ref[...] += jnp.dot(act.astype(BF16), wd_ref[0], preferred_element_type=F32)

    @pl.when(e == N_EXPERTS - 1)
    def _():
        o_ref[...] = h_ref[...] + acc_ref[...]


def _moe(h2d, gain, w_group, b_group, w_expert, b_expert, w_gate_up, w_down):
    T, D = h2d.shape
    tm = _pick_tile(T, (1408, 1280, 1024, 768, 640, 512, 256, 128))
    wr = jnp.zeros((D, 2 * LANES), F32)
    wr = wr.at[:, 0:N_EXPERTS].set(w_expert.astype(F32))
    wr = wr.at[:, LANES:LANES + N_GROUPS].set(w_group.astype(F32))
    br = jnp.zeros((1, 2 * LANES), F32)
    br = br.at[0, 0:N_EXPERTS].set(b_expert.astype(F32))
    br = br.at[0, LANES:LANES + N_GROUPS].set(b_group.astype(F32))
    wgu = w_gate_up.reshape(N_EXPERTS, D, 2 * EXPERT_FF).astype(BF16)
    wd = w_down.reshape(N_EXPERTS, EXPERT_FF, D).astype(BF16)
    return pl.pallas_call(
        _moe_kernel,
        grid=(T // tm, N_EXPERTS),
        in_specs=[pl.BlockSpec((tm, D), lambda i, e: (i, 0)),
                  pl.BlockSpec((1, D), lambda i, e: (0, 0)),
                  pl.BlockSpec((D, 2 * LANES), lambda i, e: (0, 0)),
                  pl.BlockSpec((1, 2 * LANES), lambda i, e: (0, 0)),
                  pl.BlockSpec((1, D, 2 * EXPERT_FF), lambda i, e: (e, 0, 0)),
                  pl.BlockSpec((1, EXPERT_FF, D), lambda i, e: (e, 0, 0))],
        out_specs=pl.BlockSpec((tm, D), lambda i, e: (i, 0)),
        out_shape=jax.ShapeDtypeStruct((T, D), F32),
        scratch_shapes=[pltpu.VMEM((tm, D), BF16),
                        pltpu.VMEM((tm, LANES), F32),
                        pltpu.VMEM((tm, D), F32)],
        compiler_params=_compiler_params(("parallel", "arbitrary")),
        name="moe",
    )(h2d, gain.reshape(1, D), wr, br, wgu, wd)


def _dsa_prep_kernel(p_ref, qg_ref, kg_ref, lng_ref, lnb_ref, q_ref, kt_ref, v_ref, qi_ref, kit_ref, wi_ref):
    lane = lax.broadcasted_iota(jnp.int32, (1, LANES), 1)
    for h in range(ATT_HEADS):
        x = p_ref[:, DSA_Q_OFF + h * ATT_HD:DSA_Q_OFF + (h + 1) * ATT_HD]
        y = x * lax.rsqrt(jnp.mean(x * x, axis=-1, keepdims=True) + RMS_EPS) * qg_ref[...]
        q_ref[:, h * ATT_HD:(h + 1) * ATT_HD] = (y * (ATT_HD ** -0.5)).astype(BF16)
    for n in range(ATT_KV_HEADS):
        x = p_ref[:, DSA_K_OFF + n * ATT_HD:DSA_K_OFF + (n + 1) * ATT_HD]
        y = x * lax.rsqrt(jnp.mean(x * x, axis=-1, keepdims=True) + RMS_EPS) * kg_ref[...]
        kt_ref[0, n * ATT_HD:(n + 1) * ATT_HD, :] = y.T.astype(BF16)
    v_ref[...] = p_ref[:, DSA_V_OFF:DSA_V_OFF + ATT_KV_DIM].astype(BF16)
    qi_ref[...] = p_ref[:, DSA_QI_OFF:DSA_QI_OFF + IDX_HEADS * LANES].astype(BF16)
    x = p_ref[:, DSA_KI_OFF:DSA_KI_OFF + LANES]
    live = lane < IDX_HD
    mu = jnp.sum(x, axis=-1, keepdims=True) * (1.0 / IDX_HD)
    xc = jnp.where(live, x - mu, 0.0)
    var = jnp.sum(xc * xc, axis=-1, keepdims=True) * (1.0 / IDX_HD)
    ki = xc * lax.rsqrt(var + RMS_EPS) * lng_ref[...] + lnb_ref[...]
    kit_ref[0] = jnp.where(live, ki, 0.0).T.astype(BF16)
    wi_ref[...] = p_ref[:, DSA_WI_OFF:DSA_WI_OFF + LANES] * (IDX_HEADS ** -0.5 * IDX_HD ** -0.5)


def _dsa_prep(proj2d, q_gain, k_gain, ln_g, ln_b):
    T, W = proj2d.shape
    tm = KEY_TILE
    assert T % tm == 0
    lng = jnp.zeros((1, LANES), F32).at[0, 0:IDX_HD].set(ln_g.astype(F32))
    lnb = jnp.zeros((1, LANES), F32).at[0, 0:IDX_HD].set(ln_b.astype(F32))
    row = lambda i: (i, 0)
    fix = lambda i: (0, 0)
    blk = lambda i: (i, 0, 0)
    return pl.pallas_call(
        _dsa_prep_kernel,
        grid=(T // tm,),
        in_specs=[pl.BlockSpec((tm, W), row),
                  pl.BlockSpec((1, ATT_HD), fix), pl.BlockSpec((1, ATT_HD), fix),
                  pl.BlockSpec((1, LANES), fix), pl.BlockSpec((1, LANES), fix)],
        out_specs=[pl.BlockSpec((tm, ATT_Q_DIM), row), pl.BlockSpec((1, ATT_KV_DIM, tm), blk),
                   pl.BlockSpec((tm, ATT_KV_DIM), row), pl.BlockSpec((tm, IDX_HEADS * LANES), row),
                   pl.BlockSpec((1, LANES, tm), blk), pl.BlockSpec((tm, LANES), row)],
        out_shape=[jax.ShapeDtypeStruct((T, ATT_Q_DIM), BF16), jax.ShapeDtypeStruct((T // tm, ATT_KV_DIM, tm), BF16),
                   jax.ShapeDtypeStruct((T, ATT_KV_DIM), BF16), jax.ShapeDtypeStruct((T, IDX_HEADS * LANES), BF16),
                   jax.ShapeDtypeStruct((T // tm, LANES, tm), BF16), jax.ShapeDtypeStruct((T, LANES), F32)],
        compiler_params=_compiler_params(("parallel",)),
        name="dsa_prep",
    )(proj2d, q_gain.reshape(1, ATT_HD), k_gain.reshape(1, ATT_HD), lng, lnb)


def _dsa_kernel(q_ref, qi_ref, wi_ref, kt_ref, v_ref, kit_ref, bias_ref, o_ref,
                sc_ref, qg_ref, wib_ref, m_ref, l_ref, acc_ref, *, n_select, live_rows):
    i = pl.program_id(1)
    TQ = ATT_TILE
    TK = KEY_TILE
    HALVES = TK // LANES

    @pl.when(i * TQ >= live_rows)
    def _():
        o_ref[...] = jnp.zeros_like(o_ref)

    @pl.when(i * TQ < live_rows)
    def _():
        _dsa_tile(i, q_ref, qi_ref, wi_ref, kt_ref, v_ref, kit_ref, bias_ref, o_ref,
                  sc_ref, qg_ref, wib_ref, m_ref, l_ref, acc_ref, n_select, TQ, TK, HALVES)


def _dsa_tile(i, q_ref, qi_ref, wi_ref, kt_ref, v_ref, kit_ref, bias_ref, o_ref,
              sc_ref, qg_ref, wib_ref, m_ref, l_ref, acc_ref, n_select, TQ, TK, HALVES):
    nkt = lax.shift_right_logical(i, 1) + 1
    fmin = float(jnp.finfo(F32).min)
    kf = float(n_select)

    q_row = i * TQ + lax.broadcasted_iota(jnp.int32, (TQ, 1), 0)
    q_chunk = lax.shift_right_logical(q_row, int(math.log2(CHUNK)))
    k_lane = lax.broadcasted_iota(jnp.int32, (1, TK), 1)

    def halves(x):
        return [x[:, j * LANES:(j + 1) * LANES] for j in range(HALVES)]

    def lane_wide(col):
        return jnp.broadcast_to(col, (TQ, LANES))

    wi = wi_ref[0]
    for h in range(IDX_HEADS):
        wib_ref[h] = jnp.broadcast_to(wi[:, h:h + 1], (TQ, TK))

    def score_tile(kt, carry):
        ki = kit_ref[0, kt]
        acc = jnp.zeros((TQ, TK), F32)
        for h in range(IDX_HEADS):
            d = jnp.dot(qi_ref[0, :, h * LANES:(h + 1) * LANES], ki, preferred_element_type=F32)
            acc = acc + wib_ref[h] * jnp.maximum(d, 0.0)
        sc_ref[kt] = acc
        return carry

    def paired_loop(n, fn):
        def body(j, carry):
            fn(2 * j, carry)
            fn(2 * j + 1, carry)
            return carry

        lax.fori_loop(0, lax.shift_right_arithmetic(n, 1), body, 0)

        @pl.when(lax.bitwise_and(n, 1) == 1)
        def _():
            fn(n - 1, 0)

    paired_loop(nkt, score_tile)

    def mask_inadmissible(kt):
        k_pos = kt * TK + k_lane
        adm = (lax.shift_right_logical(k_pos, int(math.log2(CHUNK))) <= q_chunk) & (k_pos >= META_PAD)
        sc_ref[kt] = jnp.where(adm, sc_ref[kt], NEG_INF)

    mask_inadmissible(0)

    @pl.when(nkt > 1)
    def _():
        mask_inadmissible(nkt - 1)

    n_adm = CHUNK * (q_chunk + 1) - META_PAD
    need = n_adm > n_select

    def row_sum(x):
        return jnp.sum(x, axis=-1, keepdims=True)

    def count_ge(t):
        tb = lane_wide(t)

        def body(kt, acc):
            for s in halves(sc_ref[kt]):
                acc = acc + jnp.where(s >= tb, 1.0, 0.0)
            return acc

        return row_sum(lax.fori_loop(0, nkt, body, jnp.zeros((TQ, LANES), F32)))

    def minmax_body(kt, carry):
        mn, mx = carry
        for s in halves(sc_ref[kt]):
            mn = jnp.minimum(mn, jnp.where(s == NEG_INF, jnp.inf, s))
            mx = jnp.maximum(mx, s)
        return mn, mx

    mn_l, mx_l = lax.fori_loop(0, nkt, minmax_body,
                               (jnp.full((TQ, LANES), jnp.inf, F32), jnp.full((TQ, LANES), NEG_INF, F32)))
    s_min = jnp.min(mn_l, axis=-1, keepdims=True)
    s_max = jnp.max(mx_l, axis=-1, keepdims=True)
    lo0 = s_min
    hi0 = s_max + (s_max - s_min) + 1.0
    cnt0 = n_adm.astype(F32)
    active0 = jnp.where(need, 1.0, 0.0)

    def any_active(active):
        return jnp.max(active) > 0.0

    def bis_step(lo, hi, cnt, active):
        mid = lo + (hi - lo) * 0.5
        c = count_ge(mid)
        ge = c >= kf
        moving = (mid > lo) & (mid < hi) & (active > 0.0)
        lo_n = jnp.where(moving & ge, mid, lo)
        hi_n = jnp.where(moving & (~ge), mid, hi)
        cnt_n = jnp.where(moving & ge, c, cnt)
        active_n = jnp.where(moving & (cnt_n != kf), 1.0, 0.0)
        return lo_n, hi_n, cnt_n, active_n

    def bis_body(st):
        it, rest = st[0], st[1:]
        for _ in range(BISECT_UNROLL):
            rest = bis_step(*rest)
        return (it + BISECT_UNROLL,) + rest

    _, lo, hi, cnt, _ = lax.while_loop(
        lambda st: jnp.logical_and(st[0] < BISECT_STEPS, any_active(st[4])), bis_body,
        (jnp.int32(0), lo0, hi0, cnt0, active0))

    def snap_body(st):
        lo, hi, cnt, active = st
        hb = lane_wide(hi)

        def body(kt, mx):
            for s in halves(sc_ref[kt]):
                mx = jnp.maximum(mx, jnp.where(s < hb, s, NEG_INF))
            return mx

        v1 = jnp.max(lax.fori_loop(0, nkt, body, jnp.full((TQ, LANES), NEG_INF, F32)),
                     axis=-1, keepdims=True)
        c1 = count_ge(v1)
        on = active > 0.0
        hit = on & (c1 >= kf)
        lo_n = jnp.where(hit, v1, lo)
        cnt_n = jnp.where(hit, c1, cnt)
        hi_n = jnp.where(on & (~hit), v1, hi)
        return lo_n, hi_n, cnt_n, jnp.where(on & (~hit), 1.0, 0.0)

    unsettled = jnp.where(need & (cnt != kf), 1.0, 0.0)
    lo, hi, cnt, _ = lax.while_loop(lambda st: any_active(st[3]), snap_body, (lo, hi, cnt, unsettled))

    thr = jnp.where(need, lo, fmin)
    tied = need & (cnt > kf)
    big_pos = jnp.int32(2 ** 30)

    def tie_break():
        tb = lane_wide(thr)

        def count_pred(pred):
            def body(kt, acc):
                pos = kt * TK + k_lane
                for j, s in enumerate(halves(sc_ref[kt])):
                    acc = acc + jnp.where(pred(s, pos[:, j * LANES:(j + 1) * LANES]), 1.0, 0.0)
                return acc
            return row_sum(lax.fori_loop(0, nkt, body, jnp.zeros((TQ, LANES), F32)))

        want = kf - count_pred(lambda s, pos: s > tb)

        def body(_, st):
            jl, jh = st
            jm = lax.shift_right_arithmetic(jl + jh, 1)
            c = count_pred(lambda s, pos: (s == tb) & (pos <= jm))
            ok = c >= want
            return jnp.where(ok, jl, jm), jnp.where(ok, jm, jh)

        n_keys = sc_ref.shape[0] * TK
        steps = int(math.ceil(math.log2(n_keys + 1))) + 1
        jl0 = jnp.full((TQ, 1), -1, jnp.int32)
        jh0 = jnp.full((TQ, 1), n_keys, jnp.int32)
        _, jh = lax.fori_loop(0, steps, body, (jl0, jh0))
        return jnp.where(tied, jh, big_pos)

    pos_cap = lax.cond(jnp.max(jnp.where(tied, 1.0, 0.0)) > 0.0, tie_break,
                       lambda: jnp.full((TQ, 1), big_pos, jnp.int32))

    def to_mask(kt, carry):
        s = sc_ref[kt]
        pos = kt * TK + k_lane
        sel = (s > thr) | ((s == thr) & (pos <= pos_cap))
        sc_ref[kt] = jnp.where(sel, 0.0, NEG_INF)
        return carry

    lax.fori_loop(0, nkt, to_mask, 0)

    for n in range(ATT_KV_HEADS):
        for g in range(ATT_GROUP):
            h = n * ATT_GROUP + g
            qg_ref[n, g * TQ:(g + 1) * TQ, :] = q_ref[0, :, h * ATT_HD:(h + 1) * ATT_HD]
    m_ref[...] = jnp.full(m_ref.shape, -1e30, F32)
    l_ref[...] = jnp.zeros_like(l_ref)
    acc_ref[...] = jnp.zeros_like(acc_ref)

    def logits(kt, n, b0):
        kk = kt_ref[0, kt, n * ATT_HD:(n + 1) * ATT_HD, :]
        s = jnp.dot(qg_ref[n], kk, preferred_element_type=F32)
        mask = sc_ref[kt]
        out = []
        for g in range(ATT_GROUP):
            sg = s[g * TQ:(g + 1) * TQ, :] + mask
            if b0 is not None:
                h = n * ATT_GROUP + g
                sg = sg + jnp.concatenate([bias_ref[h, b0 + j] for j in range(HALVES)], axis=1)
            out.append(sg)
        return out

    def max_pass(kt, b0):
        for n in range(ATT_KV_HEADS):
            for g, sg in enumerate(logits(kt, n, b0)):
                rows = slice(g * TQ, (g + 1) * TQ)
                m = m_ref[n, rows, :]
                for part in halves(sg):
                    m = jnp.maximum(m, part)
                m_ref[n, rows, :] = m

    def exp_pass(kt, b0):
        start = pl.multiple_of(kt * TK, TK)
        for n in range(ATT_KV_HEADS):
            ps = []
            for g, sg in enumerate(logits(kt, n, b0)):
                rows = slice(g * TQ, (g + 1) * TQ)
                m = m_ref[n, rows, :]
                parts = [jnp.exp(part - m) for part in halves(sg)]
                l_ref[n, rows, :] += functools.reduce(lambda a, b: a + b, parts)
                ps.append(jnp.concatenate(parts, axis=1).astype(BF16))
            p = jnp.concatenate(ps, axis=0)
            vv = v_ref[0, pl.ds(start, TK), n * ATT_HD:(n + 1) * ATT_HD]
            acc_ref[n] += jnp.dot(p, vv, preferred_element_type=F32)

    n_far = lax.shift_right_arithmetic(i - 1, 1)
    parity = lax.bitwise_and(i, 1)

    def run_pass(tile_fn):
        paired_loop(jnp.maximum(n_far, 0), lambda kt, carry: tile_fn(kt, None))

        @pl.when(i >= 1)
        def _():
            tile_fn(n_far, parity)

        @pl.when(parity == 0)
        def _():
            tile_fn(lax.shift_right_logical(i, 1), 2)

    run_pass(max_pass)
    for n in range(ATT_KV_HEADS):
        m_ref[n] = jnp.broadcast_to(jnp.max(m_ref[n], axis=-1, keepdims=True), m_ref.shape[1:])
    run_pass(exp_pass)

    for n in range(ATT_KV_HEADS):
        inv_l = 1.0 / jnp.sum(l_ref[n], axis=-1, keepdims=True)
        for g in range(ATT_GROUP):
            h = n * ATT_GROUP + g
            rows = slice(g * TQ, (g + 1) * TQ)
            o_ref[0, :, h * ATT_HD:(h + 1) * ATT_HD] = (acc_ref[n, rows, :] * inv_l[rows, :]).astype(o_ref.dtype)


def _t5_bucket(rel):
    nb = REL_BUCKETS // 2
    max_exact = nb // 2
    ret = jnp.where(rel > 0, nb, 0)
    n = jnp.abs(rel)
    nf = jnp.maximum(n, 1).astype(jnp.float32)
    large = max_exact + (jnp.log(nf / max_exact) / math.log(REL_MAX_DIST / max_exact)
                         * (nb - max_exact)).astype(jnp.int32)
    large = jnp.minimum(large, nb - 1)
    return ret + jnp.where(n < max_exact, n, large)


def _dsa_core(q, kt, v, qi, kit, wi, rel_bias, n_select, live_rows):
    B, LP, _ = q.shape
    nq = LP // ATT_TILE
    nkt = LP // KEY_TILE
    a = jnp.arange(ATT_TILE)[:, None]
    xx = jnp.arange(2 * KEY_TILE)[None, :]
    rel = (xx - KEY_TILE) - a
    far = rel_bias.astype(F32)[REL_BUCKETS // 2 - 1]
    onehot = jax.nn.one_hot(_t5_bucket(rel), REL_BUCKETS, dtype=F32)
    table = jnp.einsum("abk,kh->abh", onehot, rel_bias.astype(F32),
                       precision=lax.Precision.HIGHEST) - far
    bias_near = jnp.transpose(table.reshape(ATT_TILE, 2 * KEY_TILE // LANES, LANES, ATT_HEADS), (3, 1, 0, 2))
    kern = functools.partial(_dsa_kernel, n_select=n_select, live_rows=live_rows)
    tile = lambda b, i: (b, i, 0)
    whole3 = lambda b, i: (b, 0, 0)
    whole4 = lambda b, i: (b, 0, 0, 0)
    return pl.pallas_call(
        kern,
        grid=(B, nq),
        in_specs=[pl.BlockSpec((1, ATT_TILE, ATT_Q_DIM), tile),
                  pl.BlockSpec((1, ATT_TILE, IDX_HEADS * LANES), tile),
                  pl.BlockSpec((1, ATT_TILE, LANES), tile),
                  pl.BlockSpec((1, nkt, ATT_KV_DIM, KEY_TILE), whole4),
                  pl.BlockSpec((1, LP, ATT_KV_DIM), whole3),
                  pl.BlockSpec((1, nkt, LANES, KEY_TILE), whole4),
                  pl.BlockSpec((ATT_HEADS, 2 * KEY_TILE // LANES, ATT_TILE, LANES), lambda b, i: (0, 0, 0, 0))],
        out_specs=pl.BlockSpec((1, ATT_TILE, ATT_Q_DIM), tile),
        out_shape=jax.ShapeDtypeStruct((B, LP, ATT_Q_DIM), BF16),
        scratch_shapes=[pltpu.VMEM((nkt, ATT_TILE, KEY_TILE), F32),
                        pltpu.VMEM((ATT_KV_HEADS, ATT_GROUP * ATT_TILE, ATT_HD), BF16),
                        pltpu.VMEM((IDX_HEADS, ATT_TILE, KEY_TILE), F32),
                        pltpu.VMEM((ATT_KV_HEADS, ATT_GROUP * ATT_TILE, LANES), F32),
                        pltpu.VMEM((ATT_KV_HEADS, ATT_GROUP * ATT_TILE, LANES), F32),
                        pltpu.VMEM((ATT_KV_HEADS, ATT_GROUP * ATT_TILE, ATT_HD), F32)],
        compiler_params=_compiler_params(("parallel", "arbitrary")),
        name="dsa_core",
    )(q, qi, wi, kt, v, kit, bias_near)


def _gdn_in_weight(w_in):
    D = w_in.shape[0]
    w = jnp.zeros((D, GDN_PROJ_W), F32).at[:, 0:w_in.shape[1]].set(w_in.astype(F32))
    return w.astype(BF16)


def _dsa_in_weight(w_in):
    D = w_in.shape[0]
    w = jnp.zeros((D, DSA_PROJ_W), F32)
    w = w.at[:, 0:DSA_QI_OFF].set(w_in[:, 0:DSA_QI_OFF].astype(F32))
    src = ATT_Q_DIM + 2 * ATT_KV_DIM
    for h in range(IDX_HEADS):
        w = w.at[:, DSA_QI_OFF + h * LANES:DSA_QI_OFF + h * LANES + IDX_HD].set(
            w_in[:, src + h * IDX_HD:src + (h + 1) * IDX_HD].astype(F32))
    src += IDX_HEADS * IDX_HD
    w = w.at[:, DSA_KI_OFF:DSA_KI_OFF + IDX_HD].set(w_in[:, src:src + IDX_HD].astype(F32))
    src += IDX_HD
    w = w.at[:, DSA_WI_OFF:DSA_WI_OFF + IDX_HEADS].set(w_in[:, src:src + IDX_HEADS].astype(F32))
    return w.astype(BF16)


def kernel(x, meta_tokens, norm_mix, norm_ffn, rel_bias, gdn_w_in, gdn_conv, gdn_a_log, gdn_dt_bias,
           gdn_o_norm, gdn_w_out, dsa_w_in, dsa_q_norm, dsa_k_norm, dsa_idx_ln_g, dsa_idx_ln_b, dsa_w_out,
           moe_w_group, moe_b_group, moe_w_expert, moe_b_expert, moe_w_gate_up, moe_w_down):
    B, S, D = x.shape
    depth = norm_mix.shape[0]
    n_select = min(TOPK_MAX, S // 4)
    off = META_PAD + N_META
    LP = -(-(off + S) // ROW_ALIGN) * ROW_ALIGN
    meta = jnp.broadcast_to(meta_tokens.astype(x.dtype)[None], (B, N_META, D))
    h = jnp.concatenate([jnp.zeros((B, META_PAD, D), x.dtype), meta, x,
                         jnp.zeros((B, LP - off - S, D), x.dtype)], axis=1)
    h = h.reshape(B * LP, D)
    for i in range(depth):
        j = i // 2
        if i % 2 == 0:
            proj = _norm_matmul(h, norm_mix[i], _gdn_in_weight(gdn_w_in[j]))
            o = _gdn_core(proj.reshape(B, LP, GDN_PROJ_W), gdn_conv[j].astype(F32), gdn_a_log[j],
                          gdn_dt_bias[j], gdn_o_norm[j])
            h = _matmul_residual(o.reshape(B * LP, GDN_V_DIM), gdn_w_out[j].astype(BF16), h)
        else:
            proj = _norm_matmul(h, norm_mix[i], _dsa_in_weight(dsa_w_in[j]))
            q, kt, v, qi, kit, wi = _dsa_prep(proj, dsa_q_norm[j], dsa_k_norm[j], dsa_idx_ln_g[j], dsa_idx_ln_b[j])
            r3 = lambda t: t.reshape(B, LP, t.shape[-1])
            r4 = lambda t: t.reshape(B, LP // KEY_TILE, t.shape[-2], KEY_TILE)
            o = _dsa_core(r3(q), r4(kt), r3(v), r3(qi), r4(kit), r3(wi), rel_bias, n_select, off + S)
            h = _matmul_residual(o.reshape(B * LP, ATT_Q_DIM), dsa_w_out[j].astype(BF16), h)
        h = _moe(h, norm_ffn[i], moe_w_group[i], moe_b_group[i], moe_w_expert[i], moe_b_expert[i],
                 moe_w_gate_up[i], moe_w_down[i])
    return h.reshape(B, LP, D)[:, off:off + S]
```

```python
import functools
import math

import jax
import jax.numpy as jnp
from jax import lax
from jax.experimental import pallas as pl
from jax.experimental.pallas import tpu as pltpu

F32 = jnp.float32
BF16 = jnp.bfloat16

LANES = 128
SUBLANES = 8
VMEM_LIMIT_BYTES = 56 * 1024 * 1024

CHUNK = 64
N_META = 16
META_PAD = (-N_META) % CHUNK
RMS_EPS = 1e-6
L2_EPS = 1e-6

GDN_HEADS = 8
GDN_DK = 128
GDN_DV = 128
GDN_CONV = 4
GDN_QK_DIM = GDN_HEADS * GDN_DK
GDN_V_DIM = GDN_HEADS * GDN_DV
GDN_CONV_DIM = 2 * GDN_QK_DIM + GDN_V_DIM
GDN_GATE_OFF = GDN_CONV_DIM + GDN_V_DIM
GDN_PROJ_W = GDN_GATE_OFF + LANES

ATT_HEADS = 8
ATT_KV_HEADS = 2
ATT_GROUP = ATT_HEADS // ATT_KV_HEADS
ATT_HD = 128
ATT_Q_DIM = ATT_HEADS * ATT_HD
ATT_KV_DIM = ATT_KV_HEADS * ATT_HD
IDX_HEADS = 8
IDX_HD = 64
TOPK_MAX = 256
REL_BUCKETS = 32
REL_MAX_DIST = 128

DSA_Q_OFF = 0
DSA_K_OFF = ATT_Q_DIM
DSA_V_OFF = DSA_K_OFF + ATT_KV_DIM
DSA_QI_OFF = DSA_V_OFF + ATT_KV_DIM
DSA_KI_OFF = DSA_QI_OFF + IDX_HEADS * LANES
DSA_WI_OFF = DSA_KI_OFF + LANES
DSA_PROJ_W = DSA_WI_OFF + LANES

N_GROUPS = 4
EXPERTS_PER_GROUP = 8
N_EXPERTS = N_GROUPS * EXPERTS_PER_GROUP
EXPERT_FF = 256

ATT_TILE = 128
KEY_TILE = 256
ROW_ALIGN = KEY_TILE
BISECT_UNROLL = 4
BISECT_STEPS = 24
SHIFT_SLACK = 1.001
ROW_SUM_FLOOR = 1e-30
NEG_INF = float("-inf")


def _pick_tile(n, candidates):
    for c in candidates:
        if n % c == 0:
            return c
    raise ValueError(f"no tile for {n}")


def _compiler_params(semantics):
    return pltpu.CompilerParams(dimension_semantics=semantics, vmem_limit_bytes=VMEM_LIMIT_BYTES)


def _silu(x):
    return x * jax.nn.sigmoid(x)


def _norm_matmul_kernel(x_ref, g_ref, w_ref, o_ref):
    x = x_ref[...]
    y = x * lax.rsqrt(jnp.mean(x * x, axis=-1, keepdims=True) + RMS_EPS) * g_ref[...]
    o_ref[...] = jnp.dot(y.astype(BF16), w_ref[...], preferred_element_type=F32)


def _norm_matmul(h2d, gain, w_bf16):
    T, D = h2d.shape
    N = w_bf16.shape[1]
    tm = _pick_tile(T, (256, 128))
    return pl.pallas_call(
        _norm_matmul_kernel,
        grid=(T // tm,),
        in_specs=[pl.BlockSpec((tm, D), lambda i: (i, 0)),
                  pl.BlockSpec((1, D), lambda i: (0, 0)),
                  pl.BlockSpec((D, N), lambda i: (0, 0))],
        out_specs=pl.BlockSpec((tm, N), lambda i: (i, 0)),
        out_shape=jax.ShapeDtypeStruct((T, N), F32),
        compiler_params=_compiler_params(("parallel",)),
        name="norm_matmul",
    )(h2d, gain.reshape(1, D), w_bf16)


def _matmul_residual_kernel(a_ref, w_ref, r_ref, o_ref):
    o_ref[...] = r_ref[...] + jnp.dot(a_ref[...], w_ref[...], preferred_element_type=F32)


def _matmul_residual(a_bf16, w_bf16, res):
    T, K = a_bf16.shape
    N = w_bf16.shape[1]
    tm = _pick_tile(T, (512, 256, 128))
    return pl.pallas_call(
        _matmul_residual_kernel,
        grid=(T // tm,),
        in_specs=[pl.BlockSpec((tm, K), lambda i: (i, 0)),
                  pl.BlockSpec((K, N), lambda i: (0, 0)),
                  pl.BlockSpec((tm, N), lambda i: (i, 0))],
        out_specs=pl.BlockSpec((tm, N), lambda i: (i, 0)),
        out_shape=jax.ShapeDtypeStruct((T, N), F32),
        compiler_params=_compiler_params(("parallel",)),
        name="matmul_residual",
    )(a_bf16, w_bf16, res)


def _nt_dot(a, b):
    return lax.dot_general(a, b, (((1,), (1,)), ((), ())), preferred_element_type=F32)


def _tn_dot(a, b):
    return lax.dot_general(a, b, (((0,), (0,)), ((), ())), preferred_element_type=F32)


def _gdn_kernel(proj_ref, conv_ref, gate_ref, ogain_ref, o_ref, state_ref, xbuf_ref, qkv_ref):
    c = pl.program_id(1)
    C = CHUNK

    @pl.when(c == 0)
    def _():
        state_ref[...] = jnp.zeros_like(state_ref)
        xbuf_ref[0:SUBLANES, :] = jnp.zeros((SUBLANES, GDN_CONV_DIM), F32)

    row = c * C + lax.broadcasted_iota(jnp.int32, (C, 1), 0)
    live = row >= META_PAD
    u = jnp.where(live, proj_ref[0, :, 0:GDN_CONV_DIM], 0.0)
    xbuf_ref[SUBLANES:SUBLANES + C, :] = u
    conv = conv_ref[0:1, :] * xbuf_ref[SUBLANES - 3:SUBLANES - 3 + C, :]
    for j in range(1, GDN_CONV):
        conv = conv + conv_ref[j:j + 1, :] * xbuf_ref[SUBLANES - 3 + j:SUBLANES - 3 + j + C, :]
    xbuf_ref[0:SUBLANES, :] = u[C - SUBLANES:C, :]
    qkv_ref[...] = _silu(conv)
    qkv = qkv_ref

    gates = jnp.where(live, proj_ref[0, :, GDN_GATE_OFF:GDN_GATE_OFF + LANES], 0.0)
    beta = jax.nn.sigmoid(gates)
    neg_rate = -jnp.exp(gate_ref[0:1, :])
    g = neg_rate * jax.nn.softplus(gates + gate_ref[1:2, :])
    r_i = lax.broadcasted_iota(jnp.int32, (C, C), 0)
    c_i = lax.broadcasted_iota(jnp.int32, (C, C), 1)
    causal = c_i <= r_i
    strict = c_i < r_i
    gc = jnp.dot(causal.astype(F32), g, preferred_element_type=F32,
                 precision=lax.Precision.HIGHEST)
    gc_t = gc.T
    eg = jnp.exp(gc)
    g_last = gc[C - 1:C, :]
    eg_last = jnp.exp(g_last)
    ek = jnp.exp(g_last - gc)

    heads = range(GDN_HEADS)
    dot = functools.partial(jnp.dot, preferred_element_type=F32)
    col = lambda a, h: a[:, GDN_HEADS + h:GDN_HEADS + h + 1]
    qs, ks, k16s, xs, decays = [], [], [], [], []
    for h in heads:
        lo = h * GDN_DK
        q = qkv[:, lo:lo + GDN_DK]
        k = qkv[:, GDN_QK_DIM + lo:GDN_QK_DIM + lo + GDN_DK]
        v = qkv[:, 2 * GDN_QK_DIM + h * GDN_DV:2 * GDN_QK_DIM + (h + 1) * GDN_DV]
        q = q * lax.rsqrt(jnp.sum(q * q, axis=-1, keepdims=True) + L2_EPS) * (GDN_DK ** -0.5)
        k = k * lax.rsqrt(jnp.sum(k * k, axis=-1, keepdims=True) + L2_EPS)
        kb = k * beta[:, h:h + 1]
        vb = v * beta[:, h:h + 1]
        qs.append(q)
        ks.append(k)
        k16s.append(k.astype(BF16))
        xs.append((kb, jnp.concatenate([vb, kb * col(eg, h)], axis=1)))
        decays.append(jnp.exp(jnp.where(causal, col(gc, h) - gc_t[GDN_HEADS + h:GDN_HEADS + h + 1, :],
                                        NEG_INF)))
    kk = [_nt_dot(xs[h][0].astype(BF16), k16s[h]) for h in heads]
    qk = [_nt_dot(qs[h].astype(BF16), k16s[h]) for h in heads]
    ns = [-jnp.where(strict, kk[h] * decays[h], 0.0) for h in heads]
    xs = [x for _, x in xs]
    for it in range(6):
        n16 = [n.astype(BF16) for n in ns]
        xs = [xs[h] + dot(n16[h], xs[h].astype(BF16)) for h in heads]
        if it < 5:
            ns = [dot(n16[h], n16[h]) for h in heads]
    qk = [jnp.where(causal, qk[h] * decays[h], 0.0).astype(BF16) for h in heads]
    states = [state_ref[h] for h in heads]
    s16 = [s.astype(BF16) for s in states]
    ws = [dot(xs[h][:, GDN_DV:2 * GDN_DV].astype(BF16), s16[h]) for h in heads]
    qs_s = [dot((qs[h] * col(eg, h)).astype(BF16), s16[h]) for h in heads]
    v_new = [(xs[h][:, 0:GDN_DV] - ws[h]).astype(BF16) for h in heads]
    os_ = [qs_s[h] + dot(qk[h], v_new[h]) for h in heads]
    kv = [_tn_dot((ks[h] * col(ek, h)).astype(BF16), v_new[h]) for h in heads]
    for h in heads:
        state_ref[h] = states[h] * col(eg_last, h) + kv[h]
        z = proj_ref[0, :, GDN_CONV_DIM + h * GDN_DV:GDN_CONV_DIM + (h + 1) * GDN_DV]
        o = os_[h]
        on = o * lax.rsqrt(jnp.mean(o * o, axis=-1, keepdims=True) + RMS_EPS) * ogain_ref[...]
        o_ref[0, :, h * GDN_DV:(h + 1) * GDN_DV] = (on * _silu(z)).astype(o_ref.dtype)


def _gdn_core(proj, conv_w, a_log, dt_bias, o_gain):
    B, LP, W = proj.shape
    nc = LP // CHUNK
    gate = jnp.zeros((2, LANES), F32)
    gate = gate.at[0, GDN_HEADS:2 * GDN_HEADS].set(a_log.astype(F32))
    gate = gate.at[1, GDN_HEADS:2 * GDN_HEADS].set(dt_bias.astype(F32))
    return pl.pallas_call(
        _gdn_kernel,
        grid=(B, nc),
        in_specs=[pl.BlockSpec((1, CHUNK, W), lambda b, c: (b, c, 0)),
                  pl.BlockSpec((GDN_CONV, GDN_CONV_DIM), lambda b, c: (0, 0)),
                  pl.BlockSpec((2, LANES), lambda b, c: (0, 0)),
                  pl.BlockSpec((1, GDN_DV), lambda b, c: (0, 0))],
        out_specs=pl.BlockSpec((1, CHUNK, GDN_V_DIM), lambda b, c: (b, c, 0)),
        out_shape=jax.ShapeDtypeStruct((B, LP, GDN_V_DIM), BF16),
        scratch_shapes=[pltpu.VMEM((GDN_HEADS, GDN_DK, GDN_DV), F32),
                        pltpu.VMEM((SUBLANES + CHUNK, GDN_CONV_DIM), F32),
                        pltpu.VMEM((CHUNK, GDN_CONV_DIM), F32)],
        compiler_params=_compiler_params(("parallel", "arbitrary")),
        name="gdn_core",
    )(proj, conv_w, gate, o_gain.reshape(1, GDN_DV))


def _moe_kernel(h_ref, gain_ref, wr_ref, br_ref, wgu_ref, wd_ref, o_ref, xn_ref, comb_ref, acc_ref):
    e = pl.program_id(1)
    lane = lax.broadcasted_iota(jnp.int32, (1, LANES), 1)

    @pl.when(e == 0)
    def _():
        x = h_ref[...]
        xn = x * lax.rsqrt(jnp.mean(x * x, axis=-1, keepdims=True) + RMS_EPS) * gain_ref[...]
        xn_ref[...] = xn.astype(BF16)
        logits = jnp.dot(xn, wr_ref[...], preferred_element_type=F32,
                         precision=lax.Precision.HIGHEST) + br_ref[...]
        e_log = logits[:, 0:LANES]
        g_log = jnp.where(lane < N_GROUPS, logits[:, LANES:2 * LANES], NEG_INF)
        g_max = jnp.max(g_log, axis=-1, keepdims=True)
        g_sel = jnp.min(jnp.where(g_log == g_max, lane, LANES), axis=-1, keepdims=True)
        g_w = 1.0 / jnp.sum(jnp.exp(g_log - g_max), axis=-1, keepdims=True)
        in_grp = (lane >= g_sel * EXPERTS_PER_GROUP) & (lane < (g_sel + 1) * EXPERTS_PER_GROUP)
        l0 = jnp.where(in_grp, e_log, NEG_INF)
        m1 = jnp.max(l0, axis=-1, keepdims=True)
        i1 = jnp.min(jnp.where(l0 == m1, lane, LANES), axis=-1, keepdims=True)
        l1 = jnp.where(lane == i1, NEG_INF, l0)
        m2 = jnp.max(l1, axis=-1, keepdims=True)
        i2 = jnp.min(jnp.where(l1 == m2, lane, LANES), axis=-1, keepdims=True)
        p2 = jnp.exp(m2 - m1)
        w1 = g_w / (1.0 + p2)
        w2 = g_w * p2 / (1.0 + p2)
        comb_ref[...] = jnp.where(lane == i1, w1, 0.0) + jnp.where(lane == i2, w2, 0.0)
        acc_ref[...] = jnp.zeros_like(acc_ref)

    xn = xn_ref[...]
    hh = jnp.dot(xn, wgu_ref[0], preferred_element_type=F32)
    cw = jnp.sum(jnp.where(lane == e, comb_ref[...], 0.0), axis=-1, keepdims=True)
    act = _silu(hh[:, 0:EXPERT_FF]) * hh[:, EXPERT_FF:2 * EXPERT_FF] * cw
    acc_ref[...] += jnp.dot(act.astype(BF16), wd_ref[0], preferred_element_type=F32)

    @pl.when(e == N_EXPERTS - 1)
    def _():
        o_ref[...] = h_ref[...] + acc_ref[...]


def _moe(h2d, gain, w_group, b_group, w_expert, b_expert, w_gate_up, w_down):
    T, D = h2d.shape
    tm = _pick_tile(T, (1408, 1280, 1024, 768, 640, 512, 256, 128))
    wr = jnp.zeros((D, 2 * LANES), F32)
    wr = wr.at[:, 0:N_EXPERTS].set(w_expert.astype(F32))
    wr = wr.at[:, LANES:LANES + N_GROUPS].set(w_group.astype(F32))
    br = jnp.zeros((1, 2 * LANES), F32)
    br = br.at[0, 0:N_EXPERTS].set(b_expert.astype(F32))
    br = br.at[0, LANES:LANES + N_GROUPS].set(b_group.astype(F32))
    wgu = w_gate_up.reshape(N_EXPERTS, D, 2 * EXPERT_FF).astype(BF16)
    wd = w_down.reshape(N_EXPERTS, EXPERT_FF, D).astype(BF16)
    return pl.pallas_call(
        _moe_kernel,
        grid=(T // tm, N_EXPERTS),
        in_specs=[pl.BlockSpec((tm, D), lambda i, e: (i, 0)),
                  pl.BlockSpec((1, D), lambda i, e: (0, 0)),
                  pl.BlockSpec((D, 2 * LANES), lambda i, e: (0, 0)),
                  pl.BlockSpec((1, 2 * LANES), lambda i, e: (0, 0)),
                  pl.BlockSpec((1, D, 2 * EXPERT_FF), lambda i, e: (e, 0, 0)),
                  pl.BlockSpec((1, EXPERT_FF, D), lambda i, e: (e, 0, 0))],
        out_specs=pl.BlockSpec((tm, D), lambda i, e: (i, 0)),
        out_shape=jax.ShapeDtypeStruct((T, D), F32),
        scratch_shapes=[pltpu.VMEM((tm, D), BF16),
                        pltpu.VMEM((tm, LANES), F32),
                        pltpu.VMEM((tm, D), F32)],
        compiler_params=_compiler_params(("parallel", "arbitrary")),
        name="moe",
    )(h2d, gain.reshape(1, D), wr, br, wgu, wd)


def _dsa_prep_kernel(p_ref, qg_ref, kg_ref, lng_ref, lnb_ref, q_ref, kt_ref, v_ref, qi_ref, kit_ref, wi_ref):
    lane = lax.broadcasted_iota(jnp.int32, (1, LANES), 1)
    for h in range(ATT_HEADS):
        x = p_ref[:, DSA_Q_OFF + h * ATT_HD:DSA_Q_OFF + (h + 1) * ATT_HD]
        y = x * lax.rsqrt(jnp.mean(x * x, axis=-1, keepdims=True) + RMS_EPS) * qg_ref[...]
        q_ref[:, h * ATT_HD:(h + 1) * ATT_HD] = (y * (ATT_HD ** -0.5)).astype(BF16)
    for n in range(ATT_KV_HEADS):
        x = p_ref[:, DSA_K_OFF + n * ATT_HD:DSA_K_OFF + (n + 1) * ATT_HD]
        y = x * lax.rsqrt(jnp.mean(x * x, axis=-1, keepdims=True) + RMS_EPS) * kg_ref[...]
        kt_ref[0, n * ATT_HD:(n + 1) * ATT_HD, :] = y.T.astype(BF16)
    v_ref[...] = p_ref[:, DSA_V_OFF:DSA_V_OFF + ATT_KV_DIM].astype(BF16)
    qi_ref[...] = p_ref[:, DSA_QI_OFF:DSA_QI_OFF + IDX_HEADS * LANES].astype(BF16)
    x = p_ref[:, DSA_KI_OFF:DSA_KI_OFF + LANES]
    live = lane < IDX_HD
    mu = jnp.sum(x, axis=-1, keepdims=True) * (1.0 / IDX_HD)
    xc = jnp.where(live, x - mu, 0.0)
    var = jnp.sum(xc * xc, axis=-1, keepdims=True) * (1.0 / IDX_HD)
    ki = xc * lax.rsqrt(var + RMS_EPS) * lng_ref[...] + lnb_ref[...]
    kit_ref[0] = jnp.where(live, ki, 0.0).T.astype(BF16)
    wi_ref[...] = p_ref[:, DSA_WI_OFF:DSA_WI_OFF + LANES] * (IDX_HEADS ** -0.5 * IDX_HD ** -0.5)


def _dsa_prep(proj2d, q_gain, k_gain, ln_g, ln_b):
    T, W = proj2d.shape
    tm = KEY_TILE
    assert T % tm == 0
    lng = jnp.zeros((1, LANES), F32).at[0, 0:IDX_HD].set(ln_g.astype(F32))
    lnb = jnp.zeros((1, LANES), F32).at[0, 0:IDX_HD].set(ln_b.astype(F32))
    row = lambda i: (i, 0)
    fix = lambda i: (0, 0)
    blk = lambda i: (i, 0, 0)
    return pl.pallas_call(
        _dsa_prep_kernel,
        grid=(T // tm,),
        in_specs=[pl.BlockSpec((tm, W), row),
                  pl.BlockSpec((1, ATT_HD), fix), pl.BlockSpec((1, ATT_HD), fix),
                  pl.BlockSpec((1, LANES), fix), pl.BlockSpec((1, LANES), fix)],
        out_specs=[pl.BlockSpec((tm, ATT_Q_DIM), row), pl.BlockSpec((1, ATT_KV_DIM, tm), blk),
                   pl.BlockSpec((tm, ATT_KV_DIM), row), pl.BlockSpec((tm, IDX_HEADS * LANES), row),
                   pl.BlockSpec((1, LANES, tm), blk), pl.BlockSpec((tm, LANES), row)],
        out_shape=[jax.ShapeDtypeStruct((T, ATT_Q_DIM), BF16), jax.ShapeDtypeStruct((T // tm, ATT_KV_DIM, tm), BF16),
                   jax.ShapeDtypeStruct((T, ATT_KV_DIM), BF16), jax.ShapeDtypeStruct((T, IDX_HEADS * LANES), BF16),
                   jax.ShapeDtypeStruct((T // tm, LANES, tm), BF16), jax.ShapeDtypeStruct((T, LANES), F32)],
        compiler_params=_compiler_params(("parallel",)),
        name="dsa_prep",
    )(proj2d, q_gain.reshape(1, ATT_HD), k_gain.reshape(1, ATT_HD), lng, lnb)


def _dsa_kernel(q_ref, qi_ref, wi_ref, kt_ref, v_ref, kit_ref, bias_ref, bmax_ref, o_ref,
                sc_ref, qg_ref, wib_ref, m_ref, l_ref, acc_ref, knorm_ref, *, n_select, live_rows):
    i = pl.program_id(1)
    TQ = ATT_TILE
    TK = KEY_TILE
    HALVES = TK // LANES

    @pl.when(i == 0)
    def _():
        def body(kt, mx):
            kk = kt_ref[0, kt].astype(F32)
            return tuple(jnp.maximum(mx[n], jnp.sum(jnp.square(kk[n * ATT_HD:(n + 1) * ATT_HD, :]),
                                                    axis=0, keepdims=True)) for n in range(ATT_KV_HEADS))

        mx = lax.fori_loop(0, kt_ref.shape[1], body,
                           tuple(jnp.zeros((1, TK), F32) for _ in range(ATT_KV_HEADS)))
        for n in range(ATT_KV_HEADS):
            knorm_ref[n] = jnp.broadcast_to(jnp.sqrt(jnp.max(mx[n], axis=-1, keepdims=True)), (1, LANES))

    @pl.when(i * TQ >= live_rows)
    def _():
        o_ref[...] = jnp.zeros_like(o_ref)

    @pl.when(i * TQ < live_rows)
    def _():
        _dsa_tile(i, q_ref, qi_ref, wi_ref, kt_ref, v_ref, kit_ref, bias_ref, bmax_ref, o_ref,
                  sc_ref, qg_ref, wib_ref, m_ref, l_ref, acc_ref, knorm_ref, n_select, TQ, TK, HALVES)


def _dsa_tile(i, q_ref, qi_ref, wi_ref, kt_ref, v_ref, kit_ref, bias_ref, bmax_ref, o_ref,
              sc_ref, qg_ref, wib_ref, m_ref, l_ref, acc_ref, knorm_ref, n_select, TQ, TK, HALVES):
    nkt = lax.shift_right_logical(i, 1) + 1
    fmin = float(jnp.finfo(F32).min)
    kf = float(n_select)

    q_row = i * TQ + lax.broadcasted_iota(jnp.int32, (TQ, 1), 0)
    q_chunk = lax.shift_right_logical(q_row, int(math.log2(CHUNK)))
    k_lane = lax.broadcasted_iota(jnp.int32, (1, TK), 1)

    def halves(x):
        return [x[:, j * LANES:(j + 1) * LANES] for j in range(HALVES)]

    def lane_wide(col):
        return jnp.broadcast_to(col, (TQ, LANES))

    wi = wi_ref[0]
    for h in range(IDX_HEADS):
        wib_ref[h] = jnp.broadcast_to(wi[:, h:h + 1], (TQ, TK))

    def score_tiles(kts):
        d = [[jnp.dot(qi_ref[0, :, h * LANES:(h + 1) * LANES], kit_ref[0, kt], preferred_element_type=F32)
              for h in range(IDX_HEADS)] for kt in kts]
        for t, kt in enumerate(kts):
            acc = wib_ref[0] * jnp.maximum(d[t][0], 0.0)
            for h in range(1, IDX_HEADS):
                acc = acc + wib_ref[h] * jnp.maximum(d[t][h], 0.0)
            sc_ref[kt] = acc

    def paired_loop(n, fn):
        def body(j, carry):
            fn([2 * j, 2 * j + 1])
            return carry

        lax.fori_loop(0, lax.shift_right_arithmetic(n, 1), body, 0)

        @pl.when(lax.bitwise_and(n, 1) == 1)
        def _():
            fn([n - 1])

    paired_loop(nkt, score_tiles)

    def mask_inadmissible(kt):
        k_pos = kt * TK + k_lane
        adm = (lax.shift_right_logical(k_pos, int(math.log2(CHUNK))) <= q_chunk) & (k_pos >= META_PAD)
        sc_ref[kt] = jnp.where(adm, sc_ref[kt], NEG_INF)

    mask_inadmissible(0)

    @pl.when(nkt > 1)
    def _():
        mask_inadmissible(nkt - 1)

    n_adm = CHUNK * (q_chunk + 1) - META_PAD
    need = n_adm > n_select

    def row_sum(x):
        return jnp.sum(x, axis=-1, keepdims=True)

    def count_ge(t):
        tb = lane_wide(t)

        def body(kt, acc):
            for s in halves(sc_ref[kt]):
                acc = acc + jnp.where(s >= tb, 1.0, 0.0)
            return acc

        return row_sum(lax.fori_loop(0, nkt, body, jnp.zeros((TQ, LANES), F32)))

    def minmax_body(kt, carry):
        mn, mx = carry
        for s in halves(sc_ref[kt]):
            mn = jnp.minimum(mn, jnp.where(s == NEG_INF, jnp.inf, s))
            mx = jnp.maximum(mx, s)
        return mn, mx

    mn_l, mx_l = lax.fori_loop(0, nkt, minmax_body,
                               (jnp.full((TQ, LANES), jnp.inf, F32), jnp.full((TQ, LANES), NEG_INF, F32)))
    s_min = jnp.min(mn_l, axis=-1, keepdims=True)
    s_max = jnp.max(mx_l, axis=-1, keepdims=True)
    lo0 = s_min
    hi0 = s_max + (s_max - s_min) + 1.0
    cnt0 = n_adm.astype(F32)
    active0 = jnp.where(need, 1.0, 0.0)

    def any_active(active):
        return jnp.max(active) > 0.0

    def bis_step(lo, hi, cnt, active):
        mid = lo + (hi - lo) * 0.5
        c = count_ge(mid)
        ge = c >= kf
        moving = (mid > lo) & (mid < hi) & (active > 0.0)
        lo_n = jnp.where(moving & ge, mid, lo)
        hi_n = jnp.where(moving & (~ge), mid, hi)
        cnt_n = jnp.where(moving & ge, c, cnt)
        active_n = jnp.where(moving & (cnt_n != kf), 1.0, 0.0)
        return lo_n, hi_n, cnt_n, active_n

    def bis_body(st):
        it, rest = st[0], st[1:]
        for _ in range(BISECT_UNROLL):
            rest = bis_step(*rest)
        return (it + BISECT_UNROLL,) + rest

    _, lo, hi, cnt, _ = lax.while_loop(
        lambda st: jnp.logical_and(st[0] < BISECT_STEPS, any_active(st[4])), bis_body,
        (jnp.int32(0), lo0, hi0, cnt0, active0))

    def snap_body(st):
        lo, hi, cnt, active = st
        hb = lane_wide(hi)

        def body(kt, mx):
            for s in halves(sc_ref[kt]):
                mx = jnp.maximum(mx, jnp.where(s < hb, s, NEG_INF))
            return mx

        v1 = jnp.max(lax.fori_loop(0, nkt, body, jnp.full((TQ, LANES), NEG_INF, F32)),
                     axis=-1, keepdims=True)
        c1 = count_ge(v1)
        on = active > 0.0
        hit = on & (c1 >= kf)
        lo_n = jnp.where(hit, v1, lo)
        cnt_n = jnp.where(hit, c1, cnt)
        hi_n = jnp.where(on & (~hit), v1, hi)
        return lo_n, hi_n, cnt_n, jnp.where(on & (~hit), 1.0, 0.0)

    unsettled = jnp.where(need & (cnt != kf), 1.0, 0.0)
    lo, hi, cnt, _ = lax.while_loop(lambda st: any_active(st[3]), snap_body, (lo, hi, cnt, unsettled))

    thr = jnp.where(need, lo, fmin)
    tied = need & (cnt > kf)
    big_pos = jnp.int32(2 ** 30)

    def tie_break():
        tb = lane_wide(thr)

        def count_pred(pred):
            def body(kt, acc):
                pos = kt * TK + k_lane
                for j, s in enumerate(halves(sc_ref[kt])):
                    acc = acc + jnp.where(pred(s, pos[:, j * LANES:(j + 1) * LANES]), 1.0, 0.0)
                return acc
            return row_sum(lax.fori_loop(0, nkt, body, jnp.zeros((TQ, LANES), F32)))

        want = kf - count_pred(lambda s, pos: s > tb)

        def body(_, st):
            jl, jh = st
            jm = lax.shift_right_arithmetic(jl + jh, 1)
            c = count_pred(lambda s, pos: (s == tb) & (pos <= jm))
            ok = c >= want
            return jnp.where(ok, jl, jm), jnp.where(ok, jm, jh)

        n_keys = sc_ref.shape[0] * TK
        steps = int(math.ceil(math.log2(n_keys + 1))) + 1
        jl0 = jnp.full((TQ, 1), -1, jnp.int32)
        jh0 = jnp.full((TQ, 1), n_keys, jnp.int32)
        _, jh = lax.fori_loop(0, steps, body, (jl0, jh0))
        return jnp.where(tied, jh, big_pos)

    pos_cap = lax.cond(jnp.max(jnp.where(tied, 1.0, 0.0)) > 0.0, tie_break,
                       lambda: jnp.full((TQ, 1), big_pos, jnp.int32))

    def to_mask(kt, carry):
        s = sc_ref[kt]
        pos = kt * TK + k_lane
        sel = (s > thr) | ((s == thr) & (pos <= pos_cap))
        sc_ref[kt] = jnp.where(sel, 0.0, NEG_INF)
        return carry

    lax.fori_loop(0, nkt, to_mask, 0)

    for n in range(ATT_KV_HEADS):
        for g in range(ATT_GROUP):
            h = n * ATT_GROUP + g
            qg_ref[n, g * TQ:(g + 1) * TQ, :] = q_ref[0, :, h * ATT_HD:(h + 1) * ATT_HD]
    l_ref[...] = jnp.zeros_like(l_ref)
    acc_ref[...] = jnp.zeros_like(acc_ref)

    kv_heads = range(ATT_KV_HEADS)

    def raw_logits(kts):
        return [[jnp.dot(qg_ref[n], kt_ref[0, kt, n * ATT_HD:(n + 1) * ATT_HD, :], preferred_element_type=F32)
                 for n in kv_heads] for kt in kts]

    def masked_logits(s, mask, n, g, b0):
        sg = s[g * TQ:(g + 1) * TQ, :] + mask
        if b0 is not None:
            h = n * ATT_GROUP + g
            sg = sg + jnp.concatenate([bias_ref[h, b0 + j] for j in range(HALVES)], axis=1)
        return sg

    def max_pass(kts, b0):
        raw = raw_logits(kts)
        masks = [sc_ref[kt] for kt in kts]
        for n in kv_heads:
            for g in range(ATT_GROUP):
                rows = slice(g * TQ, (g + 1) * TQ)
                m = m_ref[n, rows, :]
                for t in range(len(kts)):
                    for part in halves(masked_logits(raw[t][n], masks[t], n, g, b0)):
                        m = jnp.maximum(m, part)
                m_ref[n, rows, :] = m

    def exp_pass(kts, b0):
        raw = raw_logits(kts)
        masks = [sc_ref[kt] for kt in kts]
        for n in kv_heads:
            ps = [[] for _ in kts]
            for g in range(ATT_GROUP):
                rows = slice(g * TQ, (g + 1) * TQ)
                m = m_ref[n, rows, :]
                l_new = l_ref[n, rows, :]
                for t in range(len(kts)):
                    parts = [jnp.exp(part - m) for part in halves(masked_logits(raw[t][n], masks[t], n, g, b0))]
                    l_new = l_new + functools.reduce(lambda a, b: a + b, parts)
                    ps[t].append(jnp.concatenate(parts, axis=1).astype(BF16))
                l_ref[n, rows, :] = l_new
            pv = None
            for t, kt in enumerate(kts):
                vv = v_ref[0, pl.ds(pl.multiple_of(kt * TK, TK), TK), n * ATT_HD:(n + 1) * ATT_HD]
                d = jnp.dot(jnp.concatenate(ps[t], axis=0), vv, preferred_element_type=F32)
                pv = d if pv is None else pv + d
            acc_ref[n] += pv

    n_far = lax.shift_right_arithmetic(i - 1, 1)
    parity = lax.bitwise_and(i, 1)

    def run_pass(tile_fn):
        paired_loop(jnp.maximum(n_far, 0), lambda kts: tile_fn(kts, None))

        @pl.when(i >= 1)
        def _():
            tile_fn([n_far], parity)

        @pl.when(parity == 0)
        def _():
            tile_fn([lax.shift_right_logical(i, 1)], 2)

    for n in range(ATT_KV_HEADS):
        for g in range(ATT_GROUP):
            h = n * ATT_GROUP + g
            qh = qg_ref[n, g * TQ:(g + 1) * TQ, :].astype(F32)
            q_len = jnp.sqrt(jnp.sum(qh * qh, axis=-1, keepdims=True))
            bound = q_len * knorm_ref[n] * SHIFT_SLACK + (bmax_ref[h] + SHIFT_SLACK - 1.0)
            m_ref[n, g * TQ:(g + 1) * TQ, :] = jnp.broadcast_to(bound, (TQ, LANES))
    run_pass(exp_pass)

    def smallest_row_sum():
        return functools.reduce(jnp.minimum, [jnp.min(jnp.sum(l_ref[n], axis=-1, keepdims=True))
                                              for n in range(ATT_KV_HEADS)])

    @pl.when(jnp.logical_not(smallest_row_sum() > ROW_SUM_FLOOR))
    def _():
        m_ref[...] = jnp.full(m_ref.shape, -1e30, F32)
        run_pass(max_pass)
        for n in range(ATT_KV_HEADS):
            m_ref[n] = jnp.broadcast_to(jnp.max(m_ref[n], axis=-1, keepdims=True), m_ref.shape[1:])
        l_ref[...] = jnp.zeros_like(l_ref)
        acc_ref[...] = jnp.zeros_like(acc_ref)
        run_pass(exp_pass)

    for n in range(ATT_KV_HEADS):
        inv_l = 1.0 / jnp.sum(l_ref[n], axis=-1, keepdims=True)
        for g in range(ATT_GROUP):
            h = n * ATT_GROUP + g
            rows = slice(g * TQ, (g + 1) * TQ)
            o_ref[0, :, h * ATT_HD:(h + 1) * ATT_HD] = (acc_ref[n, rows, :] * inv_l[rows, :]).astype(o_ref.dtype)


def _t5_bucket(rel):
    nb = REL_BUCKETS // 2
    max_exact = nb // 2
    ret = jnp.where(rel > 0, nb, 0)
    n = jnp.abs(rel)
    nf = jnp.maximum(n, 1).astype(jnp.float32)
    large = max_exact + (jnp.log(nf / max_exact) / math.log(REL_MAX_DIST / max_exact)
                         * (nb - max_exact)).astype(jnp.int32)
    large = jnp.minimum(large, nb - 1)
    return ret + jnp.where(n < max_exact, n, large)


def _dsa_core(q, kt, v, qi, kit, wi, rel_bias, n_select, live_rows):
    B, LP, _ = q.shape
    nq = LP // ATT_TILE
    nkt = LP // KEY_TILE
    a = jnp.arange(ATT_TILE)[:, None]
    xx = jnp.arange(2 * KEY_TILE)[None, :]
    rel = (xx - KEY_TILE) - a
    far = rel_bias.astype(F32)[REL_BUCKETS // 2 - 1]
    onehot = jax.nn.one_hot(_t5_bucket(rel), REL_BUCKETS, dtype=F32)
    table = jnp.einsum("abk,kh->abh", onehot, rel_bias.astype(F32),
                       precision=lax.Precision.HIGHEST) - far
    bias_near = jnp.transpose(table.reshape(ATT_TILE, 2 * KEY_TILE // LANES, LANES, ATT_HEADS), (3, 1, 0, 2))
    bias_max = jnp.broadcast_to(jnp.maximum(jnp.max(table, axis=(0, 1)), 0.0)[:, None, None],
                                (ATT_HEADS, 1, LANES))
    kern = functools.partial(_dsa_kernel, n_select=n_select, live_rows=live_rows)
    tile = lambda b, i: (b, i, 0)
    whole3 = lambda b, i: (b, 0, 0)
    whole4 = lambda b, i: (b, 0, 0, 0)
    return pl.pallas_call(
        kern,
        grid=(B, nq),
        in_specs=[pl.BlockSpec((1, ATT_TILE, ATT_Q_DIM), tile),
                  pl.BlockSpec((1, ATT_TILE, IDX_HEADS * LANES), tile),
                  pl.BlockSpec((1, ATT_TILE, LANES), tile),
                  pl.BlockSpec((1, nkt, ATT_KV_DIM, KEY_TILE), whole4),
                  pl.BlockSpec((1, LP, ATT_KV_DIM), whole3),
                  pl.BlockSpec((1, nkt, LANES, KEY_TILE), whole4),
                  pl.BlockSpec((ATT_HEADS, 2 * KEY_TILE // LANES, ATT_TILE, LANES), lambda b, i: (0, 0, 0, 0)),
                  pl.BlockSpec((ATT_HEADS, 1, LANES), lambda b, i: (0, 0, 0))],
        out_specs=pl.BlockSpec((1, ATT_TILE, ATT_Q_DIM), tile),
        out_shape=jax.ShapeDtypeStruct((B, LP, ATT_Q_DIM), BF16),
        scratch_shapes=[pltpu.VMEM((nkt, ATT_TILE, KEY_TILE), F32),
                        pltpu.VMEM((ATT_KV_HEADS, ATT_GROUP * ATT_TILE, ATT_HD), BF16),
                        pltpu.VMEM((IDX_HEADS, ATT_TILE, KEY_TILE), F32),
                        pltpu.VMEM((ATT_KV_HEADS, ATT_GROUP * ATT_TILE, LANES), F32),
                        pltpu.VMEM((ATT_KV_HEADS, ATT_GROUP * ATT_TILE, LANES), F32),
                        pltpu.VMEM((ATT_KV_HEADS, ATT_GROUP * ATT_TILE, ATT_HD), F32),
                        pltpu.VMEM((ATT_KV_HEADS, 1, LANES), F32)],
        compiler_params=_compiler_params(("arbitrary", "arbitrary")),
        name="dsa_core",
    )(q, qi, wi, kt, v, kit, bias_near, bias_max)


def _gdn_in_weight(w_in):
    D = w_in.shape[0]
    w = jnp.zeros((D, GDN_PROJ_W), F32).at[:, 0:w_in.shape[1]].set(w_in.astype(F32))
    return w.astype(BF16)


def _dsa_in_weight(w_in):
    D = w_in.shape[0]
    w = jnp.zeros((D, DSA_PROJ_W), F32)
    w = w.at[:, 0:DSA_QI_OFF].set(w_in[:, 0:DSA_QI_OFF].astype(F32))
    src = ATT_Q_DIM + 2 * ATT_KV_DIM
    for h in range(IDX_HEADS):
        w = w.at[:, DSA_QI_OFF + h * LANES:DSA_QI_OFF + h * LANES + IDX_HD].set(
            w_in[:, src + h * IDX_HD:src + (h + 1) * IDX_HD].astype(F32))
    src += IDX_HEADS * IDX_HD
    w = w.at[:, DSA_KI_OFF:DSA_KI_OFF + IDX_HD].set(w_in[:, src:src + IDX_HD].astype(F32))
    src += IDX_HD
    w = w.at[:, DSA_WI_OFF:DSA_WI_OFF + IDX_HEADS].set(w_in[:, src:src + IDX_HEADS].astype(F32))
    return w.astype(BF16)


def kernel(x, meta_tokens, norm_mix, norm_ffn, rel_bias, gdn_w_in, gdn_conv, gdn_a_log, gdn_dt_bias,
           gdn_o_norm, gdn_w_out, dsa_w_in, dsa_q_norm, dsa_k_norm, dsa_idx_ln_g, dsa_idx_ln_b, dsa_w_out,
           moe_w_group, moe_b_group, moe_w_expert, moe_b_expert, moe_w_gate_up, moe_w_down):
    B, S, D = x.shape
    depth = norm_mix.shape[0]
    n_select = min(TOPK_MAX, S // 4)
    off = META_PAD + N_META
    LP = -(-(off + S) // ROW_ALIGN) * ROW_ALIGN
    meta = jnp.broadcast_to(meta_tokens.astype(x.dtype)[None], (B, N_META, D))
    h = jnp.concatenate([jnp.zeros((B, META_PAD, D), x.dtype), meta, x,
                         jnp.zeros((B, LP - off - S, D), x.dtype)], axis=1)
    h = h.reshape(B * LP, D)
    for i in range(depth):
        j = i // 2
        if i % 2 == 0:
            proj = _norm_matmul(h, norm_mix[i], _gdn_in_weight(gdn_w_in[j]))
            o = _gdn_core(proj.reshape(B, LP, GDN_PROJ_W), gdn_conv[j].astype(F32), gdn_a_log[j],
                          gdn_dt_bias[j], gdn_o_norm[j])
            h = _matmul_residual(o.reshape(B * LP, GDN_V_DIM), gdn_w_out[j].astype(BF16), h)
        else:
            proj = _norm_matmul(h, norm_mix[i], _dsa_in_weight(dsa_w_in[j]))
            q, kt, v, qi, kit, wi = _dsa_prep(proj, dsa_q_norm[j], dsa_k_norm[j], dsa_idx_ln_g[j], dsa_idx_ln_b[j])
            r3 = lambda t: t.reshape(B, LP, t.shape[-1])
            r4 = lambda t: t.reshape(B, LP // KEY_TILE, t.shape[-2], KEY_TILE)
            o = _dsa_core(r3(q), r4(kt), r3(v), r3(qi), r4(kit), r3(wi), rel_bias, n_select, off + S)
            h = _matmul_residual(o.reshape(B * LP, ATT_Q_DIM), dsa_w_out[j].astype(BF16), h)
        h = _moe(h, norm_ffn[i], moe_w_group[i], moe_b_group[i], moe_w_expert[i], moe_b_expert[i],
                 moe_w_gate_up[i], moe_w_down[i])
    return h.reshape(B, LP, D)[:, off:off + S]
```

```python
import functools
import math

import jax
import jax.numpy as jnp
from jax import lax
from jax.experimental import pallas as pl
from jax.experimental.pallas import tpu as pltpu

F32 = jnp.float32
BF16 = jnp.bfloat16

LANES = 128
SUBLANES = 8
VMEM_LIMIT_BYTES = 56 * 1024 * 1024

CHUNK = 64
N_META = 16
META_PAD = (-N_META) % CHUNK
RMS_EPS = 1e-6
L2_EPS = 1e-6

GDN_HEADS = 8
GDN_DK = 128
GDN_DV = 128
GDN_CONV = 4
GDN_QK_DIM = GDN_HEADS * GDN_DK
GDN_V_DIM = GDN_HEADS * GDN_DV
GDN_CONV_DIM = 2 * GDN_QK_DIM + GDN_V_DIM
GDN_GATE_OFF = GDN_CONV_DIM + GDN_V_DIM
GDN_PROJ_W = GDN_GATE_OFF + LANES

ATT_HEADS = 8
ATT_KV_HEADS = 2
ATT_GROUP = ATT_HEADS // ATT_KV_HEADS
ATT_HD = 128
ATT_Q_DIM = ATT_HEADS * ATT_HD
ATT_KV_DIM = ATT_KV_HEADS * ATT_HD
IDX_HEADS = 8
IDX_HD = 64
TOPK_MAX = 256
REL_BUCKETS = 32
REL_MAX_DIST = 128

DSA_Q_OFF = 0
DSA_K_OFF = ATT_Q_DIM
DSA_V_OFF = DSA_K_OFF + ATT_KV_DIM
DSA_QI_OFF = DSA_V_OFF + ATT_KV_DIM
DSA_KI_OFF = DSA_QI_OFF + IDX_HEADS * LANES
DSA_WI_OFF = DSA_KI_OFF + LANES
DSA_PROJ_W = DSA_WI_OFF + LANES

N_GROUPS = 4
EXPERTS_PER_GROUP = 8
N_EXPERTS = N_GROUPS * EXPERTS_PER_GROUP
EXPERT_FF = 256

MOE_ROUTER_TILES = (512, 256, 128)
MOE_ROW_TILES = (512, 256, 128)

ATT_TILE = 128
KEY_TILE = 256
ROW_ALIGN = KEY_TILE
BISECT_UNROLL = 4
BISECT_STEPS = 24
SHIFT_SLACK = 1.001
ROW_SUM_FLOOR = 1e-30
NEG_INF = float("-inf")


def _pick_tile(n, candidates):
    for c in candidates:
        if n % c == 0:
            return c
    raise ValueError(f"no tile for {n}")


def _compiler_params(semantics):
    return pltpu.CompilerParams(dimension_semantics=semantics, vmem_limit_bytes=VMEM_LIMIT_BYTES)


def _silu(x):
    return x * jax.nn.sigmoid(x)


def _norm_matmul_kernel(x_ref, g_ref, w_ref, o_ref):
    x = x_ref[...]
    y = x * lax.rsqrt(jnp.mean(x * x, axis=-1, keepdims=True) + RMS_EPS) * g_ref[...]
    o_ref[...] = jnp.dot(y.astype(BF16), w_ref[...], preferred_element_type=F32)


def _norm_matmul(h2d, gain, w_bf16):
    T, D = h2d.shape
    N = w_bf16.shape[1]
    tm = _pick_tile(T, (256, 128))
    return pl.pallas_call(
        _norm_matmul_kernel,
        grid=(T // tm,),
        in_specs=[pl.BlockSpec((tm, D), lambda i: (i, 0)),
                  pl.BlockSpec((1, D), lambda i: (0, 0)),
                  pl.BlockSpec((D, N), lambda i: (0, 0))],
        out_specs=pl.BlockSpec((tm, N), lambda i: (i, 0)),
        out_shape=jax.ShapeDtypeStruct((T, N), F32),
        compiler_params=_compiler_params(("parallel",)),
        name="norm_matmul",
    )(h2d, gain.reshape(1, D), w_bf16)


def _matmul_residual_kernel(a_ref, w_ref, r_ref, o_ref):
    o_ref[...] = r_ref[...] + jnp.dot(a_ref[...], w_ref[...], preferred_element_type=F32)


def _matmul_residual(a_bf16, w_bf16, res):
    T, K = a_bf16.shape
    N = w_bf16.shape[1]
    tm = _pick_tile(T, (512, 256, 128))
    return pl.pallas_call(
        _matmul_residual_kernel,
        grid=(T // tm,),
        in_specs=[pl.BlockSpec((tm, K), lambda i: (i, 0)),
                  pl.BlockSpec((K, N), lambda i: (0, 0)),
                  pl.BlockSpec((tm, N), lambda i: (i, 0))],
        out_specs=pl.BlockSpec((tm, N), lambda i: (i, 0)),
        out_shape=jax.ShapeDtypeStruct((T, N), F32),
        compiler_params=_compiler_params(("parallel",)),
        name="matmul_residual",
    )(a_bf16, w_bf16, res)


def _nt_dot(a, b):
    return lax.dot_general(a, b, (((1,), (1,)), ((), ())), preferred_element_type=F32)


def _tn_dot(a, b):
    return lax.dot_general(a, b, (((0,), (0,)), ((), ())), preferred_element_type=F32)


def _gdn_kernel(proj_ref, conv_ref, gate_ref, ogain_ref, o_ref, state_ref, xbuf_ref, qkv_ref):
    c = pl.program_id(1)
    C = CHUNK

    @pl.when(c == 0)
    def _():
        state_ref[...] = jnp.zeros_like(state_ref)
        xbuf_ref[0:SUBLANES, :] = jnp.zeros((SUBLANES, GDN_CONV_DIM), F32)

    row = c * C + lax.broadcasted_iota(jnp.int32, (C, 1), 0)
    live = row >= META_PAD
    u = jnp.where(live, proj_ref[0, :, 0:GDN_CONV_DIM], 0.0)
    xbuf_ref[SUBLANES:SUBLANES + C, :] = u
    conv = conv_ref[0:1, :] * xbuf_ref[SUBLANES - 3:SUBLANES - 3 + C, :]
    for j in range(1, GDN_CONV):
        conv = conv + conv_ref[j:j + 1, :] * xbuf_ref[SUBLANES - 3 + j:SUBLANES - 3 + j + C, :]
    xbuf_ref[0:SUBLANES, :] = u[C - SUBLANES:C, :]
    qkv_ref[...] = _silu(conv)
    qkv = qkv_ref

    gates = jnp.where(live, proj_ref[0, :, GDN_GATE_OFF:GDN_GATE_OFF + LANES], 0.0)
    beta = jax.nn.sigmoid(gates)
    neg_rate = -jnp.exp(gate_ref[0:1, :])
    g = neg_rate * jax.nn.softplus(gates + gate_ref[1:2, :])
    r_i = lax.broadcasted_iota(jnp.int32, (C, C), 0)
    c_i = lax.broadcasted_iota(jnp.int32, (C, C), 1)
    causal = c_i <= r_i
    strict = c_i < r_i
    gc = jnp.dot(causal.astype(F32), g, preferred_element_type=F32,
                 precision=lax.Precision.HIGHEST)
    gc_t = gc.T
    eg = jnp.exp(gc)
    g_last = gc[C - 1:C, :]
    eg_last = jnp.exp(g_last)
    ek = jnp.exp(g_last - gc)

    heads = range(GDN_HEADS)
    dot = functools.partial(jnp.dot, preferred_element_type=F32)
    col = lambda a, h: a[:, GDN_HEADS + h:GDN_HEADS + h + 1]
    qs, ks, k16s, xs, decays = [], [], [], [], []
    for h in heads:
        lo = h * GDN_DK
        q = qkv[:, lo:lo + GDN_DK]
        k = qkv[:, GDN_QK_DIM + lo:GDN_QK_DIM + lo + GDN_DK]
        v = qkv[:, 2 * GDN_QK_DIM + h * GDN_DV:2 * GDN_QK_DIM + (h + 1) * GDN_DV]
        q = q * lax.rsqrt(jnp.sum(q * q, axis=-1, keepdims=True) + L2_EPS) * (GDN_DK ** -0.5)
        k = k * lax.rsqrt(jnp.sum(k * k, axis=-1, keepdims=True) + L2_EPS)
        kb = k * beta[:, h:h + 1]
        vb = v * beta[:, h:h + 1]
        qs.append(q)
        ks.append(k)
        k16s.append(k.astype(BF16))
        xs.append((kb, jnp.concatenate([vb, kb * col(eg, h)], axis=1)))
        decays.append(jnp.exp(jnp.where(causal, col(gc, h) - gc_t[GDN_HEADS + h:GDN_HEADS + h + 1, :],
                                        NEG_INF)))
    kk = [_nt_dot(xs[h][0].astype(BF16), k16s[h]) for h in heads]
    qk = [_nt_dot(qs[h].astype(BF16), k16s[h]) for h in heads]
    ns = [-jnp.where(strict, kk[h] * decays[h], 0.0) for h in heads]
    xs = [x for _, x in xs]
    for it in range(6):
        n16 = [n.astype(BF16) for n in ns]
        xs = [xs[h] + dot(n16[h], xs[h].astype(BF16)) for h in heads]
        if it < 5:
            ns = [dot(n16[h], n16[h]) for h in heads]
    qk = [jnp.where(causal, qk[h] * decays[h], 0.0).astype(BF16) for h in heads]
    states = [state_ref[h] for h in heads]
    s16 = [s.astype(BF16) for s in states]
    ws = [dot(xs[h][:, GDN_DV:2 * GDN_DV].astype(BF16), s16[h]) for h in heads]
    qs_s = [dot((qs[h] * col(eg, h)).astype(BF16), s16[h]) for h in heads]
    v_new = [(xs[h][:, 0:GDN_DV] - ws[h]).astype(BF16) for h in heads]
    os_ = [qs_s[h] + dot(qk[h], v_new[h]) for h in heads]
    kv = [_tn_dot((ks[h] * col(ek, h)).astype(BF16), v_new[h]) for h in heads]
    for h in heads:
        state_ref[h] = states[h] * col(eg_last, h) + kv[h]
        z = proj_ref[0, :, GDN_CONV_DIM + h * GDN_DV:GDN_CONV_DIM + (h + 1) * GDN_DV]
        o = os_[h]
        on = o * lax.rsqrt(jnp.mean(o * o, axis=-1, keepdims=True) + RMS_EPS) * ogain_ref[...]
        o_ref[0, :, h * GDN_DV:(h + 1) * GDN_DV] = (on * _silu(z)).astype(o_ref.dtype)


def _gdn_core(proj, conv_w, a_log, dt_bias, o_gain):
    B, LP, W = proj.shape
    nc = LP // CHUNK
    gate = jnp.zeros((2, LANES), F32)
    gate = gate.at[0, GDN_HEADS:2 * GDN_HEADS].set(a_log.astype(F32))
    gate = gate.at[1, GDN_HEADS:2 * GDN_HEADS].set(dt_bias.astype(F32))
    return pl.pallas_call(
        _gdn_kernel,
        grid=(B, nc),
        in_specs=[pl.BlockSpec((1, CHUNK, W), lambda b, c: (b, c, 0)),
                  pl.BlockSpec((GDN_CONV, GDN_CONV_DIM), lambda b, c: (0, 0)),
                  pl.BlockSpec((2, LANES), lambda b, c: (0, 0)),
                  pl.BlockSpec((1, GDN_DV), lambda b, c: (0, 0))],
        out_specs=pl.BlockSpec((1, CHUNK, GDN_V_DIM), lambda b, c: (b, c, 0)),
        out_shape=jax.ShapeDtypeStruct((B, LP, GDN_V_DIM), BF16),
        scratch_shapes=[pltpu.VMEM((GDN_HEADS, GDN_DK, GDN_DV), F32),
                        pltpu.VMEM((SUBLANES + CHUNK, GDN_CONV_DIM), F32),
                        pltpu.VMEM((CHUNK, GDN_CONV_DIM), F32)],
        compiler_params=_compiler_params(("parallel", "arbitrary")),
        name="gdn_core",
    )(proj, conv_w, gate, o_gain.reshape(1, GDN_DV))


def _moe_router_kernel(h_ref, gain_ref, wr_ref, br_ref, hx_ref, gid_ref, rank_ref, cnt_ref, carry_ref):
    i = pl.program_id(0)
    TM = h_ref.shape[0]
    lane = lax.broadcasted_iota(jnp.int32, (1, LANES), 1)

    @pl.when(i == 0)
    def _():
        carry_ref[...] = jnp.zeros_like(carry_ref)

    x = h_ref[...]
    xn = x * lax.rsqrt(jnp.mean(x * x, axis=-1, keepdims=True) + RMS_EPS) * gain_ref[...]
    logits = jnp.dot(xn, wr_ref[...], preferred_element_type=F32,
                     precision=lax.Precision.HIGHEST) + br_ref[...]
    e_log = logits[:, 0:LANES]
    g_log = jnp.where(lane < N_GROUPS, logits[:, LANES:2 * LANES], NEG_INF)
    g_max = jnp.max(g_log, axis=-1, keepdims=True)
    g_sel = jnp.min(jnp.where(g_log == g_max, lane, LANES), axis=-1, keepdims=True)
    g_w = 1.0 / jnp.sum(jnp.exp(g_log - g_max), axis=-1, keepdims=True)
    in_grp = (lane >= g_sel * EXPERTS_PER_GROUP) & (lane < (g_sel + 1) * EXPERTS_PER_GROUP)
    l0 = jnp.where(in_grp, e_log, NEG_INF)
    m1 = jnp.max(l0, axis=-1, keepdims=True)
    i1 = jnp.min(jnp.where(l0 == m1, lane, LANES), axis=-1, keepdims=True)
    l1 = jnp.where(lane == i1, NEG_INF, l0)
    m2 = jnp.max(l1, axis=-1, keepdims=True)
    i2 = jnp.min(jnp.where(l1 == m2, lane, LANES), axis=-1, keepdims=True)
    p2 = jnp.exp(m2 - m1)
    w1 = g_w / (1.0 + p2)
    w2 = g_w * p2 / (1.0 + p2)
    j1 = i1 - g_sel * EXPERTS_PER_GROUP
    j2 = i2 - g_sel * EXPERTS_PER_GROUP
    hx_ref[:, 0:xn.shape[1]] = xn
    hx_ref[:, xn.shape[1]:] = jnp.where(lane == j1, w1, 0.0) + jnp.where(lane == j2, w2, 0.0)

    onehot = jnp.where(lane == g_sel, 1.0, 0.0)
    r_i = lax.broadcasted_iota(jnp.int32, (TM, TM), 0)
    c_i = lax.broadcasted_iota(jnp.int32, (TM, TM), 1)
    before = jnp.where(c_i < r_i, 1.0, 0.0).astype(BF16)
    prior = jnp.dot(before, onehot.astype(BF16), preferred_element_type=F32) + carry_ref[...]
    rank = jnp.sum(jnp.where(lane == g_sel, prior, 0.0), axis=-1, keepdims=True)
    carry_ref[...] += jnp.sum(onehot, axis=0, keepdims=True)
    cnt_ref[...] = carry_ref[...]
    gid_ref[0] = jnp.broadcast_to(g_sel.astype(F32), (TM, LANES)).T[0:1, :].astype(jnp.int32)
    rank_ref[0] = jnp.broadcast_to(rank, (TM, LANES)).T[0:1, :].astype(jnp.int32)


def _moe_dispatch_kernel(pos_ref, hx_ref, xs_init_ref, xs_ref, sem):
    del xs_init_ref
    TM = hx_ref.shape[0]

    def row_copy(r):
        return pltpu.make_async_copy(hx_ref.at[pl.ds(r, 1), :], xs_ref.at[pl.ds(pos_ref[0, 0, r], 1), :], sem)

    def start(r, carry):
        row_copy(r).start()
        return carry

    def wait(r, carry):
        row_copy(r).wait()
        return carry

    lax.fori_loop(0, TM, start, 0)
    lax.fori_loop(0, TM, wait, 0)


def _moe_expert_kernel(grp_ref, xs_ref, wgu_ref, wd_ref, ys_ref, xn_ref, acc_ref):
    j = pl.program_id(0)
    e = pl.program_id(1)
    D = xn_ref.shape[1]
    lane = lax.broadcasted_iota(jnp.int32, (1, LANES), 1)

    @pl.when(grp_ref[j] >= 0)
    def _():
        @pl.when(e == 0)
        def _():
            xn_ref[...] = xs_ref[:, 0:D].astype(BF16)
            acc_ref[...] = jnp.zeros_like(acc_ref)

        hh = jnp.dot(xn_ref[...], wgu_ref[0], preferred_element_type=F32)
        cw = jnp.sum(jnp.where(lane == e, xs_ref[:, D:], 0.0), axis=-1, keepdims=True)
        act = _silu(hh[:, 0:EXPERT_FF]) * hh[:, EXPERT_FF:2 * EXPERT_FF] * cw
        acc_ref[...] += jnp.dot(act.astype(BF16), wd_ref[0], preferred_element_type=F32)

    last = e == EXPERTS_PER_GROUP - 1

    @pl.when(last & (grp_ref[j] >= 0))
    def _():
        ys_ref[...] = acc_ref[...]

    @pl.when(last & (grp_ref[j] < 0))
    def _():
        ys_ref[...] = jnp.zeros_like(ys_ref)


def _moe_combine_kernel(pos_ref, h_ref, ys_ref, o_ref, stage_ref, sem):
    TM = h_ref.shape[0]

    def row_copy(r):
        return pltpu.make_async_copy(ys_ref.at[pl.ds(pos_ref[0, 0, r], 1), :], stage_ref.at[pl.ds(r, 1), :], sem)

    def start(r, carry):
        row_copy(r).start()
        return carry

    def wait(r, carry):
        row_copy(r).wait()
        return carry

    lax.fori_loop(0, TM, start, 0)
    lax.fori_loop(0, TM, wait, 0)
    o_ref[...] = h_ref[...] + stage_ref[...]


def _moe(h2d, gain, w_group, b_group, w_expert, b_expert, w_gate_up, w_down):
    T, D = h2d.shape
    tr = _pick_tile(T, MOE_ROUTER_TILES)
    tm = _pick_tile(T, MOE_ROW_TILES)
    n_tiles = T // tm + N_GROUPS
    XW = D + LANES
    wr = jnp.zeros((D, 2 * LANES), F32)
    wr = wr.at[:, 0:N_EXPERTS].set(w_expert.astype(F32))
    wr = wr.at[:, LANES:LANES + N_GROUPS].set(w_group.astype(F32))
    br = jnp.zeros((1, 2 * LANES), F32)
    br = br.at[0, 0:N_EXPERTS].set(b_expert.astype(F32))
    br = br.at[0, LANES:LANES + N_GROUPS].set(b_group.astype(F32))
    wgu = w_gate_up.reshape(N_EXPERTS, D, 2 * EXPERT_FF).astype(BF16)
    wd = w_down.reshape(N_EXPERTS, EXPERT_FF, D).astype(BF16)

    hx, gid, rank, counts = pl.pallas_call(
        _moe_router_kernel,
        grid=(T // tr,),
        in_specs=[pl.BlockSpec((tr, D), lambda i: (i, 0)),
                  pl.BlockSpec((1, D), lambda i: (0, 0)),
                  pl.BlockSpec((D, 2 * LANES), lambda i: (0, 0)),
                  pl.BlockSpec((1, 2 * LANES), lambda i: (0, 0))],
        out_specs=[pl.BlockSpec((tr, XW), lambda i: (i, 0)),
                   pl.BlockSpec((1, 1, tr), lambda i: (i, 0, 0)),
                   pl.BlockSpec((1, 1, tr), lambda i: (i, 0, 0)),
                   pl.BlockSpec((1, LANES), lambda i: (0, 0))],
        out_shape=[jax.ShapeDtypeStruct((T, XW), F32),
                   jax.ShapeDtypeStruct((T // tr, 1, tr), jnp.int32),
                   jax.ShapeDtypeStruct((T // tr, 1, tr), jnp.int32),
                   jax.ShapeDtypeStruct((1, LANES), F32)],
        scratch_shapes=[pltpu.VMEM((1, LANES), F32)],
        compiler_params=_compiler_params(("arbitrary",)),
        name="moe_router",
    )(h2d, gain.reshape(1, D), wr, br)

    seg_tiles = -(-counts[0, 0:N_GROUPS].astype(jnp.int32) // tm)
    seg_end = jnp.cumsum(seg_tiles)
    seg_start = seg_end - seg_tiles
    pos = (seg_start * tm)[gid.reshape(T)] + rank.reshape(T)
    tile_ids = jnp.arange(n_tiles, dtype=jnp.int32)
    tile_group = jnp.sum((tile_ids[:, None] >= seg_end[None, :]).astype(jnp.int32), axis=1)
    tile_group = jnp.where(tile_ids < seg_end[N_GROUPS - 1], tile_group, -1)
    pos3 = pos.reshape(T // tm, 1, tm)

    xs = pl.pallas_call(
        _moe_dispatch_kernel,
        grid=(T // tm,),
        in_specs=[pl.BlockSpec((1, 1, tm), lambda i: (i, 0, 0), memory_space=pltpu.SMEM),
                  pl.BlockSpec((tm, XW), lambda i: (i, 0)),
                  pl.BlockSpec(memory_space=pl.ANY)],
        out_specs=pl.BlockSpec(memory_space=pl.ANY),
        out_shape=jax.ShapeDtypeStruct((n_tiles * tm, XW), F32),
        scratch_shapes=[pltpu.SemaphoreType.DMA(())],
        input_output_aliases={2: 0},
        compiler_params=_compiler_params(("arbitrary",)),
        name="moe_dispatch",
    )(pos3, hx, jnp.zeros((n_tiles * tm, XW), F32))

    def expert_index(j, e, grp):
        return (jnp.maximum(grp[j], 0) * EXPERTS_PER_GROUP + e, 0, 0)

    ys = pl.pallas_call(
        _moe_expert_kernel,
        grid_spec=pltpu.PrefetchScalarGridSpec(
            num_scalar_prefetch=1,
            grid=(n_tiles, EXPERTS_PER_GROUP),
            in_specs=[pl.BlockSpec((tm, XW), lambda j, e, grp: (j, 0)),
                      pl.BlockSpec((1, D, 2 * EXPERT_FF), expert_index),
                      pl.BlockSpec((1, EXPERT_FF, D), expert_index)],
            out_specs=pl.BlockSpec((tm, D), lambda j, e, grp: (j, 0)),
            scratch_shapes=[pltpu.VMEM((tm, D), BF16), pltpu.VMEM((tm, D), F32)]),
        out_shape=jax.ShapeDtypeStruct((n_tiles * tm, D), F32),
        compiler_params=_compiler_params(("parallel", "arbitrary")),
        name="moe_experts",
    )(tile_group, xs, wgu, wd)

    return pl.pallas_call(
        _moe_combine_kernel,
        grid=(T // tm,),
        in_specs=[pl.BlockSpec((1, 1, tm), lambda i: (i, 0, 0), memory_space=pltpu.SMEM),
                  pl.BlockSpec((tm, D), lambda i: (i, 0)),
                  pl.BlockSpec(memory_space=pl.ANY)],
        out_specs=pl.BlockSpec((tm, D), lambda i: (i, 0)),
        out_shape=jax.ShapeDtypeStruct((T, D), F32),
        scratch_shapes=[pltpu.VMEM((tm, D), F32), pltpu.SemaphoreType.DMA(())],
        compiler_params=_compiler_params(("arbitrary",)),
        name="moe_combine",
    )(pos3, h2d, ys)


def _dsa_prep_kernel(p_ref, qg_ref, kg_ref, lng_ref, lnb_ref, q_ref, kt_ref, v_ref, qi_ref, kit_ref, wi_ref):
    lane = lax.broadcasted_iota(jnp.int32, (1, LANES), 1)
    for h in range(ATT_HEADS):
        x = p_ref[:, DSA_Q_OFF + h * ATT_HD:DSA_Q_OFF + (h + 1) * ATT_HD]
        y = x * lax.rsqrt(jnp.mean(x * x, axis=-1, keepdims=True) + RMS_EPS) * qg_ref[...]
        q_ref[:, h * ATT_HD:(h + 1) * ATT_HD] = (y * (ATT_HD ** -0.5)).astype(BF16)
    for n in range(ATT_KV_HEADS):
        x = p_ref[:, DSA_K_OFF + n * ATT_HD:DSA_K_OFF + (n + 1) * ATT_HD]
        y = x * lax.rsqrt(jnp.mean(x * x, axis=-1, keepdims=True) + RMS_EPS) * kg_ref[...]
        kt_ref[0, n * ATT_HD:(n + 1) * ATT_HD, :] = y.T.astype(BF16)
    v_ref[...] = p_ref[:, DSA_V_OFF:DSA_V_OFF + ATT_KV_DIM].astype(BF16)
    qi_ref[...] = p_ref[:, DSA_QI_OFF:DSA_QI_OFF + IDX_HEADS * LANES].astype(BF16)
    x = p_ref[:, DSA_KI_OFF:DSA_KI_OFF + LANES]
    live = lane < IDX_HD
    mu = jnp.sum(x, axis=-1, keepdims=True) * (1.0 / IDX_HD)
    xc = jnp.where(live, x - mu, 0.0)
    var = jnp.sum(xc * xc, axis=-1, keepdims=True) * (1.0 / IDX_HD)
    ki = xc * lax.rsqrt(var + RMS_EPS) * lng_ref[...] + lnb_ref[...]
    kit_ref[0] = jnp.where(live, ki, 0.0).T.astype(BF16)
    wi_ref[...] = p_ref[:, DSA_WI_OFF:DSA_WI_OFF + LANES] * (IDX_HEADS ** -0.5 * IDX_HD ** -0.5)


def _dsa_prep(proj2d, q_gain, k_gain, ln_g, ln_b):
    T, W = proj2d.shape
    tm = KEY_TILE
    assert T % tm == 0
    lng = jnp.zeros((1, LANES), F32).at[0, 0:IDX_HD].set(ln_g.astype(F32))
    lnb = jnp.zeros((1, LANES), F32).at[0, 0:IDX_HD].set(ln_b.astype(F32))
    row = lambda i: (i, 0)
    fix = lambda i: (0, 0)
    blk = lambda i: (i, 0, 0)
    return pl.pallas_call(
        _dsa_prep_kernel,
        grid=(T // tm,),
        in_specs=[pl.BlockSpec((tm, W), row),
                  pl.BlockSpec((1, ATT_HD), fix), pl.BlockSpec((1, ATT_HD), fix),
                  pl.BlockSpec((1, LANES), fix), pl.BlockSpec((1, LANES), fix)],
        out_specs=[pl.BlockSpec((tm, ATT_Q_DIM), row), pl.BlockSpec((1, ATT_KV_DIM, tm), blk),
                   pl.BlockSpec((tm, ATT_KV_DIM), row), pl.BlockSpec((tm, IDX_HEADS * LANES), row),
                   pl.BlockSpec((1, LANES, tm), blk), pl.BlockSpec((tm, LANES), row)],
        out_shape=[jax.ShapeDtypeStruct((T, ATT_Q_DIM), BF16), jax.ShapeDtypeStruct((T // tm, ATT_KV_DIM, tm), BF16),
                   jax.ShapeDtypeStruct((T, ATT_KV_DIM), BF16), jax.ShapeDtypeStruct((T, IDX_HEADS * LANES), BF16),
                   jax.ShapeDtypeStruct((T // tm, LANES, tm), BF16), jax.ShapeDtypeStruct((T, LANES), F32)],
        compiler_params=_compiler_params(("parallel",)),
        name="dsa_prep",
    )(proj2d, q_gain.reshape(1, ATT_HD), k_gain.reshape(1, ATT_HD), lng, lnb)


def _dsa_kernel(q_ref, qi_ref, wi_ref, kt_ref, v_ref, kit_ref, bias_ref, bmax_ref, o_ref,
                sc_ref, qg_ref, wib_ref, m_ref, l_ref, acc_ref, knorm_ref, *, n_select, live_rows):
    i = pl.program_id(1)
    TQ = ATT_TILE
    TK = KEY_TILE
    HALVES = TK // LANES

    @pl.when(i == 0)
    def _():
        def body(kt, mx):
            kk = kt_ref[0, kt].astype(F32)
            return tuple(jnp.maximum(mx[n], jnp.sum(jnp.square(kk[n * ATT_HD:(n + 1) * ATT_HD, :]),
                                                    axis=0, keepdims=True)) for n in range(ATT_KV_HEADS))

        mx = lax.fori_loop(0, kt_ref.shape[1], body,
                           tuple(jnp.zeros((1, TK), F32) for _ in range(ATT_KV_HEADS)))
        for n in range(ATT_KV_HEADS):
            knorm_ref[n] = jnp.broadcast_to(jnp.sqrt(jnp.max(mx[n], axis=-1, keepdims=True)), (1, LANES))

    @pl.when(i * TQ >= live_rows)
    def _():
        o_ref[...] = jnp.zeros_like(o_ref)

    @pl.when(i * TQ < live_rows)
    def _():
        _dsa_tile(i, q_ref, qi_ref, wi_ref, kt_ref, v_ref, kit_ref, bias_ref, bmax_ref, o_ref,
                  sc_ref, qg_ref, wib_ref, m_ref, l_ref, acc_ref, knorm_ref, n_select, TQ, TK, HALVES)


def _dsa_tile(i, q_ref, qi_ref, wi_ref, kt_ref, v_ref, kit_ref, bias_ref, bmax_ref, o_ref,
              sc_ref, qg_ref, wib_ref, m_ref, l_ref, acc_ref, knorm_ref, n_select, TQ, TK, HALVES):
    nkt = lax.shift_right_logical(i, 1) + 1
    fmin = float(jnp.finfo(F32).min)
    kf = float(n_select)

    q_row = i * TQ + lax.broadcasted_iota(jnp.int32, (TQ, 1), 0)
    q_chunk = lax.shift_right_logical(q_row, int(math.log2(CHUNK)))
    k_lane = lax.broadcasted_iota(jnp.int32, (1, TK), 1)

    def halves(x):
        return [x[:, j * LANES:(j + 1) * LANES] for j in range(HALVES)]

    def lane_wide(col):
        return jnp.broadcast_to(col, (TQ, LANES))

    wi = wi_ref[0]
    for h in range(IDX_HEADS):
        wib_ref[h] = jnp.broadcast_to(wi[:, h:h + 1], (TQ, TK))

    def score_tiles(kts):
        d = [[jnp.dot(qi_ref[0, :, h * LANES:(h + 1) * LANES], kit_ref[0, kt], preferred_element_type=F32)
              for h in range(IDX_HEADS)] for kt in kts]
        for t, kt in enumerate(kts):
            acc = wib_ref[0] * jnp.maximum(d[t][0], 0.0)
            for h in range(1, IDX_HEADS):
                acc = acc + wib_ref[h] * jnp.maximum(d[t][h], 0.0)
            sc_ref[kt] = acc

    def paired_loop(n, fn):
        def body(j, carry):
            fn([2 * j, 2 * j + 1])
            return carry

        lax.fori_loop(0, lax.shift_right_arithmetic(n, 1), body, 0)

        @pl.when(lax.bitwise_and(n, 1) == 1)
        def _():
            fn([n - 1])

    paired_loop(nkt, score_tiles)

    def mask_inadmissible(kt):
        k_pos = kt * TK + k_lane
        adm = (lax.shift_right_logical(k_pos, int(math.log2(CHUNK))) <= q_chunk) & (k_pos >= META_PAD)
        sc_ref[kt] = jnp.where(adm, sc_ref[kt], NEG_INF)

    mask_inadmissible(0)

    @pl.when(nkt > 1)
    def _():
        mask_inadmissible(nkt - 1)

    n_adm = CHUNK * (q_chunk + 1) - META_PAD
    need = n_adm > n_select

    def row_sum(x):
        return jnp.sum(x, axis=-1, keepdims=True)

    def count_ge(t):
        tb = lane_wide(t)

        def body(kt, acc):
            for s in halves(sc_ref[kt]):
                acc = acc + jnp.where(s >= tb, 1.0, 0.0)
            return acc

        return row_sum(lax.fori_loop(0, nkt, body, jnp.zeros((TQ, LANES), F32)))

    def minmax_body(kt, carry):
        mn, mx = carry
        for s in halves(sc_ref[kt]):
            mn = jnp.minimum(mn, jnp.where(s == NEG_INF, jnp.inf, s))
            mx = jnp.maximum(mx, s)
        return mn, mx

    mn_l, mx_l = lax.fori_loop(0, nkt, minmax_body,
                               (jnp.full((TQ, LANES), jnp.inf, F32), jnp.full((TQ, LANES), NEG_INF, F32)))
    s_min = jnp.min(mn_l, axis=-1, keepdims=True)
    s_max = jnp.max(mx_l, axis=-1, keepdims=True)
    lo0 = s_min
    hi0 = s_max + (s_max - s_min) + 1.0
    cnt0 = n_adm.astype(F32)
    active0 = jnp.where(need, 1.0, 0.0)

    def any_active(active):
        return jnp.max(active) > 0.0

    def bis_step(lo, hi, cnt, active):
        mid = lo + (hi - lo) * 0.5
        c = count_ge(mid)
        ge = c >= kf
        moving = (mid > lo) & (mid < hi) & (active > 0.0)
        lo_n = jnp.where(moving & ge, mid, lo)
        hi_n = jnp.where(moving & (~ge), mid, hi)
        cnt_n = jnp.where(moving & ge, c, cnt)
        active_n = jnp.where(moving & (cnt_n != kf), 1.0, 0.0)
        return lo_n, hi_n, cnt_n, active_n

    def bis_body(st):
        it, rest = st[0], st[1:]
        for _ in range(BISECT_UNROLL):
            rest = bis_step(*rest)
        return (it + BISECT_UNROLL,) + rest

    _, lo, hi, cnt, _ = lax.while_loop(
        lambda st: jnp.logical_and(st[0] < BISECT_STEPS, any_active(st[4])), bis_body,
        (jnp.int32(0), lo0, hi0, cnt0, active0))

    def snap_body(st):
        lo, hi, cnt, active = st
        hb = lane_wide(hi)

        def body(kt, mx):
            for s in halves(sc_ref[kt]):
                mx = jnp.maximum(mx, jnp.where(s < hb, s, NEG_INF))
            return mx

        v1 = jnp.max(lax.fori_loop(0, nkt, body, jnp.full((TQ, LANES), NEG_INF, F32)),
                     axis=-1, keepdims=True)
        c1 = count_ge(v1)
        on = active > 0.0
        hit = on & (c1 >= kf)
        lo_n = jnp.where(hit, v1, lo)
        cnt_n = jnp.where(hit, c1, cnt)
        hi_n = jnp.where(on & (~hit), v1, hi)
        return lo_n, hi_n, cnt_n, jnp.where(on & (~hit), 1.0, 0.0)

    unsettled = jnp.where(need & (cnt != kf), 1.0, 0.0)
    lo, hi, cnt, _ = lax.while_loop(lambda st: any_active(st[3]), snap_body, (lo, hi, cnt, unsettled))

    thr = jnp.where(need, lo, fmin)
    tied = need & (cnt > kf)
    big_pos = jnp.int32(2 ** 30)

    def tie_break():
        tb = lane_wide(thr)

        def count_pred(pred):
            def body(kt, acc):
                pos = kt * TK + k_lane
                for j, s in enumerate(halves(sc_ref[kt])):
                    acc = acc + jnp.where(pred(s, pos[:, j * LANES:(j + 1) * LANES]), 1.0, 0.0)
                return acc
            return row_sum(lax.fori_loop(0, nkt, body, jnp.zeros((TQ, LANES), F32)))

        want = kf - count_pred(lambda s, pos: s > tb)

        def body(_, st):
            jl, jh = st
            jm = lax.shift_right_arithmetic(jl + jh, 1)
            c = count_pred(lambda s, pos: (s == tb) & (pos <= jm))
            ok = c >= want
            return jnp.where(ok, jl, jm), jnp.where(ok, jm, jh)

        n_keys = sc_ref.shape[0] * TK
        steps = int(math.ceil(math.log2(n_keys + 1))) + 1
        jl0 = jnp.full((TQ, 1), -1, jnp.int32)
        jh0 = jnp.full((TQ, 1), n_keys, jnp.int32)
        _, jh = lax.fori_loop(0, steps, body, (jl0, jh0))
        return jnp.where(tied, jh, big_pos)

    pos_cap = lax.cond(jnp.max(jnp.where(tied, 1.0, 0.0)) > 0.0, tie_break,
                       lambda: jnp.full((TQ, 1), big_pos, jnp.int32))

    def to_mask(kt, carry):
        s = sc_ref[kt]
        pos = kt * TK + k_lane
        sel = (s > thr) | ((s == thr) & (pos <= pos_cap))
        sc_ref[kt] = jnp.where(sel, 0.0, NEG_INF)
        return carry

    lax.fori_loop(0, nkt, to_mask, 0)

    for n in range(ATT_KV_HEADS):
        for g in range(ATT_GROUP):
            h = n * ATT_GROUP + g
            qg_ref[n, g * TQ:(g + 1) * TQ, :] = q_ref[0, :, h * ATT_HD:(h + 1) * ATT_HD]
    l_ref[...] = jnp.zeros_like(l_ref)
    acc_ref[...] = jnp.zeros_like(acc_ref)

    kv_heads = range(ATT_KV_HEADS)

    def raw_logits(kts):
        return [[jnp.dot(qg_ref[n], kt_ref[0, kt, n * ATT_HD:(n + 1) * ATT_HD, :], preferred_element_type=F32)
                 for n in kv_heads] for kt in kts]

    def masked_logits(s, mask, n, g, b0):
        sg = s[g * TQ:(g + 1) * TQ, :] + mask
        if b0 is not None:
            h = n * ATT_GROUP + g
            sg = sg + jnp.concatenate([bias_ref[h, b0 + j] for j in range(HALVES)], axis=1)
        return sg

    def max_pass(kts, b0):
        raw = raw_logits(kts)
        masks = [sc_ref[kt] for kt in kts]
        for n in kv_heads:
            for g in range(ATT_GROUP):
                rows = slice(g * TQ, (g + 1) * TQ)
                m = m_ref[n, rows, :]
                for t in range(len(kts)):
                    for part in halves(masked_logits(raw[t][n], masks[t], n, g, b0)):
                        m = jnp.maximum(m, part)
                m_ref[n, rows, :] = m

    def exp_pass(kts, b0):
        raw = raw_logits(kts)
        masks = [sc_ref[kt] for kt in kts]
        for n in kv_heads:
            ps = [[] for _ in kts]
            for g in range(ATT_GROUP):
                rows = slice(g * TQ, (g + 1) * TQ)
                m = m_ref[n, rows, :]
                l_new = l_ref[n, rows, :]
                for t in range(len(kts)):
                    parts = [jnp.exp(part - m) for part in halves(masked_logits(raw[t][n], masks[t], n, g, b0))]
                    l_new = l_new + functools.reduce(lambda a, b: a + b, parts)
                    ps[t].append(jnp.concatenate(parts, axis=1).astype(BF16))
                l_ref[n, rows, :] = l_new
            pv = None
            for t, kt in enumerate(kts):
                vv = v_ref[0, pl.ds(pl.multiple_of(kt * TK, TK), TK), n * ATT_HD:(n + 1) * ATT_HD]
                d = jnp.dot(jnp.concatenate(ps[t], axis=0), vv, preferred_element_type=F32)
                pv = d if pv is None else pv + d
            acc_ref[n] += pv

    n_far = lax.shift_right_arithmetic(i - 1, 1)
    parity = lax.bitwise_and(i, 1)

    def run_pass(tile_fn):
        paired_loop(jnp.maximum(n_far, 0), lambda kts: tile_fn(kts, None))

        @pl.when(i >= 1)
        def _():
            tile_fn([n_far], parity)

        @pl.when(parity == 0)
        def _():
            tile_fn([lax.shift_right_logical(i, 1)], 2)

    for n in range(ATT_KV_HEADS):
        for g in range(ATT_GROUP):
            h = n * ATT_GROUP + g
            qh = qg_ref[n, g * TQ:(g + 1) * TQ, :].astype(F32)
            q_len = jnp.sqrt(jnp.sum(qh * qh, axis=-1, keepdims=True))
            bound = q_len * knorm_ref[n] * SHIFT_SLACK + (bmax_ref[h] + SHIFT_SLACK - 1.0)
            m_ref[n, g * TQ:(g + 1) * TQ, :] = jnp.broadcast_to(bound, (TQ, LANES))
    run_pass(exp_pass)

    def smallest_row_sum():
        return functools.reduce(jnp.minimum, [jnp.min(jnp.sum(l_ref[n], axis=-1, keepdims=True))
                                              for n in range(ATT_KV_HEADS)])

    @pl.when(jnp.logical_not(smallest_row_sum() > ROW_SUM_FLOOR))
    def _():
        m_ref[...] = jnp.full(m_ref.shape, -1e30, F32)
        run_pass(max_pass)
        for n in range(ATT_KV_HEADS):
            m_ref[n] = jnp.broadcast_to(jnp.max(m_ref[n], axis=-1, keepdims=True), m_ref.shape[1:])
        l_ref[...] = jnp.zeros_like(l_ref)
        acc_ref[...] = jnp.zeros_like(acc_ref)
        run_pass(exp_pass)

    for n in range(ATT_KV_HEADS):
        inv_l = 1.0 / jnp.sum(l_ref[n], axis=-1, keepdims=True)
        for g in range(ATT_GROUP):
            h = n * ATT_GROUP + g
            rows = slice(g * TQ, (g + 1) * TQ)
            o_ref[0, :, h * ATT_HD:(h + 1) * ATT_HD] = (acc_ref[n, rows, :] * inv_l[rows, :]).astype(o_ref.dtype)


def _t5_bucket(rel):
    nb = REL_BUCKETS // 2
    max_exact = nb // 2
    ret = jnp.where(rel > 0, nb, 0)
    n = jnp.abs(rel)
    nf = jnp.maximum(n, 1).astype(jnp.float32)
    large = max_exact + (jnp.log(nf / max_exact) / math.log(REL_MAX_DIST / max_exact)
                         * (nb - max_exact)).astype(jnp.int32)
    large = jnp.minimum(large, nb - 1)
    return ret + jnp.where(n < max_exact, n, large)


def _dsa_core(q, kt, v, qi, kit, wi, rel_bias, n_select, live_rows):
    B, LP, _ = q.shape
    nq = LP // ATT_TILE
    nkt = LP // KEY_TILE
    a = jnp.arange(ATT_TILE)[:, None]
    xx = jnp.arange(2 * KEY_TILE)[None, :]
    rel = (xx - KEY_TILE) - a
    far = rel_bias.astype(F32)[REL_BUCKETS // 2 - 1]
    onehot = jax.nn.one_hot(_t5_bucket(rel), REL_BUCKETS, dtype=F32)
    table = jnp.einsum("abk,kh->abh", onehot, rel_bias.astype(F32),
                       precision=lax.Precision.HIGHEST) - far
    bias_near = jnp.transpose(table.reshape(ATT_TILE, 2 * KEY_TILE // LANES, LANES, ATT_HEADS), (3, 1, 0, 2))
    bias_max = jnp.broadcast_to(jnp.maximum(jnp.max(table, axis=(0, 1)), 0.0)[:, None, None],
                                (ATT_HEADS, 1, LANES))
    kern = functools.partial(_dsa_kernel, n_select=n_select, live_rows=live_rows)
    tile = lambda b, i: (b, i, 0)
    whole3 = lambda b, i: (b, 0, 0)
    whole4 = lambda b, i: (b, 0, 0, 0)
    return pl.pallas_call(
        kern,
        grid=(B, nq),
        in_specs=[pl.BlockSpec((1, ATT_TILE, ATT_Q_DIM), tile),
                  pl.BlockSpec((1, ATT_TILE, IDX_HEADS * LANES), tile),
                  pl.BlockSpec((1, ATT_TILE, LANES), tile),
                  pl.BlockSpec((1, nkt, ATT_KV_DIM, KEY_TILE), whole4),
                  pl.BlockSpec((1, LP, ATT_KV_DIM), whole3),
                  pl.BlockSpec((1, nkt, LANES, KEY_TILE), whole4),
                  pl.BlockSpec((ATT_HEADS, 2 * KEY_TILE // LANES, ATT_TILE, LANES), lambda b, i: (0, 0, 0, 0)),
                  pl.BlockSpec((ATT_HEADS, 1, LANES), lambda b, i: (0, 0, 0))],
        out_specs=pl.BlockSpec((1, ATT_TILE, ATT_Q_DIM), tile),
        out_shape=jax.ShapeDtypeStruct((B, LP, ATT_Q_DIM), BF16),
        scratch_shapes=[pltpu.VMEM((nkt, ATT_TILE, KEY_TILE), F32),
                        pltpu.VMEM((ATT_KV_HEADS, ATT_GROUP * ATT_TILE, ATT_HD), BF16),
                        pltpu.VMEM((IDX_HEADS, ATT_TILE, KEY_TILE), F32),
                        pltpu.VMEM((ATT_KV_HEADS, ATT_GROUP * ATT_TILE, LANES), F32),
                        pltpu.VMEM((ATT_KV_HEADS, ATT_GROUP * ATT_TILE, LANES), F32),
                        pltpu.VMEM((ATT_KV_HEADS, ATT_GROUP * ATT_TILE, ATT_HD), F32),
                        pltpu.VMEM((ATT_KV_HEADS, 1, LANES), F32)],
        compiler_params=_compiler_params(("arbitrary", "arbitrary")),
        name="dsa_core",
    )(q, qi, wi, kt, v, kit, bias_near, bias_max)


def _gdn_in_weight(w_in):
    D = w_in.shape[0]
    w = jnp.zeros((D, GDN_PROJ_W), F32).at[:, 0:w_in.shape[1]].set(w_in.astype(F32))
    return w.astype(BF16)


def _dsa_in_weight(w_in):
    D = w_in.shape[0]
    w = jnp.zeros((D, DSA_PROJ_W), F32)
    w = w.at[:, 0:DSA_QI_OFF].set(w_in[:, 0:DSA_QI_OFF].astype(F32))
    src = ATT_Q_DIM + 2 * ATT_KV_DIM
    for h in range(IDX_HEADS):
        w = w.at[:, DSA_QI_OFF + h * LANES:DSA_QI_OFF + h * LANES + IDX_HD].set(
            w_in[:, src + h * IDX_HD:src + (h + 1) * IDX_HD].astype(F32))
    src += IDX_HEADS * IDX_HD
    w = w.at[:, DSA_KI_OFF:DSA_KI_OFF + IDX_HD].set(w_in[:, src:src + IDX_HD].astype(F32))
    src += IDX_HD
    w = w.at[:, DSA_WI_OFF:DSA_WI_OFF + IDX_HEADS].set(w_in[:, src:src + IDX_HEADS].astype(F32))
    return w.astype(BF16)


def kernel(x, meta_tokens, norm_mix, norm_ffn, rel_bias, gdn_w_in, gdn_conv, gdn_a_log, gdn_dt_bias,
           gdn_o_norm, gdn_w_out, dsa_w_in, dsa_q_norm, dsa_k_norm, dsa_idx_ln_g, dsa_idx_ln_b, dsa_w_out,
           moe_w_group, moe_b_group, moe_w_expert, moe_b_expert, moe_w_gate_up, moe_w_down):
    B, S, D = x.shape
    depth = norm_mix.shape[0]
    n_select = min(TOPK_MAX, S // 4)
    off = META_PAD + N_META
    LP = -(-(off + S) // ROW_ALIGN) * ROW_ALIGN
    meta = jnp.broadcast_to(meta_tokens.astype(x.dtype)[None], (B, N_META, D))
    h = jnp.concatenate([jnp.zeros((B, META_PAD, D), x.dtype), meta, x,
                         jnp.zeros((B, LP - off - S, D), x.dtype)], axis=1)
    h = h.reshape(B * LP, D)
    for i in range(depth):
        j = i // 2
        if i % 2 == 0:
            proj = _norm_matmul(h, norm_mix[i], _gdn_in_weight(gdn_w_in[j]))
            o = _gdn_core(proj.reshape(B, LP, GDN_PROJ_W), gdn_conv[j].astype(F32), gdn_a_log[j],
                          gdn_dt_bias[j], gdn_o_norm[j])
            h = _matmul_residual(o.reshape(B * LP, GDN_V_DIM), gdn_w_out[j].astype(BF16), h)
        else:
            proj = _norm_matmul(h, norm_mix[i], _dsa_in_weight(dsa_w_in[j]))
            q, kt, v, qi, kit, wi = _dsa_prep(proj, dsa_q_norm[j], dsa_k_norm[j], dsa_idx_ln_g[j], dsa_idx_ln_b[j])
            r3 = lambda t: t.reshape(B, LP, t.shape[-1])
            r4 = lambda t: t.reshape(B, LP // KEY_TILE, t.shape[-2], KEY_TILE)
            o = _dsa_core(r3(q), r4(kt), r3(v), r3(qi), r4(kit), r3(wi), rel_bias, n_select, off + S)
            h = _matmul_residual(o.reshape(B * LP, ATT_Q_DIM), dsa_w_out[j].astype(BF16), h)
        h = _moe(h, norm_ffn[i], moe_w_group[i], moe_b_group[i], moe_w_expert[i], moe_b_expert[i],
                 moe_w_gate_up[i], moe_w_down[i])
    return h.reshape(B, LP, D)[:, off:off + S]
```

```python
import functools
import math

import jax
import jax.numpy as jnp
from jax import lax
from jax.experimental import pallas as pl
from jax.experimental.pallas import tpu as pltpu

F32 = jnp.float32
BF16 = jnp.bfloat16

LANES = 128
SUBLANES = 8
VMEM_LIMIT_BYTES = 56 * 1024 * 1024

CHUNK = 64
N_META = 16
META_PAD = (-N_META) % CHUNK
RMS_EPS = 1e-6
L2_EPS = 1e-6

GDN_HEADS = 8
GDN_DK = 128
GDN_DV = 128
GDN_CONV = 4
GDN_QK_DIM = GDN_HEADS * GDN_DK
GDN_V_DIM = GDN_HEADS * GDN_DV
GDN_CONV_DIM = 2 * GDN_QK_DIM + GDN_V_DIM
GDN_GATE_OFF = GDN_CONV_DIM + GDN_V_DIM
GDN_PROJ_W = GDN_GATE_OFF + LANES

ATT_HEADS = 8
ATT_KV_HEADS = 2
ATT_GROUP = ATT_HEADS // ATT_KV_HEADS
ATT_HD = 128
ATT_Q_DIM = ATT_HEADS * ATT_HD
ATT_KV_DIM = ATT_KV_HEADS * ATT_HD
IDX_HEADS = 8
IDX_HD = 64
TOPK_MAX = 256
REL_BUCKETS = 32
REL_MAX_DIST = 128

DSA_Q_OFF = 0
DSA_K_OFF = ATT_Q_DIM
DSA_V_OFF = DSA_K_OFF + ATT_KV_DIM
DSA_QI_OFF = DSA_V_OFF + ATT_KV_DIM
DSA_KI_OFF = DSA_QI_OFF + IDX_HEADS * LANES
DSA_WI_OFF = DSA_KI_OFF + LANES
DSA_PROJ_W = DSA_WI_OFF + LANES

N_GROUPS = 4
EXPERTS_PER_GROUP = 8
N_EXPERTS = N_GROUPS * EXPERTS_PER_GROUP
EXPERT_FF = 256

MOE_ROUTER_TILES = (512, 256, 128)
MOE_ROW_TILES = (512, 256, 128)

ATT_TILE = 128
KEY_TILE = 256
ROW_ALIGN = KEY_TILE
BISECT_UNROLL = 4
BISECT_STEPS = 24
SHIFT_SLACK = 1.001
ROW_SUM_FLOOR = 1e-30
NEG_INF = float("-inf")


def _pick_tile(n, candidates):
    for c in candidates:
        if n % c == 0:
            return c
    raise ValueError(f"no tile for {n}")


def _compiler_params(semantics):
    return pltpu.CompilerParams(dimension_semantics=semantics, vmem_limit_bytes=VMEM_LIMIT_BYTES)


def _silu(x):
    return x * jax.nn.sigmoid(x)


def _norm_matmul_kernel(x_ref, g_ref, w_ref, o_ref):
    x = x_ref[...]
    y = x * lax.rsqrt(jnp.mean(x * x, axis=-1, keepdims=True) + RMS_EPS) * g_ref[...]
    o_ref[...] = jnp.dot(y.astype(BF16), w_ref[...], preferred_element_type=F32)


def _norm_matmul(h2d, gain, w_bf16):
    T, D = h2d.shape
    N = w_bf16.shape[1]
    tm = _pick_tile(T, (256, 128))
    return pl.pallas_call(
        _norm_matmul_kernel,
        grid=(T // tm,),
        in_specs=[pl.BlockSpec((tm, D), lambda i: (i, 0)),
                  pl.BlockSpec((1, D), lambda i: (0, 0)),
                  pl.BlockSpec((D, N), lambda i: (0, 0))],
        out_specs=pl.BlockSpec((tm, N), lambda i: (i, 0)),
        out_shape=jax.ShapeDtypeStruct((T, N), F32),
        compiler_params=_compiler_params(("parallel",)),
        name="norm_matmul",
    )(h2d, gain.reshape(1, D), w_bf16)


def _matmul_residual_kernel(a_ref, w_ref, r_ref, o_ref):
    o_ref[...] = r_ref[...] + jnp.dot(a_ref[...], w_ref[...], preferred_element_type=F32)


def _matmul_residual(a_bf16, w_bf16, res):
    T, K = a_bf16.shape
    N = w_bf16.shape[1]
    tm = _pick_tile(T, (512, 256, 128))
    return pl.pallas_call(
        _matmul_residual_kernel,
        grid=(T // tm,),
        in_specs=[pl.BlockSpec((tm, K), lambda i: (i, 0)),
                  pl.BlockSpec((K, N), lambda i: (0, 0)),
                  pl.BlockSpec((tm, N), lambda i: (i, 0))],
        out_specs=pl.BlockSpec((tm, N), lambda i: (i, 0)),
        out_shape=jax.ShapeDtypeStruct((T, N), F32),
        compiler_params=_compiler_params(("parallel",)),
        name="matmul_residual",
    )(a_bf16, w_bf16, res)


def _nt_dot(a, b):
    return lax.dot_general(a, b, (((1,), (1,)), ((), ())), preferred_element_type=F32)


def _tn_dot(a, b):
    return lax.dot_general(a, b, (((0,), (0,)), ((), ())), preferred_element_type=F32)


def _gdn_kernel(proj_ref, conv_ref, gate_ref, ogain_ref, o_ref, state_ref, xbuf_ref, qkv_ref):
    c = pl.program_id(1)
    C = CHUNK

    @pl.when(c == 0)
    def _():
        state_ref[...] = jnp.zeros_like(state_ref)
        xbuf_ref[0:SUBLANES, :] = jnp.zeros((SUBLANES, GDN_CONV_DIM), F32)

    row = c * C + lax.broadcasted_iota(jnp.int32, (C, 1), 0)
    live = row >= META_PAD
    u = jnp.where(live, proj_ref[0, :, 0:GDN_CONV_DIM], 0.0)
    xbuf_ref[SUBLANES:SUBLANES + C, :] = u
    conv = conv_ref[0:1, :] * xbuf_ref[SUBLANES - 3:SUBLANES - 3 + C, :]
    for j in range(1, GDN_CONV):
        conv = conv + conv_ref[j:j + 1, :] * xbuf_ref[SUBLANES - 3 + j:SUBLANES - 3 + j + C, :]
    xbuf_ref[0:SUBLANES, :] = u[C - SUBLANES:C, :]
    qkv_ref[...] = _silu(conv)
    qkv = qkv_ref

    gates = jnp.where(live, proj_ref[0, :, GDN_GATE_OFF:GDN_GATE_OFF + LANES], 0.0)
    beta = jax.nn.sigmoid(gates)
    neg_rate = -jnp.exp(gate_ref[0:1, :])
    g = neg_rate * jax.nn.softplus(gates + gate_ref[1:2, :])
    r_i = lax.broadcasted_iota(jnp.int32, (C, C), 0)
    c_i = lax.broadcasted_iota(jnp.int32, (C, C), 1)
    causal = c_i <= r_i
    strict = c_i < r_i
    gc = jnp.dot(causal.astype(F32), g, preferred_element_type=F32,
                 precision=lax.Precision.HIGHEST)
    gc_t = gc.T
    eg = jnp.exp(gc)
    g_last = gc[C - 1:C, :]
    eg_last = jnp.exp(g_last)
    ek = jnp.exp(g_last - gc)

    heads = range(GDN_HEADS)
    dot = functools.partial(jnp.dot, preferred_element_type=F32)
    col = lambda a, h: a[:, GDN_HEADS + h:GDN_HEADS + h + 1]
    qs, ks, k16s, xs, decays = [], [], [], [], []
    for h in heads:
        lo = h * GDN_DK
        q = qkv[:, lo:lo + GDN_DK]
        k = qkv[:, GDN_QK_DIM + lo:GDN_QK_DIM + lo + GDN_DK]
        v = qkv[:, 2 * GDN_QK_DIM + h * GDN_DV:2 * GDN_QK_DIM + (h + 1) * GDN_DV]
        q = q * lax.rsqrt(jnp.sum(q * q, axis=-1, keepdims=True) + L2_EPS) * (GDN_DK ** -0.5)
        k = k * lax.rsqrt(jnp.sum(k * k, axis=-1, keepdims=True) + L2_EPS)
        kb = k * beta[:, h:h + 1]
        vb = v * beta[:, h:h + 1]
        qs.append(q)
        ks.append(k)
        k16s.append(k.astype(BF16))
        xs.append((kb, jnp.concatenate([vb, kb * col(eg, h)], axis=1)))
        decays.append(jnp.exp(jnp.where(causal, col(gc, h) - gc_t[GDN_HEADS + h:GDN_HEADS + h + 1, :],
                                        NEG_INF)))
    kk = [_nt_dot(xs[h][0].astype(BF16), k16s[h]) for h in heads]
    qk = [_nt_dot(qs[h].astype(BF16), k16s[h]) for h in heads]
    ns = [-jnp.where(strict, kk[h] * decays[h], 0.0) for h in heads]
    xs = [x for _, x in xs]
    for it in range(6):
        n16 = [n.astype(BF16) for n in ns]
        xs = [xs[h] + dot(n16[h], xs[h].astype(BF16)) for h in heads]
        if it < 5:
            ns = [dot(n16[h], n16[h]) for h in heads]
    qk = [jnp.where(causal, qk[h] * decays[h], 0.0).astype(BF16) for h in heads]
    states = [state_ref[h] for h in heads]
    s16 = [s.astype(BF16) for s in states]
    ws = [dot(xs[h][:, GDN_DV:2 * GDN_DV].astype(BF16), s16[h]) for h in heads]
    qs_s = [dot((qs[h] * col(eg, h)).astype(BF16), s16[h]) for h in heads]
    v_new = [(xs[h][:, 0:GDN_DV] - ws[h]).astype(BF16) for h in heads]
    os_ = [qs_s[h] + dot(qk[h], v_new[h]) for h in heads]
    kv = [_tn_dot((ks[h] * col(ek, h)).astype(BF16), v_new[h]) for h in heads]
    for h in heads:
        state_ref[h] = states[h] * col(eg_last, h) + kv[h]
        z = proj_ref[0, :, GDN_CONV_DIM + h * GDN_DV:GDN_CONV_DIM + (h + 1) * GDN_DV]
        o = os_[h]
        on = o * lax.rsqrt(jnp.mean(o * o, axis=-1, keepdims=True) + RMS_EPS) * ogain_ref[...]
        o_ref[0, :, h * GDN_DV:(h + 1) * GDN_DV] = (on * _silu(z)).astype(o_ref.dtype)


def _gdn_core(proj, conv_w, a_log, dt_bias, o_gain):
    B, LP, W = proj.shape
    nc = LP // CHUNK
    gate = jnp.zeros((2, LANES), F32)
    gate = gate.at[0, GDN_HEADS:2 * GDN_HEADS].set(a_log.astype(F32))
    gate = gate.at[1, GDN_HEADS:2 * GDN_HEADS].set(dt_bias.astype(F32))
    return pl.pallas_call(
        _gdn_kernel,
        grid=(B, nc),
        in_specs=[pl.BlockSpec((1, CHUNK, W), lambda b, c: (b, c, 0)),
                  pl.BlockSpec((GDN_CONV, GDN_CONV_DIM), lambda b, c: (0, 0)),
                  pl.BlockSpec((2, LANES), lambda b, c: (0, 0)),
                  pl.BlockSpec((1, GDN_DV), lambda b, c: (0, 0))],
        out_specs=pl.BlockSpec((1, CHUNK, GDN_V_DIM), lambda b, c: (b, c, 0)),
        out_shape=jax.ShapeDtypeStruct((B, LP, GDN_V_DIM), BF16),
        scratch_shapes=[pltpu.VMEM((GDN_HEADS, GDN_DK, GDN_DV), F32),
                        pltpu.VMEM((SUBLANES + CHUNK, GDN_CONV_DIM), F32),
                        pltpu.VMEM((CHUNK, GDN_CONV_DIM), F32)],
        compiler_params=_compiler_params(("parallel", "arbitrary")),
        name="gdn_core",
    )(proj, conv_w, gate, o_gain.reshape(1, GDN_DV))


def _moe_router_kernel(h_ref, gain_ref, wr_ref, br_ref, hx_ref, gid_ref, rank_ref, cnt_ref, carry_ref):
    i = pl.program_id(0)
    TM = h_ref.shape[0]
    lane = lax.broadcasted_iota(jnp.int32, (1, LANES), 1)

    @pl.when(i == 0)
    def _():
        carry_ref[...] = jnp.zeros_like(carry_ref)

    x = h_ref[...]
    xn = x * lax.rsqrt(jnp.mean(x * x, axis=-1, keepdims=True) + RMS_EPS) * gain_ref[...]
    logits = jnp.dot(xn, wr_ref[...], preferred_element_type=F32,
                     precision=lax.Precision.HIGHEST) + br_ref[...]
    e_log = logits[:, 0:LANES]
    g_log = jnp.where(lane < N_GROUPS, logits[:, LANES:2 * LANES], NEG_INF)
    g_max = jnp.max(g_log, axis=-1, keepdims=True)
    g_sel = jnp.min(jnp.where(g_log == g_max, lane, LANES), axis=-1, keepdims=True)
    g_w = 1.0 / jnp.sum(jnp.exp(g_log - g_max), axis=-1, keepdims=True)
    in_grp = (lane >= g_sel * EXPERTS_PER_GROUP) & (lane < (g_sel + 1) * EXPERTS_PER_GROUP)
    l0 = jnp.where(in_grp, e_log, NEG_INF)
    m1 = jnp.max(l0, axis=-1, keepdims=True)
    i1 = jnp.min(jnp.where(l0 == m1, lane, LANES), axis=-1, keepdims=True)
    l1 = jnp.where(lane == i1, NEG_INF, l0)
    m2 = jnp.max(l1, axis=-1, keepdims=True)
    i2 = jnp.min(jnp.where(l1 == m2, lane, LANES), axis=-1, keepdims=True)
    p2 = jnp.exp(m2 - m1)
    w1 = g_w / (1.0 + p2)
    w2 = g_w * p2 / (1.0 + p2)
    j1 = i1 - g_sel * EXPERTS_PER_GROUP
    j2 = i2 - g_sel * EXPERTS_PER_GROUP
    hx_ref[:, 0:xn.shape[1]] = xn
    hx_ref[:, xn.shape[1]:] = jnp.where(lane == j1, w1, 0.0) + jnp.where(lane == j2, w2, 0.0)

    onehot = jnp.where(lane == g_sel, 1.0, 0.0)
    r_i = lax.broadcasted_iota(jnp.int32, (TM, TM), 0)
    c_i = lax.broadcasted_iota(jnp.int32, (TM, TM), 1)
    before = jnp.where(c_i < r_i, 1.0, 0.0).astype(BF16)
    prior = jnp.dot(before, onehot.astype(BF16), preferred_element_type=F32) + carry_ref[...]
    rank = jnp.sum(jnp.where(lane == g_sel, prior, 0.0), axis=-1, keepdims=True)
    carry_ref[...] += jnp.sum(onehot, axis=0, keepdims=True)
    cnt_ref[...] = carry_ref[...]
    gid_ref[0] = jnp.broadcast_to(g_sel.astype(F32), (TM, LANES)).T[0:1, :].astype(jnp.int32)
    rank_ref[0] = jnp.broadcast_to(rank, (TM, LANES)).T[0:1, :].astype(jnp.int32)


def _moe_dispatch_kernel(seg_ref, pos_ref, hx_ref, xs_ref, zero_ref, sem, zero_sem):
    TM = hx_ref.shape[0]

    @pl.when(pl.program_id(0) == 0)
    def _():
        zero_ref[...] = jnp.zeros_like(zero_ref)

        def zero_tile(t):
            fill = pltpu.make_async_copy(zero_ref, xs_ref.at[pl.ds(t * TM, TM), :], zero_sem)
            fill.start()
            fill.wait()

        for g in range(N_GROUPS):
            first = seg_ref[g - 1] if g else 0
            pl.when(seg_ref[g] > first)(functools.partial(zero_tile, seg_ref[g] - 1))
            tail = seg_ref[N_GROUPS - 1] + g
            pl.when(tail < xs_ref.shape[0] // TM)(functools.partial(zero_tile, tail))

    def start(a, carry):
        for b in range(SUBLANES):
            r = a * SUBLANES + b
            pltpu.make_async_copy(hx_ref.at[pl.ds(r, 1), :], xs_ref.at[pl.ds(pos_ref[0, 0, r], 1), :],
                                  sem).start()
        return carry

    lax.fori_loop(0, TM // SUBLANES, start, 0)
    pltpu.make_async_copy(hx_ref, xs_ref.at[pl.ds(0, TM), :], sem).wait()


def _moe_expert_kernel(grp_ref, xs_ref, wgu_ref, wd_ref, ys_ref, xn_ref, acc_ref):
    j = pl.program_id(0)
    e = pl.program_id(1)
    D = xn_ref.shape[1]
    lane = lax.broadcasted_iota(jnp.int32, (1, LANES), 1)

    @pl.when(grp_ref[j] >= 0)
    def _():
        @pl.when(e == 0)
        def _():
            xn_ref[...] = xs_ref[:, 0:D].astype(BF16)
            acc_ref[...] = jnp.zeros_like(acc_ref)

        hh = jnp.dot(xn_ref[...], wgu_ref[0], preferred_element_type=F32)
        cw = jnp.sum(jnp.where(lane == e, xs_ref[:, D:], 0.0), axis=-1, keepdims=True)
        act = _silu(hh[:, 0:EXPERT_FF]) * hh[:, EXPERT_FF:2 * EXPERT_FF] * cw
        acc_ref[...] += jnp.dot(act.astype(BF16), wd_ref[0], preferred_element_type=F32)

    last = e == EXPERTS_PER_GROUP - 1

    @pl.when(last & (grp_ref[j] >= 0))
    def _():
        ys_ref[...] = acc_ref[...]

    @pl.when(last & (grp_ref[j] < 0))
    def _():
        ys_ref[...] = jnp.zeros_like(ys_ref)


def _moe_combine_kernel(pos_ref, h_ref, ys_ref, o_ref, stage_ref, sem):
    TM = h_ref.shape[0]

    def start(a, carry):
        for b in range(SUBLANES):
            r = a * SUBLANES + b
            pltpu.make_async_copy(ys_ref.at[pl.ds(pos_ref[0, 0, r], 1), :], stage_ref.at[pl.ds(r, 1), :],
                                  sem).start()
        return carry

    lax.fori_loop(0, TM // SUBLANES, start, 0)
    pltpu.make_async_copy(ys_ref.at[pl.ds(0, TM), :], stage_ref, sem).wait()
    o_ref[...] = h_ref[...] + stage_ref[...]


def _moe(h2d, gain, w_group, b_group, w_expert, b_expert, w_gate_up, w_down):
    T, D = h2d.shape
    tr = _pick_tile(T, MOE_ROUTER_TILES)
    tm = _pick_tile(T, MOE_ROW_TILES)
    n_tiles = T // tm + N_GROUPS
    XW = D + LANES
    wr = jnp.zeros((D, 2 * LANES), F32)
    wr = wr.at[:, 0:N_EXPERTS].set(w_expert.astype(F32))
    wr = wr.at[:, LANES:LANES + N_GROUPS].set(w_group.astype(F32))
    br = jnp.zeros((1, 2 * LANES), F32)
    br = br.at[0, 0:N_EXPERTS].set(b_expert.astype(F32))
    br = br.at[0, LANES:LANES + N_GROUPS].set(b_group.astype(F32))
    wgu = w_gate_up.reshape(N_EXPERTS, D, 2 * EXPERT_FF).astype(BF16)
    wd = w_down.reshape(N_EXPERTS, EXPERT_FF, D).astype(BF16)

    hx, gid, rank, counts = pl.pallas_call(
        _moe_router_kernel,
        grid=(T // tr,),
        in_specs=[pl.BlockSpec((tr, D), lambda i: (i, 0)),
                  pl.BlockSpec((1, D), lambda i: (0, 0)),
                  pl.BlockSpec((D, 2 * LANES), lambda i: (0, 0)),
                  pl.BlockSpec((1, 2 * LANES), lambda i: (0, 0))],
        out_specs=[pl.BlockSpec((tr, XW), lambda i: (i, 0)),
                   pl.BlockSpec((1, 1, tr), lambda i: (i, 0, 0)),
                   pl.BlockSpec((1, 1, tr), lambda i: (i, 0, 0)),
                   pl.BlockSpec((1, LANES), lambda i: (0, 0))],
        out_shape=[jax.ShapeDtypeStruct((T, XW), F32),
                   jax.ShapeDtypeStruct((T // tr, 1, tr), jnp.int32),
                   jax.ShapeDtypeStruct((T // tr, 1, tr), jnp.int32),
                   jax.ShapeDtypeStruct((1, LANES), F32)],
        scratch_shapes=[pltpu.VMEM((1, LANES), F32)],
        compiler_params=_compiler_params(("arbitrary",)),
        name="moe_router",
    )(h2d, gain.reshape(1, D), wr, br)

    seg_tiles = -(-counts[0, 0:N_GROUPS].astype(jnp.int32) // tm)
    seg_end = jnp.cumsum(seg_tiles)
    seg_start = seg_end - seg_tiles
    pos = (seg_start * tm)[gid.reshape(T)] + rank.reshape(T)
    tile_ids = jnp.arange(n_tiles, dtype=jnp.int32)
    tile_group = jnp.sum((tile_ids[:, None] >= seg_end[None, :]).astype(jnp.int32), axis=1)
    tile_group = jnp.where(tile_ids < seg_end[N_GROUPS - 1], tile_group, -1)
    pos3 = pos.reshape(T // tm, 1, tm)

    xs = pl.pallas_call(
        _moe_dispatch_kernel,
        grid_spec=pltpu.PrefetchScalarGridSpec(
            num_scalar_prefetch=1,
            grid=(T // tm,),
            in_specs=[pl.BlockSpec((1, 1, tm), lambda i, seg: (i, 0, 0), memory_space=pltpu.SMEM),
                      pl.BlockSpec((tm, XW), lambda i, seg: (i, 0))],
            out_specs=pl.BlockSpec(memory_space=pl.ANY),
            scratch_shapes=[pltpu.VMEM((tm, XW), F32), pltpu.SemaphoreType.DMA(()),
                            pltpu.SemaphoreType.DMA(())]),
        out_shape=jax.ShapeDtypeStruct((n_tiles * tm, XW), F32),
        compiler_params=_compiler_params(("arbitrary",)),
        name="moe_dispatch",
    )(seg_end, pos3, hx)

    def expert_index(j, e, grp):
        return (jnp.maximum(grp[j], 0) * EXPERTS_PER_GROUP + e, 0, 0)

    ys = pl.pallas_call(
        _moe_expert_kernel,
        grid_spec=pltpu.PrefetchScalarGridSpec(
            num_scalar_prefetch=1,
            grid=(n_tiles, EXPERTS_PER_GROUP),
            in_specs=[pl.BlockSpec((tm, XW), lambda j, e, grp: (j, 0)),
                      pl.BlockSpec((1, D, 2 * EXPERT_FF), expert_index),
                      pl.BlockSpec((1, EXPERT_FF, D), expert_index)],
            out_specs=pl.BlockSpec((tm, D), lambda j, e, grp: (j, 0)),
            scratch_shapes=[pltpu.VMEM((tm, D), BF16), pltpu.VMEM((tm, D), F32)]),
        out_shape=jax.ShapeDtypeStruct((n_tiles * tm, D), F32),
        compiler_params=_compiler_params(("parallel", "arbitrary")),
        name="moe_experts",
    )(tile_group, xs, wgu, wd)

    return pl.pallas_call(
        _moe_combine_kernel,
        grid=(T // tm,),
        in_specs=[pl.BlockSpec((1, 1, tm), lambda i: (i, 0, 0), memory_space=pltpu.SMEM),
                  pl.BlockSpec((tm, D), lambda i: (i, 0)),
                  pl.BlockSpec(memory_space=pl.ANY)],
        out_specs=pl.BlockSpec((tm, D), lambda i: (i, 0)),
        out_shape=jax.ShapeDtypeStruct((T, D), F32),
        scratch_shapes=[pltpu.VMEM((tm, D), F32), pltpu.SemaphoreType.DMA(())],
        compiler_params=_compiler_params(("arbitrary",)),
        name="moe_combine",
    )(pos3, h2d, ys)


def _dsa_prep_kernel(p_ref, qg_ref, kg_ref, lng_ref, lnb_ref, q_ref, kt_ref, v_ref, qi_ref, kit_ref, wi_ref):
    lane = lax.broadcasted_iota(jnp.int32, (1, LANES), 1)
    for h in range(ATT_HEADS):
        x = p_ref[:, DSA_Q_OFF + h * ATT_HD:DSA_Q_OFF + (h + 1) * ATT_HD]
        y = x * lax.rsqrt(jnp.mean(x * x, axis=-1, keepdims=True) + RMS_EPS) * qg_ref[...]
        q_ref[:, h * ATT_HD:(h + 1) * ATT_HD] = (y * (ATT_HD ** -0.5)).astype(BF16)
    for n in range(ATT_KV_HEADS):
        x = p_ref[:, DSA_K_OFF + n * ATT_HD:DSA_K_OFF + (n + 1) * ATT_HD]
        y = x * lax.rsqrt(jnp.mean(x * x, axis=-1, keepdims=True) + RMS_EPS) * kg_ref[...]
        kt_ref[0, n * ATT_HD:(n + 1) * ATT_HD, :] = y.T.astype(BF16)
    v_ref[...] = p_ref[:, DSA_V_OFF:DSA_V_OFF + ATT_KV_DIM].astype(BF16)
    qi_ref[...] = p_ref[:, DSA_QI_OFF:DSA_QI_OFF + IDX_HEADS * LANES].astype(BF16)
    x = p_ref[:, DSA_KI_OFF:DSA_KI_OFF + LANES]
    live = lane < IDX_HD
    mu = jnp.sum(x, axis=-1, keepdims=True) * (1.0 / IDX_HD)
    xc = jnp.where(live, x - mu, 0.0)
    var = jnp.sum(xc * xc, axis=-1, keepdims=True) * (1.0 / IDX_HD)
    ki = xc * lax.rsqrt(var + RMS_EPS) * lng_ref[...] + lnb_ref[...]
    kit_ref[0] = jnp.where(live, ki, 0.0).T.astype(BF16)
    wi_ref[...] = p_ref[:, DSA_WI_OFF:DSA_WI_OFF + LANES] * (IDX_HEADS ** -0.5 * IDX_HD ** -0.5)


def _dsa_prep(proj2d, q_gain, k_gain, ln_g, ln_b):
    T, W = proj2d.shape
    tm = KEY_TILE
    assert T % tm == 0
    lng = jnp.zeros((1, LANES), F32).at[0, 0:IDX_HD].set(ln_g.astype(F32))
    lnb = jnp.zeros((1, LANES), F32).at[0, 0:IDX_HD].set(ln_b.astype(F32))
    row = lambda i: (i, 0)
    fix = lambda i: (0, 0)
    blk = lambda i: (i, 0, 0)
    return pl.pallas_call(
        _dsa_prep_kernel,
        grid=(T // tm,),
        in_specs=[pl.BlockSpec((tm, W), row),
                  pl.BlockSpec((1, ATT_HD), fix), pl.BlockSpec((1, ATT_HD), fix),
                  pl.BlockSpec((1, LANES), fix), pl.BlockSpec((1, LANES), fix)],
        out_specs=[pl.BlockSpec((tm, ATT_Q_DIM), row), pl.BlockSpec((1, ATT_KV_DIM, tm), blk),
                   pl.BlockSpec((tm, ATT_KV_DIM), row), pl.BlockSpec((tm, IDX_HEADS * LANES), row),
                   pl.BlockSpec((1, LANES, tm), blk), pl.BlockSpec((tm, LANES), row)],
        out_shape=[jax.ShapeDtypeStruct((T, ATT_Q_DIM), BF16), jax.ShapeDtypeStruct((T // tm, ATT_KV_DIM, tm), BF16),
                   jax.ShapeDtypeStruct((T, ATT_KV_DIM), BF16), jax.ShapeDtypeStruct((T, IDX_HEADS * LANES), BF16),
                   jax.ShapeDtypeStruct((T // tm, LANES, tm), BF16), jax.ShapeDtypeStruct((T, LANES), F32)],
        compiler_params=_compiler_params(("parallel",)),
        name="dsa_prep",
    )(proj2d, q_gain.reshape(1, ATT_HD), k_gain.reshape(1, ATT_HD), lng, lnb)


def _dsa_kernel(q_ref, qi_ref, wi_ref, kt_ref, v_ref, kit_ref, bias_ref, bmax_ref, o_ref,
                sc_ref, qg_ref, wib_ref, m_ref, l_ref, acc_ref, knorm_ref, *, n_select, live_rows):
    i = pl.program_id(1)
    TQ = ATT_TILE
    TK = KEY_TILE
    HALVES = TK // LANES

    @pl.when(i == 0)
    def _():
        def body(kt, mx):
            kk = kt_ref[0, kt].astype(F32)
            return tuple(jnp.maximum(mx[n], jnp.sum(jnp.square(kk[n * ATT_HD:(n + 1) * ATT_HD, :]),
                                                    axis=0, keepdims=True)) for n in range(ATT_KV_HEADS))

        mx = lax.fori_loop(0, kt_ref.shape[1], body,
                           tuple(jnp.zeros((1, TK), F32) for _ in range(ATT_KV_HEADS)))
        for n in range(ATT_KV_HEADS):
            knorm_ref[n] = jnp.broadcast_to(jnp.sqrt(jnp.max(mx[n], axis=-1, keepdims=True)), (1, LANES))

    @pl.when(i * TQ >= live_rows)
    def _():
        o_ref[...] = jnp.zeros_like(o_ref)

    @pl.when(i * TQ < live_rows)
    def _():
        _dsa_tile(i, q_ref, qi_ref, wi_ref, kt_ref, v_ref, kit_ref, bias_ref, bmax_ref, o_ref,
                  sc_ref, qg_ref, wib_ref, m_ref, l_ref, acc_ref, knorm_ref, n_select, TQ, TK, HALVES)


def _dsa_tile(i, q_ref, qi_ref, wi_ref, kt_ref, v_ref, kit_ref, bias_ref, bmax_ref, o_ref,
              sc_ref, qg_ref, wib_ref, m_ref, l_ref, acc_ref, knorm_ref, n_select, TQ, TK, HALVES):
    nkt = lax.shift_right_logical(i, 1) + 1
    fmin = float(jnp.finfo(F32).min)
    kf = float(n_select)

    q_row = i * TQ + lax.broadcasted_iota(jnp.int32, (TQ, 1), 0)
    q_chunk = lax.shift_right_logical(q_row, int(math.log2(CHUNK)))
    k_lane = lax.broadcasted_iota(jnp.int32, (1, TK), 1)

    def halves(x):
        return [x[:, j * LANES:(j + 1) * LANES] for j in range(HALVES)]

    def lane_wide(col):
        return jnp.broadcast_to(col, (TQ, LANES))

    wi = wi_ref[0]
    for h in range(IDX_HEADS):
        wib_ref[h] = jnp.broadcast_to(wi[:, h:h + 1], (TQ, TK))

    def score_tiles(kts):
        d = [[jnp.dot(qi_ref[0, :, h * LANES:(h + 1) * LANES], kit_ref[0, kt], preferred_element_type=F32)
              for h in range(IDX_HEADS)] for kt in kts]
        for t, kt in enumerate(kts):
            acc = wib_ref[0] * jnp.maximum(d[t][0], 0.0)
            for h in range(1, IDX_HEADS):
                acc = acc + wib_ref[h] * jnp.maximum(d[t][h], 0.0)
            sc_ref[kt] = acc

    def paired_loop(n, fn):
        def body(j, carry):
            fn([2 * j, 2 * j + 1])
            return carry

        lax.fori_loop(0, lax.shift_right_arithmetic(n, 1), body, 0)

        @pl.when(lax.bitwise_and(n, 1) == 1)
        def _():
            fn([n - 1])

    paired_loop(nkt, score_tiles)

    def mask_inadmissible(kt):
        k_pos = kt * TK + k_lane
        adm = (lax.shift_right_logical(k_pos, int(math.log2(CHUNK))) <= q_chunk) & (k_pos >= META_PAD)
        sc_ref[kt] = jnp.where(adm, sc_ref[kt], NEG_INF)

    mask_inadmissible(0)

    @pl.when(nkt > 1)
    def _():
        mask_inadmissible(nkt - 1)

    n_adm = CHUNK * (q_chunk + 1) - META_PAD
    need = n_adm > n_select

    def row_sum(x):
        return jnp.sum(x, axis=-1, keepdims=True)

    def count_ge(t):
        tb = lane_wide(t)

        def body(kt, acc):
            for s in halves(sc_ref[kt]):
                acc = acc + jnp.where(s >= tb, 1.0, 0.0)
            return acc

        return row_sum(lax.fori_loop(0, nkt, body, jnp.zeros((TQ, LANES), F32)))

    def minmax_body(kt, carry):
        mn, mx = carry
        for s in halves(sc_ref[kt]):
            mn = jnp.minimum(mn, jnp.where(s == NEG_INF, jnp.inf, s))
            mx = jnp.maximum(mx, s)
        return mn, mx

    mn_l, mx_l = lax.fori_loop(0, nkt, minmax_body,
                               (jnp.full((TQ, LANES), jnp.inf, F32), jnp.full((TQ, LANES), NEG_INF, F32)))
    s_min = jnp.min(mn_l, axis=-1, keepdims=True)
    s_max = jnp.max(mx_l, axis=-1, keepdims=True)
    lo0 = s_min
    hi0 = s_max + (s_max - s_min) + 1.0
    cnt0 = n_adm.astype(F32)
    active0 = jnp.where(need, 1.0, 0.0)

    def any_active(active):
        return jnp.max(active) > 0.0

    def bis_step(lo, hi, cnt, active):
        mid = lo + (hi - lo) * 0.5
        c = count_ge(mid)
        ge = c >= kf
        moving = (mid > lo) & (mid < hi) & (active > 0.0)
        lo_n = jnp.where(moving & ge, mid, lo)
        hi_n = jnp.where(moving & (~ge), mid, hi)
        cnt_n = jnp.where(moving & ge, c, cnt)
        active_n = jnp.where(moving & (cnt_n != kf), 1.0, 0.0)
        return lo_n, hi_n, cnt_n, active_n

    def bis_body(st):
        it, rest = st[0], st[1:]
        for _ in range(BISECT_UNROLL):
            rest = bis_step(*rest)
        return (it + BISECT_UNROLL,) + rest

    _, lo, hi, cnt, _ = lax.while_loop(
        lambda st: jnp.logical_and(st[0] < BISECT_STEPS, any_active(st[4])), bis_body,
        (jnp.int32(0), lo0, hi0, cnt0, active0))

    def snap_body(st):
        lo, hi, cnt, active = st
        hb = lane_wide(hi)

        def body(kt, mx):
            for s in halves(sc_ref[kt]):
                mx = jnp.maximum(mx, jnp.where(s < hb, s, NEG_INF))
            return mx

        v1 = jnp.max(lax.fori_loop(0, nkt, body, jnp.full((TQ, LANES), NEG_INF, F32)),
                     axis=-1, keepdims=True)
        c1 = count_ge(v1)
        on = active > 0.0
        hit = on & (c1 >= kf)
        lo_n = jnp.where(hit, v1, lo)
        cnt_n = jnp.where(hit, c1, cnt)
        hi_n = jnp.where(on & (~hit), v1, hi)
        return lo_n, hi_n, cnt_n, jnp.where(on & (~hit), 1.0, 0.0)

    unsettled = jnp.where(need & (cnt != kf), 1.0, 0.0)
    lo, hi, cnt, _ = lax.while_loop(lambda st: any_active(st[3]), snap_body, (lo, hi, cnt, unsettled))

    thr = jnp.where(need, lo, fmin)
    tied = need & (cnt > kf)
    big_pos = jnp.int32(2 ** 30)

    def tie_break():
        tb = lane_wide(thr)

        def count_pred(pred):
            def body(kt, acc):
                pos = kt * TK + k_lane
                for j, s in enumerate(halves(sc_ref[kt])):
                    acc = acc + jnp.where(pred(s, pos[:, j * LANES:(j + 1) * LANES]), 1.0, 0.0)
                return acc
            return row_sum(lax.fori_loop(0, nkt, body, jnp.zeros((TQ, LANES), F32)))

        want = kf - count_pred(lambda s, pos: s > tb)

        def body(_, st):
            jl, jh = st
            jm = lax.shift_right_arithmetic(jl + jh, 1)
            c = count_pred(lambda s, pos: (s == tb) & (pos <= jm))
            ok = c >= want
            return jnp.where(ok, jl, jm), jnp.where(ok, jm, jh)

        n_keys = sc_ref.shape[0] * TK
        steps = int(math.ceil(math.log2(n_keys + 1))) + 1
        jl0 = jnp.full((TQ, 1), -1, jnp.int32)
        jh0 = jnp.full((TQ, 1), n_keys, jnp.int32)
        _, jh = lax.fori_loop(0, steps, body, (jl0, jh0))
        return jnp.where(tied, jh, big_pos)

    pos_cap = lax.cond(jnp.max(jnp.where(tied, 1.0, 0.0)) > 0.0, tie_break,
                       lambda: jnp.full((TQ, 1), big_pos, jnp.int32))

    def to_mask(kt, carry):
        s = sc_ref[kt]
        pos = kt * TK + k_lane
        sel = (s > thr) | ((s == thr) & (pos <= pos_cap))
        sc_ref[kt] = jnp.where(sel, 0.0, NEG_INF)
        return carry

    lax.fori_loop(0, nkt, to_mask, 0)

    for n in range(ATT_KV_HEADS):
        for g in range(ATT_GROUP):
            h = n * ATT_GROUP + g
            qg_ref[n, g * TQ:(g + 1) * TQ, :] = q_ref[0, :, h * ATT_HD:(h + 1) * ATT_HD]
    l_ref[...] = jnp.zeros_like(l_ref)
    acc_ref[...] = jnp.zeros_like(acc_ref)

    kv_heads = range(ATT_KV_HEADS)

    def raw_logits(kts):
        return [[jnp.dot(qg_ref[n], kt_ref[0, kt, n * ATT_HD:(n + 1) * ATT_HD, :], preferred_element_type=F32)
                 for n in kv_heads] for kt in kts]

    def masked_logits(s, mask, n, g, b0):
        sg = s[g * TQ:(g + 1) * TQ, :] + mask
        if b0 is not None:
            h = n * ATT_GROUP + g
            sg = sg + jnp.concatenate([bias_ref[h, b0 + j] for j in range(HALVES)], axis=1)
        return sg

    def max_pass(kts, b0):
        raw = raw_logits(kts)
        masks = [sc_ref[kt] for kt in kts]
        for n in kv_heads:
            for g in range(ATT_GROUP):
                rows = slice(g * TQ, (g + 1) * TQ)
                m = m_ref[n, rows, :]
                for t in range(len(kts)):
                    for part in halves(masked_logits(raw[t][n], masks[t], n, g, b0)):
                        m = jnp.maximum(m, part)
                m_ref[n, rows, :] = m

    def exp_pass(kts, b0):
        raw = raw_logits(kts)
        masks = [sc_ref[kt] for kt in kts]
        for n in kv_heads:
            ps = [[] for _ in kts]
            for g in range(ATT_GROUP):
                rows = slice(g * TQ, (g + 1) * TQ)
                m = m_ref[n, rows, :]
                l_new = l_ref[n, rows, :]
                for t in range(len(kts)):
                    parts = [jnp.exp(part - m) for part in halves(masked_logits(raw[t][n], masks[t], n, g, b0))]
                    l_new = l_new + functools.reduce(lambda a, b: a + b, parts)
                    ps[t].append(jnp.concatenate(parts, axis=1).astype(BF16))
                l_ref[n, rows, :] = l_new
            pv = None
            for t, kt in enumerate(kts):
                vv = v_ref[0, pl.ds(pl.multiple_of(kt * TK, TK), TK), n * ATT_HD:(n + 1) * ATT_HD]
                d = jnp.dot(jnp.concatenate(ps[t], axis=0), vv, preferred_element_type=F32)
                pv = d if pv is None else pv + d
            acc_ref[n] += pv

    n_far = lax.shift_right_arithmetic(i - 1, 1)
    parity = lax.bitwise_and(i, 1)

    def run_pass(tile_fn):
        paired_loop(jnp.maximum(n_far, 0), lambda kts: tile_fn(kts, None))

        @pl.when(i >= 1)
        def _():
            tile_fn([n_far], parity)

        @pl.when(parity == 0)
        def _():
            tile_fn([lax.shift_right_logical(i, 1)], 2)

    for n in range(ATT_KV_HEADS):
        for g in range(ATT_GROUP):
            h = n * ATT_GROUP + g
            qh = qg_ref[n, g * TQ:(g + 1) * TQ, :].astype(F32)
            q_len = jnp.sqrt(jnp.sum(qh * qh, axis=-1, keepdims=True))
            bound = q_len * knorm_ref[n] * SHIFT_SLACK + (bmax_ref[h] + SHIFT_SLACK - 1.0)
            m_ref[n, g * TQ:(g + 1) * TQ, :] = jnp.broadcast_to(bound, (TQ, LANES))
    run_pass(exp_pass)

    def smallest_row_sum():
        return functools.reduce(jnp.minimum, [jnp.min(jnp.sum(l_ref[n], axis=-1, keepdims=True))
                                              for n in range(ATT_KV_HEADS)])

    @pl.when(jnp.logical_not(smallest_row_sum() > ROW_SUM_FLOOR))
    def _():
        m_ref[...] = jnp.full(m_ref.shape, -1e30, F32)
        run_pass(max_pass)
        for n in range(ATT_KV_HEADS):
            m_ref[n] = jnp.broadcast_to(jnp.max(m_ref[n], axis=-1, keepdims=True), m_ref.shape[1:])
        l_ref[...] = jnp.zeros_like(l_ref)
        acc_ref[...] = jnp.zeros_like(acc_ref)
        run_pass(exp_pass)

    for n in range(ATT_KV_HEADS):
        inv_l = 1.0 / jnp.sum(l_ref[n], axis=-1, keepdims=True)
        for g in range(ATT_GROUP):
            h = n * ATT_GROUP + g
            rows = slice(g * TQ, (g + 1) * TQ)
            o_ref[0, :, h * ATT_HD:(h + 1) * ATT_HD] = (acc_ref[n, rows, :] * inv_l[rows, :]).astype(o_ref.dtype)


def _t5_bucket(rel):
    nb = REL_BUCKETS // 2
    max_exact = nb // 2
    ret = jnp.where(rel > 0, nb, 0)
    n = jnp.abs(rel)
    nf = jnp.maximum(n, 1).astype(jnp.float32)
    large = max_exact + (jnp.log(nf / max_exact) / math.log(REL_MAX_DIST / max_exact)
                         * (nb - max_exact)).astype(jnp.int32)
    large = jnp.minimum(large, nb - 1)
    return ret + jnp.where(n < max_exact, n, large)


def _dsa_core(q, kt, v, qi, kit, wi, rel_bias, n_select, live_rows):
    B, LP, _ = q.shape
    nq = LP // ATT_TILE
    nkt = LP // KEY_TILE
    a = jnp.arange(ATT_TILE)[:, None]
    xx = jnp.arange(2 * KEY_TILE)[None, :]
    rel = (xx - KEY_TILE) - a
    far = rel_bias.astype(F32)[REL_BUCKETS // 2 - 1]
    onehot = jax.nn.one_hot(_t5_bucket(rel), REL_BUCKETS, dtype=F32)
    table = jnp.einsum("abk,kh->abh", onehot, rel_bias.astype(F32),
                       precision=lax.Precision.HIGHEST) - far
    bias_near = jnp.transpose(table.reshape(ATT_TILE, 2 * KEY_TILE // LANES, LANES, ATT_HEADS), (3, 1, 0, 2))
    bias_max = jnp.broadcast_to(jnp.maximum(jnp.max(table, axis=(0, 1)), 0.0)[:, None, None],
                                (ATT_HEADS, 1, LANES))
    kern = functools.partial(_dsa_kernel, n_select=n_select, live_rows=live_rows)
    tile = lambda b, i: (b, i, 0)
    whole3 = lambda b, i: (b, 0, 0)
    whole4 = lambda b, i: (b, 0, 0, 0)
    return pl.pallas_call(
        kern,
        grid=(B, nq),
        in_specs=[pl.BlockSpec((1, ATT_TILE, ATT_Q_DIM), tile),
                  pl.BlockSpec((1, ATT_TILE, IDX_HEADS * LANES), tile),
                  pl.BlockSpec((1, ATT_TILE, LANES), tile),
                  pl.BlockSpec((1, nkt, ATT_KV_DIM, KEY_TILE), whole4),
                  pl.BlockSpec((1, LP, ATT_KV_DIM), whole3),
                  pl.BlockSpec((1, nkt, LANES, KEY_TILE), whole4),
                  pl.BlockSpec((ATT_HEADS, 2 * KEY_TILE // LANES, ATT_TILE, LANES), lambda b, i: (0, 0, 0, 0)),
                  pl.BlockSpec((ATT_HEADS, 1, LANES), lambda b, i: (0, 0, 0))],
        out_specs=pl.BlockSpec((1, ATT_TILE, ATT_Q_DIM), tile),
        out_shape=jax.ShapeDtypeStruct((B, LP, ATT_Q_DIM), BF16),
        scratch_shapes=[pltpu.VMEM((nkt, ATT_TILE, KEY_TILE), F32),
                        pltpu.VMEM((ATT_KV_HEADS, ATT_GROUP * ATT_TILE, ATT_HD), BF16),
                        pltpu.VMEM((IDX_HEADS, ATT_TILE, KEY_TILE), F32),
                        pltpu.VMEM((ATT_KV_HEADS, ATT_GROUP * ATT_TILE, LANES), F32),
                        pltpu.VMEM((ATT_KV_HEADS, ATT_GROUP * ATT_TILE, LANES), F32),
                        pltpu.VMEM((ATT_KV_HEADS, ATT_GROUP * ATT_TILE, ATT_HD), F32),
                        pltpu.VMEM((ATT_KV_HEADS, 1, LANES), F32)],
        compiler_params=_compiler_params(("arbitrary", "arbitrary")),
        name="dsa_core",
    )(q, qi, wi, kt, v, kit, bias_near, bias_max)


def _gdn_in_weight(w_in):
    D = w_in.shape[0]
    w = jnp.zeros((D, GDN_PROJ_W), F32).at[:, 0:w_in.shape[1]].set(w_in.astype(F32))
    return w.astype(BF16)


def _dsa_in_weight(w_in):
    D = w_in.shape[0]
    w_in = w_in.astype(BF16)
    zeros = lambda n: jnp.zeros((D, n), BF16)
    src = ATT_Q_DIM + 2 * ATT_KV_DIM
    parts = [w_in[:, 0:src]]
    for h in range(IDX_HEADS):
        parts += [w_in[:, src + h * IDX_HD:src + (h + 1) * IDX_HD], zeros(LANES - IDX_HD)]
    src += IDX_HEADS * IDX_HD
    parts += [w_in[:, src:src + IDX_HD], zeros(LANES - IDX_HD),
              w_in[:, src + IDX_HD:src + IDX_HD + IDX_HEADS], zeros(LANES - IDX_HEADS)]
    w = jnp.concatenate(parts, axis=1)
    assert w.shape[1] == DSA_PROJ_W
    return w


def kernel(x, meta_tokens, norm_mix, norm_ffn, rel_bias, gdn_w_in, gdn_conv, gdn_a_log, gdn_dt_bias,
           gdn_o_norm, gdn_w_out, dsa_w_in, dsa_q_norm, dsa_k_norm, dsa_idx_ln_g, dsa_idx_ln_b, dsa_w_out,
           moe_w_group, moe_b_group, moe_w_expert, moe_b_expert, moe_w_gate_up, moe_w_down):
    B, S, D = x.shape
    depth = norm_mix.shape[0]
    n_select = min(TOPK_MAX, S // 4)
    off = META_PAD + N_META
    LP = -(-(off + S) // ROW_ALIGN) * ROW_ALIGN
    meta = jnp.broadcast_to(meta_tokens.astype(x.dtype)[None], (B, N_META, D))
    h = jnp.concatenate([jnp.zeros((B, META_PAD, D), x.dtype), meta, x,
                         jnp.zeros((B, LP - off - S, D), x.dtype)], axis=1)
    h = h.reshape(B * LP, D)
    for i in range(depth):
        j = i // 2
        if i % 2 == 0:
            proj = _norm_matmul(h, norm_mix[i], _gdn_in_weight(gdn_w_in[j]))
            o = _gdn_core(proj.reshape(B, LP, GDN_PROJ_W), gdn_conv[j].astype(F32), gdn_a_log[j],
                          gdn_dt_bias[j], gdn_o_norm[j])
            h = _matmul_residual(o.reshape(B * LP, GDN_V_DIM), gdn_w_out[j].astype(BF16), h)
        else:
            proj = _norm_matmul(h, norm_mix[i], _dsa_in_weight(dsa_w_in[j]))
            q, kt, v, qi, kit, wi = _dsa_prep(proj, dsa_q_norm[j], dsa_k_norm[j], dsa_idx_ln_g[j], dsa_idx_ln_b[j])
            r3 = lambda t: t.reshape(B, LP, t.shape[-1])
            r4 = lambda t: t.reshape(B, LP // KEY_TILE, t.shape[-2], KEY_TILE)
            o = _dsa_core(r3(q), r4(kt), r3(v), r3(qi), r4(kit), r3(wi), rel_bias, n_select, off + S)
            h = _matmul_residual(o.reshape(B * LP, ATT_Q_DIM), dsa_w_out[j].astype(BF16), h)
        h = _moe(h, norm_ffn[i], moe_w_group[i], moe_b_group[i], moe_w_expert[i], moe_b_expert[i],
                 moe_w_gate_up[i], moe_w_down[i])
    return h.reshape(B, LP, D)[:, off:off + S]
```

```python
import functools
import math

import jax
import jax.numpy as jnp
from jax import lax
from jax.experimental import pallas as pl
from jax.experimental.pallas import tpu as pltpu

F32 = jnp.float32
BF16 = jnp.bfloat16

LANES = 128
SUBLANES = 8
VMEM_LIMIT_BYTES = 56 * 1024 * 1024

CHUNK = 64
N_META = 16
META_PAD = (-N_META) % CHUNK
RMS_EPS = 1e-6
L2_EPS = 1e-6

GDN_HEADS = 8
GDN_DK = 128
GDN_DV = 128
GDN_CONV = 4
GDN_QK_DIM = GDN_HEADS * GDN_DK
GDN_V_DIM = GDN_HEADS * GDN_DV
GDN_CONV_DIM = 2 * GDN_QK_DIM + GDN_V_DIM
GDN_GATE_OFF = GDN_CONV_DIM + GDN_V_DIM
GDN_PROJ_W = GDN_GATE_OFF + LANES

ATT_HEADS = 8
ATT_KV_HEADS = 2
ATT_GROUP = ATT_HEADS // ATT_KV_HEADS
ATT_HD = 128
ATT_Q_DIM = ATT_HEADS * ATT_HD
ATT_KV_DIM = ATT_KV_HEADS * ATT_HD
IDX_HEADS = 8
IDX_HD = 64
TOPK_MAX = 256
REL_BUCKETS = 32
REL_MAX_DIST = 128

DSA_Q_OFF = 0
DSA_K_OFF = ATT_Q_DIM
DSA_V_OFF = DSA_K_OFF + ATT_KV_DIM
DSA_QI_OFF = DSA_V_OFF + ATT_KV_DIM
DSA_KI_OFF = DSA_QI_OFF + IDX_HEADS * LANES
DSA_WI_OFF = DSA_KI_OFF + LANES
DSA_PROJ_W = DSA_WI_OFF + LANES

N_GROUPS = 4
EXPERTS_PER_GROUP = 8
N_EXPERTS = N_GROUPS * EXPERTS_PER_GROUP
EXPERT_FF = 256

MOE_ROUTER_TILES = (512, 256, 128)
MOE_ROW_TILES = (512, 256, 128)

ATT_TILE = 128
KEY_TILE = 256
ROW_ALIGN = KEY_TILE
BISECT_UNROLL = 4
BISECT_STEPS = 24
SHIFT_SLACK = 1.001
ROW_SUM_FLOOR = 1e-30
NEG_INF = float("-inf")


def _pick_tile(n, candidates):
    for c in candidates:
        if n % c == 0:
            return c
    raise ValueError(f"no tile for {n}")


def _compiler_params(semantics):
    return pltpu.CompilerParams(dimension_semantics=semantics, vmem_limit_bytes=VMEM_LIMIT_BYTES)


def _silu(x):
    return x * jax.nn.sigmoid(x)


def _norm_matmul_kernel(x_ref, g_ref, w_ref, o_ref):
    x = x_ref[...]
    y = x * lax.rsqrt(jnp.mean(x * x, axis=-1, keepdims=True) + RMS_EPS) * g_ref[...]
    o_ref[...] = jnp.dot(y.astype(BF16), w_ref[...], preferred_element_type=F32)


def _norm_matmul(h2d, gain, w_bf16):
    T, D = h2d.shape
    N = w_bf16.shape[1]
    tm = _pick_tile(T, (256, 128))
    return pl.pallas_call(
        _norm_matmul_kernel,
        grid=(T // tm,),
        in_specs=[pl.BlockSpec((tm, D), lambda i: (i, 0)),
                  pl.BlockSpec((1, D), lambda i: (0, 0)),
                  pl.BlockSpec((D, N), lambda i: (0, 0))],
        out_specs=pl.BlockSpec((tm, N), lambda i: (i, 0)),
        out_shape=jax.ShapeDtypeStruct((T, N), F32),
        compiler_params=_compiler_params(("parallel",)),
        name="norm_matmul",
    )(h2d, gain.reshape(1, D), w_bf16)


def _matmul_residual_kernel(a_ref, w_ref, r_ref, o_ref):
    o_ref[...] = r_ref[...] + jnp.dot(a_ref[...], w_ref[...], preferred_element_type=F32)


def _matmul_residual(a_bf16, w_bf16, res):
    T, K = a_bf16.shape
    N = w_bf16.shape[1]
    tm = _pick_tile(T, (512, 256, 128))
    return pl.pallas_call(
        _matmul_residual_kernel,
        grid=(T // tm,),
        in_specs=[pl.BlockSpec((tm, K), lambda i: (i, 0)),
                  pl.BlockSpec((K, N), lambda i: (0, 0)),
                  pl.BlockSpec((tm, N), lambda i: (i, 0))],
        out_specs=pl.BlockSpec((tm, N), lambda i: (i, 0)),
        out_shape=jax.ShapeDtypeStruct((T, N), F32),
        compiler_params=_compiler_params(("parallel",)),
        name="matmul_residual",
    )(a_bf16, w_bf16, res)


def _nt_dot(a, b):
    return lax.dot_general(a, b, (((1,), (1,)), ((), ())), preferred_element_type=F32)


def _tn_dot(a, b):
    return lax.dot_general(a, b, (((0,), (0,)), ((), ())), preferred_element_type=F32)


def _gdn_kernel(proj_ref, conv_ref, gate_ref, ogain_ref, o_ref, state_ref, xbuf_ref, qkv_ref):
    c = pl.program_id(1)
    C = CHUNK

    @pl.when(c == 0)
    def _():
        state_ref[...] = jnp.zeros_like(state_ref)
        xbuf_ref[0:SUBLANES, :] = jnp.zeros((SUBLANES, GDN_CONV_DIM), F32)

    row = c * C + lax.broadcasted_iota(jnp.int32, (C, 1), 0)
    live = row >= META_PAD
    u = jnp.where(live, proj_ref[0, :, 0:GDN_CONV_DIM], 0.0)
    xbuf_ref[SUBLANES:SUBLANES + C, :] = u
    conv = conv_ref[0:1, :] * xbuf_ref[SUBLANES - 3:SUBLANES - 3 + C, :]
    for j in range(1, GDN_CONV):
        conv = conv + conv_ref[j:j + 1, :] * xbuf_ref[SUBLANES - 3 + j:SUBLANES - 3 + j + C, :]
    xbuf_ref[0:SUBLANES, :] = u[C - SUBLANES:C, :]
    qkv_ref[...] = _silu(conv)
    qkv = qkv_ref

    gates = jnp.where(live, proj_ref[0, :, GDN_GATE_OFF:GDN_GATE_OFF + LANES], 0.0)
    beta = jax.nn.sigmoid(gates)
    neg_rate = -jnp.exp(gate_ref[0:1, :])
    g = neg_rate * jax.nn.softplus(gates + gate_ref[1:2, :])
    r_i = lax.broadcasted_iota(jnp.int32, (C, C), 0)
    c_i = lax.broadcasted_iota(jnp.int32, (C, C), 1)
    causal = c_i <= r_i
    strict = c_i < r_i
    gc = jnp.dot(causal.astype(F32), g, preferred_element_type=F32,
                 precision=lax.Precision.HIGHEST)
    gc_t = gc.T
    eg = jnp.exp(gc)
    g_last = gc[C - 1:C, :]
    eg_last = jnp.exp(g_last)
    ek = jnp.exp(g_last - gc)

    heads = range(GDN_HEADS)
    dot = functools.partial(jnp.dot, preferred_element_type=F32)
    col = lambda a, h: a[:, GDN_HEADS + h:GDN_HEADS + h + 1]
    qs, ks, k16s, xs, decays = [], [], [], [], []
    for h in heads:
        lo = h * GDN_DK
        q = qkv[:, lo:lo + GDN_DK]
        k = qkv[:, GDN_QK_DIM + lo:GDN_QK_DIM + lo + GDN_DK]
        v = qkv[:, 2 * GDN_QK_DIM + h * GDN_DV:2 * GDN_QK_DIM + (h + 1) * GDN_DV]
        q = q * lax.rsqrt(jnp.sum(q * q, axis=-1, keepdims=True) + L2_EPS) * (GDN_DK ** -0.5)
        k = k * lax.rsqrt(jnp.sum(k * k, axis=-1, keepdims=True) + L2_EPS)
        kb = k * beta[:, h:h + 1]
        vb = v * beta[:, h:h + 1]
        qs.append(q)
        ks.append(k)
        k16s.append(k.astype(BF16))
        xs.append((kb, jnp.concatenate([vb, kb * col(eg, h)], axis=1)))
        decays.append(jnp.exp(jnp.where(causal, col(gc, h) - gc_t[GDN_HEADS + h:GDN_HEADS + h + 1, :],
                                        NEG_INF)))
    kk = [_nt_dot(xs[h][0].astype(BF16), k16s[h]) for h in heads]
    qk = [_nt_dot(qs[h].astype(BF16), k16s[h]) for h in heads]
    ns = [-jnp.where(strict, kk[h] * decays[h], 0.0) for h in heads]
    xs = [x for _, x in xs]
    for it in range(6):
        n16 = [n.astype(BF16) for n in ns]
        xs = [xs[h] + dot(n16[h], xs[h].astype(BF16)) for h in heads]
        if it < 5:
            ns = [dot(n16[h], n16[h]) for h in heads]
    qk = [jnp.where(causal, qk[h] * decays[h], 0.0).astype(BF16) for h in heads]
    states = [state_ref[h] for h in heads]
    s16 = [s.astype(BF16) for s in states]
    ws = [dot(xs[h][:, GDN_DV:2 * GDN_DV].astype(BF16), s16[h]) for h in heads]
    qs_s = [dot((qs[h] * col(eg, h)).astype(BF16), s16[h]) for h in heads]
    v_new = [(xs[h][:, 0:GDN_DV] - ws[h]).astype(BF16) for h in heads]
    os_ = [qs_s[h] + dot(qk[h], v_new[h]) for h in heads]
    kv = [_tn_dot((ks[h] * col(ek, h)).astype(BF16), v_new[h]) for h in heads]
    for h in heads:
        state_ref[h] = states[h] * col(eg_last, h) + kv[h]
        z = proj_ref[0, :, GDN_CONV_DIM + h * GDN_DV:GDN_CONV_DIM + (h + 1) * GDN_DV]
        o = os_[h]
        on = o * lax.rsqrt(jnp.mean(o * o, axis=-1, keepdims=True) + RMS_EPS) * ogain_ref[...]
        o_ref[0, :, h * GDN_DV:(h + 1) * GDN_DV] = (on * _silu(z)).astype(o_ref.dtype)


def _gdn_core(proj, conv_w, a_log, dt_bias, o_gain):
    B, LP, W = proj.shape
    nc = LP // CHUNK
    gate = jnp.zeros((2, LANES), F32)
    gate = gate.at[0, GDN_HEADS:2 * GDN_HEADS].set(a_log.astype(F32))
    gate = gate.at[1, GDN_HEADS:2 * GDN_HEADS].set(dt_bias.astype(F32))
    return pl.pallas_call(
        _gdn_kernel,
        grid=(B, nc),
        in_specs=[pl.BlockSpec((1, CHUNK, W), lambda b, c: (b, c, 0)),
                  pl.BlockSpec((GDN_CONV, GDN_CONV_DIM), lambda b, c: (0, 0)),
                  pl.BlockSpec((2, LANES), lambda b, c: (0, 0)),
                  pl.BlockSpec((1, GDN_DV), lambda b, c: (0, 0))],
        out_specs=pl.BlockSpec((1, CHUNK, GDN_V_DIM), lambda b, c: (b, c, 0)),
        out_shape=jax.ShapeDtypeStruct((B, LP, GDN_V_DIM), BF16),
        scratch_shapes=[pltpu.VMEM((GDN_HEADS, GDN_DK, GDN_DV), F32),
                        pltpu.VMEM((SUBLANES + CHUNK, GDN_CONV_DIM), F32),
                        pltpu.VMEM((CHUNK, GDN_CONV_DIM), F32)],
        compiler_params=_compiler_params(("parallel", "arbitrary")),
        name="gdn_core",
    )(proj, conv_w, gate, o_gain.reshape(1, GDN_DV))


def _moe_router_kernel(h_ref, gain_ref, wr_ref, br_ref, hx_ref, gid_ref, rank_ref, cnt_ref, carry_ref):
    i = pl.program_id(0)
    TM = h_ref.shape[0]
    lane = lax.broadcasted_iota(jnp.int32, (1, LANES), 1)

    @pl.when(i == 0)
    def _():
        carry_ref[...] = jnp.zeros_like(carry_ref)

    x = h_ref[...]
    xn = x * lax.rsqrt(jnp.mean(x * x, axis=-1, keepdims=True) + RMS_EPS) * gain_ref[...]
    logits = jnp.dot(xn, wr_ref[...], preferred_element_type=F32,
                     precision=lax.Precision.HIGHEST) + br_ref[...]
    e_log = logits[:, 0:LANES]
    g_log = jnp.where(lane < N_GROUPS, logits[:, LANES:2 * LANES], NEG_INF)
    g_max = jnp.max(g_log, axis=-1, keepdims=True)
    g_sel = jnp.min(jnp.where(g_log == g_max, lane, LANES), axis=-1, keepdims=True)
    g_w = 1.0 / jnp.sum(jnp.exp(g_log - g_max), axis=-1, keepdims=True)
    in_grp = (lane >= g_sel * EXPERTS_PER_GROUP) & (lane < (g_sel + 1) * EXPERTS_PER_GROUP)
    l0 = jnp.where(in_grp, e_log, NEG_INF)
    m1 = jnp.max(l0, axis=-1, keepdims=True)
    i1 = jnp.min(jnp.where(l0 == m1, lane, LANES), axis=-1, keepdims=True)
    l1 = jnp.where(lane == i1, NEG_INF, l0)
    m2 = jnp.max(l1, axis=-1, keepdims=True)
    i2 = jnp.min(jnp.where(l1 == m2, lane, LANES), axis=-1, keepdims=True)
    p2 = jnp.exp(m2 - m1)
    w1 = g_w / (1.0 + p2)
    w2 = g_w * p2 / (1.0 + p2)
    j1 = i1 - g_sel * EXPERTS_PER_GROUP
    j2 = i2 - g_sel * EXPERTS_PER_GROUP
    hx_ref[:, 0:xn.shape[1]] = xn
    hx_ref[:, xn.shape[1]:] = jnp.where(lane == j1, w1, 0.0) + jnp.where(lane == j2, w2, 0.0)

    onehot = jnp.where(lane == g_sel, 1.0, 0.0)
    r_i = lax.broadcasted_iota(jnp.int32, (TM, TM), 0)
    c_i = lax.broadcasted_iota(jnp.int32, (TM, TM), 1)
    before = jnp.where(c_i < r_i, 1.0, 0.0).astype(BF16)
    prior = jnp.dot(before, onehot.astype(BF16), preferred_element_type=F32) + carry_ref[...]
    rank = jnp.sum(jnp.where(lane == g_sel, prior, 0.0), axis=-1, keepdims=True)
    carry_ref[...] += jnp.sum(onehot, axis=0, keepdims=True)
    cnt_ref[...] = carry_ref[...]
    gid_ref[0] = jnp.broadcast_to(g_sel.astype(F32), (TM, LANES)).T[0:1, :].astype(jnp.int32)
    rank_ref[0] = jnp.broadcast_to(rank, (TM, LANES)).T[0:1, :].astype(jnp.int32)


def _moe_dispatch_kernel(seg_ref, pos_ref, hx_ref, xs_ref, zero_ref, sem, zero_sem):
    TM = hx_ref.shape[0]

    @pl.when(pl.program_id(0) == 0)
    def _():
        zero_ref[...] = jnp.zeros_like(zero_ref)

        def zero_tile(t):
            fill = pltpu.make_async_copy(zero_ref, xs_ref.at[pl.ds(t * TM, TM), :], zero_sem)
            fill.start()
            fill.wait()

        for g in range(N_GROUPS):
            first = seg_ref[g - 1] if g else 0
            pl.when(seg_ref[g] > first)(functools.partial(zero_tile, seg_ref[g] - 1))
            tail = seg_ref[N_GROUPS - 1] + g
            pl.when(tail < xs_ref.shape[0] // TM)(functools.partial(zero_tile, tail))

    def start(a, carry):
        for b in range(SUBLANES):
            r = a * SUBLANES + b
            pltpu.make_async_copy(hx_ref.at[pl.ds(r, 1), :], xs_ref.at[pl.ds(pos_ref[0, 0, r], 1), :],
                                  sem).start()
        return carry

    lax.fori_loop(0, TM // SUBLANES, start, 0)
    pltpu.make_async_copy(hx_ref, xs_ref.at[pl.ds(0, TM), :], sem).wait()


def _moe_expert_kernel(grp_ref, xs_ref, wgu_ref, wd_ref, ys_ref, xn_ref, acc_ref):
    j = pl.program_id(0)
    e = pl.program_id(1)
    D = xn_ref.shape[1]
    lane = lax.broadcasted_iota(jnp.int32, (1, LANES), 1)

    @pl.when(grp_ref[j] >= 0)
    def _():
        @pl.when(e == 0)
        def _():
            xn_ref[...] = xs_ref[:, 0:D].astype(BF16)
            acc_ref[...] = jnp.zeros_like(acc_ref)

        hh = jnp.dot(xn_ref[...], wgu_ref[0], preferred_element_type=F32)
        cw = jnp.sum(jnp.where(lane == e, xs_ref[:, D:], 0.0), axis=-1, keepdims=True)
        act = _silu(hh[:, 0:EXPERT_FF]) * hh[:, EXPERT_FF:2 * EXPERT_FF] * cw
        acc_ref[...] += jnp.dot(act.astype(BF16), wd_ref[0], preferred_element_type=F32)

    last = e == EXPERTS_PER_GROUP - 1

    @pl.when(last & (grp_ref[j] >= 0))
    def _():
        ys_ref[...] = acc_ref[...]

    @pl.when(last & (grp_ref[j] < 0))
    def _():
        ys_ref[...] = jnp.zeros_like(ys_ref)


def _moe_combine_kernel(pos_ref, h_ref, ys_ref, o_ref, stage_ref, sem):
    TM = h_ref.shape[0]

    def start(a, carry):
        for b in range(SUBLANES):
            r = a * SUBLANES + b
            pltpu.make_async_copy(ys_ref.at[pl.ds(pos_ref[0, 0, r], 1), :], stage_ref.at[pl.ds(r, 1), :],
                                  sem).start()
        return carry

    lax.fori_loop(0, TM // SUBLANES, start, 0)
    pltpu.make_async_copy(ys_ref.at[pl.ds(0, TM), :], stage_ref, sem).wait()
    o_ref[...] = h_ref[...] + stage_ref[...]


def _moe(h2d, gain, w_group, b_group, w_expert, b_expert, w_gate_up, w_down):
    T, D = h2d.shape
    tr = _pick_tile(T, MOE_ROUTER_TILES)
    tm = _pick_tile(T, MOE_ROW_TILES)
    n_tiles = T // tm + N_GROUPS
    XW = D + LANES
    wr = jnp.zeros((D, 2 * LANES), F32)
    wr = wr.at[:, 0:N_EXPERTS].set(w_expert.astype(F32))
    wr = wr.at[:, LANES:LANES + N_GROUPS].set(w_group.astype(F32))
    br = jnp.zeros((1, 2 * LANES), F32)
    br = br.at[0, 0:N_EXPERTS].set(b_expert.astype(F32))
    br = br.at[0, LANES:LANES + N_GROUPS].set(b_group.astype(F32))
    wgu = w_gate_up.reshape(N_EXPERTS, D, 2 * EXPERT_FF).astype(BF16)
    wd = w_down.reshape(N_EXPERTS, EXPERT_FF, D).astype(BF16)

    hx, gid, rank, counts = pl.pallas_call(
        _moe_router_kernel,
        grid=(T // tr,),
        in_specs=[pl.BlockSpec((tr, D), lambda i: (i, 0)),
                  pl.BlockSpec((1, D), lambda i: (0, 0)),
                  pl.BlockSpec((D, 2 * LANES), lambda i: (0, 0)),
                  pl.BlockSpec((1, 2 * LANES), lambda i: (0, 0))],
        out_specs=[pl.BlockSpec((tr, XW), lambda i: (i, 0)),
                   pl.BlockSpec((1, 1, tr), lambda i: (i, 0, 0)),
                   pl.BlockSpec((1, 1, tr), lambda i: (i, 0, 0)),
                   pl.BlockSpec((1, LANES), lambda i: (0, 0))],
        out_shape=[jax.ShapeDtypeStruct((T, XW), F32),
                   jax.ShapeDtypeStruct((T // tr, 1, tr), jnp.int32),
                   jax.ShapeDtypeStruct((T // tr, 1, tr), jnp.int32),
                   jax.ShapeDtypeStruct((1, LANES), F32)],
        scratch_shapes=[pltpu.VMEM((1, LANES), F32)],
        compiler_params=_compiler_params(("arbitrary",)),
        name="moe_router",
    )(h2d, gain.reshape(1, D), wr, br)

    seg_tiles = -(-counts[0, 0:N_GROUPS].astype(jnp.int32) // tm)
    seg_end = jnp.cumsum(seg_tiles)
    seg_start = seg_end - seg_tiles
    pos = (seg_start * tm)[gid.reshape(T)] + rank.reshape(T)
    tile_ids = jnp.arange(n_tiles, dtype=jnp.int32)
    tile_group = jnp.sum((tile_ids[:, None] >= seg_end[None, :]).astype(jnp.int32), axis=1)
    tile_group = jnp.where(tile_ids < seg_end[N_GROUPS - 1], tile_group, -1)
    pos3 = pos.reshape(T // tm, 1, tm)

    xs = pl.pallas_call(
        _moe_dispatch_kernel,
        grid_spec=pltpu.PrefetchScalarGridSpec(
            num_scalar_prefetch=1,
            grid=(T // tm,),
            in_specs=[pl.BlockSpec((1, 1, tm), lambda i, seg: (i, 0, 0), memory_space=pltpu.SMEM),
                      pl.BlockSpec((tm, XW), lambda i, seg: (i, 0))],
            out_specs=pl.BlockSpec(memory_space=pl.ANY),
            scratch_shapes=[pltpu.VMEM((tm, XW), F32), pltpu.SemaphoreType.DMA(()),
                            pltpu.SemaphoreType.DMA(())]),
        out_shape=jax.ShapeDtypeStruct((n_tiles * tm, XW), F32),
        compiler_params=_compiler_params(("arbitrary",)),
        name="moe_dispatch",
    )(seg_end, pos3, hx)

    def expert_index(j, e, grp):
        return (jnp.maximum(grp[j], 0) * EXPERTS_PER_GROUP + e, 0, 0)

    ys = pl.pallas_call(
        _moe_expert_kernel,
        grid_spec=pltpu.PrefetchScalarGridSpec(
            num_scalar_prefetch=1,
            grid=(n_tiles, EXPERTS_PER_GROUP),
            in_specs=[pl.BlockSpec((tm, XW), lambda j, e, grp: (j, 0)),
                      pl.BlockSpec((1, D, 2 * EXPERT_FF), expert_index),
                      pl.BlockSpec((1, EXPERT_FF, D), expert_index)],
            out_specs=pl.BlockSpec((tm, D), lambda j, e, grp: (j, 0)),
            scratch_shapes=[pltpu.VMEM((tm, D), BF16), pltpu.VMEM((tm, D), F32)]),
        out_shape=jax.ShapeDtypeStruct((n_tiles * tm, D), F32),
        compiler_params=_compiler_params(("parallel", "arbitrary")),
        name="moe_experts",
    )(tile_group, xs, wgu, wd)

    return pl.pallas_call(
        _moe_combine_kernel,
        grid=(T // tm,),
        in_specs=[pl.BlockSpec((1, 1, tm), lambda i: (i, 0, 0), memory_space=pltpu.SMEM),
                  pl.BlockSpec((tm, D), lambda i: (i, 0)),
                  pl.BlockSpec(memory_space=pl.ANY)],
        out_specs=pl.BlockSpec((tm, D), lambda i: (i, 0)),
        out_shape=jax.ShapeDtypeStruct((T, D), F32),
        scratch_shapes=[pltpu.VMEM((tm, D), F32), pltpu.SemaphoreType.DMA(())],
        compiler_params=_compiler_params(("arbitrary",)),
        name="moe_combine",
    )(pos3, h2d, ys)


def _dsa_prep_kernel(p_ref, qg_ref, kg_ref, lng_ref, lnb_ref, q_ref, kt_ref, v_ref, qi_ref, kit_ref, wi_ref):
    lane = lax.broadcasted_iota(jnp.int32, (1, LANES), 1)
    for h in range(ATT_HEADS):
        x = p_ref[:, DSA_Q_OFF + h * ATT_HD:DSA_Q_OFF + (h + 1) * ATT_HD]
        y = x * lax.rsqrt(jnp.mean(x * x, axis=-1, keepdims=True) + RMS_EPS) * qg_ref[...]
        q_ref[:, h * ATT_HD:(h + 1) * ATT_HD] = (y * (ATT_HD ** -0.5)).astype(BF16)
    for n in range(ATT_KV_HEADS):
        x = p_ref[:, DSA_K_OFF + n * ATT_HD:DSA_K_OFF + (n + 1) * ATT_HD]
        y = x * lax.rsqrt(jnp.mean(x * x, axis=-1, keepdims=True) + RMS_EPS) * kg_ref[...]
        kt_ref[0, n * ATT_HD:(n + 1) * ATT_HD, :] = y.T.astype(BF16)
    v_ref[...] = p_ref[:, DSA_V_OFF:DSA_V_OFF + ATT_KV_DIM].astype(BF16)
    qi_ref[...] = p_ref[:, DSA_QI_OFF:DSA_QI_OFF + IDX_HEADS * LANES].astype(BF16)
    x = p_ref[:, DSA_KI_OFF:DSA_KI_OFF + LANES]
    live = lane < IDX_HD
    mu = jnp.sum(x, axis=-1, keepdims=True) * (1.0 / IDX_HD)
    xc = jnp.where(live, x - mu, 0.0)
    var = jnp.sum(xc * xc, axis=-1, keepdims=True) * (1.0 / IDX_HD)
    ki = xc * lax.rsqrt(var + RMS_EPS) * lng_ref[...] + lnb_ref[...]
    kit_ref[0] = jnp.where(live, ki, 0.0).T.astype(BF16)
    wi_ref[...] = p_ref[:, DSA_WI_OFF:DSA_WI_OFF + LANES] * (IDX_HEADS ** -0.5 * IDX_HD ** -0.5)


def _dsa_prep(proj2d, q_gain, k_gain, ln_g, ln_b):
    T, W = proj2d.shape
    tm = KEY_TILE
    assert T % tm == 0
    lng = jnp.zeros((1, LANES), F32).at[0, 0:IDX_HD].set(ln_g.astype(F32))
    lnb = jnp.zeros((1, LANES), F32).at[0, 0:IDX_HD].set(ln_b.astype(F32))
    row = lambda i: (i, 0)
    fix = lambda i: (0, 0)
    blk = lambda i: (i, 0, 0)
    return pl.pallas_call(
        _dsa_prep_kernel,
        grid=(T // tm,),
        in_specs=[pl.BlockSpec((tm, W), row),
                  pl.BlockSpec((1, ATT_HD), fix), pl.BlockSpec((1, ATT_HD), fix),
                  pl.BlockSpec((1, LANES), fix), pl.BlockSpec((1, LANES), fix)],
        out_specs=[pl.BlockSpec((tm, ATT_Q_DIM), row), pl.BlockSpec((1, ATT_KV_DIM, tm), blk),
                   pl.BlockSpec((tm, ATT_KV_DIM), row), pl.BlockSpec((tm, IDX_HEADS * LANES), row),
                   pl.BlockSpec((1, LANES, tm), blk), pl.BlockSpec((tm, LANES), row)],
        out_shape=[jax.ShapeDtypeStruct((T, ATT_Q_DIM), BF16), jax.ShapeDtypeStruct((T // tm, ATT_KV_DIM, tm), BF16),
                   jax.ShapeDtypeStruct((T, ATT_KV_DIM), BF16), jax.ShapeDtypeStruct((T, IDX_HEADS * LANES), BF16),
                   jax.ShapeDtypeStruct((T // tm, LANES, tm), BF16), jax.ShapeDtypeStruct((T, LANES), F32)],
        compiler_params=_compiler_params(("parallel",)),
        name="dsa_prep",
    )(proj2d, q_gain.reshape(1, ATT_HD), k_gain.reshape(1, ATT_HD), lng, lnb)


def _dsa_kernel(q_ref, qi_ref, wi_ref, kt_ref, v_ref, kit_ref, bias_ref, bmax_ref, o_ref,
                sc_ref, mask_ref, qg_ref, wib_ref, m_ref, l_ref, acc_ref, knorm_ref, *, n_select, live_rows):
    i = pl.program_id(1)
    TQ = ATT_TILE
    TK = KEY_TILE
    HALVES = TK // LANES

    @pl.when(i == 0)
    def _():
        def body(kt, mx):
            kk = kt_ref[0, kt].astype(F32)
            return tuple(jnp.maximum(mx[n], jnp.sum(jnp.square(kk[n * ATT_HD:(n + 1) * ATT_HD, :]),
                                                    axis=0, keepdims=True)) for n in range(ATT_KV_HEADS))

        mx = lax.fori_loop(0, kt_ref.shape[1], body,
                           tuple(jnp.zeros((1, TK), F32) for _ in range(ATT_KV_HEADS)))
        for n in range(ATT_KV_HEADS):
            knorm_ref[n] = jnp.broadcast_to(jnp.sqrt(jnp.max(mx[n], axis=-1, keepdims=True)), (1, LANES))

    @pl.when(i * TQ >= live_rows)
    def _():
        o_ref[...] = jnp.zeros_like(o_ref)

    @pl.when(i * TQ < live_rows)
    def _():
        _dsa_tile(i, q_ref, qi_ref, wi_ref, kt_ref, v_ref, kit_ref, bias_ref, bmax_ref, o_ref,
                  sc_ref, mask_ref, qg_ref, wib_ref, m_ref, l_ref, acc_ref, knorm_ref, n_select, TQ, TK, HALVES)


def _dsa_tile(i, q_ref, qi_ref, wi_ref, kt_ref, v_ref, kit_ref, bias_ref, bmax_ref, o_ref,
              sc_ref, mask_ref, qg_ref, wib_ref, m_ref, l_ref, acc_ref, knorm_ref, n_select, TQ, TK, HALVES):
    nkt = lax.shift_right_logical(i, 1) + 1
    fmin = float(jnp.finfo(F32).min)
    kf = float(n_select)

    chunk_shift = int(math.log2(CHUNK))
    q_pos = i * TQ + lax.broadcasted_iota(jnp.int32, (1, TQ), 1)
    q_chunk = lax.shift_right_logical(q_pos, chunk_shift)
    k_row = lax.broadcasted_iota(jnp.int32, (TK, 1), 0)

    def halves(x):
        return [x[:, j * LANES:(j + 1) * LANES] for j in range(HALVES)]

    def fold(x, op):
        parts = [x[r * SUBLANES:(r + 1) * SUBLANES, :] for r in range(TK // SUBLANES)]
        while len(parts) > 1:
            parts = [op(parts[j], parts[j + 1]) for j in range(0, len(parts), 2)]
        return parts[0]

    wi = wi_ref[0]
    for h in range(IDX_HEADS):
        wib_ref[h] = jnp.broadcast_to(wi[:, h:h + 1], (TQ, TK))

    def score_tiles(kts):
        d = [[jnp.dot(qi_ref[0, :, h * LANES:(h + 1) * LANES], kit_ref[0, kt], preferred_element_type=F32)
              for h in range(IDX_HEADS)] for kt in kts]
        for t, kt in enumerate(kts):
            acc = wib_ref[0] * jnp.maximum(d[t][0], 0.0)
            for h in range(1, IDX_HEADS):
                acc = acc + wib_ref[h] * jnp.maximum(d[t][h], 0.0)
            mask_ref[kt] = acc
            sc_ref[kt] = acc.T

    def paired_loop(n, fn):
        def body(j, carry):
            fn([2 * j, 2 * j + 1])
            return carry

        lax.fori_loop(0, lax.shift_right_arithmetic(n, 1), body, 0)

        @pl.when(lax.bitwise_and(n, 1) == 1)
        def _():
            fn([n - 1])

    paired_loop(nkt, score_tiles)

    def mask_inadmissible(kt):
        k_pos = kt * TK + k_row
        adm = (lax.shift_right_logical(k_pos, chunk_shift) <= q_chunk) & (k_pos >= META_PAD)
        sc_ref[kt] = jnp.where(adm, sc_ref[kt], NEG_INF)

    mask_inadmissible(0)

    @pl.when(nkt > 1)
    def _():
        mask_inadmissible(nkt - 1)

    n_adm = CHUNK * (q_chunk + 1) - META_PAD
    need = n_adm > n_select

    def count_where(pred):
        def one(kt):
            return fold(jnp.where(pred(sc_ref[kt], kt), 1.0, 0.0), jnp.add)

        def two(j, acc):
            return acc + (one(2 * j) + one(2 * j + 1))

        acc = lax.fori_loop(0, lax.shift_right_logical(nkt, 1), two, jnp.zeros((SUBLANES, TQ), F32))
        acc = lax.cond(lax.bitwise_and(nkt, 1) == 1, lambda a: a + one(nkt - 1), lambda a: a, acc)
        return jnp.sum(acc, axis=0, keepdims=True)

    def count_ge(t):
        return count_where(lambda s, kt: s >= t)

    def minmax_body(kt, carry):
        mn, mx = carry
        s = sc_ref[kt]
        mn = jnp.minimum(mn, fold(jnp.where(s == NEG_INF, jnp.inf, s), jnp.minimum))
        mx = jnp.maximum(mx, fold(s, jnp.maximum))
        return mn, mx

    mn_l, mx_l = lax.fori_loop(0, nkt, minmax_body,
                               (jnp.full((SUBLANES, TQ), jnp.inf, F32), jnp.full((SUBLANES, TQ), NEG_INF, F32)))
    s_min = jnp.min(mn_l, axis=0, keepdims=True)
    s_max = jnp.max(mx_l, axis=0, keepdims=True)
    lo0 = s_min
    hi0 = s_max + (s_max - s_min) + 1.0
    cnt0 = n_adm.astype(F32)
    active0 = jnp.where(need, 1.0, 0.0)

    def any_active(active):
        return jnp.max(active) > 0.0

    def bis_step(lo, hi, cnt, active):
        mid = lo + (hi - lo) * 0.5
        c = count_ge(mid)
        ge = c >= kf
        moving = (mid > lo) & (mid < hi) & (active > 0.0)
        lo_n = jnp.where(moving & ge, mid, lo)
        hi_n = jnp.where(moving & (~ge), mid, hi)
        cnt_n = jnp.where(moving & ge, c, cnt)
        active_n = jnp.where(moving & (cnt_n != kf), 1.0, 0.0)
        return lo_n, hi_n, cnt_n, active_n

    def bis_body(st):
        it, rest = st[0], st[1:]
        for _ in range(BISECT_UNROLL):
            rest = bis_step(*rest)
        return (it + BISECT_UNROLL,) + rest

    _, lo, hi, cnt, _ = lax.while_loop(
        lambda st: jnp.logical_and(st[0] < BISECT_STEPS, any_active(st[4])), bis_body,
        (jnp.int32(0), lo0, hi0, cnt0, active0))

    def snap_body(st):
        lo, hi, cnt, active = st

        def body(kt, mx):
            s = sc_ref[kt]
            return jnp.maximum(mx, fold(jnp.where(s < hi, s, NEG_INF), jnp.maximum))

        v1 = jnp.max(lax.fori_loop(0, nkt, body, jnp.full((SUBLANES, TQ), NEG_INF, F32)),
                     axis=0, keepdims=True)
        c1 = count_ge(v1)
        on = active > 0.0
        hit = on & (c1 >= kf)
        lo_n = jnp.where(hit, v1, lo)
        cnt_n = jnp.where(hit, c1, cnt)
        hi_n = jnp.where(on & (~hit), v1, hi)
        return lo_n, hi_n, cnt_n, jnp.where(on & (~hit), 1.0, 0.0)

    unsettled = jnp.where(need & (cnt != kf), 1.0, 0.0)
    lo, hi, cnt, _ = lax.while_loop(lambda st: any_active(st[3]), snap_body, (lo, hi, cnt, unsettled))

    thr = jnp.where(need, lo, fmin)
    tied = need & (cnt > kf)
    big_pos = jnp.int32(2 ** 30)

    def tie_break():
        want = kf - count_where(lambda s, kt: s > thr)

        def body(_, st):
            jl, jh = st
            jm = lax.shift_right_arithmetic(jl + jh, 1)
            c = count_where(lambda s, kt: (s == thr) & (kt * TK + k_row <= jm))
            ok = c >= want
            return jnp.where(ok, jl, jm), jnp.where(ok, jm, jh)

        n_keys = sc_ref.shape[0] * TK
        steps = int(math.ceil(math.log2(n_keys + 1))) + 1
        jl0 = jnp.full((1, TQ), -1, jnp.int32)
        jh0 = jnp.full((1, TQ), n_keys, jnp.int32)
        _, jh = lax.fori_loop(0, steps, body, (jl0, jh0))
        return jnp.where(tied, jh, big_pos)

    pos_cap = lax.cond(jnp.max(jnp.where(tied, 1.0, 0.0)) > 0.0, tie_break,
                       lambda: jnp.full((1, TQ), big_pos, jnp.int32))

    def lane_wide(row):
        assert TQ == LANES
        return jnp.broadcast_to(row, (LANES, TQ)).T

    thr_w = lane_wide(thr)
    cap_w = lane_wide(pos_cap)
    chunk_w = lane_wide(q_chunk)
    k_lane = lax.broadcasted_iota(jnp.int32, (1, LANES), 1)

    def to_mask(kt, edge_tile):
        out = []
        for j, s in enumerate(halves(mask_ref[kt])):
            k_pos = kt * TK + j * LANES + k_lane
            sel = (s > thr_w) | ((s == thr_w) & (k_pos <= cap_w))
            if edge_tile:
                sel = sel & (lax.shift_right_logical(k_pos, chunk_shift) <= chunk_w) & (k_pos >= META_PAD)
            out.append(jnp.where(sel, 0.0, NEG_INF))
        mask_ref[kt] = jnp.concatenate(out, axis=1)

    def inner_mask(kt, carry):
        to_mask(kt, False)
        return carry

    lax.fori_loop(1, nkt - 1, inner_mask, 0)
    to_mask(0, True)

    @pl.when(nkt > 1)
    def _():
        to_mask(nkt - 1, True)

    for n in range(ATT_KV_HEADS):
        for g in range(ATT_GROUP):
            h = n * ATT_GROUP + g
            qg_ref[n, g * TQ:(g + 1) * TQ, :] = q_ref[0, :, h * ATT_HD:(h + 1) * ATT_HD]
    l_ref[...] = jnp.zeros_like(l_ref)
    acc_ref[...] = jnp.zeros_like(acc_ref)

    kv_heads = range(ATT_KV_HEADS)

    def raw_logits(kts):
        return [[jnp.dot(qg_ref[n], kt_ref[0, kt, n * ATT_HD:(n + 1) * ATT_HD, :], preferred_element_type=F32)
                 for n in kv_heads] for kt in kts]

    def masked_logits(s, mask, n, g, b0):
        sg = s[g * TQ:(g + 1) * TQ, :] + mask
        if b0 is not None:
            h = n * ATT_GROUP + g
            sg = sg + jnp.concatenate([bias_ref[h, b0 + j] for j in range(HALVES)], axis=1)
        return sg

    def max_pass(kts, b0):
        raw = raw_logits(kts)
        masks = [mask_ref[kt] for kt in kts]
        for n in kv_heads:
            for g in range(ATT_GROUP):
                rows = slice(g * TQ, (g + 1) * TQ)
                m = m_ref[n, rows, :]
                for t in range(len(kts)):
                    for part in halves(masked_logits(raw[t][n], masks[t], n, g, b0)):
                        m = jnp.maximum(m, part)
                m_ref[n, rows, :] = m

    def exp_pass(kts, b0):
        raw = raw_logits(kts)
        masks = [mask_ref[kt] for kt in kts]
        for n in kv_heads:
            ps = [[] for _ in kts]
            for g in range(ATT_GROUP):
                rows = slice(g * TQ, (g + 1) * TQ)
                m = m_ref[n, rows, :]
                l_new = l_ref[n, rows, :]
                for t in range(len(kts)):
                    parts = [jnp.exp(part - m) for part in halves(masked_logits(raw[t][n], masks[t], n, g, b0))]
                    l_new = l_new + functools.reduce(lambda a, b: a + b, parts)
                    ps[t].append(jnp.concatenate(parts, axis=1).astype(BF16))
                l_ref[n, rows, :] = l_new
            pv = None
            for t, kt in enumerate(kts):
                vv = v_ref[0, pl.ds(pl.multiple_of(kt * TK, TK), TK), n * ATT_HD:(n + 1) * ATT_HD]
                d = jnp.dot(jnp.concatenate(ps[t], axis=0), vv, preferred_element_type=F32)
                pv = d if pv is None else pv + d
            acc_ref[n] += pv

    n_far = lax.shift_right_arithmetic(i - 1, 1)
    parity = lax.bitwise_and(i, 1)

    def run_pass(tile_fn):
        paired_loop(jnp.maximum(n_far, 0), lambda kts: tile_fn(kts, None))

        @pl.when(i >= 1)
        def _():
            tile_fn([n_far], parity)

        @pl.when(parity == 0)
        def _():
            tile_fn([lax.shift_right_logical(i, 1)], 2)

    for n in range(ATT_KV_HEADS):
        for g in range(ATT_GROUP):
            h = n * ATT_GROUP + g
            qh = qg_ref[n, g * TQ:(g + 1) * TQ, :].astype(F32)
            q_len = jnp.sqrt(jnp.sum(qh * qh, axis=-1, keepdims=True))
            bound = q_len * knorm_ref[n] * SHIFT_SLACK + (bmax_ref[h] + SHIFT_SLACK - 1.0)
            m_ref[n, g * TQ:(g + 1) * TQ, :] = jnp.broadcast_to(bound, (TQ, LANES))
    run_pass(exp_pass)

    def smallest_row_sum():
        return functools.reduce(jnp.minimum, [jnp.min(jnp.sum(l_ref[n], axis=-1, keepdims=True))
                                              for n in range(ATT_KV_HEADS)])

    @pl.when(jnp.logical_not(smallest_row_sum() > ROW_SUM_FLOOR))
    def _():
        m_ref[...] = jnp.full(m_ref.shape, -1e30, F32)
        run_pass(max_pass)
        for n in range(ATT_KV_HEADS):
            m_ref[n] = jnp.broadcast_to(jnp.max(m_ref[n], axis=-1, keepdims=True), m_ref.shape[1:])
        l_ref[...] = jnp.zeros_like(l_ref)
        acc_ref[...] = jnp.zeros_like(acc_ref)
        run_pass(exp_pass)

    for n in range(ATT_KV_HEADS):
        inv_l = 1.0 / jnp.sum(l_ref[n], axis=-1, keepdims=True)
        for g in range(ATT_GROUP):
            h = n * ATT_GROUP + g
            rows = slice(g * TQ, (g + 1) * TQ)
            o_ref[0, :, h * ATT_HD:(h + 1) * ATT_HD] = (acc_ref[n, rows, :] * inv_l[rows, :]).astype(o_ref.dtype)


def _t5_bucket(rel):
    nb = REL_BUCKETS // 2
    max_exact = nb // 2
    ret = jnp.where(rel > 0, nb, 0)
    n = jnp.abs(rel)
    nf = jnp.maximum(n, 1).astype(jnp.float32)
    large = max_exact + (jnp.log(nf / max_exact) / math.log(REL_MAX_DIST / max_exact)
                         * (nb - max_exact)).astype(jnp.int32)
    large = jnp.minimum(large, nb - 1)
    return ret + jnp.where(n < max_exact, n, large)


def _dsa_core(q, kt, v, qi, kit, wi, rel_bias, n_select, live_rows):
    B, LP, _ = q.shape
    nq = LP // ATT_TILE
    nkt = LP // KEY_TILE
    a = jnp.arange(ATT_TILE)[:, None]
    xx = jnp.arange(2 * KEY_TILE)[None, :]
    rel = (xx - KEY_TILE) - a
    far = rel_bias.astype(F32)[REL_BUCKETS // 2 - 1]
    onehot = jax.nn.one_hot(_t5_bucket(rel), REL_BUCKETS, dtype=F32)
    table = jnp.einsum("abk,kh->abh", onehot, rel_bias.astype(F32),
                       precision=lax.Precision.HIGHEST) - far
    bias_near = jnp.transpose(table.reshape(ATT_TILE, 2 * KEY_TILE // LANES, LANES, ATT_HEADS), (3, 1, 0, 2))
    bias_max = jnp.broadcast_to(jnp.maximum(jnp.max(table, axis=(0, 1)), 0.0)[:, None, None],
                                (ATT_HEADS, 1, LANES))
    kern = functools.partial(_dsa_kernel, n_select=n_select, live_rows=live_rows)
    tile = lambda b, i: (b, i, 0)
    whole3 = lambda b, i: (b, 0, 0)
    whole4 = lambda b, i: (b, 0, 0, 0)
    return pl.pallas_call(
        kern,
        grid=(B, nq),
        in_specs=[pl.BlockSpec((1, ATT_TILE, ATT_Q_DIM), tile),
                  pl.BlockSpec((1, ATT_TILE, IDX_HEADS * LANES), tile),
                  pl.BlockSpec((1, ATT_TILE, LANES), tile),
                  pl.BlockSpec((1, nkt, ATT_KV_DIM, KEY_TILE), whole4),
                  pl.BlockSpec((1, LP, ATT_KV_DIM), whole3),
                  pl.BlockSpec((1, nkt, LANES, KEY_TILE), whole4),
                  pl.BlockSpec((ATT_HEADS, 2 * KEY_TILE // LANES, ATT_TILE, LANES), lambda b, i: (0, 0, 0, 0)),
                  pl.BlockSpec((ATT_HEADS, 1, LANES), lambda b, i: (0, 0, 0))],
        out_specs=pl.BlockSpec((1, ATT_TILE, ATT_Q_DIM), tile),
        out_shape=jax.ShapeDtypeStruct((B, LP, ATT_Q_DIM), BF16),
        scratch_shapes=[pltpu.VMEM((nkt, KEY_TILE, ATT_TILE), F32),
                        pltpu.VMEM((nkt, ATT_TILE, KEY_TILE), F32),
                        pltpu.VMEM((ATT_KV_HEADS, ATT_GROUP * ATT_TILE, ATT_HD), BF16),
                        pltpu.VMEM((IDX_HEADS, ATT_TILE, KEY_TILE), F32),
                        pltpu.VMEM((ATT_KV_HEADS, ATT_GROUP * ATT_TILE, LANES), F32),
                        pltpu.VMEM((ATT_KV_HEADS, ATT_GROUP * ATT_TILE, LANES), F32),
                        pltpu.VMEM((ATT_KV_HEADS, ATT_GROUP * ATT_TILE, ATT_HD), F32),
                        pltpu.VMEM((ATT_KV_HEADS, 1, LANES), F32)],
        compiler_params=_compiler_params(("arbitrary", "arbitrary")),
        name="dsa_core",
    )(q, qi, wi, kt, v, kit, bias_near, bias_max)


def _gdn_in_weight(w_in):
    D = w_in.shape[0]
    w = jnp.zeros((D, GDN_PROJ_W), F32).at[:, 0:w_in.shape[1]].set(w_in.astype(F32))
    return w.astype(BF16)


def _dsa_in_weight(w_in):
    D = w_in.shape[0]
    w_in = w_in.astype(BF16)
    zeros = lambda n: jnp.zeros((D, n), BF16)
    src = ATT_Q_DIM + 2 * ATT_KV_DIM
    parts = [w_in[:, 0:src]]
    for h in range(IDX_HEADS):
        parts += [w_in[:, src + h * IDX_HD:src + (h + 1) * IDX_HD], zeros(LANES - IDX_HD)]
    src += IDX_HEADS * IDX_HD
    parts += [w_in[:, src:src + IDX_HD], zeros(LANES - IDX_HD),
              w_in[:, src + IDX_HD:src + IDX_HD + IDX_HEADS], zeros(LANES - IDX_HEADS)]
    w = jnp.concatenate(parts, axis=1)
    assert w.shape[1] == DSA_PROJ_W
    return w


def kernel(x, meta_tokens, norm_mix, norm_ffn, rel_bias, gdn_w_in, gdn_conv, gdn_a_log, gdn_dt_bias,
           gdn_o_norm, gdn_w_out, dsa_w_in, dsa_q_norm, dsa_k_norm, dsa_idx_ln_g, dsa_idx_ln_b, dsa_w_out,
           moe_w_group, moe_b_group, moe_w_expert, moe_b_expert, moe_w_gate_up, moe_w_down):
    B, S, D = x.shape
    depth = norm_mix.shape[0]
    n_select = min(TOPK_MAX, S // 4)
    off = META_PAD + N_META
    LP = -(-(off + S) // ROW_ALIGN) * ROW_ALIGN
    meta = jnp.broadcast_to(meta_tokens.astype(x.dtype)[None], (B, N_META, D))
    h = jnp.concatenate([jnp.zeros((B, META_PAD, D), x.dtype), meta, x,
                         jnp.zeros((B, LP - off - S, D), x.dtype)], axis=1)
    h = h.reshape(B * LP, D)
    for i in range(depth):
        j = i // 2
        if i % 2 == 0:
            proj = _norm_matmul(h, norm_mix[i], _gdn_in_weight(gdn_w_in[j]))
            o = _gdn_core(proj.reshape(B, LP, GDN_PROJ_W), gdn_conv[j].astype(F32), gdn_a_log[j],
                          gdn_dt_bias[j], gdn_o_norm[j])
            h = _matmul_residual(o.reshape(B * LP, GDN_V_DIM), gdn_w_out[j].astype(BF16), h)
        else:
            proj = _norm_matmul(h, norm_mix[i], _dsa_in_weight(dsa_w_in[j]))
            q, kt, v, qi, kit, wi = _dsa_prep(proj, dsa_q_norm[j], dsa_k_norm[j], dsa_idx_ln_g[j], dsa_idx_ln_b[j])
            r3 = lambda t: t.reshape(B, LP, t.shape[-1])
            r4 = lambda t: t.reshape(B, LP // KEY_TILE, t.shape[-2], KEY_TILE)
            o = _dsa_core(r3(q), r4(kt), r3(v), r3(qi), r4(kit), r3(wi), rel_bias, n_select, off + S)
            h = _matmul_residual(o.reshape(B * LP, ATT_Q_DIM), dsa_w_out[j].astype(BF16), h)
        h = _moe(h, norm_ffn[i], moe_w_group[i], moe_b_group[i], moe_w_expert[i], moe_b_expert[i],
                 moe_w_gate_up[i], moe_w_down[i])
    return h.reshape(B, LP, D)[:, off:off + S]
```

```python
import functools
import math

import jax
import jax.numpy as jnp
from jax import lax
from jax.experimental import pallas as pl
from jax.experimental.pallas import tpu as pltpu

F32 = jnp.float32
BF16 = jnp.bfloat16

LANES = 128
SUBLANES = 8
VMEM_LIMIT_BYTES = 56 * 1024 * 1024

CHUNK = 64
N_META = 16
META_PAD = (-N_META) % CHUNK
RMS_EPS = 1e-6
L2_EPS = 1e-6

GDN_HEADS = 8
GDN_DK = 128
GDN_DV = 128
GDN_CONV = 4
GDN_QK_DIM = GDN_HEADS * GDN_DK
GDN_V_DIM = GDN_HEADS * GDN_DV
GDN_CONV_DIM = 2 * GDN_QK_DIM + GDN_V_DIM
GDN_GATE_OFF = GDN_CONV_DIM + GDN_V_DIM
GDN_PROJ_W = GDN_GATE_OFF + LANES

ATT_HEADS = 8
ATT_KV_HEADS = 2
ATT_GROUP = ATT_HEADS // ATT_KV_HEADS
ATT_HD = 128
ATT_Q_DIM = ATT_HEADS * ATT_HD
ATT_KV_DIM = ATT_KV_HEADS * ATT_HD
IDX_HEADS = 8
IDX_HD = 64
TOPK_MAX = 256
REL_BUCKETS = 32
REL_MAX_DIST = 128

DSA_Q_OFF = 0
DSA_K_OFF = ATT_Q_DIM
DSA_V_OFF = DSA_K_OFF + ATT_KV_DIM
DSA_QI_OFF = DSA_V_OFF + ATT_KV_DIM
DSA_KI_OFF = DSA_QI_OFF + IDX_HEADS * LANES
DSA_WI_OFF = DSA_KI_OFF + LANES
DSA_PROJ_W = DSA_WI_OFF + LANES

N_GROUPS = 4
EXPERTS_PER_GROUP = 8
N_EXPERTS = N_GROUPS * EXPERTS_PER_GROUP
EXPERT_FF = 256

MOE_ROUTER_TILES = (512, 256, 128)
MOE_ROW_TILES = (512, 256, 128)

ATT_TILE = 256
KEY_TILE = 256
ROW_ALIGN = KEY_TILE
BISECT_UNROLL = 4
BISECT_STEPS = 24
SHIFT_SLACK = 1.001
ROW_SUM_FLOOR = 1e-30
NEG_INF = float("-inf")


def _pick_tile(n, candidates):
    for c in candidates:
        if n % c == 0:
            return c
    raise ValueError(f"no tile for {n}")


def _compiler_params(semantics):
    return pltpu.CompilerParams(dimension_semantics=semantics, vmem_limit_bytes=VMEM_LIMIT_BYTES)


def _silu(x):
    return x * jax.nn.sigmoid(x)


def _norm_matmul_kernel(x_ref, g_ref, w_ref, o_ref):
    x = x_ref[...]
    y = x * lax.rsqrt(jnp.mean(x * x, axis=-1, keepdims=True) + RMS_EPS) * g_ref[...]
    o_ref[...] = jnp.dot(y.astype(BF16), w_ref[...], preferred_element_type=F32)


def _norm_matmul(h2d, gain, w_bf16):
    T, D = h2d.shape
    N = w_bf16.shape[1]
    tm = _pick_tile(T, (256, 128))
    return pl.pallas_call(
        _norm_matmul_kernel,
        grid=(T // tm,),
        in_specs=[pl.BlockSpec((tm, D), lambda i: (i, 0)),
                  pl.BlockSpec((1, D), lambda i: (0, 0)),
                  pl.BlockSpec((D, N), lambda i: (0, 0))],
        out_specs=pl.BlockSpec((tm, N), lambda i: (i, 0)),
        out_shape=jax.ShapeDtypeStruct((T, N), F32),
        compiler_params=_compiler_params(("parallel",)),
        name="norm_matmul",
    )(h2d, gain.reshape(1, D), w_bf16)


def _matmul_residual_kernel(a_ref, w_ref, r_ref, o_ref):
    o_ref[...] = r_ref[...] + jnp.dot(a_ref[...], w_ref[...], preferred_element_type=F32)


def _matmul_residual(a_bf16, w_bf16, res):
    T, K = a_bf16.shape
    N = w_bf16.shape[1]
    tm = _pick_tile(T, (512, 256, 128))
    return pl.pallas_call(
        _matmul_residual_kernel,
        grid=(T // tm,),
        in_specs=[pl.BlockSpec((tm, K), lambda i: (i, 0)),
                  pl.BlockSpec((K, N), lambda i: (0, 0)),
                  pl.BlockSpec((tm, N), lambda i: (i, 0))],
        out_specs=pl.BlockSpec((tm, N), lambda i: (i, 0)),
        out_shape=jax.ShapeDtypeStruct((T, N), F32),
        compiler_params=_compiler_params(("parallel",)),
        name="matmul_residual",
    )(a_bf16, w_bf16, res)


def _nt_dot(a, b):
    return lax.dot_general(a, b, (((1,), (1,)), ((), ())), preferred_element_type=F32)


def _tn_dot(a, b):
    return lax.dot_general(a, b, (((0,), (0,)), ((), ())), preferred_element_type=F32)


def _gdn_kernel(proj_ref, conv_ref, gate_ref, ogain_ref, o_ref, state_ref, xbuf_ref, qkv_ref):
    c = pl.program_id(1)
    C = CHUNK

    @pl.when(c == 0)
    def _():
        state_ref[...] = jnp.zeros_like(state_ref)
        xbuf_ref[0:SUBLANES, :] = jnp.zeros((SUBLANES, GDN_CONV_DIM), F32)

    row = c * C + lax.broadcasted_iota(jnp.int32, (C, 1), 0)
    live = row >= META_PAD
    u = jnp.where(live, proj_ref[0, :, 0:GDN_CONV_DIM], 0.0)
    xbuf_ref[SUBLANES:SUBLANES + C, :] = u
    conv = conv_ref[0:1, :] * xbuf_ref[SUBLANES - 3:SUBLANES - 3 + C, :]
    for j in range(1, GDN_CONV):
        conv = conv + conv_ref[j:j + 1, :] * xbuf_ref[SUBLANES - 3 + j:SUBLANES - 3 + j + C, :]
    xbuf_ref[0:SUBLANES, :] = u[C - SUBLANES:C, :]
    qkv_ref[...] = _silu(conv)
    qkv = qkv_ref

    gates = jnp.where(live, proj_ref[0, :, GDN_GATE_OFF:GDN_GATE_OFF + LANES], 0.0)
    beta = jax.nn.sigmoid(gates)
    neg_rate = -jnp.exp(gate_ref[0:1, :])
    g = neg_rate * jax.nn.softplus(gates + gate_ref[1:2, :])
    r_i = lax.broadcasted_iota(jnp.int32, (C, C), 0)
    c_i = lax.broadcasted_iota(jnp.int32, (C, C), 1)
    causal = c_i <= r_i
    strict = c_i < r_i
    gc = jnp.dot(causal.astype(F32), g, preferred_element_type=F32,
                 precision=lax.Precision.HIGHEST)
    gc_t = gc.T
    eg = jnp.exp(gc)
    g_last = gc[C - 1:C, :]
    eg_last = jnp.exp(g_last)
    ek = jnp.exp(g_last - gc)

    heads = range(GDN_HEADS)
    dot = functools.partial(jnp.dot, preferred_element_type=F32)
    col = lambda a, h: a[:, GDN_HEADS + h:GDN_HEADS + h + 1]
    qs, ks, k16s, xs, decays = [], [], [], [], []
    for h in heads:
        lo = h * GDN_DK
        q = qkv[:, lo:lo + GDN_DK]
        k = qkv[:, GDN_QK_DIM + lo:GDN_QK_DIM + lo + GDN_DK]
        v = qkv[:, 2 * GDN_QK_DIM + h * GDN_DV:2 * GDN_QK_DIM + (h + 1) * GDN_DV]
        q = q * lax.rsqrt(jnp.sum(q * q, axis=-1, keepdims=True) + L2_EPS) * (GDN_DK ** -0.5)
        k = k * lax.rsqrt(jnp.sum(k * k, axis=-1, keepdims=True) + L2_EPS)
        kb = k * beta[:, h:h + 1]
        vb = v * beta[:, h:h + 1]
        qs.append(q)
        ks.append(k)
        k16s.append(k.astype(BF16))
        xs.append((kb, jnp.concatenate([vb, kb * col(eg, h)], axis=1)))
        decays.append(jnp.exp(jnp.where(causal, col(gc, h) - gc_t[GDN_HEADS + h:GDN_HEADS + h + 1, :],
                                        NEG_INF)))
    kk = [_nt_dot(xs[h][0].astype(BF16), k16s[h]) for h in heads]
    qk = [_nt_dot(qs[h].astype(BF16), k16s[h]) for h in heads]
    ns = [-jnp.where(strict, kk[h] * decays[h], 0.0) for h in heads]
    xs = [x for _, x in xs]
    for it in range(6):
        n16 = [n.astype(BF16) for n in ns]
        xs = [xs[h] + dot(n16[h], xs[h].astype(BF16)) for h in heads]
        if it < 5:
            ns = [dot(n16[h], n16[h]) for h in heads]
    qk = [jnp.where(causal, qk[h] * decays[h], 0.0).astype(BF16) for h in heads]
    states = [state_ref[h] for h in heads]
    s16 = [s.astype(BF16) for s in states]
    ws = [dot(xs[h][:, GDN_DV:2 * GDN_DV].astype(BF16), s16[h]) for h in heads]
    qs_s = [dot((qs[h] * col(eg, h)).astype(BF16), s16[h]) for h in heads]
    v_new = [(xs[h][:, 0:GDN_DV] - ws[h]).astype(BF16) for h in heads]
    os_ = [qs_s[h] + dot(qk[h], v_new[h]) for h in heads]
    kv = [_tn_dot((ks[h] * col(ek, h)).astype(BF16), v_new[h]) for h in heads]
    for h in heads:
        state_ref[h] = states[h] * col(eg_last, h) + kv[h]
        z = proj_ref[0, :, GDN_CONV_DIM + h * GDN_DV:GDN_CONV_DIM + (h + 1) * GDN_DV]
        o = os_[h]
        on = o * lax.rsqrt(jnp.mean(o * o, axis=-1, keepdims=True) + RMS_EPS) * ogain_ref[...]
        o_ref[0, :, h * GDN_DV:(h + 1) * GDN_DV] = (on * _silu(z)).astype(o_ref.dtype)


def _gdn_core(proj, conv_w, a_log, dt_bias, o_gain):
    B, LP, W = proj.shape
    nc = LP // CHUNK
    gate = jnp.zeros((2, LANES), F32)
    gate = gate.at[0, GDN_HEADS:2 * GDN_HEADS].set(a_log.astype(F32))
    gate = gate.at[1, GDN_HEADS:2 * GDN_HEADS].set(dt_bias.astype(F32))
    return pl.pallas_call(
        _gdn_kernel,
        grid=(B, nc),
        in_specs=[pl.BlockSpec((1, CHUNK, W), lambda b, c: (b, c, 0)),
                  pl.BlockSpec((GDN_CONV, GDN_CONV_DIM), lambda b, c: (0, 0)),
                  pl.BlockSpec((2, LANES), lambda b, c: (0, 0)),
                  pl.BlockSpec((1, GDN_DV), lambda b, c: (0, 0))],
        out_specs=pl.BlockSpec((1, CHUNK, GDN_V_DIM), lambda b, c: (b, c, 0)),
        out_shape=jax.ShapeDtypeStruct((B, LP, GDN_V_DIM), BF16),
        scratch_shapes=[pltpu.VMEM((GDN_HEADS, GDN_DK, GDN_DV), F32),
                        pltpu.VMEM((SUBLANES + CHUNK, GDN_CONV_DIM), F32),
                        pltpu.VMEM((CHUNK, GDN_CONV_DIM), F32)],
        compiler_params=_compiler_params(("parallel", "arbitrary")),
        name="gdn_core",
    )(proj, conv_w, gate, o_gain.reshape(1, GDN_DV))


def _moe_router_kernel(h_ref, gain_ref, wr_ref, br_ref, hx_ref, gid_ref, rank_ref, cnt_ref, carry_ref):
    i = pl.program_id(0)
    TM = h_ref.shape[0]
    lane = lax.broadcasted_iota(jnp.int32, (1, LANES), 1)

    @pl.when(i == 0)
    def _():
        carry_ref[...] = jnp.zeros_like(carry_ref)

    x = h_ref[...]
    xn = x * lax.rsqrt(jnp.mean(x * x, axis=-1, keepdims=True) + RMS_EPS) * gain_ref[...]
    logits = jnp.dot(xn, wr_ref[...], preferred_element_type=F32,
                     precision=lax.Precision.HIGHEST) + br_ref[...]
    e_log = logits[:, 0:LANES]
    g_log = jnp.where(lane < N_GROUPS, logits[:, LANES:2 * LANES], NEG_INF)
    g_max = jnp.max(g_log, axis=-1, keepdims=True)
    g_sel = jnp.min(jnp.where(g_log == g_max, lane, LANES), axis=-1, keepdims=True)
    g_w = 1.0 / jnp.sum(jnp.exp(g_log - g_max), axis=-1, keepdims=True)
    in_grp = (lane >= g_sel * EXPERTS_PER_GROUP) & (lane < (g_sel + 1) * EXPERTS_PER_GROUP)
    l0 = jnp.where(in_grp, e_log, NEG_INF)
    m1 = jnp.max(l0, axis=-1, keepdims=True)
    i1 = jnp.min(jnp.where(l0 == m1, lane, LANES), axis=-1, keepdims=True)
    l1 = jnp.where(lane == i1, NEG_INF, l0)
    m2 = jnp.max(l1, axis=-1, keepdims=True)
    i2 = jnp.min(jnp.where(l1 == m2, lane, LANES), axis=-1, keepdims=True)
    p2 = jnp.exp(m2 - m1)
    w1 = g_w / (1.0 + p2)
    w2 = g_w * p2 / (1.0 + p2)
    j1 = i1 - g_sel * EXPERTS_PER_GROUP
    j2 = i2 - g_sel * EXPERTS_PER_GROUP
    hx_ref[:, 0:xn.shape[1]] = xn
    hx_ref[:, xn.shape[1]:] = jnp.where(lane == j1, w1, 0.0) + jnp.where(lane == j2, w2, 0.0)

    onehot = jnp.where(lane == g_sel, 1.0, 0.0)
    r_i = lax.broadcasted_iota(jnp.int32, (TM, TM), 0)
    c_i = lax.broadcasted_iota(jnp.int32, (TM, TM), 1)
    before = jnp.where(c_i < r_i, 1.0, 0.0).astype(BF16)
    prior = jnp.dot(before, onehot.astype(BF16), preferred_element_type=F32) + carry_ref[...]
    rank = jnp.sum(jnp.where(lane == g_sel, prior, 0.0), axis=-1, keepdims=True)
    carry_ref[...] += jnp.sum(onehot, axis=0, keepdims=True)
    cnt_ref[...] = carry_ref[...]
    gid_ref[0] = jnp.broadcast_to(g_sel.astype(F32), (TM, LANES)).T[0:1, :].astype(jnp.int32)
    rank_ref[0] = jnp.broadcast_to(rank, (TM, LANES)).T[0:1, :].astype(jnp.int32)


def _moe_dispatch_kernel(seg_ref, pos_ref, hx_ref, xs_ref, zero_ref, sem, zero_sem):
    TM = hx_ref.shape[0]

    @pl.when(pl.program_id(0) == 0)
    def _():
        zero_ref[...] = jnp.zeros_like(zero_ref)

        def zero_tile(t):
            fill = pltpu.make_async_copy(zero_ref, xs_ref.at[pl.ds(t * TM, TM), :], zero_sem)
            fill.start()
            fill.wait()

        for g in range(N_GROUPS):
            first = seg_ref[g - 1] if g else 0
            pl.when(seg_ref[g] > first)(functools.partial(zero_tile, seg_ref[g] - 1))
            tail = seg_ref[N_GROUPS - 1] + g
            pl.when(tail < xs_ref.shape[0] // TM)(functools.partial(zero_tile, tail))

    def start(a, carry):
        for b in range(SUBLANES):
            r = a * SUBLANES + b
            pltpu.make_async_copy(hx_ref.at[pl.ds(r, 1), :], xs_ref.at[pl.ds(pos_ref[0, 0, r], 1), :],
                                  sem).start()
        return carry

    lax.fori_loop(0, TM // SUBLANES, start, 0)
    pltpu.make_async_copy(hx_ref, xs_ref.at[pl.ds(0, TM), :], sem).wait()


def _moe_expert_kernel(grp_ref, xs_ref, wgu_ref, wd_ref, ys_ref, xn_ref, acc_ref):
    j = pl.program_id(0)
    e = pl.program_id(1)
    D = xn_ref.shape[1]
    lane = lax.broadcasted_iota(jnp.int32, (1, LANES), 1)

    @pl.when(grp_ref[j] >= 0)
    def _():
        @pl.when(e == 0)
        def _():
            xn_ref[...] = xs_ref[:, 0:D].astype(BF16)
            acc_ref[...] = jnp.zeros_like(acc_ref)

        hh = jnp.dot(xn_ref[...], wgu_ref[0], preferred_element_type=F32)
        cw = jnp.sum(jnp.where(lane == e, xs_ref[:, D:], 0.0), axis=-1, keepdims=True)
        act = _silu(hh[:, 0:EXPERT_FF]) * hh[:, EXPERT_FF:2 * EXPERT_FF] * cw
        acc_ref[...] += jnp.dot(act.astype(BF16), wd_ref[0], preferred_element_type=F32)

    last = e == EXPERTS_PER_GROUP - 1

    @pl.when(last & (grp_ref[j] >= 0))
    def _():
        ys_ref[...] = acc_ref[...]

    @pl.when(last & (grp_ref[j] < 0))
    def _():
        ys_ref[...] = jnp.zeros_like(ys_ref)


def _moe_combine_kernel(pos_ref, h_ref, ys_ref, o_ref, stage_ref, sem):
    TM = h_ref.shape[0]

    def start(a, carry):
        for b in range(SUBLANES):
            r = a * SUBLANES + b
            pltpu.make_async_copy(ys_ref.at[pl.ds(pos_ref[0, 0, r], 1), :], stage_ref.at[pl.ds(r, 1), :],
                                  sem).start()
        return carry

    lax.fori_loop(0, TM // SUBLANES, start, 0)
    pltpu.make_async_copy(ys_ref.at[pl.ds(0, TM), :], stage_ref, sem).wait()
    o_ref[...] = h_ref[...] + stage_ref[...]


def _moe(h2d, gain, w_group, b_group, w_expert, b_expert, w_gate_up, w_down):
    T, D = h2d.shape
    tr = _pick_tile(T, MOE_ROUTER_TILES)
    tm = _pick_tile(T, MOE_ROW_TILES)
    n_tiles = T // tm + N_GROUPS
    XW = D + LANES
    wr = jnp.zeros((D, 2 * LANES), F32)
    wr = wr.at[:, 0:N_EXPERTS].set(w_expert.astype(F32))
    wr = wr.at[:, LANES:LANES + N_GROUPS].set(w_group.astype(F32))
    br = jnp.zeros((1, 2 * LANES), F32)
    br = br.at[0, 0:N_EXPERTS].set(b_expert.astype(F32))
    br = br.at[0, LANES:LANES + N_GROUPS].set(b_group.astype(F32))
    wgu = w_gate_up.reshape(N_EXPERTS, D, 2 * EXPERT_FF).astype(BF16)
    wd = w_down.reshape(N_EXPERTS, EXPERT_FF, D).astype(BF16)

    hx, gid, rank, counts = pl.pallas_call(
        _moe_router_kernel,
        grid=(T // tr,),
        in_specs=[pl.BlockSpec((tr, D), lambda i: (i, 0)),
                  pl.BlockSpec((1, D), lambda i: (0, 0)),
                  pl.BlockSpec((D, 2 * LANES), lambda i: (0, 0)),
                  pl.BlockSpec((1, 2 * LANES), lambda i: (0, 0))],
        out_specs=[pl.BlockSpec((tr, XW), lambda i: (i, 0)),
                   pl.BlockSpec((1, 1, tr), lambda i: (i, 0, 0)),
                   pl.BlockSpec((1, 1, tr), lambda i: (i, 0, 0)),
                   pl.BlockSpec((1, LANES), lambda i: (0, 0))],
        out_shape=[jax.ShapeDtypeStruct((T, XW), F32),
                   jax.ShapeDtypeStruct((T // tr, 1, tr), jnp.int32),
                   jax.ShapeDtypeStruct((T // tr, 1, tr), jnp.int32),
                   jax.ShapeDtypeStruct((1, LANES), F32)],
        scratch_shapes=[pltpu.VMEM((1, LANES), F32)],
        compiler_params=_compiler_params(("arbitrary",)),
        name="moe_router",
    )(h2d, gain.reshape(1, D), wr, br)

    seg_tiles = -(-counts[0, 0:N_GROUPS].astype(jnp.int32) // tm)
    seg_end = jnp.cumsum(seg_tiles)
    seg_start = seg_end - seg_tiles
    pos = (seg_start * tm)[gid.reshape(T)] + rank.reshape(T)
    tile_ids = jnp.arange(n_tiles, dtype=jnp.int32)
    tile_group = jnp.sum((tile_ids[:, None] >= seg_end[None, :]).astype(jnp.int32), axis=1)
    tile_group = jnp.where(tile_ids < seg_end[N_GROUPS - 1], tile_group, -1)
    pos3 = pos.reshape(T // tm, 1, tm)

    xs = pl.pallas_call(
        _moe_dispatch_kernel,
        grid_spec=pltpu.PrefetchScalarGridSpec(
            num_scalar_prefetch=1,
            grid=(T // tm,),
            in_specs=[pl.BlockSpec((1, 1, tm), lambda i, seg: (i, 0, 0), memory_space=pltpu.SMEM),
                      pl.BlockSpec((tm, XW), lambda i, seg: (i, 0))],
            out_specs=pl.BlockSpec(memory_space=pl.ANY),
            scratch_shapes=[pltpu.VMEM((tm, XW), F32), pltpu.SemaphoreType.DMA(()),
                            pltpu.SemaphoreType.DMA(())]),
        out_shape=jax.ShapeDtypeStruct((n_tiles * tm, XW), F32),
        compiler_params=_compiler_params(("arbitrary",)),
        name="moe_dispatch",
    )(seg_end, pos3, hx)

    def expert_index(j, e, grp):
        return (jnp.maximum(grp[j], 0) * EXPERTS_PER_GROUP + e, 0, 0)

    ys = pl.pallas_call(
        _moe_expert_kernel,
        grid_spec=pltpu.PrefetchScalarGridSpec(
            num_scalar_prefetch=1,
            grid=(n_tiles, EXPERTS_PER_GROUP),
            in_specs=[pl.BlockSpec((tm, XW), lambda j, e, grp: (j, 0)),
                      pl.BlockSpec((1, D, 2 * EXPERT_FF), expert_index),
                      pl.BlockSpec((1, EXPERT_FF, D), expert_index)],
            out_specs=pl.BlockSpec((tm, D), lambda j, e, grp: (j, 0)),
            scratch_shapes=[pltpu.VMEM((tm, D), BF16), pltpu.VMEM((tm, D), F32)]),
        out_shape=jax.ShapeDtypeStruct((n_tiles * tm, D), F32),
        compiler_params=_compiler_params(("parallel", "arbitrary")),
        name="moe_experts",
    )(tile_group, xs, wgu, wd)

    return pl.pallas_call(
        _moe_combine_kernel,
        grid=(T // tm,),
        in_specs=[pl.BlockSpec((1, 1, tm), lambda i: (i, 0, 0), memory_space=pltpu.SMEM),
                  pl.BlockSpec((tm, D), lambda i: (i, 0)),
                  pl.BlockSpec(memory_space=pl.ANY)],
        out_specs=pl.BlockSpec((tm, D), lambda i: (i, 0)),
        out_shape=jax.ShapeDtypeStruct((T, D), F32),
        scratch_shapes=[pltpu.VMEM((tm, D), F32), pltpu.SemaphoreType.DMA(())],
        compiler_params=_compiler_params(("arbitrary",)),
        name="moe_combine",
    )(pos3, h2d, ys)


def _dsa_prep_kernel(p_ref, qg_ref, kg_ref, lng_ref, lnb_ref, q_ref, kt_ref, v_ref, qit_ref, ki_ref, wit_ref):
    lane = lax.broadcasted_iota(jnp.int32, (1, LANES), 1)
    for h in range(ATT_HEADS):
        x = p_ref[:, DSA_Q_OFF + h * ATT_HD:DSA_Q_OFF + (h + 1) * ATT_HD]
        y = x * lax.rsqrt(jnp.mean(x * x, axis=-1, keepdims=True) + RMS_EPS) * qg_ref[...]
        q_ref[:, h * ATT_HD:(h + 1) * ATT_HD] = (y * (ATT_HD ** -0.5)).astype(BF16)
    for n in range(ATT_KV_HEADS):
        x = p_ref[:, DSA_K_OFF + n * ATT_HD:DSA_K_OFF + (n + 1) * ATT_HD]
        y = x * lax.rsqrt(jnp.mean(x * x, axis=-1, keepdims=True) + RMS_EPS) * kg_ref[...]
        kt_ref[0, n * ATT_HD:(n + 1) * ATT_HD, :] = y.T.astype(BF16)
    v_ref[...] = p_ref[:, DSA_V_OFF:DSA_V_OFF + ATT_KV_DIM].astype(BF16)
    qit_ref[0] = p_ref[:, DSA_QI_OFF:DSA_QI_OFF + IDX_HEADS * LANES].T.astype(BF16)
    x = p_ref[:, DSA_KI_OFF:DSA_KI_OFF + LANES]
    live = lane < IDX_HD
    mu = jnp.sum(x, axis=-1, keepdims=True) * (1.0 / IDX_HD)
    xc = jnp.where(live, x - mu, 0.0)
    var = jnp.sum(xc * xc, axis=-1, keepdims=True) * (1.0 / IDX_HD)
    ki = xc * lax.rsqrt(var + RMS_EPS) * lng_ref[...] + lnb_ref[...]
    ki_ref[...] = jnp.where(live, ki, 0.0).astype(BF16)
    wit_ref[0] = (p_ref[:, DSA_WI_OFF:DSA_WI_OFF + LANES] * (IDX_HEADS ** -0.5 * IDX_HD ** -0.5)).T


def _dsa_prep(proj2d, q_gain, k_gain, ln_g, ln_b):
    T, W = proj2d.shape
    tm = KEY_TILE
    assert T % tm == 0
    lng = jnp.zeros((1, LANES), F32).at[0, 0:IDX_HD].set(ln_g.astype(F32))
    lnb = jnp.zeros((1, LANES), F32).at[0, 0:IDX_HD].set(ln_b.astype(F32))
    row = lambda i: (i, 0)
    fix = lambda i: (0, 0)
    blk = lambda i: (i, 0, 0)
    return pl.pallas_call(
        _dsa_prep_kernel,
        grid=(T // tm,),
        in_specs=[pl.BlockSpec((tm, W), row),
                  pl.BlockSpec((1, ATT_HD), fix), pl.BlockSpec((1, ATT_HD), fix),
                  pl.BlockSpec((1, LANES), fix), pl.BlockSpec((1, LANES), fix)],
        out_specs=[pl.BlockSpec((tm, ATT_Q_DIM), row), pl.BlockSpec((1, ATT_KV_DIM, tm), blk),
                   pl.BlockSpec((tm, ATT_KV_DIM), row), pl.BlockSpec((1, IDX_HEADS * LANES, tm), blk),
                   pl.BlockSpec((tm, LANES), row), pl.BlockSpec((1, LANES, tm), blk)],
        out_shape=[jax.ShapeDtypeStruct((T, ATT_Q_DIM), BF16), jax.ShapeDtypeStruct((T // tm, ATT_KV_DIM, tm), BF16),
                   jax.ShapeDtypeStruct((T, ATT_KV_DIM), BF16),
                   jax.ShapeDtypeStruct((T // tm, IDX_HEADS * LANES, tm), BF16),
                   jax.ShapeDtypeStruct((T, LANES), BF16), jax.ShapeDtypeStruct((T // tm, LANES, tm), F32)],
        compiler_params=_compiler_params(("parallel",)),
        name="dsa_prep",
    )(proj2d, q_gain.reshape(1, ATT_HD), k_gain.reshape(1, ATT_HD), lng, lnb)


def _dsa_kernel(q_ref, qit_ref, wit_ref, kt_ref, v_ref, ki_ref, bias_ref, bmax_ref, o_ref,
                sc_ref, mask_ref, qg_ref, m_ref, l_ref, acc_ref, knorm_ref, *, n_select, live_rows):
    i = pl.program_id(1)
    TQ = ATT_TILE
    TK = KEY_TILE
    HALVES = TK // LANES

    @pl.when(i == 0)
    def _():
        def body(kt, mx):
            kk = kt_ref[0, kt].astype(F32)
            return tuple(jnp.maximum(mx[n], jnp.sum(jnp.square(kk[n * ATT_HD:(n + 1) * ATT_HD, :]),
                                                    axis=0, keepdims=True)) for n in range(ATT_KV_HEADS))

        mx = lax.fori_loop(0, kt_ref.shape[1], body,
                           tuple(jnp.zeros((1, TK), F32) for _ in range(ATT_KV_HEADS)))
        for n in range(ATT_KV_HEADS):
            knorm_ref[n] = jnp.broadcast_to(jnp.sqrt(jnp.max(mx[n], axis=-1, keepdims=True)), (1, LANES))

    @pl.when(i * TQ >= live_rows)
    def _():
        o_ref[...] = jnp.zeros_like(o_ref)

    @pl.when(i * TQ < live_rows)
    def _():
        _dsa_tile(i, q_ref, qit_ref, wit_ref, kt_ref, v_ref, ki_ref, bias_ref, bmax_ref, o_ref,
                  sc_ref, mask_ref, qg_ref, m_ref, l_ref, acc_ref, knorm_ref, n_select, TQ, TK, HALVES)


def _dsa_tile(i, q_ref, qit_ref, wit_ref, kt_ref, v_ref, ki_ref, bias_ref, bmax_ref, o_ref,
              sc_ref, mask_ref, qg_ref, m_ref, l_ref, acc_ref, knorm_ref, n_select, TQ, TK, HALVES):
    assert TQ == TK
    nkt = i + 1
    fmin = float(jnp.finfo(F32).min)
    kf = float(n_select)

    chunk_shift = int(math.log2(CHUNK))
    q_pos = i * TQ + lax.broadcasted_iota(jnp.int32, (1, TQ), 1)
    q_chunk = lax.shift_right_logical(q_pos, chunk_shift)
    k_row = lax.broadcasted_iota(jnp.int32, (TK, 1), 0)

    def halves(x):
        return [x[:, j * LANES:(j + 1) * LANES] for j in range(HALVES)]

    def fold(x, op):
        parts = [x[r * SUBLANES:(r + 1) * SUBLANES, :] for r in range(TK // SUBLANES)]
        while len(parts) > 1:
            parts = [op(parts[j], parts[j + 1]) for j in range(0, len(parts), 2)]
        return parts[0]

    def score_tiles(kts):
        for kt in kts:
            keys = ki_ref[0, pl.ds(pl.multiple_of(kt * TK, TK), TK), :]
            acc = None
            for h in range(IDX_HEADS):
                d = jnp.dot(keys, qit_ref[0, 0, h * LANES:(h + 1) * LANES, :], preferred_element_type=F32)
                term = wit_ref[0, 0, h:h + 1, :] * jnp.maximum(d, 0.0)
                acc = term if acc is None else acc + term
            sc_ref[kt] = acc
            mask_ref[kt] = acc.T

    def tile_loop(n, fn):
        def body(kt, carry):
            fn([kt])
            return carry

        lax.fori_loop(0, n, body, 0)

    def paired_loop(n, fn):
        def body(j, carry):
            fn([2 * j, 2 * j + 1])
            return carry

        lax.fori_loop(0, lax.shift_right_logical(n, 1), body, 0)

        @pl.when(lax.bitwise_and(n, 1) == 1)
        def _():
            fn([n - 1])

    paired_loop(nkt, score_tiles)

    def mask_inadmissible(kt):
        k_pos = kt * TK + k_row
        adm = (lax.shift_right_logical(k_pos, chunk_shift) <= q_chunk) & (k_pos >= META_PAD)
        sc_ref[kt] = jnp.where(adm, sc_ref[kt], NEG_INF)

    mask_inadmissible(0)

    @pl.when(nkt > 1)
    def _():
        mask_inadmissible(nkt - 1)

    n_adm = CHUNK * (q_chunk + 1) - META_PAD
    need = n_adm > n_select

    def count_where(pred):
        def one(kt):
            return fold(jnp.where(pred(sc_ref[kt], kt), 1.0, 0.0), jnp.add)

        def two(j, acc):
            return acc + (one(2 * j) + one(2 * j + 1))

        acc = lax.fori_loop(0, lax.shift_right_logical(nkt, 1), two, jnp.zeros((SUBLANES, TQ), F32))
        acc = lax.cond(lax.bitwise_and(nkt, 1) == 1, lambda a: a + one(nkt - 1), lambda a: a, acc)
        return jnp.sum(acc, axis=0, keepdims=True)

    def count_ge(t):
        return count_where(lambda s, kt: s >= t)

    def minmax_body(kt, carry):
        mn, mx = carry
        s = sc_ref[kt]
        mn = jnp.minimum(mn, fold(jnp.where(s == NEG_INF, jnp.inf, s), jnp.minimum))
        mx = jnp.maximum(mx, fold(s, jnp.maximum))
        return mn, mx

    mn_l, mx_l = lax.fori_loop(0, nkt, minmax_body,
                               (jnp.full((SUBLANES, TQ), jnp.inf, F32), jnp.full((SUBLANES, TQ), NEG_INF, F32)))
    s_min = jnp.min(mn_l, axis=0, keepdims=True)
    s_max = jnp.max(mx_l, axis=0, keepdims=True)
    lo0 = s_min
    hi0 = s_max + (s_max - s_min) + 1.0
    cnt0 = n_adm.astype(F32)
    active0 = jnp.where(need, 1.0, 0.0)

    def any_active(active):
        return jnp.max(active) > 0.0

    def bis_step(lo, hi, cnt, active):
        mid = lo + (hi - lo) * 0.5
        c = count_ge(mid)
        ge = c >= kf
        moving = (mid > lo) & (mid < hi) & (active > 0.0)
        lo_n = jnp.where(moving & ge, mid, lo)
        hi_n = jnp.where(moving & (~ge), mid, hi)
        cnt_n = jnp.where(moving & ge, c, cnt)
        active_n = jnp.where(moving & (cnt_n != kf), 1.0, 0.0)
        return lo_n, hi_n, cnt_n, active_n

    def bis_body(st):
        it, rest = st[0], st[1:]
        for _ in range(BISECT_UNROLL):
            rest = bis_step(*rest)
        return (it + BISECT_UNROLL,) + rest

    _, lo, hi, cnt, _ = lax.while_loop(
        lambda st: jnp.logical_and(st[0] < BISECT_STEPS, any_active(st[4])), bis_body,
        (jnp.int32(0), lo0, hi0, cnt0, active0))

    def snap_body(st):
        lo, hi, cnt, active = st

        def body(kt, mx):
            s = sc_ref[kt]
            return jnp.maximum(mx, fold(jnp.where(s < hi, s, NEG_INF), jnp.maximum))

        v1 = jnp.max(lax.fori_loop(0, nkt, body, jnp.full((SUBLANES, TQ), NEG_INF, F32)),
                     axis=0, keepdims=True)
        c1 = count_ge(v1)
        on = active > 0.0
        hit = on & (c1 >= kf)
        lo_n = jnp.where(hit, v1, lo)
        cnt_n = jnp.where(hit, c1, cnt)
        hi_n = jnp.where(on & (~hit), v1, hi)
        return lo_n, hi_n, cnt_n, jnp.where(on & (~hit), 1.0, 0.0)

    unsettled = jnp.where(need & (cnt != kf), 1.0, 0.0)
    lo, hi, cnt, _ = lax.while_loop(lambda st: any_active(st[3]), snap_body, (lo, hi, cnt, unsettled))

    thr = jnp.where(need, lo, fmin)
    tied = need & (cnt > kf)
    big_pos = jnp.int32(2 ** 30)

    def tie_break():
        want = kf - count_where(lambda s, kt: s > thr)

        def body(_, st):
            jl, jh = st
            jm = lax.shift_right_arithmetic(jl + jh, 1)
            c = count_where(lambda s, kt: (s == thr) & (kt * TK + k_row <= jm))
            ok = c >= want
            return jnp.where(ok, jl, jm), jnp.where(ok, jm, jh)

        n_keys = sc_ref.shape[0] * TK
        steps = int(math.ceil(math.log2(n_keys + 1))) + 1
        jl0 = jnp.full((1, TQ), -1, jnp.int32)
        jh0 = jnp.full((1, TQ), n_keys, jnp.int32)
        _, jh = lax.fori_loop(0, steps, body, (jl0, jh0))
        return jnp.where(tied, jh, big_pos)

    pos_cap = lax.cond(jnp.max(jnp.where(tied, 1.0, 0.0)) > 0.0, tie_break,
                       lambda: jnp.full((1, TQ), big_pos, jnp.int32))

    def lane_wide(row):
        return jnp.broadcast_to(row, (LANES, TQ)).T

    thr_w = lane_wide(thr)
    cap_w = lane_wide(pos_cap)
    chunk_w = lane_wide(q_chunk)
    k_lane = lax.broadcasted_iota(jnp.int32, (1, LANES), 1)

    def to_mask(kt, edge_tile):
        out = []
        for j, s in enumerate(halves(mask_ref[kt])):
            k_pos = kt * TK + j * LANES + k_lane
            sel = (s > thr_w) | ((s == thr_w) & (k_pos <= cap_w))
            if edge_tile:
                sel = sel & (lax.shift_right_logical(k_pos, chunk_shift) <= chunk_w) & (k_pos >= META_PAD)
            out.append(jnp.where(sel, 0.0, NEG_INF))
        mask_ref[kt] = jnp.concatenate(out, axis=1)

    def inner_mask(kt, carry):
        to_mask(kt, False)
        return carry

    lax.fori_loop(1, nkt - 1, inner_mask, 0)
    to_mask(0, True)

    @pl.when(nkt > 1)
    def _():
        to_mask(nkt - 1, True)

    for n in range(ATT_KV_HEADS):
        for g in range(ATT_GROUP):
            h = n * ATT_GROUP + g
            qg_ref[n, g * TQ:(g + 1) * TQ, :] = q_ref[0, :, h * ATT_HD:(h + 1) * ATT_HD]
    l_ref[...] = jnp.zeros_like(l_ref)
    acc_ref[...] = jnp.zeros_like(acc_ref)

    kv_heads = range(ATT_KV_HEADS)

    def raw_logits(kts):
        return [[jnp.dot(qg_ref[n], kt_ref[0, kt, n * ATT_HD:(n + 1) * ATT_HD, :], preferred_element_type=F32)
                 for n in kv_heads] for kt in kts]

    def masked_logits(s, mask, n, g, b0):
        sg = s[g * TQ:(g + 1) * TQ, :] + mask
        if b0 is not None:
            h = n * ATT_GROUP + g
            sg = sg + jnp.concatenate([bias_ref[h, b0 + j] for j in range(HALVES)], axis=1)
        return sg

    def max_pass(kts, b0):
        raw = raw_logits(kts)
        masks = [mask_ref[kt] for kt in kts]
        for n in kv_heads:
            for g in range(ATT_GROUP):
                rows = slice(g * TQ, (g + 1) * TQ)
                m = m_ref[n, rows, :]
                for t in range(len(kts)):
                    for part in halves(masked_logits(raw[t][n], masks[t], n, g, b0)):
                        m = jnp.maximum(m, part)
                m_ref[n, rows, :] = m

    def exp_pass(kts, b0):
        raw = raw_logits(kts)
        masks = [mask_ref[kt] for kt in kts]
        for n in kv_heads:
            ps = [[] for _ in kts]
            for g in range(ATT_GROUP):
                rows = slice(g * TQ, (g + 1) * TQ)
                m = m_ref[n, rows, :]
                l_new = l_ref[n, rows, :]
                for t in range(len(kts)):
                    parts = [jnp.exp(part - m) for part in halves(masked_logits(raw[t][n], masks[t], n, g, b0))]
                    l_new = l_new + functools.reduce(lambda a, b: a + b, parts)
                    ps[t].append(jnp.concatenate(parts, axis=1).astype(BF16))
                l_ref[n, rows, :] = l_new
            pv = None
            for t, kt in enumerate(kts):
                vv = v_ref[0, pl.ds(pl.multiple_of(kt * TK, TK), TK), n * ATT_HD:(n + 1) * ATT_HD]
                d = jnp.dot(jnp.concatenate(ps[t], axis=0), vv, preferred_element_type=F32)
                pv = d if pv is None else pv + d
            acc_ref[n] += pv

    def run_pass(tile_fn):
        paired_loop(jnp.maximum(i - 1, 0), lambda kts: tile_fn(kts, None))

        @pl.when(i >= 1)
        def _():
            tile_fn([i - 1], 0)

        tile_fn([i], HALVES)

    for n in range(ATT_KV_HEADS):
        for g in range(ATT_GROUP):
            h = n * ATT_GROUP + g
            qh = qg_ref[n, g * TQ:(g + 1) * TQ, :].astype(F32)
            q_len = jnp.sqrt(jnp.sum(qh * qh, axis=-1, keepdims=True))
            bound = q_len * knorm_ref[n] * SHIFT_SLACK + (bmax_ref[h] + SHIFT_SLACK - 1.0)
            m_ref[n, g * TQ:(g + 1) * TQ, :] = jnp.broadcast_to(bound, (TQ, LANES))
    run_pass(exp_pass)

    def smallest_row_sum():
        return functools.reduce(jnp.minimum, [jnp.min(jnp.sum(l_ref[n], axis=-1, keepdims=True))
                                              for n in range(ATT_KV_HEADS)])

    @pl.when(jnp.logical_not(smallest_row_sum() > ROW_SUM_FLOOR))
    def _():
        m_ref[...] = jnp.full(m_ref.shape, -1e30, F32)
        run_pass(max_pass)
        for n in range(ATT_KV_HEADS):
            m_ref[n] = jnp.broadcast_to(jnp.max(m_ref[n], axis=-1, keepdims=True), m_ref.shape[1:])
        l_ref[...] = jnp.zeros_like(l_ref)
        acc_ref[...] = jnp.zeros_like(acc_ref)
        run_pass(exp_pass)

    for n in range(ATT_KV_HEADS):
        inv_l = 1.0 / jnp.sum(l_ref[n], axis=-1, keepdims=True)
        for g in range(ATT_GROUP):
            h = n * ATT_GROUP + g
            rows = slice(g * TQ, (g + 1) * TQ)
            o_ref[0, :, h * ATT_HD:(h + 1) * ATT_HD] = (acc_ref[n, rows, :] * inv_l[rows, :]).astype(o_ref.dtype)


def _t5_bucket(rel):
    nb = REL_BUCKETS // 2
    max_exact = nb // 2
    ret = jnp.where(rel > 0, nb, 0)
    n = jnp.abs(rel)
    nf = jnp.maximum(n, 1).astype(jnp.float32)
    large = max_exact + (jnp.log(nf / max_exact) / math.log(REL_MAX_DIST / max_exact)
                         * (nb - max_exact)).astype(jnp.int32)
    large = jnp.minimum(large, nb - 1)
    return ret + jnp.where(n < max_exact, n, large)


def _dsa_core(q, kt, v, qit, ki, wit, rel_bias, n_select, live_rows):
    B, LP, _ = q.shape
    nq = LP // ATT_TILE
    nkt = LP // KEY_TILE
    a = jnp.arange(ATT_TILE)[:, None]
    xx = jnp.arange(2 * KEY_TILE)[None, :]
    rel = (xx - KEY_TILE) - a
    far = rel_bias.astype(F32)[REL_BUCKETS // 2 - 1]
    onehot = jax.nn.one_hot(_t5_bucket(rel), REL_BUCKETS, dtype=F32)
    table = jnp.einsum("abk,kh->abh", onehot, rel_bias.astype(F32),
                       precision=lax.Precision.HIGHEST) - far
    bias_near = jnp.transpose(table.reshape(ATT_TILE, 2 * KEY_TILE // LANES, LANES, ATT_HEADS), (3, 1, 0, 2))
    bias_max = jnp.broadcast_to(jnp.maximum(jnp.max(table, axis=(0, 1)), 0.0)[:, None, None],
                                (ATT_HEADS, 1, LANES))
    kern = functools.partial(_dsa_kernel, n_select=n_select, live_rows=live_rows)
    tile = lambda b, i: (b, i, 0)
    tile4 = lambda b, i: (b, i, 0, 0)
    whole3 = lambda b, i: (b, 0, 0)
    whole4 = lambda b, i: (b, 0, 0, 0)
    once = pl.Buffered(1)
    return pl.pallas_call(
        kern,
        grid=(B, nq),
        in_specs=[pl.BlockSpec((1, ATT_TILE, ATT_Q_DIM), tile),
                  pl.BlockSpec((1, 1, IDX_HEADS * LANES, ATT_TILE), tile4),
                  pl.BlockSpec((1, 1, LANES, ATT_TILE), tile4),
                  pl.BlockSpec((1, nkt, ATT_KV_DIM, KEY_TILE), whole4, pipeline_mode=once),
                  pl.BlockSpec((1, LP, ATT_KV_DIM), whole3, pipeline_mode=once),
                  pl.BlockSpec((1, LP, LANES), whole3, pipeline_mode=once),
                  pl.BlockSpec((ATT_HEADS, 2 * KEY_TILE // LANES, ATT_TILE, LANES), lambda b, i: (0, 0, 0, 0),
                               pipeline_mode=once),
                  pl.BlockSpec((ATT_HEADS, 1, LANES), lambda b, i: (0, 0, 0))],
        out_specs=pl.BlockSpec((1, ATT_TILE, ATT_Q_DIM), tile),
        out_shape=jax.ShapeDtypeStruct((B, LP, ATT_Q_DIM), BF16),
        scratch_shapes=[pltpu.VMEM((nkt, KEY_TILE, ATT_TILE), F32),
                        pltpu.VMEM((nkt, ATT_TILE, KEY_TILE), F32),
                        pltpu.VMEM((ATT_KV_HEADS, ATT_GROUP * ATT_TILE, ATT_HD), BF16),
                        pltpu.VMEM((ATT_KV_HEADS, ATT_GROUP * ATT_TILE, LANES), F32),
                        pltpu.VMEM((ATT_KV_HEADS, ATT_GROUP * ATT_TILE, LANES), F32),
                        pltpu.VMEM((ATT_KV_HEADS, ATT_GROUP * ATT_TILE, ATT_HD), F32),
                        pltpu.VMEM((ATT_KV_HEADS, 1, LANES), F32)],
        compiler_params=_compiler_params(("arbitrary", "arbitrary")),
        name="dsa_core",
    )(q, qit, wit, kt, v, ki, bias_near, bias_max)


def _gdn_in_weight(w_in):
    D = w_in.shape[0]
    w = jnp.zeros((D, GDN_PROJ_W), F32).at[:, 0:w_in.shape[1]].set(w_in.astype(F32))
    return w.astype(BF16)


def _dsa_in_weight(w_in):
    D = w_in.shape[0]
    w_in = w_in.astype(BF16)
    zeros = lambda n: jnp.zeros((D, n), BF16)
    src = ATT_Q_DIM + 2 * ATT_KV_DIM
    parts = [w_in[:, 0:src]]
    for h in range(IDX_HEADS):
        parts += [w_in[:, src + h * IDX_HD:src + (h + 1) * IDX_HD], zeros(LANES - IDX_HD)]
    src += IDX_HEADS * IDX_HD
    parts += [w_in[:, src:src + IDX_HD], zeros(LANES - IDX_HD),
              w_in[:, src + IDX_HD:src + IDX_HD + IDX_HEADS], zeros(LANES - IDX_HEADS)]
    w = jnp.concatenate(parts, axis=1)
    assert w.shape[1] == DSA_PROJ_W
    return w


def kernel(x, meta_tokens, norm_mix, norm_ffn, rel_bias, gdn_w_in, gdn_conv, gdn_a_log, gdn_dt_bias,
           gdn_o_norm, gdn_w_out, dsa_w_in, dsa_q_norm, dsa_k_norm, dsa_idx_ln_g, dsa_idx_ln_b, dsa_w_out,
           moe_w_group, moe_b_group, moe_w_expert, moe_b_expert, moe_w_gate_up, moe_w_down):
    B, S, D = x.shape
    depth = norm_mix.shape[0]
    n_select = min(TOPK_MAX, S // 4)
    off = META_PAD + N_META
    LP = -(-(off + S) // ROW_ALIGN) * ROW_ALIGN
    meta = jnp.broadcast_to(meta_tokens.astype(x.dtype)[None], (B, N_META, D))
    h = jnp.concatenate([jnp.zeros((B, META_PAD, D), x.dtype), meta, x,
                         jnp.zeros((B, LP - off - S, D), x.dtype)], axis=1)
    h = h.reshape(B * LP, D)
    for i in range(depth):
        j = i // 2
        if i % 2 == 0:
            proj = _norm_matmul(h, norm_mix[i], _gdn_in_weight(gdn_w_in[j]))
            o = _gdn_core(proj.reshape(B, LP, GDN_PROJ_W), gdn_conv[j].astype(F32), gdn_a_log[j],
                          gdn_dt_bias[j], gdn_o_norm[j])
            h = _matmul_residual(o.reshape(B * LP, GDN_V_DIM), gdn_w_out[j].astype(BF16), h)
        else:
            proj = _norm_matmul(h, norm_mix[i], _dsa_in_weight(dsa_w_in[j]))
            q, kt, v, qit, ki, wit = _dsa_prep(proj, dsa_q_norm[j], dsa_k_norm[j], dsa_idx_ln_g[j], dsa_idx_ln_b[j])
            r3 = lambda t: t.reshape(B, LP, t.shape[-1])
            r4 = lambda t: t.reshape(B, LP // KEY_TILE, t.shape[-2], KEY_TILE)
            o = _dsa_core(r3(q), r4(kt), r3(v), r4(qit), r3(ki), r4(wit), rel_bias, n_select, off + S)
            h = _matmul_residual(o.reshape(B * LP, ATT_Q_DIM), dsa_w_out[j].astype(BF16), h)
        h = _moe(h, norm_ffn[i], moe_w_group[i], moe_b_group[i], moe_w_expert[i], moe_b_expert[i],
                 moe_w_gate_up[i], moe_w_down[i])
    return h.reshape(B, LP, D)[:, off:off + S]
```

```python
import functools
import math

import jax
import jax.numpy as jnp
from jax import lax
from jax.experimental import pallas as pl
from jax.experimental.pallas import tpu as pltpu

F32 = jnp.float32
BF16 = jnp.bfloat16

LANES = 128
SUBLANES = 8
VMEM_LIMIT_BYTES = 56 * 1024 * 1024

CHUNK = 64
N_META = 16
META_PAD = (-N_META) % CHUNK
RMS_EPS = 1e-6
L2_EPS = 1e-6

GDN_HEADS = 8
GDN_DK = 128
GDN_DV = 128
GDN_CONV = 4
GDN_QK_DIM = GDN_HEADS * GDN_DK
GDN_V_DIM = GDN_HEADS * GDN_DV
GDN_CONV_DIM = 2 * GDN_QK_DIM + GDN_V_DIM
GDN_GATE_OFF = GDN_CONV_DIM + GDN_V_DIM
GDN_PROJ_W = GDN_GATE_OFF + LANES

ATT_HEADS = 8
ATT_KV_HEADS = 2
ATT_GROUP = ATT_HEADS // ATT_KV_HEADS
ATT_HD = 128
ATT_Q_DIM = ATT_HEADS * ATT_HD
ATT_KV_DIM = ATT_KV_HEADS * ATT_HD
IDX_HEADS = 8
IDX_HD = 64
TOPK_MAX = 256
REL_BUCKETS = 32
REL_MAX_DIST = 128

DSA_Q_OFF = 0
DSA_K_OFF = ATT_Q_DIM
DSA_V_OFF = DSA_K_OFF + ATT_KV_DIM
DSA_QI_OFF = DSA_V_OFF + ATT_KV_DIM
DSA_KI_OFF = DSA_QI_OFF + IDX_HEADS * LANES
DSA_WI_OFF = DSA_KI_OFF + LANES
DSA_PROJ_W = DSA_WI_OFF + LANES

N_GROUPS = 4
EXPERTS_PER_GROUP = 8
N_EXPERTS = N_GROUPS * EXPERTS_PER_GROUP
EXPERT_FF = 256

MOE_ROUTER_TILES = (512, 256, 128)
MOE_ROW_TILES = (512, 256, 128)

ATT_TILE = 256
KEY_TILE = 256
ROW_ALIGN = KEY_TILE
FRONT_PAD = ROW_ALIGN - N_META
assert FRONT_PAD % CHUNK == META_PAD
BISECT_UNROLL = 4
BISECT_STEPS = 24
SHIFT_SLACK = 1.001
ROW_SUM_FLOOR = 1e-30
NEG_INF = float("-inf")


def _pick_tile(n, candidates):
    for c in candidates:
        if n % c == 0:
            return c
    raise ValueError(f"no tile for {n}")


def _compiler_params(semantics):
    return pltpu.CompilerParams(dimension_semantics=semantics, vmem_limit_bytes=VMEM_LIMIT_BYTES)


def _silu(x):
    return x * jax.nn.sigmoid(x)


def _norm_matmul_kernel(x_ref, g_ref, w_ref, o_ref):
    x = x_ref[...]
    y = x * lax.rsqrt(jnp.mean(x * x, axis=-1, keepdims=True) + RMS_EPS) * g_ref[...]
    o_ref[...] = jnp.dot(y.astype(BF16), w_ref[...], preferred_element_type=F32)


def _norm_matmul(h2d, gain, w_bf16):
    T, D = h2d.shape
    N = w_bf16.shape[1]
    tm = _pick_tile(T, (256, 128))
    return pl.pallas_call(
        _norm_matmul_kernel,
        grid=(T // tm,),
        in_specs=[pl.BlockSpec((tm, D), lambda i: (i, 0)),
                  pl.BlockSpec((1, D), lambda i: (0, 0)),
                  pl.BlockSpec((D, N), lambda i: (0, 0))],
        out_specs=pl.BlockSpec((tm, N), lambda i: (i, 0)),
        out_shape=jax.ShapeDtypeStruct((T, N), F32),
        compiler_params=_compiler_params(("parallel",)),
        name="norm_matmul",
    )(h2d, gain.reshape(1, D), w_bf16)


def _matmul_residual_kernel(a_ref, w_ref, r_ref, o_ref):
    o_ref[...] = r_ref[...] + jnp.dot(a_ref[...], w_ref[...], preferred_element_type=F32)


def _matmul_residual(a_bf16, w_bf16, res):
    T, K = a_bf16.shape
    N = w_bf16.shape[1]
    tm = _pick_tile(T, (512, 256, 128))
    return pl.pallas_call(
        _matmul_residual_kernel,
        grid=(T // tm,),
        in_specs=[pl.BlockSpec((tm, K), lambda i: (i, 0)),
                  pl.BlockSpec((K, N), lambda i: (0, 0)),
                  pl.BlockSpec((tm, N), lambda i: (i, 0))],
        out_specs=pl.BlockSpec((tm, N), lambda i: (i, 0)),
        out_shape=jax.ShapeDtypeStruct((T, N), F32),
        compiler_params=_compiler_params(("parallel",)),
        name="matmul_residual",
    )(a_bf16, w_bf16, res)


def _nt_dot(a, b):
    return lax.dot_general(a, b, (((1,), (1,)), ((), ())), preferred_element_type=F32)


def _tn_dot(a, b):
    return lax.dot_general(a, b, (((0,), (0,)), ((), ())), preferred_element_type=F32)


def _gdn_kernel(proj_ref, conv_ref, gate_ref, ogain_ref, o_ref, state_ref, xbuf_ref, qkv_ref):
    c = pl.program_id(1)
    C = CHUNK

    @pl.when(c == 0)
    def _():
        state_ref[...] = jnp.zeros_like(state_ref)
        xbuf_ref[0:SUBLANES, :] = jnp.zeros((SUBLANES, GDN_CONV_DIM), F32)

    row = c * C + lax.broadcasted_iota(jnp.int32, (C, 1), 0)
    live = row >= FRONT_PAD
    u = jnp.where(live, proj_ref[0, :, 0:GDN_CONV_DIM], 0.0)
    xbuf_ref[SUBLANES:SUBLANES + C, :] = u
    conv = conv_ref[0:1, :] * xbuf_ref[SUBLANES - 3:SUBLANES - 3 + C, :]
    for j in range(1, GDN_CONV):
        conv = conv + conv_ref[j:j + 1, :] * xbuf_ref[SUBLANES - 3 + j:SUBLANES - 3 + j + C, :]
    xbuf_ref[0:SUBLANES, :] = u[C - SUBLANES:C, :]
    qkv_ref[...] = _silu(conv)
    qkv = qkv_ref

    gates = jnp.where(live, proj_ref[0, :, GDN_GATE_OFF:GDN_GATE_OFF + LANES], 0.0)
    beta = jax.nn.sigmoid(gates)
    neg_rate = -jnp.exp(gate_ref[0:1, :])
    g = neg_rate * jax.nn.softplus(gates + gate_ref[1:2, :])
    r_i = lax.broadcasted_iota(jnp.int32, (C, C), 0)
    c_i = lax.broadcasted_iota(jnp.int32, (C, C), 1)
    causal = c_i <= r_i
    strict = c_i < r_i
    gc = jnp.dot(causal.astype(F32), g, preferred_element_type=F32,
                 precision=lax.Precision.HIGHEST)
    gc_t = gc.T
    eg = jnp.exp(gc)
    g_last = gc[C - 1:C, :]
    eg_last = jnp.exp(g_last)
    ek = jnp.exp(g_last - gc)

    heads = range(GDN_HEADS)
    dot = functools.partial(jnp.dot, preferred_element_type=F32)
    col = lambda a, h: a[:, GDN_HEADS + h:GDN_HEADS + h + 1]
    qs, ks, k16s, xs, decays = [], [], [], [], []
    for h in heads:
        lo = h * GDN_DK
        q = qkv[:, lo:lo + GDN_DK]
        k = qkv[:, GDN_QK_DIM + lo:GDN_QK_DIM + lo + GDN_DK]
        v = qkv[:, 2 * GDN_QK_DIM + h * GDN_DV:2 * GDN_QK_DIM + (h + 1) * GDN_DV]
        q = q * lax.rsqrt(jnp.sum(q * q, axis=-1, keepdims=True) + L2_EPS) * (GDN_DK ** -0.5)
        k = k * lax.rsqrt(jnp.sum(k * k, axis=-1, keepdims=True) + L2_EPS)
        kb = k * beta[:, h:h + 1]
        vb = v * beta[:, h:h + 1]
        qs.append(q)
        ks.append(k)
        k16s.append(k.astype(BF16))
        xs.append((kb, jnp.concatenate([vb, kb * col(eg, h)], axis=1)))
        decays.append(jnp.exp(jnp.where(causal, col(gc, h) - gc_t[GDN_HEADS + h:GDN_HEADS + h + 1, :],
                                        NEG_INF)))
    kk = [_nt_dot(xs[h][0].astype(BF16), k16s[h]) for h in heads]
    qk = [_nt_dot(qs[h].astype(BF16), k16s[h]) for h in heads]
    ns = [-jnp.where(strict, kk[h] * decays[h], 0.0) for h in heads]
    xs = [x for _, x in xs]
    for it in range(6):
        n16 = [n.astype(BF16) for n in ns]
        xs = [xs[h] + dot(n16[h], xs[h].astype(BF16)) for h in heads]
        if it < 5:
            ns = [dot(n16[h], n16[h]) for h in heads]
    qk = [jnp.where(causal, qk[h] * decays[h], 0.0).astype(BF16) for h in heads]
    states = [state_ref[h] for h in heads]
    s16 = [s.astype(BF16) for s in states]
    ws = [dot(xs[h][:, GDN_DV:2 * GDN_DV].astype(BF16), s16[h]) for h in heads]
    qs_s = [dot((qs[h] * col(eg, h)).astype(BF16), s16[h]) for h in heads]
    v_new = [(xs[h][:, 0:GDN_DV] - ws[h]).astype(BF16) for h in heads]
    os_ = [qs_s[h] + dot(qk[h], v_new[h]) for h in heads]
    kv = [_tn_dot((ks[h] * col(ek, h)).astype(BF16), v_new[h]) for h in heads]
    for h in heads:
        state_ref[h] = states[h] * col(eg_last, h) + kv[h]
        z = proj_ref[0, :, GDN_CONV_DIM + h * GDN_DV:GDN_CONV_DIM + (h + 1) * GDN_DV]
        o = os_[h]
        on = o * lax.rsqrt(jnp.mean(o * o, axis=-1, keepdims=True) + RMS_EPS) * ogain_ref[...]
        o_ref[0, :, h * GDN_DV:(h + 1) * GDN_DV] = (on * _silu(z)).astype(o_ref.dtype)


def _gdn_core(proj, conv_w, a_log, dt_bias, o_gain):
    B, LP, W = proj.shape
    nc = LP // CHUNK
    gate = jnp.zeros((2, LANES), F32)
    gate = gate.at[0, GDN_HEADS:2 * GDN_HEADS].set(a_log.astype(F32))
    gate = gate.at[1, GDN_HEADS:2 * GDN_HEADS].set(dt_bias.astype(F32))
    return pl.pallas_call(
        _gdn_kernel,
        grid=(B, nc),
        in_specs=[pl.BlockSpec((1, CHUNK, W), lambda b, c: (b, c, 0)),
                  pl.BlockSpec((GDN_CONV, GDN_CONV_DIM), lambda b, c: (0, 0)),
                  pl.BlockSpec((2, LANES), lambda b, c: (0, 0)),
                  pl.BlockSpec((1, GDN_DV), lambda b, c: (0, 0))],
        out_specs=pl.BlockSpec((1, CHUNK, GDN_V_DIM), lambda b, c: (b, c, 0)),
        out_shape=jax.ShapeDtypeStruct((B, LP, GDN_V_DIM), BF16),
        scratch_shapes=[pltpu.VMEM((GDN_HEADS, GDN_DK, GDN_DV), F32),
                        pltpu.VMEM((SUBLANES + CHUNK, GDN_CONV_DIM), F32),
                        pltpu.VMEM((CHUNK, GDN_CONV_DIM), F32)],
        compiler_params=_compiler_params(("parallel", "arbitrary")),
        name="gdn_core",
    )(proj, conv_w, gate, o_gain.reshape(1, GDN_DV))


def _moe_router_kernel(h_ref, gain_ref, wr_ref, br_ref, hx_ref, gid_ref, rank_ref, cnt_ref, carry_ref):
    i = pl.program_id(0)
    TM = h_ref.shape[0]
    lane = lax.broadcasted_iota(jnp.int32, (1, LANES), 1)

    @pl.when(i == 0)
    def _():
        carry_ref[...] = jnp.zeros_like(carry_ref)

    x = h_ref[...]
    xn = x * lax.rsqrt(jnp.mean(x * x, axis=-1, keepdims=True) + RMS_EPS) * gain_ref[...]
    logits = jnp.dot(xn, wr_ref[...], preferred_element_type=F32,
                     precision=lax.Precision.HIGHEST) + br_ref[...]
    e_log = logits[:, 0:LANES]
    g_log = jnp.where(lane < N_GROUPS, logits[:, LANES:2 * LANES], NEG_INF)
    g_max = jnp.max(g_log, axis=-1, keepdims=True)
    g_sel = jnp.min(jnp.where(g_log == g_max, lane, LANES), axis=-1, keepdims=True)
    g_sel = jnp.minimum(g_sel, N_GROUPS - 1)
    g_w = 1.0 / jnp.sum(jnp.exp(g_log - g_max), axis=-1, keepdims=True)
    in_grp = (lane >= g_sel * EXPERTS_PER_GROUP) & (lane < (g_sel + 1) * EXPERTS_PER_GROUP)
    l0 = jnp.where(in_grp, e_log, NEG_INF)
    m1 = jnp.max(l0, axis=-1, keepdims=True)
    i1 = jnp.min(jnp.where(l0 == m1, lane, LANES), axis=-1, keepdims=True)
    l1 = jnp.where(lane == i1, NEG_INF, l0)
    m2 = jnp.max(l1, axis=-1, keepdims=True)
    i2 = jnp.min(jnp.where(l1 == m2, lane, LANES), axis=-1, keepdims=True)
    p2 = jnp.exp(m2 - m1)
    w1 = g_w / (1.0 + p2)
    w2 = g_w * p2 / (1.0 + p2)
    j1 = i1 - g_sel * EXPERTS_PER_GROUP
    j2 = i2 - g_sel * EXPERTS_PER_GROUP
    hx_ref[:, 0:xn.shape[1]] = xn
    hx_ref[:, xn.shape[1]:] = jnp.where(lane == j1, w1, 0.0) + jnp.where(lane == j2, w2, 0.0)

    onehot = jnp.where(lane == g_sel, 1.0, 0.0)
    r_i = lax.broadcasted_iota(jnp.int32, (TM, TM), 0)
    c_i = lax.broadcasted_iota(jnp.int32, (TM, TM), 1)
    before = jnp.where(c_i < r_i, 1.0, 0.0).astype(BF16)
    prior = jnp.dot(before, onehot.astype(BF16), preferred_element_type=F32) + carry_ref[...]
    rank = jnp.sum(jnp.where(lane == g_sel, prior, 0.0), axis=-1, keepdims=True)
    carry_ref[...] += jnp.sum(onehot, axis=0, keepdims=True)
    cnt_ref[...] = carry_ref[...]
    gid_ref[0] = jnp.broadcast_to(g_sel.astype(F32), (TM, LANES)).T[0:1, :].astype(jnp.int32)
    rank_ref[0] = jnp.broadcast_to(rank, (TM, LANES)).T[0:1, :].astype(jnp.int32)


def _moe_dispatch_kernel(seg_ref, pos_ref, hx_ref, xs_ref, zero_ref, sem, zero_sem):
    TM = hx_ref.shape[0]

    @pl.when(pl.program_id(0) == 0)
    def _():
        zero_ref[...] = jnp.zeros_like(zero_ref)

        def zero_tile(t):
            fill = pltpu.make_async_copy(zero_ref, xs_ref.at[pl.ds(t * TM, TM), :], zero_sem)
            fill.start()
            fill.wait()

        for g in range(N_GROUPS):
            first = seg_ref[g - 1] if g else 0
            pl.when(seg_ref[g] > first)(functools.partial(zero_tile, seg_ref[g] - 1))
            tail = seg_ref[N_GROUPS - 1] + g
            pl.when(tail < xs_ref.shape[0] // TM)(functools.partial(zero_tile, tail))

    def start(a, carry):
        for b in range(SUBLANES):
            r = a * SUBLANES + b
            pltpu.make_async_copy(hx_ref.at[pl.ds(r, 1), :], xs_ref.at[pl.ds(pos_ref[0, 0, r], 1), :],
                                  sem).start()
        return carry

    lax.fori_loop(0, TM // SUBLANES, start, 0)
    pltpu.make_async_copy(hx_ref, xs_ref.at[pl.ds(0, TM), :], sem).wait()


def _moe_expert_kernel(grp_ref, xs_ref, wgu_ref, wd_ref, ys_ref, xn_ref, acc_ref):
    j = pl.program_id(0)
    e = pl.program_id(1)
    D = xn_ref.shape[1]
    lane = lax.broadcasted_iota(jnp.int32, (1, LANES), 1)

    @pl.when(grp_ref[j] >= 0)
    def _():
        @pl.when(e == 0)
        def _():
            xn_ref[...] = xs_ref[:, 0:D].astype(BF16)
            acc_ref[...] = jnp.zeros_like(acc_ref)

        hh = jnp.dot(xn_ref[...], wgu_ref[0], preferred_element_type=F32)
        cw = jnp.sum(jnp.where(lane == e, xs_ref[:, D:], 0.0), axis=-1, keepdims=True)
        act = _silu(hh[:, 0:EXPERT_FF]) * hh[:, EXPERT_FF:2 * EXPERT_FF] * cw
        acc_ref[...] += jnp.dot(act.astype(BF16), wd_ref[0], preferred_element_type=F32)

    last = e == EXPERTS_PER_GROUP - 1

    @pl.when(last & (grp_ref[j] >= 0))
    def _():
        ys_ref[...] = acc_ref[...]

    @pl.when(last & (grp_ref[j] < 0))
    def _():
        ys_ref[...] = jnp.zeros_like(ys_ref)


def _moe_combine_kernel(pos_ref, h_ref, ys_ref, o_ref, stage_ref, sem):
    TM = h_ref.shape[0]

    def start(a, carry):
        for b in range(SUBLANES):
            r = a * SUBLANES + b
            pltpu.make_async_copy(ys_ref.at[pl.ds(pos_ref[0, 0, r], 1), :], stage_ref.at[pl.ds(r, 1), :],
                                  sem).start()
        return carry

    lax.fori_loop(0, TM // SUBLANES, start, 0)
    pltpu.make_async_copy(ys_ref.at[pl.ds(0, TM), :], stage_ref, sem).wait()
    o_ref[...] = h_ref[...] + stage_ref[...]


def _moe(h2d, gain, w_group, b_group, w_expert, b_expert, w_gate_up, w_down):
    T, D = h2d.shape
    tr = _pick_tile(T, MOE_ROUTER_TILES)
    tm = _pick_tile(T, MOE_ROW_TILES)
    n_tiles = T // tm + N_GROUPS
    XW = D + LANES
    wr = jnp.zeros((D, 2 * LANES), F32)
    wr = wr.at[:, 0:N_EXPERTS].set(w_expert.astype(F32))
    wr = wr.at[:, LANES:LANES + N_GROUPS].set(w_group.astype(F32))
    br = jnp.zeros((1, 2 * LANES), F32)
    br = br.at[0, 0:N_EXPERTS].set(b_expert.astype(F32))
    br = br.at[0, LANES:LANES + N_GROUPS].set(b_group.astype(F32))
    wgu = w_gate_up.reshape(N_EXPERTS, D, 2 * EXPERT_FF).astype(BF16)
    wd = w_down.reshape(N_EXPERTS, EXPERT_FF, D).astype(BF16)

    hx, gid, rank, counts = pl.pallas_call(
        _moe_router_kernel,
        grid=(T // tr,),
        in_specs=[pl.BlockSpec((tr, D), lambda i: (i, 0)),
                  pl.BlockSpec((1, D), lambda i: (0, 0)),
                  pl.BlockSpec((D, 2 * LANES), lambda i: (0, 0)),
                  pl.BlockSpec((1, 2 * LANES), lambda i: (0, 0))],
        out_specs=[pl.BlockSpec((tr, XW), lambda i: (i, 0)),
                   pl.BlockSpec((1, 1, tr), lambda i: (i, 0, 0)),
                   pl.BlockSpec((1, 1, tr), lambda i: (i, 0, 0)),
                   pl.BlockSpec((1, LANES), lambda i: (0, 0))],
        out_shape=[jax.ShapeDtypeStruct((T, XW), F32),
                   jax.ShapeDtypeStruct((T // tr, 1, tr), jnp.int32),
                   jax.ShapeDtypeStruct((T // tr, 1, tr), jnp.int32),
                   jax.ShapeDtypeStruct((1, LANES), F32)],
        scratch_shapes=[pltpu.VMEM((1, LANES), F32)],
        compiler_params=_compiler_params(("arbitrary",)),
        name="moe_router",
    )(h2d, gain.reshape(1, D), wr, br)

    seg_tiles = -(-counts[0, 0:N_GROUPS].astype(jnp.int32) // tm)
    seg_end = jnp.cumsum(seg_tiles)
    seg_start = seg_end - seg_tiles
    pos = (seg_start * tm)[gid.reshape(T)] + rank.reshape(T)
    tile_ids = jnp.arange(n_tiles, dtype=jnp.int32)
    tile_group = jnp.sum((tile_ids[:, None] >= seg_end[None, :]).astype(jnp.int32), axis=1)
    tile_group = jnp.where(tile_ids < seg_end[N_GROUPS - 1], tile_group, -1)
    pos3 = pos.reshape(T // tm, 1, tm)

    xs = pl.pallas_call(
        _moe_dispatch_kernel,
        grid_spec=pltpu.PrefetchScalarGridSpec(
            num_scalar_prefetch=1,
            grid=(T // tm,),
            in_specs=[pl.BlockSpec((1, 1, tm), lambda i, seg: (i, 0, 0), memory_space=pltpu.SMEM),
                      pl.BlockSpec((tm, XW), lambda i, seg: (i, 0))],
            out_specs=pl.BlockSpec(memory_space=pl.ANY),
            scratch_shapes=[pltpu.VMEM((tm, XW), F32), pltpu.SemaphoreType.DMA(()),
                            pltpu.SemaphoreType.DMA(())]),
        out_shape=jax.ShapeDtypeStruct((n_tiles * tm, XW), F32),
        compiler_params=_compiler_params(("arbitrary",)),
        name="moe_dispatch",
    )(seg_end, pos3, hx)

    def expert_index(j, e, grp):
        return (jnp.maximum(grp[j], 0) * EXPERTS_PER_GROUP + e, 0, 0)

    ys = pl.pallas_call(
        _moe_expert_kernel,
        grid_spec=pltpu.PrefetchScalarGridSpec(
            num_scalar_prefetch=1,
            grid=(n_tiles, EXPERTS_PER_GROUP),
            in_specs=[pl.BlockSpec((tm, XW), lambda j, e, grp: (j, 0)),
                      pl.BlockSpec((1, D, 2 * EXPERT_FF), expert_index),
                      pl.BlockSpec((1, EXPERT_FF, D), expert_index)],
            out_specs=pl.BlockSpec((tm, D), lambda j, e, grp: (j, 0)),
            scratch_shapes=[pltpu.VMEM((tm, D), BF16), pltpu.VMEM((tm, D), F32)]),
        out_shape=jax.ShapeDtypeStruct((n_tiles * tm, D), F32),
        compiler_params=_compiler_params(("parallel", "arbitrary")),
        name="moe_experts",
    )(tile_group, xs, wgu, wd)

    return pl.pallas_call(
        _moe_combine_kernel,
        grid=(T // tm,),
        in_specs=[pl.BlockSpec((1, 1, tm), lambda i: (i, 0, 0), memory_space=pltpu.SMEM),
                  pl.BlockSpec((tm, D), lambda i: (i, 0)),
                  pl.BlockSpec(memory_space=pl.ANY)],
        out_specs=pl.BlockSpec((tm, D), lambda i: (i, 0)),
        out_shape=jax.ShapeDtypeStruct((T, D), F32),
        scratch_shapes=[pltpu.VMEM((tm, D), F32), pltpu.SemaphoreType.DMA(())],
        compiler_params=_compiler_params(("arbitrary",)),
        name="moe_combine",
    )(pos3, h2d, ys)


def _dsa_prep_kernel(p_ref, qg_ref, kg_ref, lng_ref, lnb_ref, q_ref, kt_ref, v_ref, qit_ref, ki_ref, wit_ref):
    lane = lax.broadcasted_iota(jnp.int32, (1, LANES), 1)
    for h in range(ATT_HEADS):
        x = p_ref[:, DSA_Q_OFF + h * ATT_HD:DSA_Q_OFF + (h + 1) * ATT_HD]
        y = x * lax.rsqrt(jnp.mean(x * x, axis=-1, keepdims=True) + RMS_EPS) * qg_ref[...]
        q_ref[:, h * ATT_HD:(h + 1) * ATT_HD] = (y * (ATT_HD ** -0.5)).astype(BF16)
    for n in range(ATT_KV_HEADS):
        x = p_ref[:, DSA_K_OFF + n * ATT_HD:DSA_K_OFF + (n + 1) * ATT_HD]
        y = x * lax.rsqrt(jnp.mean(x * x, axis=-1, keepdims=True) + RMS_EPS) * kg_ref[...]
        kt_ref[0, n * ATT_HD:(n + 1) * ATT_HD, :] = y.T.astype(BF16)
    v_ref[...] = p_ref[:, DSA_V_OFF:DSA_V_OFF + ATT_KV_DIM].astype(BF16)
    qit_ref[0] = p_ref[:, DSA_QI_OFF:DSA_QI_OFF + IDX_HEADS * LANES].T.astype(BF16)
    x = p_ref[:, DSA_KI_OFF:DSA_KI_OFF + LANES]
    live = lane < IDX_HD
    mu = jnp.sum(x, axis=-1, keepdims=True) * (1.0 / IDX_HD)
    xc = jnp.where(live, x - mu, 0.0)
    var = jnp.sum(xc * xc, axis=-1, keepdims=True) * (1.0 / IDX_HD)
    ki = xc * lax.rsqrt(var + RMS_EPS) * lng_ref[...] + lnb_ref[...]
    ki_ref[...] = jnp.where(live, ki, 0.0).astype(BF16)
    wit_ref[0] = (p_ref[:, DSA_WI_OFF:DSA_WI_OFF + LANES] * (IDX_HEADS ** -0.5 * IDX_HD ** -0.5)).T


def _dsa_prep(proj2d, q_gain, k_gain, ln_g, ln_b):
    T, W = proj2d.shape
    tm = KEY_TILE
    assert T % tm == 0
    lng = jnp.zeros((1, LANES), F32).at[0, 0:IDX_HD].set(ln_g.astype(F32))
    lnb = jnp.zeros((1, LANES), F32).at[0, 0:IDX_HD].set(ln_b.astype(F32))
    row = lambda i: (i, 0)
    fix = lambda i: (0, 0)
    blk = lambda i: (i, 0, 0)
    return pl.pallas_call(
        _dsa_prep_kernel,
        grid=(T // tm,),
        in_specs=[pl.BlockSpec((tm, W), row),
                  pl.BlockSpec((1, ATT_HD), fix), pl.BlockSpec((1, ATT_HD), fix),
                  pl.BlockSpec((1, LANES), fix), pl.BlockSpec((1, LANES), fix)],
        out_specs=[pl.BlockSpec((tm, ATT_Q_DIM), row), pl.BlockSpec((1, ATT_KV_DIM, tm), blk),
                   pl.BlockSpec((tm, ATT_KV_DIM), row), pl.BlockSpec((1, IDX_HEADS * LANES, tm), blk),
                   pl.BlockSpec((tm, LANES), row), pl.BlockSpec((1, LANES, tm), blk)],
        out_shape=[jax.ShapeDtypeStruct((T, ATT_Q_DIM), BF16), jax.ShapeDtypeStruct((T // tm, ATT_KV_DIM, tm), BF16),
                   jax.ShapeDtypeStruct((T, ATT_KV_DIM), BF16),
                   jax.ShapeDtypeStruct((T // tm, IDX_HEADS * LANES, tm), BF16),
                   jax.ShapeDtypeStruct((T, LANES), BF16), jax.ShapeDtypeStruct((T // tm, LANES, tm), F32)],
        compiler_params=_compiler_params(("parallel",)),
        name="dsa_prep",
    )(proj2d, q_gain.reshape(1, ATT_HD), k_gain.reshape(1, ATT_HD), lng, lnb)


def _dsa_kernel(q_ref, qit_ref, wit_ref, kt_ref, v_ref, ki_ref, bias_ref, bmax_ref, o_ref,
                sc_ref, mask_ref, qg_ref, m_ref, l_ref, acc_ref, knorm_ref, *, n_select, live_rows):
    i = pl.program_id(1)
    TQ = ATT_TILE
    TK = KEY_TILE
    HALVES = TK // LANES

    @pl.when(i == 0)
    def _():
        def body(kt, mx):
            kk = kt_ref[0, kt].astype(F32)
            return tuple(jnp.maximum(mx[n], jnp.sum(jnp.square(kk[n * ATT_HD:(n + 1) * ATT_HD, :]),
                                                    axis=0, keepdims=True)) for n in range(ATT_KV_HEADS))

        mx = lax.fori_loop(0, kt_ref.shape[1], body,
                           tuple(jnp.zeros((1, TK), F32) for _ in range(ATT_KV_HEADS)))
        for n in range(ATT_KV_HEADS):
            knorm_ref[n] = jnp.broadcast_to(jnp.sqrt(jnp.max(mx[n], axis=-1, keepdims=True)), (1, LANES))

    @pl.when(i * TQ >= live_rows)
    def _():
        o_ref[...] = jnp.zeros_like(o_ref)

    @pl.when(i * TQ < live_rows)
    def _():
        _dsa_tile(i, q_ref, qit_ref, wit_ref, kt_ref, v_ref, ki_ref, bias_ref, bmax_ref, o_ref,
                  sc_ref, mask_ref, qg_ref, m_ref, l_ref, acc_ref, knorm_ref, n_select, TQ, TK, HALVES)


def _dsa_tile(i, q_ref, qit_ref, wit_ref, kt_ref, v_ref, ki_ref, bias_ref, bmax_ref, o_ref,
              sc_ref, mask_ref, qg_ref, m_ref, l_ref, acc_ref, knorm_ref, n_select, TQ, TK, HALVES):
    assert TQ == TK
    nkt = i + 1
    fmin = float(jnp.finfo(F32).min)
    kf = float(n_select)

    chunk_shift = int(math.log2(CHUNK))
    q_pos = i * TQ + lax.broadcasted_iota(jnp.int32, (1, TQ), 1)
    q_chunk = jnp.maximum(lax.shift_right_logical(q_pos, chunk_shift), FRONT_PAD // CHUNK)
    k_row = lax.broadcasted_iota(jnp.int32, (TK, 1), 0)

    def halves(x):
        return [x[:, j * LANES:(j + 1) * LANES] for j in range(HALVES)]

    def fold(x, op):
        parts = [x[r * SUBLANES:(r + 1) * SUBLANES, :] for r in range(TK // SUBLANES)]
        while len(parts) > 1:
            parts = [op(parts[j], parts[j + 1]) for j in range(0, len(parts), 2)]
        return parts[0]

    def score_tiles(kts):
        for kt in kts:
            keys = ki_ref[0, pl.ds(pl.multiple_of(kt * TK, TK), TK), :]
            acc = None
            for h in range(IDX_HEADS):
                d = jnp.dot(keys, qit_ref[0, 0, h * LANES:(h + 1) * LANES, :], preferred_element_type=F32)
                term = wit_ref[0, 0, h:h + 1, :] * jnp.maximum(d, 0.0)
                acc = term if acc is None else acc + term
            sc_ref[kt] = acc
            mask_ref[kt] = acc.T

    def tile_loop(n, fn):
        def body(kt, carry):
            fn([kt])
            return carry

        lax.fori_loop(0, n, body, 0)

    def paired_loop(n, fn):
        def body(j, carry):
            fn([2 * j, 2 * j + 1])
            return carry

        lax.fori_loop(0, lax.shift_right_logical(n, 1), body, 0)

        @pl.when(lax.bitwise_and(n, 1) == 1)
        def _():
            fn([n - 1])

    paired_loop(nkt, score_tiles)

    def mask_inadmissible(kt):
        k_pos = kt * TK + k_row
        adm = (lax.shift_right_logical(k_pos, chunk_shift) <= q_chunk) & (k_pos >= FRONT_PAD)
        sc_ref[kt] = jnp.where(adm, sc_ref[kt], NEG_INF)

    mask_inadmissible(0)

    @pl.when(nkt > 1)
    def _():
        mask_inadmissible(nkt - 1)

    n_adm = CHUNK * (q_chunk + 1) - FRONT_PAD
    need = n_adm > n_select

    def count_where(pred):
        def one(kt):
            return fold(jnp.where(pred(sc_ref[kt], kt), 1.0, 0.0), jnp.add)

        def two(j, acc):
            return acc + (one(2 * j) + one(2 * j + 1))

        acc = lax.fori_loop(0, lax.shift_right_logical(nkt, 1), two, jnp.zeros((SUBLANES, TQ), F32))
        acc = lax.cond(lax.bitwise_and(nkt, 1) == 1, lambda a: a + one(nkt - 1), lambda a: a, acc)
        return jnp.sum(acc, axis=0, keepdims=True)

    def count_ge(t):
        return count_where(lambda s, kt: s >= t)

    def minmax_body(kt, carry):
        mn, mx = carry
        s = sc_ref[kt]
        mn = jnp.minimum(mn, fold(jnp.where(s == NEG_INF, jnp.inf, s), jnp.minimum))
        mx = jnp.maximum(mx, fold(s, jnp.maximum))
        return mn, mx

    mn_l, mx_l = lax.fori_loop(0, nkt, minmax_body,
                               (jnp.full((SUBLANES, TQ), jnp.inf, F32), jnp.full((SUBLANES, TQ), NEG_INF, F32)))
    s_min = jnp.min(mn_l, axis=0, keepdims=True)
    s_max = jnp.max(mx_l, axis=0, keepdims=True)
    lo0 = s_min
    hi0 = s_max + (s_max - s_min) + 1.0
    cnt0 = n_adm.astype(F32)
    active0 = jnp.where(need, 1.0, 0.0)

    def any_active(active):
        return jnp.max(active) > 0.0

    def bis_step(lo, hi, cnt, active):
        mid = lo + (hi - lo) * 0.5
        c = count_ge(mid)
        ge = c >= kf
        moving = (mid > lo) & (mid < hi) & (active > 0.0)
        lo_n = jnp.where(moving & ge, mid, lo)
        hi_n = jnp.where(moving & (~ge), mid, hi)
        cnt_n = jnp.where(moving & ge, c, cnt)
        active_n = jnp.where(moving & (cnt_n != kf), 1.0, 0.0)
        return lo_n, hi_n, cnt_n, active_n

    def bis_body(st):
        it, rest = st[0], st[1:]
        for _ in range(BISECT_UNROLL):
            rest = bis_step(*rest)
        return (it + BISECT_UNROLL,) + rest

    _, lo, hi, cnt, _ = lax.while_loop(
        lambda st: jnp.logical_and(st[0] < BISECT_STEPS, any_active(st[4])), bis_body,
        (jnp.int32(0), lo0, hi0, cnt0, active0))

    def snap_body(st):
        lo, hi, cnt, active = st

        def body(kt, mx):
            s = sc_ref[kt]
            return jnp.maximum(mx, fold(jnp.where(s < hi, s, NEG_INF), jnp.maximum))

        v1 = jnp.max(lax.fori_loop(0, nkt, body, jnp.full((SUBLANES, TQ), NEG_INF, F32)),
                     axis=0, keepdims=True)
        c1 = count_ge(v1)
        on = active > 0.0
        hit = on & (c1 >= kf)
        lo_n = jnp.where(hit, v1, lo)
        cnt_n = jnp.where(hit, c1, cnt)
        hi_n = jnp.where(on & (~hit), v1, hi)
        return lo_n, hi_n, cnt_n, jnp.where(on & (~hit), 1.0, 0.0)

    unsettled = jnp.where(need & (cnt != kf), 1.0, 0.0)
    lo, hi, cnt, _ = lax.while_loop(lambda st: any_active(st[3]), snap_body, (lo, hi, cnt, unsettled))

    thr = jnp.where(need, lo, fmin)
    tied = need & (cnt > kf)
    big_pos = jnp.int32(2 ** 30)

    def tie_break():
        want = kf - count_where(lambda s, kt: s > thr)

        def body(_, st):
            jl, jh = st
            jm = lax.shift_right_arithmetic(jl + jh, 1)
            c = count_where(lambda s, kt: (s == thr) & (kt * TK + k_row <= jm))
            ok = c >= want
            return jnp.where(ok, jl, jm), jnp.where(ok, jm, jh)

        n_keys = sc_ref.shape[0] * TK
        steps = int(math.ceil(math.log2(n_keys + 1))) + 1
        jl0 = jnp.full((1, TQ), -1, jnp.int32)
        jh0 = jnp.full((1, TQ), n_keys, jnp.int32)
        _, jh = lax.fori_loop(0, steps, body, (jl0, jh0))
        return jnp.where(tied, jh, big_pos)

    pos_cap = lax.cond(jnp.max(jnp.where(tied, 1.0, 0.0)) > 0.0, tie_break,
                       lambda: jnp.full((1, TQ), big_pos, jnp.int32))

    def lane_wide(row):
        return jnp.broadcast_to(row, (LANES, TQ)).T

    thr_w = lane_wide(thr)
    cap_w = lane_wide(pos_cap)
    chunk_w = lane_wide(q_chunk)
    k_lane = lax.broadcasted_iota(jnp.int32, (1, LANES), 1)

    def to_mask(kt, edge_tile):
        out = []
        for j, s in enumerate(halves(mask_ref[kt])):
            k_pos = kt * TK + j * LANES + k_lane
            sel = (s > thr_w) | ((s == thr_w) & (k_pos <= cap_w))
            if edge_tile:
                sel = sel & (lax.shift_right_logical(k_pos, chunk_shift) <= chunk_w) & (k_pos >= FRONT_PAD)
            out.append(jnp.where(sel, 0.0, NEG_INF))
        mask_ref[kt] = jnp.concatenate(out, axis=1)

    def inner_mask(kt, carry):
        to_mask(kt, False)
        return carry

    lax.fori_loop(1, nkt - 1, inner_mask, 0)
    to_mask(0, True)

    @pl.when(nkt > 1)
    def _():
        to_mask(nkt - 1, True)

    for n in range(ATT_KV_HEADS):
        for g in range(ATT_GROUP):
            h = n * ATT_GROUP + g
            qg_ref[n, g * TQ:(g + 1) * TQ, :] = q_ref[0, :, h * ATT_HD:(h + 1) * ATT_HD]
    l_ref[...] = jnp.zeros_like(l_ref)
    acc_ref[...] = jnp.zeros_like(acc_ref)

    kv_heads = range(ATT_KV_HEADS)

    def raw_logits(kts):
        return [[jnp.dot(qg_ref[n], kt_ref[0, kt, n * ATT_HD:(n + 1) * ATT_HD, :], preferred_element_type=F32)
                 for n in kv_heads] for kt in kts]

    def masked_logits(s, mask, n, g, b0):
        sg = s[g * TQ:(g + 1) * TQ, :] + mask
        if b0 is not None:
            h = n * ATT_GROUP + g
            sg = sg + jnp.concatenate([bias_ref[h, b0 + j] for j in range(HALVES)], axis=1)
        return sg

    def max_pass(kts, b0):
        raw = raw_logits(kts)
        masks = [mask_ref[kt] for kt in kts]
        for n in kv_heads:
            for g in range(ATT_GROUP):
                rows = slice(g * TQ, (g + 1) * TQ)
                m = m_ref[n, rows, :]
                for t in range(len(kts)):
                    for part in halves(masked_logits(raw[t][n], masks[t], n, g, b0)):
                        m = jnp.maximum(m, part)
                m_ref[n, rows, :] = m

    def exp_pass(kts, b0):
        raw = raw_logits(kts)
        masks = [mask_ref[kt] for kt in kts]
        for n in kv_heads:
            ps = [[] for _ in kts]
            for g in range(ATT_GROUP):
                rows = slice(g * TQ, (g + 1) * TQ)
                m = m_ref[n, rows, :]
                l_new = l_ref[n, rows, :]
                for t in range(len(kts)):
                    parts = [jnp.exp(part - m) for part in halves(masked_logits(raw[t][n], masks[t], n, g, b0))]
                    l_new = l_new + functools.reduce(lambda a, b: a + b, parts)
                    ps[t].append(jnp.concatenate(parts, axis=1).astype(BF16))
                l_ref[n, rows, :] = l_new
            pv = None
            for t, kt in enumerate(kts):
                vv = v_ref[0, pl.ds(pl.multiple_of(kt * TK, TK), TK), n * ATT_HD:(n + 1) * ATT_HD]
                d = jnp.dot(jnp.concatenate(ps[t], axis=0), vv, preferred_element_type=F32)
                pv = d if pv is None else pv + d
            acc_ref[n] += pv

    def run_pass(tile_fn):
        paired_loop(jnp.maximum(i - 1, 0), lambda kts: tile_fn(kts, None))

        @pl.when(i >= 1)
        def _():
            tile_fn([i - 1], 0)

        tile_fn([i], HALVES)

    for n in range(ATT_KV_HEADS):
        for g in range(ATT_GROUP):
            h = n * ATT_GROUP + g
            qh = qg_ref[n, g * TQ:(g + 1) * TQ, :].astype(F32)
            q_len = jnp.sqrt(jnp.sum(qh * qh, axis=-1, keepdims=True))
            bound = q_len * knorm_ref[n] * SHIFT_SLACK + (bmax_ref[h] + SHIFT_SLACK - 1.0)
            m_ref[n, g * TQ:(g + 1) * TQ, :] = jnp.broadcast_to(bound, (TQ, LANES))
    run_pass(exp_pass)

    def smallest_row_sum():
        return functools.reduce(jnp.minimum, [jnp.min(jnp.sum(l_ref[n], axis=-1, keepdims=True))
                                              for n in range(ATT_KV_HEADS)])

    @pl.when(jnp.logical_not(smallest_row_sum() > ROW_SUM_FLOOR))
    def _():
        m_ref[...] = jnp.full(m_ref.shape, -1e30, F32)
        run_pass(max_pass)
        for n in range(ATT_KV_HEADS):
            m_ref[n] = jnp.broadcast_to(jnp.max(m_ref[n], axis=-1, keepdims=True), m_ref.shape[1:])
        l_ref[...] = jnp.zeros_like(l_ref)
        acc_ref[...] = jnp.zeros_like(acc_ref)
        run_pass(exp_pass)

    for n in range(ATT_KV_HEADS):
        inv_l = 1.0 / jnp.sum(l_ref[n], axis=-1, keepdims=True)
        for g in range(ATT_GROUP):
            h = n * ATT_GROUP + g
            rows = slice(g * TQ, (g + 1) * TQ)
            o_ref[0, :, h * ATT_HD:(h + 1) * ATT_HD] = (acc_ref[n, rows, :] * inv_l[rows, :]).astype(o_ref.dtype)


def _t5_bucket(rel):
    nb = REL_BUCKETS // 2
    max_exact = nb // 2
    ret = jnp.where(rel > 0, nb, 0)
    n = jnp.abs(rel)
    nf = jnp.maximum(n, 1).astype(jnp.float32)
    large = max_exact + (jnp.log(nf / max_exact) / math.log(REL_MAX_DIST / max_exact)
                         * (nb - max_exact)).astype(jnp.int32)
    large = jnp.minimum(large, nb - 1)
    return ret + jnp.where(n < max_exact, n, large)


def _dsa_core(q, kt, v, qit, ki, wit, rel_bias, n_select, live_rows):
    B, LP, _ = q.shape
    nq = LP // ATT_TILE
    nkt = LP // KEY_TILE
    a = jnp.arange(ATT_TILE)[:, None]
    xx = jnp.arange(2 * KEY_TILE)[None, :]
    rel = (xx - KEY_TILE) - a
    far = rel_bias.astype(F32)[REL_BUCKETS // 2 - 1]
    onehot = jax.nn.one_hot(_t5_bucket(rel), REL_BUCKETS, dtype=F32)
    table = jnp.einsum("abk,kh->abh", onehot, rel_bias.astype(F32),
                       precision=lax.Precision.HIGHEST) - far
    bias_near = jnp.transpose(table.reshape(ATT_TILE, 2 * KEY_TILE // LANES, LANES, ATT_HEADS), (3, 1, 0, 2))
    bias_max = jnp.broadcast_to(jnp.maximum(jnp.max(table, axis=(0, 1)), 0.0)[:, None, None],
                                (ATT_HEADS, 1, LANES))
    kern = functools.partial(_dsa_kernel, n_select=n_select, live_rows=live_rows)
    tile = lambda b, i: (b, i, 0)
    tile4 = lambda b, i: (b, i, 0, 0)
    whole3 = lambda b, i: (b, 0, 0)
    whole4 = lambda b, i: (b, 0, 0, 0)
    once = pl.Buffered(1)
    return pl.pallas_call(
        kern,
        grid=(B, nq),
        in_specs=[pl.BlockSpec((1, ATT_TILE, ATT_Q_DIM), tile),
                  pl.BlockSpec((1, 1, IDX_HEADS * LANES, ATT_TILE), tile4),
                  pl.BlockSpec((1, 1, LANES, ATT_TILE), tile4),
                  pl.BlockSpec((1, nkt, ATT_KV_DIM, KEY_TILE), whole4, pipeline_mode=once),
                  pl.BlockSpec((1, LP, ATT_KV_DIM), whole3, pipeline_mode=once),
                  pl.BlockSpec((1, LP, LANES), whole3, pipeline_mode=once),
                  pl.BlockSpec((ATT_HEADS, 2 * KEY_TILE // LANES, ATT_TILE, LANES), lambda b, i: (0, 0, 0, 0),
                               pipeline_mode=once),
                  pl.BlockSpec((ATT_HEADS, 1, LANES), lambda b, i: (0, 0, 0))],
        out_specs=pl.BlockSpec((1, ATT_TILE, ATT_Q_DIM), tile),
        out_shape=jax.ShapeDtypeStruct((B, LP, ATT_Q_DIM), BF16),
        scratch_shapes=[pltpu.VMEM((nkt, KEY_TILE, ATT_TILE), F32),
                        pltpu.VMEM((nkt, ATT_TILE, KEY_TILE), F32),
                        pltpu.VMEM((ATT_KV_HEADS, ATT_GROUP * ATT_TILE, ATT_HD), BF16),
                        pltpu.VMEM((ATT_KV_HEADS, ATT_GROUP * ATT_TILE, LANES), F32),
                        pltpu.VMEM((ATT_KV_HEADS, ATT_GROUP * ATT_TILE, LANES), F32),
                        pltpu.VMEM((ATT_KV_HEADS, ATT_GROUP * ATT_TILE, ATT_HD), F32),
                        pltpu.VMEM((ATT_KV_HEADS, 1, LANES), F32)],
        compiler_params=_compiler_params(("arbitrary", "arbitrary")),
        name="dsa_core",
    )(q, qit, wit, kt, v, ki, bias_near, bias_max)


def _gdn_in_weight(w_in):
    D = w_in.shape[0]
    w = jnp.zeros((D, GDN_PROJ_W), F32).at[:, 0:w_in.shape[1]].set(w_in.astype(F32))
    return w.astype(BF16)


def _dsa_in_weight(w_in):
    D = w_in.shape[0]
    w_in = w_in.astype(BF16)
    zeros = lambda n: jnp.zeros((D, n), BF16)
    src = ATT_Q_DIM + 2 * ATT_KV_DIM
    parts = [w_in[:, 0:src]]
    for h in range(IDX_HEADS):
        parts += [w_in[:, src + h * IDX_HD:src + (h + 1) * IDX_HD], zeros(LANES - IDX_HD)]
    src += IDX_HEADS * IDX_HD
    parts += [w_in[:, src:src + IDX_HD], zeros(LANES - IDX_HD),
              w_in[:, src + IDX_HD:src + IDX_HD + IDX_HEADS], zeros(LANES - IDX_HEADS)]
    w = jnp.concatenate(parts, axis=1)
    assert w.shape[1] == DSA_PROJ_W
    return w


def kernel(x, meta_tokens, norm_mix, norm_ffn, rel_bias, gdn_w_in, gdn_conv, gdn_a_log, gdn_dt_bias,
           gdn_o_norm, gdn_w_out, dsa_w_in, dsa_q_norm, dsa_k_norm, dsa_idx_ln_g, dsa_idx_ln_b, dsa_w_out,
           moe_w_group, moe_b_group, moe_w_expert, moe_b_expert, moe_w_gate_up, moe_w_down):
    B, S, D = x.shape
    depth = norm_mix.shape[0]
    n_select = min(TOPK_MAX, S // 4)
    off = FRONT_PAD + N_META
    LP = -(-(off + S) // ROW_ALIGN) * ROW_ALIGN
    meta = jnp.broadcast_to(meta_tokens.astype(x.dtype)[None], (B, N_META, D))
    h = jnp.concatenate([jnp.zeros((B, FRONT_PAD, D), x.dtype), meta, x,
                         jnp.zeros((B, LP - off - S, D), x.dtype)], axis=1)
    h = h.reshape(B * LP, D)
    for i in range(depth):
        j = i // 2
        if i % 2 == 0:
            proj = _norm_matmul(h, norm_mix[i], _gdn_in_weight(gdn_w_in[j]))
            o = _gdn_core(proj.reshape(B, LP, GDN_PROJ_W), gdn_conv[j].astype(F32), gdn_a_log[j],
                          gdn_dt_bias[j], gdn_o_norm[j])
            h = _matmul_residual(o.reshape(B * LP, GDN_V_DIM), gdn_w_out[j].astype(BF16), h)
        else:
            proj = _norm_matmul(h, norm_mix[i], _dsa_in_weight(dsa_w_in[j]))
            q, kt, v, qit, ki, wit = _dsa_prep(proj, dsa_q_norm[j], dsa_k_norm[j], dsa_idx_ln_g[j], dsa_idx_ln_b[j])
            r3 = lambda t: t.reshape(B, LP, t.shape[-1])
            r4 = lambda t: t.reshape(B, LP // KEY_TILE, t.shape[-2], KEY_TILE)
            o = _dsa_core(r3(q), r4(kt), r3(v), r4(qit), r3(ki), r4(wit), rel_bias, n_select, off + S)
            h = _matmul_residual(o.reshape(B * LP, ATT_Q_DIM), dsa_w_out[j].astype(BF16), h)
        h = _moe(h, norm_ffn[i], moe_w_group[i], moe_b_group[i], moe_w_expert[i], moe_b_expert[i],
                 moe_w_gate_up[i], moe_w_down[i])
    return h.reshape(B, LP, D)[:, off:off + S]
```

```python
import functools
import math

import jax
import jax.numpy as jnp
from jax import lax
from jax.experimental import pallas as pl
from jax.experimental.pallas import tpu as pltpu

F32 = jnp.float32
BF16 = jnp.bfloat16

LANES = 128
SUBLANES = 8
VMEM_LIMIT_BYTES = 56 * 1024 * 1024

CHUNK = 64
N_META = 16
META_PAD = (-N_META) % CHUNK
RMS_EPS = 1e-6
L2_EPS = 1e-6

GDN_HEADS = 8
GDN_DK = 128
GDN_DV = 128
GDN_CONV = 4
GDN_QK_DIM = GDN_HEADS * GDN_DK
GDN_V_DIM = GDN_HEADS * GDN_DV
GDN_CONV_DIM = 2 * GDN_QK_DIM + GDN_V_DIM
GDN_GATE_OFF = GDN_CONV_DIM + GDN_V_DIM
GDN_PROJ_W = GDN_GATE_OFF + LANES

ATT_HEADS = 8
ATT_KV_HEADS = 2
ATT_GROUP = ATT_HEADS // ATT_KV_HEADS
ATT_HD = 128
ATT_Q_DIM = ATT_HEADS * ATT_HD
ATT_KV_DIM = ATT_KV_HEADS * ATT_HD
IDX_HEADS = 8
IDX_HD = 64
TOPK_MAX = 256
REL_BUCKETS = 32
REL_MAX_DIST = 128

DSA_Q_OFF = 0
DSA_K_OFF = ATT_Q_DIM
DSA_V_OFF = DSA_K_OFF + ATT_KV_DIM
DSA_QI_OFF = DSA_V_OFF + ATT_KV_DIM
DSA_KI_OFF = DSA_QI_OFF + IDX_HEADS * LANES
DSA_WI_OFF = DSA_KI_OFF + LANES
DSA_PROJ_W = DSA_WI_OFF + LANES

N_GROUPS = 4
EXPERTS_PER_GROUP = 8
N_EXPERTS = N_GROUPS * EXPERTS_PER_GROUP
EXPERT_FF = 256

GDN_ROW_TILES = (256, 128, 64)
MOE_ROUTER_TILES = (512, 256, 128)
MOE_ROW_TILES = (512, 256, 128)

ATT_TILE = 256
KEY_TILE = 256
ROW_ALIGN = KEY_TILE
FRONT_PAD = ROW_ALIGN - N_META
assert FRONT_PAD % CHUNK == META_PAD
BISECT_UNROLL = 4
BISECT_STEPS = 24
SHIFT_SLACK = 1.001
ROW_SUM_FLOOR = 1e-30
NEG_INF = float("-inf")


def _pick_tile(n, candidates):
    for c in candidates:
        if n % c == 0:
            return c
    raise ValueError(f"no tile for {n}")


def _compiler_params(semantics):
    return pltpu.CompilerParams(dimension_semantics=semantics, vmem_limit_bytes=VMEM_LIMIT_BYTES)


def _silu(x):
    return x * jax.nn.sigmoid(x)


def _norm_matmul_kernel(x_ref, g_ref, w_ref, o_ref):
    x = x_ref[...]
    y = x * lax.rsqrt(jnp.mean(x * x, axis=-1, keepdims=True) + RMS_EPS) * g_ref[...]
    o_ref[...] = jnp.dot(y.astype(BF16), w_ref[...], preferred_element_type=F32)


def _matmul_residual_kernel(a_ref, w_ref, r_ref, o_ref):
    o_ref[...] = r_ref[...] + jnp.dot(a_ref[...], w_ref[...], preferred_element_type=F32)


def _matmul_residual(a_bf16, w_bf16, res):
    T, K = a_bf16.shape
    N = w_bf16.shape[1]
    tm = _pick_tile(T, (512, 256, 128))
    return pl.pallas_call(
        _matmul_residual_kernel,
        grid=(T // tm,),
        in_specs=[pl.BlockSpec((tm, K), lambda i: (i, 0)),
                  pl.BlockSpec((K, N), lambda i: (0, 0)),
                  pl.BlockSpec((tm, N), lambda i: (i, 0))],
        out_specs=pl.BlockSpec((tm, N), lambda i: (i, 0)),
        out_shape=jax.ShapeDtypeStruct((T, N), F32),
        compiler_params=_compiler_params(("parallel",)),
        name="matmul_residual",
    )(a_bf16, w_bf16, res)


def _nt_dot(a, b):
    return lax.dot_general(a, b, (((1,), (1,)), ((), ())), preferred_element_type=F32)


def _tn_dot(a, b):
    return lax.dot_general(a, b, (((0,), (0,)), ((), ())), preferred_element_type=F32)


def _gdn_kernel(x_ref, g_ref, w_ref, conv_ref, gate_ref, ogain_ref, o_ref,
                state_ref, xbuf_ref, qkv_ref, proj_ref):
    t = pl.program_id(1)
    rows = x_ref.shape[1]

    @pl.when(t == 0)
    def _():
        state_ref[...] = jnp.zeros_like(state_ref)
        xbuf_ref[0:SUBLANES, :] = jnp.zeros((SUBLANES, GDN_CONV_DIM), F32)

    _norm_matmul_kernel(x_ref.at[0], g_ref, w_ref, proj_ref)

    def chunk(cc, carry):
        r0 = pl.multiple_of(cc * CHUNK, CHUNK)
        _gdn_chunk(t * rows + r0, r0, proj_ref, conv_ref, gate_ref, ogain_ref, o_ref, state_ref, xbuf_ref, qkv_ref)
        return carry

    lax.fori_loop(0, rows // CHUNK, chunk, 0)


def _gdn_chunk(row0, r0, proj_ref, conv_ref, gate_ref, ogain_ref, o_ref, state_ref, xbuf_ref, qkv_ref):
    C = CHUNK
    row = row0 + lax.broadcasted_iota(jnp.int32, (C, 1), 0)
    live = row >= FRONT_PAD
    u = jnp.where(live, proj_ref[pl.ds(r0, C), 0:GDN_CONV_DIM], 0.0)
    xbuf_ref[SUBLANES:SUBLANES + C, :] = u
    conv = conv_ref[0:1, :] * xbuf_ref[SUBLANES - 3:SUBLANES - 3 + C, :]
    for j in range(1, GDN_CONV):
        conv = conv + conv_ref[j:j + 1, :] * xbuf_ref[SUBLANES - 3 + j:SUBLANES - 3 + j + C, :]
    xbuf_ref[0:SUBLANES, :] = u[C - SUBLANES:C, :]
    qkv_ref[...] = _silu(conv)
    qkv = qkv_ref

    gates = jnp.where(live, proj_ref[pl.ds(r0, C), GDN_GATE_OFF:GDN_GATE_OFF + LANES], 0.0)
    beta = jax.nn.sigmoid(gates)
    neg_rate = -jnp.exp(gate_ref[0:1, :])
    g = neg_rate * jax.nn.softplus(gates + gate_ref[1:2, :])
    r_i = lax.broadcasted_iota(jnp.int32, (C, C), 0)
    c_i = lax.broadcasted_iota(jnp.int32, (C, C), 1)
    causal = c_i <= r_i
    strict = c_i < r_i
    gc = jnp.dot(causal.astype(F32), g, preferred_element_type=F32,
                 precision=lax.Precision.HIGHEST)
    gc_t = gc.T
    eg = jnp.exp(gc)
    g_last = gc[C - 1:C, :]
    eg_last = jnp.exp(g_last)
    ek = jnp.exp(g_last - gc)

    heads = range(GDN_HEADS)
    dot = functools.partial(jnp.dot, preferred_element_type=F32)
    col = lambda a, h: a[:, GDN_HEADS + h:GDN_HEADS + h + 1]
    qs, ks, k16s, xs, decays = [], [], [], [], []
    for h in heads:
        lo = h * GDN_DK
        q = qkv[:, lo:lo + GDN_DK]
        k = qkv[:, GDN_QK_DIM + lo:GDN_QK_DIM + lo + GDN_DK]
        v = qkv[:, 2 * GDN_QK_DIM + h * GDN_DV:2 * GDN_QK_DIM + (h + 1) * GDN_DV]
        q = q * lax.rsqrt(jnp.sum(q * q, axis=-1, keepdims=True) + L2_EPS) * (GDN_DK ** -0.5)
        k = k * lax.rsqrt(jnp.sum(k * k, axis=-1, keepdims=True) + L2_EPS)
        kb = k * beta[:, h:h + 1]
        vb = v * beta[:, h:h + 1]
        qs.append(q)
        ks.append(k)
        k16s.append(k.astype(BF16))
        xs.append((kb, jnp.concatenate([vb, kb * col(eg, h)], axis=1)))
        decays.append(jnp.exp(jnp.where(causal, col(gc, h) - gc_t[GDN_HEADS + h:GDN_HEADS + h + 1, :],
                                        NEG_INF)))
    kk = [_nt_dot(xs[h][0].astype(BF16), k16s[h]) for h in heads]
    qk = [_nt_dot(qs[h].astype(BF16), k16s[h]) for h in heads]
    ns = [-jnp.where(strict, kk[h] * decays[h], 0.0) for h in heads]
    xs = [x for _, x in xs]
    for it in range(6):
        n16 = [n.astype(BF16) for n in ns]
        xs = [xs[h] + dot(n16[h], xs[h].astype(BF16)) for h in heads]
        if it < 5:
            ns = [dot(n16[h], n16[h]) for h in heads]
    qk = [jnp.where(causal, qk[h] * decays[h], 0.0).astype(BF16) for h in heads]
    states = [state_ref[h] for h in heads]
    s16 = [s.astype(BF16) for s in states]
    ws = [dot(xs[h][:, GDN_DV:2 * GDN_DV].astype(BF16), s16[h]) for h in heads]
    qs_s = [dot((qs[h] * col(eg, h)).astype(BF16), s16[h]) for h in heads]
    v_new = [(xs[h][:, 0:GDN_DV] - ws[h]).astype(BF16) for h in heads]
    os_ = [qs_s[h] + dot(qk[h], v_new[h]) for h in heads]
    kv = [_tn_dot((ks[h] * col(ek, h)).astype(BF16), v_new[h]) for h in heads]
    for h in heads:
        state_ref[h] = states[h] * col(eg_last, h) + kv[h]
        z = proj_ref[pl.ds(r0, C), GDN_CONV_DIM + h * GDN_DV:GDN_CONV_DIM + (h + 1) * GDN_DV]
        o = os_[h]
        on = o * lax.rsqrt(jnp.mean(o * o, axis=-1, keepdims=True) + RMS_EPS) * ogain_ref[...]
        o_ref[0, pl.ds(r0, C), h * GDN_DV:(h + 1) * GDN_DV] = (on * _silu(z)).astype(o_ref.dtype)


def _gdn_core(h3d, gain, w_bf16, conv_w, a_log, dt_bias, o_gain):
    B, LP, D = h3d.shape
    W = w_bf16.shape[1]
    rows = _pick_tile(LP, GDN_ROW_TILES)
    gate = jnp.zeros((2, LANES), F32)
    gate = gate.at[0, GDN_HEADS:2 * GDN_HEADS].set(a_log.astype(F32))
    gate = gate.at[1, GDN_HEADS:2 * GDN_HEADS].set(dt_bias.astype(F32))
    fix = lambda b, t: (0, 0)
    return pl.pallas_call(
        _gdn_kernel,
        grid=(B, LP // rows),
        in_specs=[pl.BlockSpec((1, rows, D), lambda b, t: (b, t, 0)),
                  pl.BlockSpec((1, D), fix),
                  pl.BlockSpec((D, W), fix, pipeline_mode=pl.Buffered(1)),
                  pl.BlockSpec((GDN_CONV, GDN_CONV_DIM), fix),
                  pl.BlockSpec((2, LANES), fix),
                  pl.BlockSpec((1, GDN_DV), fix)],
        out_specs=pl.BlockSpec((1, rows, GDN_V_DIM), lambda b, t: (b, t, 0)),
        out_shape=jax.ShapeDtypeStruct((B, LP, GDN_V_DIM), BF16),
        scratch_shapes=[pltpu.VMEM((GDN_HEADS, GDN_DK, GDN_DV), F32),
                        pltpu.VMEM((SUBLANES + CHUNK, GDN_CONV_DIM), F32),
                        pltpu.VMEM((CHUNK, GDN_CONV_DIM), F32),
                        pltpu.VMEM((rows, W), F32)],
        compiler_params=_compiler_params(("parallel", "arbitrary")),
        name="gdn_core",
    )(h3d, gain.reshape(1, D), w_bf16, conv_w, gate, o_gain.reshape(1, GDN_DV))


def _moe_router_kernel(h_ref, gain_ref, wr_ref, br_ref, hx_ref, gid_ref, rank_ref, cnt_ref, carry_ref):
    i = pl.program_id(0)
    TM = h_ref.shape[0]
    lane = lax.broadcasted_iota(jnp.int32, (1, LANES), 1)

    @pl.when(i == 0)
    def _():
        carry_ref[...] = jnp.zeros_like(carry_ref)

    x = h_ref[...]
    xn = x * lax.rsqrt(jnp.mean(x * x, axis=-1, keepdims=True) + RMS_EPS) * gain_ref[...]
    logits = jnp.dot(xn, wr_ref[...], preferred_element_type=F32,
                     precision=lax.Precision.HIGHEST) + br_ref[...]
    e_log = logits[:, 0:LANES]
    g_log = jnp.where(lane < N_GROUPS, logits[:, LANES:2 * LANES], NEG_INF)
    g_max = jnp.max(g_log, axis=-1, keepdims=True)
    g_sel = jnp.min(jnp.where(g_log == g_max, lane, LANES), axis=-1, keepdims=True)
    g_sel = jnp.minimum(g_sel, N_GROUPS - 1)
    g_w = 1.0 / jnp.sum(jnp.exp(g_log - g_max), axis=-1, keepdims=True)
    in_grp = (lane >= g_sel * EXPERTS_PER_GROUP) & (lane < (g_sel + 1) * EXPERTS_PER_GROUP)
    l0 = jnp.where(in_grp, e_log, NEG_INF)
    m1 = jnp.max(l0, axis=-1, keepdims=True)
    i1 = jnp.min(jnp.where(l0 == m1, lane, LANES), axis=-1, keepdims=True)
    l1 = jnp.where(lane == i1, NEG_INF, l0)
    m2 = jnp.max(l1, axis=-1, keepdims=True)
    i2 = jnp.min(jnp.where(l1 == m2, lane, LANES), axis=-1, keepdims=True)
    p2 = jnp.exp(m2 - m1)
    w1 = g_w / (1.0 + p2)
    w2 = g_w * p2 / (1.0 + p2)
    j1 = i1 - g_sel * EXPERTS_PER_GROUP
    j2 = i2 - g_sel * EXPERTS_PER_GROUP
    hx_ref[:, 0:xn.shape[1]] = xn
    hx_ref[:, xn.shape[1]:] = jnp.where(lane == j1, w1, 0.0) + jnp.where(lane == j2, w2, 0.0)

    onehot = jnp.where(lane == g_sel, 1.0, 0.0)
    r_i = lax.broadcasted_iota(jnp.int32, (TM, TM), 0)
    c_i = lax.broadcasted_iota(jnp.int32, (TM, TM), 1)
    before = jnp.where(c_i < r_i, 1.0, 0.0).astype(BF16)
    prior = jnp.dot(before, onehot.astype(BF16), preferred_element_type=F32) + carry_ref[...]
    rank = jnp.sum(jnp.where(lane == g_sel, prior, 0.0), axis=-1, keepdims=True)
    carry_ref[...] += jnp.sum(onehot, axis=0, keepdims=True)
    cnt_ref[...] = carry_ref[...]
    gid_ref[0] = jnp.broadcast_to(g_sel.astype(F32), (TM, LANES)).T[0:1, :].astype(jnp.int32)
    rank_ref[0] = jnp.broadcast_to(rank, (TM, LANES)).T[0:1, :].astype(jnp.int32)


def _moe_dispatch_kernel(seg_ref, pos_ref, hx_ref, xs_ref, zero_ref, sem, zero_sem):
    TM = hx_ref.shape[0]

    @pl.when(pl.program_id(0) == 0)
    def _():
        zero_ref[...] = jnp.zeros_like(zero_ref)

        def zero_tile(t):
            fill = pltpu.make_async_copy(zero_ref, xs_ref.at[pl.ds(t * TM, TM), :], zero_sem)
            fill.start()
            fill.wait()

        for g in range(N_GROUPS):
            first = seg_ref[g - 1] if g else 0
            pl.when(seg_ref[g] > first)(functools.partial(zero_tile, seg_ref[g] - 1))
            tail = seg_ref[N_GROUPS - 1] + g
            pl.when(tail < xs_ref.shape[0] // TM)(functools.partial(zero_tile, tail))

    def start(a, carry):
        for b in range(SUBLANES):
            r = a * SUBLANES + b
            pltpu.make_async_copy(hx_ref.at[pl.ds(r, 1), :], xs_ref.at[pl.ds(pos_ref[0, 0, r], 1), :],
                                  sem).start()
        return carry

    lax.fori_loop(0, TM // SUBLANES, start, 0)
    pltpu.make_async_copy(hx_ref, xs_ref.at[pl.ds(0, TM), :], sem).wait()


def _moe_expert_kernel(grp_ref, xs_ref, wgu_ref, wd_ref, ys_ref, xn_ref, acc_ref):
    j = pl.program_id(0)
    e = pl.program_id(1)
    D = xn_ref.shape[1]
    lane = lax.broadcasted_iota(jnp.int32, (1, LANES), 1)

    @pl.when(grp_ref[j] >= 0)
    def _():
        @pl.when(e == 0)
        def _():
            xn_ref[...] = xs_ref[:, 0:D].astype(BF16)
            acc_ref[...] = jnp.zeros_like(acc_ref)

        hh = jnp.dot(xn_ref[...], wgu_ref[0], preferred_element_type=F32)
        cw = jnp.sum(jnp.where(lane == e, xs_ref[:, D:], 0.0), axis=-1, keepdims=True)
        act = _silu(hh[:, 0:EXPERT_FF]) * hh[:, EXPERT_FF:2 * EXPERT_FF] * cw
        acc_ref[...] += jnp.dot(act.astype(BF16), wd_ref[0], preferred_element_type=F32)

    last = e == EXPERTS_PER_GROUP - 1

    @pl.when(last & (grp_ref[j] >= 0))
    def _():
        ys_ref[...] = acc_ref[...]

    @pl.when(last & (grp_ref[j] < 0))
    def _():
        ys_ref[...] = jnp.zeros_like(ys_ref)


def _moe_combine_kernel(pos_ref, h_ref, ys_ref, o_ref, stage_ref, sem):
    TM = h_ref.shape[0]

    def start(a, carry):
        for b in range(SUBLANES):
            r = a * SUBLANES + b
            pltpu.make_async_copy(ys_ref.at[pl.ds(pos_ref[0, 0, r], 1), :], stage_ref.at[pl.ds(r, 1), :],
                                  sem).start()
        return carry

    lax.fori_loop(0, TM // SUBLANES, start, 0)
    pltpu.make_async_copy(ys_ref.at[pl.ds(0, TM), :], stage_ref, sem).wait()
    o_ref[...] = h_ref[...] + stage_ref[...]


def _moe(h2d, gain, w_group, b_group, w_expert, b_expert, w_gate_up, w_down):
    T, D = h2d.shape
    tr = _pick_tile(T, MOE_ROUTER_TILES)
    tm = _pick_tile(T, MOE_ROW_TILES)
    n_tiles = T // tm + N_GROUPS
    XW = D + LANES
    wr = jnp.zeros((D, 2 * LANES), F32)
    wr = wr.at[:, 0:N_EXPERTS].set(w_expert.astype(F32))
    wr = wr.at[:, LANES:LANES + N_GROUPS].set(w_group.astype(F32))
    br = jnp.zeros((1, 2 * LANES), F32)
    br = br.at[0, 0:N_EXPERTS].set(b_expert.astype(F32))
    br = br.at[0, LANES:LANES + N_GROUPS].set(b_group.astype(F32))
    wgu = w_gate_up.reshape(N_EXPERTS, D, 2 * EXPERT_FF).astype(BF16)
    wd = w_down.reshape(N_EXPERTS, EXPERT_FF, D).astype(BF16)

    hx, gid, rank, counts = pl.pallas_call(
        _moe_router_kernel,
        grid=(T // tr,),
        in_specs=[pl.BlockSpec((tr, D), lambda i: (i, 0)),
                  pl.BlockSpec((1, D), lambda i: (0, 0)),
                  pl.BlockSpec((D, 2 * LANES), lambda i: (0, 0)),
                  pl.BlockSpec((1, 2 * LANES), lambda i: (0, 0))],
        out_specs=[pl.BlockSpec((tr, XW), lambda i: (i, 0)),
                   pl.BlockSpec((1, 1, tr), lambda i: (i, 0, 0)),
                   pl.BlockSpec((1, 1, tr), lambda i: (i, 0, 0)),
                   pl.BlockSpec((1, LANES), lambda i: (0, 0))],
        out_shape=[jax.ShapeDtypeStruct((T, XW), F32),
                   jax.ShapeDtypeStruct((T // tr, 1, tr), jnp.int32),
                   jax.ShapeDtypeStruct((T // tr, 1, tr), jnp.int32),
                   jax.ShapeDtypeStruct((1, LANES), F32)],
        scratch_shapes=[pltpu.VMEM((1, LANES), F32)],
        compiler_params=_compiler_params(("arbitrary",)),
        name="moe_router",
    )(h2d, gain.reshape(1, D), wr, br)

    seg_tiles = -(-counts[0, 0:N_GROUPS].astype(jnp.int32) // tm)
    seg_end = jnp.cumsum(seg_tiles)
    seg_start = seg_end - seg_tiles
    pos = (seg_start * tm)[gid.reshape(T)] + rank.reshape(T)
    tile_ids = jnp.arange(n_tiles, dtype=jnp.int32)
    tile_group = jnp.sum((tile_ids[:, None] >= seg_end[None, :]).astype(jnp.int32), axis=1)
    tile_group = jnp.where(tile_ids < seg_end[N_GROUPS - 1], tile_group, -1)
    pos3 = pos.reshape(T // tm, 1, tm)

    xs = pl.pallas_call(
        _moe_dispatch_kernel,
        grid_spec=pltpu.PrefetchScalarGridSpec(
            num_scalar_prefetch=1,
            grid=(T // tm,),
            in_specs=[pl.BlockSpec((1, 1, tm), lambda i, seg: (i, 0, 0), memory_space=pltpu.SMEM),
                      pl.BlockSpec((tm, XW), lambda i, seg: (i, 0))],
            out_specs=pl.BlockSpec(memory_space=pl.ANY),
            scratch_shapes=[pltpu.VMEM((tm, XW), F32), pltpu.SemaphoreType.DMA(()),
                            pltpu.SemaphoreType.DMA(())]),
        out_shape=jax.ShapeDtypeStruct((n_tiles * tm, XW), F32),
        compiler_params=_compiler_params(("arbitrary",)),
        name="moe_dispatch",
    )(seg_end, pos3, hx)

    def expert_index(j, e, grp):
        return (jnp.maximum(grp[j], 0) * EXPERTS_PER_GROUP + e, 0, 0)

    ys = pl.pallas_call(
        _moe_expert_kernel,
        grid_spec=pltpu.PrefetchScalarGridSpec(
            num_scalar_prefetch=1,
            grid=(n_tiles, EXPERTS_PER_GROUP),
            in_specs=[pl.BlockSpec((tm, XW), lambda j, e, grp: (j, 0)),
                      pl.BlockSpec((1, D, 2 * EXPERT_FF), expert_index),
                      pl.BlockSpec((1, EXPERT_FF, D), expert_index)],
            out_specs=pl.BlockSpec((tm, D), lambda j, e, grp: (j, 0)),
            scratch_shapes=[pltpu.VMEM((tm, D), BF16), pltpu.VMEM((tm, D), F32)]),
        out_shape=jax.ShapeDtypeStruct((n_tiles * tm, D), F32),
        compiler_params=_compiler_params(("parallel", "arbitrary")),
        name="moe_experts",
    )(tile_group, xs, wgu, wd)

    return pl.pallas_call(
        _moe_combine_kernel,
        grid=(T // tm,),
        in_specs=[pl.BlockSpec((1, 1, tm), lambda i: (i, 0, 0), memory_space=pltpu.SMEM),
                  pl.BlockSpec((tm, D), lambda i: (i, 0)),
                  pl.BlockSpec(memory_space=pl.ANY)],
        out_specs=pl.BlockSpec((tm, D), lambda i: (i, 0)),
        out_shape=jax.ShapeDtypeStruct((T, D), F32),
        scratch_shapes=[pltpu.VMEM((tm, D), F32), pltpu.SemaphoreType.DMA(())],
        compiler_params=_compiler_params(("arbitrary",)),
        name="moe_combine",
    )(pos3, h2d, ys)


def _dsa_prep_kernel(p_ref, qg_ref, kg_ref, lng_ref, lnb_ref, q_ref, kt_ref, v_ref, qit_ref, ki_ref, wit_ref):
    lane = lax.broadcasted_iota(jnp.int32, (1, LANES), 1)
    for h in range(ATT_HEADS):
        x = p_ref[:, DSA_Q_OFF + h * ATT_HD:DSA_Q_OFF + (h + 1) * ATT_HD]
        y = x * lax.rsqrt(jnp.mean(x * x, axis=-1, keepdims=True) + RMS_EPS) * qg_ref[...]
        q_ref[:, h * ATT_HD:(h + 1) * ATT_HD] = (y * (ATT_HD ** -0.5)).astype(BF16)
    for n in range(ATT_KV_HEADS):
        x = p_ref[:, DSA_K_OFF + n * ATT_HD:DSA_K_OFF + (n + 1) * ATT_HD]
        y = x * lax.rsqrt(jnp.mean(x * x, axis=-1, keepdims=True) + RMS_EPS) * kg_ref[...]
        kt_ref[0, n * ATT_HD:(n + 1) * ATT_HD, :] = y.T.astype(BF16)
    v_ref[...] = p_ref[:, DSA_V_OFF:DSA_V_OFF + ATT_KV_DIM].astype(BF16)
    qit_ref[0] = p_ref[:, DSA_QI_OFF:DSA_QI_OFF + IDX_HEADS * LANES].T.astype(BF16)
    x = p_ref[:, DSA_KI_OFF:DSA_KI_OFF + LANES]
    live = lane < IDX_HD
    mu = jnp.sum(x, axis=-1, keepdims=True) * (1.0 / IDX_HD)
    xc = jnp.where(live, x - mu, 0.0)
    var = jnp.sum(xc * xc, axis=-1, keepdims=True) * (1.0 / IDX_HD)
    ki = xc * lax.rsqrt(var + RMS_EPS) * lng_ref[...] + lnb_ref[...]
    ki_ref[...] = jnp.where(live, ki, 0.0).astype(BF16)
    wit_ref[0] = (p_ref[:, DSA_WI_OFF:DSA_WI_OFF + LANES] * (IDX_HEADS ** -0.5 * IDX_HD ** -0.5)).T


def _dsa_in_kernel(x_ref, g_ref, w_ref, qg_ref, kg_ref, lng_ref, lnb_ref,
                   q_ref, kt_ref, v_ref, qit_ref, ki_ref, wit_ref, proj_ref):
    _norm_matmul_kernel(x_ref, g_ref, w_ref, proj_ref)
    _dsa_prep_kernel(proj_ref, qg_ref, kg_ref, lng_ref, lnb_ref, q_ref, kt_ref, v_ref, qit_ref, ki_ref, wit_ref)


def _dsa_in(h2d, gain, w_bf16, q_gain, k_gain, ln_g, ln_b):
    T, D = h2d.shape
    W = w_bf16.shape[1]
    tm = KEY_TILE
    assert T % tm == 0 and ATT_TILE == KEY_TILE
    lng = jnp.zeros((1, LANES), F32).at[0, 0:IDX_HD].set(ln_g.astype(F32))
    lnb = jnp.zeros((1, LANES), F32).at[0, 0:IDX_HD].set(ln_b.astype(F32))
    row = lambda i: (i, 0)
    fix = lambda i: (0, 0)
    blk = lambda i: (i, 0, 0)
    return pl.pallas_call(
        _dsa_in_kernel,
        grid=(T // tm,),
        in_specs=[pl.BlockSpec((tm, D), row), pl.BlockSpec((1, D), fix), pl.BlockSpec((D, W), fix),
                  pl.BlockSpec((1, ATT_HD), fix), pl.BlockSpec((1, ATT_HD), fix),
                  pl.BlockSpec((1, LANES), fix), pl.BlockSpec((1, LANES), fix)],
        out_specs=[pl.BlockSpec((tm, ATT_Q_DIM), row), pl.BlockSpec((1, ATT_KV_DIM, tm), blk),
                   pl.BlockSpec((tm, ATT_KV_DIM), row), pl.BlockSpec((1, IDX_HEADS * LANES, tm), blk),
                   pl.BlockSpec((tm, LANES), row), pl.BlockSpec((1, LANES, tm), blk)],
        out_shape=[jax.ShapeDtypeStruct((T, ATT_Q_DIM), BF16), jax.ShapeDtypeStruct((T // tm, ATT_KV_DIM, tm), BF16),
                   jax.ShapeDtypeStruct((T, ATT_KV_DIM), BF16),
                   jax.ShapeDtypeStruct((T // tm, IDX_HEADS * LANES, tm), BF16),
                   jax.ShapeDtypeStruct((T, LANES), BF16), jax.ShapeDtypeStruct((T // tm, LANES, tm), F32)],
        scratch_shapes=[pltpu.VMEM((tm, W), F32)],
        compiler_params=_compiler_params(("parallel",)),
        name="dsa_in",
    )(h2d, gain.reshape(1, D), w_bf16, q_gain.reshape(1, ATT_HD), k_gain.reshape(1, ATT_HD), lng, lnb)


def _dsa_kernel(q_ref, qit_ref, wit_ref, kt_ref, v_ref, ki_ref, bias_ref, bmax_ref, o_ref,
                sc_ref, mask_ref, qg_ref, m_ref, l_ref, acc_ref, knorm_ref, *, n_select, live_rows):
    i = pl.program_id(1)
    TQ = ATT_TILE
    TK = KEY_TILE
    HALVES = TK // LANES

    @pl.when(i == 0)
    def _():
        def body(kt, mx):
            kk = kt_ref[0, kt].astype(F32)
            return tuple(jnp.maximum(mx[n], jnp.sum(jnp.square(kk[n * ATT_HD:(n + 1) * ATT_HD, :]),
                                                    axis=0, keepdims=True)) for n in range(ATT_KV_HEADS))

        mx = lax.fori_loop(0, kt_ref.shape[1], body,
                           tuple(jnp.zeros((1, TK), F32) for _ in range(ATT_KV_HEADS)))
        for n in range(ATT_KV_HEADS):
            knorm_ref[n] = jnp.broadcast_to(jnp.sqrt(jnp.max(mx[n], axis=-1, keepdims=True)), (1, LANES))

    @pl.when(i * TQ >= live_rows)
    def _():
        o_ref[...] = jnp.zeros_like(o_ref)

    @pl.when(i * TQ < live_rows)
    def _():
        _dsa_tile(i, q_ref, qit_ref, wit_ref, kt_ref, v_ref, ki_ref, bias_ref, bmax_ref, o_ref,
                  sc_ref, mask_ref, qg_ref, m_ref, l_ref, acc_ref, knorm_ref, n_select, TQ, TK, HALVES)


def _dsa_tile(i, q_ref, qit_ref, wit_ref, kt_ref, v_ref, ki_ref, bias_ref, bmax_ref, o_ref,
              sc_ref, mask_ref, qg_ref, m_ref, l_ref, acc_ref, knorm_ref, n_select, TQ, TK, HALVES):
    assert TQ == TK
    nkt = i + 1
    fmin = float(jnp.finfo(F32).min)
    kf = float(n_select)

    chunk_shift = int(math.log2(CHUNK))
    q_pos = i * TQ + lax.broadcasted_iota(jnp.int32, (1, TQ), 1)
    q_chunk = jnp.maximum(lax.shift_right_logical(q_pos, chunk_shift), FRONT_PAD // CHUNK)
    k_row = lax.broadcasted_iota(jnp.int32, (TK, 1), 0)

    def halves(x):
        return [x[:, j * LANES:(j + 1) * LANES] for j in range(HALVES)]

    def fold(x, op):
        parts = [x[r * SUBLANES:(r + 1) * SUBLANES, :] for r in range(TK // SUBLANES)]
        while len(parts) > 1:
            parts = [op(parts[j], parts[j + 1]) for j in range(0, len(parts), 2)]
        return parts[0]

    def score_tiles(kts):
        for kt in kts:
            keys = ki_ref[0, pl.ds(pl.multiple_of(kt * TK, TK), TK), :]
            acc = None
            for h in range(IDX_HEADS):
                d = jnp.dot(keys, qit_ref[0, 0, h * LANES:(h + 1) * LANES, :], preferred_element_type=F32)
                term = wit_ref[0, 0, h:h + 1, :] * jnp.maximum(d, 0.0)
                acc = term if acc is None else acc + term
            sc_ref[kt] = acc
            mask_ref[kt] = acc.T

    def tile_loop(n, fn):
        def body(kt, carry):
            fn([kt])
            return carry

        lax.fori_loop(0, n, body, 0)

    def paired_loop(n, fn):
        def body(j, carry):
            fn([2 * j, 2 * j + 1])
            return carry

        lax.fori_loop(0, lax.shift_right_logical(n, 1), body, 0)

        @pl.when(lax.bitwise_and(n, 1) == 1)
        def _():
            fn([n - 1])

    paired_loop(nkt, score_tiles)

    def mask_inadmissible(kt):
        k_pos = kt * TK + k_row
        adm = (lax.shift_right_logical(k_pos, chunk_shift) <= q_chunk) & (k_pos >= FRONT_PAD)
        sc_ref[kt] = jnp.where(adm, sc_ref[kt], NEG_INF)

    mask_inadmissible(0)

    @pl.when(nkt > 1)
    def _():
        mask_inadmissible(nkt - 1)

    n_adm = CHUNK * (q_chunk + 1) - FRONT_PAD
    need = n_adm > n_select

    def count_where(pred):
        def one(kt):
            return fold(jnp.where(pred(sc_ref[kt], kt), 1.0, 0.0), jnp.add)

        def two(j, acc):
            return acc + (one(2 * j) + one(2 * j + 1))

        acc = lax.fori_loop(0, lax.shift_right_logical(nkt, 1), two, jnp.zeros((SUBLANES, TQ), F32))
        acc = lax.cond(lax.bitwise_and(nkt, 1) == 1, lambda a: a + one(nkt - 1), lambda a: a, acc)
        return jnp.sum(acc, axis=0, keepdims=True)

    def count_ge(t):
        return count_where(lambda s, kt: s >= t)

    def minmax_body(kt, carry):
        mn, mx = carry
        s = sc_ref[kt]
        mn = jnp.minimum(mn, fold(jnp.where(s == NEG_INF, jnp.inf, s), jnp.minimum))
        mx = jnp.maximum(mx, fold(s, jnp.maximum))
        return mn, mx

    mn_l, mx_l = lax.fori_loop(0, nkt, minmax_body,
                               (jnp.full((SUBLANES, TQ), jnp.inf, F32), jnp.full((SUBLANES, TQ), NEG_INF, F32)))
    s_min = jnp.min(mn_l, axis=0, keepdims=True)
    s_max = jnp.max(mx_l, axis=0, keepdims=True)
    lo0 = s_min
    hi0 = s_max + (s_max - s_min) + 1.0
    cnt0 = n_adm.astype(F32)
    active0 = jnp.where(need, 1.0, 0.0)

    def any_active(active):
        return jnp.max(active) > 0.0

    def bis_step(lo, hi, cnt, active):
        mid = lo + (hi - lo) * 0.5
        c = count_ge(mid)
        ge = c >= kf
        moving = (mid > lo) & (mid < hi) & (active > 0.0)
        lo_n = jnp.where(moving & ge, mid, lo)
        hi_n = jnp.where(moving & (~ge), mid, hi)
        cnt_n = jnp.where(moving & ge, c, cnt)
        active_n = jnp.where(moving & (cnt_n != kf), 1.0, 0.0)
        return lo_n, hi_n, cnt_n, active_n

    def bis_body(st):
        it, rest = st[0], st[1:]
        for _ in range(BISECT_UNROLL):
            rest = bis_step(*rest)
        return (it + BISECT_UNROLL,) + rest

    _, lo, hi, cnt, _ = lax.while_loop(
        lambda st: jnp.logical_and(st[0] < BISECT_STEPS, any_active(st[4])), bis_body,
        (jnp.int32(0), lo0, hi0, cnt0, active0))

    def snap_body(st):
        lo, hi, cnt, active = st

        def body(kt, mx):
            s = sc_ref[kt]
            return jnp.maximum(mx, fold(jnp.where(s < hi, s, NEG_INF), jnp.maximum))

        v1 = jnp.max(lax.fori_loop(0, nkt, body, jnp.full((SUBLANES, TQ), NEG_INF, F32)),
                     axis=0, keepdims=True)
        c1 = count_ge(v1)
        on = active > 0.0
        hit = on & (c1 >= kf)
        lo_n = jnp.where(hit, v1, lo)
        cnt_n = jnp.where(hit, c1, cnt)
        hi_n = jnp.where(on & (~hit), v1, hi)
        return lo_n, hi_n, cnt_n, jnp.where(on & (~hit), 1.0, 0.0)

    unsettled = jnp.where(need & (cnt != kf), 1.0, 0.0)
    lo, hi, cnt, _ = lax.while_loop(lambda st: any_active(st[3]), snap_body, (lo, hi, cnt, unsettled))

    thr = jnp.where(need, lo, fmin)
    tied = need & (cnt > kf)
    big_pos = jnp.int32(2 ** 30)

    def tie_break():
        want = kf - count_where(lambda s, kt: s > thr)

        def body(_, st):
            jl, jh = st
            jm = lax.shift_right_arithmetic(jl + jh, 1)
            c = count_where(lambda s, kt: (s == thr) & (kt * TK + k_row <= jm))
            ok = c >= want
            return jnp.where(ok, jl, jm), jnp.where(ok, jm, jh)

        n_keys = sc_ref.shape[0] * TK
        steps = int(math.ceil(math.log2(n_keys + 1))) + 1
        jl0 = jnp.full((1, TQ), -1, jnp.int32)
        jh0 = jnp.full((1, TQ), n_keys, jnp.int32)
        _, jh = lax.fori_loop(0, steps, body, (jl0, jh0))
        return jnp.where(tied, jh, big_pos)

    pos_cap = lax.cond(jnp.max(jnp.where(tied, 1.0, 0.0)) > 0.0, tie_break,
                       lambda: jnp.full((1, TQ), big_pos, jnp.int32))

    def lane_wide(row):
        return jnp.broadcast_to(row, (LANES, TQ)).T

    thr_w = lane_wide(thr)
    cap_w = lane_wide(pos_cap)
    chunk_w = lane_wide(q_chunk)
    k_lane = lax.broadcasted_iota(jnp.int32, (1, LANES), 1)

    def to_mask(kt, edge_tile):
        out = []
        for j, s in enumerate(halves(mask_ref[kt])):
            k_pos = kt * TK + j * LANES + k_lane
            sel = (s > thr_w) | ((s == thr_w) & (k_pos <= cap_w))
            if edge_tile:
                sel = sel & (lax.shift_right_logical(k_pos, chunk_shift) <= chunk_w) & (k_pos >= FRONT_PAD)
            out.append(jnp.where(sel, 0.0, NEG_INF))
        mask_ref[kt] = jnp.concatenate(out, axis=1)

    def inner_mask(kt, carry):
        to_mask(kt, False)
        return carry

    lax.fori_loop(1, nkt - 1, inner_mask, 0)
    to_mask(0, True)

    @pl.when(nkt > 1)
    def _():
        to_mask(nkt - 1, True)

    for n in range(ATT_KV_HEADS):
        for g in range(ATT_GROUP):
            h = n * ATT_GROUP + g
            qg_ref[n, g * TQ:(g + 1) * TQ, :] = q_ref[0, :, h * ATT_HD:(h + 1) * ATT_HD]
    l_ref[...] = jnp.zeros_like(l_ref)
    acc_ref[...] = jnp.zeros_like(acc_ref)

    kv_heads = range(ATT_KV_HEADS)

    def raw_logits(kts):
        return [[jnp.dot(qg_ref[n], kt_ref[0, kt, n * ATT_HD:(n + 1) * ATT_HD, :], preferred_element_type=F32)
                 for n in kv_heads] for kt in kts]

    def masked_logits(s, mask, n, g, b0):
        sg = s[g * TQ:(g + 1) * TQ, :] + mask
        if b0 is not None:
            h = n * ATT_GROUP + g
            sg = sg + jnp.concatenate([bias_ref[h, b0 + j] for j in range(HALVES)], axis=1)
        return sg

    def max_pass(kts, b0):
        raw = raw_logits(kts)
        masks = [mask_ref[kt] for kt in kts]
        for n in kv_heads:
            for g in range(ATT_GROUP):
                rows = slice(g * TQ, (g + 1) * TQ)
                m = m_ref[n, rows, :]
                for t in range(len(kts)):
                    for part in halves(masked_logits(raw[t][n], masks[t], n, g, b0)):
                        m = jnp.maximum(m, part)
                m_ref[n, rows, :] = m

    def exp_pass(kts, b0):
        raw = raw_logits(kts)
        masks = [mask_ref[kt] for kt in kts]
        for n in kv_heads:
            ps = [[] for _ in kts]
            for g in range(ATT_GROUP):
                rows = slice(g * TQ, (g + 1) * TQ)
                m = m_ref[n, rows, :]
                l_new = l_ref[n, rows, :]
                for t in range(len(kts)):
                    parts = [jnp.exp(part - m) for part in halves(masked_logits(raw[t][n], masks[t], n, g, b0))]
                    l_new = l_new + functools.reduce(lambda a, b: a + b, parts)
                    ps[t].append(jnp.concatenate(parts, axis=1).astype(BF16))
                l_ref[n, rows, :] = l_new
            pv = None
            for t, kt in enumerate(kts):
                vv = v_ref[0, pl.ds(pl.multiple_of(kt * TK, TK), TK), n * ATT_HD:(n + 1) * ATT_HD]
                d = jnp.dot(jnp.concatenate(ps[t], axis=0), vv, preferred_element_type=F32)
                pv = d if pv is None else pv + d
            acc_ref[n] += pv

    def run_pass(tile_fn):
        paired_loop(jnp.maximum(i - 1, 0), lambda kts: tile_fn(kts, None))

        @pl.when(i >= 1)
        def _():
            tile_fn([i - 1], 0)

        tile_fn([i], HALVES)

    for n in range(ATT_KV_HEADS):
        for g in range(ATT_GROUP):
            h = n * ATT_GROUP + g
            qh = qg_ref[n, g * TQ:(g + 1) * TQ, :].astype(F32)
            q_len = jnp.sqrt(jnp.sum(qh * qh, axis=-1, keepdims=True))
            bound = q_len * knorm_ref[n] * SHIFT_SLACK + (bmax_ref[h] + SHIFT_SLACK - 1.0)
            m_ref[n, g * TQ:(g + 1) * TQ, :] = jnp.broadcast_to(bound, (TQ, LANES))
    run_pass(exp_pass)

    def smallest_row_sum():
        return functools.reduce(jnp.minimum, [jnp.min(jnp.sum(l_ref[n], axis=-1, keepdims=True))
                                              for n in range(ATT_KV_HEADS)])

    @pl.when(jnp.logical_not(smallest_row_sum() > ROW_SUM_FLOOR))
    def _():
        m_ref[...] = jnp.full(m_ref.shape, -1e30, F32)
        run_pass(max_pass)
        for n in range(ATT_KV_HEADS):
            m_ref[n] = jnp.broadcast_to(jnp.max(m_ref[n], axis=-1, keepdims=True), m_ref.shape[1:])
        l_ref[...] = jnp.zeros_like(l_ref)
        acc_ref[...] = jnp.zeros_like(acc_ref)
        run_pass(exp_pass)

    for n in range(ATT_KV_HEADS):
        inv_l = 1.0 / jnp.sum(l_ref[n], axis=-1, keepdims=True)
        for g in range(ATT_GROUP):
            h = n * ATT_GROUP + g
            rows = slice(g * TQ, (g + 1) * TQ)
            o_ref[0, :, h * ATT_HD:(h + 1) * ATT_HD] = (acc_ref[n, rows, :] * inv_l[rows, :]).astype(o_ref.dtype)


def _t5_bucket(rel):
    nb = REL_BUCKETS // 2
    max_exact = nb // 2
    ret = jnp.where(rel > 0, nb, 0)
    n = jnp.abs(rel)
    nf = jnp.maximum(n, 1).astype(jnp.float32)
    large = max_exact + (jnp.log(nf / max_exact) / math.log(REL_MAX_DIST / max_exact)
                         * (nb - max_exact)).astype(jnp.int32)
    large = jnp.minimum(large, nb - 1)
    return ret + jnp.where(n < max_exact, n, large)


def _dsa_core(q, kt, v, qit, ki, wit, rel_bias, n_select, live_rows):
    B, LP, _ = q.shape
    nq = LP // ATT_TILE
    nkt = LP // KEY_TILE
    a = jnp.arange(ATT_TILE)[:, None]
    xx = jnp.arange(2 * KEY_TILE)[None, :]
    rel = (xx - KEY_TILE) - a
    far = rel_bias.astype(F32)[REL_BUCKETS // 2 - 1]
    onehot = jax.nn.one_hot(_t5_bucket(rel), REL_BUCKETS, dtype=F32)
    table = jnp.einsum("abk,kh->abh", onehot, rel_bias.astype(F32),
                       precision=lax.Precision.HIGHEST) - far
    bias_near = jnp.transpose(table.reshape(ATT_TILE, 2 * KEY_TILE // LANES, LANES, ATT_HEADS), (3, 1, 0, 2))
    bias_max = jnp.broadcast_to(jnp.maximum(jnp.max(table, axis=(0, 1)), 0.0)[:, None, None],
                                (ATT_HEADS, 1, LANES))
    kern = functools.partial(_dsa_kernel, n_select=n_select, live_rows=live_rows)
    tile = lambda b, i: (b, i, 0)
    tile4 = lambda b, i: (b, i, 0, 0)
    whole3 = lambda b, i: (b, 0, 0)
    whole4 = lambda b, i: (b, 0, 0, 0)
    once = pl.Buffered(1)
    return pl.pallas_call(
        kern,
        grid=(B, nq),
        in_specs=[pl.BlockSpec((1, ATT_TILE, ATT_Q_DIM), tile),
                  pl.BlockSpec((1, 1, IDX_HEADS * LANES, ATT_TILE), tile4),
                  pl.BlockSpec((1, 1, LANES, ATT_TILE), tile4),
                  pl.BlockSpec((1, nkt, ATT_KV_DIM, KEY_TILE), whole4, pipeline_mode=once),
                  pl.BlockSpec((1, LP, ATT_KV_DIM), whole3, pipeline_mode=once),
                  pl.BlockSpec((1, LP, LANES), whole3, pipeline_mode=once),
                  pl.BlockSpec((ATT_HEADS, 2 * KEY_TILE // LANES, ATT_TILE, LANES), lambda b, i: (0, 0, 0, 0),
                               pipeline_mode=once),
                  pl.BlockSpec((ATT_HEADS, 1, LANES), lambda b, i: (0, 0, 0))],
        out_specs=pl.BlockSpec((1, ATT_TILE, ATT_Q_DIM), tile),
        out_shape=jax.ShapeDtypeStruct((B, LP, ATT_Q_DIM), BF16),
        scratch_shapes=[pltpu.VMEM((nkt, KEY_TILE, ATT_TILE), F32),
                        pltpu.VMEM((nkt, ATT_TILE, KEY_TILE), F32),
                        pltpu.VMEM((ATT_KV_HEADS, ATT_GROUP * ATT_TILE, ATT_HD), BF16),
                        pltpu.VMEM((ATT_KV_HEADS, ATT_GROUP * ATT_TILE, LANES), F32),
                        pltpu.VMEM((ATT_KV_HEADS, ATT_GROUP * ATT_TILE, LANES), F32),
                        pltpu.VMEM((ATT_KV_HEADS, ATT_GROUP * ATT_TILE, ATT_HD), F32),
                        pltpu.VMEM((ATT_KV_HEADS, 1, LANES), F32)],
        compiler_params=_compiler_params(("arbitrary", "arbitrary")),
        name="dsa_core",
    )(q, qit, wit, kt, v, ki, bias_near, bias_max)


def _gdn_in_weight(w_in):
    D = w_in.shape[0]
    w = jnp.zeros((D, GDN_PROJ_W), F32).at[:, 0:w_in.shape[1]].set(w_in.astype(F32))
    return w.astype(BF16)


def _dsa_in_weight(w_in):
    D = w_in.shape[0]
    w_in = w_in.astype(BF16)
    zeros = lambda n: jnp.zeros((D, n), BF16)
    src = ATT_Q_DIM + 2 * ATT_KV_DIM
    parts = [w_in[:, 0:src]]
    for h in range(IDX_HEADS):
        parts += [w_in[:, src + h * IDX_HD:src + (h + 1) * IDX_HD], zeros(LANES - IDX_HD)]
    src += IDX_HEADS * IDX_HD
    parts += [w_in[:, src:src + IDX_HD], zeros(LANES - IDX_HD),
              w_in[:, src + IDX_HD:src + IDX_HD + IDX_HEADS], zeros(LANES - IDX_HEADS)]
    w = jnp.concatenate(parts, axis=1)
    assert w.shape[1] == DSA_PROJ_W
    return w


def kernel(x, meta_tokens, norm_mix, norm_ffn, rel_bias, gdn_w_in, gdn_conv, gdn_a_log, gdn_dt_bias,
           gdn_o_norm, gdn_w_out, dsa_w_in, dsa_q_norm, dsa_k_norm, dsa_idx_ln_g, dsa_idx_ln_b, dsa_w_out,
           moe_w_group, moe_b_group, moe_w_expert, moe_b_expert, moe_w_gate_up, moe_w_down):
    B, S, D = x.shape
    depth = norm_mix.shape[0]
    n_select = min(TOPK_MAX, S // 4)
    off = FRONT_PAD + N_META
    LP = -(-(off + S) // ROW_ALIGN) * ROW_ALIGN
    meta = jnp.broadcast_to(meta_tokens.astype(x.dtype)[None], (B, N_META, D))
    h = jnp.concatenate([jnp.zeros((B, FRONT_PAD, D), x.dtype), meta, x,
                         jnp.zeros((B, LP - off - S, D), x.dtype)], axis=1)
    h = h.reshape(B * LP, D)
    for i in range(depth):
        j = i // 2
        if i % 2 == 0:
            o = _gdn_core(h.reshape(B, LP, D), norm_mix[i], _gdn_in_weight(gdn_w_in[j]),
                          gdn_conv[j].astype(F32), gdn_a_log[j], gdn_dt_bias[j], gdn_o_norm[j])
            h = _matmul_residual(o.reshape(B * LP, GDN_V_DIM), gdn_w_out[j].astype(BF16), h)
        else:
            q, kt, v, qit, ki, wit = _dsa_in(h, norm_mix[i], _dsa_in_weight(dsa_w_in[j]), dsa_q_norm[j],
                                             dsa_k_norm[j], dsa_idx_ln_g[j], dsa_idx_ln_b[j])
            r3 = lambda t: t.reshape(B, LP, t.shape[-1])
            r4 = lambda t: t.reshape(B, LP // KEY_TILE, t.shape[-2], KEY_TILE)
            o = _dsa_core(r3(q), r4(kt), r3(v), r4(qit), r3(ki), r4(wit), rel_bias, n_select, off + S)
            h = _matmul_residual(o.reshape(B * LP, ATT_Q_DIM), dsa_w_out[j].astype(BF16), h)
        h = _moe(h, norm_ffn[i], moe_w_group[i], moe_b_group[i], moe_w_expert[i], moe_b_expert[i],
                 moe_w_gate_up[i], moe_w_down[i])
    return h.reshape(B, LP, D)[:, off:off + S]
```

```python
import functools
import math

import jax
import jax.numpy as jnp
from jax import lax
from jax.experimental import pallas as pl
from jax.experimental.pallas import tpu as pltpu

F32 = jnp.float32
BF16 = jnp.bfloat16

LANES = 128
SUBLANES = 8
VMEM_LIMIT_BYTES = 56 * 1024 * 1024

CHUNK = 64
N_META = 16
META_PAD = (-N_META) % CHUNK
RMS_EPS = 1e-6
L2_EPS = 1e-6

GDN_HEADS = 8
GDN_DK = 128
GDN_DV = 128
GDN_CONV = 4
GDN_QK_DIM = GDN_HEADS * GDN_DK
GDN_V_DIM = GDN_HEADS * GDN_DV
GDN_CONV_DIM = 2 * GDN_QK_DIM + GDN_V_DIM
GDN_GATE_OFF = GDN_CONV_DIM + GDN_V_DIM
GDN_PROJ_W = GDN_GATE_OFF + LANES

ATT_HEADS = 8
ATT_KV_HEADS = 2
ATT_GROUP = ATT_HEADS // ATT_KV_HEADS
ATT_HD = 128
ATT_Q_DIM = ATT_HEADS * ATT_HD
ATT_KV_DIM = ATT_KV_HEADS * ATT_HD
IDX_HEADS = 8
IDX_HD = 64
TOPK_MAX = 256
REL_BUCKETS = 32
REL_MAX_DIST = 128

DSA_Q_OFF = 0
DSA_K_OFF = ATT_Q_DIM
DSA_V_OFF = DSA_K_OFF + ATT_KV_DIM
DSA_QI_OFF = DSA_V_OFF + ATT_KV_DIM
DSA_KI_OFF = DSA_QI_OFF + IDX_HEADS * LANES
DSA_WI_OFF = DSA_KI_OFF + LANES
DSA_PROJ_W = DSA_WI_OFF + LANES

N_GROUPS = 4
EXPERTS_PER_GROUP = 8
N_EXPERTS = N_GROUPS * EXPERTS_PER_GROUP
EXPERT_FF = 256

GDN_ROW_TILES = (256, 128, 64)
MOE_ROUTER_TILES = (512, 256, 128)
MOE_ROW_TILES = (512, 256, 128)

ATT_TILE = 256
KEY_TILE = 256
ROW_ALIGN = KEY_TILE
FRONT_PAD = ROW_ALIGN - N_META
assert FRONT_PAD % CHUNK == META_PAD
BISECT_UNROLL = 4
BISECT_STEPS = 20
SHIFT_SLACK = 1.001
ROW_SUM_FLOOR = 1e-30
NEG_INF = float("-inf")


def _pick_tile(n, candidates):
    for c in candidates:
        if n % c == 0:
            return c
    raise ValueError(f"no tile for {n}")


def _compiler_params(semantics):
    return pltpu.CompilerParams(dimension_semantics=semantics, vmem_limit_bytes=VMEM_LIMIT_BYTES)


def _silu(x):
    return x * jax.nn.sigmoid(x)


def _norm_matmul_kernel(x_ref, g_ref, w_ref, o_ref):
    x = x_ref[...]
    y = x * lax.rsqrt(jnp.mean(x * x, axis=-1, keepdims=True) + RMS_EPS) * g_ref[...]
    o_ref[...] = jnp.dot(y.astype(BF16), w_ref[...], preferred_element_type=F32)


def _matmul_residual_kernel(a_ref, w_ref, r_ref, o_ref):
    o_ref[...] = r_ref[...] + jnp.dot(a_ref[...], w_ref[...], preferred_element_type=F32)


def _nt_dot(a, b):
    return lax.dot_general(a, b, (((1,), (1,)), ((), ())), preferred_element_type=F32)


def _tn_dot(a, b):
    return lax.dot_general(a, b, (((0,), (0,)), ((), ())), preferred_element_type=F32)


def _gdn_kernel(x_ref, g_ref, w_ref, conv_ref, gate_ref, ogain_ref, o_ref,
                state_ref, xbuf_ref, qkv_ref, proj_ref):
    t = pl.program_id(1)
    rows = x_ref.shape[1]

    @pl.when(t == 0)
    def _():
        state_ref[...] = jnp.zeros_like(state_ref)
        xbuf_ref[0:SUBLANES, :] = jnp.zeros((SUBLANES, GDN_CONV_DIM), F32)

    _norm_matmul_kernel(x_ref.at[0], g_ref, w_ref, proj_ref)

    def chunk(cc, carry):
        r0 = pl.multiple_of(cc * CHUNK, CHUNK)
        _gdn_chunk(t * rows + r0, r0, proj_ref, conv_ref, gate_ref, ogain_ref, o_ref, state_ref, xbuf_ref, qkv_ref)
        return carry

    lax.fori_loop(0, rows // CHUNK, chunk, 0)


def _gdn_chunk(row0, r0, proj_ref, conv_ref, gate_ref, ogain_ref, o_ref, state_ref, xbuf_ref, qkv_ref):
    C = CHUNK
    row = row0 + lax.broadcasted_iota(jnp.int32, (C, 1), 0)
    live = row >= FRONT_PAD
    u = jnp.where(live, proj_ref[pl.ds(r0, C), 0:GDN_CONV_DIM], 0.0)
    xbuf_ref[SUBLANES:SUBLANES + C, :] = u
    conv = conv_ref[0:1, :] * xbuf_ref[SUBLANES - 3:SUBLANES - 3 + C, :]
    for j in range(1, GDN_CONV):
        conv = conv + conv_ref[j:j + 1, :] * xbuf_ref[SUBLANES - 3 + j:SUBLANES - 3 + j + C, :]
    xbuf_ref[0:SUBLANES, :] = u[C - SUBLANES:C, :]
    qkv_ref[...] = _silu(conv)
    qkv = qkv_ref

    gates = jnp.where(live, proj_ref[pl.ds(r0, C), GDN_GATE_OFF:GDN_GATE_OFF + LANES], 0.0)
    beta = jax.nn.sigmoid(gates)
    neg_rate = -jnp.exp(gate_ref[0:1, :])
    g = neg_rate * jax.nn.softplus(gates + gate_ref[1:2, :])
    r_i = lax.broadcasted_iota(jnp.int32, (C, C), 0)
    c_i = lax.broadcasted_iota(jnp.int32, (C, C), 1)
    causal = c_i <= r_i
    strict = c_i < r_i
    gc = jnp.dot(causal.astype(F32), g, preferred_element_type=F32,
                 precision=lax.Precision.HIGHEST)
    gc_t = gc.T
    eg = jnp.exp(gc)
    g_last = gc[C - 1:C, :]
    eg_last = jnp.exp(g_last)
    ek = jnp.exp(g_last - gc)

    heads = range(GDN_HEADS)
    dot = functools.partial(jnp.dot, preferred_element_type=F32)
    col = lambda a, h: a[:, GDN_HEADS + h:GDN_HEADS + h + 1]
    qs, ks, k16s, xs, decays = [], [], [], [], []
    for h in heads:
        lo = h * GDN_DK
        q = qkv[:, lo:lo + GDN_DK]
        k = qkv[:, GDN_QK_DIM + lo:GDN_QK_DIM + lo + GDN_DK]
        v = qkv[:, 2 * GDN_QK_DIM + h * GDN_DV:2 * GDN_QK_DIM + (h + 1) * GDN_DV]
        q = q * lax.rsqrt(jnp.sum(q * q, axis=-1, keepdims=True) + L2_EPS) * (GDN_DK ** -0.5)
        k = k * lax.rsqrt(jnp.sum(k * k, axis=-1, keepdims=True) + L2_EPS)
        kb = k * beta[:, h:h + 1]
        vb = v * beta[:, h:h + 1]
        qs.append(q)
        ks.append(k)
        k16s.append(k.astype(BF16))
        xs.append((kb, jnp.concatenate([vb, kb * col(eg, h)], axis=1)))
        decays.append(jnp.exp(jnp.where(causal, col(gc, h) - gc_t[GDN_HEADS + h:GDN_HEADS + h + 1, :],
                                        NEG_INF)))
    kk = [_nt_dot(xs[h][0].astype(BF16), k16s[h]) for h in heads]
    qk = [_nt_dot(qs[h].astype(BF16), k16s[h]) for h in heads]
    ns = [-jnp.where(strict, kk[h] * decays[h], 0.0) for h in heads]
    xs = [x for _, x in xs]
    for it in range(6):
        n16 = [n.astype(BF16) for n in ns]
        xs = [xs[h] + dot(n16[h], xs[h].astype(BF16)) for h in heads]
        if it < 5:
            ns = [dot(n16[h], n16[h]) for h in heads]
    qk = [jnp.where(causal, qk[h] * decays[h], 0.0).astype(BF16) for h in heads]
    states = [state_ref[h] for h in heads]
    s16 = [s.astype(BF16) for s in states]
    ws = [dot(xs[h][:, GDN_DV:2 * GDN_DV].astype(BF16), s16[h]) for h in heads]
    qs_s = [dot((qs[h] * col(eg, h)).astype(BF16), s16[h]) for h in heads]
    v_new = [(xs[h][:, 0:GDN_DV] - ws[h]).astype(BF16) for h in heads]
    os_ = [qs_s[h] + dot(qk[h], v_new[h]) for h in heads]
    kv = [_tn_dot((ks[h] * col(ek, h)).astype(BF16), v_new[h]) for h in heads]
    for h in heads:
        state_ref[h] = states[h] * col(eg_last, h) + kv[h]
        z = proj_ref[pl.ds(r0, C), GDN_CONV_DIM + h * GDN_DV:GDN_CONV_DIM + (h + 1) * GDN_DV]
        o = os_[h]
        on = o * lax.rsqrt(jnp.mean(o * o, axis=-1, keepdims=True) + RMS_EPS) * ogain_ref[...]
        o_ref[0, pl.ds(r0, C), h * GDN_DV:(h + 1) * GDN_DV] = (on * _silu(z)).astype(o_ref.dtype)


def _gdn_core(h3d, gain, w_bf16, conv_w, a_log, dt_bias, o_gain):
    B, LP, D = h3d.shape
    W = w_bf16.shape[1]
    rows = _pick_tile(LP, GDN_ROW_TILES)
    gate = jnp.zeros((2, LANES), F32)
    gate = gate.at[0, GDN_HEADS:2 * GDN_HEADS].set(a_log.astype(F32))
    gate = gate.at[1, GDN_HEADS:2 * GDN_HEADS].set(dt_bias.astype(F32))
    fix = lambda b, t: (0, 0)
    return pl.pallas_call(
        _gdn_kernel,
        grid=(B, LP // rows),
        in_specs=[pl.BlockSpec((1, rows, D), lambda b, t: (b, t, 0)),
                  pl.BlockSpec((1, D), fix),
                  pl.BlockSpec((D, W), fix, pipeline_mode=pl.Buffered(1)),
                  pl.BlockSpec((GDN_CONV, GDN_CONV_DIM), fix),
                  pl.BlockSpec((2, LANES), fix),
                  pl.BlockSpec((1, GDN_DV), fix)],
        out_specs=pl.BlockSpec((1, rows, GDN_V_DIM), lambda b, t: (b, t, 0)),
        out_shape=jax.ShapeDtypeStruct((B, LP, GDN_V_DIM), BF16),
        scratch_shapes=[pltpu.VMEM((GDN_HEADS, GDN_DK, GDN_DV), F32),
                        pltpu.VMEM((SUBLANES + CHUNK, GDN_CONV_DIM), F32),
                        pltpu.VMEM((CHUNK, GDN_CONV_DIM), F32),
                        pltpu.VMEM((rows, W), F32)],
        compiler_params=_compiler_params(("parallel", "arbitrary")),
        name="gdn_core",
    )(h3d, gain.reshape(1, D), w_bf16, conv_w, gate, o_gain.reshape(1, GDN_DV))


def _moe_router_kernel(res_ref, mix_ref, wo_ref, gain_ref, wr_ref, br_ref,
                       h_ref, hx_ref, gid_ref, rank_ref, cnt_ref, carry_ref):
    i = pl.program_id(0)
    TM = h_ref.shape[0]
    lane = lax.broadcasted_iota(jnp.int32, (1, LANES), 1)

    @pl.when(i == 0)
    def _():
        carry_ref[...] = jnp.zeros_like(carry_ref)

    _matmul_residual_kernel(mix_ref, wo_ref, res_ref, h_ref)
    x = h_ref[...]
    xn = x * lax.rsqrt(jnp.mean(x * x, axis=-1, keepdims=True) + RMS_EPS) * gain_ref[...]
    logits = jnp.dot(xn, wr_ref[...], preferred_element_type=F32,
                     precision=lax.Precision.HIGHEST) + br_ref[...]
    e_log = logits[:, 0:LANES]
    g_log = jnp.where(lane < N_GROUPS, logits[:, LANES:2 * LANES], NEG_INF)
    g_max = jnp.max(g_log, axis=-1, keepdims=True)
    g_sel = jnp.min(jnp.where(g_log == g_max, lane, LANES), axis=-1, keepdims=True)
    g_sel = jnp.minimum(g_sel, N_GROUPS - 1)
    g_w = 1.0 / jnp.sum(jnp.exp(g_log - g_max), axis=-1, keepdims=True)
    in_grp = (lane >= g_sel * EXPERTS_PER_GROUP) & (lane < (g_sel + 1) * EXPERTS_PER_GROUP)
    l0 = jnp.where(in_grp, e_log, NEG_INF)
    m1 = jnp.max(l0, axis=-1, keepdims=True)
    i1 = jnp.min(jnp.where(l0 == m1, lane, LANES), axis=-1, keepdims=True)
    l1 = jnp.where(lane == i1, NEG_INF, l0)
    m2 = jnp.max(l1, axis=-1, keepdims=True)
    i2 = jnp.min(jnp.where(l1 == m2, lane, LANES), axis=-1, keepdims=True)
    p2 = jnp.exp(m2 - m1)
    w1 = g_w / (1.0 + p2)
    w2 = g_w * p2 / (1.0 + p2)
    j1 = i1 - g_sel * EXPERTS_PER_GROUP
    j2 = i2 - g_sel * EXPERTS_PER_GROUP
    hx_ref[:, 0:xn.shape[1]] = xn
    hx_ref[:, xn.shape[1]:] = jnp.where(lane == j1, w1, 0.0) + jnp.where(lane == j2, w2, 0.0)

    onehot = jnp.where(lane == g_sel, 1.0, 0.0)
    r_i = lax.broadcasted_iota(jnp.int32, (TM, TM), 0)
    c_i = lax.broadcasted_iota(jnp.int32, (TM, TM), 1)
    before = jnp.where(c_i < r_i, 1.0, 0.0).astype(BF16)
    prior = jnp.dot(before, onehot.astype(BF16), preferred_element_type=F32) + carry_ref[...]
    rank = jnp.sum(jnp.where(lane == g_sel, prior, 0.0), axis=-1, keepdims=True)
    carry_ref[...] += jnp.sum(onehot, axis=0, keepdims=True)
    cnt_ref[...] = carry_ref[...]
    gid_ref[0] = jnp.broadcast_to(g_sel.astype(F32), (TM, LANES)).T[0:1, :].astype(jnp.int32)
    rank_ref[0] = jnp.broadcast_to(rank, (TM, LANES)).T[0:1, :].astype(jnp.int32)


def _moe_dispatch_kernel(seg_ref, pos_ref, hx_ref, xs_ref, zero_ref, sem, zero_sem):
    TM = hx_ref.shape[0]

    @pl.when(pl.program_id(0) == 0)
    def _():
        zero_ref[...] = jnp.zeros_like(zero_ref)

        def zero_tile(t):
            fill = pltpu.make_async_copy(zero_ref, xs_ref.at[pl.ds(t * TM, TM), :], zero_sem)
            fill.start()
            fill.wait()

        for g in range(N_GROUPS):
            first = seg_ref[g - 1] if g else 0
            pl.when(seg_ref[g] > first)(functools.partial(zero_tile, seg_ref[g] - 1))
            tail = seg_ref[N_GROUPS - 1] + g
            pl.when(tail < xs_ref.shape[0] // TM)(functools.partial(zero_tile, tail))

    def start(a, carry):
        for b in range(SUBLANES):
            r = a * SUBLANES + b
            pltpu.make_async_copy(hx_ref.at[pl.ds(r, 1), :], xs_ref.at[pl.ds(pos_ref[0, 0, r], 1), :],
                                  sem).start()
        return carry

    lax.fori_loop(0, TM // SUBLANES, start, 0)
    pltpu.make_async_copy(hx_ref, xs_ref.at[pl.ds(0, TM), :], sem).wait()


def _moe_expert_kernel(grp_ref, xs_ref, wgu_ref, wd_ref, ys_ref, xn_ref, acc_ref):
    j = pl.program_id(0)
    e = pl.program_id(1)
    D = xn_ref.shape[1]
    lane = lax.broadcasted_iota(jnp.int32, (1, LANES), 1)

    @pl.when(grp_ref[j] >= 0)
    def _():
        @pl.when(e == 0)
        def _():
            xn_ref[...] = xs_ref[:, 0:D].astype(BF16)
            acc_ref[...] = jnp.zeros_like(acc_ref)

        hh = jnp.dot(xn_ref[...], wgu_ref[0], preferred_element_type=F32)
        cw = jnp.sum(jnp.where(lane == e, xs_ref[:, D:], 0.0), axis=-1, keepdims=True)
        act = _silu(hh[:, 0:EXPERT_FF]) * hh[:, EXPERT_FF:2 * EXPERT_FF] * cw
        acc_ref[...] += jnp.dot(act.astype(BF16), wd_ref[0], preferred_element_type=F32)

    last = e == EXPERTS_PER_GROUP - 1

    @pl.when(last & (grp_ref[j] >= 0))
    def _():
        ys_ref[...] = acc_ref[...]

    @pl.when(last & (grp_ref[j] < 0))
    def _():
        ys_ref[...] = jnp.zeros_like(ys_ref)


def _moe_combine_kernel(pos_ref, h_ref, ys_ref, o_ref, stage_ref, sem):
    TM = h_ref.shape[0]

    def start(a, carry):
        for b in range(SUBLANES):
            r = a * SUBLANES + b
            pltpu.make_async_copy(ys_ref.at[pl.ds(pos_ref[0, 0, r], 1), :], stage_ref.at[pl.ds(r, 1), :],
                                  sem).start()
        return carry

    lax.fori_loop(0, TM // SUBLANES, start, 0)
    pltpu.make_async_copy(ys_ref.at[pl.ds(0, TM), :], stage_ref, sem).wait()
    o_ref[...] = h_ref[...] + stage_ref[...]


def _moe(res2d, mix_bf16, w_out_bf16, gain, w_group, b_group, w_expert, b_expert, w_gate_up, w_down):
    T, D = res2d.shape
    KM = mix_bf16.shape[1]
    tr = _pick_tile(T, MOE_ROUTER_TILES)
    tm = _pick_tile(T, MOE_ROW_TILES)
    n_tiles = T // tm + N_GROUPS
    XW = D + LANES
    wr = jnp.zeros((D, 2 * LANES), F32)
    wr = wr.at[:, 0:N_EXPERTS].set(w_expert.astype(F32))
    wr = wr.at[:, LANES:LANES + N_GROUPS].set(w_group.astype(F32))
    br = jnp.zeros((1, 2 * LANES), F32)
    br = br.at[0, 0:N_EXPERTS].set(b_expert.astype(F32))
    br = br.at[0, LANES:LANES + N_GROUPS].set(b_group.astype(F32))
    wgu = w_gate_up.reshape(N_EXPERTS, D, 2 * EXPERT_FF).astype(BF16)
    wd = w_down.reshape(N_EXPERTS, EXPERT_FF, D).astype(BF16)

    h2d, hx, gid, rank, counts = pl.pallas_call(
        _moe_router_kernel,
        grid=(T // tr,),
        in_specs=[pl.BlockSpec((tr, D), lambda i: (i, 0)),
                  pl.BlockSpec((tr, KM), lambda i: (i, 0)),
                  pl.BlockSpec((KM, D), lambda i: (0, 0)),
                  pl.BlockSpec((1, D), lambda i: (0, 0)),
                  pl.BlockSpec((D, 2 * LANES), lambda i: (0, 0)),
                  pl.BlockSpec((1, 2 * LANES), lambda i: (0, 0))],
        out_specs=[pl.BlockSpec((tr, D), lambda i: (i, 0)),
                   pl.BlockSpec((tr, XW), lambda i: (i, 0)),
                   pl.BlockSpec((1, 1, tr), lambda i: (i, 0, 0)),
                   pl.BlockSpec((1, 1, tr), lambda i: (i, 0, 0)),
                   pl.BlockSpec((1, LANES), lambda i: (0, 0))],
        out_shape=[jax.ShapeDtypeStruct((T, D), F32),
                   jax.ShapeDtypeStruct((T, XW), F32),
                   jax.ShapeDtypeStruct((T // tr, 1, tr), jnp.int32),
                   jax.ShapeDtypeStruct((T // tr, 1, tr), jnp.int32),
                   jax.ShapeDtypeStruct((1, LANES), F32)],
        scratch_shapes=[pltpu.VMEM((1, LANES), F32)],
        compiler_params=_compiler_params(("arbitrary",)),
        name="moe_router",
    )(res2d, mix_bf16, w_out_bf16, gain.reshape(1, D), wr, br)

    seg_tiles = -(-counts[0, 0:N_GROUPS].astype(jnp.int32) // tm)
    seg_end = jnp.cumsum(seg_tiles)
    seg_start = seg_end - seg_tiles
    pos = (seg_start * tm)[gid.reshape(T)] + rank.reshape(T)
    tile_ids = jnp.arange(n_tiles, dtype=jnp.int32)
    tile_group = jnp.sum((tile_ids[:, None] >= seg_end[None, :]).astype(jnp.int32), axis=1)
    tile_group = jnp.where(tile_ids < seg_end[N_GROUPS - 1], tile_group, -1)
    pos3 = pos.reshape(T // tm, 1, tm)

    xs = pl.pallas_call(
        _moe_dispatch_kernel,
        grid_spec=pltpu.PrefetchScalarGridSpec(
            num_scalar_prefetch=1,
            grid=(T // tm,),
            in_specs=[pl.BlockSpec((1, 1, tm), lambda i, seg: (i, 0, 0), memory_space=pltpu.SMEM),
                      pl.BlockSpec((tm, XW), lambda i, seg: (i, 0))],
            out_specs=pl.BlockSpec(memory_space=pl.ANY),
            scratch_shapes=[pltpu.VMEM((tm, XW), F32), pltpu.SemaphoreType.DMA(()),
                            pltpu.SemaphoreType.DMA(())]),
        out_shape=jax.ShapeDtypeStruct((n_tiles * tm, XW), F32),
        compiler_params=_compiler_params(("arbitrary",)),
        name="moe_dispatch",
    )(seg_end, pos3, hx)

    def expert_index(j, e, grp):
        return (jnp.maximum(grp[j], 0) * EXPERTS_PER_GROUP + e, 0, 0)

    ys = pl.pallas_call(
        _moe_expert_kernel,
        grid_spec=pltpu.PrefetchScalarGridSpec(
            num_scalar_prefetch=1,
            grid=(n_tiles, EXPERTS_PER_GROUP),
            in_specs=[pl.BlockSpec((tm, XW), lambda j, e, grp: (j, 0)),
                      pl.BlockSpec((1, D, 2 * EXPERT_FF), expert_index),
                      pl.BlockSpec((1, EXPERT_FF, D), expert_index)],
            out_specs=pl.BlockSpec((tm, D), lambda j, e, grp: (j, 0)),
            scratch_shapes=[pltpu.VMEM((tm, D), BF16), pltpu.VMEM((tm, D), F32)]),
        out_shape=jax.ShapeDtypeStruct((n_tiles * tm, D), F32),
        compiler_params=_compiler_params(("parallel", "arbitrary")),
        name="moe_experts",
    )(tile_group, xs, wgu, wd)

    return pl.pallas_call(
        _moe_combine_kernel,
        grid=(T // tm,),
        in_specs=[pl.BlockSpec((1, 1, tm), lambda i: (i, 0, 0), memory_space=pltpu.SMEM),
                  pl.BlockSpec((tm, D), lambda i: (i, 0)),
                  pl.BlockSpec(memory_space=pl.ANY)],
        out_specs=pl.BlockSpec((tm, D), lambda i: (i, 0)),
        out_shape=jax.ShapeDtypeStruct((T, D), F32),
        scratch_shapes=[pltpu.VMEM((tm, D), F32), pltpu.SemaphoreType.DMA(())],
        compiler_params=_compiler_params(("arbitrary",)),
        name="moe_combine",
    )(pos3, h2d, ys)


def _dsa_prep_kernel(p_ref, qg_ref, kg_ref, lng_ref, lnb_ref, q_ref, kt_ref, v_ref, qit_ref, ki_ref, wit_ref):
    lane = lax.broadcasted_iota(jnp.int32, (1, LANES), 1)
    for h in range(ATT_HEADS):
        x = p_ref[:, DSA_Q_OFF + h * ATT_HD:DSA_Q_OFF + (h + 1) * ATT_HD]
        y = x * lax.rsqrt(jnp.mean(x * x, axis=-1, keepdims=True) + RMS_EPS) * qg_ref[...]
        q_ref[:, h * ATT_HD:(h + 1) * ATT_HD] = (y * (ATT_HD ** -0.5)).astype(BF16)
    for n in range(ATT_KV_HEADS):
        x = p_ref[:, DSA_K_OFF + n * ATT_HD:DSA_K_OFF + (n + 1) * ATT_HD]
        y = x * lax.rsqrt(jnp.mean(x * x, axis=-1, keepdims=True) + RMS_EPS) * kg_ref[...]
        kt_ref[0, n * ATT_HD:(n + 1) * ATT_HD, :] = y.T.astype(BF16)
    v_ref[...] = p_ref[:, DSA_V_OFF:DSA_V_OFF + ATT_KV_DIM].astype(BF16)
    qit_ref[0] = p_ref[:, DSA_QI_OFF:DSA_QI_OFF + IDX_HEADS * LANES].T.astype(BF16)
    x = p_ref[:, DSA_KI_OFF:DSA_KI_OFF + LANES]
    live = lane < IDX_HD
    mu = jnp.sum(x, axis=-1, keepdims=True) * (1.0 / IDX_HD)
    xc = jnp.where(live, x - mu, 0.0)
    var = jnp.sum(xc * xc, axis=-1, keepdims=True) * (1.0 / IDX_HD)
    ki = xc * lax.rsqrt(var + RMS_EPS) * lng_ref[...] + lnb_ref[...]
    ki_ref[...] = jnp.where(live, ki, 0.0).astype(BF16)
    wit_ref[0] = (p_ref[:, DSA_WI_OFF:DSA_WI_OFF + LANES] * (IDX_HEADS ** -0.5 * IDX_HD ** -0.5)).T


def _dsa_in_kernel(x_ref, g_ref, w_ref, qg_ref, kg_ref, lng_ref, lnb_ref,
                   q_ref, kt_ref, v_ref, qit_ref, ki_ref, wit_ref, proj_ref):
    _norm_matmul_kernel(x_ref, g_ref, w_ref, proj_ref)
    _dsa_prep_kernel(proj_ref, qg_ref, kg_ref, lng_ref, lnb_ref, q_ref, kt_ref, v_ref, qit_ref, ki_ref, wit_ref)


def _dsa_in(h2d, gain, w_bf16, q_gain, k_gain, ln_g, ln_b):
    T, D = h2d.shape
    W = w_bf16.shape[1]
    tm = KEY_TILE
    assert T % tm == 0 and ATT_TILE == KEY_TILE
    lng = jnp.zeros((1, LANES), F32).at[0, 0:IDX_HD].set(ln_g.astype(F32))
    lnb = jnp.zeros((1, LANES), F32).at[0, 0:IDX_HD].set(ln_b.astype(F32))
    row = lambda i: (i, 0)
    fix = lambda i: (0, 0)
    blk = lambda i: (i, 0, 0)
    return pl.pallas_call(
        _dsa_in_kernel,
        grid=(T // tm,),
        in_specs=[pl.BlockSpec((tm, D), row), pl.BlockSpec((1, D), fix), pl.BlockSpec((D, W), fix),
                  pl.BlockSpec((1, ATT_HD), fix), pl.BlockSpec((1, ATT_HD), fix),
                  pl.BlockSpec((1, LANES), fix), pl.BlockSpec((1, LANES), fix)],
        out_specs=[pl.BlockSpec((tm, ATT_Q_DIM), row), pl.BlockSpec((1, ATT_KV_DIM, tm), blk),
                   pl.BlockSpec((tm, ATT_KV_DIM), row), pl.BlockSpec((1, IDX_HEADS * LANES, tm), blk),
                   pl.BlockSpec((tm, LANES), row), pl.BlockSpec((1, LANES, tm), blk)],
        out_shape=[jax.ShapeDtypeStruct((T, ATT_Q_DIM), BF16), jax.ShapeDtypeStruct((T // tm, ATT_KV_DIM, tm), BF16),
                   jax.ShapeDtypeStruct((T, ATT_KV_DIM), BF16),
                   jax.ShapeDtypeStruct((T // tm, IDX_HEADS * LANES, tm), BF16),
                   jax.ShapeDtypeStruct((T, LANES), BF16), jax.ShapeDtypeStruct((T // tm, LANES, tm), F32)],
        scratch_shapes=[pltpu.VMEM((tm, W), F32)],
        compiler_params=_compiler_params(("parallel",)),
        name="dsa_in",
    )(h2d, gain.reshape(1, D), w_bf16, q_gain.reshape(1, ATT_HD), k_gain.reshape(1, ATT_HD), lng, lnb)


def _dsa_kernel(q_ref, qit_ref, wit_ref, kt_ref, v_ref, ki_ref, bias_ref, bmax_ref, o_ref,
                sc_ref, mask_ref, qg_ref, m_ref, l_ref, acc_ref, knorm_ref, *, n_select, live_rows):
    i = pl.program_id(1)
    TQ = ATT_TILE
    TK = KEY_TILE
    HALVES = TK // LANES

    @pl.when(i == 0)
    def _():
        def body(kt, mx):
            kk = kt_ref[0, kt].astype(F32)
            return tuple(jnp.maximum(mx[n], jnp.sum(jnp.square(kk[n * ATT_HD:(n + 1) * ATT_HD, :]),
                                                    axis=0, keepdims=True)) for n in range(ATT_KV_HEADS))

        mx = lax.fori_loop(0, kt_ref.shape[1], body,
                           tuple(jnp.zeros((1, TK), F32) for _ in range(ATT_KV_HEADS)))
        for n in range(ATT_KV_HEADS):
            knorm_ref[n] = jnp.broadcast_to(jnp.sqrt(jnp.max(mx[n], axis=-1, keepdims=True)), (1, LANES))

    @pl.when(i * TQ >= live_rows)
    def _():
        o_ref[...] = jnp.zeros_like(o_ref)

    @pl.when(i * TQ < live_rows)
    def _():
        _dsa_tile(i, q_ref, qit_ref, wit_ref, kt_ref, v_ref, ki_ref, bias_ref, bmax_ref, o_ref,
                  sc_ref, mask_ref, qg_ref, m_ref, l_ref, acc_ref, knorm_ref, n_select, TQ, TK, HALVES)


def _dsa_tile(i, q_ref, qit_ref, wit_ref, kt_ref, v_ref, ki_ref, bias_ref, bmax_ref, o_ref,
              sc_ref, mask_ref, qg_ref, m_ref, l_ref, acc_ref, knorm_ref, n_select, TQ, TK, HALVES):
    assert TQ == TK
    nkt = i + 1
    fmin = float(jnp.finfo(F32).min)
    kf = float(n_select)

    chunk_shift = int(math.log2(CHUNK))
    q_pos = i * TQ + lax.broadcasted_iota(jnp.int32, (1, TQ), 1)
    q_chunk = jnp.maximum(lax.shift_right_logical(q_pos, chunk_shift), FRONT_PAD // CHUNK)
    k_row = lax.broadcasted_iota(jnp.int32, (TK, 1), 0)

    def halves(x):
        return [x[:, j * LANES:(j + 1) * LANES] for j in range(HALVES)]

    def fold(x, op):
        parts = [x[r * SUBLANES:(r + 1) * SUBLANES, :] for r in range(TK // SUBLANES)]
        while len(parts) > 1:
            parts = [op(parts[j], parts[j + 1]) for j in range(0, len(parts), 2)]
        return parts[0]

    def score_tiles(kts):
        for kt in kts:
            keys = ki_ref[0, pl.ds(pl.multiple_of(kt * TK, TK), TK), :]
            acc = None
            for h in range(IDX_HEADS):
                d = jnp.dot(keys, qit_ref[0, 0, h * LANES:(h + 1) * LANES, :], preferred_element_type=F32)
                term = wit_ref[0, 0, h:h + 1, :] * jnp.maximum(d, 0.0)
                acc = term if acc is None else acc + term
            sc_ref[kt] = acc
            mask_ref[kt] = acc.T

    def tile_loop(n, fn):
        def body(kt, carry):
            fn([kt])
            return carry

        lax.fori_loop(0, n, body, 0)

    def paired_loop(n, fn):
        def body(j, carry):
            fn([2 * j, 2 * j + 1])
            return carry

        lax.fori_loop(0, lax.shift_right_logical(n, 1), body, 0)

        @pl.when(lax.bitwise_and(n, 1) == 1)
        def _():
            fn([n - 1])

    paired_loop(nkt, score_tiles)

    def mask_inadmissible(kt):
        k_pos = kt * TK + k_row
        adm = (lax.shift_right_logical(k_pos, chunk_shift) <= q_chunk) & (k_pos >= FRONT_PAD)
        sc_ref[kt] = jnp.where(adm, sc_ref[kt], NEG_INF)

    mask_inadmissible(0)

    @pl.when(nkt > 1)
    def _():
        mask_inadmissible(nkt - 1)

    n_adm = CHUNK * (q_chunk + 1) - FRONT_PAD
    need = n_adm > n_select

    def count_where(pred):
        def one(kt):
            return fold(jnp.where(pred(sc_ref[kt], kt), 1.0, 0.0), jnp.add)

        def two(j, acc):
            return acc + (one(2 * j) + one(2 * j + 1))

        acc = lax.fori_loop(0, lax.shift_right_logical(nkt, 1), two, jnp.zeros((SUBLANES, TQ), F32))
        acc = lax.cond(lax.bitwise_and(nkt, 1) == 1, lambda a: a + one(nkt - 1), lambda a: a, acc)
        return jnp.sum(acc, axis=0, keepdims=True)

    def count_ge(t):
        return count_where(lambda s, kt: s >= t)

    def minmax_body(kt, carry):
        mn, mx = carry
        s = sc_ref[kt]
        mn = jnp.minimum(mn, fold(jnp.where(s == NEG_INF, jnp.inf, s), jnp.minimum))
        mx = jnp.maximum(mx, fold(s, jnp.maximum))
        return mn, mx

    mn_l, mx_l = lax.fori_loop(0, nkt, minmax_body,
                               (jnp.full((SUBLANES, TQ), jnp.inf, F32), jnp.full((SUBLANES, TQ), NEG_INF, F32)))
    s_min = jnp.min(mn_l, axis=0, keepdims=True)
    s_max = jnp.max(mx_l, axis=0, keepdims=True)
    lo0 = s_min
    hi0 = s_max + (s_max - s_min) + 1.0
    cnt0 = n_adm.astype(F32)
    active0 = jnp.where(need, 1.0, 0.0)

    def any_active(active):
        return jnp.max(active) > 0.0

    def bis_step(lo, hi, cnt, active):
        mid = lo + (hi - lo) * 0.5
        c = count_ge(mid)
        ge = c >= kf
        moving = (mid > lo) & (mid < hi) & (active > 0.0)
        lo_n = jnp.where(moving & ge, mid, lo)
        hi_n = jnp.where(moving & (~ge), mid, hi)
        cnt_n = jnp.where(moving & ge, c, cnt)
        active_n = jnp.where(moving & (cnt_n != kf), 1.0, 0.0)
        return lo_n, hi_n, cnt_n, active_n

    def bis_body(st):
        it, rest = st[0], st[1:]
        for _ in range(BISECT_UNROLL):
            rest = bis_step(*rest)
        return (it + BISECT_UNROLL,) + rest

    _, lo, hi, cnt, _ = lax.while_loop(
        lambda st: jnp.logical_and(st[0] < BISECT_STEPS, any_active(st[4])), bis_body,
        (jnp.int32(0), lo0, hi0, cnt0, active0))

    def snap_body(st):
        lo, hi, cnt, active = st

        def body(kt, mx):
            s = sc_ref[kt]
            return jnp.maximum(mx, fold(jnp.where(s < hi, s, NEG_INF), jnp.maximum))

        v1 = jnp.max(lax.fori_loop(0, nkt, body, jnp.full((SUBLANES, TQ), NEG_INF, F32)),
                     axis=0, keepdims=True)
        c1 = count_ge(v1)
        on = active > 0.0
        hit = on & (c1 >= kf)
        lo_n = jnp.where(hit, v1, lo)
        cnt_n = jnp.where(hit, c1, cnt)
        hi_n = jnp.where(on & (~hit), v1, hi)
        return lo_n, hi_n, cnt_n, jnp.where(on & (~hit), 1.0, 0.0)

    unsettled = jnp.where(need & (cnt != kf), 1.0, 0.0)
    lo, hi, cnt, _ = lax.while_loop(lambda st: any_active(st[3]), snap_body, (lo, hi, cnt, unsettled))

    thr = jnp.where(need, lo, fmin)
    tied = need & (cnt > kf)
    big_pos = jnp.int32(2 ** 30)

    def tie_break():
        want = kf - count_where(lambda s, kt: s > thr)

        def body(_, st):
            jl, jh = st
            jm = lax.shift_right_arithmetic(jl + jh, 1)
            c = count_where(lambda s, kt: (s == thr) & (kt * TK + k_row <= jm))
            ok = c >= want
            return jnp.where(ok, jl, jm), jnp.where(ok, jm, jh)

        n_keys = sc_ref.shape[0] * TK
        steps = int(math.ceil(math.log2(n_keys + 1))) + 1
        jl0 = jnp.full((1, TQ), -1, jnp.int32)
        jh0 = jnp.full((1, TQ), n_keys, jnp.int32)
        _, jh = lax.fori_loop(0, steps, body, (jl0, jh0))
        return jnp.where(tied, jh, big_pos)

    pos_cap = lax.cond(jnp.max(jnp.where(tied, 1.0, 0.0)) > 0.0, tie_break,
                       lambda: jnp.full((1, TQ), big_pos, jnp.int32))

    def lane_wide(row):
        return jnp.broadcast_to(row, (LANES, TQ)).T

    thr_w = lane_wide(thr)
    cap_w = lane_wide(pos_cap)
    chunk_w = lane_wide(q_chunk)
    k_lane = lax.broadcasted_iota(jnp.int32, (1, LANES), 1)

    def to_mask(kt, edge_tile):
        out = []
        for j, s in enumerate(halves(mask_ref[kt])):
            k_pos = kt * TK + j * LANES + k_lane
            sel = (s > thr_w) | ((s == thr_w) & (k_pos <= cap_w))
            if edge_tile:
                sel = sel & (lax.shift_right_logical(k_pos, chunk_shift) <= chunk_w) & (k_pos >= FRONT_PAD)
            out.append(jnp.where(sel, 0.0, NEG_INF))
        mask_ref[kt] = jnp.concatenate(out, axis=1)

    def inner_mask(kt, carry):
        to_mask(kt, False)
        return carry

    lax.fori_loop(1, nkt - 1, inner_mask, 0)
    to_mask(0, True)

    @pl.when(nkt > 1)
    def _():
        to_mask(nkt - 1, True)

    for n in range(ATT_KV_HEADS):
        for g in range(ATT_GROUP):
            h = n * ATT_GROUP + g
            qg_ref[n, g * TQ:(g + 1) * TQ, :] = q_ref[0, :, h * ATT_HD:(h + 1) * ATT_HD]
    l_ref[...] = jnp.zeros_like(l_ref)
    acc_ref[...] = jnp.zeros_like(acc_ref)

    kv_heads = range(ATT_KV_HEADS)

    def raw_logits(kts):
        return [[jnp.dot(qg_ref[n], kt_ref[0, kt, n * ATT_HD:(n + 1) * ATT_HD, :], preferred_element_type=F32)
                 for n in kv_heads] for kt in kts]

    def masked_logits(s, mask, n, g, b0):
        sg = s[g * TQ:(g + 1) * TQ, :] + mask
        if b0 is not None:
            h = n * ATT_GROUP + g
            sg = sg + jnp.concatenate([bias_ref[h, b0 + j] for j in range(HALVES)], axis=1)
        return sg

    def max_pass(kts, b0):
        raw = raw_logits(kts)
        masks = [mask_ref[kt] for kt in kts]
        for n in kv_heads:
            for g in range(ATT_GROUP):
                rows = slice(g * TQ, (g + 1) * TQ)
                m = m_ref[n, rows, :]
                for t in range(len(kts)):
                    for part in halves(masked_logits(raw[t][n], masks[t], n, g, b0)):
                        m = jnp.maximum(m, part)
                m_ref[n, rows, :] = m

    def exp_pass(kts, b0):
        raw = raw_logits(kts)
        masks = [mask_ref[kt] for kt in kts]
        for n in kv_heads:
            ps = [[] for _ in kts]
            for g in range(ATT_GROUP):
                rows = slice(g * TQ, (g + 1) * TQ)
                m = m_ref[n, rows, :]
                l_new = l_ref[n, rows, :]
                for t in range(len(kts)):
                    parts = [jnp.exp(part - m) for part in halves(masked_logits(raw[t][n], masks[t], n, g, b0))]
                    l_new = l_new + functools.reduce(lambda a, b: a + b, parts)
                    ps[t].append(jnp.concatenate(parts, axis=1).astype(BF16))
                l_ref[n, rows, :] = l_new
            pv = None
            for t, kt in enumerate(kts):
                vv = v_ref[0, pl.ds(pl.multiple_of(kt * TK, TK), TK), n * ATT_HD:(n + 1) * ATT_HD]
                d = jnp.dot(jnp.concatenate(ps[t], axis=0), vv, preferred_element_type=F32)
                pv = d if pv is None else pv + d
            acc_ref[n] += pv

    def run_pass(tile_fn):
        paired_loop(jnp.maximum(i - 1, 0), lambda kts: tile_fn(kts, None))

        @pl.when(i >= 1)
        def _():
            tile_fn([i - 1], 0)

        tile_fn([i], HALVES)

    for n in range(ATT_KV_HEADS):
        for g in range(ATT_GROUP):
            h = n * ATT_GROUP + g
            qh = qg_ref[n, g * TQ:(g + 1) * TQ, :].astype(F32)
            q_len = jnp.sqrt(jnp.sum(qh * qh, axis=-1, keepdims=True))
            bound = q_len * knorm_ref[n] * SHIFT_SLACK + (bmax_ref[h] + SHIFT_SLACK - 1.0)
            m_ref[n, g * TQ:(g + 1) * TQ, :] = jnp.broadcast_to(bound, (TQ, LANES))
    run_pass(exp_pass)

    def smallest_row_sum():
        return functools.reduce(jnp.minimum, [jnp.min(jnp.sum(l_ref[n], axis=-1, keepdims=True))
                                              for n in range(ATT_KV_HEADS)])

    @pl.when(jnp.logical_not(smallest_row_sum() > ROW_SUM_FLOOR))
    def _():
        m_ref[...] = jnp.full(m_ref.shape, -1e30, F32)
        run_pass(max_pass)
        for n in range(ATT_KV_HEADS):
            m_ref[n] = jnp.broadcast_to(jnp.max(m_ref[n], axis=-1, keepdims=True), m_ref.shape[1:])
        l_ref[...] = jnp.zeros_like(l_ref)
        acc_ref[...] = jnp.zeros_like(acc_ref)
        run_pass(exp_pass)

    for n in range(ATT_KV_HEADS):
        inv_l = 1.0 / jnp.sum(l_ref[n], axis=-1, keepdims=True)
        for g in range(ATT_GROUP):
            h = n * ATT_GROUP + g
            rows = slice(g * TQ, (g + 1) * TQ)
            o_ref[0, :, h * ATT_HD:(h + 1) * ATT_HD] = (acc_ref[n, rows, :] * inv_l[rows, :]).astype(o_ref.dtype)


def _t5_bucket(rel):
    nb = REL_BUCKETS // 2
    max_exact = nb // 2
    ret = jnp.where(rel > 0, nb, 0)
    n = jnp.abs(rel)
    nf = jnp.maximum(n, 1).astype(jnp.float32)
    large = max_exact + (jnp.log(nf / max_exact) / math.log(REL_MAX_DIST / max_exact)
                         * (nb - max_exact)).astype(jnp.int32)
    large = jnp.minimum(large, nb - 1)
    return ret + jnp.where(n < max_exact, n, large)


def _dsa_core(q, kt, v, qit, ki, wit, rel_bias, n_select, live_rows):
    B, LP, _ = q.shape
    nq = LP // ATT_TILE
    nkt = LP // KEY_TILE
    a = jnp.arange(ATT_TILE)[:, None]
    xx = jnp.arange(2 * KEY_TILE)[None, :]
    rel = (xx - KEY_TILE) - a
    far = rel_bias.astype(F32)[REL_BUCKETS // 2 - 1]
    onehot = jax.nn.one_hot(_t5_bucket(rel), REL_BUCKETS, dtype=F32)
    table = jnp.einsum("abk,kh->abh", onehot, rel_bias.astype(F32),
                       precision=lax.Precision.HIGHEST) - far
    bias_near = jnp.transpose(table.reshape(ATT_TILE, 2 * KEY_TILE // LANES, LANES, ATT_HEADS), (3, 1, 0, 2))
    bias_max = jnp.broadcast_to(jnp.maximum(jnp.max(table, axis=(0, 1)), 0.0)[:, None, None],
                                (ATT_HEADS, 1, LANES))
    kern = functools.partial(_dsa_kernel, n_select=n_select, live_rows=live_rows)
    tile = lambda b, i: (b, i, 0)
    tile4 = lambda b, i: (b, i, 0, 0)
    whole3 = lambda b, i: (b, 0, 0)
    whole4 = lambda b, i: (b, 0, 0, 0)
    once = pl.Buffered(1)
    return pl.pallas_call(
        kern,
        grid=(B, nq),
        in_specs=[pl.BlockSpec((1, ATT_TILE, ATT_Q_DIM), tile),
                  pl.BlockSpec((1, 1, IDX_HEADS * LANES, ATT_TILE), tile4),
                  pl.BlockSpec((1, 1, LANES, ATT_TILE), tile4),
                  pl.BlockSpec((1, nkt, ATT_KV_DIM, KEY_TILE), whole4, pipeline_mode=once),
                  pl.BlockSpec((1, LP, ATT_KV_DIM), whole3, pipeline_mode=once),
                  pl.BlockSpec((1, LP, LANES), whole3, pipeline_mode=once),
                  pl.BlockSpec((ATT_HEADS, 2 * KEY_TILE // LANES, ATT_TILE, LANES), lambda b, i: (0, 0, 0, 0),
                               pipeline_mode=once),
                  pl.BlockSpec((ATT_HEADS, 1, LANES), lambda b, i: (0, 0, 0))],
        out_specs=pl.BlockSpec((1, ATT_TILE, ATT_Q_DIM), tile),
        out_shape=jax.ShapeDtypeStruct((B, LP, ATT_Q_DIM), BF16),
        scratch_shapes=[pltpu.VMEM((nkt, KEY_TILE, ATT_TILE), F32),
                        pltpu.VMEM((nkt, ATT_TILE, KEY_TILE), F32),
                        pltpu.VMEM((ATT_KV_HEADS, ATT_GROUP * ATT_TILE, ATT_HD), BF16),
                        pltpu.VMEM((ATT_KV_HEADS, ATT_GROUP * ATT_TILE, LANES), F32),
                        pltpu.VMEM((ATT_KV_HEADS, ATT_GROUP * ATT_TILE, LANES), F32),
                        pltpu.VMEM((ATT_KV_HEADS, ATT_GROUP * ATT_TILE, ATT_HD), F32),
                        pltpu.VMEM((ATT_KV_HEADS, 1, LANES), F32)],
        compiler_params=_compiler_params(("arbitrary", "arbitrary")),
        name="dsa_core",
    )(q, qit, wit, kt, v, ki, bias_near, bias_max)


def _gdn_in_weight(w_in):
    D = w_in.shape[0]
    w = jnp.zeros((D, GDN_PROJ_W), F32).at[:, 0:w_in.shape[1]].set(w_in.astype(F32))
    return w.astype(BF16)


def _dsa_in_weight(w_in):
    D = w_in.shape[0]
    w_in = w_in.astype(BF16)
    zeros = lambda n: jnp.zeros((D, n), BF16)
    src = ATT_Q_DIM + 2 * ATT_KV_DIM
    parts = [w_in[:, 0:src]]
    for h in range(IDX_HEADS):
        parts += [w_in[:, src + h * IDX_HD:src + (h + 1) * IDX_HD], zeros(LANES - IDX_HD)]
    src += IDX_HEADS * IDX_HD
    parts += [w_in[:, src:src + IDX_HD], zeros(LANES - IDX_HD),
              w_in[:, src + IDX_HD:src + IDX_HD + IDX_HEADS], zeros(LANES - IDX_HEADS)]
    w = jnp.concatenate(parts, axis=1)
    assert w.shape[1] == DSA_PROJ_W
    return w


def kernel(x, meta_tokens, norm_mix, norm_ffn, rel_bias, gdn_w_in, gdn_conv, gdn_a_log, gdn_dt_bias,
           gdn_o_norm, gdn_w_out, dsa_w_in, dsa_q_norm, dsa_k_norm, dsa_idx_ln_g, dsa_idx_ln_b, dsa_w_out,
           moe_w_group, moe_b_group, moe_w_expert, moe_b_expert, moe_w_gate_up, moe_w_down):
    B, S, D = x.shape
    depth = norm_mix.shape[0]
    n_select = min(TOPK_MAX, S // 4)
    off = FRONT_PAD + N_META
    LP = -(-(off + S) // ROW_ALIGN) * ROW_ALIGN
    meta = jnp.broadcast_to(meta_tokens.astype(x.dtype)[None], (B, N_META, D))
    h = jnp.concatenate([jnp.zeros((B, FRONT_PAD, D), x.dtype), meta, x,
                         jnp.zeros((B, LP - off - S, D), x.dtype)], axis=1)
    h = h.reshape(B * LP, D)
    for i in range(depth):
        j = i // 2
        if i % 2 == 0:
            o = _gdn_core(h.reshape(B, LP, D), norm_mix[i], _gdn_in_weight(gdn_w_in[j]),
                          gdn_conv[j].astype(F32), gdn_a_log[j], gdn_dt_bias[j], gdn_o_norm[j])
            mix, w_out = o.reshape(B * LP, GDN_V_DIM), gdn_w_out[j].astype(BF16)
        else:
            q, kt, v, qit, ki, wit = _dsa_in(h, norm_mix[i], _dsa_in_weight(dsa_w_in[j]), dsa_q_norm[j],
                                             dsa_k_norm[j], dsa_idx_ln_g[j], dsa_idx_ln_b[j])
            r3 = lambda t: t.reshape(B, LP, t.shape[-1])
            r4 = lambda t: t.reshape(B, LP // KEY_TILE, t.shape[-2], KEY_TILE)
            o = _dsa_core(r3(q), r4(kt), r3(v), r4(qit), r3(ki), r4(wit), rel_bias, n_select, off + S)
            mix, w_out = o.reshape(B * LP, ATT_Q_DIM), dsa_w_out[j].astype(BF16)
        h = _moe(h, mix, w_out, norm_ffn[i], moe_w_group[i], moe_b_group[i], moe_w_expert[i], moe_b_expert[i],
                 moe_w_gate_up[i], moe_w_down[i])
    return h.reshape(B, LP, D)[:, off:off + S]
```

```python
import functools
import math

import jax
import jax.numpy as jnp
from jax import lax
from jax.experimental import pallas as pl
from jax.experimental.pallas import tpu as pltpu

F32 = jnp.float32
BF16 = jnp.bfloat16

LANES = 128
SUBLANES = 8
VMEM_LIMIT_BYTES = 56 * 1024 * 1024

CHUNK = 64
N_META = 16
META_PAD = (-N_META) % CHUNK
RMS_EPS = 1e-6
L2_EPS = 1e-6

GDN_HEADS = 8
GDN_DK = 128
GDN_DV = 128
GDN_CONV = 4
GDN_QK_DIM = GDN_HEADS * GDN_DK
GDN_V_DIM = GDN_HEADS * GDN_DV
GDN_CONV_DIM = 2 * GDN_QK_DIM + GDN_V_DIM
GDN_GATE_OFF = GDN_CONV_DIM + GDN_V_DIM
GDN_PROJ_W = GDN_GATE_OFF + LANES

ATT_HEADS = 8
ATT_KV_HEADS = 2
ATT_GROUP = ATT_HEADS // ATT_KV_HEADS
ATT_HD = 128
ATT_Q_DIM = ATT_HEADS * ATT_HD
ATT_KV_DIM = ATT_KV_HEADS * ATT_HD
IDX_HEADS = 8
IDX_HD = 64
TOPK_MAX = 256
REL_BUCKETS = 32
REL_MAX_DIST = 128

DSA_Q_OFF = 0
DSA_K_OFF = ATT_Q_DIM
DSA_V_OFF = DSA_K_OFF + ATT_KV_DIM
DSA_QI_OFF = DSA_V_OFF + ATT_KV_DIM
DSA_KI_OFF = DSA_QI_OFF + IDX_HEADS * LANES
DSA_WI_OFF = DSA_KI_OFF + LANES
DSA_PROJ_W = DSA_WI_OFF + LANES

N_GROUPS = 4
EXPERTS_PER_GROUP = 8
N_EXPERTS = N_GROUPS * EXPERTS_PER_GROUP
EXPERT_FF = 256

GDN_ROW_TILES = (256, 128, 64)
MOE_ROUTER_TILES = (512, 256, 128)
MOE_ROW_TILES = (512, 256, 128)

ATT_TILE = 256
KEY_TILE = 256
ROW_ALIGN = KEY_TILE
FRONT_PAD = ROW_ALIGN - N_META
assert FRONT_PAD % CHUNK == META_PAD
BISECT_UNROLL = 4
BISECT_STEPS = 20
SHIFT_SLACK = 1.001
ROW_SUM_FLOOR = 1e-30
NEG_INF = float("-inf")


def _pick_tile(n, candidates):
    for c in candidates:
        if n % c == 0:
            return c
    raise ValueError(f"no tile for {n}")


def _compiler_params(semantics):
    return pltpu.CompilerParams(dimension_semantics=semantics, vmem_limit_bytes=VMEM_LIMIT_BYTES)


def _silu(x):
    return x * jax.nn.sigmoid(x)


def _norm_matmul_kernel(x_ref, g_ref, w_ref, o_ref):
    x = x_ref[...]
    y = x * lax.rsqrt(jnp.mean(x * x, axis=-1, keepdims=True) + RMS_EPS) * g_ref[...]
    o_ref[...] = jnp.dot(y.astype(BF16), w_ref[...], preferred_element_type=F32)


def _matmul_residual_kernel(a_ref, w_ref, r_ref, o_ref):
    o_ref[...] = r_ref[...] + jnp.dot(a_ref[...], w_ref[...], preferred_element_type=F32)


def _matmul_residual(a_bf16, w_bf16, res):
    T, K = a_bf16.shape
    N = w_bf16.shape[1]
    tm = _pick_tile(T, (512, 256, 128))
    return pl.pallas_call(
        _matmul_residual_kernel,
        grid=(T // tm,),
        in_specs=[pl.BlockSpec((tm, K), lambda i: (i, 0)),
                  pl.BlockSpec((K, N), lambda i: (0, 0)),
                  pl.BlockSpec((tm, N), lambda i: (i, 0))],
        out_specs=pl.BlockSpec((tm, N), lambda i: (i, 0)),
        out_shape=jax.ShapeDtypeStruct((T, N), F32),
        compiler_params=_compiler_params(("parallel",)),
        name="matmul_residual",
    )(a_bf16, w_bf16, res)


def _nt_dot(a, b):
    return lax.dot_general(a, b, (((1,), (1,)), ((), ())), preferred_element_type=F32)


def _tn_dot(a, b):
    return lax.dot_general(a, b, (((0,), (0,)), ((), ())), preferred_element_type=F32)


def _gdn_kernel(x_ref, g_ref, w_ref, conv_ref, gate_ref, ogain_ref, o_ref,
                state_ref, xbuf_ref, qkv_ref, proj_ref):
    t = pl.program_id(0)
    nb, rows = x_ref.shape[0], x_ref.shape[1]

    @pl.when(t == 0)
    def _():
        state_ref[...] = jnp.zeros_like(state_ref)
        xbuf_ref[:, 0:SUBLANES, :] = jnp.zeros((nb, SUBLANES, GDN_CONV_DIM), F32)

    for b in range(nb):
        _norm_matmul_kernel(x_ref.at[b], g_ref, w_ref, proj_ref.at[b])

    def chunk(cc, carry):
        r0 = pl.multiple_of(cc * CHUNK, CHUNK)
        _gdn_chunk(t * rows + r0, r0, nb, proj_ref, conv_ref, gate_ref, ogain_ref, o_ref,
                   state_ref, xbuf_ref, qkv_ref)
        return carry

    lax.fori_loop(0, rows // CHUNK, chunk, 0)


def _gdn_chunk(row0, r0, nb, proj_ref, conv_ref, gate_ref, ogain_ref, o_ref, state_ref, xbuf_ref, qkv_ref):
    C = CHUNK
    row = row0 + lax.broadcasted_iota(jnp.int32, (C, 1), 0)
    live = row >= FRONT_PAD
    r_i = lax.broadcasted_iota(jnp.int32, (C, C), 0)
    c_i = lax.broadcasted_iota(jnp.int32, (C, C), 1)
    causal = c_i <= r_i
    strict = c_i < r_i
    neg_rate = -jnp.exp(gate_ref[0:1, :])
    betas, gcs, gc_ts, egs, eg_lasts, eks = [], [], [], [], [], []
    for b in range(nb):
        u = jnp.where(live, proj_ref[b, pl.ds(r0, C), 0:GDN_CONV_DIM], 0.0)
        xbuf_ref[b, SUBLANES:SUBLANES + C, :] = u
        conv = conv_ref[0:1, :] * xbuf_ref[b, SUBLANES - 3:SUBLANES - 3 + C, :]
        for j in range(1, GDN_CONV):
            conv = conv + conv_ref[j:j + 1, :] * xbuf_ref[b, SUBLANES - 3 + j:SUBLANES - 3 + j + C, :]
        xbuf_ref[b, 0:SUBLANES, :] = u[C - SUBLANES:C, :]
        qkv_ref[b] = _silu(conv)

        gates = jnp.where(live, proj_ref[b, pl.ds(r0, C), GDN_GATE_OFF:GDN_GATE_OFF + LANES], 0.0)
        g = neg_rate * jax.nn.softplus(gates + gate_ref[1:2, :])
        gc = jnp.dot(causal.astype(F32), g, preferred_element_type=F32,
                     precision=lax.Precision.HIGHEST)
        g_last = gc[C - 1:C, :]
        betas.append(jax.nn.sigmoid(gates))
        gcs.append(gc)
        gc_ts.append(gc.T)
        egs.append(jnp.exp(gc))
        eg_lasts.append(jnp.exp(g_last))
        eks.append(jnp.exp(g_last - gc))

    chains = [(b, h) for b in range(nb) for h in range(GDN_HEADS)]
    heads = range(len(chains))
    dot = functools.partial(jnp.dot, preferred_element_type=F32)
    col = lambda a, h: a[:, GDN_HEADS + h:GDN_HEADS + h + 1]
    qs, ks, k16s, xs, decays = [], [], [], [], []
    for b, h in chains:
        lo = h * GDN_DK
        q = qkv_ref[b, :, lo:lo + GDN_DK]
        k = qkv_ref[b, :, GDN_QK_DIM + lo:GDN_QK_DIM + lo + GDN_DK]
        v = qkv_ref[b, :, 2 * GDN_QK_DIM + h * GDN_DV:2 * GDN_QK_DIM + (h + 1) * GDN_DV]
        q = q * lax.rsqrt(jnp.sum(q * q, axis=-1, keepdims=True) + L2_EPS) * (GDN_DK ** -0.5)
        k = k * lax.rsqrt(jnp.sum(k * k, axis=-1, keepdims=True) + L2_EPS)
        kb = k * betas[b][:, h:h + 1]
        vb = v * betas[b][:, h:h + 1]
        qs.append(q)
        ks.append(k)
        k16s.append(k.astype(BF16))
        xs.append((kb, jnp.concatenate([vb, kb * col(egs[b], h)], axis=1)))
        decays.append(jnp.exp(jnp.where(causal, col(gcs[b], h) - gc_ts[b][GDN_HEADS + h:GDN_HEADS + h + 1, :],
                                        NEG_INF)))
    kk = [_nt_dot(xs[h][0].astype(BF16), k16s[h]) for h in heads]
    qk = [_nt_dot(qs[h].astype(BF16), k16s[h]) for h in heads]
    ns = [-jnp.where(strict, kk[h] * decays[h], 0.0) for h in heads]
    xs = [x for _, x in xs]
    for it in range(6):
        n16 = [n.astype(BF16) for n in ns]
        xs = [xs[h] + dot(n16[h], xs[h].astype(BF16)) for h in heads]
        if it < 5:
            ns = [dot(n16[h], n16[h]) for h in heads]
    qk = [jnp.where(causal, qk[h] * decays[h], 0.0).astype(BF16) for h in heads]
    states = [state_ref[c] for c in heads]
    s16 = [s.astype(BF16) for s in states]
    ws = [dot(xs[c][:, GDN_DV:2 * GDN_DV].astype(BF16), s16[c]) for c in heads]
    qs_s = [dot((qs[c] * col(egs[b], h)).astype(BF16), s16[c]) for c, (b, h) in enumerate(chains)]
    v_new = [(xs[c][:, 0:GDN_DV] - ws[c]).astype(BF16) for c in heads]
    os_ = [qs_s[c] + dot(qk[c], v_new[c]) for c in heads]
    kv = [_tn_dot((ks[c] * col(eks[b], h)).astype(BF16), v_new[c]) for c, (b, h) in enumerate(chains)]
    for c, (b, h) in enumerate(chains):
        state_ref[c] = states[c] * col(eg_lasts[b], h) + kv[c]
        z = proj_ref[b, pl.ds(r0, C), GDN_CONV_DIM + h * GDN_DV:GDN_CONV_DIM + (h + 1) * GDN_DV]
        o = os_[c]
        on = o * lax.rsqrt(jnp.mean(o * o, axis=-1, keepdims=True) + RMS_EPS) * ogain_ref[...]
        o_ref[b, pl.ds(r0, C), h * GDN_DV:(h + 1) * GDN_DV] = (on * _silu(z)).astype(o_ref.dtype)


def _gdn_core(h3d, gain, w_bf16, conv_w, a_log, dt_bias, o_gain):
    B, LP, D = h3d.shape
    W = w_bf16.shape[1]
    rows = _pick_tile(LP, GDN_ROW_TILES)
    gate = jnp.zeros((2, LANES), F32)
    gate = gate.at[0, GDN_HEADS:2 * GDN_HEADS].set(a_log.astype(F32))
    gate = gate.at[1, GDN_HEADS:2 * GDN_HEADS].set(dt_bias.astype(F32))
    fix = lambda t: (0, 0)
    return pl.pallas_call(
        _gdn_kernel,
        grid=(LP // rows,),
        in_specs=[pl.BlockSpec((B, rows, D), lambda t: (0, t, 0)),
                  pl.BlockSpec((1, D), fix),
                  pl.BlockSpec((D, W), fix, pipeline_mode=pl.Buffered(1)),
                  pl.BlockSpec((GDN_CONV, GDN_CONV_DIM), fix),
                  pl.BlockSpec((2, LANES), fix),
                  pl.BlockSpec((1, GDN_DV), fix)],
        out_specs=pl.BlockSpec((B, rows, GDN_V_DIM), lambda t: (0, t, 0)),
        out_shape=jax.ShapeDtypeStruct((B, LP, GDN_V_DIM), BF16),
        scratch_shapes=[pltpu.VMEM((B * GDN_HEADS, GDN_DK, GDN_DV), F32),
                        pltpu.VMEM((B, SUBLANES + CHUNK, GDN_CONV_DIM), F32),
                        pltpu.VMEM((B, CHUNK, GDN_CONV_DIM), F32),
                        pltpu.VMEM((B, rows, W), F32)],
        compiler_params=_compiler_params(("arbitrary",)),
        name="gdn_core",
    )(h3d, gain.reshape(1, D), w_bf16, conv_w, gate, o_gain.reshape(1, GDN_DV))


def _moe_router_kernel(h_ref, gain_ref, wr_ref, br_ref, hx_ref, gid_ref, rank_ref, cnt_ref, carry_ref):
    i = pl.program_id(0)
    TM = h_ref.shape[0]
    lane = lax.broadcasted_iota(jnp.int32, (1, LANES), 1)

    @pl.when(i == 0)
    def _():
        carry_ref[...] = jnp.zeros_like(carry_ref)

    x = h_ref[...]
    xn = x * lax.rsqrt(jnp.mean(x * x, axis=-1, keepdims=True) + RMS_EPS) * gain_ref[...]
    logits = jnp.dot(xn, wr_ref[...], preferred_element_type=F32,
                     precision=lax.Precision.HIGHEST) + br_ref[...]
    e_log = logits
    g_lane = lane - N_EXPERTS
    g_log = jnp.where((g_lane >= 0) & (g_lane < N_GROUPS), logits, NEG_INF)
    g_max = jnp.max(g_log, axis=-1, keepdims=True)
    g_sel = jnp.min(jnp.where(g_log == g_max, g_lane, LANES), axis=-1, keepdims=True)
    g_sel = jnp.minimum(g_sel, N_GROUPS - 1)
    g_w = 1.0 / jnp.sum(jnp.exp(g_log - g_max), axis=-1, keepdims=True)
    in_grp = (lane >= g_sel * EXPERTS_PER_GROUP) & (lane < (g_sel + 1) * EXPERTS_PER_GROUP)
    l0 = jnp.where(in_grp, e_log, NEG_INF)
    m1 = jnp.max(l0, axis=-1, keepdims=True)
    i1 = jnp.min(jnp.where(l0 == m1, lane, LANES), axis=-1, keepdims=True)
    l1 = jnp.where(lane == i1, NEG_INF, l0)
    m2 = jnp.max(l1, axis=-1, keepdims=True)
    i2 = jnp.min(jnp.where(l1 == m2, lane, LANES), axis=-1, keepdims=True)
    p2 = jnp.exp(m2 - m1)
    w1 = g_w / (1.0 + p2)
    w2 = g_w * p2 / (1.0 + p2)
    j1 = i1 - g_sel * EXPERTS_PER_GROUP
    j2 = i2 - g_sel * EXPERTS_PER_GROUP
    hx_ref[:, 0:xn.shape[1]] = xn
    hx_ref[:, xn.shape[1]:] = jnp.where(lane == j1, w1, 0.0) + jnp.where(lane == j2, w2, 0.0)

    onehot = jnp.where(lane == g_sel, 1.0, 0.0)
    r_i = lax.broadcasted_iota(jnp.int32, (TM, TM), 0)
    c_i = lax.broadcasted_iota(jnp.int32, (TM, TM), 1)
    before = jnp.where(c_i < r_i, 1.0, 0.0).astype(BF16)
    prior = jnp.dot(before, onehot.astype(BF16), preferred_element_type=F32) + carry_ref[...]
    rank = jnp.sum(jnp.where(lane == g_sel, prior, 0.0), axis=-1, keepdims=True)
    carry_ref[...] += jnp.sum(onehot, axis=0, keepdims=True)
    cnt_ref[...] = carry_ref[...]
    gid_ref[0] = jnp.broadcast_to(g_sel.astype(F32), (TM, LANES)).T[0:1, :].astype(jnp.int32)
    rank_ref[0] = jnp.broadcast_to(rank, (TM, LANES)).T[0:1, :].astype(jnp.int32)


def _moe_dispatch_kernel(seg_ref, pos_ref, hx_ref, xs_ref, zero_ref, sem, zero_sem):
    TM = hx_ref.shape[0]

    @pl.when(pl.program_id(0) == 0)
    def _():
        zero_ref[...] = jnp.zeros_like(zero_ref)

        def zero_tile(t):
            fill = pltpu.make_async_copy(zero_ref, xs_ref.at[pl.ds(t * TM, TM), :], zero_sem)
            fill.start()
            fill.wait()

        for g in range(N_GROUPS):
            first = seg_ref[g - 1] if g else 0
            pl.when(seg_ref[g] > first)(functools.partial(zero_tile, seg_ref[g] - 1))
            tail = seg_ref[N_GROUPS - 1] + g
            pl.when(tail < xs_ref.shape[0] // TM)(functools.partial(zero_tile, tail))

    def start(a, carry):
        for b in range(SUBLANES):
            r = a * SUBLANES + b
            pltpu.make_async_copy(hx_ref.at[pl.ds(r, 1), :], xs_ref.at[pl.ds(pos_ref[0, 0, r], 1), :],
                                  sem).start()
        return carry

    lax.fori_loop(0, TM // SUBLANES, start, 0)
    pltpu.make_async_copy(hx_ref, xs_ref.at[pl.ds(0, TM), :], sem).wait()


def _moe_expert_kernel(grp_ref, xs_ref, wgu_ref, wd_ref, ys_ref, xn_ref, acc_ref):
    j = pl.program_id(0)
    e = pl.program_id(1)
    D = xn_ref.shape[1]
    lane = lax.broadcasted_iota(jnp.int32, (1, LANES), 1)

    @pl.when(grp_ref[j] >= 0)
    def _():
        @pl.when(e == 0)
        def _():
            xn_ref[...] = xs_ref[:, 0:D].astype(BF16)
            acc_ref[...] = jnp.zeros_like(acc_ref)

        hh = jnp.dot(xn_ref[...], wgu_ref[0], preferred_element_type=F32)
        cw = jnp.sum(jnp.where(lane == e, xs_ref[:, D:], 0.0), axis=-1, keepdims=True)
        act = _silu(hh[:, 0:EXPERT_FF]) * hh[:, EXPERT_FF:2 * EXPERT_FF] * cw
        acc_ref[...] += jnp.dot(act.astype(BF16), wd_ref[0], preferred_element_type=F32)

    last = e == EXPERTS_PER_GROUP - 1

    @pl.when(last & (grp_ref[j] >= 0))
    def _():
        ys_ref[...] = acc_ref[...]

    @pl.when(last & (grp_ref[j] < 0))
    def _():
        ys_ref[...] = jnp.zeros_like(ys_ref)


def _moe_combine_kernel(pos_ref, h_ref, ys_ref, o_ref, stage_ref, sem):
    TM = h_ref.shape[0]

    def start(a, carry):
        for b in range(SUBLANES):
            r = a * SUBLANES + b
            pltpu.make_async_copy(ys_ref.at[pl.ds(pos_ref[0, 0, r], 1), :], stage_ref.at[pl.ds(r, 1), :],
                                  sem).start()
        return carry

    lax.fori_loop(0, TM // SUBLANES, start, 0)
    pltpu.make_async_copy(ys_ref.at[pl.ds(0, TM), :], stage_ref, sem).wait()
    o_ref[...] = h_ref[...] + stage_ref[...]


def _moe(h2d, gain, w_group, b_group, w_expert, b_expert, w_gate_up, w_down):
    T, D = h2d.shape
    tr = _pick_tile(T, MOE_ROUTER_TILES)
    tm = _pick_tile(T, MOE_ROW_TILES)
    n_tiles = T // tm + N_GROUPS
    XW = D + LANES
    assert N_EXPERTS + N_GROUPS <= LANES
    wr = jnp.zeros((D, LANES), F32)
    wr = wr.at[:, 0:N_EXPERTS].set(w_expert.astype(F32))
    wr = wr.at[:, N_EXPERTS:N_EXPERTS + N_GROUPS].set(w_group.astype(F32))
    br = jnp.zeros((1, LANES), F32)
    br = br.at[0, 0:N_EXPERTS].set(b_expert.astype(F32))
    br = br.at[0, N_EXPERTS:N_EXPERTS + N_GROUPS].set(b_group.astype(F32))
    wgu = w_gate_up.reshape(N_EXPERTS, D, 2 * EXPERT_FF).astype(BF16)
    wd = w_down.reshape(N_EXPERTS, EXPERT_FF, D).astype(BF16)

    hx, gid, rank, counts = pl.pallas_call(
        _moe_router_kernel,
        grid=(T // tr,),
        in_specs=[pl.BlockSpec((tr, D), lambda i: (i, 0)),
                  pl.BlockSpec((1, D), lambda i: (0, 0)),
                  pl.BlockSpec((D, LANES), lambda i: (0, 0)),
                  pl.BlockSpec((1, LANES), lambda i: (0, 0))],
        out_specs=[pl.BlockSpec((tr, XW), lambda i: (i, 0)),
                   pl.BlockSpec((1, 1, tr), lambda i: (i, 0, 0)),
                   pl.BlockSpec((1, 1, tr), lambda i: (i, 0, 0)),
                   pl.BlockSpec((1, LANES), lambda i: (0, 0))],
        out_shape=[jax.ShapeDtypeStruct((T, XW), F32),
                   jax.ShapeDtypeStruct((T // tr, 1, tr), jnp.int32),
                   jax.ShapeDtypeStruct((T // tr, 1, tr), jnp.int32),
                   jax.ShapeDtypeStruct((1, LANES), F32)],
        scratch_shapes=[pltpu.VMEM((1, LANES), F32)],
        compiler_params=_compiler_params(("arbitrary",)),
        name="moe_router",
    )(h2d, gain.reshape(1, D), wr, br)

    seg_tiles = -(-counts[0, 0:N_GROUPS].astype(jnp.int32) // tm)
    seg_end = jnp.cumsum(seg_tiles)
    seg_start = seg_end - seg_tiles
    pos = (seg_start * tm)[gid.reshape(T)] + rank.reshape(T)
    tile_ids = jnp.arange(n_tiles, dtype=jnp.int32)
    tile_group = jnp.sum((tile_ids[:, None] >= seg_end[None, :]).astype(jnp.int32), axis=1)
    tile_group = jnp.where(tile_ids < seg_end[N_GROUPS - 1], tile_group, -1)
    pos3 = pos.reshape(T // tm, 1, tm)

    xs = pl.pallas_call(
        _moe_dispatch_kernel,
        grid_spec=pltpu.PrefetchScalarGridSpec(
            num_scalar_prefetch=1,
            grid=(T // tm,),
            in_specs=[pl.BlockSpec((1, 1, tm), lambda i, seg: (i, 0, 0), memory_space=pltpu.SMEM),
                      pl.BlockSpec((tm, XW), lambda i, seg: (i, 0))],
            out_specs=pl.BlockSpec(memory_space=pl.ANY),
            scratch_shapes=[pltpu.VMEM((tm, XW), F32), pltpu.SemaphoreType.DMA(()),
                            pltpu.SemaphoreType.DMA(())]),
        out_shape=jax.ShapeDtypeStruct((n_tiles * tm, XW), F32),
        compiler_params=_compiler_params(("arbitrary",)),
        name="moe_dispatch",
    )(seg_end, pos3, hx)

    def expert_index(j, e, grp):
        return (jnp.maximum(grp[j], 0) * EXPERTS_PER_GROUP + e, 0, 0)

    ys = pl.pallas_call(
        _moe_expert_kernel,
        grid_spec=pltpu.PrefetchScalarGridSpec(
            num_scalar_prefetch=1,
            grid=(n_tiles, EXPERTS_PER_GROUP),
            in_specs=[pl.BlockSpec((tm, XW), lambda j, e, grp: (j, 0)),
                      pl.BlockSpec((1, D, 2 * EXPERT_FF), expert_index),
                      pl.BlockSpec((1, EXPERT_FF, D), expert_index)],
            out_specs=pl.BlockSpec((tm, D), lambda j, e, grp: (j, 0)),
            scratch_shapes=[pltpu.VMEM((tm, D), BF16), pltpu.VMEM((tm, D), F32)]),
        out_shape=jax.ShapeDtypeStruct((n_tiles * tm, D), F32),
        compiler_params=_compiler_params(("parallel", "arbitrary")),
        name="moe_experts",
    )(tile_group, xs, wgu, wd)

    return pl.pallas_call(
        _moe_combine_kernel,
        grid=(T // tm,),
        in_specs=[pl.BlockSpec((1, 1, tm), lambda i: (i, 0, 0), memory_space=pltpu.SMEM),
                  pl.BlockSpec((tm, D), lambda i: (i, 0)),
                  pl.BlockSpec(memory_space=pl.ANY)],
        out_specs=pl.BlockSpec((tm, D), lambda i: (i, 0)),
        out_shape=jax.ShapeDtypeStruct((T, D), F32),
        scratch_shapes=[pltpu.VMEM((tm, D), F32), pltpu.SemaphoreType.DMA(())],
        compiler_params=_compiler_params(("arbitrary",)),
        name="moe_combine",
    )(pos3, h2d, ys)


def _dsa_prep_kernel(p_ref, qg_ref, kg_ref, lng_ref, lnb_ref, q_ref, kt_ref, v_ref, qit_ref, ki_ref, wit_ref):
    lane = lax.broadcasted_iota(jnp.int32, (1, LANES), 1)
    for h in range(ATT_HEADS):
        x = p_ref[:, DSA_Q_OFF + h * ATT_HD:DSA_Q_OFF + (h + 1) * ATT_HD]
        y = x * lax.rsqrt(jnp.mean(x * x, axis=-1, keepdims=True) + RMS_EPS) * qg_ref[...]
        q_ref[:, h * ATT_HD:(h + 1) * ATT_HD] = (y * (ATT_HD ** -0.5)).astype(BF16)
    for n in range(ATT_KV_HEADS):
        x = p_ref[:, DSA_K_OFF + n * ATT_HD:DSA_K_OFF + (n + 1) * ATT_HD]
        y = x * lax.rsqrt(jnp.mean(x * x, axis=-1, keepdims=True) + RMS_EPS) * kg_ref[...]
        kt_ref[0, n * ATT_HD:(n + 1) * ATT_HD, :] = y.T.astype(BF16)
    v_ref[...] = p_ref[:, DSA_V_OFF:DSA_V_OFF + ATT_KV_DIM].astype(BF16)
    qit_ref[0] = p_ref[:, DSA_QI_OFF:DSA_QI_OFF + IDX_HEADS * LANES].T.astype(BF16)
    x = p_ref[:, DSA_KI_OFF:DSA_KI_OFF + LANES]
    live = lane < IDX_HD
    mu = jnp.sum(x, axis=-1, keepdims=True) * (1.0 / IDX_HD)
    xc = jnp.where(live, x - mu, 0.0)
    var = jnp.sum(xc * xc, axis=-1, keepdims=True) * (1.0 / IDX_HD)
    ki = xc * lax.rsqrt(var + RMS_EPS) * lng_ref[...] + lnb_ref[...]
    ki_ref[...] = jnp.where(live, ki, 0.0).astype(BF16)
    wit_ref[0] = (p_ref[:, DSA_WI_OFF:DSA_WI_OFF + LANES] * (IDX_HEADS ** -0.5 * IDX_HD ** -0.5)).T


def _dsa_in_kernel(x_ref, g_ref, w_ref, qg_ref, kg_ref, lng_ref, lnb_ref,
                   q_ref, kt_ref, v_ref, qit_ref, ki_ref, wit_ref, proj_ref):
    _norm_matmul_kernel(x_ref, g_ref, w_ref, proj_ref)
    _dsa_prep_kernel(proj_ref, qg_ref, kg_ref, lng_ref, lnb_ref, q_ref, kt_ref, v_ref, qit_ref, ki_ref, wit_ref)


def _dsa_in(h2d, gain, w_bf16, q_gain, k_gain, ln_g, ln_b):
    T, D = h2d.shape
    W = w_bf16.shape[1]
    tm = KEY_TILE
    assert T % tm == 0 and ATT_TILE == KEY_TILE
    lng = jnp.zeros((1, LANES), F32).at[0, 0:IDX_HD].set(ln_g.astype(F32))
    lnb = jnp.zeros((1, LANES), F32).at[0, 0:IDX_HD].set(ln_b.astype(F32))
    row = lambda i: (i, 0)
    fix = lambda i: (0, 0)
    blk = lambda i: (i, 0, 0)
    return pl.pallas_call(
        _dsa_in_kernel,
        grid=(T // tm,),
        in_specs=[pl.BlockSpec((tm, D), row), pl.BlockSpec((1, D), fix), pl.BlockSpec((D, W), fix),
                  pl.BlockSpec((1, ATT_HD), fix), pl.BlockSpec((1, ATT_HD), fix),
                  pl.BlockSpec((1, LANES), fix), pl.BlockSpec((1, LANES), fix)],
        out_specs=[pl.BlockSpec((tm, ATT_Q_DIM), row), pl.BlockSpec((1, ATT_KV_DIM, tm), blk),
                   pl.BlockSpec((tm, ATT_KV_DIM), row), pl.BlockSpec((1, IDX_HEADS * LANES, tm), blk),
                   pl.BlockSpec((tm, LANES), row), pl.BlockSpec((1, LANES, tm), blk)],
        out_shape=[jax.ShapeDtypeStruct((T, ATT_Q_DIM), BF16), jax.ShapeDtypeStruct((T // tm, ATT_KV_DIM, tm), BF16),
                   jax.ShapeDtypeStruct((T, ATT_KV_DIM), BF16),
                   jax.ShapeDtypeStruct((T // tm, IDX_HEADS * LANES, tm), BF16),
                   jax.ShapeDtypeStruct((T, LANES), BF16), jax.ShapeDtypeStruct((T // tm, LANES, tm), F32)],
        scratch_shapes=[pltpu.VMEM((tm, W), F32)],
        compiler_params=_compiler_params(("parallel",)),
        name="dsa_in",
    )(h2d, gain.reshape(1, D), w_bf16, q_gain.reshape(1, ATT_HD), k_gain.reshape(1, ATT_HD), lng, lnb)


def _dsa_kernel(q_ref, qit_ref, wit_ref, kt_ref, v_ref, ki_ref, bias_ref, bmax_ref, o_ref,
                sc_ref, mask_ref, qg_ref, m_ref, l_ref, acc_ref, knorm_ref, *, n_select, live_rows):
    i = pl.program_id(1)
    TQ = ATT_TILE
    TK = KEY_TILE
    HALVES = TK // LANES

    @pl.when(i == 0)
    def _():
        def body(kt, mx):
            kk = kt_ref[0, kt].astype(F32)
            return tuple(jnp.maximum(mx[n], jnp.sum(jnp.square(kk[n * ATT_HD:(n + 1) * ATT_HD, :]),
                                                    axis=0, keepdims=True)) for n in range(ATT_KV_HEADS))

        mx = lax.fori_loop(0, kt_ref.shape[1], body,
                           tuple(jnp.zeros((1, TK), F32) for _ in range(ATT_KV_HEADS)))
        for n in range(ATT_KV_HEADS):
            knorm_ref[n] = jnp.broadcast_to(jnp.sqrt(jnp.max(mx[n], axis=-1, keepdims=True)), (1, LANES))

    @pl.when(i * TQ >= live_rows)
    def _():
        o_ref[...] = jnp.zeros_like(o_ref)

    @pl.when(i * TQ < live_rows)
    def _():
        _dsa_tile(i, q_ref, qit_ref, wit_ref, kt_ref, v_ref, ki_ref, bias_ref, bmax_ref, o_ref,
                  sc_ref, mask_ref, qg_ref, m_ref, l_ref, acc_ref, knorm_ref, n_select, TQ, TK, HALVES)


def _dsa_tile(i, q_ref, qit_ref, wit_ref, kt_ref, v_ref, ki_ref, bias_ref, bmax_ref, o_ref,
              sc_ref, mask_ref, qg_ref, m_ref, l_ref, acc_ref, knorm_ref, n_select, TQ, TK, HALVES):
    assert TQ == TK
    nkt = i + 1
    fmin = float(jnp.finfo(F32).min)
    kf = float(n_select)

    chunk_shift = int(math.log2(CHUNK))
    q_pos = i * TQ + lax.broadcasted_iota(jnp.int32, (1, TQ), 1)
    q_chunk = jnp.maximum(lax.shift_right_logical(q_pos, chunk_shift), FRONT_PAD // CHUNK)
    k_row = lax.broadcasted_iota(jnp.int32, (TK, 1), 0)

    def halves(x):
        return [x[:, j * LANES:(j + 1) * LANES] for j in range(HALVES)]

    def fold(x, op):
        parts = [x[r * SUBLANES:(r + 1) * SUBLANES, :] for r in range(TK // SUBLANES)]
        while len(parts) > 1:
            parts = [op(parts[j], parts[j + 1]) for j in range(0, len(parts), 2)]
        return parts[0]

    def score_tiles(kts):
        for kt in kts:
            keys = ki_ref[0, pl.ds(pl.multiple_of(kt * TK, TK), TK), :]
            acc = None
            for h in range(IDX_HEADS):
                d = jnp.dot(keys, qit_ref[0, 0, h * LANES:(h + 1) * LANES, :], preferred_element_type=F32)
                term = wit_ref[0, 0, h:h + 1, :] * jnp.maximum(d, 0.0)
                acc = term if acc is None else acc + term
            sc_ref[kt] = acc
            mask_ref[kt] = acc.T

    def tile_loop(n, fn):
        def body(kt, carry):
            fn([kt])
            return carry

        lax.fori_loop(0, n, body, 0)

    def paired_loop(n, fn):
        def body(j, carry):
            fn([2 * j, 2 * j + 1])
            return carry

        lax.fori_loop(0, lax.shift_right_logical(n, 1), body, 0)

        @pl.when(lax.bitwise_and(n, 1) == 1)
        def _():
            fn([n - 1])

    paired_loop(nkt, score_tiles)

    def mask_inadmissible(kt):
        k_pos = kt * TK + k_row
        adm = (lax.shift_right_logical(k_pos, chunk_shift) <= q_chunk) & (k_pos >= FRONT_PAD)
        sc_ref[kt] = jnp.where(adm, sc_ref[kt], NEG_INF)

    mask_inadmissible(0)

    @pl.when(nkt > 1)
    def _():
        mask_inadmissible(nkt - 1)

    n_adm = CHUNK * (q_chunk + 1) - FRONT_PAD
    need = n_adm > n_select

    def count_where(pred):
        def one(kt):
            return fold(jnp.where(pred(sc_ref[kt], kt), 1.0, 0.0), jnp.add)

        def two(j, acc):
            return acc + (one(2 * j) + one(2 * j + 1))

        acc = lax.fori_loop(0, lax.shift_right_logical(nkt, 1), two, jnp.zeros((SUBLANES, TQ), F32))
        acc = lax.cond(lax.bitwise_and(nkt, 1) == 1, lambda a: a + one(nkt - 1), lambda a: a, acc)
        return jnp.sum(acc, axis=0, keepdims=True)

    def count_ge(t):
        return count_where(lambda s, kt: s >= t)

    def minmax_body(kt, carry):
        mn, mx = carry
        s = sc_ref[kt]
        mn = jnp.minimum(mn, fold(jnp.where(s == NEG_INF, jnp.inf, s), jnp.minimum))
        mx = jnp.maximum(mx, fold(s, jnp.maximum))
        return mn, mx

    mn_l, mx_l = lax.fori_loop(0, nkt, minmax_body,
                               (jnp.full((SUBLANES, TQ), jnp.inf, F32), jnp.full((SUBLANES, TQ), NEG_INF, F32)))
    s_min = jnp.min(mn_l, axis=0, keepdims=True)
    s_max = jnp.max(mx_l, axis=0, keepdims=True)
    lo0 = s_min
    hi0 = s_max + (s_max - s_min) + 1.0
    cnt0 = n_adm.astype(F32)
    active0 = jnp.where(need, 1.0, 0.0)

    def any_active(active):
        return jnp.max(active) > 0.0

    def bis_step(lo, hi, cnt, active):
        mid = lo + (hi - lo) * 0.5
        c = count_ge(mid)
        ge = c >= kf
        moving = (mid > lo) & (mid < hi) & (active > 0.0)
        lo_n = jnp.where(moving & ge, mid, lo)
        hi_n = jnp.where(moving & (~ge), mid, hi)
        cnt_n = jnp.where(moving & ge, c, cnt)
        active_n = jnp.where(moving & (cnt_n != kf), 1.0, 0.0)
        return lo_n, hi_n, cnt_n, active_n

    def bis_body(st):
        it, rest = st[0], st[1:]
        for _ in range(BISECT_UNROLL):
            rest = bis_step(*rest)
        return (it + BISECT_UNROLL,) + rest

    _, lo, hi, cnt, _ = lax.while_loop(
        lambda st: jnp.logical_and(st[0] < BISECT_STEPS, any_active(st[4])), bis_body,
        (jnp.int32(0), lo0, hi0, cnt0, active0))

    def snap_body(st):
        lo, hi, cnt, active = st

        def body(kt, mx):
            s = sc_ref[kt]
            return jnp.maximum(mx, fold(jnp.where(s < hi, s, NEG_INF), jnp.maximum))

        v1 = jnp.max(lax.fori_loop(0, nkt, body, jnp.full((SUBLANES, TQ), NEG_INF, F32)),
                     axis=0, keepdims=True)
        c1 = count_ge(v1)
        on = active > 0.0
        hit = on & (c1 >= kf)
        lo_n = jnp.where(hit, v1, lo)
        cnt_n = jnp.where(hit, c1, cnt)
        hi_n = jnp.where(on & (~hit), v1, hi)
        return lo_n, hi_n, cnt_n, jnp.where(on & (~hit), 1.0, 0.0)

    unsettled = jnp.where(need & (cnt != kf), 1.0, 0.0)
    lo, hi, cnt, _ = lax.while_loop(lambda st: any_active(st[3]), snap_body, (lo, hi, cnt, unsettled))

    thr = jnp.where(need, lo, fmin)
    tied = need & (cnt > kf)
    big_pos = jnp.int32(2 ** 30)

    def tie_break():
        want = kf - count_where(lambda s, kt: s > thr)

        def body(_, st):
            jl, jh = st
            jm = lax.shift_right_arithmetic(jl + jh, 1)
            c = count_where(lambda s, kt: (s == thr) & (kt * TK + k_row <= jm))
            ok = c >= want
            return jnp.where(ok, jl, jm), jnp.where(ok, jm, jh)

        n_keys = sc_ref.shape[0] * TK
        steps = int(math.ceil(math.log2(n_keys + 1))) + 1
        jl0 = jnp.full((1, TQ), -1, jnp.int32)
        jh0 = jnp.full((1, TQ), n_keys, jnp.int32)
        _, jh = lax.fori_loop(0, steps, body, (jl0, jh0))
        return jnp.where(tied, jh, big_pos)

    pos_cap = lax.cond(jnp.max(jnp.where(tied, 1.0, 0.0)) > 0.0, tie_break,
                       lambda: jnp.full((1, TQ), big_pos, jnp.int32))

    def lane_wide(row):
        return jnp.broadcast_to(row, (LANES, TQ)).T

    thr_w = lane_wide(thr)
    cap_w = lane_wide(pos_cap)
    chunk_w = lane_wide(q_chunk)
    k_lane = lax.broadcasted_iota(jnp.int32, (1, LANES), 1)

    def to_mask(kt, edge_tile):
        out = []
        for j, s in enumerate(halves(mask_ref[kt])):
            k_pos = kt * TK + j * LANES + k_lane
            sel = (s > thr_w) | ((s == thr_w) & (k_pos <= cap_w))
            if edge_tile:
                sel = sel & (lax.shift_right_logical(k_pos, chunk_shift) <= chunk_w) & (k_pos >= FRONT_PAD)
            out.append(jnp.where(sel, 0.0, NEG_INF))
        mask_ref[kt] = jnp.concatenate(out, axis=1)

    def inner_mask(kt, carry):
        to_mask(kt, False)
        return carry

    lax.fori_loop(1, nkt - 1, inner_mask, 0)
    to_mask(0, True)

    @pl.when(nkt > 1)
    def _():
        to_mask(nkt - 1, True)

    for n in range(ATT_KV_HEADS):
        for g in range(ATT_GROUP):
            h = n * ATT_GROUP + g
            qg_ref[n, g * TQ:(g + 1) * TQ, :] = q_ref[0, :, h * ATT_HD:(h + 1) * ATT_HD]
    l_ref[...] = jnp.zeros_like(l_ref)
    acc_ref[...] = jnp.zeros_like(acc_ref)

    kv_heads = range(ATT_KV_HEADS)

    def raw_logits(kts):
        return [[jnp.dot(qg_ref[n], kt_ref[0, kt, n * ATT_HD:(n + 1) * ATT_HD, :], preferred_element_type=F32)
                 for n in kv_heads] for kt in kts]

    def masked_logits(s, mask, n, g, b0):
        sg = s[g * TQ:(g + 1) * TQ, :] + mask
        if b0 is not None:
            h = n * ATT_GROUP + g
            sg = sg + jnp.concatenate([bias_ref[h, b0 + j] for j in range(HALVES)], axis=1)
        return sg

    def max_pass(kts, b0):
        raw = raw_logits(kts)
        masks = [mask_ref[kt] for kt in kts]
        for n in kv_heads:
            for g in range(ATT_GROUP):
                rows = slice(g * TQ, (g + 1) * TQ)
                m = m_ref[n, rows, :]
                for t in range(len(kts)):
                    for part in halves(masked_logits(raw[t][n], masks[t], n, g, b0)):
                        m = jnp.maximum(m, part)
                m_ref[n, rows, :] = m

    def exp_pass(kts, b0):
        raw = raw_logits(kts)
        masks = [mask_ref[kt] for kt in kts]
        for n in kv_heads:
            ps = [[] for _ in kts]
            for g in range(ATT_GROUP):
                rows = slice(g * TQ, (g + 1) * TQ)
                m = m_ref[n, rows, :]
                l_new = l_ref[n, rows, :]
                for t in range(len(kts)):
                    parts = [jnp.exp(part - m) for part in halves(masked_logits(raw[t][n], masks[t], n, g, b0))]
                    l_new = l_new + functools.reduce(lambda a, b: a + b, parts)
                    ps[t].append(jnp.concatenate(parts, axis=1).astype(BF16))
                l_ref[n, rows, :] = l_new
            pv = None
            for t, kt in enumerate(kts):
                vv = v_ref[0, pl.ds(pl.multiple_of(kt * TK, TK), TK), n * ATT_HD:(n + 1) * ATT_HD]
                d = jnp.dot(jnp.concatenate(ps[t], axis=0), vv, preferred_element_type=F32)
                pv = d if pv is None else pv + d
            acc_ref[n] += pv

    def run_pass(tile_fn):
        paired_loop(jnp.maximum(i - 1, 0), lambda kts: tile_fn(kts, None))

        @pl.when(i >= 1)
        def _():
            tile_fn([i - 1], 0)

        tile_fn([i], HALVES)

    for n in range(ATT_KV_HEADS):
        for g in range(ATT_GROUP):
            h = n * ATT_GROUP + g
            qh = qg_ref[n, g * TQ:(g + 1) * TQ, :].astype(F32)
            q_len = jnp.sqrt(jnp.sum(qh * qh, axis=-1, keepdims=True))
            bound = q_len * knorm_ref[n] * SHIFT_SLACK + (bmax_ref[h] + SHIFT_SLACK - 1.0)
            m_ref[n, g * TQ:(g + 1) * TQ, :] = jnp.broadcast_to(bound, (TQ, LANES))
    run_pass(exp_pass)

    def smallest_row_sum():
        return functools.reduce(jnp.minimum, [jnp.min(jnp.sum(l_ref[n], axis=-1, keepdims=True))
                                              for n in range(ATT_KV_HEADS)])

    @pl.when(jnp.logical_not(smallest_row_sum() > ROW_SUM_FLOOR))
    def _():
        m_ref[...] = jnp.full(m_ref.shape, -1e30, F32)
        run_pass(max_pass)
        for n in range(ATT_KV_HEADS):
            m_ref[n] = jnp.broadcast_to(jnp.max(m_ref[n], axis=-1, keepdims=True), m_ref.shape[1:])
        l_ref[...] = jnp.zeros_like(l_ref)
        acc_ref[...] = jnp.zeros_like(acc_ref)
        run_pass(exp_pass)

    for n in range(ATT_KV_HEADS):
        inv_l = 1.0 / jnp.sum(l_ref[n], axis=-1, keepdims=True)
        for g in range(ATT_GROUP):
            h = n * ATT_GROUP + g
            rows = slice(g * TQ, (g + 1) * TQ)
            o_ref[0, :, h * ATT_HD:(h + 1) * ATT_HD] = (acc_ref[n, rows, :] * inv_l[rows, :]).astype(o_ref.dtype)


def _t5_bucket(rel):
    nb = REL_BUCKETS // 2
    max_exact = nb // 2
    ret = jnp.where(rel > 0, nb, 0)
    n = jnp.abs(rel)
    nf = jnp.maximum(n, 1).astype(jnp.float32)
    large = max_exact + (jnp.log(nf / max_exact) / math.log(REL_MAX_DIST / max_exact)
                         * (nb - max_exact)).astype(jnp.int32)
    large = jnp.minimum(large, nb - 1)
    return ret + jnp.where(n < max_exact, n, large)


def _dsa_core(q, kt, v, qit, ki, wit, rel_bias, n_select, live_rows):
    B, LP, _ = q.shape
    nq = LP // ATT_TILE
    nkt = LP // KEY_TILE
    a = jnp.arange(ATT_TILE)[:, None]
    xx = jnp.arange(2 * KEY_TILE)[None, :]
    rel = (xx - KEY_TILE) - a
    far = rel_bias.astype(F32)[REL_BUCKETS // 2 - 1]
    onehot = jax.nn.one_hot(_t5_bucket(rel), REL_BUCKETS, dtype=F32)
    table = jnp.einsum("abk,kh->abh", onehot, rel_bias.astype(F32),
                       precision=lax.Precision.HIGHEST) - far
    bias_near = jnp.transpose(table.reshape(ATT_TILE, 2 * KEY_TILE // LANES, LANES, ATT_HEADS), (3, 1, 0, 2))
    bias_max = jnp.broadcast_to(jnp.maximum(jnp.max(table, axis=(0, 1)), 0.0)[:, None, None],
                                (ATT_HEADS, 1, LANES))
    kern = functools.partial(_dsa_kernel, n_select=n_select, live_rows=live_rows)
    tile = lambda b, i: (b, i, 0)
    tile4 = lambda b, i: (b, i, 0, 0)
    whole3 = lambda b, i: (b, 0, 0)
    whole4 = lambda b, i: (b, 0, 0, 0)
    once = pl.Buffered(1)
    return pl.pallas_call(
        kern,
        grid=(B, nq),
        in_specs=[pl.BlockSpec((1, ATT_TILE, ATT_Q_DIM), tile),
                  pl.BlockSpec((1, 1, IDX_HEADS * LANES, ATT_TILE), tile4),
                  pl.BlockSpec((1, 1, LANES, ATT_TILE), tile4),
                  pl.BlockSpec((1, nkt, ATT_KV_DIM, KEY_TILE), whole4, pipeline_mode=once),
                  pl.BlockSpec((1, LP, ATT_KV_DIM), whole3, pipeline_mode=once),
                  pl.BlockSpec((1, LP, LANES), whole3, pipeline_mode=once),
                  pl.BlockSpec((ATT_HEADS, 2 * KEY_TILE // LANES, ATT_TILE, LANES), lambda b, i: (0, 0, 0, 0),
                               pipeline_mode=once),
                  pl.BlockSpec((ATT_HEADS, 1, LANES), lambda b, i: (0, 0, 0))],
        out_specs=pl.BlockSpec((1, ATT_TILE, ATT_Q_DIM), tile),
        out_shape=jax.ShapeDtypeStruct((B, LP, ATT_Q_DIM), BF16),
        scratch_shapes=[pltpu.VMEM((nkt, KEY_TILE, ATT_TILE), F32),
                        pltpu.VMEM((nkt, ATT_TILE, KEY_TILE), F32),
                        pltpu.VMEM((ATT_KV_HEADS, ATT_GROUP * ATT_TILE, ATT_HD), BF16),
                        pltpu.VMEM((ATT_KV_HEADS, ATT_GROUP * ATT_TILE, LANES), F32),
                        pltpu.VMEM((ATT_KV_HEADS, ATT_GROUP * ATT_TILE, LANES), F32),
                        pltpu.VMEM((ATT_KV_HEADS, ATT_GROUP * ATT_TILE, ATT_HD), F32),
                        pltpu.VMEM((ATT_KV_HEADS, 1, LANES), F32)],
        compiler_params=_compiler_params(("arbitrary", "arbitrary")),
        name="dsa_core",
    )(q, qit, wit, kt, v, ki, bias_near, bias_max)


def _gdn_in_weight(w_in):
    D = w_in.shape[0]
    w = jnp.zeros((D, GDN_PROJ_W), F32).at[:, 0:w_in.shape[1]].set(w_in.astype(F32))
    return w.astype(BF16)


def _dsa_in_weight(w_in):
    D = w_in.shape[0]
    w_in = w_in.astype(BF16)
    zeros = lambda n: jnp.zeros((D, n), BF16)
    src = ATT_Q_DIM + 2 * ATT_KV_DIM
    parts = [w_in[:, 0:src]]
    for h in range(IDX_HEADS):
        parts += [w_in[:, src + h * IDX_HD:src + (h + 1) * IDX_HD], zeros(LANES - IDX_HD)]
    src += IDX_HEADS * IDX_HD
    parts += [w_in[:, src:src + IDX_HD], zeros(LANES - IDX_HD),
              w_in[:, src + IDX_HD:src + IDX_HD + IDX_HEADS], zeros(LANES - IDX_HEADS)]
    w = jnp.concatenate(parts, axis=1)
    assert w.shape[1] == DSA_PROJ_W
    return w


def kernel(x, meta_tokens, norm_mix, norm_ffn, rel_bias, gdn_w_in, gdn_conv, gdn_a_log, gdn_dt_bias,
           gdn_o_norm, gdn_w_out, dsa_w_in, dsa_q_norm, dsa_k_norm, dsa_idx_ln_g, dsa_idx_ln_b, dsa_w_out,
           moe_w_group, moe_b_group, moe_w_expert, moe_b_expert, moe_w_gate_up, moe_w_down):
    B, S, D = x.shape
    depth = norm_mix.shape[0]
    n_select = min(TOPK_MAX, S // 4)
    off = FRONT_PAD + N_META
    LP = -(-(off + S) // ROW_ALIGN) * ROW_ALIGN
    meta = jnp.broadcast_to(meta_tokens.astype(x.dtype)[None], (B, N_META, D))
    h = jnp.concatenate([jnp.zeros((B, FRONT_PAD, D), x.dtype), meta, x,
                         jnp.zeros((B, LP - off - S, D), x.dtype)], axis=1)
    h = h.reshape(B * LP, D)
    for i in range(depth):
        j = i // 2
        if i % 2 == 0:
            o = _gdn_core(h.reshape(B, LP, D), norm_mix[i], _gdn_in_weight(gdn_w_in[j]),
                          gdn_conv[j].astype(F32), gdn_a_log[j], gdn_dt_bias[j], gdn_o_norm[j])
            h = _matmul_residual(o.reshape(B * LP, GDN_V_DIM), gdn_w_out[j].astype(BF16), h)
        else:
            q, kt, v, qit, ki, wit = _dsa_in(h, norm_mix[i], _dsa_in_weight(dsa_w_in[j]), dsa_q_norm[j],
                                             dsa_k_norm[j], dsa_idx_ln_g[j], dsa_idx_ln_b[j])
            r3 = lambda t: t.reshape(B, LP, t.shape[-1])
            r4 = lambda t: t.reshape(B, LP // KEY_TILE, t.shape[-2], KEY_TILE)
            o = _dsa_core(r3(q), r4(kt), r3(v), r4(qit), r3(ki), r4(wit), rel_bias, n_select, off + S)
            h = _matmul_residual(o.reshape(B * LP, ATT_Q_DIM), dsa_w_out[j].astype(BF16), h)
        h = _moe(h, norm_ffn[i], moe_w_group[i], moe_b_group[i], moe_w_expert[i], moe_b_expert[i],
                 moe_w_gate_up[i], moe_w_down[i])
    return h.reshape(B, LP, D)[:, off:off + S]
```

```python
import functools
import math

import jax
import jax.numpy as jnp
from jax import lax
from jax.experimental import pallas as pl
from jax.experimental.pallas import tpu as pltpu

F32 = jnp.float32
BF16 = jnp.bfloat16

LANES = 128
SUBLANES = 8
VMEM_LIMIT_BYTES = 56 * 1024 * 1024

CHUNK = 64
N_META = 16
META_PAD = (-N_META) % CHUNK
RMS_EPS = 1e-6
L2_EPS = 1e-6

GDN_HEADS = 8
GDN_DK = 128
GDN_DV = 128
GDN_CONV = 4
GDN_QK_DIM = GDN_HEADS * GDN_DK
GDN_V_DIM = GDN_HEADS * GDN_DV
GDN_CONV_DIM = 2 * GDN_QK_DIM + GDN_V_DIM
GDN_GATE_OFF = GDN_CONV_DIM + GDN_V_DIM
GDN_PROJ_W = GDN_GATE_OFF + LANES

ATT_HEADS = 8
ATT_KV_HEADS = 2
ATT_GROUP = ATT_HEADS // ATT_KV_HEADS
ATT_HD = 128
ATT_Q_DIM = ATT_HEADS * ATT_HD
ATT_KV_DIM = ATT_KV_HEADS * ATT_HD
IDX_HEADS = 8
IDX_HD = 64
TOPK_MAX = 256
REL_BUCKETS = 32
REL_MAX_DIST = 128

DSA_Q_OFF = 0
DSA_K_OFF = ATT_Q_DIM
DSA_V_OFF = DSA_K_OFF + ATT_KV_DIM
DSA_QI_OFF = DSA_V_OFF + ATT_KV_DIM
DSA_KI_OFF = DSA_QI_OFF + IDX_HEADS * LANES
DSA_WI_OFF = DSA_KI_OFF + LANES
DSA_PROJ_W = DSA_WI_OFF + LANES

N_GROUPS = 4
EXPERTS_PER_GROUP = 8
N_EXPERTS = N_GROUPS * EXPERTS_PER_GROUP
EXPERT_FF = 256

GDN_ROW_TILES = (256, 128, 64)
MOE_ROUTER_TILES = (512, 256, 128)
MOE_ROW_TILES = (512, 256, 128)

ATT_TILE = 256
KEY_TILE = 256
ROW_ALIGN = KEY_TILE
FRONT_PAD = ROW_ALIGN - N_META
assert FRONT_PAD % CHUNK == META_PAD
BISECT_UNROLL = 4
BISECT_STEPS = 20
SHIFT_SLACK = 1.001
ROW_SUM_FLOOR = 1e-30
NEG_INF = float("-inf")


def _pick_tile(n, candidates):
    for c in candidates:
        if n % c == 0:
            return c
    raise ValueError(f"no tile for {n}")


def _compiler_params(semantics):
    return pltpu.CompilerParams(dimension_semantics=semantics, vmem_limit_bytes=VMEM_LIMIT_BYTES)


def _silu(x):
    return x * jax.nn.sigmoid(x)


def _norm_matmul_kernel(x_ref, g_ref, w_ref, o_ref):
    x = x_ref[...]
    y = x * lax.rsqrt(jnp.mean(x * x, axis=-1, keepdims=True) + RMS_EPS) * g_ref[...]
    o_ref[...] = jnp.dot(y.astype(BF16), w_ref[...], preferred_element_type=F32)


def _matmul_residual_kernel(a_ref, w_ref, r_ref, o_ref):
    o_ref[...] = r_ref[...] + jnp.dot(a_ref[...], w_ref[...], preferred_element_type=F32)


def _matmul_residual(a_bf16, w_bf16, res):
    T, K = a_bf16.shape
    N = w_bf16.shape[1]
    tm = _pick_tile(T, (512, 256, 128))
    return pl.pallas_call(
        _matmul_residual_kernel,
        grid=(T // tm,),
        in_specs=[pl.BlockSpec((tm, K), lambda i: (i, 0)),
                  pl.BlockSpec((K, N), lambda i: (0, 0)),
                  pl.BlockSpec((tm, N), lambda i: (i, 0))],
        out_specs=pl.BlockSpec((tm, N), lambda i: (i, 0)),
        out_shape=jax.ShapeDtypeStruct((T, N), F32),
        compiler_params=_compiler_params(("parallel",)),
        name="matmul_residual",
    )(a_bf16, w_bf16, res)


def _nt_dot(a, b):
    return lax.dot_general(a, b, (((1,), (1,)), ((), ())), preferred_element_type=F32)


def _tn_dot(a, b):
    return lax.dot_general(a, b, (((0,), (0,)), ((), ())), preferred_element_type=F32)


def _gdn_kernel(x_ref, g_ref, w_ref, conv_ref, gate_ref, ogain_ref, o_ref,
                state_ref, xbuf_ref, qkv_ref, proj_ref):
    t = pl.program_id(0)
    nb, rows = x_ref.shape[0], x_ref.shape[1]

    @pl.when(t == 0)
    def _():
        state_ref[...] = jnp.zeros_like(state_ref)
        xbuf_ref[:, 0:SUBLANES, :] = jnp.zeros((nb, SUBLANES, GDN_CONV_DIM), F32)

    for b in range(nb):
        _norm_matmul_kernel(x_ref.at[b], g_ref, w_ref, proj_ref.at[b])

    def chunk(cc, carry):
        r0 = pl.multiple_of(cc * CHUNK, CHUNK)
        _gdn_chunk(t * rows + r0, r0, nb, proj_ref, conv_ref, gate_ref, ogain_ref, o_ref,
                   state_ref, xbuf_ref, qkv_ref)
        return carry

    lax.fori_loop(0, rows // CHUNK, chunk, 0)


def _gdn_chunk(row0, r0, nb, proj_ref, conv_ref, gate_ref, ogain_ref, o_ref, state_ref, xbuf_ref, qkv_ref):
    C = CHUNK
    row = row0 + lax.broadcasted_iota(jnp.int32, (C, 1), 0)
    live = row >= FRONT_PAD
    r_i = lax.broadcasted_iota(jnp.int32, (C, C), 0)
    c_i = lax.broadcasted_iota(jnp.int32, (C, C), 1)
    causal = c_i <= r_i
    strict = c_i < r_i
    neg_rate = -jnp.exp(gate_ref[0:1, :])
    betas, gcs, gc_ts, egs, eg_lasts, eks = [], [], [], [], [], []
    for b in range(nb):
        u = jnp.where(live, proj_ref[b, pl.ds(r0, C), 0:GDN_CONV_DIM], 0.0)
        xbuf_ref[b, SUBLANES:SUBLANES + C, :] = u
        conv = conv_ref[0:1, :] * xbuf_ref[b, SUBLANES - 3:SUBLANES - 3 + C, :]
        for j in range(1, GDN_CONV):
            conv = conv + conv_ref[j:j + 1, :] * xbuf_ref[b, SUBLANES - 3 + j:SUBLANES - 3 + j + C, :]
        xbuf_ref[b, 0:SUBLANES, :] = u[C - SUBLANES:C, :]
        qkv_ref[b] = _silu(conv)

        gates = jnp.where(live, proj_ref[b, pl.ds(r0, C), GDN_GATE_OFF:GDN_GATE_OFF + LANES], 0.0)
        g = neg_rate * jax.nn.softplus(gates + gate_ref[1:2, :])
        gc = jnp.dot(causal.astype(F32), g, preferred_element_type=F32,
                     precision=lax.Precision.HIGHEST)
        g_last = gc[C - 1:C, :]
        betas.append(jax.nn.sigmoid(gates))
        gcs.append(gc)
        gc_ts.append(gc.T)
        egs.append(jnp.exp(gc))
        eg_lasts.append(jnp.exp(g_last))
        eks.append(jnp.exp(g_last - gc))

    chains = [(b, h) for b in range(nb) for h in range(GDN_HEADS)]
    heads = range(len(chains))
    dot = functools.partial(jnp.dot, preferred_element_type=F32)
    col = lambda a, h: a[:, GDN_HEADS + h:GDN_HEADS + h + 1]
    qs, ks, k16s, xs, decays = [], [], [], [], []
    for b, h in chains:
        lo = h * GDN_DK
        q = qkv_ref[b, :, lo:lo + GDN_DK]
        k = qkv_ref[b, :, GDN_QK_DIM + lo:GDN_QK_DIM + lo + GDN_DK]
        v = qkv_ref[b, :, 2 * GDN_QK_DIM + h * GDN_DV:2 * GDN_QK_DIM + (h + 1) * GDN_DV]
        q = q * lax.rsqrt(jnp.sum(q * q, axis=-1, keepdims=True) + L2_EPS) * (GDN_DK ** -0.5)
        k = k * lax.rsqrt(jnp.sum(k * k, axis=-1, keepdims=True) + L2_EPS)
        kb = k * betas[b][:, h:h + 1]
        vb = v * betas[b][:, h:h + 1]
        qs.append(q)
        ks.append(k)
        k16s.append(k.astype(BF16))
        xs.append((kb, jnp.concatenate([vb, kb * col(egs[b], h)], axis=1)))
        decays.append(jnp.exp(jnp.where(causal, col(gcs[b], h) - gc_ts[b][GDN_HEADS + h:GDN_HEADS + h + 1, :],
                                        NEG_INF)))
    kk = [_nt_dot(xs[h][0].astype(BF16), k16s[h]) for h in heads]
    qk = [_nt_dot(qs[h].astype(BF16), k16s[h]) for h in heads]
    ns = [-jnp.where(strict, kk[h] * decays[h], 0.0) for h in heads]
    xs = [x for _, x in xs]
    for it in range(6):
        n16 = [n.astype(BF16) for n in ns]
        xs = [xs[h] + dot(n16[h], xs[h].astype(BF16)) for h in heads]
        if it < 5:
            ns = [dot(n16[h], n16[h]) for h in heads]
    qk = [jnp.where(causal, qk[h] * decays[h], 0.0).astype(BF16) for h in heads]
    states = [state_ref[c] for c in heads]
    s16 = [s.astype(BF16) for s in states]
    ws = [dot(xs[c][:, GDN_DV:2 * GDN_DV].astype(BF16), s16[c]) for c in heads]
    qs_s = [dot((qs[c] * col(egs[b], h)).astype(BF16), s16[c]) for c, (b, h) in enumerate(chains)]
    v_new = [(xs[c][:, 0:GDN_DV] - ws[c]).astype(BF16) for c in heads]
    os_ = [qs_s[c] + dot(qk[c], v_new[c]) for c in heads]
    kv = [_tn_dot((ks[c] * col(eks[b], h)).astype(BF16), v_new[c]) for c, (b, h) in enumerate(chains)]
    for c, (b, h) in enumerate(chains):
        state_ref[c] = states[c] * col(eg_lasts[b], h) + kv[c]
        z = proj_ref[b, pl.ds(r0, C), GDN_CONV_DIM + h * GDN_DV:GDN_CONV_DIM + (h + 1) * GDN_DV]
        o = os_[c]
        on = o * lax.rsqrt(jnp.mean(o * o, axis=-1, keepdims=True) + RMS_EPS) * ogain_ref[...]
        o_ref[b, pl.ds(r0, C), h * GDN_DV:(h + 1) * GDN_DV] = (on * _silu(z)).astype(o_ref.dtype)


def _gdn_core(h3d, gain, w_bf16, conv_w, a_log, dt_bias, o_gain):
    B, LP, D = h3d.shape
    W = w_bf16.shape[1]
    rows = _pick_tile(LP, GDN_ROW_TILES)
    gate = jnp.zeros((2, LANES), F32)
    gate = gate.at[0, GDN_HEADS:2 * GDN_HEADS].set(a_log.astype(F32))
    gate = gate.at[1, GDN_HEADS:2 * GDN_HEADS].set(dt_bias.astype(F32))
    fix = lambda t: (0, 0)
    return pl.pallas_call(
        _gdn_kernel,
        grid=(LP // rows,),
        in_specs=[pl.BlockSpec((B, rows, D), lambda t: (0, t, 0)),
                  pl.BlockSpec((1, D), fix),
                  pl.BlockSpec((D, W), fix, pipeline_mode=pl.Buffered(1)),
                  pl.BlockSpec((GDN_CONV, GDN_CONV_DIM), fix),
                  pl.BlockSpec((2, LANES), fix),
                  pl.BlockSpec((1, GDN_DV), fix)],
        out_specs=pl.BlockSpec((B, rows, GDN_V_DIM), lambda t: (0, t, 0)),
        out_shape=jax.ShapeDtypeStruct((B, LP, GDN_V_DIM), BF16),
        scratch_shapes=[pltpu.VMEM((B * GDN_HEADS, GDN_DK, GDN_DV), F32),
                        pltpu.VMEM((B, SUBLANES + CHUNK, GDN_CONV_DIM), F32),
                        pltpu.VMEM((B, CHUNK, GDN_CONV_DIM), F32),
                        pltpu.VMEM((B, rows, W), F32)],
        compiler_params=_compiler_params(("arbitrary",)),
        name="gdn_core",
    )(h3d, gain.reshape(1, D), w_bf16, conv_w, gate, o_gain.reshape(1, GDN_DV))


def _moe_router_kernel(h_ref, gain_ref, wr_ref, br_ref, hx_ref, gid_ref, rank_ref, cnt_ref, carry_ref):
    i = pl.program_id(0)
    TM = h_ref.shape[0]
    lane = lax.broadcasted_iota(jnp.int32, (1, LANES), 1)

    @pl.when(i == 0)
    def _():
        carry_ref[...] = jnp.zeros_like(carry_ref)

    x = h_ref[...]
    xn = x * lax.rsqrt(jnp.mean(x * x, axis=-1, keepdims=True) + RMS_EPS) * gain_ref[...]
    logits = jnp.dot(xn, wr_ref[...], preferred_element_type=F32,
                     precision=lax.Precision.HIGHEST) + br_ref[...]
    e_log = logits
    g_lane = lane - N_EXPERTS
    g_log = jnp.where((g_lane >= 0) & (g_lane < N_GROUPS), logits, NEG_INF)
    g_max = jnp.max(g_log, axis=-1, keepdims=True)
    g_sel = jnp.min(jnp.where(g_log == g_max, g_lane, LANES), axis=-1, keepdims=True)
    g_sel = jnp.minimum(g_sel, N_GROUPS - 1)
    g_w = 1.0 / jnp.sum(jnp.exp(g_log - g_max), axis=-1, keepdims=True)
    in_grp = (lane >= g_sel * EXPERTS_PER_GROUP) & (lane < (g_sel + 1) * EXPERTS_PER_GROUP)
    l0 = jnp.where(in_grp, e_log, NEG_INF)
    m1 = jnp.max(l0, axis=-1, keepdims=True)
    i1 = jnp.min(jnp.where(l0 == m1, lane, LANES), axis=-1, keepdims=True)
    l1 = jnp.where(lane == i1, NEG_INF, l0)
    m2 = jnp.max(l1, axis=-1, keepdims=True)
    i2 = jnp.min(jnp.where(l1 == m2, lane, LANES), axis=-1, keepdims=True)
    p2 = jnp.exp(m2 - m1)
    w1 = g_w / (1.0 + p2)
    w2 = g_w * p2 / (1.0 + p2)
    j1 = i1 - g_sel * EXPERTS_PER_GROUP
    j2 = i2 - g_sel * EXPERTS_PER_GROUP
    hx_ref[:, 0:xn.shape[1]] = xn
    hx_ref[:, xn.shape[1]:] = jnp.where(lane == j1, w1, 0.0) + jnp.where(lane == j2, w2, 0.0)

    onehot = jnp.where(lane == g_sel, 1.0, 0.0)
    r_i = lax.broadcasted_iota(jnp.int32, (TM, TM), 0)
    c_i = lax.broadcasted_iota(jnp.int32, (TM, TM), 1)
    before = jnp.where(c_i < r_i, 1.0, 0.0).astype(BF16)
    prior = jnp.dot(before, onehot.astype(BF16), preferred_element_type=F32) + carry_ref[...]
    rank = jnp.sum(jnp.where(lane == g_sel, prior, 0.0), axis=-1, keepdims=True)
    carry_ref[...] += jnp.sum(onehot, axis=0, keepdims=True)
    cnt_ref[...] = carry_ref[...]
    gid_ref[0] = jnp.broadcast_to(g_sel.astype(F32), (TM, LANES)).T[0:1, :].astype(jnp.int32)
    rank_ref[0] = jnp.broadcast_to(rank, (TM, LANES)).T[0:1, :].astype(jnp.int32)


def _moe_dispatch_kernel(seg_ref, pos_ref, hx_ref, xs_ref, zero_ref, sem, zero_sem):
    TM = hx_ref.shape[0]

    @pl.when(pl.program_id(0) == 0)
    def _():
        zero_ref[...] = jnp.zeros_like(zero_ref)

        def zero_tile(t):
            fill = pltpu.make_async_copy(zero_ref, xs_ref.at[pl.ds(t * TM, TM), :], zero_sem)
            fill.start()
            fill.wait()

        for g in range(N_GROUPS):
            first = seg_ref[g - 1] if g else 0
            pl.when(seg_ref[g] > first)(functools.partial(zero_tile, seg_ref[g] - 1))
            tail = seg_ref[N_GROUPS - 1] + g
            pl.when(tail < xs_ref.shape[0] // TM)(functools.partial(zero_tile, tail))

    def start(a, carry):
        for b in range(SUBLANES):
            r = a * SUBLANES + b
            pltpu.make_async_copy(hx_ref.at[pl.ds(r, 1), :], xs_ref.at[pl.ds(pos_ref[0, 0, r], 1), :],
                                  sem).start()
        return carry

    lax.fori_loop(0, TM // SUBLANES, start, 0)
    pltpu.make_async_copy(hx_ref, xs_ref.at[pl.ds(0, TM), :], sem).wait()


def _moe_expert_kernel(grp_ref, xs_ref, wgu_ref, wd_ref, ys_ref, xn_ref, acc_ref):
    j = pl.program_id(0)
    e = pl.program_id(1)
    D = xn_ref.shape[1]
    lane = lax.broadcasted_iota(jnp.int32, (1, LANES), 1)

    @pl.when(grp_ref[j] >= 0)
    def _():
        @pl.when(e == 0)
        def _():
            xn_ref[...] = xs_ref[:, 0:D].astype(BF16)
            acc_ref[...] = jnp.zeros_like(acc_ref)

        hh = jnp.dot(xn_ref[...], wgu_ref[0].astype(BF16), preferred_element_type=F32)
        cw = jnp.sum(jnp.where(lane == e, xs_ref[:, D:], 0.0), axis=-1, keepdims=True)
        act = _silu(hh[:, 0:EXPERT_FF]) * hh[:, EXPERT_FF:2 * EXPERT_FF] * cw
        acc_ref[...] += jnp.dot(act.astype(BF16), wd_ref[0].astype(BF16), preferred_element_type=F32)

    last = e == EXPERTS_PER_GROUP - 1

    @pl.when(last & (grp_ref[j] >= 0))
    def _():
        ys_ref[...] = acc_ref[...]

    @pl.when(last & (grp_ref[j] < 0))
    def _():
        ys_ref[...] = jnp.zeros_like(ys_ref)


def _moe_combine_kernel(pos_ref, h_ref, ys_ref, o_ref, stage_ref, sem):
    TM = h_ref.shape[0]

    def start(a, carry):
        for b in range(SUBLANES):
            r = a * SUBLANES + b
            pltpu.make_async_copy(ys_ref.at[pl.ds(pos_ref[0, 0, r], 1), :], stage_ref.at[pl.ds(r, 1), :],
                                  sem).start()
        return carry

    lax.fori_loop(0, TM // SUBLANES, start, 0)
    pltpu.make_async_copy(ys_ref.at[pl.ds(0, TM), :], stage_ref, sem).wait()
    o_ref[...] = h_ref[...] + stage_ref[...]


def _moe(h2d, gain, w_group, b_group, w_expert, b_expert, w_gate_up, w_down):
    T, D = h2d.shape
    tr = _pick_tile(T, MOE_ROUTER_TILES)
    tm = _pick_tile(T, MOE_ROW_TILES)
    n_tiles = T // tm + N_GROUPS
    XW = D + LANES
    assert N_EXPERTS + N_GROUPS <= LANES
    wr = jnp.zeros((D, LANES), F32)
    wr = wr.at[:, 0:N_EXPERTS].set(w_expert.astype(F32))
    wr = wr.at[:, N_EXPERTS:N_EXPERTS + N_GROUPS].set(w_group.astype(F32))
    br = jnp.zeros((1, LANES), F32)
    br = br.at[0, 0:N_EXPERTS].set(b_expert.astype(F32))
    br = br.at[0, N_EXPERTS:N_EXPERTS + N_GROUPS].set(b_group.astype(F32))
    wgu = w_gate_up.reshape(N_EXPERTS, D, 2 * EXPERT_FF)
    wd = w_down.reshape(N_EXPERTS, EXPERT_FF, D)

    hx, gid, rank, counts = pl.pallas_call(
        _moe_router_kernel,
        grid=(T // tr,),
        in_specs=[pl.BlockSpec((tr, D), lambda i: (i, 0)),
                  pl.BlockSpec((1, D), lambda i: (0, 0)),
                  pl.BlockSpec((D, LANES), lambda i: (0, 0)),
                  pl.BlockSpec((1, LANES), lambda i: (0, 0))],
        out_specs=[pl.BlockSpec((tr, XW), lambda i: (i, 0)),
                   pl.BlockSpec((1, 1, tr), lambda i: (i, 0, 0)),
                   pl.BlockSpec((1, 1, tr), lambda i: (i, 0, 0)),
                   pl.BlockSpec((1, LANES), lambda i: (0, 0))],
        out_shape=[jax.ShapeDtypeStruct((T, XW), F32),
                   jax.ShapeDtypeStruct((T // tr, 1, tr), jnp.int32),
                   jax.ShapeDtypeStruct((T // tr, 1, tr), jnp.int32),
                   jax.ShapeDtypeStruct((1, LANES), F32)],
        scratch_shapes=[pltpu.VMEM((1, LANES), F32)],
        compiler_params=_compiler_params(("arbitrary",)),
        name="moe_router",
    )(h2d, gain.reshape(1, D), wr, br)

    seg_tiles = -(-counts[0, 0:N_GROUPS].astype(jnp.int32) // tm)
    seg_end = jnp.cumsum(seg_tiles)
    seg_start = seg_end - seg_tiles
    pos = (seg_start * tm)[gid.reshape(T)] + rank.reshape(T)
    tile_ids = jnp.arange(n_tiles, dtype=jnp.int32)
    tile_group = jnp.sum((tile_ids[:, None] >= seg_end[None, :]).astype(jnp.int32), axis=1)
    tile_group = jnp.where(tile_ids < seg_end[N_GROUPS - 1], tile_group, -1)
    pos3 = pos.reshape(T // tm, 1, tm)

    xs = pl.pallas_call(
        _moe_dispatch_kernel,
        grid_spec=pltpu.PrefetchScalarGridSpec(
            num_scalar_prefetch=1,
            grid=(T // tm,),
            in_specs=[pl.BlockSpec((1, 1, tm), lambda i, seg: (i, 0, 0), memory_space=pltpu.SMEM),
                      pl.BlockSpec((tm, XW), lambda i, seg: (i, 0))],
            out_specs=pl.BlockSpec(memory_space=pl.ANY),
            scratch_shapes=[pltpu.VMEM((tm, XW), F32), pltpu.SemaphoreType.DMA(()),
                            pltpu.SemaphoreType.DMA(())]),
        out_shape=jax.ShapeDtypeStruct((n_tiles * tm, XW), F32),
        compiler_params=_compiler_params(("arbitrary",)),
        name="moe_dispatch",
    )(seg_end, pos3, hx)

    def expert_index(j, e, grp):
        return (jnp.maximum(grp[j], 0) * EXPERTS_PER_GROUP + e, 0, 0)

    ys = pl.pallas_call(
        _moe_expert_kernel,
        grid_spec=pltpu.PrefetchScalarGridSpec(
            num_scalar_prefetch=1,
            grid=(n_tiles, EXPERTS_PER_GROUP),
            in_specs=[pl.BlockSpec((tm, XW), lambda j, e, grp: (j, 0)),
                      pl.BlockSpec((1, D, 2 * EXPERT_FF), expert_index),
                      pl.BlockSpec((1, EXPERT_FF, D), expert_index)],
            out_specs=pl.BlockSpec((tm, D), lambda j, e, grp: (j, 0)),
            scratch_shapes=[pltpu.VMEM((tm, D), BF16), pltpu.VMEM((tm, D), F32)]),
        out_shape=jax.ShapeDtypeStruct((n_tiles * tm, D), F32),
        compiler_params=_compiler_params(("parallel", "arbitrary")),
        name="moe_experts",
    )(tile_group, xs, wgu, wd)

    return pl.pallas_call(
        _moe_combine_kernel,
        grid=(T // tm,),
        in_specs=[pl.BlockSpec((1, 1, tm), lambda i: (i, 0, 0), memory_space=pltpu.SMEM),
                  pl.BlockSpec((tm, D), lambda i: (i, 0)),
                  pl.BlockSpec(memory_space=pl.ANY)],
        out_specs=pl.BlockSpec((tm, D), lambda i: (i, 0)),
        out_shape=jax.ShapeDtypeStruct((T, D), F32),
        scratch_shapes=[pltpu.VMEM((tm, D), F32), pltpu.SemaphoreType.DMA(())],
        compiler_params=_compiler_params(("arbitrary",)),
        name="moe_combine",
    )(pos3, h2d, ys)


def _dsa_prep_kernel(p_ref, qg_ref, kg_ref, lng_ref, lnb_ref, q_ref, kt_ref, v_ref, qit_ref, ki_ref, wit_ref):
    lane = lax.broadcasted_iota(jnp.int32, (1, LANES), 1)
    for h in range(ATT_HEADS):
        x = p_ref[:, DSA_Q_OFF + h * ATT_HD:DSA_Q_OFF + (h + 1) * ATT_HD]
        y = x * lax.rsqrt(jnp.mean(x * x, axis=-1, keepdims=True) + RMS_EPS) * qg_ref[...]
        q_ref[:, h * ATT_HD:(h + 1) * ATT_HD] = (y * (ATT_HD ** -0.5)).astype(BF16)
    for n in range(ATT_KV_HEADS):
        x = p_ref[:, DSA_K_OFF + n * ATT_HD:DSA_K_OFF + (n + 1) * ATT_HD]
        y = x * lax.rsqrt(jnp.mean(x * x, axis=-1, keepdims=True) + RMS_EPS) * kg_ref[...]
        kt_ref[0, n * ATT_HD:(n + 1) * ATT_HD, :] = y.T.astype(BF16)
    v_ref[...] = p_ref[:, DSA_V_OFF:DSA_V_OFF + ATT_KV_DIM].astype(BF16)
    qit_ref[0] = p_ref[:, DSA_QI_OFF:DSA_QI_OFF + IDX_HEADS * LANES].T.astype(BF16)
    x = p_ref[:, DSA_KI_OFF:DSA_KI_OFF + LANES]
    live = lane < IDX_HD
    mu = jnp.sum(x, axis=-1, keepdims=True) * (1.0 / IDX_HD)
    xc = jnp.where(live, x - mu, 0.0)
    var = jnp.sum(xc * xc, axis=-1, keepdims=True) * (1.0 / IDX_HD)
    ki = xc * lax.rsqrt(var + RMS_EPS) * lng_ref[...] + lnb_ref[...]
    ki_ref[...] = jnp.where(live, ki, 0.0).astype(BF16)
    wit_ref[0] = (p_ref[:, DSA_WI_OFF:DSA_WI_OFF + LANES] * (IDX_HEADS ** -0.5 * IDX_HD ** -0.5)).T


def _dsa_in_kernel(x_ref, g_ref, w_ref, qg_ref, kg_ref, lng_ref, lnb_ref,
                   q_ref, kt_ref, v_ref, qit_ref, ki_ref, wit_ref, proj_ref):
    _norm_matmul_kernel(x_ref, g_ref, w_ref, proj_ref)
    _dsa_prep_kernel(proj_ref, qg_ref, kg_ref, lng_ref, lnb_ref, q_ref, kt_ref, v_ref, qit_ref, ki_ref, wit_ref)


def _dsa_in(h2d, gain, w_bf16, q_gain, k_gain, ln_g, ln_b):
    T, D = h2d.shape
    W = w_bf16.shape[1]
    tm = KEY_TILE
    assert T % tm == 0 and ATT_TILE == KEY_TILE
    lng = jnp.zeros((1, LANES), F32).at[0, 0:IDX_HD].set(ln_g.astype(F32))
    lnb = jnp.zeros((1, LANES), F32).at[0, 0:IDX_HD].set(ln_b.astype(F32))
    row = lambda i: (i, 0)
    fix = lambda i: (0, 0)
    blk = lambda i: (i, 0, 0)
    return pl.pallas_call(
        _dsa_in_kernel,
        grid=(T // tm,),
        in_specs=[pl.BlockSpec((tm, D), row), pl.BlockSpec((1, D), fix), pl.BlockSpec((D, W), fix),
                  pl.BlockSpec((1, ATT_HD), fix), pl.BlockSpec((1, ATT_HD), fix),
                  pl.BlockSpec((1, LANES), fix), pl.BlockSpec((1, LANES), fix)],
        out_specs=[pl.BlockSpec((tm, ATT_Q_DIM), row), pl.BlockSpec((1, ATT_KV_DIM, tm), blk),
                   pl.BlockSpec((tm, ATT_KV_DIM), row), pl.BlockSpec((1, IDX_HEADS * LANES, tm), blk),
                   pl.BlockSpec((tm, LANES), row), pl.BlockSpec((1, LANES, tm), blk)],
        out_shape=[jax.ShapeDtypeStruct((T, ATT_Q_DIM), BF16), jax.ShapeDtypeStruct((T // tm, ATT_KV_DIM, tm), BF16),
                   jax.ShapeDtypeStruct((T, ATT_KV_DIM), BF16),
                   jax.ShapeDtypeStruct((T // tm, IDX_HEADS * LANES, tm), BF16),
                   jax.ShapeDtypeStruct((T, LANES), BF16), jax.ShapeDtypeStruct((T // tm, LANES, tm), F32)],
        scratch_shapes=[pltpu.VMEM((tm, W), F32)],
        compiler_params=_compiler_params(("parallel",)),
        name="dsa_in",
    )(h2d, gain.reshape(1, D), w_bf16, q_gain.reshape(1, ATT_HD), k_gain.reshape(1, ATT_HD), lng, lnb)


def _dsa_kernel(q_ref, qit_ref, wit_ref, kt_ref, v_ref, ki_ref, bias_ref, bmax_ref, o_ref,
                sc_ref, mask_ref, qg_ref, m_ref, l_ref, acc_ref, knorm_ref, *, n_select, live_rows):
    i = pl.program_id(1)
    TQ = ATT_TILE
    TK = KEY_TILE
    HALVES = TK // LANES

    @pl.when(i == 0)
    def _():
        def body(kt, mx):
            kk = kt_ref[0, kt].astype(F32)
            return tuple(jnp.maximum(mx[n], jnp.sum(jnp.square(kk[n * ATT_HD:(n + 1) * ATT_HD, :]),
                                                    axis=0, keepdims=True)) for n in range(ATT_KV_HEADS))

        mx = lax.fori_loop(0, kt_ref.shape[1], body,
                           tuple(jnp.zeros((1, TK), F32) for _ in range(ATT_KV_HEADS)))
        for n in range(ATT_KV_HEADS):
            knorm_ref[n] = jnp.broadcast_to(jnp.sqrt(jnp.max(mx[n], axis=-1, keepdims=True)), (1, LANES))

    @pl.when(i * TQ >= live_rows)
    def _():
        o_ref[...] = jnp.zeros_like(o_ref)

    @pl.when(i * TQ < live_rows)
    def _():
        _dsa_tile(i, q_ref, qit_ref, wit_ref, kt_ref, v_ref, ki_ref, bias_ref, bmax_ref, o_ref,
                  sc_ref, mask_ref, qg_ref, m_ref, l_ref, acc_ref, knorm_ref, n_select, TQ, TK, HALVES)


def _dsa_tile(i, q_ref, qit_ref, wit_ref, kt_ref, v_ref, ki_ref, bias_ref, bmax_ref, o_ref,
              sc_ref, mask_ref, qg_ref, m_ref, l_ref, acc_ref, knorm_ref, n_select, TQ, TK, HALVES):
    assert TQ == TK
    nkt = i + 1
    fmin = float(jnp.finfo(F32).min)
    kf = float(n_select)

    chunk_shift = int(math.log2(CHUNK))
    q_pos = i * TQ + lax.broadcasted_iota(jnp.int32, (1, TQ), 1)
    q_chunk = jnp.maximum(lax.shift_right_logical(q_pos, chunk_shift), FRONT_PAD // CHUNK)
    k_row = lax.broadcasted_iota(jnp.int32, (TK, 1), 0)

    def halves(x):
        return [x[:, j * LANES:(j + 1) * LANES] for j in range(HALVES)]

    def fold(x, op):
        parts = [x[r * SUBLANES:(r + 1) * SUBLANES, :] for r in range(TK // SUBLANES)]
        while len(parts) > 1:
            parts = [op(parts[j], parts[j + 1]) for j in range(0, len(parts), 2)]
        return parts[0]

    def score_tiles(kts):
        for kt in kts:
            keys = ki_ref[0, pl.ds(pl.multiple_of(kt * TK, TK), TK), :]
            acc = None
            for h in range(IDX_HEADS):
                d = jnp.dot(keys, qit_ref[0, 0, h * LANES:(h + 1) * LANES, :], preferred_element_type=F32)
                term = wit_ref[0, 0, h:h + 1, :] * jnp.maximum(d, 0.0)
                acc = term if acc is None else acc + term
            sc_ref[kt] = acc
            mask_ref[kt] = acc.T

    def tile_loop(n, fn):
        def body(kt, carry):
            fn([kt])
            return carry

        lax.fori_loop(0, n, body, 0)

    def paired_loop(n, fn):
        def body(j, carry):
            fn([2 * j, 2 * j + 1])
            return carry

        lax.fori_loop(0, lax.shift_right_logical(n, 1), body, 0)

        @pl.when(lax.bitwise_and(n, 1) == 1)
        def _():
            fn([n - 1])

    paired_loop(nkt, score_tiles)

    def mask_inadmissible(kt):
        k_pos = kt * TK + k_row
        adm = (lax.shift_right_logical(k_pos, chunk_shift) <= q_chunk) & (k_pos >= FRONT_PAD)
        sc_ref[kt] = jnp.where(adm, sc_ref[kt], NEG_INF)

    mask_inadmissible(0)

    @pl.when(nkt > 1)
    def _():
        mask_inadmissible(nkt - 1)

    n_adm = CHUNK * (q_chunk + 1) - FRONT_PAD
    need = n_adm > n_select

    def count_where(pred):
        def one(kt):
            return fold(jnp.where(pred(sc_ref[kt], kt), 1.0, 0.0), jnp.add)

        def two(j, acc):
            return acc + (one(2 * j) + one(2 * j + 1))

        acc = lax.fori_loop(0, lax.shift_right_logical(nkt, 1), two, jnp.zeros((SUBLANES, TQ), F32))
        acc = lax.cond(lax.bitwise_and(nkt, 1) == 1, lambda a: a + one(nkt - 1), lambda a: a, acc)
        return jnp.sum(acc, axis=0, keepdims=True)

    def count_ge(t):
        return count_where(lambda s, kt: s >= t)

    def minmax_body(kt, carry):
        mn, mx = carry
        s = sc_ref[kt]
        mn = jnp.minimum(mn, fold(jnp.where(s == NEG_INF, jnp.inf, s), jnp.minimum))
        mx = jnp.maximum(mx, fold(s, jnp.maximum))
        return mn, mx

    mn_l, mx_l = lax.fori_loop(0, nkt, minmax_body,
                               (jnp.full((SUBLANES, TQ), jnp.inf, F32), jnp.full((SUBLANES, TQ), NEG_INF, F32)))
    s_min = jnp.min(mn_l, axis=0, keepdims=True)
    s_max = jnp.max(mx_l, axis=0, keepdims=True)
    lo0 = s_min
    hi0 = s_max + (s_max - s_min) + 1.0
    cnt0 = n_adm.astype(F32)
    active0 = jnp.where(need, 1.0, 0.0)

    def any_active(active):
        return jnp.max(active) > 0.0

    def bis_step(lo, hi, cnt, active):
        mid = lo + (hi - lo) * 0.5
        c = count_ge(mid)
        ge = c >= kf
        moving = (mid > lo) & (mid < hi) & (active > 0.0)
        lo_n = jnp.where(moving & ge, mid, lo)
        hi_n = jnp.where(moving & (~ge), mid, hi)
        cnt_n = jnp.where(moving & ge, c, cnt)
        active_n = jnp.where(moving & (cnt_n != kf), 1.0, 0.0)
        return lo_n, hi_n, cnt_n, active_n

    def bis_body(st):
        it, rest = st[0], st[1:]
        for _ in range(BISECT_UNROLL):
            rest = bis_step(*rest)
        return (it + BISECT_UNROLL,) + rest

    _, lo, hi, cnt, _ = lax.while_loop(
        lambda st: jnp.logical_and(st[0] < BISECT_STEPS, any_active(st[4])), bis_body,
        (jnp.int32(0), lo0, hi0, cnt0, active0))

    def snap_body(st):
        lo, hi, cnt, active = st

        def body(kt, mx):
            s = sc_ref[kt]
            return jnp.maximum(mx, fold(jnp.where(s < hi, s, NEG_INF), jnp.maximum))

        v1 = jnp.max(lax.fori_loop(0, nkt, body, jnp.full((SUBLANES, TQ), NEG_INF, F32)),
                     axis=0, keepdims=True)
        c1 = count_ge(v1)
        on = active > 0.0
        hit = on & (c1 >= kf)
        lo_n = jnp.where(hit, v1, lo)
        cnt_n = jnp.where(hit, c1, cnt)
        hi_n = jnp.where(on & (~hit), v1, hi)
        return lo_n, hi_n, cnt_n, jnp.where(on & (~hit), 1.0, 0.0)

    unsettled = jnp.where(need & (cnt != kf), 1.0, 0.0)
    lo, hi, cnt, _ = lax.while_loop(lambda st: any_active(st[3]), snap_body, (lo, hi, cnt, unsettled))

    thr = jnp.where(need, lo, fmin)
    tied = need & (cnt > kf)
    big_pos = jnp.int32(2 ** 30)

    def tie_break():
        want = kf - count_where(lambda s, kt: s > thr)

        def body(_, st):
            jl, jh = st
            jm = lax.shift_right_arithmetic(jl + jh, 1)
            c = count_where(lambda s, kt: (s == thr) & (kt * TK + k_row <= jm))
            ok = c >= want
            return jnp.where(ok, jl, jm), jnp.where(ok, jm, jh)

        n_keys = sc_ref.shape[0] * TK
        steps = int(math.ceil(math.log2(n_keys + 1))) + 1
        jl0 = jnp.full((1, TQ), -1, jnp.int32)
        jh0 = jnp.full((1, TQ), n_keys, jnp.int32)
        _, jh = lax.fori_loop(0, steps, body, (jl0, jh0))
        return jnp.where(tied, jh, big_pos)

    pos_cap = lax.cond(jnp.max(jnp.where(tied, 1.0, 0.0)) > 0.0, tie_break,
                       lambda: jnp.full((1, TQ), big_pos, jnp.int32))

    def lane_wide(row):
        return jnp.broadcast_to(row, (LANES, TQ)).T

    thr_w = lane_wide(thr)
    cap_w = lane_wide(pos_cap)
    chunk_w = lane_wide(q_chunk)
    k_lane = lax.broadcasted_iota(jnp.int32, (1, LANES), 1)

    def to_mask(kt, edge_tile):
        out = []
        for j, s in enumerate(halves(mask_ref[kt])):
            k_pos = kt * TK + j * LANES + k_lane
            sel = (s > thr_w) | ((s == thr_w) & (k_pos <= cap_w))
            if edge_tile:
                sel = sel & (lax.shift_right_logical(k_pos, chunk_shift) <= chunk_w) & (k_pos >= FRONT_PAD)
            out.append(jnp.where(sel, 0.0, NEG_INF))
        mask_ref[kt] = jnp.concatenate(out, axis=1)

    def inner_mask(kt, carry):
        to_mask(kt, False)
        return carry

    lax.fori_loop(1, nkt - 1, inner_mask, 0)
    to_mask(0, True)

    @pl.when(nkt > 1)
    def _():
        to_mask(nkt - 1, True)

    for n in range(ATT_KV_HEADS):
        for g in range(ATT_GROUP):
            h = n * ATT_GROUP + g
            qg_ref[n, g * TQ:(g + 1) * TQ, :] = q_ref[0, :, h * ATT_HD:(h + 1) * ATT_HD]
    l_ref[...] = jnp.zeros_like(l_ref)
    acc_ref[...] = jnp.zeros_like(acc_ref)

    kv_heads = range(ATT_KV_HEADS)

    def raw_logits(kts):
        return [[jnp.dot(qg_ref[n], kt_ref[0, kt, n * ATT_HD:(n + 1) * ATT_HD, :], preferred_element_type=F32)
                 for n in kv_heads] for kt in kts]

    def masked_logits(s, mask, n, g, b0):
        sg = s[g * TQ:(g + 1) * TQ, :] + mask
        if b0 is not None:
            h = n * ATT_GROUP + g
            sg = sg + jnp.concatenate([bias_ref[h, b0 + j] for j in range(HALVES)], axis=1)
        return sg

    def max_pass(kts, b0):
        raw = raw_logits(kts)
        masks = [mask_ref[kt] for kt in kts]
        for n in kv_heads:
            for g in range(ATT_GROUP):
                rows = slice(g * TQ, (g + 1) * TQ)
                m = m_ref[n, rows, :]
                for t in range(len(kts)):
                    for part in halves(masked_logits(raw[t][n], masks[t], n, g, b0)):
                        m = jnp.maximum(m, part)
                m_ref[n, rows, :] = m

    def exp_pass(kts, b0):
        raw = raw_logits(kts)
        masks = [mask_ref[kt] for kt in kts]
        for n in kv_heads:
            ps = [[] for _ in kts]
            for g in range(ATT_GROUP):
                rows = slice(g * TQ, (g + 1) * TQ)
                m = m_ref[n, rows, :]
                l_new = l_ref[n, rows, :]
                for t in range(len(kts)):
                    parts = [jnp.exp(part - m) for part in halves(masked_logits(raw[t][n], masks[t], n, g, b0))]
                    l_new = l_new + functools.reduce(lambda a, b: a + b, parts)
                    ps[t].append(jnp.concatenate(parts, axis=1).astype(BF16))
                l_ref[n, rows, :] = l_new
            pv = None
            for t, kt in enumerate(kts):
                vv = v_ref[0, pl.ds(pl.multiple_of(kt * TK, TK), TK), n * ATT_HD:(n + 1) * ATT_HD]
                d = jnp.dot(jnp.concatenate(ps[t], axis=0), vv, preferred_element_type=F32)
                pv = d if pv is None else pv + d
            acc_ref[n] += pv

    def run_pass(tile_fn):
        paired_loop(jnp.maximum(i - 1, 0), lambda kts: tile_fn(kts, None))

        @pl.when(i >= 1)
        def _():
            tile_fn([i - 1], 0)

        tile_fn([i], HALVES)

    for n in range(ATT_KV_HEADS):
        for g in range(ATT_GROUP):
            h = n * ATT_GROUP + g
            qh = qg_ref[n, g * TQ:(g + 1) * TQ, :].astype(F32)
            q_len = jnp.sqrt(jnp.sum(qh * qh, axis=-1, keepdims=True))
            bound = q_len * knorm_ref[n] * SHIFT_SLACK + (bmax_ref[h] + SHIFT_SLACK - 1.0)
            m_ref[n, g * TQ:(g + 1) * TQ, :] = jnp.broadcast_to(bound, (TQ, LANES))
    run_pass(exp_pass)

    def smallest_row_sum():
        return functools.reduce(jnp.minimum, [jnp.min(jnp.sum(l_ref[n], axis=-1, keepdims=True))
                                              for n in range(ATT_KV_HEADS)])

    @pl.when(jnp.logical_not(smallest_row_sum() > ROW_SUM_FLOOR))
    def _():
        m_ref[...] = jnp.full(m_ref.shape, -1e30, F32)
        run_pass(max_pass)
        for n in range(ATT_KV_HEADS):
            m_ref[n] = jnp.broadcast_to(jnp.max(m_ref[n], axis=-1, keepdims=True), m_ref.shape[1:])
        l_ref[...] = jnp.zeros_like(l_ref)
        acc_ref[...] = jnp.zeros_like(acc_ref)
        run_pass(exp_pass)

    for n in range(ATT_KV_HEADS):
        inv_l = 1.0 / jnp.sum(l_ref[n], axis=-1, keepdims=True)
        for g in range(ATT_GROUP):
            h = n * ATT_GROUP + g
            rows = slice(g * TQ, (g + 1) * TQ)
            o_ref[0, :, h * ATT_HD:(h + 1) * ATT_HD] = (acc_ref[n, rows, :] * inv_l[rows, :]).astype(o_ref.dtype)


def _t5_bucket(rel):
    nb = REL_BUCKETS // 2
    max_exact = nb // 2
    ret = jnp.where(rel > 0, nb, 0)
    n = jnp.abs(rel)
    nf = jnp.maximum(n, 1).astype(jnp.float32)
    large = max_exact + (jnp.log(nf / max_exact) / math.log(REL_MAX_DIST / max_exact)
                         * (nb - max_exact)).astype(jnp.int32)
    large = jnp.minimum(large, nb - 1)
    return ret + jnp.where(n < max_exact, n, large)


def _dsa_core(q, kt, v, qit, ki, wit, rel_bias, n_select, live_rows):
    B, LP, _ = q.shape
    nq = LP // ATT_TILE
    nkt = LP // KEY_TILE
    a = jnp.arange(ATT_TILE)[:, None]
    xx = jnp.arange(2 * KEY_TILE)[None, :]
    rel = (xx - KEY_TILE) - a
    far = rel_bias.astype(F32)[REL_BUCKETS // 2 - 1]
    onehot = jax.nn.one_hot(_t5_bucket(rel), REL_BUCKETS, dtype=F32)
    table = jnp.einsum("abk,kh->abh", onehot, rel_bias.astype(F32),
                       precision=lax.Precision.HIGHEST) - far
    bias_near = jnp.transpose(table.reshape(ATT_TILE, 2 * KEY_TILE // LANES, LANES, ATT_HEADS), (3, 1, 0, 2))
    bias_max = jnp.broadcast_to(jnp.maximum(jnp.max(table, axis=(0, 1)), 0.0)[:, None, None],
                                (ATT_HEADS, 1, LANES))
    kern = functools.partial(_dsa_kernel, n_select=n_select, live_rows=live_rows)
    tile = lambda b, i: (b, i, 0)
    tile4 = lambda b, i: (b, i, 0, 0)
    whole3 = lambda b, i: (b, 0, 0)
    whole4 = lambda b, i: (b, 0, 0, 0)
    once = pl.Buffered(1)
    return pl.pallas_call(
        kern,
        grid=(B, nq),
        in_specs=[pl.BlockSpec((1, ATT_TILE, ATT_Q_DIM), tile),
                  pl.BlockSpec((1, 1, IDX_HEADS * LANES, ATT_TILE), tile4),
                  pl.BlockSpec((1, 1, LANES, ATT_TILE), tile4),
                  pl.BlockSpec((1, nkt, ATT_KV_DIM, KEY_TILE), whole4, pipeline_mode=once),
                  pl.BlockSpec((1, LP, ATT_KV_DIM), whole3, pipeline_mode=once),
                  pl.BlockSpec((1, LP, LANES), whole3, pipeline_mode=once),
                  pl.BlockSpec((ATT_HEADS, 2 * KEY_TILE // LANES, ATT_TILE, LANES), lambda b, i: (0, 0, 0, 0),
                               pipeline_mode=once),
                  pl.BlockSpec((ATT_HEADS, 1, LANES), lambda b, i: (0, 0, 0))],
        out_specs=pl.BlockSpec((1, ATT_TILE, ATT_Q_DIM), tile),
        out_shape=jax.ShapeDtypeStruct((B, LP, ATT_Q_DIM), BF16),
        scratch_shapes=[pltpu.VMEM((nkt, KEY_TILE, ATT_TILE), F32),
                        pltpu.VMEM((nkt, ATT_TILE, KEY_TILE), F32),
                        pltpu.VMEM((ATT_KV_HEADS, ATT_GROUP * ATT_TILE, ATT_HD), BF16),
                        pltpu.VMEM((ATT_KV_HEADS, ATT_GROUP * ATT_TILE, LANES), F32),
                        pltpu.VMEM((ATT_KV_HEADS, ATT_GROUP * ATT_TILE, LANES), F32),
                        pltpu.VMEM((ATT_KV_HEADS, ATT_GROUP * ATT_TILE, ATT_HD), F32),
                        pltpu.VMEM((ATT_KV_HEADS, 1, LANES), F32)],
        compiler_params=_compiler_params(("arbitrary", "arbitrary")),
        name="dsa_core",
    )(q, qit, wit, kt, v, ki, bias_near, bias_max)


def _gdn_in_weight(w_in):
    D = w_in.shape[0]
    w = jnp.zeros((D, GDN_PROJ_W), F32).at[:, 0:w_in.shape[1]].set(w_in.astype(F32))
    return w.astype(BF16)


def _dsa_in_weight(w_in):
    D = w_in.shape[0]
    w_in = w_in.astype(BF16)
    zeros = lambda n: jnp.zeros((D, n), BF16)
    src = ATT_Q_DIM + 2 * ATT_KV_DIM
    parts = [w_in[:, 0:src]]
    for h in range(IDX_HEADS):
        parts += [w_in[:, src + h * IDX_HD:src + (h + 1) * IDX_HD], zeros(LANES - IDX_HD)]
    src += IDX_HEADS * IDX_HD
    parts += [w_in[:, src:src + IDX_HD], zeros(LANES - IDX_HD),
              w_in[:, src + IDX_HD:src + IDX_HD + IDX_HEADS], zeros(LANES - IDX_HEADS)]
    w = jnp.concatenate(parts, axis=1)
    assert w.shape[1] == DSA_PROJ_W
    return w


def kernel(x, meta_tokens, norm_mix, norm_ffn, rel_bias, gdn_w_in, gdn_conv, gdn_a_log, gdn_dt_bias,
           gdn_o_norm, gdn_w_out, dsa_w_in, dsa_q_norm, dsa_k_norm, dsa_idx_ln_g, dsa_idx_ln_b, dsa_w_out,
           moe_w_group, moe_b_group, moe_w_expert, moe_b_expert, moe_w_gate_up, moe_w_down):
    B, S, D = x.shape
    depth = norm_mix.shape[0]
    n_select = min(TOPK_MAX, S // 4)
    off = FRONT_PAD + N_META
    LP = -(-(off + S) // ROW_ALIGN) * ROW_ALIGN
    meta = jnp.broadcast_to(meta_tokens.astype(x.dtype)[None], (B, N_META, D))
    h = jnp.concatenate([jnp.zeros((B, FRONT_PAD, D), x.dtype), meta, x,
                         jnp.zeros((B, LP - off - S, D), x.dtype)], axis=1)
    h = h.reshape(B * LP, D)
    for i in range(depth):
        j = i // 2
        if i % 2 == 0:
            o = _gdn_core(h.reshape(B, LP, D), norm_mix[i], _gdn_in_weight(gdn_w_in[j]),
                          gdn_conv[j].astype(F32), gdn_a_log[j], gdn_dt_bias[j], gdn_o_norm[j])
            h = _matmul_residual(o.reshape(B * LP, GDN_V_DIM), gdn_w_out[j].astype(BF16), h)
        else:
            q, kt, v, qit, ki, wit = _dsa_in(h, norm_mix[i], _dsa_in_weight(dsa_w_in[j]), dsa_q_norm[j],
                                             dsa_k_norm[j], dsa_idx_ln_g[j], dsa_idx_ln_b[j])
            r3 = lambda t: t.reshape(B, LP, t.shape[-1])
            r4 = lambda t: t.reshape(B, LP // KEY_TILE, t.shape[-2], KEY_TILE)
            o = _dsa_core(r3(q), r4(kt), r3(v), r4(qit), r3(ki), r4(wit), rel_bias, n_select, off + S)
            h = _matmul_residual(o.reshape(B * LP, ATT_Q_DIM), dsa_w_out[j].astype(BF16), h)
        h = _moe(h, norm_ffn[i], moe_w_group[i], moe_b_group[i], moe_w_expert[i], moe_b_expert[i],
                 moe_w_gate_up[i], moe_w_down[i])
    return h.reshape(B, LP, D)[:, off:off + S]
```

```python
import functools
import math

import jax
import jax.numpy as jnp
from jax import lax
from jax.experimental import pallas as pl
from jax.experimental.pallas import tpu as pltpu

F32 = jnp.float32
BF16 = jnp.bfloat16

LANES = 128
SUBLANES = 8
VMEM_LIMIT_BYTES = 56 * 1024 * 1024

CHUNK = 64
N_META = 16
META_PAD = (-N_META) % CHUNK
RMS_EPS = 1e-6
L2_EPS = 1e-6

GDN_HEADS = 8
GDN_DK = 128
GDN_DV = 128
GDN_CONV = 4
GDN_QK_DIM = GDN_HEADS * GDN_DK
GDN_V_DIM = GDN_HEADS * GDN_DV
GDN_CONV_DIM = 2 * GDN_QK_DIM + GDN_V_DIM
GDN_GATE_OFF = GDN_CONV_DIM + GDN_V_DIM
GDN_PROJ_W = GDN_GATE_OFF + LANES

ATT_HEADS = 8
ATT_KV_HEADS = 2
ATT_GROUP = ATT_HEADS // ATT_KV_HEADS
ATT_HD = 128
ATT_Q_DIM = ATT_HEADS * ATT_HD
ATT_KV_DIM = ATT_KV_HEADS * ATT_HD
IDX_HEADS = 8
IDX_HD = 64
TOPK_MAX = 256
REL_BUCKETS = 32
REL_MAX_DIST = 128

DSA_Q_OFF = 0
DSA_K_OFF = ATT_Q_DIM
DSA_V_OFF = DSA_K_OFF + ATT_KV_DIM
DSA_QI_OFF = DSA_V_OFF + ATT_KV_DIM
DSA_KI_OFF = DSA_QI_OFF + IDX_HEADS * LANES
DSA_WI_OFF = DSA_KI_OFF + LANES
DSA_PROJ_W = DSA_WI_OFF + LANES

N_GROUPS = 4
EXPERTS_PER_GROUP = 8
N_EXPERTS = N_GROUPS * EXPERTS_PER_GROUP
EXPERT_FF = 256

GDN_ROW_TILES = (256, 128, 64)
MOE_ROUTER_TILES = (512, 256, 128)
MOE_ROW_TILES = (512, 256, 128)

ATT_TILE = 256
KEY_TILE = 256
ROW_ALIGN = KEY_TILE
FRONT_PAD = ROW_ALIGN - N_META
assert FRONT_PAD % CHUNK == META_PAD
BISECT_UNROLL = 4
BISECT_STEPS = 20
SHIFT_SLACK = 1.001
ROW_SUM_FLOOR = 1e-30
NEG_INF = float("-inf")


def _pick_tile(n, candidates):
    for c in candidates:
        if n % c == 0:
            return c
    raise ValueError(f"no tile for {n}")


def _compiler_params(semantics):
    return pltpu.CompilerParams(dimension_semantics=semantics, vmem_limit_bytes=VMEM_LIMIT_BYTES)


def _silu(x):
    return x * jax.nn.sigmoid(x)


def _norm_matmul_kernel(x_ref, g_ref, w_ref, o_ref):
    x = x_ref[...]
    y = x * lax.rsqrt(jnp.mean(x * x, axis=-1, keepdims=True) + RMS_EPS) * g_ref[...]
    o_ref[...] = jnp.dot(y.astype(BF16), w_ref[...], preferred_element_type=F32)


def _matmul_residual_kernel(a_ref, w_ref, r_ref, o_ref):
    o_ref[...] = r_ref[...] + jnp.dot(a_ref[...], w_ref[...], preferred_element_type=F32)


def _matmul_residual(a_bf16, w_bf16, res):
    T, K = a_bf16.shape
    N = w_bf16.shape[1]
    tm = _pick_tile(T, (512, 256, 128))
    return pl.pallas_call(
        _matmul_residual_kernel,
        grid=(T // tm,),
        in_specs=[pl.BlockSpec((tm, K), lambda i: (i, 0)),
                  pl.BlockSpec((K, N), lambda i: (0, 0)),
                  pl.BlockSpec((tm, N), lambda i: (i, 0))],
        out_specs=pl.BlockSpec((tm, N), lambda i: (i, 0)),
        out_shape=jax.ShapeDtypeStruct((T, N), F32),
        compiler_params=_compiler_params(("parallel",)),
        name="matmul_residual",
    )(a_bf16, w_bf16, res)


def _nt_dot(a, b):
    return lax.dot_general(a, b, (((1,), (1,)), ((), ())), preferred_element_type=F32)


def _tn_dot(a, b):
    return lax.dot_general(a, b, (((0,), (0,)), ((), ())), preferred_element_type=F32)


def _gdn_kernel(x_ref, g_ref, w_ref, conv_ref, gate_ref, ogain_ref, o_ref,
                state_ref, xbuf_ref, qkv_ref, proj_ref):
    t = pl.program_id(0)
    nb, rows = x_ref.shape[0], x_ref.shape[1]

    @pl.when(t == 0)
    def _():
        state_ref[...] = jnp.zeros_like(state_ref)
        xbuf_ref[:, 0:SUBLANES, :] = jnp.zeros((nb, SUBLANES, GDN_CONV_DIM), F32)

    for b in range(nb):
        _norm_matmul_kernel(x_ref.at[b], g_ref, w_ref, proj_ref.at[b])

    def chunk(cc, carry):
        r0 = pl.multiple_of(cc * CHUNK, CHUNK)
        _gdn_chunk(t * rows + r0, r0, nb, proj_ref, conv_ref, gate_ref, ogain_ref, o_ref,
                   state_ref, xbuf_ref, qkv_ref)
        return carry

    lax.fori_loop(0, rows // CHUNK, chunk, 0)


def _gdn_chunk(row0, r0, nb, proj_ref, conv_ref, gate_ref, ogain_ref, o_ref, state_ref, xbuf_ref, qkv_ref):
    C = CHUNK
    row = row0 + lax.broadcasted_iota(jnp.int32, (C, 1), 0)
    live = row >= FRONT_PAD
    r_i = lax.broadcasted_iota(jnp.int32, (C, C), 0)
    c_i = lax.broadcasted_iota(jnp.int32, (C, C), 1)
    causal = c_i <= r_i
    strict = c_i < r_i
    neg_rate = -jnp.exp(gate_ref[0:1, :])
    betas, gcs, gc_ts, egs, eg_lasts, eks = [], [], [], [], [], []
    for b in range(nb):
        u = jnp.where(live, proj_ref[b, pl.ds(r0, C), 0:GDN_CONV_DIM], 0.0)
        xbuf_ref[b, SUBLANES:SUBLANES + C, :] = u
        conv = conv_ref[0:1, :] * xbuf_ref[b, SUBLANES - 3:SUBLANES - 3 + C, :]
        for j in range(1, GDN_CONV):
            conv = conv + conv_ref[j:j + 1, :] * xbuf_ref[b, SUBLANES - 3 + j:SUBLANES - 3 + j + C, :]
        xbuf_ref[b, 0:SUBLANES, :] = u[C - SUBLANES:C, :]
        qkv_ref[b] = _silu(conv)

        gates = jnp.where(live, proj_ref[b, pl.ds(r0, C), GDN_GATE_OFF:GDN_GATE_OFF + LANES], 0.0)
        g = neg_rate * jax.nn.softplus(gates + gate_ref[1:2, :])
        gc = jnp.dot(causal.astype(F32), g, preferred_element_type=F32,
                     precision=lax.Precision.HIGHEST)
        g_last = gc[C - 1:C, :]
        betas.append(jax.nn.sigmoid(gates))
        gcs.append(gc)
        gc_ts.append(gc.T)
        egs.append(jnp.exp(gc))
        eg_lasts.append(jnp.exp(g_last))
        eks.append(jnp.exp(g_last - gc))

    chains = [(b, h) for b in range(nb) for h in range(GDN_HEADS)]
    heads = range(len(chains))
    dot = functools.partial(jnp.dot, preferred_element_type=F32)
    col = lambda a, h: a[:, GDN_HEADS + h:GDN_HEADS + h + 1]
    qs, ks, k16s, xs, decays = [], [], [], [], []
    for b, h in chains:
        lo = h * GDN_DK
        q = qkv_ref[b, :, lo:lo + GDN_DK]
        k = qkv_ref[b, :, GDN_QK_DIM + lo:GDN_QK_DIM + lo + GDN_DK]
        v = qkv_ref[b, :, 2 * GDN_QK_DIM + h * GDN_DV:2 * GDN_QK_DIM + (h + 1) * GDN_DV]
        q = q * lax.rsqrt(jnp.sum(q * q, axis=-1, keepdims=True) + L2_EPS) * (GDN_DK ** -0.5)
        k = k * lax.rsqrt(jnp.sum(k * k, axis=-1, keepdims=True) + L2_EPS)
        kb = k * betas[b][:, h:h + 1]
        vb = v * betas[b][:, h:h + 1]
        qs.append(q)
        ks.append(k)
        k16s.append(k.astype(BF16))
        xs.append((kb, jnp.concatenate([vb, kb * col(egs[b], h)], axis=1)))
        decays.append(jnp.exp(jnp.where(causal, col(gcs[b], h) - gc_ts[b][GDN_HEADS + h:GDN_HEADS + h + 1, :],
                                        NEG_INF)))
    kk = [_nt_dot(xs[h][0].astype(BF16), k16s[h]) for h in heads]
    qk = [_nt_dot(qs[h].astype(BF16), k16s[h]) for h in heads]
    ns = [-jnp.where(strict, kk[h] * decays[h], 0.0) for h in heads]
    xs = [x for _, x in xs]
    for it in range(6):
        n16 = [n.astype(BF16) for n in ns]
        xs = [xs[h] + dot(n16[h], xs[h].astype(BF16)) for h in heads]
        if it < 5:
            ns = [dot(n16[h], n16[h]) for h in heads]
    qk = [jnp.where(causal, qk[h] * decays[h], 0.0).astype(BF16) for h in heads]
    states = [state_ref[c] for c in heads]
    s16 = [s.astype(BF16) for s in states]
    ws = [dot(xs[c][:, GDN_DV:2 * GDN_DV].astype(BF16), s16[c]) for c in heads]
    qs_s = [dot((qs[c] * col(egs[b], h)).astype(BF16), s16[c]) for c, (b, h) in enumerate(chains)]
    v_new = [(xs[c][:, 0:GDN_DV] - ws[c]).astype(BF16) for c in heads]
    os_ = [qs_s[c] + dot(qk[c], v_new[c]) for c in heads]
    kv = [_tn_dot((ks[c] * col(eks[b], h)).astype(BF16), v_new[c]) for c, (b, h) in enumerate(chains)]
    for c, (b, h) in enumerate(chains):
        state_ref[c] = states[c] * col(eg_lasts[b], h) + kv[c]
        z = proj_ref[b, pl.ds(r0, C), GDN_CONV_DIM + h * GDN_DV:GDN_CONV_DIM + (h + 1) * GDN_DV]
        o = os_[c]
        on = o * lax.rsqrt(jnp.mean(o * o, axis=-1, keepdims=True) + RMS_EPS) * ogain_ref[...]
        o_ref[b, pl.ds(r0, C), h * GDN_DV:(h + 1) * GDN_DV] = (on * _silu(z)).astype(o_ref.dtype)


def _gdn_core(h3d, gain, w_bf16, conv_w, a_log, dt_bias, o_gain):
    B, LP, D = h3d.shape
    W = w_bf16.shape[1]
    rows = _pick_tile(LP, GDN_ROW_TILES)
    gate = jnp.zeros((2, LANES), F32)
    gate = gate.at[0, GDN_HEADS:2 * GDN_HEADS].set(a_log.astype(F32))
    gate = gate.at[1, GDN_HEADS:2 * GDN_HEADS].set(dt_bias.astype(F32))
    fix = lambda t: (0, 0)
    return pl.pallas_call(
        _gdn_kernel,
        grid=(LP // rows,),
        in_specs=[pl.BlockSpec((B, rows, D), lambda t: (0, t, 0)),
                  pl.BlockSpec((1, D), fix),
                  pl.BlockSpec((D, W), fix, pipeline_mode=pl.Buffered(1)),
                  pl.BlockSpec((GDN_CONV, GDN_CONV_DIM), fix),
                  pl.BlockSpec((2, LANES), fix),
                  pl.BlockSpec((1, GDN_DV), fix)],
        out_specs=pl.BlockSpec((B, rows, GDN_V_DIM), lambda t: (0, t, 0)),
        out_shape=jax.ShapeDtypeStruct((B, LP, GDN_V_DIM), BF16),
        scratch_shapes=[pltpu.VMEM((B * GDN_HEADS, GDN_DK, GDN_DV), F32),
                        pltpu.VMEM((B, SUBLANES + CHUNK, GDN_CONV_DIM), F32),
                        pltpu.VMEM((B, CHUNK, GDN_CONV_DIM), F32),
                        pltpu.VMEM((B, rows, W), F32)],
        compiler_params=_compiler_params(("arbitrary",)),
        name="gdn_core",
    )(h3d, gain.reshape(1, D), w_bf16, conv_w, gate, o_gain.reshape(1, GDN_DV))


def _moe_router_kernel(h_ref, gain_ref, wr_ref, br_ref, hx_ref, gid_ref, rank_ref, cnt_ref, carry_ref):
    i = pl.program_id(0)
    TM = h_ref.shape[0]
    lane = lax.broadcasted_iota(jnp.int32, (1, LANES), 1)

    @pl.when(i == 0)
    def _():
        carry_ref[...] = jnp.zeros_like(carry_ref)

    x = h_ref[...]
    xn = x * lax.rsqrt(jnp.mean(x * x, axis=-1, keepdims=True) + RMS_EPS) * gain_ref[...]
    logits = jnp.dot(xn, wr_ref[...], preferred_element_type=F32,
                     precision=lax.Precision.HIGHEST) + br_ref[...]
    e_log = logits
    g_lane = lane - N_EXPERTS
    g_log = jnp.where((g_lane >= 0) & (g_lane < N_GROUPS), logits, NEG_INF)
    g_max = jnp.max(g_log, axis=-1, keepdims=True)
    g_sel = jnp.min(jnp.where(g_log == g_max, g_lane, LANES), axis=-1, keepdims=True)
    g_sel = jnp.minimum(g_sel, N_GROUPS - 1)
    g_w = 1.0 / jnp.sum(jnp.exp(g_log - g_max), axis=-1, keepdims=True)
    in_grp = (lane >= g_sel * EXPERTS_PER_GROUP) & (lane < (g_sel + 1) * EXPERTS_PER_GROUP)
    l0 = jnp.where(in_grp, e_log, NEG_INF)
    m1 = jnp.max(l0, axis=-1, keepdims=True)
    i1 = jnp.min(jnp.where(l0 == m1, lane, LANES), axis=-1, keepdims=True)
    l1 = jnp.where(lane == i1, NEG_INF, l0)
    m2 = jnp.max(l1, axis=-1, keepdims=True)
    i2 = jnp.min(jnp.where(l1 == m2, lane, LANES), axis=-1, keepdims=True)
    p2 = jnp.exp(m2 - m1)
    w1 = g_w / (1.0 + p2)
    w2 = g_w * p2 / (1.0 + p2)
    j1 = i1 - g_sel * EXPERTS_PER_GROUP
    j2 = i2 - g_sel * EXPERTS_PER_GROUP
    hx_ref[:, 0:xn.shape[1]] = xn
    hx_ref[:, xn.shape[1]:] = jnp.where(lane == j1, w1, 0.0) + jnp.where(lane == j2, w2, 0.0)

    onehot = jnp.where(lane == g_sel, 1.0, 0.0)
    r_i = lax.broadcasted_iota(jnp.int32, (TM, TM), 0)
    c_i = lax.broadcasted_iota(jnp.int32, (TM, TM), 1)
    before = jnp.where(c_i < r_i, 1.0, 0.0).astype(BF16)
    prior = jnp.dot(before, onehot.astype(BF16), preferred_element_type=F32) + carry_ref[...]
    rank = jnp.sum(jnp.where(lane == g_sel, prior, 0.0), axis=-1, keepdims=True)
    carry_ref[...] += jnp.sum(onehot, axis=0, keepdims=True)
    cnt_ref[...] = carry_ref[...]
    gid_ref[0] = jnp.broadcast_to(g_sel.astype(F32), (TM, LANES)).T[0:1, :].astype(jnp.int32)
    rank_ref[0] = jnp.broadcast_to(rank, (TM, LANES)).T[0:1, :].astype(jnp.int32)


def _moe_dispatch_kernel(seg_ref, pos_ref, hx_ref, xs_ref, zero_ref, sem, zero_sem):
    TM = hx_ref.shape[0]

    @pl.when(pl.program_id(0) == 0)
    def _():
        zero_ref[...] = jnp.zeros_like(zero_ref)

        def zero_tile(t):
            fill = pltpu.make_async_copy(zero_ref, xs_ref.at[pl.ds(t * TM, TM), :], zero_sem)
            fill.start()
            fill.wait()

        for g in range(N_GROUPS):
            first = seg_ref[g - 1] if g else 0
            pl.when(seg_ref[g] > first)(functools.partial(zero_tile, seg_ref[g] - 1))
            tail = seg_ref[N_GROUPS - 1] + g
            pl.when(tail < xs_ref.shape[0] // TM)(functools.partial(zero_tile, tail))

    def start(a, carry):
        for b in range(SUBLANES):
            r = a * SUBLANES + b
            pltpu.make_async_copy(hx_ref.at[pl.ds(r, 1), :], xs_ref.at[pl.ds(pos_ref[0, 0, r], 1), :],
                                  sem).start()
        return carry

    lax.fori_loop(0, TM // SUBLANES, start, 0)
    pltpu.make_async_copy(hx_ref, xs_ref.at[pl.ds(0, TM), :], sem).wait()


def _moe_expert_kernel(grp_ref, xs_ref, wgu_ref, wd_ref, ys_ref, xn_ref, acc_ref):
    j = pl.program_id(0)
    e = pl.program_id(1)
    D = xn_ref.shape[1]
    lane = lax.broadcasted_iota(jnp.int32, (1, LANES), 1)

    @pl.when(grp_ref[j] >= 0)
    def _():
        @pl.when(e == 0)
        def _():
            xn_ref[...] = xs_ref[:, 0:D].astype(BF16)
            acc_ref[...] = jnp.zeros_like(acc_ref)

        hh = jnp.dot(xn_ref[...], wgu_ref[0], preferred_element_type=F32)
        cw = jnp.sum(jnp.where(lane == e, xs_ref[:, D:], 0.0), axis=-1, keepdims=True)
        act = _silu(hh[:, 0:EXPERT_FF]) * hh[:, EXPERT_FF:2 * EXPERT_FF] * cw
        acc_ref[...] += jnp.dot(act.astype(BF16), wd_ref[0], preferred_element_type=F32)

    last = e == EXPERTS_PER_GROUP - 1

    @pl.when(last & (grp_ref[j] >= 0))
    def _():
        ys_ref[...] = acc_ref[...]

    @pl.when(last & (grp_ref[j] < 0))
    def _():
        ys_ref[...] = jnp.zeros_like(ys_ref)


def _moe_combine_kernel(pos_ref, h_ref, ys_ref, o_ref, stage_ref, sem):
    TM = h_ref.shape[0]

    def start(a, carry):
        for b in range(SUBLANES):
            r = a * SUBLANES + b
            pltpu.make_async_copy(ys_ref.at[pl.ds(pos_ref[0, 0, r], 1), :], stage_ref.at[pl.ds(r, 1), :],
                                  sem).start()
        return carry

    lax.fori_loop(0, TM // SUBLANES, start, 0)
    pltpu.make_async_copy(ys_ref.at[pl.ds(0, TM), :], stage_ref, sem).wait()
    o_ref[...] = h_ref[...] + stage_ref[...]


def _moe(h2d, gain, w_group, b_group, w_expert, b_expert, wgu, wd, layer, frames=None):
    T, D = h2d.shape
    tr = _pick_tile(T, MOE_ROUTER_TILES)
    tm = _pick_tile(T, MOE_ROW_TILES)
    n_tiles = T // tm + N_GROUPS
    XW = D + LANES
    assert N_EXPERTS + N_GROUPS <= LANES
    wr = jnp.zeros((D, LANES), F32)
    wr = wr.at[:, 0:N_EXPERTS].set(w_expert.astype(F32))
    wr = wr.at[:, N_EXPERTS:N_EXPERTS + N_GROUPS].set(w_group.astype(F32))
    br = jnp.zeros((1, LANES), F32)
    br = br.at[0, 0:N_EXPERTS].set(b_expert.astype(F32))
    br = br.at[0, N_EXPERTS:N_EXPERTS + N_GROUPS].set(b_group.astype(F32))

    hx, gid, rank, counts = pl.pallas_call(
        _moe_router_kernel,
        grid=(T // tr,),
        in_specs=[pl.BlockSpec((tr, D), lambda i: (i, 0)),
                  pl.BlockSpec((1, D), lambda i: (0, 0)),
                  pl.BlockSpec((D, LANES), lambda i: (0, 0)),
                  pl.BlockSpec((1, LANES), lambda i: (0, 0))],
        out_specs=[pl.BlockSpec((tr, XW), lambda i: (i, 0)),
                   pl.BlockSpec((1, 1, tr), lambda i: (i, 0, 0)),
                   pl.BlockSpec((1, 1, tr), lambda i: (i, 0, 0)),
                   pl.BlockSpec((1, LANES), lambda i: (0, 0))],
        out_shape=[jax.ShapeDtypeStruct((T, XW), F32),
                   jax.ShapeDtypeStruct((T // tr, 1, tr), jnp.int32),
                   jax.ShapeDtypeStruct((T // tr, 1, tr), jnp.int32),
                   jax.ShapeDtypeStruct((1, LANES), F32)],
        scratch_shapes=[pltpu.VMEM((1, LANES), F32)],
        compiler_params=_compiler_params(("arbitrary",)),
        name="moe_router",
    )(h2d, gain.reshape(1, D), wr, br)

    seg_tiles = -(-counts[0, 0:N_GROUPS].astype(jnp.int32) // tm)
    seg_end = jnp.cumsum(seg_tiles)
    seg_start = seg_end - seg_tiles
    pos = (seg_start * tm)[gid.reshape(T)] + rank.reshape(T)
    tile_ids = jnp.arange(n_tiles, dtype=jnp.int32)
    tile_group = jnp.sum((tile_ids[:, None] >= seg_end[None, :]).astype(jnp.int32), axis=1)
    tile_group = jnp.where(tile_ids < seg_end[N_GROUPS - 1], tile_group, -1)
    pos3 = pos.reshape(T // tm, 1, tm)

    xs = pl.pallas_call(
        _moe_dispatch_kernel,
        grid_spec=pltpu.PrefetchScalarGridSpec(
            num_scalar_prefetch=1,
            grid=(T // tm,),
            in_specs=[pl.BlockSpec((1, 1, tm), lambda i, seg: (i, 0, 0), memory_space=pltpu.SMEM),
                      pl.BlockSpec((tm, XW), lambda i, seg: (i, 0))],
            out_specs=pl.BlockSpec(memory_space=pl.ANY),
            scratch_shapes=[pltpu.VMEM((tm, XW), F32), pltpu.SemaphoreType.DMA(()),
                            pltpu.SemaphoreType.DMA(())]),
        out_shape=jax.ShapeDtypeStruct((n_tiles * tm, XW), F32),
        compiler_params=_compiler_params(("arbitrary",)),
        name="moe_dispatch",
    )(seg_end, pos3, hx)

    def expert_index(j, e, grp):
        return (layer * N_EXPERTS + jnp.maximum(grp[j], 0) * EXPERTS_PER_GROUP + e, 0, 0)

    ys = pl.pallas_call(
        _moe_expert_kernel,
        grid_spec=pltpu.PrefetchScalarGridSpec(
            num_scalar_prefetch=1,
            grid=(n_tiles, EXPERTS_PER_GROUP),
            in_specs=[pl.BlockSpec((tm, XW), lambda j, e, grp: (j, 0)),
                      pl.BlockSpec((1, D, 2 * EXPERT_FF), expert_index),
                      pl.BlockSpec((1, EXPERT_FF, D), expert_index)],
            out_specs=pl.BlockSpec((tm, D), lambda j, e, grp: (j, 0)),
            scratch_shapes=[pltpu.VMEM((tm, D), BF16), pltpu.VMEM((tm, D), F32)]),
        out_shape=jax.ShapeDtypeStruct((n_tiles * tm, D), F32),
        compiler_params=_compiler_params(("parallel", "arbitrary")),
        name="moe_experts",
    )(tile_group, xs, wgu, wd)

    if frames is None:
        tc, out_rows, out_index = tm, T, lambda i: (i, 0)
    else:
        batch_rows, n_frames, first = frames
        tc = ROW_ALIGN
        assert batch_rows % tc == 0 and n_frames % tc == 0 and first % tc == 0 and first >= tc
        out_rows = (T // batch_rows) * n_frames

        def out_index(i):
            b, t = i // (batch_rows // tc), i % (batch_rows // tc)
            return (b * (n_frames // tc) + jnp.maximum(t - first // tc, 0), 0)

    return pl.pallas_call(
        _moe_combine_kernel,
        grid=(T // tc,),
        in_specs=[pl.BlockSpec((1, 1, tc), lambda i: (i, 0, 0), memory_space=pltpu.SMEM),
                  pl.BlockSpec((tc, D), lambda i: (i, 0)),
                  pl.BlockSpec(memory_space=pl.ANY)],
        out_specs=pl.BlockSpec((tc, D), out_index),
        out_shape=jax.ShapeDtypeStruct((out_rows, D), F32),
        scratch_shapes=[pltpu.VMEM((tc, D), F32), pltpu.SemaphoreType.DMA(())],
        compiler_params=_compiler_params(("arbitrary",)),
        name="moe_combine",
    )(pos.reshape(T // tc, 1, tc), h2d, ys)


def _dsa_prep_kernel(p_ref, qg_ref, kg_ref, lng_ref, lnb_ref, q_ref, kt_ref, v_ref, qit_ref, ki_ref, wit_ref):
    lane = lax.broadcasted_iota(jnp.int32, (1, LANES), 1)
    for h in range(ATT_HEADS):
        x = p_ref[:, DSA_Q_OFF + h * ATT_HD:DSA_Q_OFF + (h + 1) * ATT_HD]
        y = x * lax.rsqrt(jnp.mean(x * x, axis=-1, keepdims=True) + RMS_EPS) * qg_ref[...]
        q_ref[:, h * ATT_HD:(h + 1) * ATT_HD] = (y * (ATT_HD ** -0.5)).astype(BF16)
    for n in range(ATT_KV_HEADS):
        x = p_ref[:, DSA_K_OFF + n * ATT_HD:DSA_K_OFF + (n + 1) * ATT_HD]
        y = x * lax.rsqrt(jnp.mean(x * x, axis=-1, keepdims=True) + RMS_EPS) * kg_ref[...]
        kt_ref[0, n * ATT_HD:(n + 1) * ATT_HD, :] = y.T.astype(BF16)
    v_ref[...] = p_ref[:, DSA_V_OFF:DSA_V_OFF + ATT_KV_DIM].astype(BF16)
    qit_ref[0] = p_ref[:, DSA_QI_OFF:DSA_QI_OFF + IDX_HEADS * LANES].T.astype(BF16)
    x = p_ref[:, DSA_KI_OFF:DSA_KI_OFF + LANES]
    live = lane < IDX_HD
    mu = jnp.sum(x, axis=-1, keepdims=True) * (1.0 / IDX_HD)
    xc = jnp.where(live, x - mu, 0.0)
    var = jnp.sum(xc * xc, axis=-1, keepdims=True) * (1.0 / IDX_HD)
    ki = xc * lax.rsqrt(var + RMS_EPS) * lng_ref[...] + lnb_ref[...]
    ki_ref[...] = jnp.where(live, ki, 0.0).astype(BF16)
    wit_ref[0] = (p_ref[:, DSA_WI_OFF:DSA_WI_OFF + LANES] * (IDX_HEADS ** -0.5 * IDX_HD ** -0.5)).T


def _dsa_in_kernel(x_ref, g_ref, w_ref, qg_ref, kg_ref, lng_ref, lnb_ref,
                   q_ref, kt_ref, v_ref, qit_ref, ki_ref, wit_ref, proj_ref):
    _norm_matmul_kernel(x_ref, g_ref, w_ref, proj_ref)
    _dsa_prep_kernel(proj_ref, qg_ref, kg_ref, lng_ref, lnb_ref, q_ref, kt_ref, v_ref, qit_ref, ki_ref, wit_ref)


def _dsa_in(h2d, gain, w_bf16, q_gain, k_gain, ln_g, ln_b):
    T, D = h2d.shape
    W = w_bf16.shape[1]
    tm = KEY_TILE
    assert T % tm == 0 and ATT_TILE == KEY_TILE
    lng = jnp.zeros((1, LANES), F32).at[0, 0:IDX_HD].set(ln_g.astype(F32))
    lnb = jnp.zeros((1, LANES), F32).at[0, 0:IDX_HD].set(ln_b.astype(F32))
    row = lambda i: (i, 0)
    fix = lambda i: (0, 0)
    blk = lambda i: (i, 0, 0)
    return pl.pallas_call(
        _dsa_in_kernel,
        grid=(T // tm,),
        in_specs=[pl.BlockSpec((tm, D), row), pl.BlockSpec((1, D), fix), pl.BlockSpec((D, W), fix),
                  pl.BlockSpec((1, ATT_HD), fix), pl.BlockSpec((1, ATT_HD), fix),
                  pl.BlockSpec((1, LANES), fix), pl.BlockSpec((1, LANES), fix)],
        out_specs=[pl.BlockSpec((tm, ATT_Q_DIM), row), pl.BlockSpec((1, ATT_KV_DIM, tm), blk),
                   pl.BlockSpec((tm, ATT_KV_DIM), row), pl.BlockSpec((1, IDX_HEADS * LANES, tm), blk),
                   pl.BlockSpec((tm, LANES), row), pl.BlockSpec((1, LANES, tm), blk)],
        out_shape=[jax.ShapeDtypeStruct((T, ATT_Q_DIM), BF16), jax.ShapeDtypeStruct((T // tm, ATT_KV_DIM, tm), BF16),
                   jax.ShapeDtypeStruct((T, ATT_KV_DIM), BF16),
                   jax.ShapeDtypeStruct((T // tm, IDX_HEADS * LANES, tm), BF16),
                   jax.ShapeDtypeStruct((T, LANES), BF16), jax.ShapeDtypeStruct((T // tm, LANES, tm), F32)],
        scratch_shapes=[pltpu.VMEM((tm, W), F32)],
        compiler_params=_compiler_params(("parallel",)),
        name="dsa_in",
    )(h2d, gain.reshape(1, D), w_bf16, q_gain.reshape(1, ATT_HD), k_gain.reshape(1, ATT_HD), lng, lnb)


def _dsa_kernel(q_ref, qit_ref, wit_ref, kt_ref, v_ref, ki_ref, bias_ref, bmax_ref, o_ref,
                sc_ref, mask_ref, qg_ref, m_ref, l_ref, acc_ref, knorm_ref, *, n_select, live_rows):
    i = pl.program_id(1)
    TQ = ATT_TILE
    TK = KEY_TILE
    HALVES = TK // LANES

    @pl.when(i == 0)
    def _():
        def body(kt, mx):
            kk = kt_ref[0, kt].astype(F32)
            return tuple(jnp.maximum(mx[n], jnp.sum(jnp.square(kk[n * ATT_HD:(n + 1) * ATT_HD, :]),
                                                    axis=0, keepdims=True)) for n in range(ATT_KV_HEADS))

        mx = lax.fori_loop(0, kt_ref.shape[1], body,
                           tuple(jnp.zeros((1, TK), F32) for _ in range(ATT_KV_HEADS)))
        for n in range(ATT_KV_HEADS):
            knorm_ref[n] = jnp.broadcast_to(jnp.sqrt(jnp.max(mx[n], axis=-1, keepdims=True)), (1, LANES))

    @pl.when(i * TQ >= live_rows)
    def _():
        o_ref[...] = jnp.zeros_like(o_ref)

    @pl.when(i * TQ < live_rows)
    def _():
        _dsa_tile(i, q_ref, qit_ref, wit_ref, kt_ref, v_ref, ki_ref, bias_ref, bmax_ref, o_ref,
                  sc_ref, mask_ref, qg_ref, m_ref, l_ref, acc_ref, knorm_ref, n_select, TQ, TK, HALVES)


def _dsa_tile(i, q_ref, qit_ref, wit_ref, kt_ref, v_ref, ki_ref, bias_ref, bmax_ref, o_ref,
              sc_ref, mask_ref, qg_ref, m_ref, l_ref, acc_ref, knorm_ref, n_select, TQ, TK, HALVES):
    assert TQ == TK
    nkt = i + 1
    fmin = float(jnp.finfo(F32).min)
    kf = float(n_select)

    chunk_shift = int(math.log2(CHUNK))
    q_pos = i * TQ + lax.broadcasted_iota(jnp.int32, (1, TQ), 1)
    q_chunk = jnp.maximum(lax.shift_right_logical(q_pos, chunk_shift), FRONT_PAD // CHUNK)
    k_row = lax.broadcasted_iota(jnp.int32, (TK, 1), 0)

    def halves(x):
        return [x[:, j * LANES:(j + 1) * LANES] for j in range(HALVES)]

    def fold(x, op):
        parts = [x[r * SUBLANES:(r + 1) * SUBLANES, :] for r in range(TK // SUBLANES)]
        while len(parts) > 1:
            parts = [op(parts[j], parts[j + 1]) for j in range(0, len(parts), 2)]
        return parts[0]

    def score_tiles(kts):
        for kt in kts:
            keys = ki_ref[0, pl.ds(pl.multiple_of(kt * TK, TK), TK), :]
            acc = None
            for h in range(IDX_HEADS):
                d = jnp.dot(keys, qit_ref[0, 0, h * LANES:(h + 1) * LANES, :], preferred_element_type=F32)
                term = wit_ref[0, 0, h:h + 1, :] * jnp.maximum(d, 0.0)
                acc = term if acc is None else acc + term
            sc_ref[kt] = acc
            mask_ref[kt] = acc.T

    def tile_loop(n, fn):
        def body(kt, carry):
            fn([kt])
            return carry

        lax.fori_loop(0, n, body, 0)

    def paired_loop(n, fn):
        def body(j, carry):
            fn([2 * j, 2 * j + 1])
            return carry

        lax.fori_loop(0, lax.shift_right_logical(n, 1), body, 0)

        @pl.when(lax.bitwise_and(n, 1) == 1)
        def _():
            fn([n - 1])

    paired_loop(nkt, score_tiles)

    def mask_inadmissible(kt):
        k_pos = kt * TK + k_row
        adm = (lax.shift_right_logical(k_pos, chunk_shift) <= q_chunk) & (k_pos >= FRONT_PAD)
        sc_ref[kt] = jnp.where(adm, sc_ref[kt], NEG_INF)

    mask_inadmissible(0)

    @pl.when(nkt > 1)
    def _():
        mask_inadmissible(nkt - 1)

    n_adm = CHUNK * (q_chunk + 1) - FRONT_PAD
    need = n_adm > n_select

    def count_where(pred):
        def one(kt):
            return fold(jnp.where(pred(sc_ref[kt], kt), 1.0, 0.0), jnp.add)

        def two(j, acc):
            return acc + (one(2 * j) + one(2 * j + 1))

        acc = lax.fori_loop(0, lax.shift_right_logical(nkt, 1), two, jnp.zeros((SUBLANES, TQ), F32))
        acc = lax.cond(lax.bitwise_and(nkt, 1) == 1, lambda a: a + one(nkt - 1), lambda a: a, acc)
        return jnp.sum(acc, axis=0, keepdims=True)

    def count_ge(t):
        return count_where(lambda s, kt: s >= t)

    def minmax_body(kt, carry):
        mn, mx = carry
        s = sc_ref[kt]
        mn = jnp.minimum(mn, fold(jnp.where(s == NEG_INF, jnp.inf, s), jnp.minimum))
        mx = jnp.maximum(mx, fold(s, jnp.maximum))
        return mn, mx

    mn_l, mx_l = lax.fori_loop(0, nkt, minmax_body,
                               (jnp.full((SUBLANES, TQ), jnp.inf, F32), jnp.full((SUBLANES, TQ), NEG_INF, F32)))
    s_min = jnp.min(mn_l, axis=0, keepdims=True)
    s_max = jnp.max(mx_l, axis=0, keepdims=True)
    lo0 = s_min
    hi0 = s_max + (s_max - s_min) + 1.0
    cnt0 = n_adm.astype(F32)
    active0 = jnp.where(need, 1.0, 0.0)

    def any_active(active):
        return jnp.max(active) > 0.0

    def bis_step(lo, hi, cnt, active):
        mid = lo + (hi - lo) * 0.5
        c = count_ge(mid)
        ge = c >= kf
        moving = (mid > lo) & (mid < hi) & (active > 0.0)
        lo_n = jnp.where(moving & ge, mid, lo)
        hi_n = jnp.where(moving & (~ge), mid, hi)
        cnt_n = jnp.where(moving & ge, c, cnt)
        active_n = jnp.where(moving & (cnt_n != kf), 1.0, 0.0)
        return lo_n, hi_n, cnt_n, active_n

    def bis_body(st):
        it, rest = st[0], st[1:]
        for _ in range(BISECT_UNROLL):
            rest = bis_step(*rest)
        return (it + BISECT_UNROLL,) + rest

    _, lo, hi, cnt, _ = lax.while_loop(
        lambda st: jnp.logical_and(st[0] < BISECT_STEPS, any_active(st[4])), bis_body,
        (jnp.int32(0), lo0, hi0, cnt0, active0))

    def snap_body(st):
        lo, hi, cnt, active = st

        def body(kt, mx):
            s = sc_ref[kt]
            return jnp.maximum(mx, fold(jnp.where(s < hi, s, NEG_INF), jnp.maximum))

        v1 = jnp.max(lax.fori_loop(0, nkt, body, jnp.full((SUBLANES, TQ), NEG_INF, F32)),
                     axis=0, keepdims=True)
        c1 = count_ge(v1)
        on = active > 0.0
        hit = on & (c1 >= kf)
        lo_n = jnp.where(hit, v1, lo)
        cnt_n = jnp.where(hit, c1, cnt)
        hi_n = jnp.where(on & (~hit), v1, hi)
        return lo_n, hi_n, cnt_n, jnp.where(on & (~hit), 1.0, 0.0)

    unsettled = jnp.where(need & (cnt != kf), 1.0, 0.0)
    lo, hi, cnt, _ = lax.while_loop(lambda st: any_active(st[3]), snap_body, (lo, hi, cnt, unsettled))

    thr = jnp.where(need, lo, fmin)
    tied = need & (cnt > kf)
    big_pos = jnp.int32(2 ** 30)

    def tie_break():
        want = kf - count_where(lambda s, kt: s > thr)

        def body(_, st):
            jl, jh = st
            jm = lax.shift_right_arithmetic(jl + jh, 1)
            c = count_where(lambda s, kt: (s == thr) & (kt * TK + k_row <= jm))
            ok = c >= want
            return jnp.where(ok, jl, jm), jnp.where(ok, jm, jh)

        n_keys = sc_ref.shape[0] * TK
        steps = int(math.ceil(math.log2(n_keys + 1))) + 1
        jl0 = jnp.full((1, TQ), -1, jnp.int32)
        jh0 = jnp.full((1, TQ), n_keys, jnp.int32)
        _, jh = lax.fori_loop(0, steps, body, (jl0, jh0))
        return jnp.where(tied, jh, big_pos)

    pos_cap = lax.cond(jnp.max(jnp.where(tied, 1.0, 0.0)) > 0.0, tie_break,
                       lambda: jnp.full((1, TQ), big_pos, jnp.int32))

    def lane_wide(row):
        return jnp.broadcast_to(row, (LANES, TQ)).T

    thr_w = lane_wide(thr)
    cap_w = lane_wide(pos_cap)
    chunk_w = lane_wide(q_chunk)
    k_lane = lax.broadcasted_iota(jnp.int32, (1, LANES), 1)

    def to_mask(kt, edge_tile):
        out = []
        for j, s in enumerate(halves(mask_ref[kt])):
            k_pos = kt * TK + j * LANES + k_lane
            sel = (s > thr_w) | ((s == thr_w) & (k_pos <= cap_w))
            if edge_tile:
                sel = sel & (lax.shift_right_logical(k_pos, chunk_shift) <= chunk_w) & (k_pos >= FRONT_PAD)
            out.append(jnp.where(sel, 0.0, NEG_INF))
        mask_ref[kt] = jnp.concatenate(out, axis=1)

    def inner_mask(kt, carry):
        to_mask(kt, False)
        return carry

    lax.fori_loop(1, nkt - 1, inner_mask, 0)
    to_mask(0, True)

    @pl.when(nkt > 1)
    def _():
        to_mask(nkt - 1, True)

    for n in range(ATT_KV_HEADS):
        for g in range(ATT_GROUP):
            h = n * ATT_GROUP + g
            qg_ref[n, g * TQ:(g + 1) * TQ, :] = q_ref[0, :, h * ATT_HD:(h + 1) * ATT_HD]
    l_ref[...] = jnp.zeros_like(l_ref)
    acc_ref[...] = jnp.zeros_like(acc_ref)

    kv_heads = range(ATT_KV_HEADS)

    def raw_logits(kts):
        return [[jnp.dot(qg_ref[n], kt_ref[0, kt, n * ATT_HD:(n + 1) * ATT_HD, :], preferred_element_type=F32)
                 for n in kv_heads] for kt in kts]

    def masked_logits(s, mask, n, g, b0):
        sg = s[g * TQ:(g + 1) * TQ, :] + mask
        if b0 is not None:
            h = n * ATT_GROUP + g
            sg = sg + jnp.concatenate([bias_ref[h, b0 + j] for j in range(HALVES)], axis=1)
        return sg

    def max_pass(kts, b0):
        raw = raw_logits(kts)
        masks = [mask_ref[kt] for kt in kts]
        for n in kv_heads:
            for g in range(ATT_GROUP):
                rows = slice(g * TQ, (g + 1) * TQ)
                m = m_ref[n, rows, :]
                for t in range(len(kts)):
                    for part in halves(masked_logits(raw[t][n], masks[t], n, g, b0)):
                        m = jnp.maximum(m, part)
                m_ref[n, rows, :] = m

    def exp_pass(kts, b0):
        raw = raw_logits(kts)
        masks = [mask_ref[kt] for kt in kts]
        for n in kv_heads:
            ps = [[] for _ in kts]
            for g in range(ATT_GROUP):
                rows = slice(g * TQ, (g + 1) * TQ)
                m = m_ref[n, rows, :]
                l_new = l_ref[n, rows, :]
                for t in range(len(kts)):
                    parts = [jnp.exp(part - m) for part in halves(masked_logits(raw[t][n], masks[t], n, g, b0))]
                    l_new = l_new + functools.reduce(lambda a, b: a + b, parts)
                    ps[t].append(jnp.concatenate(parts, axis=1).astype(BF16))
                l_ref[n, rows, :] = l_new
            pv = None
            for t, kt in enumerate(kts):
                vv = v_ref[0, pl.ds(pl.multiple_of(kt * TK, TK), TK), n * ATT_HD:(n + 1) * ATT_HD]
                d = jnp.dot(jnp.concatenate(ps[t], axis=0), vv, preferred_element_type=F32)
                pv = d if pv is None else pv + d
            acc_ref[n] += pv

    def run_pass(tile_fn):
        paired_loop(jnp.maximum(i - 1, 0), lambda kts: tile_fn(kts, None))

        @pl.when(i >= 1)
        def _():
            tile_fn([i - 1], 0)

        tile_fn([i], HALVES)

    for n in range(ATT_KV_HEADS):
        for g in range(ATT_GROUP):
            h = n * ATT_GROUP + g
            qh = qg_ref[n, g * TQ:(g + 1) * TQ, :].astype(F32)
            q_len = jnp.sqrt(jnp.sum(qh * qh, axis=-1, keepdims=True))
            bound = q_len * knorm_ref[n] * SHIFT_SLACK + (bmax_ref[h] + SHIFT_SLACK - 1.0)
            m_ref[n, g * TQ:(g + 1) * TQ, :] = jnp.broadcast_to(bound, (TQ, LANES))
    run_pass(exp_pass)

    def smallest_row_sum():
        return functools.reduce(jnp.minimum, [jnp.min(jnp.sum(l_ref[n], axis=-1, keepdims=True))
                                              for n in range(ATT_KV_HEADS)])

    @pl.when(jnp.logical_not(smallest_row_sum() > ROW_SUM_FLOOR))
    def _():
        m_ref[...] = jnp.full(m_ref.shape, -1e30, F32)
        run_pass(max_pass)
        for n in range(ATT_KV_HEADS):
            m_ref[n] = jnp.broadcast_to(jnp.max(m_ref[n], axis=-1, keepdims=True), m_ref.shape[1:])
        l_ref[...] = jnp.zeros_like(l_ref)
        acc_ref[...] = jnp.zeros_like(acc_ref)
        run_pass(exp_pass)

    for n in range(ATT_KV_HEADS):
        inv_l = 1.0 / jnp.sum(l_ref[n], axis=-1, keepdims=True)
        for g in range(ATT_GROUP):
            h = n * ATT_GROUP + g
            rows = slice(g * TQ, (g + 1) * TQ)
            o_ref[0, :, h * ATT_HD:(h + 1) * ATT_HD] = (acc_ref[n, rows, :] * inv_l[rows, :]).astype(o_ref.dtype)


def _t5_bucket(rel):
    nb = REL_BUCKETS // 2
    max_exact = nb // 2
    ret = jnp.where(rel > 0, nb, 0)
    n = jnp.abs(rel)
    nf = jnp.maximum(n, 1).astype(jnp.float32)
    large = max_exact + (jnp.log(nf / max_exact) / math.log(REL_MAX_DIST / max_exact)
                         * (nb - max_exact)).astype(jnp.int32)
    large = jnp.minimum(large, nb - 1)
    return ret + jnp.where(n < max_exact, n, large)


def _dsa_core(q, kt, v, qit, ki, wit, rel_bias, n_select, live_rows):
    B, LP, _ = q.shape
    nq = LP // ATT_TILE
    nkt = LP // KEY_TILE
    a = jnp.arange(ATT_TILE)[:, None]
    xx = jnp.arange(2 * KEY_TILE)[None, :]
    rel = (xx - KEY_TILE) - a
    far = rel_bias.astype(F32)[REL_BUCKETS // 2 - 1]
    onehot = jax.nn.one_hot(_t5_bucket(rel), REL_BUCKETS, dtype=F32)
    table = jnp.einsum("abk,kh->abh", onehot, rel_bias.astype(F32),
                       precision=lax.Precision.HIGHEST) - far
    bias_near = jnp.transpose(table.reshape(ATT_TILE, 2 * KEY_TILE // LANES, LANES, ATT_HEADS), (3, 1, 0, 2))
    bias_max = jnp.broadcast_to(jnp.maximum(jnp.max(table, axis=(0, 1)), 0.0)[:, None, None],
                                (ATT_HEADS, 1, LANES))
    kern = functools.partial(_dsa_kernel, n_select=n_select, live_rows=live_rows)
    tile = lambda b, i: (b, i, 0)
    tile4 = lambda b, i: (b, i, 0, 0)
    whole3 = lambda b, i: (b, 0, 0)
    whole4 = lambda b, i: (b, 0, 0, 0)
    once = pl.Buffered(1)
    return pl.pallas_call(
        kern,
        grid=(B, nq),
        in_specs=[pl.BlockSpec((1, ATT_TILE, ATT_Q_DIM), tile),
                  pl.BlockSpec((1, 1, IDX_HEADS * LANES, ATT_TILE), tile4),
                  pl.BlockSpec((1, 1, LANES, ATT_TILE), tile4),
                  pl.BlockSpec((1, nkt, ATT_KV_DIM, KEY_TILE), whole4, pipeline_mode=once),
                  pl.BlockSpec((1, LP, ATT_KV_DIM), whole3, pipeline_mode=once),
                  pl.BlockSpec((1, LP, LANES), whole3, pipeline_mode=once),
                  pl.BlockSpec((ATT_HEADS, 2 * KEY_TILE // LANES, ATT_TILE, LANES), lambda b, i: (0, 0, 0, 0),
                               pipeline_mode=once),
                  pl.BlockSpec((ATT_HEADS, 1, LANES), lambda b, i: (0, 0, 0))],
        out_specs=pl.BlockSpec((1, ATT_TILE, ATT_Q_DIM), tile),
        out_shape=jax.ShapeDtypeStruct((B, LP, ATT_Q_DIM), BF16),
        scratch_shapes=[pltpu.VMEM((nkt, KEY_TILE, ATT_TILE), F32),
                        pltpu.VMEM((nkt, ATT_TILE, KEY_TILE), F32),
                        pltpu.VMEM((ATT_KV_HEADS, ATT_GROUP * ATT_TILE, ATT_HD), BF16),
                        pltpu.VMEM((ATT_KV_HEADS, ATT_GROUP * ATT_TILE, LANES), F32),
                        pltpu.VMEM((ATT_KV_HEADS, ATT_GROUP * ATT_TILE, LANES), F32),
                        pltpu.VMEM((ATT_KV_HEADS, ATT_GROUP * ATT_TILE, ATT_HD), F32),
                        pltpu.VMEM((ATT_KV_HEADS, 1, LANES), F32)],
        compiler_params=_compiler_params(("arbitrary", "arbitrary")),
        name="dsa_core",
    )(q, qit, wit, kt, v, ki, bias_near, bias_max)


def _gdn_in_weight(w_in):
    D = w_in.shape[0]
    w = jnp.zeros((D, GDN_PROJ_W), F32).at[:, 0:w_in.shape[1]].set(w_in.astype(F32))
    return w.astype(BF16)


def _dsa_in_weight(w_in):
    D = w_in.shape[0]
    w_in = w_in.astype(BF16)
    zeros = lambda n: jnp.zeros((D, n), BF16)
    src = ATT_Q_DIM + 2 * ATT_KV_DIM
    parts = [w_in[:, 0:src]]
    for h in range(IDX_HEADS):
        parts += [w_in[:, src + h * IDX_HD:src + (h + 1) * IDX_HD], zeros(LANES - IDX_HD)]
    src += IDX_HEADS * IDX_HD
    parts += [w_in[:, src:src + IDX_HD], zeros(LANES - IDX_HD),
              w_in[:, src + IDX_HD:src + IDX_HD + IDX_HEADS], zeros(LANES - IDX_HEADS)]
    w = jnp.concatenate(parts, axis=1)
    assert w.shape[1] == DSA_PROJ_W
    return w


def kernel(x, meta_tokens, norm_mix, norm_ffn, rel_bias, gdn_w_in, gdn_conv, gdn_a_log, gdn_dt_bias,
           gdn_o_norm, gdn_w_out, dsa_w_in, dsa_q_norm, dsa_k_norm, dsa_idx_ln_g, dsa_idx_ln_b, dsa_w_out,
           moe_w_group, moe_b_group, moe_w_expert, moe_b_expert, moe_w_gate_up, moe_w_down):
    B, S, D = x.shape
    depth = norm_mix.shape[0]
    n_select = min(TOPK_MAX, S // 4)
    off = FRONT_PAD + N_META
    LP = -(-(off + S) // ROW_ALIGN) * ROW_ALIGN
    meta = jnp.broadcast_to(meta_tokens.astype(x.dtype)[None], (B, N_META, D))
    h = jnp.concatenate([jnp.zeros((B, FRONT_PAD, D), x.dtype), meta, x,
                         jnp.zeros((B, LP - off - S, D), x.dtype)], axis=1)
    h = h.reshape(B * LP, D)
    wgu_all = moe_w_gate_up.astype(BF16).reshape(depth * N_EXPERTS, D, 2 * EXPERT_FF)
    wd_all = moe_w_down.astype(BF16).reshape(depth * N_EXPERTS, EXPERT_FF, D)
    for i in range(depth):
        j = i // 2
        if i % 2 == 0:
            o = _gdn_core(h.reshape(B, LP, D), norm_mix[i], _gdn_in_weight(gdn_w_in[j]),
                          gdn_conv[j].astype(F32), gdn_a_log[j], gdn_dt_bias[j], gdn_o_norm[j])
            h = _matmul_residual(o.reshape(B * LP, GDN_V_DIM), gdn_w_out[j].astype(BF16), h)
        else:
            q, kt, v, qit, ki, wit = _dsa_in(h, norm_mix[i], _dsa_in_weight(dsa_w_in[j]), dsa_q_norm[j],
                                             dsa_k_norm[j], dsa_idx_ln_g[j], dsa_idx_ln_b[j])
            r3 = lambda t: t.reshape(B, LP, t.shape[-1])
            r4 = lambda t: t.reshape(B, LP // KEY_TILE, t.shape[-2], KEY_TILE)
            o = _dsa_core(r3(q), r4(kt), r3(v), r4(qit), r3(ki), r4(wit), rel_bias, n_select, off + S)
            h = _matmul_residual(o.reshape(B * LP, ATT_Q_DIM), dsa_w_out[j].astype(BF16), h)
        h = _moe(h, norm_ffn[i], moe_w_group[i], moe_b_group[i], moe_w_expert[i], moe_b_expert[i],
                 wgu_all, wd_all, i, frames=(LP, S, off) if i == depth - 1 else None)
    return h.reshape(B, S, D)
```

```python
import functools
import math

import jax
import jax.numpy as jnp
from jax import lax
from jax.experimental import pallas as pl
from jax.experimental.pallas import tpu as pltpu

F32 = jnp.float32
BF16 = jnp.bfloat16

LANES = 128
SUBLANES = 8
VMEM_LIMIT_BYTES = 56 * 1024 * 1024

CHUNK = 64
N_META = 16
META_PAD = (-N_META) % CHUNK
RMS_EPS = 1e-6
L2_EPS = 1e-6

GDN_HEADS = 8
GDN_DK = 128
GDN_DV = 128
GDN_CONV = 4
GDN_QK_DIM = GDN_HEADS * GDN_DK
GDN_V_DIM = GDN_HEADS * GDN_DV
GDN_CONV_DIM = 2 * GDN_QK_DIM + GDN_V_DIM
GDN_GATE_OFF = GDN_CONV_DIM + GDN_V_DIM
GDN_PROJ_W = GDN_GATE_OFF + LANES

ATT_HEADS = 8
ATT_KV_HEADS = 2
ATT_GROUP = ATT_HEADS // ATT_KV_HEADS
ATT_HD = 128
ATT_Q_DIM = ATT_HEADS * ATT_HD
ATT_KV_DIM = ATT_KV_HEADS * ATT_HD
IDX_HEADS = 8
IDX_HD = 64
TOPK_MAX = 256
REL_BUCKETS = 32
REL_MAX_DIST = 128

DSA_Q_OFF = 0
DSA_K_OFF = ATT_Q_DIM
DSA_V_OFF = DSA_K_OFF + ATT_KV_DIM
DSA_QI_OFF = DSA_V_OFF + ATT_KV_DIM
DSA_KI_OFF = DSA_QI_OFF + IDX_HEADS * LANES
DSA_WI_OFF = DSA_KI_OFF + LANES
DSA_PROJ_W = DSA_WI_OFF + LANES

N_GROUPS = 4
EXPERTS_PER_GROUP = 8
N_EXPERTS = N_GROUPS * EXPERTS_PER_GROUP
EXPERT_FF = 256

GDN_ROW_TILES = (256, 128, 64)
MOE_ROUTER_TILES = (512, 256, 128)
MOE_ROW_TILES = (512, 256, 128)

ATT_TILE = 256
KEY_TILE = 256
ROW_ALIGN = KEY_TILE
INDEXER_GROUP = 4
ATTEND_GROUP = 2
FRONT_PAD = ROW_ALIGN - N_META
assert FRONT_PAD % CHUNK == META_PAD
BISECT_UNROLL = 4
BISECT_STEPS = 20
SHIFT_SLACK = 1.001
ROW_SUM_FLOOR = 1e-30
NEG_INF = float("-inf")


def _pick_tile(n, candidates):
    for c in candidates:
        if n % c == 0:
            return c
    raise ValueError(f"no tile for {n}")


def _compiler_params(semantics):
    return pltpu.CompilerParams(dimension_semantics=semantics, vmem_limit_bytes=VMEM_LIMIT_BYTES)


def _silu(x):
    return x * jax.nn.sigmoid(x)


def _norm_matmul_kernel(x_ref, g_ref, w_ref, o_ref):
    x = x_ref[...]
    y = x * lax.rsqrt(jnp.mean(x * x, axis=-1, keepdims=True) + RMS_EPS) * g_ref[...]
    o_ref[...] = jnp.dot(y.astype(BF16), w_ref[...], preferred_element_type=F32)


def _matmul_residual_kernel(a_ref, w_ref, r_ref, o_ref):
    o_ref[...] = r_ref[...] + jnp.dot(a_ref[...], w_ref[...], preferred_element_type=F32)


def _matmul_residual(a_bf16, w_bf16, res):
    T, K = a_bf16.shape
    N = w_bf16.shape[1]
    tm = _pick_tile(T, (512, 256, 128))
    return pl.pallas_call(
        _matmul_residual_kernel,
        grid=(T // tm,),
        in_specs=[pl.BlockSpec((tm, K), lambda i: (i, 0)),
                  pl.BlockSpec((K, N), lambda i: (0, 0)),
                  pl.BlockSpec((tm, N), lambda i: (i, 0))],
        out_specs=pl.BlockSpec((tm, N), lambda i: (i, 0)),
        out_shape=jax.ShapeDtypeStruct((T, N), F32),
        compiler_params=_compiler_params(("parallel",)),
        name="matmul_residual",
    )(a_bf16, w_bf16, res)


def _nt_dot(a, b):
    return lax.dot_general(a, b, (((1,), (1,)), ((), ())), preferred_element_type=F32)


def _tn_dot(a, b):
    return lax.dot_general(a, b, (((0,), (0,)), ((), ())), preferred_element_type=F32)


def _gdn_kernel(x_ref, g_ref, w_ref, conv_ref, gate_ref, ogain_ref, o_ref,
                state_ref, xbuf_ref, qkv_ref, proj_ref):
    t = pl.program_id(0)
    nb, rows = x_ref.shape[0], x_ref.shape[1]

    @pl.when(t == 0)
    def _():
        state_ref[...] = jnp.zeros_like(state_ref)
        xbuf_ref[:, 0:SUBLANES, :] = jnp.zeros((nb, SUBLANES, GDN_CONV_DIM), F32)

    for b in range(nb):
        _norm_matmul_kernel(x_ref.at[b], g_ref, w_ref, proj_ref.at[b])

    def chunk(cc, carry):
        r0 = pl.multiple_of(cc * CHUNK, CHUNK)
        _gdn_chunk(t * rows + r0, r0, nb, proj_ref, conv_ref, gate_ref, ogain_ref, o_ref,
                   state_ref, xbuf_ref, qkv_ref)
        return carry

    lax.fori_loop(0, rows // CHUNK, chunk, 0)


def _gdn_chunk(row0, r0, nb, proj_ref, conv_ref, gate_ref, ogain_ref, o_ref, state_ref, xbuf_ref, qkv_ref):
    C = CHUNK
    row = row0 + lax.broadcasted_iota(jnp.int32, (C, 1), 0)
    live = row >= FRONT_PAD
    r_i = lax.broadcasted_iota(jnp.int32, (C, C), 0)
    c_i = lax.broadcasted_iota(jnp.int32, (C, C), 1)
    causal = c_i <= r_i
    strict = c_i < r_i
    neg_rate = -jnp.exp(gate_ref[0:1, :])
    betas, gcs, gc_ts, egs, eg_lasts, eks = [], [], [], [], [], []
    for b in range(nb):
        u = jnp.where(live, proj_ref[b, pl.ds(r0, C), 0:GDN_CONV_DIM], 0.0)
        xbuf_ref[b, SUBLANES:SUBLANES + C, :] = u
        conv = conv_ref[0:1, :] * xbuf_ref[b, SUBLANES - 3:SUBLANES - 3 + C, :]
        for j in range(1, GDN_CONV):
            conv = conv + conv_ref[j:j + 1, :] * xbuf_ref[b, SUBLANES - 3 + j:SUBLANES - 3 + j + C, :]
        xbuf_ref[b, 0:SUBLANES, :] = u[C - SUBLANES:C, :]
        qkv_ref[b] = _silu(conv)

        gates = jnp.where(live, proj_ref[b, pl.ds(r0, C), GDN_GATE_OFF:GDN_GATE_OFF + LANES], 0.0)
        g = neg_rate * jax.nn.softplus(gates + gate_ref[1:2, :])
        gc = jnp.dot(causal.astype(F32), g, preferred_element_type=F32,
                     precision=lax.Precision.HIGHEST)
        g_last = gc[C - 1:C, :]
        betas.append(jax.nn.sigmoid(gates))
        gcs.append(gc)
        gc_ts.append(gc.T)
        egs.append(jnp.exp(gc))
        eg_lasts.append(jnp.exp(g_last))
        eks.append(jnp.exp(g_last - gc))

    chains = [(b, h) for b in range(nb) for h in range(GDN_HEADS)]
    heads = range(len(chains))
    dot = functools.partial(jnp.dot, preferred_element_type=F32)
    col = lambda a, h: a[:, GDN_HEADS + h:GDN_HEADS + h + 1]
    qs, ks, k16s, xs, decays = [], [], [], [], []
    for b, h in chains:
        lo = h * GDN_DK
        q = qkv_ref[b, :, lo:lo + GDN_DK]
        k = qkv_ref[b, :, GDN_QK_DIM + lo:GDN_QK_DIM + lo + GDN_DK]
        v = qkv_ref[b, :, 2 * GDN_QK_DIM + h * GDN_DV:2 * GDN_QK_DIM + (h + 1) * GDN_DV]
        q = q * lax.rsqrt(jnp.sum(q * q, axis=-1, keepdims=True) + L2_EPS) * (GDN_DK ** -0.5)
        k = k * lax.rsqrt(jnp.sum(k * k, axis=-1, keepdims=True) + L2_EPS)
        kb = k * betas[b][:, h:h + 1]
        vb = v * betas[b][:, h:h + 1]
        qs.append(q)
        ks.append(k)
        k16s.append(k.astype(BF16))
        xs.append((kb, jnp.concatenate([vb, kb * col(egs[b], h)], axis=1)))
        decays.append(jnp.exp(jnp.where(causal, col(gcs[b], h) - gc_ts[b][GDN_HEADS + h:GDN_HEADS + h + 1, :],
                                        NEG_INF)))
    kk = [_nt_dot(xs[h][0].astype(BF16), k16s[h]) for h in heads]
    qk = [_nt_dot(qs[h].astype(BF16), k16s[h]) for h in heads]
    ns = [-jnp.where(strict, kk[h] * decays[h], 0.0) for h in heads]
    xs = [x for _, x in xs]
    for it in range(6):
        n16 = [n.astype(BF16) for n in ns]
        xs = [xs[h] + dot(n16[h], xs[h].astype(BF16)) for h in heads]
        if it < 5:
            ns = [dot(n16[h], n16[h]) for h in heads]
    qk = [jnp.where(causal, qk[h] * decays[h], 0.0).astype(BF16) for h in heads]
    states = [state_ref[c] for c in heads]
    s16 = [s.astype(BF16) for s in states]
    ws = [dot(xs[c][:, GDN_DV:2 * GDN_DV].astype(BF16), s16[c]) for c in heads]
    qs_s = [dot((qs[c] * col(egs[b], h)).astype(BF16), s16[c]) for c, (b, h) in enumerate(chains)]
    v_new = [(xs[c][:, 0:GDN_DV] - ws[c]).astype(BF16) for c in heads]
    os_ = [qs_s[c] + dot(qk[c], v_new[c]) for c in heads]
    kv = [_tn_dot((ks[c] * col(eks[b], h)).astype(BF16), v_new[c]) for c, (b, h) in enumerate(chains)]
    for c, (b, h) in enumerate(chains):
        state_ref[c] = states[c] * col(eg_lasts[b], h) + kv[c]
        z = proj_ref[b, pl.ds(r0, C), GDN_CONV_DIM + h * GDN_DV:GDN_CONV_DIM + (h + 1) * GDN_DV]
        o = os_[c]
        on = o * lax.rsqrt(jnp.mean(o * o, axis=-1, keepdims=True) + RMS_EPS) * ogain_ref[...]
        o_ref[b, pl.ds(r0, C), h * GDN_DV:(h + 1) * GDN_DV] = (on * _silu(z)).astype(o_ref.dtype)


def _gdn_core(h3d, gain, w_bf16, conv_w, a_log, dt_bias, o_gain):
    B, LP, D = h3d.shape
    W = w_bf16.shape[1]
    rows = _pick_tile(LP, GDN_ROW_TILES)
    gate = jnp.zeros((2, LANES), F32)
    gate = gate.at[0, GDN_HEADS:2 * GDN_HEADS].set(a_log.astype(F32))
    gate = gate.at[1, GDN_HEADS:2 * GDN_HEADS].set(dt_bias.astype(F32))
    fix = lambda t: (0, 0)
    return pl.pallas_call(
        _gdn_kernel,
        grid=(LP // rows,),
        in_specs=[pl.BlockSpec((B, rows, D), lambda t: (0, t, 0)),
                  pl.BlockSpec((1, D), fix),
                  pl.BlockSpec((D, W), fix, pipeline_mode=pl.Buffered(1)),
                  pl.BlockSpec((GDN_CONV, GDN_CONV_DIM), fix),
                  pl.BlockSpec((2, LANES), fix),
                  pl.BlockSpec((1, GDN_DV), fix)],
        out_specs=pl.BlockSpec((B, rows, GDN_V_DIM), lambda t: (0, t, 0)),
        out_shape=jax.ShapeDtypeStruct((B, LP, GDN_V_DIM), BF16),
        scratch_shapes=[pltpu.VMEM((B * GDN_HEADS, GDN_DK, GDN_DV), F32),
                        pltpu.VMEM((B, SUBLANES + CHUNK, GDN_CONV_DIM), F32),
                        pltpu.VMEM((B, CHUNK, GDN_CONV_DIM), F32),
                        pltpu.VMEM((B, rows, W), F32)],
        compiler_params=_compiler_params(("arbitrary",)),
        name="gdn_core",
    )(h3d, gain.reshape(1, D), w_bf16, conv_w, gate, o_gain.reshape(1, GDN_DV))


def _moe_router_kernel(h_ref, gain_ref, wr_ref, br_ref, hx_ref, gid_ref, rank_ref, cnt_ref, carry_ref):
    i = pl.program_id(0)
    TM = h_ref.shape[0]
    lane = lax.broadcasted_iota(jnp.int32, (1, LANES), 1)

    @pl.when(i == 0)
    def _():
        carry_ref[...] = jnp.zeros_like(carry_ref)

    x = h_ref[...]
    xn = x * lax.rsqrt(jnp.mean(x * x, axis=-1, keepdims=True) + RMS_EPS) * gain_ref[...]
    logits = jnp.dot(xn, wr_ref[...], preferred_element_type=F32,
                     precision=lax.Precision.HIGHEST) + br_ref[...]
    e_log = logits
    g_lane = lane - N_EXPERTS
    g_log = jnp.where((g_lane >= 0) & (g_lane < N_GROUPS), logits, NEG_INF)
    g_max = jnp.max(g_log, axis=-1, keepdims=True)
    g_sel = jnp.min(jnp.where(g_log == g_max, g_lane, LANES), axis=-1, keepdims=True)
    g_sel = jnp.minimum(g_sel, N_GROUPS - 1)
    g_w = 1.0 / jnp.sum(jnp.exp(g_log - g_max), axis=-1, keepdims=True)
    in_grp = (lane >= g_sel * EXPERTS_PER_GROUP) & (lane < (g_sel + 1) * EXPERTS_PER_GROUP)
    l0 = jnp.where(in_grp, e_log, NEG_INF)
    m1 = jnp.max(l0, axis=-1, keepdims=True)
    i1 = jnp.min(jnp.where(l0 == m1, lane, LANES), axis=-1, keepdims=True)
    l1 = jnp.where(lane == i1, NEG_INF, l0)
    m2 = jnp.max(l1, axis=-1, keepdims=True)
    i2 = jnp.min(jnp.where(l1 == m2, lane, LANES), axis=-1, keepdims=True)
    p2 = jnp.exp(m2 - m1)
    w1 = g_w / (1.0 + p2)
    w2 = g_w * p2 / (1.0 + p2)
    j1 = i1 - g_sel * EXPERTS_PER_GROUP
    j2 = i2 - g_sel * EXPERTS_PER_GROUP
    hx_ref[:, 0:xn.shape[1]] = xn
    hx_ref[:, xn.shape[1]:] = jnp.where(lane == j1, w1, 0.0) + jnp.where(lane == j2, w2, 0.0)

    onehot = jnp.where(lane == g_sel, 1.0, 0.0)
    r_i = lax.broadcasted_iota(jnp.int32, (TM, TM), 0)
    c_i = lax.broadcasted_iota(jnp.int32, (TM, TM), 1)
    before = jnp.where(c_i < r_i, 1.0, 0.0).astype(BF16)
    prior = jnp.dot(before, onehot.astype(BF16), preferred_element_type=F32) + carry_ref[...]
    rank = jnp.sum(jnp.where(lane == g_sel, prior, 0.0), axis=-1, keepdims=True)
    carry_ref[...] += jnp.sum(onehot, axis=0, keepdims=True)
    cnt_ref[...] = carry_ref[...]
    gid_ref[0] = jnp.broadcast_to(g_sel.astype(F32), (TM, LANES)).T[0:1, :].astype(jnp.int32)
    rank_ref[0] = jnp.broadcast_to(rank, (TM, LANES)).T[0:1, :].astype(jnp.int32)


def _moe_dispatch_kernel(seg_ref, pos_ref, hx_ref, xs_ref, zero_ref, sem, zero_sem):
    TM = hx_ref.shape[0]

    @pl.when(pl.program_id(0) == 0)
    def _():
        zero_ref[...] = jnp.zeros_like(zero_ref)

        def zero_tile(t):
            fill = pltpu.make_async_copy(zero_ref, xs_ref.at[pl.ds(t * TM, TM), :], zero_sem)
            fill.start()
            fill.wait()

        for g in range(N_GROUPS):
            first = seg_ref[g - 1] if g else 0
            pl.when(seg_ref[g] > first)(functools.partial(zero_tile, seg_ref[g] - 1))
            tail = seg_ref[N_GROUPS - 1] + g
            pl.when(tail < xs_ref.shape[0] // TM)(functools.partial(zero_tile, tail))

    def start(a, carry):
        for b in range(SUBLANES):
            r = a * SUBLANES + b
            pltpu.make_async_copy(hx_ref.at[pl.ds(r, 1), :], xs_ref.at[pl.ds(pos_ref[0, 0, r], 1), :],
                                  sem).start()
        return carry

    lax.fori_loop(0, TM // SUBLANES, start, 0)
    pltpu.make_async_copy(hx_ref, xs_ref.at[pl.ds(0, TM), :], sem).wait()


def _moe_expert_kernel(grp_ref, xs_ref, wgu_ref, wd_ref, ys_ref, xn_ref, acc_ref):
    j = pl.program_id(0)
    e = pl.program_id(1)
    D = xn_ref.shape[1]
    lane = lax.broadcasted_iota(jnp.int32, (1, LANES), 1)

    @pl.when(grp_ref[j] >= 0)
    def _():
        @pl.when(e == 0)
        def _():
            xn_ref[...] = xs_ref[:, 0:D].astype(BF16)
            acc_ref[...] = jnp.zeros_like(acc_ref)

        hh = jnp.dot(xn_ref[...], wgu_ref[0], preferred_element_type=F32)
        cw = jnp.sum(jnp.where(lane == e, xs_ref[:, D:], 0.0), axis=-1, keepdims=True)
        act = _silu(hh[:, 0:EXPERT_FF]) * hh[:, EXPERT_FF:2 * EXPERT_FF] * cw
        acc_ref[...] += jnp.dot(act.astype(BF16), wd_ref[0], preferred_element_type=F32)

    last = e == EXPERTS_PER_GROUP - 1

    @pl.when(last & (grp_ref[j] >= 0))
    def _():
        ys_ref[...] = acc_ref[...]

    @pl.when(last & (grp_ref[j] < 0))
    def _():
        ys_ref[...] = jnp.zeros_like(ys_ref)


def _moe_combine_kernel(pos_ref, h_ref, ys_ref, o_ref, stage_ref, sem):
    TM = h_ref.shape[0]

    def start(a, carry):
        for b in range(SUBLANES):
            r = a * SUBLANES + b
            pltpu.make_async_copy(ys_ref.at[pl.ds(pos_ref[0, 0, r], 1), :], stage_ref.at[pl.ds(r, 1), :],
                                  sem).start()
        return carry

    lax.fori_loop(0, TM // SUBLANES, start, 0)
    pltpu.make_async_copy(ys_ref.at[pl.ds(0, TM), :], stage_ref, sem).wait()
    o_ref[...] = h_ref[...] + stage_ref[...]


def _moe(h2d, gain, w_group, b_group, w_expert, b_expert, wgu, wd, layer, frames=None):
    T, D = h2d.shape
    tr = _pick_tile(T, MOE_ROUTER_TILES)
    tm = _pick_tile(T, MOE_ROW_TILES)
    n_tiles = T // tm + N_GROUPS
    XW = D + LANES
    assert N_EXPERTS + N_GROUPS <= LANES
    wr = jnp.zeros((D, LANES), F32)
    wr = wr.at[:, 0:N_EXPERTS].set(w_expert.astype(F32))
    wr = wr.at[:, N_EXPERTS:N_EXPERTS + N_GROUPS].set(w_group.astype(F32))
    br = jnp.zeros((1, LANES), F32)
    br = br.at[0, 0:N_EXPERTS].set(b_expert.astype(F32))
    br = br.at[0, N_EXPERTS:N_EXPERTS + N_GROUPS].set(b_group.astype(F32))

    hx, gid, rank, counts = pl.pallas_call(
        _moe_router_kernel,
        grid=(T // tr,),
        in_specs=[pl.BlockSpec((tr, D), lambda i: (i, 0)),
                  pl.BlockSpec((1, D), lambda i: (0, 0)),
                  pl.BlockSpec((D, LANES), lambda i: (0, 0)),
                  pl.BlockSpec((1, LANES), lambda i: (0, 0))],
        out_specs=[pl.BlockSpec((tr, XW), lambda i: (i, 0)),
                   pl.BlockSpec((1, 1, tr), lambda i: (i, 0, 0)),
                   pl.BlockSpec((1, 1, tr), lambda i: (i, 0, 0)),
                   pl.BlockSpec((1, LANES), lambda i: (0, 0))],
        out_shape=[jax.ShapeDtypeStruct((T, XW), F32),
                   jax.ShapeDtypeStruct((T // tr, 1, tr), jnp.int32),
                   jax.ShapeDtypeStruct((T // tr, 1, tr), jnp.int32),
                   jax.ShapeDtypeStruct((1, LANES), F32)],
        scratch_shapes=[pltpu.VMEM((1, LANES), F32)],
        compiler_params=_compiler_params(("arbitrary",)),
        name="moe_router",
    )(h2d, gain.reshape(1, D), wr, br)

    seg_tiles = -(-counts[0, 0:N_GROUPS].astype(jnp.int32) // tm)
    seg_end = jnp.cumsum(seg_tiles)
    seg_start = seg_end - seg_tiles
    pos = (seg_start * tm)[gid.reshape(T)] + rank.reshape(T)
    tile_ids = jnp.arange(n_tiles, dtype=jnp.int32)
    tile_group = jnp.sum((tile_ids[:, None] >= seg_end[None, :]).astype(jnp.int32), axis=1)
    tile_group = jnp.where(tile_ids < seg_end[N_GROUPS - 1], tile_group, -1)
    pos3 = pos.reshape(T // tm, 1, tm)

    xs = pl.pallas_call(
        _moe_dispatch_kernel,
        grid_spec=pltpu.PrefetchScalarGridSpec(
            num_scalar_prefetch=1,
            grid=(T // tm,),
            in_specs=[pl.BlockSpec((1, 1, tm), lambda i, seg: (i, 0, 0), memory_space=pltpu.SMEM),
                      pl.BlockSpec((tm, XW), lambda i, seg: (i, 0))],
            out_specs=pl.BlockSpec(memory_space=pl.ANY),
            scratch_shapes=[pltpu.VMEM((tm, XW), F32), pltpu.SemaphoreType.DMA(()),
                            pltpu.SemaphoreType.DMA(())]),
        out_shape=jax.ShapeDtypeStruct((n_tiles * tm, XW), F32),
        compiler_params=_compiler_params(("arbitrary",)),
        name="moe_dispatch",
    )(seg_end, pos3, hx)

    def expert_index(j, e, grp):
        return (layer * N_EXPERTS + jnp.maximum(grp[j], 0) * EXPERTS_PER_GROUP + e, 0, 0)

    ys = pl.pallas_call(
        _moe_expert_kernel,
        grid_spec=pltpu.PrefetchScalarGridSpec(
            num_scalar_prefetch=1,
            grid=(n_tiles, EXPERTS_PER_GROUP),
            in_specs=[pl.BlockSpec((tm, XW), lambda j, e, grp: (j, 0)),
                      pl.BlockSpec((1, D, 2 * EXPERT_FF), expert_index),
                      pl.BlockSpec((1, EXPERT_FF, D), expert_index)],
            out_specs=pl.BlockSpec((tm, D), lambda j, e, grp: (j, 0)),
            scratch_shapes=[pltpu.VMEM((tm, D), BF16), pltpu.VMEM((tm, D), F32)]),
        out_shape=jax.ShapeDtypeStruct((n_tiles * tm, D), F32),
        compiler_params=_compiler_params(("parallel", "arbitrary")),
        name="moe_experts",
    )(tile_group, xs, wgu, wd)

    if frames is None:
        tc, out_rows, out_index = tm, T, lambda i: (i, 0)
    else:
        batch_rows, n_frames, first = frames
        tc = ROW_ALIGN
        assert batch_rows % tc == 0 and n_frames % tc == 0 and first % tc == 0 and first >= tc
        out_rows = (T // batch_rows) * n_frames

        def out_index(i):
            b, t = i // (batch_rows // tc), i % (batch_rows // tc)
            return (b * (n_frames // tc) + jnp.maximum(t - first // tc, 0), 0)

    return pl.pallas_call(
        _moe_combine_kernel,
        grid=(T // tc,),
        in_specs=[pl.BlockSpec((1, 1, tc), lambda i: (i, 0, 0), memory_space=pltpu.SMEM),
                  pl.BlockSpec((tc, D), lambda i: (i, 0)),
                  pl.BlockSpec(memory_space=pl.ANY)],
        out_specs=pl.BlockSpec((tc, D), out_index),
        out_shape=jax.ShapeDtypeStruct((out_rows, D), F32),
        scratch_shapes=[pltpu.VMEM((tc, D), F32), pltpu.SemaphoreType.DMA(())],
        compiler_params=_compiler_params(("arbitrary",)),
        name="moe_combine",
    )(pos.reshape(T // tc, 1, tc), h2d, ys)


def _dsa_prep_kernel(p_ref, qg_ref, kg_ref, lng_ref, lnb_ref, q_ref, kt_ref, v_ref, qit_ref, ki_ref, wit_ref):
    lane = lax.broadcasted_iota(jnp.int32, (1, LANES), 1)
    for h in range(ATT_HEADS):
        x = p_ref[:, DSA_Q_OFF + h * ATT_HD:DSA_Q_OFF + (h + 1) * ATT_HD]
        y = x * lax.rsqrt(jnp.mean(x * x, axis=-1, keepdims=True) + RMS_EPS) * qg_ref[...]
        q_ref[:, h * ATT_HD:(h + 1) * ATT_HD] = (y * (ATT_HD ** -0.5)).astype(BF16)
    for n in range(ATT_KV_HEADS):
        x = p_ref[:, DSA_K_OFF + n * ATT_HD:DSA_K_OFF + (n + 1) * ATT_HD]
        y = x * lax.rsqrt(jnp.mean(x * x, axis=-1, keepdims=True) + RMS_EPS) * kg_ref[...]
        kt_ref[0, n * ATT_HD:(n + 1) * ATT_HD, :] = y.T.astype(BF16)
    v_ref[...] = p_ref[:, DSA_V_OFF:DSA_V_OFF + ATT_KV_DIM].astype(BF16)
    qit_ref[0] = p_ref[:, DSA_QI_OFF:DSA_QI_OFF + IDX_HEADS * LANES].T.astype(BF16)
    x = p_ref[:, DSA_KI_OFF:DSA_KI_OFF + LANES]
    live = lane < IDX_HD
    mu = jnp.sum(x, axis=-1, keepdims=True) * (1.0 / IDX_HD)
    xc = jnp.where(live, x - mu, 0.0)
    var = jnp.sum(xc * xc, axis=-1, keepdims=True) * (1.0 / IDX_HD)
    ki = xc * lax.rsqrt(var + RMS_EPS) * lng_ref[...] + lnb_ref[...]
    ki_ref[...] = jnp.where(live, ki, 0.0).astype(BF16)
    wit_ref[0] = (p_ref[:, DSA_WI_OFF:DSA_WI_OFF + LANES] * (IDX_HEADS ** -0.5 * IDX_HD ** -0.5)).T


def _dsa_in_kernel(x_ref, g_ref, w_ref, qg_ref, kg_ref, lng_ref, lnb_ref,
                   q_ref, kt_ref, v_ref, qit_ref, ki_ref, wit_ref, proj_ref):
    _norm_matmul_kernel(x_ref, g_ref, w_ref, proj_ref)
    _dsa_prep_kernel(proj_ref, qg_ref, kg_ref, lng_ref, lnb_ref, q_ref, kt_ref, v_ref, qit_ref, ki_ref, wit_ref)


def _dsa_in(h2d, gain, w_bf16, q_gain, k_gain, ln_g, ln_b):
    T, D = h2d.shape
    W = w_bf16.shape[1]
    tm = KEY_TILE
    assert T % tm == 0 and ATT_TILE == KEY_TILE
    lng = jnp.zeros((1, LANES), F32).at[0, 0:IDX_HD].set(ln_g.astype(F32))
    lnb = jnp.zeros((1, LANES), F32).at[0, 0:IDX_HD].set(ln_b.astype(F32))
    row = lambda i: (i, 0)
    fix = lambda i: (0, 0)
    blk = lambda i: (i, 0, 0)
    return pl.pallas_call(
        _dsa_in_kernel,
        grid=(T // tm,),
        in_specs=[pl.BlockSpec((tm, D), row), pl.BlockSpec((1, D), fix), pl.BlockSpec((D, W), fix),
                  pl.BlockSpec((1, ATT_HD), fix), pl.BlockSpec((1, ATT_HD), fix),
                  pl.BlockSpec((1, LANES), fix), pl.BlockSpec((1, LANES), fix)],
        out_specs=[pl.BlockSpec((tm, ATT_Q_DIM), row), pl.BlockSpec((1, ATT_KV_DIM, tm), blk),
                   pl.BlockSpec((tm, ATT_KV_DIM), row), pl.BlockSpec((1, IDX_HEADS * LANES, tm), blk),
                   pl.BlockSpec((tm, LANES), row), pl.BlockSpec((1, LANES, tm), blk)],
        out_shape=[jax.ShapeDtypeStruct((T, ATT_Q_DIM), BF16), jax.ShapeDtypeStruct((T // tm, ATT_KV_DIM, tm), BF16),
                   jax.ShapeDtypeStruct((T, ATT_KV_DIM), BF16),
                   jax.ShapeDtypeStruct((T // tm, IDX_HEADS * LANES, tm), BF16),
                   jax.ShapeDtypeStruct((T, LANES), BF16), jax.ShapeDtypeStruct((T // tm, LANES, tm), F32)],
        scratch_shapes=[pltpu.VMEM((tm, W), F32)],
        compiler_params=_compiler_params(("parallel",)),
        name="dsa_in",
    )(h2d, gain.reshape(1, D), w_bf16, q_gain.reshape(1, ATT_HD), k_gain.reshape(1, ATT_HD), lng, lnb)


def _dsa_kernel(q_ref, qit_ref, wit_ref, kt_ref, v_ref, ki_ref, bias_ref, bmax_ref, o_ref,
                sc_ref, mask_ref, qg_ref, m_ref, l_ref, acc_ref, knorm_ref, *, n_select, live_rows):
    i = pl.program_id(1)
    TQ = ATT_TILE
    TK = KEY_TILE
    HALVES = TK // LANES

    @pl.when(i == 0)
    def _():
        def body(kt, mx):
            kk = kt_ref[0, kt].astype(F32)
            return tuple(jnp.maximum(mx[n], jnp.sum(jnp.square(kk[n * ATT_HD:(n + 1) * ATT_HD, :]),
                                                    axis=0, keepdims=True)) for n in range(ATT_KV_HEADS))

        mx = lax.fori_loop(0, kt_ref.shape[1], body,
                           tuple(jnp.zeros((1, TK), F32) for _ in range(ATT_KV_HEADS)))
        for n in range(ATT_KV_HEADS):
            knorm_ref[n] = jnp.broadcast_to(jnp.sqrt(jnp.max(mx[n], axis=-1, keepdims=True)), (1, LANES))

    @pl.when(i * TQ >= live_rows)
    def _():
        o_ref[...] = jnp.zeros_like(o_ref)

    @pl.when(i * TQ < live_rows)
    def _():
        _dsa_tile(i, q_ref, qit_ref, wit_ref, kt_ref, v_ref, ki_ref, bias_ref, bmax_ref, o_ref,
                  sc_ref, mask_ref, qg_ref, m_ref, l_ref, acc_ref, knorm_ref, n_select, TQ, TK, HALVES)


def _dsa_tile(i, q_ref, qit_ref, wit_ref, kt_ref, v_ref, ki_ref, bias_ref, bmax_ref, o_ref,
              sc_ref, mask_ref, qg_ref, m_ref, l_ref, acc_ref, knorm_ref, n_select, TQ, TK, HALVES):
    assert TQ == TK
    nkt = i + 1
    fmin = float(jnp.finfo(F32).min)
    kf = float(n_select)

    chunk_shift = int(math.log2(CHUNK))
    q_pos = i * TQ + lax.broadcasted_iota(jnp.int32, (1, TQ), 1)
    q_chunk = jnp.maximum(lax.shift_right_logical(q_pos, chunk_shift), FRONT_PAD // CHUNK)
    k_row = lax.broadcasted_iota(jnp.int32, (TK, 1), 0)

    def halves(x):
        return [x[:, j * LANES:(j + 1) * LANES] for j in range(HALVES)]

    def fold(x, op):
        parts = [x[r * SUBLANES:(r + 1) * SUBLANES, :] for r in range(TK // SUBLANES)]
        while len(parts) > 1:
            parts = [op(parts[j], parts[j + 1]) for j in range(0, len(parts), 2)]
        return parts[0]

    def score_tiles(kts):
        for kt in kts:
            keys = ki_ref[0, pl.ds(pl.multiple_of(kt * TK, TK), TK), :]
            acc = None
            for h in range(IDX_HEADS):
                d = jnp.dot(keys, qit_ref[0, 0, h * LANES:(h + 1) * LANES, :], preferred_element_type=F32)
                term = wit_ref[0, 0, h:h + 1, :] * jnp.maximum(d, 0.0)
                acc = term if acc is None else acc + term
            sc_ref[kt] = acc
            mask_ref[kt] = acc.T

    def grouped_loop(n, fn, group):
        def body(j, carry):
            fn([group * j + r for r in range(group)])
            return carry

        lax.fori_loop(0, lax.shift_right_logical(n, int(math.log2(group))), body, 0)
        size = group // 2
        while size >= 1:
            first = lax.bitwise_and(n, ~(2 * size - 1))

            @pl.when(lax.bitwise_and(n, size) != 0)
            def _(size=size, first=first):
                fn([first + r for r in range(size)])

            size //= 2

    grouped_loop(nkt, score_tiles, INDEXER_GROUP)

    def mask_inadmissible(kt):
        k_pos = kt * TK + k_row
        adm = (lax.shift_right_logical(k_pos, chunk_shift) <= q_chunk) & (k_pos >= FRONT_PAD)
        sc_ref[kt] = jnp.where(adm, sc_ref[kt], NEG_INF)

    mask_inadmissible(0)

    @pl.when(nkt > 1)
    def _():
        mask_inadmissible(nkt - 1)

    n_adm = CHUNK * (q_chunk + 1) - FRONT_PAD
    need = n_adm > n_select

    def count_where(pred):
        def one(kt):
            return fold(jnp.where(pred(sc_ref[kt], kt), 1.0, 0.0), jnp.add)

        def two(j, acc):
            return acc + (one(2 * j) + one(2 * j + 1))

        acc = lax.fori_loop(0, lax.shift_right_logical(nkt, 1), two, jnp.zeros((SUBLANES, TQ), F32))
        acc = lax.cond(lax.bitwise_and(nkt, 1) == 1, lambda a: a + one(nkt - 1), lambda a: a, acc)
        return jnp.sum(acc, axis=0, keepdims=True)

    def count_ge(t):
        return count_where(lambda s, kt: s >= t)

    def minmax_body(kt, carry):
        mn, mx = carry
        s = sc_ref[kt]
        mn = jnp.minimum(mn, fold(jnp.where(s == NEG_INF, jnp.inf, s), jnp.minimum))
        mx = jnp.maximum(mx, fold(s, jnp.maximum))
        return mn, mx

    mn_l, mx_l = lax.fori_loop(0, nkt, minmax_body,
                               (jnp.full((SUBLANES, TQ), jnp.inf, F32), jnp.full((SUBLANES, TQ), NEG_INF, F32)))
    s_min = jnp.min(mn_l, axis=0, keepdims=True)
    s_max = jnp.max(mx_l, axis=0, keepdims=True)
    lo0 = s_min
    hi0 = s_max + (s_max - s_min) + 1.0
    cnt0 = n_adm.astype(F32)
    active0 = jnp.where(need, 1.0, 0.0)

    def any_active(active):
        return jnp.max(active) > 0.0

    def bis_step(lo, hi, cnt, active):
        mid = lo + (hi - lo) * 0.5
        c = count_ge(mid)
        ge = c >= kf
        moving = (mid > lo) & (mid < hi) & (active > 0.0)
        lo_n = jnp.where(moving & ge, mid, lo)
        hi_n = jnp.where(moving & (~ge), mid, hi)
        cnt_n = jnp.where(moving & ge, c, cnt)
        active_n = jnp.where(moving & (cnt_n != kf), 1.0, 0.0)
        return lo_n, hi_n, cnt_n, active_n

    def bis_body(st):
        it, rest = st[0], st[1:]
        for _ in range(BISECT_UNROLL):
            rest = bis_step(*rest)
        return (it + BISECT_UNROLL,) + rest

    _, lo, hi, cnt, _ = lax.while_loop(
        lambda st: jnp.logical_and(st[0] < BISECT_STEPS, any_active(st[4])), bis_body,
        (jnp.int32(0), lo0, hi0, cnt0, active0))

    def snap_body(st):
        lo, hi, cnt, active = st

        def body(kt, mx):
            s = sc_ref[kt]
            return jnp.maximum(mx, fold(jnp.where(s < hi, s, NEG_INF), jnp.maximum))

        v1 = jnp.max(lax.fori_loop(0, nkt, body, jnp.full((SUBLANES, TQ), NEG_INF, F32)),
                     axis=0, keepdims=True)
        c1 = count_ge(v1)
        on = active > 0.0
        hit = on & (c1 >= kf)
        lo_n = jnp.where(hit, v1, lo)
        cnt_n = jnp.where(hit, c1, cnt)
        hi_n = jnp.where(on & (~hit), v1, hi)
        return lo_n, hi_n, cnt_n, jnp.where(on & (~hit), 1.0, 0.0)

    unsettled = jnp.where(need & (cnt != kf), 1.0, 0.0)
    lo, hi, cnt, _ = lax.while_loop(lambda st: any_active(st[3]), snap_body, (lo, hi, cnt, unsettled))

    thr = jnp.where(need, lo, fmin)
    tied = need & (cnt > kf)
    big_pos = jnp.int32(2 ** 30)

    def tie_break():
        want = kf - count_where(lambda s, kt: s > thr)

        def body(_, st):
            jl, jh = st
            jm = lax.shift_right_arithmetic(jl + jh, 1)
            c = count_where(lambda s, kt: (s == thr) & (kt * TK + k_row <= jm))
            ok = c >= want
            return jnp.where(ok, jl, jm), jnp.where(ok, jm, jh)

        n_keys = sc_ref.shape[0] * TK
        steps = int(math.ceil(math.log2(n_keys + 1))) + 1
        jl0 = jnp.full((1, TQ), -1, jnp.int32)
        jh0 = jnp.full((1, TQ), n_keys, jnp.int32)
        _, jh = lax.fori_loop(0, steps, body, (jl0, jh0))
        return jnp.where(tied, jh, big_pos)

    pos_cap = lax.cond(jnp.max(jnp.where(tied, 1.0, 0.0)) > 0.0, tie_break,
                       lambda: jnp.full((1, TQ), big_pos, jnp.int32))

    def lane_wide(row):
        return jnp.broadcast_to(row, (LANES, TQ)).T

    thr_w = lane_wide(thr)
    cap_w = lane_wide(pos_cap)
    chunk_w = lane_wide(q_chunk)
    k_lane = lax.broadcasted_iota(jnp.int32, (1, LANES), 1)

    def to_mask(kt, edge_tile):
        out = []
        for j, s in enumerate(halves(mask_ref[kt])):
            k_pos = kt * TK + j * LANES + k_lane
            sel = (s > thr_w) | ((s == thr_w) & (k_pos <= cap_w))
            if edge_tile:
                sel = sel & (lax.shift_right_logical(k_pos, chunk_shift) <= chunk_w) & (k_pos >= FRONT_PAD)
            out.append(jnp.where(sel, 0.0, NEG_INF))
        mask_ref[kt] = jnp.concatenate(out, axis=1)

    def inner_mask(kt, carry):
        to_mask(kt, False)
        return carry

    lax.fori_loop(1, nkt - 1, inner_mask, 0)
    to_mask(0, True)

    @pl.when(nkt > 1)
    def _():
        to_mask(nkt - 1, True)

    for n in range(ATT_KV_HEADS):
        for g in range(ATT_GROUP):
            h = n * ATT_GROUP + g
            qg_ref[n, g * TQ:(g + 1) * TQ, :] = q_ref[0, :, h * ATT_HD:(h + 1) * ATT_HD]
    l_ref[...] = jnp.zeros_like(l_ref)
    acc_ref[...] = jnp.zeros_like(acc_ref)

    kv_heads = range(ATT_KV_HEADS)

    def raw_logits(kts):
        return [[jnp.dot(qg_ref[n], kt_ref[0, kt, n * ATT_HD:(n + 1) * ATT_HD, :], preferred_element_type=F32)
                 for n in kv_heads] for kt in kts]

    def masked_logits(s, mask, n, g, b0):
        sg = s[g * TQ:(g + 1) * TQ, :] + mask
        if b0 is not None:
            h = n * ATT_GROUP + g
            sg = sg + jnp.concatenate([bias_ref[h, b0 + j] for j in range(HALVES)], axis=1)
        return sg

    def max_pass(kts, b0):
        raw = raw_logits(kts)
        masks = [mask_ref[kt] for kt in kts]
        for n in kv_heads:
            for g in range(ATT_GROUP):
                rows = slice(g * TQ, (g + 1) * TQ)
                m = m_ref[n, rows, :]
                for t in range(len(kts)):
                    for part in halves(masked_logits(raw[t][n], masks[t], n, g, b0)):
                        m = jnp.maximum(m, part)
                m_ref[n, rows, :] = m

    def exp_pass(kts, b0):
        raw = raw_logits(kts)
        masks = [mask_ref[kt] for kt in kts]
        for n in kv_heads:
            ps = [[] for _ in kts]
            for g in range(ATT_GROUP):
                rows = slice(g * TQ, (g + 1) * TQ)
                m = m_ref[n, rows, :]
                l_new = l_ref[n, rows, :]
                for t in range(len(kts)):
                    parts = [jnp.exp(part - m) for part in halves(masked_logits(raw[t][n], masks[t], n, g, b0))]
                    l_new = l_new + functools.reduce(lambda a, b: a + b, parts)
                    ps[t].append(jnp.concatenate(parts, axis=1).astype(BF16))
                l_ref[n, rows, :] = l_new
            pv = None
            for t, kt in enumerate(kts):
                vv = v_ref[0, pl.ds(pl.multiple_of(kt * TK, TK), TK), n * ATT_HD:(n + 1) * ATT_HD]
                d = jnp.dot(jnp.concatenate(ps[t], axis=0), vv, preferred_element_type=F32)
                pv = d if pv is None else pv + d
            acc_ref[n] += pv

    def run_pass(tile_fn):
        grouped_loop(jnp.maximum(i - 1, 0), lambda kts: tile_fn(kts, None), ATTEND_GROUP)

        @pl.when(i >= 1)
        def _():
            tile_fn([i - 1], 0)

        tile_fn([i], HALVES)

    for n in range(ATT_KV_HEADS):
        for g in range(ATT_GROUP):
            h = n * ATT_GROUP + g
            qh = qg_ref[n, g * TQ:(g + 1) * TQ, :].astype(F32)
            q_len = jnp.sqrt(jnp.sum(qh * qh, axis=-1, keepdims=True))
            bound = q_len * knorm_ref[n] * SHIFT_SLACK + (bmax_ref[h] + SHIFT_SLACK - 1.0)
            m_ref[n, g * TQ:(g + 1) * TQ, :] = jnp.broadcast_to(bound, (TQ, LANES))
    run_pass(exp_pass)

    def smallest_row_sum():
        return functools.reduce(jnp.minimum, [jnp.min(jnp.sum(l_ref[n], axis=-1, keepdims=True))
                                              for n in range(ATT_KV_HEADS)])

    @pl.when(jnp.logical_not(smallest_row_sum() > ROW_SUM_FLOOR))
    def _():
        m_ref[...] = jnp.full(m_ref.shape, -1e30, F32)
        run_pass(max_pass)
        for n in range(ATT_KV_HEADS):
            m_ref[n] = jnp.broadcast_to(jnp.max(m_ref[n], axis=-1, keepdims=True), m_ref.shape[1:])
        l_ref[...] = jnp.zeros_like(l_ref)
        acc_ref[...] = jnp.zeros_like(acc_ref)
        run_pass(exp_pass)

    for n in range(ATT_KV_HEADS):
        inv_l = 1.0 / jnp.sum(l_ref[n], axis=-1, keepdims=True)
        for g in range(ATT_GROUP):
            h = n * ATT_GROUP + g
            rows = slice(g * TQ, (g + 1) * TQ)
            o_ref[0, :, h * ATT_HD:(h + 1) * ATT_HD] = (acc_ref[n, rows, :] * inv_l[rows, :]).astype(o_ref.dtype)


def _t5_bucket(rel):
    nb = REL_BUCKETS // 2
    max_exact = nb // 2
    ret = jnp.where(rel > 0, nb, 0)
    n = jnp.abs(rel)
    nf = jnp.maximum(n, 1).astype(jnp.float32)
    large = max_exact + (jnp.log(nf / max_exact) / math.log(REL_MAX_DIST / max_exact)
                         * (nb - max_exact)).astype(jnp.int32)
    large = jnp.minimum(large, nb - 1)
    return ret + jnp.where(n < max_exact, n, large)


def _dsa_core(q, kt, v, qit, ki, wit, rel_bias, n_select, live_rows):
    B, LP, _ = q.shape
    nq = LP // ATT_TILE
    nkt = LP // KEY_TILE
    a = jnp.arange(ATT_TILE)[:, None]
    xx = jnp.arange(2 * KEY_TILE)[None, :]
    rel = (xx - KEY_TILE) - a
    far = rel_bias.astype(F32)[REL_BUCKETS // 2 - 1]
    onehot = jax.nn.one_hot(_t5_bucket(rel), REL_BUCKETS, dtype=F32)
    table = jnp.einsum("abk,kh->abh", onehot, rel_bias.astype(F32),
                       precision=lax.Precision.HIGHEST) - far
    bias_near = jnp.transpose(table.reshape(ATT_TILE, 2 * KEY_TILE // LANES, LANES, ATT_HEADS), (3, 1, 0, 2))
    bias_max = jnp.broadcast_to(jnp.maximum(jnp.max(table, axis=(0, 1)), 0.0)[:, None, None],
                                (ATT_HEADS, 1, LANES))
    kern = functools.partial(_dsa_kernel, n_select=n_select, live_rows=live_rows)
    tile = lambda b, i: (b, i, 0)
    tile4 = lambda b, i: (b, i, 0, 0)
    whole3 = lambda b, i: (b, 0, 0)
    whole4 = lambda b, i: (b, 0, 0, 0)
    once = pl.Buffered(1)
    return pl.pallas_call(
        kern,
        grid=(B, nq),
        in_specs=[pl.BlockSpec((1, ATT_TILE, ATT_Q_DIM), tile),
                  pl.BlockSpec((1, 1, IDX_HEADS * LANES, ATT_TILE), tile4),
                  pl.BlockSpec((1, 1, LANES, ATT_TILE), tile4),
                  pl.BlockSpec((1, nkt, ATT_KV_DIM, KEY_TILE), whole4, pipeline_mode=once),
                  pl.BlockSpec((1, LP, ATT_KV_DIM), whole3, pipeline_mode=once),
                  pl.BlockSpec((1, LP, LANES), whole3, pipeline_mode=once),
                  pl.BlockSpec((ATT_HEADS, 2 * KEY_TILE // LANES, ATT_TILE, LANES), lambda b, i: (0, 0, 0, 0),
                               pipeline_mode=once),
                  pl.BlockSpec((ATT_HEADS, 1, LANES), lambda b, i: (0, 0, 0))],
        out_specs=pl.BlockSpec((1, ATT_TILE, ATT_Q_DIM), tile),
        out_shape=jax.ShapeDtypeStruct((B, LP, ATT_Q_DIM), BF16),
        scratch_shapes=[pltpu.VMEM((nkt, KEY_TILE, ATT_TILE), F32),
                        pltpu.VMEM((nkt, ATT_TILE, KEY_TILE), F32),
                        pltpu.VMEM((ATT_KV_HEADS, ATT_GROUP * ATT_TILE, ATT_HD), BF16),
                        pltpu.VMEM((ATT_KV_HEADS, ATT_GROUP * ATT_TILE, LANES), F32),
                        pltpu.VMEM((ATT_KV_HEADS, ATT_GROUP * ATT_TILE, LANES), F32),
                        pltpu.VMEM((ATT_KV_HEADS, ATT_GROUP * ATT_TILE, ATT_HD), F32),
                        pltpu.VMEM((ATT_KV_HEADS, 1, LANES), F32)],
        compiler_params=_compiler_params(("arbitrary", "arbitrary")),
        name="dsa_core",
    )(q, qit, wit, kt, v, ki, bias_near, bias_max)


def _gdn_in_weight(w_in):
    D = w_in.shape[0]
    w = jnp.zeros((D, GDN_PROJ_W), F32).at[:, 0:w_in.shape[1]].set(w_in.astype(F32))
    return w.astype(BF16)


def _dsa_in_weight(w_in):
    D = w_in.shape[0]
    w_in = w_in.astype(BF16)
    zeros = lambda n: jnp.zeros((D, n), BF16)
    src = ATT_Q_DIM + 2 * ATT_KV_DIM
    parts = [w_in[:, 0:src]]
    for h in range(IDX_HEADS):
        parts += [w_in[:, src + h * IDX_HD:src + (h + 1) * IDX_HD], zeros(LANES - IDX_HD)]
    src += IDX_HEADS * IDX_HD
    parts += [w_in[:, src:src + IDX_HD], zeros(LANES - IDX_HD),
              w_in[:, src + IDX_HD:src + IDX_HD + IDX_HEADS], zeros(LANES - IDX_HEADS)]
    w = jnp.concatenate(parts, axis=1)
    assert w.shape[1] == DSA_PROJ_W
    return w


def kernel(x, meta_tokens, norm_mix, norm_ffn, rel_bias, gdn_w_in, gdn_conv, gdn_a_log, gdn_dt_bias,
           gdn_o_norm, gdn_w_out, dsa_w_in, dsa_q_norm, dsa_k_norm, dsa_idx_ln_g, dsa_idx_ln_b, dsa_w_out,
           moe_w_group, moe_b_group, moe_w_expert, moe_b_expert, moe_w_gate_up, moe_w_down):
    B, S, D = x.shape
    depth = norm_mix.shape[0]
    n_select = min(TOPK_MAX, S // 4)
    off = FRONT_PAD + N_META
    LP = -(-(off + S) // ROW_ALIGN) * ROW_ALIGN
    meta = jnp.broadcast_to(meta_tokens.astype(x.dtype)[None], (B, N_META, D))
    h = jnp.concatenate([jnp.zeros((B, FRONT_PAD, D), x.dtype), meta, x,
                         jnp.zeros((B, LP - off - S, D), x.dtype)], axis=1)
    h = h.reshape(B * LP, D)
    wgu_all = moe_w_gate_up.astype(BF16).reshape(depth * N_EXPERTS, D, 2 * EXPERT_FF)
    wd_all = moe_w_down.astype(BF16).reshape(depth * N_EXPERTS, EXPERT_FF, D)
    for i in range(depth):
        j = i // 2
        if i % 2 == 0:
            o = _gdn_core(h.reshape(B, LP, D), norm_mix[i], _gdn_in_weight(gdn_w_in[j]),
                          gdn_conv[j].astype(F32), gdn_a_log[j], gdn_dt_bias[j], gdn_o_norm[j])
            h = _matmul_residual(o.reshape(B * LP, GDN_V_DIM), gdn_w_out[j].astype(BF16), h)
        else:
            q, kt, v, qit, ki, wit = _dsa_in(h, norm_mix[i], _dsa_in_weight(dsa_w_in[j]), dsa_q_norm[j],
                                             dsa_k_norm[j], dsa_idx_ln_g[j], dsa_idx_ln_b[j])
            r3 = lambda t: t.reshape(B, LP, t.shape[-1])
            r4 = lambda t: t.reshape(B, LP // KEY_TILE, t.shape[-2], KEY_TILE)
            o = _dsa_core(r3(q), r4(kt), r3(v), r4(qit), r3(ki), r4(wit), rel_bias, n_select, off + S)
            h = _matmul_residual(o.reshape(B * LP, ATT_Q_DIM), dsa_w_out[j].astype(BF16), h)
        h = _moe(h, norm_ffn[i], moe_w_group[i], moe_b_group[i], moe_w_expert[i], moe_b_expert[i],
                 wgu_all, wd_all, i, frames=(LP, S, off) if i == depth - 1 else None)
    return h.reshape(B, S, D)
```

```python
import functools
import math

import jax
import jax.numpy as jnp
from jax import lax
from jax.experimental import pallas as pl
from jax.experimental.pallas import tpu as pltpu

F32 = jnp.float32
BF16 = jnp.bfloat16

LANES = 128
SUBLANES = 8
VMEM_LIMIT_BYTES = 56 * 1024 * 1024

CHUNK = 64
N_META = 16
META_PAD = (-N_META) % CHUNK
RMS_EPS = 1e-6
L2_EPS = 1e-6

GDN_HEADS = 8
GDN_DK = 128
GDN_DV = 128
GDN_CONV = 4
GDN_QK_DIM = GDN_HEADS * GDN_DK
GDN_V_DIM = GDN_HEADS * GDN_DV
GDN_CONV_DIM = 2 * GDN_QK_DIM + GDN_V_DIM
GDN_GATE_OFF = GDN_CONV_DIM + GDN_V_DIM
GDN_PROJ_W = GDN_GATE_OFF + LANES

ATT_HEADS = 8
ATT_KV_HEADS = 2
ATT_GROUP = ATT_HEADS // ATT_KV_HEADS
ATT_HD = 128
ATT_Q_DIM = ATT_HEADS * ATT_HD
ATT_KV_DIM = ATT_KV_HEADS * ATT_HD
IDX_HEADS = 8
IDX_HD = 64
TOPK_MAX = 256
REL_BUCKETS = 32
REL_MAX_DIST = 128

DSA_Q_OFF = 0
DSA_K_OFF = ATT_Q_DIM
DSA_V_OFF = DSA_K_OFF + ATT_KV_DIM
DSA_QI_OFF = DSA_V_OFF + ATT_KV_DIM
DSA_KI_OFF = DSA_QI_OFF + IDX_HEADS * LANES
DSA_WI_OFF = DSA_KI_OFF + LANES
DSA_PROJ_W = DSA_WI_OFF + LANES

N_GROUPS = 4
EXPERTS_PER_GROUP = 8
N_EXPERTS = N_GROUPS * EXPERTS_PER_GROUP
EXPERT_FF = 256

GDN_ROW_TILES = (256, 128, 64)
MOE_ROUTER_TILES = (512, 256, 128)
MOE_ROW_TILES = (512, 256, 128)

ATT_TILE = 256
KEY_TILE = 256
ROW_ALIGN = KEY_TILE
INDEXER_GROUP = 4
ATTEND_GROUP = 2
FRONT_PAD = ROW_ALIGN - N_META
assert FRONT_PAD % CHUNK == META_PAD
BISECT_UNROLL = 4
BISECT_STEPS = 20
SHIFT_SLACK = 1.001
ROW_SUM_FLOOR = 1e-30
NEG_INF = float("-inf")


def _pick_tile(n, candidates):
    for c in candidates:
        if n % c == 0:
            return c
    raise ValueError(f"no tile for {n}")


def _compiler_params(semantics):
    return pltpu.CompilerParams(dimension_semantics=semantics, vmem_limit_bytes=VMEM_LIMIT_BYTES)


def _silu(x):
    return x * jax.nn.sigmoid(x)


def _norm_matmul_kernel(x_ref, g_ref, w_ref, o_ref):
    x = x_ref[...]
    y = x * lax.rsqrt(jnp.mean(x * x, axis=-1, keepdims=True) + RMS_EPS) * g_ref[...]
    o_ref[...] = jnp.dot(y.astype(BF16), w_ref[...], preferred_element_type=F32)


def _matmul_residual_kernel(a_ref, w_ref, r_ref, o_ref):
    o_ref[...] = r_ref[...] + jnp.dot(a_ref[...], w_ref[...], preferred_element_type=F32)


def _matmul_residual(a_bf16, w_bf16, res):
    T, K = a_bf16.shape
    N = w_bf16.shape[1]
    tm = _pick_tile(T, (512, 256, 128))
    return pl.pallas_call(
        _matmul_residual_kernel,
        grid=(T // tm,),
        in_specs=[pl.BlockSpec((tm, K), lambda i: (i, 0)),
                  pl.BlockSpec((K, N), lambda i: (0, 0)),
                  pl.BlockSpec((tm, N), lambda i: (i, 0))],
        out_specs=pl.BlockSpec((tm, N), lambda i: (i, 0)),
        out_shape=jax.ShapeDtypeStruct((T, N), F32),
        compiler_params=_compiler_params(("parallel",)),
        name="matmul_residual",
    )(a_bf16, w_bf16, res)


def _nt_dot(a, b):
    return lax.dot_general(a, b, (((1,), (1,)), ((), ())), preferred_element_type=F32)


def _tn_dot(a, b):
    return lax.dot_general(a, b, (((0,), (0,)), ((), ())), preferred_element_type=F32)


def _gdn_kernel(x_ref, g_ref, w_ref, conv_ref, gate_ref, ogain_ref, o_ref,
                state_ref, xbuf_ref, qkv_ref, proj_ref):
    t = pl.program_id(0)
    nb, rows = x_ref.shape[0], x_ref.shape[1]

    @pl.when(t == 0)
    def _():
        state_ref[...] = jnp.zeros_like(state_ref)
        xbuf_ref[:, 0:SUBLANES, :] = jnp.zeros((nb, SUBLANES, GDN_CONV_DIM), F32)

    for b in range(nb):
        _norm_matmul_kernel(x_ref.at[b], g_ref, w_ref, proj_ref.at[b])

    def chunk(cc, carry):
        r0 = pl.multiple_of(cc * CHUNK, CHUNK)
        _gdn_chunk(t * rows + r0, r0, nb, proj_ref, conv_ref, gate_ref, ogain_ref, o_ref,
                   state_ref, xbuf_ref, qkv_ref)
        return carry

    lax.fori_loop(0, rows // CHUNK, chunk, 0)


def _gdn_chunk(row0, r0, nb, proj_ref, conv_ref, gate_ref, ogain_ref, o_ref, state_ref, xbuf_ref, qkv_ref):
    C = CHUNK
    row = row0 + lax.broadcasted_iota(jnp.int32, (C, 1), 0)
    live = row >= FRONT_PAD
    r_i = lax.broadcasted_iota(jnp.int32, (C, C), 0)
    c_i = lax.broadcasted_iota(jnp.int32, (C, C), 1)
    causal = c_i <= r_i
    strict = c_i < r_i
    neg_rate = -jnp.exp(gate_ref[0:1, :])
    betas, gcs, gc_ts, egs, eg_lasts, eks = [], [], [], [], [], []
    for b in range(nb):
        u = jnp.where(live, proj_ref[b, pl.ds(r0, C), 0:GDN_CONV_DIM], 0.0)
        xbuf_ref[b, SUBLANES:SUBLANES + C, :] = u
        conv = conv_ref[0:1, :] * xbuf_ref[b, SUBLANES - 3:SUBLANES - 3 + C, :]
        for j in range(1, GDN_CONV):
            conv = conv + conv_ref[j:j + 1, :] * xbuf_ref[b, SUBLANES - 3 + j:SUBLANES - 3 + j + C, :]
        xbuf_ref[b, 0:SUBLANES, :] = u[C - SUBLANES:C, :]
        qkv_ref[b] = _silu(conv)

        gates = jnp.where(live, proj_ref[b, pl.ds(r0, C), GDN_GATE_OFF:GDN_GATE_OFF + LANES], 0.0)
        g = neg_rate * jax.nn.softplus(gates + gate_ref[1:2, :])
        gc = jnp.dot(causal.astype(F32), g, preferred_element_type=F32,
                     precision=lax.Precision.HIGHEST)
        g_last = gc[C - 1:C, :]
        betas.append(jax.nn.sigmoid(gates))
        gcs.append(gc)
        gc_ts.append(gc.T)
        egs.append(jnp.exp(gc))
        eg_lasts.append(jnp.exp(g_last))
        eks.append(jnp.exp(g_last - gc))

    chains = [(b, h) for b in range(nb) for h in range(GDN_HEADS)]
    heads = range(len(chains))
    dot = functools.partial(jnp.dot, preferred_element_type=F32)
    col = lambda a, h: a[:, GDN_HEADS + h:GDN_HEADS + h + 1]
    qs, ks, k16s, xs, decays = [], [], [], [], []
    for b, h in chains:
        lo = h * GDN_DK
        q = qkv_ref[b, :, lo:lo + GDN_DK]
        k = qkv_ref[b, :, GDN_QK_DIM + lo:GDN_QK_DIM + lo + GDN_DK]
        v = qkv_ref[b, :, 2 * GDN_QK_DIM + h * GDN_DV:2 * GDN_QK_DIM + (h + 1) * GDN_DV]
        q = q * lax.rsqrt(jnp.sum(q * q, axis=-1, keepdims=True) + L2_EPS) * (GDN_DK ** -0.5)
        k = k * lax.rsqrt(jnp.sum(k * k, axis=-1, keepdims=True) + L2_EPS)
        kb = k * betas[b][:, h:h + 1]
        vb = v * betas[b][:, h:h + 1]
        qs.append(q)
        ks.append(k)
        k16s.append(k.astype(BF16))
        xs.append((kb, jnp.concatenate([vb, kb * col(egs[b], h)], axis=1)))
        decays.append(jnp.exp(jnp.where(causal, col(gcs[b], h) - gc_ts[b][GDN_HEADS + h:GDN_HEADS + h + 1, :],
                                        NEG_INF)))
    kk = [_nt_dot(xs[h][0].astype(BF16), k16s[h]) for h in heads]
    qk = [_nt_dot(qs[h].astype(BF16), k16s[h]) for h in heads]
    ns = [-jnp.where(strict, kk[h] * decays[h], 0.0) for h in heads]
    xs = [x for _, x in xs]
    for it in range(6):
        n16 = [n.astype(BF16) for n in ns]
        xs = [xs[h] + dot(n16[h], xs[h].astype(BF16)) for h in heads]
        if it < 5:
            ns = [dot(n16[h], n16[h]) for h in heads]
    qk = [jnp.where(causal, qk[h] * decays[h], 0.0).astype(BF16) for h in heads]
    states = [state_ref[c] for c in heads]
    s16 = [s.astype(BF16) for s in states]
    ws = [dot(xs[c][:, GDN_DV:2 * GDN_DV].astype(BF16), s16[c]) for c in heads]
    qs_s = [dot((qs[c] * col(egs[b], h)).astype(BF16), s16[c]) for c, (b, h) in enumerate(chains)]
    v_new = [(xs[c][:, 0:GDN_DV] - ws[c]).astype(BF16) for c in heads]
    os_ = [qs_s[c] + dot(qk[c], v_new[c]) for c in heads]
    kv = [_tn_dot((ks[c] * col(eks[b], h)).astype(BF16), v_new[c]) for c, (b, h) in enumerate(chains)]
    for c, (b, h) in enumerate(chains):
        state_ref[c] = states[c] * col(eg_lasts[b], h) + kv[c]
        z = proj_ref[b, pl.ds(r0, C), GDN_CONV_DIM + h * GDN_DV:GDN_CONV_DIM + (h + 1) * GDN_DV]
        o = os_[c]
        on = o * lax.rsqrt(jnp.mean(o * o, axis=-1, keepdims=True) + RMS_EPS) * ogain_ref[...]
        o_ref[b, pl.ds(r0, C), h * GDN_DV:(h + 1) * GDN_DV] = (on * _silu(z)).astype(o_ref.dtype)


def _gdn_first_kernel(x_ref, head_ref, g_ref, w_ref, conv_ref, gate_ref, ogain_ref, o_ref, h_ref, *scratch):
    t = pl.program_id(0)

    @pl.when(t == 0)
    def _():
        h_ref[...] = head_ref[...]

    @pl.when(t > 0)
    def _():
        h_ref[...] = x_ref[...]

    _gdn_kernel(h_ref, g_ref, w_ref, conv_ref, gate_ref, ogain_ref, o_ref, *scratch)


def _gdn_core(src, gain, w_bf16, conv_w, a_log, dt_bias, o_gain, head=None):
    W = w_bf16.shape[1]
    gate = jnp.zeros((2, LANES), F32)
    gate = gate.at[0, GDN_HEADS:2 * GDN_HEADS].set(a_log.astype(F32))
    gate = gate.at[1, GDN_HEADS:2 * GDN_HEADS].set(dt_bias.astype(F32))
    fix = lambda t: (0, 0)
    if head is not None:
        B, S, D = src.shape
        rows = head.shape[1]
        LP = rows + S
        assert S % rows == 0
        scratch = [pltpu.VMEM((B * GDN_HEADS, GDN_DK, GDN_DV), F32),
                   pltpu.VMEM((B, SUBLANES + CHUNK, GDN_CONV_DIM), F32),
                   pltpu.VMEM((B, CHUNK, GDN_CONV_DIM), F32),
                   pltpu.VMEM((B, rows, W), F32)]
        return pl.pallas_call(
            _gdn_first_kernel,
            grid=(LP // rows,),
            in_specs=[pl.BlockSpec((B, rows, D), lambda t: (0, jnp.maximum(t - 1, 0), 0)),
                      pl.BlockSpec((B, rows, D), lambda t: (0, 0, 0), pipeline_mode=pl.Buffered(1)),
                      pl.BlockSpec((1, D), fix),
                      pl.BlockSpec((D, W), fix, pipeline_mode=pl.Buffered(1)),
                      pl.BlockSpec((GDN_CONV, GDN_CONV_DIM), fix),
                      pl.BlockSpec((2, LANES), fix),
                      pl.BlockSpec((1, GDN_DV), fix)],
            out_specs=[pl.BlockSpec((B, rows, GDN_V_DIM), lambda t: (0, t, 0)),
                       pl.BlockSpec((B, rows, D), lambda t: (0, t, 0))],
            out_shape=[jax.ShapeDtypeStruct((B, LP, GDN_V_DIM), BF16),
                       jax.ShapeDtypeStruct((B, LP, D), F32)],
            scratch_shapes=scratch,
            compiler_params=_compiler_params(("arbitrary",)),
            name="gdn_core",
        )(src, head, gain.reshape(1, D), w_bf16, conv_w, gate, o_gain.reshape(1, GDN_DV))
    h3d = src
    B, LP, D = h3d.shape
    rows = _pick_tile(LP, GDN_ROW_TILES)
    return pl.pallas_call(
        _gdn_kernel,
        grid=(LP // rows,),
        in_specs=[pl.BlockSpec((B, rows, D), lambda t: (0, t, 0)),
                  pl.BlockSpec((1, D), fix),
                  pl.BlockSpec((D, W), fix, pipeline_mode=pl.Buffered(1)),
                  pl.BlockSpec((GDN_CONV, GDN_CONV_DIM), fix),
                  pl.BlockSpec((2, LANES), fix),
                  pl.BlockSpec((1, GDN_DV), fix)],
        out_specs=pl.BlockSpec((B, rows, GDN_V_DIM), lambda t: (0, t, 0)),
        out_shape=jax.ShapeDtypeStruct((B, LP, GDN_V_DIM), BF16),
        scratch_shapes=[pltpu.VMEM((B * GDN_HEADS, GDN_DK, GDN_DV), F32),
                        pltpu.VMEM((B, SUBLANES + CHUNK, GDN_CONV_DIM), F32),
                        pltpu.VMEM((B, CHUNK, GDN_CONV_DIM), F32),
                        pltpu.VMEM((B, rows, W), F32)],
        compiler_params=_compiler_params(("arbitrary",)),
        name="gdn_core",
    )(h3d, gain.reshape(1, D), w_bf16, conv_w, gate, o_gain.reshape(1, GDN_DV))


def _moe_router_kernel(h_ref, gain_ref, wr_ref, br_ref, hx_ref, gid_ref, rank_ref, cnt_ref, carry_ref):
    i = pl.program_id(0)
    TM = h_ref.shape[0]
    lane = lax.broadcasted_iota(jnp.int32, (1, LANES), 1)

    @pl.when(i == 0)
    def _():
        carry_ref[...] = jnp.zeros_like(carry_ref)

    x = h_ref[...]
    xn = x * lax.rsqrt(jnp.mean(x * x, axis=-1, keepdims=True) + RMS_EPS) * gain_ref[...]
    logits = jnp.dot(xn, wr_ref[...], preferred_element_type=F32,
                     precision=lax.Precision.HIGHEST) + br_ref[...]
    e_log = logits
    g_lane = lane - N_EXPERTS
    g_log = jnp.where((g_lane >= 0) & (g_lane < N_GROUPS), logits, NEG_INF)
    g_max = jnp.max(g_log, axis=-1, keepdims=True)
    g_sel = jnp.min(jnp.where(g_log == g_max, g_lane, LANES), axis=-1, keepdims=True)
    g_sel = jnp.minimum(g_sel, N_GROUPS - 1)
    g_w = 1.0 / jnp.sum(jnp.exp(g_log - g_max), axis=-1, keepdims=True)
    in_grp = (lane >= g_sel * EXPERTS_PER_GROUP) & (lane < (g_sel + 1) * EXPERTS_PER_GROUP)
    l0 = jnp.where(in_grp, e_log, NEG_INF)
    m1 = jnp.max(l0, axis=-1, keepdims=True)
    i1 = jnp.min(jnp.where(l0 == m1, lane, LANES), axis=-1, keepdims=True)
    l1 = jnp.where(lane == i1, NEG_INF, l0)
    m2 = jnp.max(l1, axis=-1, keepdims=True)
    i2 = jnp.min(jnp.where(l1 == m2, lane, LANES), axis=-1, keepdims=True)
    p2 = jnp.exp(m2 - m1)
    w1 = g_w / (1.0 + p2)
    w2 = g_w * p2 / (1.0 + p2)
    j1 = i1 - g_sel * EXPERTS_PER_GROUP
    j2 = i2 - g_sel * EXPERTS_PER_GROUP
    hx_ref[:, 0:xn.shape[1]] = xn
    hx_ref[:, xn.shape[1]:] = jnp.where(lane == j1, w1, 0.0) + jnp.where(lane == j2, w2, 0.0)

    onehot = jnp.where(lane == g_sel, 1.0, 0.0)
    r_i = lax.broadcasted_iota(jnp.int32, (TM, TM), 0)
    c_i = lax.broadcasted_iota(jnp.int32, (TM, TM), 1)
    before = jnp.where(c_i < r_i, 1.0, 0.0).astype(BF16)
    prior = jnp.dot(before, onehot.astype(BF16), preferred_element_type=F32) + carry_ref[...]
    rank = jnp.sum(jnp.where(lane == g_sel, prior, 0.0), axis=-1, keepdims=True)
    carry_ref[...] += jnp.sum(onehot, axis=0, keepdims=True)
    cnt_ref[...] = carry_ref[...]
    gid_ref[0] = jnp.broadcast_to(g_sel.astype(F32), (TM, LANES)).T[0:1, :].astype(jnp.int32)
    rank_ref[0] = jnp.broadcast_to(rank, (TM, LANES)).T[0:1, :].astype(jnp.int32)


def _moe_dispatch_kernel(seg_ref, pos_ref, hx_ref, xs_ref, zero_ref, sem, zero_sem):
    TM = hx_ref.shape[0]

    @pl.when(pl.program_id(0) == 0)
    def _():
        zero_ref[...] = jnp.zeros_like(zero_ref)

        def zero_tile(t):
            fill = pltpu.make_async_copy(zero_ref, xs_ref.at[pl.ds(t * TM, TM), :], zero_sem)
            fill.start()
            fill.wait()

        for g in range(N_GROUPS):
            first = seg_ref[g - 1] if g else 0
            pl.when(seg_ref[g] > first)(functools.partial(zero_tile, seg_ref[g] - 1))
            tail = seg_ref[N_GROUPS - 1] + g
            pl.when(tail < xs_ref.shape[0] // TM)(functools.partial(zero_tile, tail))

    def start(a, carry):
        for b in range(SUBLANES):
            r = a * SUBLANES + b
            pltpu.make_async_copy(hx_ref.at[pl.ds(r, 1), :], xs_ref.at[pl.ds(pos_ref[0, 0, r], 1), :],
                                  sem).start()
        return carry

    lax.fori_loop(0, TM // SUBLANES, start, 0)
    pltpu.make_async_copy(hx_ref, xs_ref.at[pl.ds(0, TM), :], sem).wait()


def _moe_expert_kernel(grp_ref, xs_ref, wgu_ref, wd_ref, ys_ref, xn_ref, acc_ref):
    j = pl.program_id(0)
    e = pl.program_id(1)
    D = xn_ref.shape[1]
    lane = lax.broadcasted_iota(jnp.int32, (1, LANES), 1)

    @pl.when(grp_ref[j] >= 0)
    def _():
        @pl.when(e == 0)
        def _():
            xn_ref[...] = xs_ref[:, 0:D].astype(BF16)
            acc_ref[...] = jnp.zeros_like(acc_ref)

        hh = jnp.dot(xn_ref[...], wgu_ref[0], preferred_element_type=F32)
        cw = jnp.sum(jnp.where(lane == e, xs_ref[:, D:], 0.0), axis=-1, keepdims=True)
        act = _silu(hh[:, 0:EXPERT_FF]) * hh[:, EXPERT_FF:2 * EXPERT_FF] * cw
        acc_ref[...] += jnp.dot(act.astype(BF16), wd_ref[0], preferred_element_type=F32)

    last = e == EXPERTS_PER_GROUP - 1

    @pl.when(last & (grp_ref[j] >= 0))
    def _():
        ys_ref[...] = acc_ref[...]

    @pl.when(last & (grp_ref[j] < 0))
    def _():
        ys_ref[...] = jnp.zeros_like(ys_ref)


def _moe_combine_kernel(pos_ref, h_ref, ys_ref, o_ref, stage_ref, sem):
    TM = h_ref.shape[0]

    def start(a, carry):
        for b in range(SUBLANES):
            r = a * SUBLANES + b
            pltpu.make_async_copy(ys_ref.at[pl.ds(pos_ref[0, 0, r], 1), :], stage_ref.at[pl.ds(r, 1), :],
                                  sem).start()
        return carry

    lax.fori_loop(0, TM // SUBLANES, start, 0)
    pltpu.make_async_copy(ys_ref.at[pl.ds(0, TM), :], stage_ref, sem).wait()
    o_ref[...] = h_ref[...] + stage_ref[...]


def _moe(h2d, gain, w_group, b_group, w_expert, b_expert, wgu, wd, layer, frames=None):
    T, D = h2d.shape
    tr = _pick_tile(T, MOE_ROUTER_TILES)
    tm = _pick_tile(T, MOE_ROW_TILES)
    n_tiles = T // tm + N_GROUPS
    XW = D + LANES
    assert N_EXPERTS + N_GROUPS <= LANES
    wr = jnp.zeros((D, LANES), F32)
    wr = wr.at[:, 0:N_EXPERTS].set(w_expert.astype(F32))
    wr = wr.at[:, N_EXPERTS:N_EXPERTS + N_GROUPS].set(w_group.astype(F32))
    br = jnp.zeros((1, LANES), F32)
    br = br.at[0, 0:N_EXPERTS].set(b_expert.astype(F32))
    br = br.at[0, N_EXPERTS:N_EXPERTS + N_GROUPS].set(b_group.astype(F32))

    hx, gid, rank, counts = pl.pallas_call(
        _moe_router_kernel,
        grid=(T // tr,),
        in_specs=[pl.BlockSpec((tr, D), lambda i: (i, 0)),
                  pl.BlockSpec((1, D), lambda i: (0, 0)),
                  pl.BlockSpec((D, LANES), lambda i: (0, 0)),
                  pl.BlockSpec((1, LANES), lambda i: (0, 0))],
        out_specs=[pl.BlockSpec((tr, XW), lambda i: (i, 0)),
                   pl.BlockSpec((1, 1, tr), lambda i: (i, 0, 0)),
                   pl.BlockSpec((1, 1, tr), lambda i: (i, 0, 0)),
                   pl.BlockSpec((1, LANES), lambda i: (0, 0))],
        out_shape=[jax.ShapeDtypeStruct((T, XW), F32),
                   jax.ShapeDtypeStruct((T // tr, 1, tr), jnp.int32),
                   jax.ShapeDtypeStruct((T // tr, 1, tr), jnp.int32),
                   jax.ShapeDtypeStruct((1, LANES), F32)],
        scratch_shapes=[pltpu.VMEM((1, LANES), F32)],
        compiler_params=_compiler_params(("arbitrary",)),
        name="moe_router",
    )(h2d, gain.reshape(1, D), wr, br)

    seg_tiles = -(-counts[0, 0:N_GROUPS].astype(jnp.int32) // tm)
    seg_end = jnp.cumsum(seg_tiles)
    seg_start = seg_end - seg_tiles
    pos = (seg_start * tm)[gid.reshape(T)] + rank.reshape(T)
    tile_ids = jnp.arange(n_tiles, dtype=jnp.int32)
    tile_group = jnp.sum((tile_ids[:, None] >= seg_end[None, :]).astype(jnp.int32), axis=1)
    tile_group = jnp.where(tile_ids < seg_end[N_GROUPS - 1], tile_group, -1)
    pos3 = pos.reshape(T // tm, 1, tm)

    xs = pl.pallas_call(
        _moe_dispatch_kernel,
        grid_spec=pltpu.PrefetchScalarGridSpec(
            num_scalar_prefetch=1,
            grid=(T // tm,),
            in_specs=[pl.BlockSpec((1, 1, tm), lambda i, seg: (i, 0, 0), memory_space=pltpu.SMEM),
                      pl.BlockSpec((tm, XW), lambda i, seg: (i, 0))],
            out_specs=pl.BlockSpec(memory_space=pl.ANY),
            scratch_shapes=[pltpu.VMEM((tm, XW), F32), pltpu.SemaphoreType.DMA(()),
                            pltpu.SemaphoreType.DMA(())]),
        out_shape=jax.ShapeDtypeStruct((n_tiles * tm, XW), F32),
        compiler_params=_compiler_params(("arbitrary",)),
        name="moe_dispatch",
    )(seg_end, pos3, hx)

    def expert_index(j, e, grp):
        return (layer * N_EXPERTS + jnp.maximum(grp[j], 0) * EXPERTS_PER_GROUP + e, 0, 0)

    ys = pl.pallas_call(
        _moe_expert_kernel,
        grid_spec=pltpu.PrefetchScalarGridSpec(
            num_scalar_prefetch=1,
            grid=(n_tiles, EXPERTS_PER_GROUP),
            in_specs=[pl.BlockSpec((tm, XW), lambda j, e, grp: (j, 0)),
                      pl.BlockSpec((1, D, 2 * EXPERT_FF), expert_index),
                      pl.BlockSpec((1, EXPERT_FF, D), expert_index)],
            out_specs=pl.BlockSpec((tm, D), lambda j, e, grp: (j, 0)),
            scratch_shapes=[pltpu.VMEM((tm, D), BF16), pltpu.VMEM((tm, D), F32)]),
        out_shape=jax.ShapeDtypeStruct((n_tiles * tm, D), F32),
        compiler_params=_compiler_params(("parallel", "arbitrary")),
        name="moe_experts",
    )(tile_group, xs, wgu, wd)

    if frames is None:
        tc, out_rows, out_index = tm, T, lambda i: (i, 0)
    else:
        batch_rows, n_frames, first = frames
        tc = ROW_ALIGN
        assert batch_rows % tc == 0 and n_frames % tc == 0 and first % tc == 0 and first >= tc
        out_rows = (T // batch_rows) * n_frames

        def out_index(i):
            b, t = i // (batch_rows // tc), i % (batch_rows // tc)
            return (b * (n_frames // tc) + jnp.maximum(t - first // tc, 0), 0)

    return pl.pallas_call(
        _moe_combine_kernel,
        grid=(T // tc,),
        in_specs=[pl.BlockSpec((1, 1, tc), lambda i: (i, 0, 0), memory_space=pltpu.SMEM),
                  pl.BlockSpec((tc, D), lambda i: (i, 0)),
                  pl.BlockSpec(memory_space=pl.ANY)],
        out_specs=pl.BlockSpec((tc, D), out_index),
        out_shape=jax.ShapeDtypeStruct((out_rows, D), F32),
        scratch_shapes=[pltpu.VMEM((tc, D), F32), pltpu.SemaphoreType.DMA(())],
        compiler_params=_compiler_params(("arbitrary",)),
        name="moe_combine",
    )(pos.reshape(T // tc, 1, tc), h2d, ys)


def _dsa_prep_kernel(p_ref, qg_ref, kg_ref, lng_ref, lnb_ref, q_ref, kt_ref, v_ref, qit_ref, ki_ref, wit_ref):
    lane = lax.broadcasted_iota(jnp.int32, (1, LANES), 1)
    for h in range(ATT_HEADS):
        x = p_ref[:, DSA_Q_OFF + h * ATT_HD:DSA_Q_OFF + (h + 1) * ATT_HD]
        y = x * lax.rsqrt(jnp.mean(x * x, axis=-1, keepdims=True) + RMS_EPS) * qg_ref[...]
        q_ref[:, h * ATT_HD:(h + 1) * ATT_HD] = (y * (ATT_HD ** -0.5)).astype(BF16)
    for n in range(ATT_KV_HEADS):
        x = p_ref[:, DSA_K_OFF + n * ATT_HD:DSA_K_OFF + (n + 1) * ATT_HD]
        y = x * lax.rsqrt(jnp.mean(x * x, axis=-1, keepdims=True) + RMS_EPS) * kg_ref[...]
        kt_ref[0, n * ATT_HD:(n + 1) * ATT_HD, :] = y.T.astype(BF16)
    v_ref[...] = p_ref[:, DSA_V_OFF:DSA_V_OFF + ATT_KV_DIM].astype(BF16)
    qit_ref[0] = p_ref[:, DSA_QI_OFF:DSA_QI_OFF + IDX_HEADS * LANES].T.astype(BF16)
    x = p_ref[:, DSA_KI_OFF:DSA_KI_OFF + LANES]
    live = lane < IDX_HD
    mu = jnp.sum(x, axis=-1, keepdims=True) * (1.0 / IDX_HD)
    xc = jnp.where(live, x - mu, 0.0)
    var = jnp.sum(xc * xc, axis=-1, keepdims=True) * (1.0 / IDX_HD)
    ki = xc * lax.rsqrt(var + RMS_EPS) * lng_ref[...] + lnb_ref[...]
    ki_ref[...] = jnp.where(live, ki, 0.0).astype(BF16)
    wit_ref[0] = (p_ref[:, DSA_WI_OFF:DSA_WI_OFF + LANES] * (IDX_HEADS ** -0.5 * IDX_HD ** -0.5)).T


def _dsa_in_kernel(x_ref, g_ref, w_ref, qg_ref, kg_ref, lng_ref, lnb_ref,
                   q_ref, kt_ref, v_ref, qit_ref, ki_ref, wit_ref, proj_ref):
    _norm_matmul_kernel(x_ref, g_ref, w_ref, proj_ref)
    _dsa_prep_kernel(proj_ref, qg_ref, kg_ref, lng_ref, lnb_ref, q_ref, kt_ref, v_ref, qit_ref, ki_ref, wit_ref)


def _dsa_in(h2d, gain, w_bf16, q_gain, k_gain, ln_g, ln_b):
    T, D = h2d.shape
    W = w_bf16.shape[1]
    tm = KEY_TILE
    assert T % tm == 0 and ATT_TILE == KEY_TILE
    lng = jnp.zeros((1, LANES), F32).at[0, 0:IDX_HD].set(ln_g.astype(F32))
    lnb = jnp.zeros((1, LANES), F32).at[0, 0:IDX_HD].set(ln_b.astype(F32))
    row = lambda i: (i, 0)
    fix = lambda i: (0, 0)
    blk = lambda i: (i, 0, 0)
    return pl.pallas_call(
        _dsa_in_kernel,
        grid=(T // tm,),
        in_specs=[pl.BlockSpec((tm, D), row), pl.BlockSpec((1, D), fix), pl.BlockSpec((D, W), fix),
                  pl.BlockSpec((1, ATT_HD), fix), pl.BlockSpec((1, ATT_HD), fix),
                  pl.BlockSpec((1, LANES), fix), pl.BlockSpec((1, LANES), fix)],
        out_specs=[pl.BlockSpec((tm, ATT_Q_DIM), row), pl.BlockSpec((1, ATT_KV_DIM, tm), blk),
                   pl.BlockSpec((tm, ATT_KV_DIM), row), pl.BlockSpec((1, IDX_HEADS * LANES, tm), blk),
                   pl.BlockSpec((tm, LANES), row), pl.BlockSpec((1, LANES, tm), blk)],
        out_shape=[jax.ShapeDtypeStruct((T, ATT_Q_DIM), BF16), jax.ShapeDtypeStruct((T // tm, ATT_KV_DIM, tm), BF16),
                   jax.ShapeDtypeStruct((T, ATT_KV_DIM), BF16),
                   jax.ShapeDtypeStruct((T // tm, IDX_HEADS * LANES, tm), BF16),
                   jax.ShapeDtypeStruct((T, LANES), BF16), jax.ShapeDtypeStruct((T // tm, LANES, tm), F32)],
        scratch_shapes=[pltpu.VMEM((tm, W), F32)],
        compiler_params=_compiler_params(("parallel",)),
        name="dsa_in",
    )(h2d, gain.reshape(1, D), w_bf16, q_gain.reshape(1, ATT_HD), k_gain.reshape(1, ATT_HD), lng, lnb)


def _dsa_kernel(q_ref, qit_ref, wit_ref, kt_ref, v_ref, ki_ref, bias_ref, bmax_ref, o_ref,
                sc_ref, mask_ref, qg_ref, m_ref, l_ref, acc_ref, knorm_ref, *, n_select, live_rows):
    i = pl.program_id(1)
    TQ = ATT_TILE
    TK = KEY_TILE
    HALVES = TK // LANES

    @pl.when(i == 0)
    def _():
        def body(kt, mx):
            kk = kt_ref[0, kt].astype(F32)
            return tuple(jnp.maximum(mx[n], jnp.sum(jnp.square(kk[n * ATT_HD:(n + 1) * ATT_HD, :]),
                                                    axis=0, keepdims=True)) for n in range(ATT_KV_HEADS))

        mx = lax.fori_loop(0, kt_ref.shape[1], body,
                           tuple(jnp.zeros((1, TK), F32) for _ in range(ATT_KV_HEADS)))
        for n in range(ATT_KV_HEADS):
            knorm_ref[n] = jnp.broadcast_to(jnp.sqrt(jnp.max(mx[n], axis=-1, keepdims=True)), (1, LANES))

    @pl.when(i * TQ >= live_rows)
    def _():
        o_ref[...] = jnp.zeros_like(o_ref)

    @pl.when(i * TQ < live_rows)
    def _():
        _dsa_tile(i, q_ref, qit_ref, wit_ref, kt_ref, v_ref, ki_ref, bias_ref, bmax_ref, o_ref,
                  sc_ref, mask_ref, qg_ref, m_ref, l_ref, acc_ref, knorm_ref, n_select, TQ, TK, HALVES)


def _dsa_tile(i, q_ref, qit_ref, wit_ref, kt_ref, v_ref, ki_ref, bias_ref, bmax_ref, o_ref,
              sc_ref, mask_ref, qg_ref, m_ref, l_ref, acc_ref, knorm_ref, n_select, TQ, TK, HALVES):
    assert TQ == TK
    nkt = i + 1
    fmin = float(jnp.finfo(F32).min)
    kf = float(n_select)

    chunk_shift = int(math.log2(CHUNK))
    q_pos = i * TQ + lax.broadcasted_iota(jnp.int32, (1, TQ), 1)
    q_chunk = jnp.maximum(lax.shift_right_logical(q_pos, chunk_shift), FRONT_PAD // CHUNK)
    k_row = lax.broadcasted_iota(jnp.int32, (TK, 1), 0)

    def halves(x):
        return [x[:, j * LANES:(j + 1) * LANES] for j in range(HALVES)]

    def fold(x, op):
        parts = [x[r * SUBLANES:(r + 1) * SUBLANES, :] for r in range(TK // SUBLANES)]
        while len(parts) > 1:
            parts = [op(parts[j], parts[j + 1]) for j in range(0, len(parts), 2)]
        return parts[0]

    def score_tiles(kts):
        for kt in kts:
            keys = ki_ref[0, pl.ds(pl.multiple_of(kt * TK, TK), TK), :]
            acc = None
            for h in range(IDX_HEADS):
                d = jnp.dot(keys, qit_ref[0, 0, h * LANES:(h + 1) * LANES, :], preferred_element_type=F32)
                term = wit_ref[0, 0, h:h + 1, :] * jnp.maximum(d, 0.0)
                acc = term if acc is None else acc + term
            sc_ref[kt] = acc
            mask_ref[kt] = acc.T

    def grouped_loop(n, fn, group):
        def body(j, carry):
            fn([group * j + r for r in range(group)])
            return carry

        lax.fori_loop(0, lax.shift_right_logical(n, int(math.log2(group))), body, 0)
        size = group // 2
        while size >= 1:
            first = lax.bitwise_and(n, ~(2 * size - 1))

            @pl.when(lax.bitwise_and(n, size) != 0)
            def _(size=size, first=first):
                fn([first + r for r in range(size)])

            size //= 2

    grouped_loop(nkt, score_tiles, INDEXER_GROUP)

    def mask_inadmissible(kt):
        k_pos = kt * TK + k_row
        adm = (lax.shift_right_logical(k_pos, chunk_shift) <= q_chunk) & (k_pos >= FRONT_PAD)
        sc_ref[kt] = jnp.where(adm, sc_ref[kt], NEG_INF)

    mask_inadmissible(0)

    @pl.when(nkt > 1)
    def _():
        mask_inadmissible(nkt - 1)

    n_adm = CHUNK * (q_chunk + 1) - FRONT_PAD
    need = n_adm > n_select

    def count_where(pred):
        def one(kt):
            return fold(jnp.where(pred(sc_ref[kt], kt), 1.0, 0.0), jnp.add)

        def two(j, acc):
            return acc + (one(2 * j) + one(2 * j + 1))

        acc = lax.fori_loop(0, lax.shift_right_logical(nkt, 1), two, jnp.zeros((SUBLANES, TQ), F32))
        acc = lax.cond(lax.bitwise_and(nkt, 1) == 1, lambda a: a + one(nkt - 1), lambda a: a, acc)
        return jnp.sum(acc, axis=0, keepdims=True)

    def count_ge(t):
        return count_where(lambda s, kt: s >= t)

    def minmax_body(kt, carry):
        mn, mx = carry
        s = sc_ref[kt]
        mn = jnp.minimum(mn, fold(jnp.where(s == NEG_INF, jnp.inf, s), jnp.minimum))
        mx = jnp.maximum(mx, fold(s, jnp.maximum))
        return mn, mx

    mn_l, mx_l = lax.fori_loop(0, nkt, minmax_body,
                               (jnp.full((SUBLANES, TQ), jnp.inf, F32), jnp.full((SUBLANES, TQ), NEG_INF, F32)))
    s_min = jnp.min(mn_l, axis=0, keepdims=True)
    s_max = jnp.max(mx_l, axis=0, keepdims=True)
    lo0 = s_min
    hi0 = s_max + (s_max - s_min) + 1.0
    cnt0 = n_adm.astype(F32)
    active0 = jnp.where(need, 1.0, 0.0)

    def any_active(active):
        return jnp.max(active) > 0.0

    def bis_step(lo, hi, cnt, active):
        mid = lo + (hi - lo) * 0.5
        c = count_ge(mid)
        ge = c >= kf
        moving = (mid > lo) & (mid < hi) & (active > 0.0)
        lo_n = jnp.where(moving & ge, mid, lo)
        hi_n = jnp.where(moving & (~ge), mid, hi)
        cnt_n = jnp.where(moving & ge, c, cnt)
        active_n = jnp.where(moving & (cnt_n != kf), 1.0, 0.0)
        return lo_n, hi_n, cnt_n, active_n

    def bis_body(st):
        it, rest = st[0], st[1:]
        for _ in range(BISECT_UNROLL):
            rest = bis_step(*rest)
        return (it + BISECT_UNROLL,) + rest

    _, lo, hi, cnt, _ = lax.while_loop(
        lambda st: jnp.logical_and(st[0] < BISECT_STEPS, any_active(st[4])), bis_body,
        (jnp.int32(0), lo0, hi0, cnt0, active0))

    def snap_body(st):
        lo, hi, cnt, active = st

        def body(kt, mx):
            s = sc_ref[kt]
            return jnp.maximum(mx, fold(jnp.where(s < hi, s, NEG_INF), jnp.maximum))

        v1 = jnp.max(lax.fori_loop(0, nkt, body, jnp.full((SUBLANES, TQ), NEG_INF, F32)),
                     axis=0, keepdims=True)
        c1 = count_ge(v1)
        on = active > 0.0
        hit = on & (c1 >= kf)
        lo_n = jnp.where(hit, v1, lo)
        cnt_n = jnp.where(hit, c1, cnt)
        hi_n = jnp.where(on & (~hit), v1, hi)
        return lo_n, hi_n, cnt_n, jnp.where(on & (~hit), 1.0, 0.0)

    unsettled = jnp.where(need & (cnt != kf), 1.0, 0.0)
    lo, hi, cnt, _ = lax.while_loop(lambda st: any_active(st[3]), snap_body, (lo, hi, cnt, unsettled))

    thr = jnp.where(need, lo, fmin)
    tied = need & (cnt > kf)
    big_pos = jnp.int32(2 ** 30)

    def tie_break():
        want = kf - count_where(lambda s, kt: s > thr)

        def body(_, st):
            jl, jh = st
            jm = lax.shift_right_arithmetic(jl + jh, 1)
            c = count_where(lambda s, kt: (s == thr) & (kt * TK + k_row <= jm))
            ok = c >= want
            return jnp.where(ok, jl, jm), jnp.where(ok, jm, jh)

        n_keys = sc_ref.shape[0] * TK
        steps = int(math.ceil(math.log2(n_keys + 1))) + 1
        jl0 = jnp.full((1, TQ), -1, jnp.int32)
        jh0 = jnp.full((1, TQ), n_keys, jnp.int32)
        _, jh = lax.fori_loop(0, steps, body, (jl0, jh0))
        return jnp.where(tied, jh, big_pos)

    pos_cap = lax.cond(jnp.max(jnp.where(tied, 1.0, 0.0)) > 0.0, tie_break,
                       lambda: jnp.full((1, TQ), big_pos, jnp.int32))

    def lane_wide(row):
        return jnp.broadcast_to(row, (LANES, TQ)).T

    thr_w = lane_wide(thr)
    cap_w = lane_wide(pos_cap)
    chunk_w = lane_wide(q_chunk)
    k_lane = lax.broadcasted_iota(jnp.int32, (1, LANES), 1)

    def to_mask(kt, edge_tile):
        out = []
        for j, s in enumerate(halves(mask_ref[kt])):
            k_pos = kt * TK + j * LANES + k_lane
            sel = (s > thr_w) | ((s == thr_w) & (k_pos <= cap_w))
            if edge_tile:
                sel = sel & (lax.shift_right_logical(k_pos, chunk_shift) <= chunk_w) & (k_pos >= FRONT_PAD)
            out.append(jnp.where(sel, 0.0, NEG_INF))
        mask_ref[kt] = jnp.concatenate(out, axis=1)

    def inner_mask(kt, carry):
        to_mask(kt, False)
        return carry

    lax.fori_loop(1, nkt - 1, inner_mask, 0)
    to_mask(0, True)

    @pl.when(nkt > 1)
    def _():
        to_mask(nkt - 1, True)

    for n in range(ATT_KV_HEADS):
        for g in range(ATT_GROUP):
            h = n * ATT_GROUP + g
            qg_ref[n, g * TQ:(g + 1) * TQ, :] = q_ref[0, :, h * ATT_HD:(h + 1) * ATT_HD]
    l_ref[...] = jnp.zeros_like(l_ref)
    acc_ref[...] = jnp.zeros_like(acc_ref)

    kv_heads = range(ATT_KV_HEADS)

    def raw_logits(kts):
        return [[jnp.dot(qg_ref[n], kt_ref[0, kt, n * ATT_HD:(n + 1) * ATT_HD, :], preferred_element_type=F32)
                 for n in kv_heads] for kt in kts]

    def masked_logits(s, mask, n, g, b0):
        sg = s[g * TQ:(g + 1) * TQ, :] + mask
        if b0 is not None:
            h = n * ATT_GROUP + g
            sg = sg + jnp.concatenate([bias_ref[h, b0 + j] for j in range(HALVES)], axis=1)
        return sg

    def max_pass(kts, b0):
        raw = raw_logits(kts)
        masks = [mask_ref[kt] for kt in kts]
        for n in kv_heads:
            for g in range(ATT_GROUP):
                rows = slice(g * TQ, (g + 1) * TQ)
                m = m_ref[n, rows, :]
                for t in range(len(kts)):
                    for part in halves(masked_logits(raw[t][n], masks[t], n, g, b0)):
                        m = jnp.maximum(m, part)
                m_ref[n, rows, :] = m

    def exp_pass(kts, b0):
        raw = raw_logits(kts)
        masks = [mask_ref[kt] for kt in kts]
        for n in kv_heads:
            ps = [[] for _ in kts]
            for g in range(ATT_GROUP):
                rows = slice(g * TQ, (g + 1) * TQ)
                m = m_ref[n, rows, :]
                l_new = l_ref[n, rows, :]
                for t in range(len(kts)):
                    parts = [jnp.exp(part - m) for part in halves(masked_logits(raw[t][n], masks[t], n, g, b0))]
                    l_new = l_new + functools.reduce(lambda a, b: a + b, parts)
                    ps[t].append(jnp.concatenate(parts, axis=1).astype(BF16))
                l_ref[n, rows, :] = l_new
            pv = None
            for t, kt in enumerate(kts):
                vv = v_ref[0, pl.ds(pl.multiple_of(kt * TK, TK), TK), n * ATT_HD:(n + 1) * ATT_HD]
                d = jnp.dot(jnp.concatenate(ps[t], axis=0), vv, preferred_element_type=F32)
                pv = d if pv is None else pv + d
            acc_ref[n] += pv

    def run_pass(tile_fn):
        grouped_loop(jnp.maximum(i - 1, 0), lambda kts: tile_fn(kts, None), ATTEND_GROUP)

        @pl.when(i >= 1)
        def _():
            tile_fn([i - 1], 0)

        tile_fn([i], HALVES)

    for n in range(ATT_KV_HEADS):
        for g in range(ATT_GROUP):
            h = n * ATT_GROUP + g
            qh = qg_ref[n, g * TQ:(g + 1) * TQ, :].astype(F32)
            q_len = jnp.sqrt(jnp.sum(qh * qh, axis=-1, keepdims=True))
            bound = q_len * knorm_ref[n] * SHIFT_SLACK + (bmax_ref[h] + SHIFT_SLACK - 1.0)
            m_ref[n, g * TQ:(g + 1) * TQ, :] = jnp.broadcast_to(bound, (TQ, LANES))
    run_pass(exp_pass)

    def smallest_row_sum():
        return functools.reduce(jnp.minimum, [jnp.min(jnp.sum(l_ref[n], axis=-1, keepdims=True))
                                              for n in range(ATT_KV_HEADS)])

    @pl.when(jnp.logical_not(smallest_row_sum() > ROW_SUM_FLOOR))
    def _():
        m_ref[...] = jnp.full(m_ref.shape, -1e30, F32)
        run_pass(max_pass)
        for n in range(ATT_KV_HEADS):
            m_ref[n] = jnp.broadcast_to(jnp.max(m_ref[n], axis=-1, keepdims=True), m_ref.shape[1:])
        l_ref[...] = jnp.zeros_like(l_ref)
        acc_ref[...] = jnp.zeros_like(acc_ref)
        run_pass(exp_pass)

    for n in range(ATT_KV_HEADS):
        inv_l = 1.0 / jnp.sum(l_ref[n], axis=-1, keepdims=True)
        for g in range(ATT_GROUP):
            h = n * ATT_GROUP + g
            rows = slice(g * TQ, (g + 1) * TQ)
            o_ref[0, :, h * ATT_HD:(h + 1) * ATT_HD] = (acc_ref[n, rows, :] * inv_l[rows, :]).astype(o_ref.dtype)


def _t5_bucket(rel):
    nb = REL_BUCKETS // 2
    max_exact = nb // 2
    ret = jnp.where(rel > 0, nb, 0)
    n = jnp.abs(rel)
    nf = jnp.maximum(n, 1).astype(jnp.float32)
    large = max_exact + (jnp.log(nf / max_exact) / math.log(REL_MAX_DIST / max_exact)
                         * (nb - max_exact)).astype(jnp.int32)
    large = jnp.minimum(large, nb - 1)
    return ret + jnp.where(n < max_exact, n, large)


def _dsa_core(q, kt, v, qit, ki, wit, rel_bias, n_select, live_rows):
    B, LP, _ = q.shape
    nq = LP // ATT_TILE
    nkt = LP // KEY_TILE
    a = jnp.arange(ATT_TILE)[:, None]
    xx = jnp.arange(2 * KEY_TILE)[None, :]
    rel = (xx - KEY_TILE) - a
    far = rel_bias.astype(F32)[REL_BUCKETS // 2 - 1]
    onehot = jax.nn.one_hot(_t5_bucket(rel), REL_BUCKETS, dtype=F32)
    table = jnp.einsum("abk,kh->abh", onehot, rel_bias.astype(F32),
                       precision=lax.Precision.HIGHEST) - far
    bias_near = jnp.transpose(table.reshape(ATT_TILE, 2 * KEY_TILE // LANES, LANES, ATT_HEADS), (3, 1, 0, 2))
    bias_max = jnp.broadcast_to(jnp.maximum(jnp.max(table, axis=(0, 1)), 0.0)[:, None, None],
                                (ATT_HEADS, 1, LANES))
    kern = functools.partial(_dsa_kernel, n_select=n_select, live_rows=live_rows)
    tile = lambda b, i: (b, i, 0)
    tile4 = lambda b, i: (b, i, 0, 0)
    whole3 = lambda b, i: (b, 0, 0)
    whole4 = lambda b, i: (b, 0, 0, 0)
    once = pl.Buffered(1)
    return pl.pallas_call(
        kern,
        grid=(B, nq),
        in_specs=[pl.BlockSpec((1, ATT_TILE, ATT_Q_DIM), tile),
                  pl.BlockSpec((1, 1, IDX_HEADS * LANES, ATT_TILE), tile4),
                  pl.BlockSpec((1, 1, LANES, ATT_TILE), tile4),
                  pl.BlockSpec((1, nkt, ATT_KV_DIM, KEY_TILE), whole4, pipeline_mode=once),
                  pl.BlockSpec((1, LP, ATT_KV_DIM), whole3, pipeline_mode=once),
                  pl.BlockSpec((1, LP, LANES), whole3, pipeline_mode=once),
                  pl.BlockSpec((ATT_HEADS, 2 * KEY_TILE // LANES, ATT_TILE, LANES), lambda b, i: (0, 0, 0, 0),
                               pipeline_mode=once),
                  pl.BlockSpec((ATT_HEADS, 1, LANES), lambda b, i: (0, 0, 0))],
        out_specs=pl.BlockSpec((1, ATT_TILE, ATT_Q_DIM), tile),
        out_shape=jax.ShapeDtypeStruct((B, LP, ATT_Q_DIM), BF16),
        scratch_shapes=[pltpu.VMEM((nkt, KEY_TILE, ATT_TILE), F32),
                        pltpu.VMEM((nkt, ATT_TILE, KEY_TILE), F32),
                        pltpu.VMEM((ATT_KV_HEADS, ATT_GROUP * ATT_TILE, ATT_HD), BF16),
                        pltpu.VMEM((ATT_KV_HEADS, ATT_GROUP * ATT_TILE, LANES), F32),
                        pltpu.VMEM((ATT_KV_HEADS, ATT_GROUP * ATT_TILE, LANES), F32),
                        pltpu.VMEM((ATT_KV_HEADS, ATT_GROUP * ATT_TILE, ATT_HD), F32),
                        pltpu.VMEM((ATT_KV_HEADS, 1, LANES), F32)],
        compiler_params=_compiler_params(("arbitrary", "arbitrary")),
        name="dsa_core",
    )(q, qit, wit, kt, v, ki, bias_near, bias_max)


def _gdn_in_weight(w_in):
    D = w_in.shape[0]
    w = jnp.zeros((D, GDN_PROJ_W), F32).at[:, 0:w_in.shape[1]].set(w_in.astype(F32))
    return w.astype(BF16)


def _dsa_in_weight(w_in):
    D = w_in.shape[0]
    w_in = w_in.astype(BF16)
    zeros = lambda n: jnp.zeros((D, n), BF16)
    src = ATT_Q_DIM + 2 * ATT_KV_DIM
    parts = [w_in[:, 0:src]]
    for h in range(IDX_HEADS):
        parts += [w_in[:, src + h * IDX_HD:src + (h + 1) * IDX_HD], zeros(LANES - IDX_HD)]
    src += IDX_HEADS * IDX_HD
    parts += [w_in[:, src:src + IDX_HD], zeros(LANES - IDX_HD),
              w_in[:, src + IDX_HD:src + IDX_HD + IDX_HEADS], zeros(LANES - IDX_HEADS)]
    w = jnp.concatenate(parts, axis=1)
    assert w.shape[1] == DSA_PROJ_W
    return w


def kernel(x, meta_tokens, norm_mix, norm_ffn, rel_bias, gdn_w_in, gdn_conv, gdn_a_log, gdn_dt_bias,
           gdn_o_norm, gdn_w_out, dsa_w_in, dsa_q_norm, dsa_k_norm, dsa_idx_ln_g, dsa_idx_ln_b, dsa_w_out,
           moe_w_group, moe_b_group, moe_w_expert, moe_b_expert, moe_w_gate_up, moe_w_down):
    B, S, D = x.shape
    depth = norm_mix.shape[0]
    n_select = min(TOPK_MAX, S // 4)
    off = FRONT_PAD + N_META
    LP = -(-(off + S) // ROW_ALIGN) * ROW_ALIGN
    assert LP == off + S, "frames must fill whole attention tiles"
    meta = jnp.broadcast_to(meta_tokens.astype(x.dtype)[None], (B, N_META, D))
    head = jnp.concatenate([jnp.zeros((B, FRONT_PAD, D), x.dtype), meta], axis=1)
    wgu_all = moe_w_gate_up.astype(BF16).reshape(depth * N_EXPERTS, D, 2 * EXPERT_FF)
    wd_all = moe_w_down.astype(BF16).reshape(depth * N_EXPERTS, EXPERT_FF, D)
    h = None
    for i in range(depth):
        j = i // 2
        if i % 2 == 0:
            gdn_args = (norm_mix[i], _gdn_in_weight(gdn_w_in[j]), gdn_conv[j].astype(F32), gdn_a_log[j],
                        gdn_dt_bias[j], gdn_o_norm[j])
            if i == 0:
                o, h = _gdn_core(x, *gdn_args, head=head)
                h = h.reshape(B * LP, D)
            else:
                o = _gdn_core(h.reshape(B, LP, D), *gdn_args)
            h = _matmul_residual(o.reshape(B * LP, GDN_V_DIM), gdn_w_out[j].astype(BF16), h)
        else:
            q, kt, v, qit, ki, wit = _dsa_in(h, norm_mix[i], _dsa_in_weight(dsa_w_in[j]), dsa_q_norm[j],
                                             dsa_k_norm[j], dsa_idx_ln_g[j], dsa_idx_ln_b[j])
            r3 = lambda t: t.reshape(B, LP, t.shape[-1])
            r4 = lambda t: t.reshape(B, LP // KEY_TILE, t.shape[-2], KEY_TILE)
            o = _dsa_core(r3(q), r4(kt), r3(v), r4(qit), r3(ki), r4(wit), rel_bias, n_select, off + S)
            h = _matmul_residual(o.reshape(B * LP, ATT_Q_DIM), dsa_w_out[j].astype(BF16), h)
        h = _moe(h, norm_ffn[i], moe_w_group[i], moe_b_group[i], moe_w_expert[i], moe_b_expert[i],
                 wgu_all, wd_all, i, frames=(LP, S, off) if i == depth - 1 else None)
    return h.reshape(B, S, D)
```

```python
import functools
import math

import jax
import jax.numpy as jnp
from jax import lax
from jax.experimental import pallas as pl
from jax.experimental.pallas import tpu as pltpu

F32 = jnp.float32
BF16 = jnp.bfloat16

LANES = 128
SUBLANES = 8
VMEM_LIMIT_BYTES = 56 * 1024 * 1024

CHUNK = 64
N_META = 16
META_PAD = (-N_META) % CHUNK
RMS_EPS = 1e-6
L2_EPS = 1e-6

GDN_HEADS = 8
GDN_DK = 128
GDN_DV = 128
GDN_CONV = 4
GDN_QK_DIM = GDN_HEADS * GDN_DK
GDN_V_DIM = GDN_HEADS * GDN_DV
GDN_CONV_DIM = 2 * GDN_QK_DIM + GDN_V_DIM
GDN_GATE_OFF = GDN_CONV_DIM + GDN_V_DIM
GDN_PROJ_W = GDN_GATE_OFF + LANES

ATT_HEADS = 8
ATT_KV_HEADS = 2
ATT_GROUP = ATT_HEADS // ATT_KV_HEADS
ATT_HD = 128
ATT_Q_DIM = ATT_HEADS * ATT_HD
ATT_KV_DIM = ATT_KV_HEADS * ATT_HD
IDX_HEADS = 8
IDX_HD = 64
TOPK_MAX = 256
REL_BUCKETS = 32
REL_MAX_DIST = 128

DSA_Q_OFF = 0
DSA_K_OFF = ATT_Q_DIM
DSA_V_OFF = DSA_K_OFF + ATT_KV_DIM
DSA_QI_OFF = DSA_V_OFF + ATT_KV_DIM
DSA_KI_OFF = DSA_QI_OFF + IDX_HEADS * LANES
DSA_WI_OFF = DSA_KI_OFF + LANES
DSA_PROJ_W = DSA_WI_OFF + LANES

N_GROUPS = 4
EXPERTS_PER_GROUP = 8
N_EXPERTS = N_GROUPS * EXPERTS_PER_GROUP
EXPERT_FF = 256

GDN_ROW_TILES = (256, 128, 64)
MOE_ROUTER_TILES = (512, 256, 128)
MOE_ROW_TILES = (512, 256, 128)

ATT_TILE = 256
KEY_TILE = 256
ROW_ALIGN = KEY_TILE
INDEXER_GROUP = 4
ATTEND_GROUP = 2
FRONT_PAD = ROW_ALIGN - N_META
assert FRONT_PAD % CHUNK == META_PAD
BISECT_UNROLL = 4
BISECT_STEPS = 20
SHIFT_SLACK = 1.001
ROW_SUM_FLOOR = 1e-30
NEG_INF = float("-inf")


def _pick_tile(n, candidates):
    for c in candidates:
        if n % c == 0:
            return c
    raise ValueError(f"no tile for {n}")


def _compiler_params(semantics):
    return pltpu.CompilerParams(dimension_semantics=semantics, vmem_limit_bytes=VMEM_LIMIT_BYTES)


def _silu(x):
    return x * jax.nn.sigmoid(x)


def _norm_matmul_kernel(x_ref, g_ref, w_ref, o_ref):
    x = x_ref[...]
    y = x * lax.rsqrt(jnp.mean(x * x, axis=-1, keepdims=True) + RMS_EPS) * g_ref[...]
    o_ref[...] = jnp.dot(y.astype(BF16), w_ref[...], preferred_element_type=F32)


def _matmul_residual_kernel(a_ref, w_ref, r_ref, o_ref):
    o_ref[...] = r_ref[...] + jnp.dot(a_ref[...], w_ref[...], preferred_element_type=F32)


def _matmul_residual(a_bf16, w_bf16, res):
    T, K = a_bf16.shape
    N = w_bf16.shape[1]
    tm = _pick_tile(T, (512, 256, 128))
    return pl.pallas_call(
        _matmul_residual_kernel,
        grid=(T // tm,),
        in_specs=[pl.BlockSpec((tm, K), lambda i: (i, 0)),
                  pl.BlockSpec((K, N), lambda i: (0, 0)),
                  pl.BlockSpec((tm, N), lambda i: (i, 0))],
        out_specs=pl.BlockSpec((tm, N), lambda i: (i, 0)),
        out_shape=jax.ShapeDtypeStruct((T, N), F32),
        compiler_params=_compiler_params(("parallel",)),
        name="matmul_residual",
    )(a_bf16, w_bf16, res)


def _nt_dot(a, b):
    return lax.dot_general(a, b, (((1,), (1,)), ((), ())), preferred_element_type=F32)


def _tn_dot(a, b):
    return lax.dot_general(a, b, (((0,), (0,)), ((), ())), preferred_element_type=F32)


def _gdn_kernel(x_ref, g_ref, w_ref, conv_ref, gate_ref, ogain_ref, o_ref,
                state_ref, xbuf_ref, qkv_ref, proj_ref):
    t = pl.program_id(0)
    nb, rows = x_ref.shape[0], x_ref.shape[1]

    @pl.when(t == 0)
    def _():
        state_ref[...] = jnp.zeros_like(state_ref)
        xbuf_ref[:, 0:SUBLANES, :] = jnp.zeros((nb, SUBLANES, GDN_CONV_DIM), F32)

    for b in range(nb):
        _norm_matmul_kernel(x_ref.at[b], g_ref, w_ref, proj_ref.at[b])

    def chunk(cc, carry):
        r0 = pl.multiple_of(cc * CHUNK, CHUNK)
        _gdn_chunk(t * rows + r0, r0, nb, proj_ref, conv_ref, gate_ref, ogain_ref, o_ref,
                   state_ref, xbuf_ref, qkv_ref)
        return carry

    lax.fori_loop(0, rows // CHUNK, chunk, 0)


def _gdn_chunk(row0, r0, nb, proj_ref, conv_ref, gate_ref, ogain_ref, o_ref, state_ref, xbuf_ref, qkv_ref):
    C = CHUNK
    row = row0 + lax.broadcasted_iota(jnp.int32, (C, 1), 0)
    live = row >= FRONT_PAD
    r_i = lax.broadcasted_iota(jnp.int32, (C, C), 0)
    c_i = lax.broadcasted_iota(jnp.int32, (C, C), 1)
    causal = c_i <= r_i
    strict = c_i < r_i
    neg_rate = -jnp.exp(gate_ref[0:1, :])
    betas, gcs, gc_ts, egs, eg_lasts, eks = [], [], [], [], [], []
    for b in range(nb):
        u = jnp.where(live, proj_ref[b, pl.ds(r0, C), 0:GDN_CONV_DIM], 0.0)
        xbuf_ref[b, SUBLANES:SUBLANES + C, :] = u
        conv = conv_ref[0:1, :] * xbuf_ref[b, SUBLANES - 3:SUBLANES - 3 + C, :]
        for j in range(1, GDN_CONV):
            conv = conv + conv_ref[j:j + 1, :] * xbuf_ref[b, SUBLANES - 3 + j:SUBLANES - 3 + j + C, :]
        xbuf_ref[b, 0:SUBLANES, :] = u[C - SUBLANES:C, :]
        qkv_ref[b] = _silu(conv)

        gates = jnp.where(live, proj_ref[b, pl.ds(r0, C), GDN_GATE_OFF:GDN_GATE_OFF + LANES], 0.0)
        g = neg_rate * jax.nn.softplus(gates + gate_ref[1:2, :])
        gc = jnp.dot(causal.astype(F32), g, preferred_element_type=F32,
                     precision=lax.Precision.HIGHEST)
        g_last = gc[C - 1:C, :]
        betas.append(jax.nn.sigmoid(gates))
        gcs.append(gc)
        gc_ts.append(gc.T)
        egs.append(jnp.exp(gc))
        eg_lasts.append(jnp.exp(g_last))
        eks.append(jnp.exp(g_last - gc))

    chains = [(b, h) for b in range(nb) for h in range(GDN_HEADS)]
    heads = range(len(chains))
    dot = functools.partial(jnp.dot, preferred_element_type=F32)
    col = lambda a, h: a[:, GDN_HEADS + h:GDN_HEADS + h + 1]
    qs, ks, k16s, xs, decays = [], [], [], [], []
    for b, h in chains:
        lo = h * GDN_DK
        q = qkv_ref[b, :, lo:lo + GDN_DK]
        k = qkv_ref[b, :, GDN_QK_DIM + lo:GDN_QK_DIM + lo + GDN_DK]
        v = qkv_ref[b, :, 2 * GDN_QK_DIM + h * GDN_DV:2 * GDN_QK_DIM + (h + 1) * GDN_DV]
        q = q * lax.rsqrt(jnp.sum(q * q, axis=-1, keepdims=True) + L2_EPS) * (GDN_DK ** -0.5)
        k = k * lax.rsqrt(jnp.sum(k * k, axis=-1, keepdims=True) + L2_EPS)
        kb = k * betas[b][:, h:h + 1]
        vb = v * betas[b][:, h:h + 1]
        qs.append(q)
        ks.append(k)
        k16s.append(k.astype(BF16))
        xs.append((kb, jnp.concatenate([vb, kb * col(egs[b], h)], axis=1)))
        decays.append(jnp.exp(jnp.where(causal, col(gcs[b], h) - gc_ts[b][GDN_HEADS + h:GDN_HEADS + h + 1, :],
                                        NEG_INF)))
    kk = [_nt_dot(xs[h][0].astype(BF16), k16s[h]) for h in heads]
    qk = [_nt_dot(qs[h].astype(BF16), k16s[h]) for h in heads]
    ns = [-jnp.where(strict, kk[h] * decays[h], 0.0) for h in heads]
    xs = [x for _, x in xs]
    for it in range(6):
        n16 = [n.astype(BF16) for n in ns]
        xs = [xs[h] + dot(n16[h], xs[h].astype(BF16)) for h in heads]
        if it < 5:
            ns = [dot(n16[h], n16[h]) for h in heads]
    qk = [jnp.where(causal, qk[h] * decays[h], 0.0).astype(BF16) for h in heads]
    states = [state_ref[c] for c in heads]
    s16 = [s.astype(BF16) for s in states]
    ws = [dot(xs[c][:, GDN_DV:2 * GDN_DV].astype(BF16), s16[c]) for c in heads]
    qs_s = [dot((qs[c] * col(egs[b], h)).astype(BF16), s16[c]) for c, (b, h) in enumerate(chains)]
    v_new = [(xs[c][:, 0:GDN_DV] - ws[c]).astype(BF16) for c in heads]
    os_ = [qs_s[c] + dot(qk[c], v_new[c]) for c in heads]
    kv = [_tn_dot((ks[c] * col(eks[b], h)).astype(BF16), v_new[c]) for c, (b, h) in enumerate(chains)]
    for c, (b, h) in enumerate(chains):
        state_ref[c] = states[c] * col(eg_lasts[b], h) + kv[c]
        z = proj_ref[b, pl.ds(r0, C), GDN_CONV_DIM + h * GDN_DV:GDN_CONV_DIM + (h + 1) * GDN_DV]
        o = os_[c]
        on = o * lax.rsqrt(jnp.mean(o * o, axis=-1, keepdims=True) + RMS_EPS) * ogain_ref[...]
        o_ref[b, pl.ds(r0, C), h * GDN_DV:(h + 1) * GDN_DV] = (on * _silu(z)).astype(o_ref.dtype)


def _gdn_first_kernel(x_ref, head_ref, g_ref, w_ref, conv_ref, gate_ref, ogain_ref, o_ref, h_ref, *scratch):
    t = pl.program_id(0)

    @pl.when(t == 0)
    def _():
        h_ref[...] = head_ref[...]

    @pl.when(t > 0)
    def _():
        h_ref[...] = x_ref[...]

    _gdn_kernel(h_ref, g_ref, w_ref, conv_ref, gate_ref, ogain_ref, o_ref, *scratch)


def _gdn_core(src, gain, w_bf16, conv_w, a_log, dt_bias, o_gain, head=None):
    W = w_bf16.shape[1]
    gate = jnp.zeros((2, LANES), F32)
    gate = gate.at[0, GDN_HEADS:2 * GDN_HEADS].set(a_log.astype(F32))
    gate = gate.at[1, GDN_HEADS:2 * GDN_HEADS].set(dt_bias.astype(F32))
    fix = lambda t: (0, 0)
    if head is not None:
        B, S, D = src.shape
        rows = head.shape[1]
        LP = rows + S
        assert S % rows == 0
        scratch = [pltpu.VMEM((B * GDN_HEADS, GDN_DK, GDN_DV), F32),
                   pltpu.VMEM((B, SUBLANES + CHUNK, GDN_CONV_DIM), F32),
                   pltpu.VMEM((B, CHUNK, GDN_CONV_DIM), F32),
                   pltpu.VMEM((B, rows, W), F32)]
        return pl.pallas_call(
            _gdn_first_kernel,
            grid=(LP // rows,),
            in_specs=[pl.BlockSpec((B, rows, D), lambda t: (0, jnp.maximum(t - 1, 0), 0)),
                      pl.BlockSpec((B, rows, D), lambda t: (0, 0, 0), pipeline_mode=pl.Buffered(1)),
                      pl.BlockSpec((1, D), fix),
                      pl.BlockSpec((D, W), fix, pipeline_mode=pl.Buffered(1)),
                      pl.BlockSpec((GDN_CONV, GDN_CONV_DIM), fix),
                      pl.BlockSpec((2, LANES), fix),
                      pl.BlockSpec((1, GDN_DV), fix)],
            out_specs=[pl.BlockSpec((B, rows, GDN_V_DIM), lambda t: (0, t, 0)),
                       pl.BlockSpec((B, rows, D), lambda t: (0, t, 0))],
            out_shape=[jax.ShapeDtypeStruct((B, LP, GDN_V_DIM), BF16),
                       jax.ShapeDtypeStruct((B, LP, D), F32)],
            scratch_shapes=scratch,
            compiler_params=_compiler_params(("arbitrary",)),
            name="gdn_core",
        )(src, head, gain.reshape(1, D), w_bf16, conv_w, gate, o_gain.reshape(1, GDN_DV))
    h3d = src
    B, LP, D = h3d.shape
    rows = _pick_tile(LP, GDN_ROW_TILES)
    return pl.pallas_call(
        _gdn_kernel,
        grid=(LP // rows,),
        in_specs=[pl.BlockSpec((B, rows, D), lambda t: (0, t, 0)),
                  pl.BlockSpec((1, D), fix),
                  pl.BlockSpec((D, W), fix, pipeline_mode=pl.Buffered(1)),
                  pl.BlockSpec((GDN_CONV, GDN_CONV_DIM), fix),
                  pl.BlockSpec((2, LANES), fix),
                  pl.BlockSpec((1, GDN_DV), fix)],
        out_specs=pl.BlockSpec((B, rows, GDN_V_DIM), lambda t: (0, t, 0)),
        out_shape=jax.ShapeDtypeStruct((B, LP, GDN_V_DIM), BF16),
        scratch_shapes=[pltpu.VMEM((B * GDN_HEADS, GDN_DK, GDN_DV), F32),
                        pltpu.VMEM((B, SUBLANES + CHUNK, GDN_CONV_DIM), F32),
                        pltpu.VMEM((B, CHUNK, GDN_CONV_DIM), F32),
                        pltpu.VMEM((B, rows, W), F32)],
        compiler_params=_compiler_params(("arbitrary",)),
        name="gdn_core",
    )(h3d, gain.reshape(1, D), w_bf16, conv_w, gate, o_gain.reshape(1, GDN_DV))


def _moe_router_kernel(h_ref, gain_ref, wr_ref, br_ref, hx_ref, gid_ref, rank_ref, cnt_ref, carry_ref):
    i = pl.program_id(0)
    TM = h_ref.shape[0]
    lane = lax.broadcasted_iota(jnp.int32, (1, LANES), 1)

    @pl.when(i == 0)
    def _():
        carry_ref[...] = jnp.zeros_like(carry_ref)

    x = h_ref[...]
    xn = x * lax.rsqrt(jnp.mean(x * x, axis=-1, keepdims=True) + RMS_EPS) * gain_ref[...]
    logits = jnp.dot(xn, wr_ref[...], preferred_element_type=F32,
                     precision=lax.Precision.HIGHEST) + br_ref[...]
    e_log = logits
    g_lane = lane - N_EXPERTS
    g_log = jnp.where((g_lane >= 0) & (g_lane < N_GROUPS), logits, NEG_INF)
    g_max = jnp.max(g_log, axis=-1, keepdims=True)
    g_sel = jnp.min(jnp.where(g_log == g_max, g_lane, LANES), axis=-1, keepdims=True)
    g_sel = jnp.minimum(g_sel, N_GROUPS - 1)
    g_w = 1.0 / jnp.sum(jnp.exp(g_log - g_max), axis=-1, keepdims=True)
    in_grp = (lane >= g_sel * EXPERTS_PER_GROUP) & (lane < (g_sel + 1) * EXPERTS_PER_GROUP)
    l0 = jnp.where(in_grp, e_log, NEG_INF)
    m1 = jnp.max(l0, axis=-1, keepdims=True)
    i1 = jnp.min(jnp.where(l0 == m1, lane, LANES), axis=-1, keepdims=True)
    l1 = jnp.where(lane == i1, NEG_INF, l0)
    m2 = jnp.max(l1, axis=-1, keepdims=True)
    i2 = jnp.min(jnp.where(l1 == m2, lane, LANES), axis=-1, keepdims=True)
    p2 = jnp.exp(m2 - m1)
    w1 = g_w / (1.0 + p2)
    w2 = g_w * p2 / (1.0 + p2)
    j1 = i1 - g_sel * EXPERTS_PER_GROUP
    j2 = i2 - g_sel * EXPERTS_PER_GROUP
    hx_ref[:, 0:xn.shape[1]] = xn
    hx_ref[:, xn.shape[1]:] = jnp.where(lane == j1, w1, 0.0) + jnp.where(lane == j2, w2, 0.0)

    onehot = jnp.where(lane == g_sel, 1.0, 0.0)
    r_i = lax.broadcasted_iota(jnp.int32, (TM, TM), 0)
    c_i = lax.broadcasted_iota(jnp.int32, (TM, TM), 1)
    before = jnp.where(c_i < r_i, 1.0, 0.0).astype(BF16)
    prior = jnp.dot(before, onehot.astype(BF16), preferred_element_type=F32) + carry_ref[...]
    rank = jnp.sum(jnp.where(lane == g_sel, prior, 0.0), axis=-1, keepdims=True)
    carry_ref[...] += jnp.sum(onehot, axis=0, keepdims=True)
    cnt_ref[...] = carry_ref[...]
    gid_ref[0] = jnp.broadcast_to(g_sel.astype(F32), (TM, LANES)).T[0:1, :].astype(jnp.int32)
    rank_ref[0] = jnp.broadcast_to(rank, (TM, LANES)).T[0:1, :].astype(jnp.int32)


def _moe_dispatch_kernel(seg_ref, pos_ref, hx_ref, xs_ref, zero_ref, sem, zero_sem):
    TM = hx_ref.shape[0]

    @pl.when(pl.program_id(0) == 0)
    def _():
        zero_ref[...] = jnp.zeros_like(zero_ref)

        def zero_tile(t):
            fill = pltpu.make_async_copy(zero_ref, xs_ref.at[pl.ds(t * TM, TM), :], zero_sem)
            fill.start()
            fill.wait()

        for g in range(N_GROUPS):
            first = seg_ref[g - 1] if g else 0
            pl.when(seg_ref[g] > first)(functools.partial(zero_tile, seg_ref[g] - 1))
            tail = seg_ref[N_GROUPS - 1] + g
            pl.when(tail < xs_ref.shape[0] // TM)(functools.partial(zero_tile, tail))

    def start(a, carry):
        for b in range(SUBLANES):
            r = a * SUBLANES + b
            pltpu.make_async_copy(hx_ref.at[pl.ds(r, 1), :], xs_ref.at[pl.ds(pos_ref[0, 0, r], 1), :],
                                  sem).start(priority=b % 2)
        return carry

    lax.fori_loop(0, TM // SUBLANES, start, 0)
    pltpu.make_async_copy(hx_ref, xs_ref.at[pl.ds(0, TM), :], sem).wait()


def _moe_expert_kernel(grp_ref, xs_ref, wgu_ref, wd_ref, ys_ref, xn_ref, acc_ref):
    j = pl.program_id(0)
    e = pl.program_id(1)
    D = xn_ref.shape[1]
    lane = lax.broadcasted_iota(jnp.int32, (1, LANES), 1)

    @pl.when(grp_ref[j] >= 0)
    def _():
        @pl.when(e == 0)
        def _():
            xn_ref[...] = xs_ref[:, 0:D].astype(BF16)
            acc_ref[...] = jnp.zeros_like(acc_ref)

        hh = jnp.dot(xn_ref[...], wgu_ref[0], preferred_element_type=F32)
        cw = jnp.sum(jnp.where(lane == e, xs_ref[:, D:], 0.0), axis=-1, keepdims=True)
        act = _silu(hh[:, 0:EXPERT_FF]) * hh[:, EXPERT_FF:2 * EXPERT_FF] * cw
        acc_ref[...] += jnp.dot(act.astype(BF16), wd_ref[0], preferred_element_type=F32)

    last = e == EXPERTS_PER_GROUP - 1

    @pl.when(last & (grp_ref[j] >= 0))
    def _():
        ys_ref[...] = acc_ref[...]

    @pl.when(last & (grp_ref[j] < 0))
    def _():
        ys_ref[...] = jnp.zeros_like(ys_ref)


def _moe_combine_kernel(pos_ref, h_ref, ys_ref, o_ref, stage_ref, sem):
    TM = h_ref.shape[0]

    def start(a, carry):
        for b in range(SUBLANES):
            r = a * SUBLANES + b
            pltpu.make_async_copy(ys_ref.at[pl.ds(pos_ref[0, 0, r], 1), :], stage_ref.at[pl.ds(r, 1), :],
                                  sem).start(priority=b % 2)
        return carry

    lax.fori_loop(0, TM // SUBLANES, start, 0)
    pltpu.make_async_copy(ys_ref.at[pl.ds(0, TM), :], stage_ref, sem).wait()
    o_ref[...] = h_ref[...] + stage_ref[...]


def _moe(h2d, gain, w_group, b_group, w_expert, b_expert, wgu, wd, layer, frames=None):
    T, D = h2d.shape
    tr = _pick_tile(T, MOE_ROUTER_TILES)
    tm = _pick_tile(T, MOE_ROW_TILES)
    n_tiles = T // tm + N_GROUPS
    XW = D + LANES
    assert N_EXPERTS + N_GROUPS <= LANES
    wr = jnp.zeros((D, LANES), F32)
    wr = wr.at[:, 0:N_EXPERTS].set(w_expert.astype(F32))
    wr = wr.at[:, N_EXPERTS:N_EXPERTS + N_GROUPS].set(w_group.astype(F32))
    br = jnp.zeros((1, LANES), F32)
    br = br.at[0, 0:N_EXPERTS].set(b_expert.astype(F32))
    br = br.at[0, N_EXPERTS:N_EXPERTS + N_GROUPS].set(b_group.astype(F32))

    hx, gid, rank, counts = pl.pallas_call(
        _moe_router_kernel,
        grid=(T // tr,),
        in_specs=[pl.BlockSpec((tr, D), lambda i: (i, 0)),
                  pl.BlockSpec((1, D), lambda i: (0, 0)),
                  pl.BlockSpec((D, LANES), lambda i: (0, 0)),
                  pl.BlockSpec((1, LANES), lambda i: (0, 0))],
        out_specs=[pl.BlockSpec((tr, XW), lambda i: (i, 0)),
                   pl.BlockSpec((1, 1, tr), lambda i: (i, 0, 0)),
                   pl.BlockSpec((1, 1, tr), lambda i: (i, 0, 0)),
                   pl.BlockSpec((1, LANES), lambda i: (0, 0))],
        out_shape=[jax.ShapeDtypeStruct((T, XW), F32),
                   jax.ShapeDtypeStruct((T // tr, 1, tr), jnp.int32),
                   jax.ShapeDtypeStruct((T // tr, 1, tr), jnp.int32),
                   jax.ShapeDtypeStruct((1, LANES), F32)],
        scratch_shapes=[pltpu.VMEM((1, LANES), F32)],
        compiler_params=_compiler_params(("arbitrary",)),
        name="moe_router",
    )(h2d, gain.reshape(1, D), wr, br)

    seg_tiles = -(-counts[0, 0:N_GROUPS].astype(jnp.int32) // tm)
    seg_end = jnp.cumsum(seg_tiles)
    seg_start = seg_end - seg_tiles
    pos = (seg_start * tm)[gid.reshape(T)] + rank.reshape(T)
    tile_ids = jnp.arange(n_tiles, dtype=jnp.int32)
    tile_group = jnp.sum((tile_ids[:, None] >= seg_end[None, :]).astype(jnp.int32), axis=1)
    tile_group = jnp.where(tile_ids < seg_end[N_GROUPS - 1], tile_group, -1)
    pos3 = pos.reshape(T // tm, 1, tm)

    xs = pl.pallas_call(
        _moe_dispatch_kernel,
        grid_spec=pltpu.PrefetchScalarGridSpec(
            num_scalar_prefetch=1,
            grid=(T // tm,),
            in_specs=[pl.BlockSpec((1, 1, tm), lambda i, seg: (i, 0, 0), memory_space=pltpu.SMEM),
                      pl.BlockSpec((tm, XW), lambda i, seg: (i, 0))],
            out_specs=pl.BlockSpec(memory_space=pl.ANY),
            scratch_shapes=[pltpu.VMEM((tm, XW), F32), pltpu.SemaphoreType.DMA(()),
                            pltpu.SemaphoreType.DMA(())]),
        out_shape=jax.ShapeDtypeStruct((n_tiles * tm, XW), F32),
        compiler_params=_compiler_params(("arbitrary",)),
        name="moe_dispatch",
    )(seg_end, pos3, hx)

    def expert_index(j, e, grp):
        return (layer * N_EXPERTS + jnp.maximum(grp[j], 0) * EXPERTS_PER_GROUP + e, 0, 0)

    ys = pl.pallas_call(
        _moe_expert_kernel,
        grid_spec=pltpu.PrefetchScalarGridSpec(
            num_scalar_prefetch=1,
            grid=(n_tiles, EXPERTS_PER_GROUP),
            in_specs=[pl.BlockSpec((tm, XW), lambda j, e, grp: (j, 0)),
                      pl.BlockSpec((1, D, 2 * EXPERT_FF), expert_index),
                      pl.BlockSpec((1, EXPERT_FF, D), expert_index)],
            out_specs=pl.BlockSpec((tm, D), lambda j, e, grp: (j, 0)),
            scratch_shapes=[pltpu.VMEM((tm, D), BF16), pltpu.VMEM((tm, D), F32)]),
        out_shape=jax.ShapeDtypeStruct((n_tiles * tm, D), F32),
        compiler_params=_compiler_params(("parallel", "arbitrary")),
        name="moe_experts",
    )(tile_group, xs, wgu, wd)

    if frames is None:
        tc, out_rows, out_index = tm, T, lambda i: (i, 0)
    else:
        batch_rows, n_frames, first = frames
        tc = ROW_ALIGN
        assert batch_rows % tc == 0 and n_frames % tc == 0 and first % tc == 0 and first >= tc
        out_rows = (T // batch_rows) * n_frames

        def out_index(i):
            b, t = i // (batch_rows // tc), i % (batch_rows // tc)
            return (b * (n_frames // tc) + jnp.maximum(t - first // tc, 0), 0)

    return pl.pallas_call(
        _moe_combine_kernel,
        grid=(T // tc,),
        in_specs=[pl.BlockSpec((1, 1, tc), lambda i: (i, 0, 0), memory_space=pltpu.SMEM),
                  pl.BlockSpec((tc, D), lambda i: (i, 0)),
                  pl.BlockSpec(memory_space=pl.ANY)],
        out_specs=pl.BlockSpec((tc, D), out_index),
        out_shape=jax.ShapeDtypeStruct((out_rows, D), F32),
        scratch_shapes=[pltpu.VMEM((tc, D), F32), pltpu.SemaphoreType.DMA(())],
        compiler_params=_compiler_params(("arbitrary",)),
        name="moe_combine",
    )(pos.reshape(T // tc, 1, tc), h2d, ys)


def _dsa_prep_kernel(p_ref, qg_ref, kg_ref, lng_ref, lnb_ref, q_ref, kt_ref, v_ref, qit_ref, ki_ref, wit_ref):
    lane = lax.broadcasted_iota(jnp.int32, (1, LANES), 1)
    for h in range(ATT_HEADS):
        x = p_ref[:, DSA_Q_OFF + h * ATT_HD:DSA_Q_OFF + (h + 1) * ATT_HD]
        y = x * lax.rsqrt(jnp.mean(x * x, axis=-1, keepdims=True) + RMS_EPS) * qg_ref[...]
        q_ref[:, h * ATT_HD:(h + 1) * ATT_HD] = (y * (ATT_HD ** -0.5)).astype(BF16)
    for n in range(ATT_KV_HEADS):
        x = p_ref[:, DSA_K_OFF + n * ATT_HD:DSA_K_OFF + (n + 1) * ATT_HD]
        y = x * lax.rsqrt(jnp.mean(x * x, axis=-1, keepdims=True) + RMS_EPS) * kg_ref[...]
        kt_ref[0, n * ATT_HD:(n + 1) * ATT_HD, :] = y.T.astype(BF16)
    v_ref[...] = p_ref[:, DSA_V_OFF:DSA_V_OFF + ATT_KV_DIM].astype(BF16)
    qit_ref[0] = p_ref[:, DSA_QI_OFF:DSA_QI_OFF + IDX_HEADS * LANES].T.astype(BF16)
    x = p_ref[:, DSA_KI_OFF:DSA_KI_OFF + LANES]
    live = lane < IDX_HD
    mu = jnp.sum(x, axis=-1, keepdims=True) * (1.0 / IDX_HD)
    xc = jnp.where(live, x - mu, 0.0)
    var = jnp.sum(xc * xc, axis=-1, keepdims=True) * (1.0 / IDX_HD)
    ki = xc * lax.rsqrt(var + RMS_EPS) * lng_ref[...] + lnb_ref[...]
    ki_ref[...] = jnp.where(live, ki, 0.0).astype(BF16)
    wit_ref[0] = (p_ref[:, DSA_WI_OFF:DSA_WI_OFF + LANES] * (IDX_HEADS ** -0.5 * IDX_HD ** -0.5)).T


def _dsa_in_kernel(x_ref, g_ref, w_ref, qg_ref, kg_ref, lng_ref, lnb_ref,
                   q_ref, kt_ref, v_ref, qit_ref, ki_ref, wit_ref, proj_ref):
    _norm_matmul_kernel(x_ref, g_ref, w_ref, proj_ref)
    _dsa_prep_kernel(proj_ref, qg_ref, kg_ref, lng_ref, lnb_ref, q_ref, kt_ref, v_ref, qit_ref, ki_ref, wit_ref)


def _dsa_in(h2d, gain, w_bf16, q_gain, k_gain, ln_g, ln_b):
    T, D = h2d.shape
    W = w_bf16.shape[1]
    tm = KEY_TILE
    assert T % tm == 0 and ATT_TILE == KEY_TILE
    lng = jnp.zeros((1, LANES), F32).at[0, 0:IDX_HD].set(ln_g.astype(F32))
    lnb = jnp.zeros((1, LANES), F32).at[0, 0:IDX_HD].set(ln_b.astype(F32))
    row = lambda i: (i, 0)
    fix = lambda i: (0, 0)
    blk = lambda i: (i, 0, 0)
    return pl.pallas_call(
        _dsa_in_kernel,
        grid=(T // tm,),
        in_specs=[pl.BlockSpec((tm, D), row), pl.BlockSpec((1, D), fix), pl.BlockSpec((D, W), fix),
                  pl.BlockSpec((1, ATT_HD), fix), pl.BlockSpec((1, ATT_HD), fix),
                  pl.BlockSpec((1, LANES), fix), pl.BlockSpec((1, LANES), fix)],
        out_specs=[pl.BlockSpec((tm, ATT_Q_DIM), row), pl.BlockSpec((1, ATT_KV_DIM, tm), blk),
                   pl.BlockSpec((tm, ATT_KV_DIM), row), pl.BlockSpec((1, IDX_HEADS * LANES, tm), blk),
                   pl.BlockSpec((tm, LANES), row), pl.BlockSpec((1, LANES, tm), blk)],
        out_shape=[jax.ShapeDtypeStruct((T, ATT_Q_DIM), BF16), jax.ShapeDtypeStruct((T // tm, ATT_KV_DIM, tm), BF16),
                   jax.ShapeDtypeStruct((T, ATT_KV_DIM), BF16),
                   jax.ShapeDtypeStruct((T // tm, IDX_HEADS * LANES, tm), BF16),
                   jax.ShapeDtypeStruct((T, LANES), BF16), jax.ShapeDtypeStruct((T // tm, LANES, tm), F32)],
        scratch_shapes=[pltpu.VMEM((tm, W), F32)],
        compiler_params=_compiler_params(("parallel",)),
        name="dsa_in",
    )(h2d, gain.reshape(1, D), w_bf16, q_gain.reshape(1, ATT_HD), k_gain.reshape(1, ATT_HD), lng, lnb)


def _dsa_kernel(q_ref, qit_ref, wit_ref, kt_ref, v_ref, ki_ref, bias_ref, bmax_ref, o_ref,
                sc_ref, mask_ref, qg_ref, m_ref, l_ref, acc_ref, knorm_ref, *, n_select, live_rows):
    i = pl.program_id(1)
    TQ = ATT_TILE
    TK = KEY_TILE
    HALVES = TK // LANES

    @pl.when(i == 0)
    def _():
        def body(kt, mx):
            kk = kt_ref[0, kt].astype(F32)
            return tuple(jnp.maximum(mx[n], jnp.sum(jnp.square(kk[n * ATT_HD:(n + 1) * ATT_HD, :]),
                                                    axis=0, keepdims=True)) for n in range(ATT_KV_HEADS))

        mx = lax.fori_loop(0, kt_ref.shape[1], body,
                           tuple(jnp.zeros((1, TK), F32) for _ in range(ATT_KV_HEADS)))
        for n in range(ATT_KV_HEADS):
            knorm_ref[n] = jnp.broadcast_to(jnp.sqrt(jnp.max(mx[n], axis=-1, keepdims=True)), (1, LANES))

    @pl.when(i * TQ >= live_rows)
    def _():
        o_ref[...] = jnp.zeros_like(o_ref)

    @pl.when(i * TQ < live_rows)
    def _():
        _dsa_tile(i, q_ref, qit_ref, wit_ref, kt_ref, v_ref, ki_ref, bias_ref, bmax_ref, o_ref,
                  sc_ref, mask_ref, qg_ref, m_ref, l_ref, acc_ref, knorm_ref, n_select, TQ, TK, HALVES)


def _dsa_tile(i, q_ref, qit_ref, wit_ref, kt_ref, v_ref, ki_ref, bias_ref, bmax_ref, o_ref,
              sc_ref, mask_ref, qg_ref, m_ref, l_ref, acc_ref, knorm_ref, n_select, TQ, TK, HALVES):
    assert TQ == TK
    nkt = i + 1
    fmin = float(jnp.finfo(F32).min)
    kf = float(n_select)

    chunk_shift = int(math.log2(CHUNK))
    q_pos = i * TQ + lax.broadcasted_iota(jnp.int32, (1, TQ), 1)
    q_chunk = jnp.maximum(lax.shift_right_logical(q_pos, chunk_shift), FRONT_PAD // CHUNK)
    k_row = lax.broadcasted_iota(jnp.int32, (TK, 1), 0)

    def halves(x):
        return [x[:, j * LANES:(j + 1) * LANES] for j in range(HALVES)]

    def fold(x, op):
        parts = [x[r * SUBLANES:(r + 1) * SUBLANES, :] for r in range(TK // SUBLANES)]
        while len(parts) > 1:
            parts = [op(parts[j], parts[j + 1]) for j in range(0, len(parts), 2)]
        return parts[0]

    def score_tiles(kts):
        for kt in kts:
            keys = ki_ref[0, pl.ds(pl.multiple_of(kt * TK, TK), TK), :]
            acc = None
            for h in range(IDX_HEADS):
                d = jnp.dot(keys, qit_ref[0, 0, h * LANES:(h + 1) * LANES, :], preferred_element_type=F32)
                term = wit_ref[0, 0, h:h + 1, :] * jnp.maximum(d, 0.0)
                acc = term if acc is None else acc + term
            sc_ref[kt] = acc
            mask_ref[kt] = acc.T

    def grouped_loop(n, fn, group):
        def body(j, carry):
            fn([group * j + r for r in range(group)])
            return carry

        lax.fori_loop(0, lax.shift_right_logical(n, int(math.log2(group))), body, 0)
        size = group // 2
        while size >= 1:
            first = lax.bitwise_and(n, ~(2 * size - 1))

            @pl.when(lax.bitwise_and(n, size) != 0)
            def _(size=size, first=first):
                fn([first + r for r in range(size)])

            size //= 2

    grouped_loop(nkt, score_tiles, INDEXER_GROUP)

    def mask_inadmissible(kt):
        k_pos = kt * TK + k_row
        adm = (lax.shift_right_logical(k_pos, chunk_shift) <= q_chunk) & (k_pos >= FRONT_PAD)
        sc_ref[kt] = jnp.where(adm, sc_ref[kt], NEG_INF)

    mask_inadmissible(0)

    @pl.when(nkt > 1)
    def _():
        mask_inadmissible(nkt - 1)

    n_adm = CHUNK * (q_chunk + 1) - FRONT_PAD
    need = n_adm > n_select

    def count_where(pred):
        def one(kt):
            return fold(jnp.where(pred(sc_ref[kt], kt), 1.0, 0.0), jnp.add)

        def two(j, acc):
            return acc + (one(2 * j) + one(2 * j + 1))

        acc = lax.fori_loop(0, lax.shift_right_logical(nkt, 1), two, jnp.zeros((SUBLANES, TQ), F32))
        acc = lax.cond(lax.bitwise_and(nkt, 1) == 1, lambda a: a + one(nkt - 1), lambda a: a, acc)
        return jnp.sum(acc, axis=0, keepdims=True)

    def count_ge(t):
        return count_where(lambda s, kt: s >= t)

    def minmax_body(kt, carry):
        mn, mx = carry
        s = sc_ref[kt]
        mn = jnp.minimum(mn, fold(jnp.where(s == NEG_INF, jnp.inf, s), jnp.minimum))
        mx = jnp.maximum(mx, fold(s, jnp.maximum))
        return mn, mx

    mn_l, mx_l = lax.fori_loop(0, nkt, minmax_body,
                               (jnp.full((SUBLANES, TQ), jnp.inf, F32), jnp.full((SUBLANES, TQ), NEG_INF, F32)))
    s_min = jnp.min(mn_l, axis=0, keepdims=True)
    s_max = jnp.max(mx_l, axis=0, keepdims=True)
    lo0 = s_min
    hi0 = s_max + (s_max - s_min) + 1.0
    cnt0 = n_adm.astype(F32)
    active0 = jnp.where(need, 1.0, 0.0)

    def any_active(active):
        return jnp.max(active) > 0.0

    def bis_step(lo, hi, cnt, active):
        mid = lo + (hi - lo) * 0.5
        c = count_ge(mid)
        ge = c >= kf
        moving = (mid > lo) & (mid < hi) & (active > 0.0)
        lo_n = jnp.where(moving & ge, mid, lo)
        hi_n = jnp.where(moving & (~ge), mid, hi)
        cnt_n = jnp.where(moving & ge, c, cnt)
        active_n = jnp.where(moving & (cnt_n != kf), 1.0, 0.0)
        return lo_n, hi_n, cnt_n, active_n

    def bis_body(st):
        it, rest = st[0], st[1:]
        for _ in range(BISECT_UNROLL):
            rest = bis_step(*rest)
        return (it + BISECT_UNROLL,) + rest

    _, lo, hi, cnt, _ = lax.while_loop(
        lambda st: jnp.logical_and(st[0] < BISECT_STEPS, any_active(st[4])), bis_body,
        (jnp.int32(0), lo0, hi0, cnt0, active0))

    def snap_body(st):
        lo, hi, cnt, active = st

        def body(kt, mx):
            s = sc_ref[kt]
            return jnp.maximum(mx, fold(jnp.where(s < hi, s, NEG_INF), jnp.maximum))

        v1 = jnp.max(lax.fori_loop(0, nkt, body, jnp.full((SUBLANES, TQ), NEG_INF, F32)),
                     axis=0, keepdims=True)
        c1 = count_ge(v1)
        on = active > 0.0
        hit = on & (c1 >= kf)
        lo_n = jnp.where(hit, v1, lo)
        cnt_n = jnp.where(hit, c1, cnt)
        hi_n = jnp.where(on & (~hit), v1, hi)
        return lo_n, hi_n, cnt_n, jnp.where(on & (~hit), 1.0, 0.0)

    unsettled = jnp.where(need & (cnt != kf), 1.0, 0.0)
    lo, hi, cnt, _ = lax.while_loop(lambda st: any_active(st[3]), snap_body, (lo, hi, cnt, unsettled))

    thr = jnp.where(need, lo, fmin)
    tied = need & (cnt > kf)
    big_pos = jnp.int32(2 ** 30)

    def tie_break():
        want = kf - count_where(lambda s, kt: s > thr)

        def body(_, st):
            jl, jh = st
            jm = lax.shift_right_arithmetic(jl + jh, 1)
            c = count_where(lambda s, kt: (s == thr) & (kt * TK + k_row <= jm))
            ok = c >= want
            return jnp.where(ok, jl, jm), jnp.where(ok, jm, jh)

        n_keys = sc_ref.shape[0] * TK
        steps = int(math.ceil(math.log2(n_keys + 1))) + 1
        jl0 = jnp.full((1, TQ), -1, jnp.int32)
        jh0 = jnp.full((1, TQ), n_keys, jnp.int32)
        _, jh = lax.fori_loop(0, steps, body, (jl0, jh0))
        return jnp.where(tied, jh, big_pos)

    pos_cap = lax.cond(jnp.max(jnp.where(tied, 1.0, 0.0)) > 0.0, tie_break,
                       lambda: jnp.full((1, TQ), big_pos, jnp.int32))

    def lane_wide(row):
        return jnp.broadcast_to(row, (LANES, TQ)).T

    thr_w = lane_wide(thr)
    cap_w = lane_wide(pos_cap)
    chunk_w = lane_wide(q_chunk)
    k_lane = lax.broadcasted_iota(jnp.int32, (1, LANES), 1)

    def to_mask(kt, edge_tile):
        out = []
        for j, s in enumerate(halves(mask_ref[kt])):
            k_pos = kt * TK + j * LANES + k_lane
            sel = (s > thr_w) | ((s == thr_w) & (k_pos <= cap_w))
            if edge_tile:
                sel = sel & (lax.shift_right_logical(k_pos, chunk_shift) <= chunk_w) & (k_pos >= FRONT_PAD)
            out.append(jnp.where(sel, 0.0, NEG_INF))
        mask_ref[kt] = jnp.concatenate(out, axis=1)

    def inner_mask(kt, carry):
        to_mask(kt, False)
        return carry

    lax.fori_loop(1, nkt - 1, inner_mask, 0)
    to_mask(0, True)

    @pl.when(nkt > 1)
    def _():
        to_mask(nkt - 1, True)

    for n in range(ATT_KV_HEADS):
        for g in range(ATT_GROUP):
            h = n * ATT_GROUP + g
            qg_ref[n, g * TQ:(g + 1) * TQ, :] = q_ref[0, :, h * ATT_HD:(h + 1) * ATT_HD]
    l_ref[...] = jnp.zeros_like(l_ref)
    acc_ref[...] = jnp.zeros_like(acc_ref)

    kv_heads = range(ATT_KV_HEADS)

    def raw_logits(kts):
        return [[jnp.dot(qg_ref[n], kt_ref[0, kt, n * ATT_HD:(n + 1) * ATT_HD, :], preferred_element_type=F32)
                 for n in kv_heads] for kt in kts]

    def masked_logits(s, mask, n, g, b0):
        sg = s[g * TQ:(g + 1) * TQ, :] + mask
        if b0 is not None:
            h = n * ATT_GROUP + g
            sg = sg + jnp.concatenate([bias_ref[h, b0 + j] for j in range(HALVES)], axis=1)
        return sg

    def max_pass(kts, b0):
        raw = raw_logits(kts)
        masks = [mask_ref[kt] for kt in kts]
        for n in kv_heads:
            for g in range(ATT_GROUP):
                rows = slice(g * TQ, (g + 1) * TQ)
                m = m_ref[n, rows, :]
                for t in range(len(kts)):
                    for part in halves(masked_logits(raw[t][n], masks[t], n, g, b0)):
                        m = jnp.maximum(m, part)
                m_ref[n, rows, :] = m

    def exp_pass(kts, b0):
        raw = raw_logits(kts)
        masks = [mask_ref[kt] for kt in kts]
        for n in kv_heads:
            ps = [[] for _ in kts]
            for g in range(ATT_GROUP):
                rows = slice(g * TQ, (g + 1) * TQ)
                m = m_ref[n, rows, :]
                l_new = l_ref[n, rows, :]
                for t in range(len(kts)):
                    parts = [jnp.exp(part - m) for part in halves(masked_logits(raw[t][n], masks[t], n, g, b0))]
                    l_new = l_new + functools.reduce(lambda a, b: a + b, parts)
                    ps[t].append(jnp.concatenate(parts, axis=1).astype(BF16))
                l_ref[n, rows, :] = l_new
            pv = None
            for t, kt in enumerate(kts):
                vv = v_ref[0, pl.ds(pl.multiple_of(kt * TK, TK), TK), n * ATT_HD:(n + 1) * ATT_HD]
                d = jnp.dot(jnp.concatenate(ps[t], axis=0), vv, preferred_element_type=F32)
                pv = d if pv is None else pv + d
            acc_ref[n] += pv

    def run_pass(tile_fn):
        grouped_loop(jnp.maximum(i - 1, 0), lambda kts: tile_fn(kts, None), ATTEND_GROUP)

        @pl.when(i >= 1)
        def _():
            tile_fn([i - 1], 0)

        tile_fn([i], HALVES)

    for n in range(ATT_KV_HEADS):
        for g in range(ATT_GROUP):
            h = n * ATT_GROUP + g
            qh = qg_ref[n, g * TQ:(g + 1) * TQ, :].astype(F32)
            q_len = jnp.sqrt(jnp.sum(qh * qh, axis=-1, keepdims=True))
            bound = q_len * knorm_ref[n] * SHIFT_SLACK + (bmax_ref[h] + SHIFT_SLACK - 1.0)
            m_ref[n, g * TQ:(g + 1) * TQ, :] = jnp.broadcast_to(bound, (TQ, LANES))
    run_pass(exp_pass)

    def smallest_row_sum():
        return functools.reduce(jnp.minimum, [jnp.min(jnp.sum(l_ref[n], axis=-1, keepdims=True))
                                              for n in range(ATT_KV_HEADS)])

    @pl.when(jnp.logical_not(smallest_row_sum() > ROW_SUM_FLOOR))
    def _():
        m_ref[...] = jnp.full(m_ref.shape, -1e30, F32)
        run_pass(max_pass)
        for n in range(ATT_KV_HEADS):
            m_ref[n] = jnp.broadcast_to(jnp.max(m_ref[n], axis=-1, keepdims=True), m_ref.shape[1:])
        l_ref[...] = jnp.zeros_like(l_ref)
        acc_ref[...] = jnp.zeros_like(acc_ref)
        run_pass(exp_pass)

    for n in range(ATT_KV_HEADS):
        inv_l = 1.0 / jnp.sum(l_ref[n], axis=-1, keepdims=True)
        for g in range(ATT_GROUP):
            h = n * ATT_GROUP + g
            rows = slice(g * TQ, (g + 1) * TQ)
            o_ref[0, :, h * ATT_HD:(h + 1) * ATT_HD] = (acc_ref[n, rows, :] * inv_l[rows, :]).astype(o_ref.dtype)


def _t5_bucket(rel):
    nb = REL_BUCKETS // 2
    max_exact = nb // 2
    ret = jnp.where(rel > 0, nb, 0)
    n = jnp.abs(rel)
    nf = jnp.maximum(n, 1).astype(jnp.float32)
    large = max_exact + (jnp.log(nf / max_exact) / math.log(REL_MAX_DIST / max_exact)
                         * (nb - max_exact)).astype(jnp.int32)
    large = jnp.minimum(large, nb - 1)
    return ret + jnp.where(n < max_exact, n, large)


def _dsa_core(q, kt, v, qit, ki, wit, rel_bias, n_select, live_rows):
    B, LP, _ = q.shape
    nq = LP // ATT_TILE
    nkt = LP // KEY_TILE
    a = jnp.arange(ATT_TILE)[:, None]
    xx = jnp.arange(2 * KEY_TILE)[None, :]
    rel = (xx - KEY_TILE) - a
    far = rel_bias.astype(F32)[REL_BUCKETS // 2 - 1]
    onehot = jax.nn.one_hot(_t5_bucket(rel), REL_BUCKETS, dtype=F32)
    table = jnp.einsum("abk,kh->abh", onehot, rel_bias.astype(F32),
                       precision=lax.Precision.HIGHEST) - far
    bias_near = jnp.transpose(table.reshape(ATT_TILE, 2 * KEY_TILE // LANES, LANES, ATT_HEADS), (3, 1, 0, 2))
    bias_max = jnp.broadcast_to(jnp.maximum(jnp.max(table, axis=(0, 1)), 0.0)[:, None, None],
                                (ATT_HEADS, 1, LANES))
    kern = functools.partial(_dsa_kernel, n_select=n_select, live_rows=live_rows)
    tile = lambda b, i: (b, i, 0)
    tile4 = lambda b, i: (b, i, 0, 0)
    whole3 = lambda b, i: (b, 0, 0)
    whole4 = lambda b, i: (b, 0, 0, 0)
    once = pl.Buffered(1)
    return pl.pallas_call(
        kern,
        grid=(B, nq),
        in_specs=[pl.BlockSpec((1, ATT_TILE, ATT_Q_DIM), tile),
                  pl.BlockSpec((1, 1, IDX_HEADS * LANES, ATT_TILE), tile4),
                  pl.BlockSpec((1, 1, LANES, ATT_TILE), tile4),
                  pl.BlockSpec((1, nkt, ATT_KV_DIM, KEY_TILE), whole4, pipeline_mode=once),
                  pl.BlockSpec((1, LP, ATT_KV_DIM), whole3, pipeline_mode=once),
                  pl.BlockSpec((1, LP, LANES), whole3, pipeline_mode=once),
                  pl.BlockSpec((ATT_HEADS, 2 * KEY_TILE // LANES, ATT_TILE, LANES), lambda b, i: (0, 0, 0, 0),
                               pipeline_mode=once),
                  pl.BlockSpec((ATT_HEADS, 1, LANES), lambda b, i: (0, 0, 0))],
        out_specs=pl.BlockSpec((1, ATT_TILE, ATT_Q_DIM), tile),
        out_shape=jax.ShapeDtypeStruct((B, LP, ATT_Q_DIM), BF16),
        scratch_shapes=[pltpu.VMEM((nkt, KEY_TILE, ATT_TILE), F32),
                        pltpu.VMEM((nkt, ATT_TILE, KEY_TILE), F32),
                        pltpu.VMEM((ATT_KV_HEADS, ATT_GROUP * ATT_TILE, ATT_HD), BF16),
                        pltpu.VMEM((ATT_KV_HEADS, ATT_GROUP * ATT_TILE, LANES), F32),
                        pltpu.VMEM((ATT_KV_HEADS, ATT_GROUP * ATT_TILE, LANES), F32),
                        pltpu.VMEM((ATT_KV_HEADS, ATT_GROUP * ATT_TILE, ATT_HD), F32),
                        pltpu.VMEM((ATT_KV_HEADS, 1, LANES), F32)],
        compiler_params=_compiler_params(("arbitrary", "arbitrary")),
        name="dsa_core",
    )(q, qit, wit, kt, v, ki, bias_near, bias_max)


def _gdn_in_weight(w_in):
    D = w_in.shape[0]
    w = jnp.zeros((D, GDN_PROJ_W), F32).at[:, 0:w_in.shape[1]].set(w_in.astype(F32))
    return w.astype(BF16)


def _dsa_in_weight(w_in):
    D = w_in.shape[0]
    w_in = w_in.astype(BF16)
    zeros = lambda n: jnp.zeros((D, n), BF16)
    src = ATT_Q_DIM + 2 * ATT_KV_DIM
    parts = [w_in[:, 0:src]]
    for h in range(IDX_HEADS):
        parts += [w_in[:, src + h * IDX_HD:src + (h + 1) * IDX_HD], zeros(LANES - IDX_HD)]
    src += IDX_HEADS * IDX_HD
    parts += [w_in[:, src:src + IDX_HD], zeros(LANES - IDX_HD),
              w_in[:, src + IDX_HD:src + IDX_HD + IDX_HEADS], zeros(LANES - IDX_HEADS)]
    w = jnp.concatenate(parts, axis=1)
    assert w.shape[1] == DSA_PROJ_W
    return w


def kernel(x, meta_tokens, norm_mix, norm_ffn, rel_bias, gdn_w_in, gdn_conv, gdn_a_log, gdn_dt_bias,
           gdn_o_norm, gdn_w_out, dsa_w_in, dsa_q_norm, dsa_k_norm, dsa_idx_ln_g, dsa_idx_ln_b, dsa_w_out,
           moe_w_group, moe_b_group, moe_w_expert, moe_b_expert, moe_w_gate_up, moe_w_down):
    B, S, D = x.shape
    depth = norm_mix.shape[0]
    n_select = min(TOPK_MAX, S // 4)
    off = FRONT_PAD + N_META
    LP = -(-(off + S) // ROW_ALIGN) * ROW_ALIGN
    assert LP == off + S, "frames must fill whole attention tiles"
    meta = jnp.broadcast_to(meta_tokens.astype(x.dtype)[None], (B, N_META, D))
    head = jnp.concatenate([jnp.zeros((B, FRONT_PAD, D), x.dtype), meta], axis=1)
    wgu_all = moe_w_gate_up.astype(BF16).reshape(depth * N_EXPERTS, D, 2 * EXPERT_FF)
    wd_all = moe_w_down.astype(BF16).reshape(depth * N_EXPERTS, EXPERT_FF, D)
    h = None
    for i in range(depth):
        j = i // 2
        if i % 2 == 0:
            gdn_args = (norm_mix[i], _gdn_in_weight(gdn_w_in[j]), gdn_conv[j].astype(F32), gdn_a_log[j],
                        gdn_dt_bias[j], gdn_o_norm[j])
            if i == 0:
                o, h = _gdn_core(x, *gdn_args, head=head)
                h = h.reshape(B * LP, D)
            else:
                o = _gdn_core(h.reshape(B, LP, D), *gdn_args)
            h = _matmul_residual(o.reshape(B * LP, GDN_V_DIM), gdn_w_out[j].astype(BF16), h)
        else:
            q, kt, v, qit, ki, wit = _dsa_in(h, norm_mix[i], _dsa_in_weight(dsa_w_in[j]), dsa_q_norm[j],
                                             dsa_k_norm[j], dsa_idx_ln_g[j], dsa_idx_ln_b[j])
            r3 = lambda t: t.reshape(B, LP, t.shape[-1])
            r4 = lambda t: t.reshape(B, LP // KEY_TILE, t.shape[-2], KEY_TILE)
            o = _dsa_core(r3(q), r4(kt), r3(v), r4(qit), r3(ki), r4(wit), rel_bias, n_select, off + S)
            h = _matmul_residual(o.reshape(B * LP, ATT_Q_DIM), dsa_w_out[j].astype(BF16), h)
        h = _moe(h, norm_ffn[i], moe_w_group[i], moe_b_group[i], moe_w_expert[i], moe_b_expert[i],
                 wgu_all, wd_all, i, frames=(LP, S, off) if i == depth - 1 else None)
    return h.reshape(B, S, D)
```

```python
import functools
import math

import jax
import jax.numpy as jnp
from jax import lax
from jax.experimental import pallas as pl
from jax.experimental.pallas import tpu as pltpu

F32 = jnp.float32
BF16 = jnp.bfloat16

LANES = 128
SUBLANES = 8
VMEM_LIMIT_BYTES = 56 * 1024 * 1024

CHUNK = 64
N_META = 16
META_PAD = (-N_META) % CHUNK
RMS_EPS = 1e-6
L2_EPS = 1e-6

GDN_HEADS = 8
GDN_DK = 128
GDN_DV = 128
GDN_CONV = 4
GDN_QK_DIM = GDN_HEADS * GDN_DK
GDN_V_DIM = GDN_HEADS * GDN_DV
GDN_CONV_DIM = 2 * GDN_QK_DIM + GDN_V_DIM
GDN_GATE_OFF = GDN_CONV_DIM + GDN_V_DIM
GDN_PROJ_W = GDN_GATE_OFF + LANES

ATT_HEADS = 8
ATT_KV_HEADS = 2
ATT_GROUP = ATT_HEADS // ATT_KV_HEADS
ATT_HD = 128
ATT_Q_DIM = ATT_HEADS * ATT_HD
ATT_KV_DIM = ATT_KV_HEADS * ATT_HD
IDX_HEADS = 8
IDX_HD = 64
TOPK_MAX = 256
REL_BUCKETS = 32
REL_MAX_DIST = 128

DSA_Q_OFF = 0
DSA_K_OFF = ATT_Q_DIM
DSA_V_OFF = DSA_K_OFF + ATT_KV_DIM
DSA_QI_OFF = DSA_V_OFF + ATT_KV_DIM
DSA_KI_OFF = DSA_QI_OFF + IDX_HEADS * LANES
DSA_WI_OFF = DSA_KI_OFF + LANES
DSA_PROJ_W = DSA_WI_OFF + LANES

N_GROUPS = 4
EXPERTS_PER_GROUP = 8
N_EXPERTS = N_GROUPS * EXPERTS_PER_GROUP
EXPERT_FF = 256

GDN_ROW_TILES = (256, 128, 64)
MOE_ROUTER_TILES = (512, 256, 128)
MOE_ROW_TILES = (512, 256, 128)

ATT_TILE = 256
KEY_TILE = 256
ROW_ALIGN = KEY_TILE
INDEXER_GROUP = 4
ATTEND_GROUP = 2
FRONT_PAD = ROW_ALIGN - N_META
assert FRONT_PAD % CHUNK == META_PAD
BISECT_UNROLL = 4
BISECT_STEPS = 20
SHIFT_SLACK = 1.001
ROW_SUM_FLOOR = 1e-30
NEG_INF = float("-inf")


def _pick_tile(n, candidates):
    for c in candidates:
        if n % c == 0:
            return c
    raise ValueError(f"no tile for {n}")


def _compiler_params(semantics):
    return pltpu.CompilerParams(dimension_semantics=semantics, vmem_limit_bytes=VMEM_LIMIT_BYTES)


def _silu(x):
    return x * jax.nn.sigmoid(x)


def _norm_matmul_kernel(x_ref, g_ref, w_ref, o_ref):
    x = x_ref[...]
    y = x * lax.rsqrt(jnp.mean(x * x, axis=-1, keepdims=True) + RMS_EPS) * g_ref[...]
    o_ref[...] = jnp.dot(y.astype(BF16), w_ref[...], preferred_element_type=F32)


def _matmul_residual_kernel(a_ref, w_ref, r_ref, o_ref):
    o_ref[...] = r_ref[...] + jnp.dot(a_ref[...], w_ref[...], preferred_element_type=F32)


def _matmul_residual(a_bf16, w_bf16, res):
    T, K = a_bf16.shape
    N = w_bf16.shape[1]
    tm = _pick_tile(T, (512, 256, 128))
    return pl.pallas_call(
        _matmul_residual_kernel,
        grid=(T // tm,),
        in_specs=[pl.BlockSpec((tm, K), lambda i: (i, 0)),
                  pl.BlockSpec((K, N), lambda i: (0, 0)),
                  pl.BlockSpec((tm, N), lambda i: (i, 0))],
        out_specs=pl.BlockSpec((tm, N), lambda i: (i, 0)),
        out_shape=jax.ShapeDtypeStruct((T, N), F32),
        compiler_params=_compiler_params(("parallel",)),
        name="matmul_residual",
    )(a_bf16, w_bf16, res)


def _nt_dot(a, b):
    return lax.dot_general(a, b, (((1,), (1,)), ((), ())), preferred_element_type=F32)


def _tn_dot(a, b):
    return lax.dot_general(a, b, (((0,), (0,)), ((), ())), preferred_element_type=F32)


def _gdn_kernel(x_ref, g_ref, w_ref, conv_ref, gate_ref, ogain_ref, o_ref,
                state_ref, xbuf_ref, qkv_ref, proj_ref):
    t = pl.program_id(0)
    nb, rows = x_ref.shape[0], x_ref.shape[1]

    @pl.when(t == 0)
    def _():
        state_ref[...] = jnp.zeros_like(state_ref)
        xbuf_ref[:, 0:SUBLANES, :] = jnp.zeros((nb, SUBLANES, GDN_CONV_DIM), F32)

    for b in range(nb):
        _norm_matmul_kernel(x_ref.at[b], g_ref, w_ref, proj_ref.at[b])

    def chunk(cc, carry):
        r0 = pl.multiple_of(cc * CHUNK, CHUNK)
        _gdn_chunk(t * rows + r0, r0, nb, proj_ref, conv_ref, gate_ref, ogain_ref, o_ref,
                   state_ref, xbuf_ref, qkv_ref)
        return carry

    lax.fori_loop(0, rows // CHUNK, chunk, 0)


def _gdn_chunk(row0, r0, nb, proj_ref, conv_ref, gate_ref, ogain_ref, o_ref, state_ref, xbuf_ref, qkv_ref):
    C = CHUNK
    row = row0 + lax.broadcasted_iota(jnp.int32, (C, 1), 0)
    live = row >= FRONT_PAD
    r_i = lax.broadcasted_iota(jnp.int32, (C, C), 0)
    c_i = lax.broadcasted_iota(jnp.int32, (C, C), 1)
    causal = c_i <= r_i
    strict = c_i < r_i
    neg_rate = -jnp.exp(gate_ref[0:1, :])
    betas, gcs, gc_ts, egs, eg_lasts, eks = [], [], [], [], [], []
    for b in range(nb):
        u = jnp.where(live, proj_ref[b, pl.ds(r0, C), 0:GDN_CONV_DIM], 0.0)
        xbuf_ref[b, SUBLANES:SUBLANES + C, :] = u
        conv = conv_ref[0:1, :] * xbuf_ref[b, SUBLANES - 3:SUBLANES - 3 + C, :]
        for j in range(1, GDN_CONV):
            conv = conv + conv_ref[j:j + 1, :] * xbuf_ref[b, SUBLANES - 3 + j:SUBLANES - 3 + j + C, :]
        xbuf_ref[b, 0:SUBLANES, :] = u[C - SUBLANES:C, :]
        qkv_ref[b] = _silu(conv)

        gates = jnp.where(live, proj_ref[b, pl.ds(r0, C), GDN_GATE_OFF:GDN_GATE_OFF + LANES], 0.0)
        g = neg_rate * jax.nn.softplus(gates + gate_ref[1:2, :])
        gc = jnp.dot(causal.astype(F32), g, preferred_element_type=F32,
                     precision=lax.Precision.HIGHEST)
        g_last = gc[C - 1:C, :]
        betas.append(jax.nn.sigmoid(gates))
        gcs.append(gc)
        gc_ts.append(gc.T)
        egs.append(jnp.exp(gc))
        eg_lasts.append(jnp.exp(g_last))
        eks.append(jnp.exp(g_last - gc))

    chains = [(b, h) for b in range(nb) for h in range(GDN_HEADS)]
    heads = range(len(chains))
    dot = functools.partial(jnp.dot, preferred_element_type=F32)
    col = lambda a, h: a[:, GDN_HEADS + h:GDN_HEADS + h + 1]
    qs, ks, k16s, xs, decays = [], [], [], [], []
    for b, h in chains:
        lo = h * GDN_DK
        q = qkv_ref[b, :, lo:lo + GDN_DK]
        k = qkv_ref[b, :, GDN_QK_DIM + lo:GDN_QK_DIM + lo + GDN_DK]
        v = qkv_ref[b, :, 2 * GDN_QK_DIM + h * GDN_DV:2 * GDN_QK_DIM + (h + 1) * GDN_DV]
        q = q * lax.rsqrt(jnp.sum(q * q, axis=-1, keepdims=True) + L2_EPS) * (GDN_DK ** -0.5)
        k = k * lax.rsqrt(jnp.sum(k * k, axis=-1, keepdims=True) + L2_EPS)
        kb = k * betas[b][:, h:h + 1]
        vb = v * betas[b][:, h:h + 1]
        qs.append(q)
        ks.append(k)
        k16s.append(k.astype(BF16))
        xs.append((kb, jnp.concatenate([vb, kb * col(egs[b], h)], axis=1)))
        decays.append(jnp.exp(jnp.where(causal, col(gcs[b], h) - gc_ts[b][GDN_HEADS + h:GDN_HEADS + h + 1, :],
                                        NEG_INF)))
    kk = [_nt_dot(xs[h][0].astype(BF16), k16s[h]) for h in heads]
    qk = [_nt_dot(qs[h].astype(BF16), k16s[h]) for h in heads]
    ns = [-jnp.where(strict, kk[h] * decays[h], 0.0) for h in heads]
    xs = [x for _, x in xs]
    for it in range(6):
        n16 = [n.astype(BF16) for n in ns]
        xs = [xs[h] + dot(n16[h], xs[h].astype(BF16)) for h in heads]
        if it < 5:
            ns = [dot(n16[h], n16[h]) for h in heads]
    qk = [jnp.where(causal, qk[h] * decays[h], 0.0).astype(BF16) for h in heads]
    states = [state_ref[c] for c in heads]
    s16 = [s.astype(BF16) for s in states]
    ws = [dot(xs[c][:, GDN_DV:2 * GDN_DV].astype(BF16), s16[c]) for c in heads]
    qs_s = [dot((qs[c] * col(egs[b], h)).astype(BF16), s16[c]) for c, (b, h) in enumerate(chains)]
    v_new = [(xs[c][:, 0:GDN_DV] - ws[c]).astype(BF16) for c in heads]
    os_ = [qs_s[c] + dot(qk[c], v_new[c]) for c in heads]
    kv = [_tn_dot((ks[c] * col(eks[b], h)).astype(BF16), v_new[c]) for c, (b, h) in enumerate(chains)]
    for c, (b, h) in enumerate(chains):
        state_ref[c] = states[c] * col(eg_lasts[b], h) + kv[c]
        z = proj_ref[b, pl.ds(r0, C), GDN_CONV_DIM + h * GDN_DV:GDN_CONV_DIM + (h + 1) * GDN_DV]
        o = os_[c]
        on = o * lax.rsqrt(jnp.mean(o * o, axis=-1, keepdims=True) + RMS_EPS) * ogain_ref[...]
        o_ref[b, pl.ds(r0, C), h * GDN_DV:(h + 1) * GDN_DV] = (on * _silu(z)).astype(o_ref.dtype)


def _gdn_first_kernel(x_ref, head_ref, g_ref, w_ref, conv_ref, gate_ref, ogain_ref, o_ref, h_ref, *scratch):
    t = pl.program_id(0)

    @pl.when(t == 0)
    def _():
        h_ref[...] = head_ref[...]

    @pl.when(t > 0)
    def _():
        h_ref[...] = x_ref[...]

    _gdn_kernel(h_ref, g_ref, w_ref, conv_ref, gate_ref, ogain_ref, o_ref, *scratch)


def _gdn_core(src, gain, w_bf16, conv_w, a_log, dt_bias, o_gain, head=None):
    W = w_bf16.shape[1]
    gate = jnp.zeros((2, LANES), F32)
    gate = gate.at[0, GDN_HEADS:2 * GDN_HEADS].set(a_log.astype(F32))
    gate = gate.at[1, GDN_HEADS:2 * GDN_HEADS].set(dt_bias.astype(F32))
    fix = lambda t: (0, 0)
    if head is not None:
        B, S, D = src.shape
        rows = head.shape[1]
        LP = rows + S
        assert S % rows == 0
        scratch = [pltpu.VMEM((B * GDN_HEADS, GDN_DK, GDN_DV), F32),
                   pltpu.VMEM((B, SUBLANES + CHUNK, GDN_CONV_DIM), F32),
                   pltpu.VMEM((B, CHUNK, GDN_CONV_DIM), F32),
                   pltpu.VMEM((B, rows, W), F32)]
        return pl.pallas_call(
            _gdn_first_kernel,
            grid=(LP // rows,),
            in_specs=[pl.BlockSpec((B, rows, D), lambda t: (0, jnp.maximum(t - 1, 0), 0)),
                      pl.BlockSpec((B, rows, D), lambda t: (0, 0, 0), pipeline_mode=pl.Buffered(1)),
                      pl.BlockSpec((1, D), fix),
                      pl.BlockSpec((D, W), fix, pipeline_mode=pl.Buffered(1)),
                      pl.BlockSpec((GDN_CONV, GDN_CONV_DIM), fix),
                      pl.BlockSpec((2, LANES), fix),
                      pl.BlockSpec((1, GDN_DV), fix)],
            out_specs=[pl.BlockSpec((B, rows, GDN_V_DIM), lambda t: (0, t, 0)),
                       pl.BlockSpec((B, rows, D), lambda t: (0, t, 0))],
            out_shape=[jax.ShapeDtypeStruct((B, LP, GDN_V_DIM), BF16),
                       jax.ShapeDtypeStruct((B, LP, D), F32)],
            scratch_shapes=scratch,
            compiler_params=_compiler_params(("arbitrary",)),
            name="gdn_core",
        )(src, head, gain.reshape(1, D), w_bf16, conv_w, gate, o_gain.reshape(1, GDN_DV))
    h3d = src
    B, LP, D = h3d.shape
    rows = _pick_tile(LP, GDN_ROW_TILES)
    return pl.pallas_call(
        _gdn_kernel,
        grid=(LP // rows,),
        in_specs=[pl.BlockSpec((B, rows, D), lambda t: (0, t, 0)),
                  pl.BlockSpec((1, D), fix),
                  pl.BlockSpec((D, W), fix, pipeline_mode=pl.Buffered(1)),
                  pl.BlockSpec((GDN_CONV, GDN_CONV_DIM), fix),
                  pl.BlockSpec((2, LANES), fix),
                  pl.BlockSpec((1, GDN_DV), fix)],
        out_specs=pl.BlockSpec((B, rows, GDN_V_DIM), lambda t: (0, t, 0)),
        out_shape=jax.ShapeDtypeStruct((B, LP, GDN_V_DIM), BF16),
        scratch_shapes=[pltpu.VMEM((B * GDN_HEADS, GDN_DK, GDN_DV), F32),
                        pltpu.VMEM((B, SUBLANES + CHUNK, GDN_CONV_DIM), F32),
                        pltpu.VMEM((B, CHUNK, GDN_CONV_DIM), F32),
                        pltpu.VMEM((B, rows, W), F32)],
        compiler_params=_compiler_params(("arbitrary",)),
        name="gdn_core",
    )(h3d, gain.reshape(1, D), w_bf16, conv_w, gate, o_gain.reshape(1, GDN_DV))


def _moe_router_kernel(h_ref, gain_ref, wr_ref, br_ref, hx_ref, gid_ref, rank_ref, cnt_ref, carry_ref):
    i = pl.program_id(0)
    TM = h_ref.shape[0]
    lane = lax.broadcasted_iota(jnp.int32, (1, LANES), 1)

    @pl.when(i == 0)
    def _():
        carry_ref[...] = jnp.zeros_like(carry_ref)

    x = h_ref[...]
    xn = x * lax.rsqrt(jnp.mean(x * x, axis=-1, keepdims=True) + RMS_EPS) * gain_ref[...]
    logits = jnp.dot(xn, wr_ref[...], preferred_element_type=F32,
                     precision=lax.Precision.HIGHEST) + br_ref[...]
    e_log = logits
    g_lane = lane - N_EXPERTS
    g_log = jnp.where((g_lane >= 0) & (g_lane < N_GROUPS), logits, NEG_INF)
    g_max = jnp.max(g_log, axis=-1, keepdims=True)
    g_sel = jnp.min(jnp.where(g_log == g_max, g_lane, LANES), axis=-1, keepdims=True)
    g_sel = jnp.minimum(g_sel, N_GROUPS - 1)
    g_w = 1.0 / jnp.sum(jnp.exp(g_log - g_max), axis=-1, keepdims=True)
    in_grp = (lane >= g_sel * EXPERTS_PER_GROUP) & (lane < (g_sel + 1) * EXPERTS_PER_GROUP)
    l0 = jnp.where(in_grp, e_log, NEG_INF)
    m1 = jnp.max(l0, axis=-1, keepdims=True)
    i1 = jnp.min(jnp.where(l0 == m1, lane, LANES), axis=-1, keepdims=True)
    l1 = jnp.where(lane == i1, NEG_INF, l0)
    m2 = jnp.max(l1, axis=-1, keepdims=True)
    i2 = jnp.min(jnp.where(l1 == m2, lane, LANES), axis=-1, keepdims=True)
    p2 = jnp.exp(m2 - m1)
    w1 = g_w / (1.0 + p2)
    w2 = g_w * p2 / (1.0 + p2)
    j1 = i1 - g_sel * EXPERTS_PER_GROUP
    j2 = i2 - g_sel * EXPERTS_PER_GROUP
    hx_ref[:, 0:xn.shape[1]] = xn
    hx_ref[:, xn.shape[1]:] = jnp.where(lane == j1, w1, 0.0) + jnp.where(lane == j2, w2, 0.0)

    onehot = jnp.where(lane == g_sel, 1.0, 0.0)
    r_i = lax.broadcasted_iota(jnp.int32, (TM, TM), 0)
    c_i = lax.broadcasted_iota(jnp.int32, (TM, TM), 1)
    before = jnp.where(c_i < r_i, 1.0, 0.0).astype(BF16)
    prior = jnp.dot(before, onehot.astype(BF16), preferred_element_type=F32) + carry_ref[...]
    rank = jnp.sum(jnp.where(lane == g_sel, prior, 0.0), axis=-1, keepdims=True)
    carry_ref[...] += jnp.sum(onehot, axis=0, keepdims=True)
    cnt_ref[...] = carry_ref[...]
    gid_ref[0] = jnp.broadcast_to(g_sel.astype(F32), (TM, LANES)).T[0:1, :].astype(jnp.int32)
    rank_ref[0] = jnp.broadcast_to(rank, (TM, LANES)).T[0:1, :].astype(jnp.int32)


def _moe_dispatch_kernel(seg_ref, pos_ref, hx_ref, xs_ref, zero_ref, sem, zero_sem):
    TM = hx_ref.shape[0]

    @pl.when(pl.program_id(0) == 0)
    def _():
        zero_ref[...] = jnp.zeros_like(zero_ref)

        def zero_tile(t):
            fill = pltpu.make_async_copy(zero_ref, xs_ref.at[pl.ds(t * TM, TM), :], zero_sem)
            fill.start()
            fill.wait()

        for g in range(N_GROUPS):
            first = seg_ref[g - 1] if g else 0
            pl.when(seg_ref[g] > first)(functools.partial(zero_tile, seg_ref[g] - 1))
            tail = seg_ref[N_GROUPS - 1] + g
            pl.when(tail < xs_ref.shape[0] // TM)(functools.partial(zero_tile, tail))

    def start(a, carry):
        for b in range(SUBLANES):
            r = a * SUBLANES + b
            pltpu.make_async_copy(hx_ref.at[pl.ds(r, 1), :], xs_ref.at[pl.ds(pos_ref[0, 0, r], 1), :],
                                  sem).start()
        return carry

    lax.fori_loop(0, TM // SUBLANES, start, 0)
    pltpu.make_async_copy(hx_ref, xs_ref.at[pl.ds(0, TM), :], sem).wait()


def _moe_expert_kernel(grp_ref, xs_ref, wgu_ref, wd_ref, ys_ref, xn_ref, acc_ref):
    j = pl.program_id(0)
    e = pl.program_id(1)
    D = xn_ref.shape[1]
    lane = lax.broadcasted_iota(jnp.int32, (1, LANES), 1)

    @pl.when(grp_ref[j] >= 0)
    def _():
        @pl.when(e == 0)
        def _():
            xn_ref[...] = xs_ref[:, 0:D].astype(BF16)
            acc_ref[...] = jnp.zeros_like(acc_ref)

        hh = jnp.dot(xn_ref[...], wgu_ref[0], preferred_element_type=F32)
        cw = jnp.sum(jnp.where(lane == e, xs_ref[:, D:], 0.0), axis=-1, keepdims=True)
        act = _silu(hh[:, 0:EXPERT_FF]) * hh[:, EXPERT_FF:2 * EXPERT_FF] * cw
        acc_ref[...] += jnp.dot(act.astype(BF16), wd_ref[0], preferred_element_type=F32)

    last = e == EXPERTS_PER_GROUP - 1

    @pl.when(last & (grp_ref[j] >= 0))
    def _():
        ys_ref[...] = acc_ref[...]

    @pl.when(last & (grp_ref[j] < 0))
    def _():
        ys_ref[...] = jnp.zeros_like(ys_ref)


def _moe_combine_kernel(pos_ref, h_ref, ys_ref, o_ref, stage_ref, sem):
    TM = h_ref.shape[0]

    def start(a, carry):
        for b in range(SUBLANES):
            r = a * SUBLANES + b
            pltpu.make_async_copy(ys_ref.at[pl.ds(pos_ref[0, 0, r], 1), :], stage_ref.at[pl.ds(r, 1), :],
                                  sem).start()
        return carry

    lax.fori_loop(0, TM // SUBLANES, start, 0)
    pltpu.make_async_copy(ys_ref.at[pl.ds(0, TM), :], stage_ref, sem).wait()
    o_ref[...] = h_ref[...] + stage_ref[...]


def _moe(h2d, gain, w_group, b_group, w_expert, b_expert, wgu, wd, layer, frames=None):
    T, D = h2d.shape
    tr = _pick_tile(T, MOE_ROUTER_TILES)
    tm = _pick_tile(T, MOE_ROW_TILES)
    n_tiles = T // tm + N_GROUPS
    XW = D + LANES
    assert N_EXPERTS + N_GROUPS <= LANES
    wr = jnp.zeros((D, LANES), F32)
    wr = wr.at[:, 0:N_EXPERTS].set(w_expert.astype(F32))
    wr = wr.at[:, N_EXPERTS:N_EXPERTS + N_GROUPS].set(w_group.astype(F32))
    br = jnp.zeros((1, LANES), F32)
    br = br.at[0, 0:N_EXPERTS].set(b_expert.astype(F32))
    br = br.at[0, N_EXPERTS:N_EXPERTS + N_GROUPS].set(b_group.astype(F32))

    hx, gid, rank, counts = pl.pallas_call(
        _moe_router_kernel,
        grid=(T // tr,),
        in_specs=[pl.BlockSpec((tr, D), lambda i: (i, 0)),
                  pl.BlockSpec((1, D), lambda i: (0, 0)),
                  pl.BlockSpec((D, LANES), lambda i: (0, 0)),
                  pl.BlockSpec((1, LANES), lambda i: (0, 0))],
        out_specs=[pl.BlockSpec((tr, XW), lambda i: (i, 0)),
                   pl.BlockSpec((1, 1, tr), lambda i: (i, 0, 0)),
                   pl.BlockSpec((1, 1, tr), lambda i: (i, 0, 0)),
                   pl.BlockSpec((1, LANES), lambda i: (0, 0))],
        out_shape=[jax.ShapeDtypeStruct((T, XW), F32),
                   jax.ShapeDtypeStruct((T // tr, 1, tr), jnp.int32),
                   jax.ShapeDtypeStruct((T // tr, 1, tr), jnp.int32),
                   jax.ShapeDtypeStruct((1, LANES), F32)],
        scratch_shapes=[pltpu.VMEM((1, LANES), F32)],
        compiler_params=_compiler_params(("arbitrary",)),
        name="moe_router",
    )(h2d, gain.reshape(1, D), wr, br)

    seg_tiles = -(-counts[0, 0:N_GROUPS].astype(jnp.int32) // tm)
    seg_end = jnp.cumsum(seg_tiles)
    seg_start = seg_end - seg_tiles
    pos = (seg_start * tm)[gid.reshape(T)] + rank.reshape(T)
    tile_ids = jnp.arange(n_tiles, dtype=jnp.int32)
    tile_group = jnp.sum((tile_ids[:, None] >= seg_end[None, :]).astype(jnp.int32), axis=1)
    tile_group = jnp.where(tile_ids < seg_end[N_GROUPS - 1], tile_group, -1)
    pos3 = pos.reshape(T // tm, 1, tm)

    xs = pl.pallas_call(
        _moe_dispatch_kernel,
        grid_spec=pltpu.PrefetchScalarGridSpec(
            num_scalar_prefetch=1,
            grid=(T // tm,),
            in_specs=[pl.BlockSpec((1, 1, tm), lambda i, seg: (i, 0, 0), memory_space=pltpu.SMEM),
                      pl.BlockSpec((tm, XW), lambda i, seg: (i, 0))],
            out_specs=pl.BlockSpec(memory_space=pl.ANY),
            scratch_shapes=[pltpu.VMEM((tm, XW), F32), pltpu.SemaphoreType.DMA(()),
                            pltpu.SemaphoreType.DMA(())]),
        out_shape=jax.ShapeDtypeStruct((n_tiles * tm, XW), F32),
        compiler_params=_compiler_params(("arbitrary",)),
        name="moe_dispatch",
    )(seg_end, pos3, hx)

    def expert_index(j, e, grp):
        return (layer * N_EXPERTS + jnp.maximum(grp[j], 0) * EXPERTS_PER_GROUP + e, 0, 0)

    ys = pl.pallas_call(
        _moe_expert_kernel,
        grid_spec=pltpu.PrefetchScalarGridSpec(
            num_scalar_prefetch=1,
            grid=(n_tiles, EXPERTS_PER_GROUP),
            in_specs=[pl.BlockSpec((tm, XW), lambda j, e, grp: (j, 0)),
                      pl.BlockSpec((1, D, 2 * EXPERT_FF), expert_index),
                      pl.BlockSpec((1, EXPERT_FF, D), expert_index)],
            out_specs=pl.BlockSpec((tm, D), lambda j, e, grp: (j, 0)),
            scratch_shapes=[pltpu.VMEM((tm, D), BF16), pltpu.VMEM((tm, D), F32)]),
        out_shape=jax.ShapeDtypeStruct((n_tiles * tm, D), F32),
        compiler_params=_compiler_params(("parallel", "arbitrary")),
        name="moe_experts",
    )(tile_group, xs, wgu, wd)

    if frames is None:
        tc, out_rows, out_index = tm, T, lambda i: (i, 0)
    else:
        batch_rows, n_frames, first = frames
        tc = ROW_ALIGN
        assert batch_rows % tc == 0 and n_frames % tc == 0 and first % tc == 0 and first >= tc
        out_rows = (T // batch_rows) * n_frames

        def out_index(i):
            b, t = i // (batch_rows // tc), i % (batch_rows // tc)
            return (b * (n_frames // tc) + jnp.maximum(t - first // tc, 0), 0)

    return pl.pallas_call(
        _moe_combine_kernel,
        grid=(T // tc,),
        in_specs=[pl.BlockSpec((1, 1, tc), lambda i: (i, 0, 0), memory_space=pltpu.SMEM),
                  pl.BlockSpec((tc, D), lambda i: (i, 0)),
                  pl.BlockSpec(memory_space=pl.ANY)],
        out_specs=pl.BlockSpec((tc, D), out_index),
        out_shape=jax.ShapeDtypeStruct((out_rows, D), F32),
        scratch_shapes=[pltpu.VMEM((tc, D), F32), pltpu.SemaphoreType.DMA(())],
        compiler_params=_compiler_params(("arbitrary",)),
        name="moe_combine",
    )(pos.reshape(T // tc, 1, tc), h2d, ys)


def _dsa_prep_kernel(p_ref, qg_ref, kg_ref, lng_ref, lnb_ref, q_ref, kt_ref, v_ref, qit_ref, ki_ref, wit_ref):
    lane = lax.broadcasted_iota(jnp.int32, (1, LANES), 1)
    for h in range(ATT_HEADS):
        x = p_ref[:, DSA_Q_OFF + h * ATT_HD:DSA_Q_OFF + (h + 1) * ATT_HD]
        y = x * lax.rsqrt(jnp.mean(x * x, axis=-1, keepdims=True) + RMS_EPS) * qg_ref[...]
        q_ref[:, h * ATT_HD:(h + 1) * ATT_HD] = (y * (ATT_HD ** -0.5)).astype(BF16)
    for n in range(ATT_KV_HEADS):
        x = p_ref[:, DSA_K_OFF + n * ATT_HD:DSA_K_OFF + (n + 1) * ATT_HD]
        y = x * lax.rsqrt(jnp.mean(x * x, axis=-1, keepdims=True) + RMS_EPS) * kg_ref[...]
        kt_ref[0, n * ATT_HD:(n + 1) * ATT_HD, :] = y.T.astype(BF16)
    v_ref[...] = p_ref[:, DSA_V_OFF:DSA_V_OFF + ATT_KV_DIM].astype(BF16)
    qit_ref[0] = p_ref[:, DSA_QI_OFF:DSA_QI_OFF + IDX_HEADS * LANES].T.astype(BF16)
    x = p_ref[:, DSA_KI_OFF:DSA_KI_OFF + LANES]
    live = lane < IDX_HD
    mu = jnp.sum(x, axis=-1, keepdims=True) * (1.0 / IDX_HD)
    xc = jnp.where(live, x - mu, 0.0)
    var = jnp.sum(xc * xc, axis=-1, keepdims=True) * (1.0 / IDX_HD)
    ki = xc * lax.rsqrt(var + RMS_EPS) * lng_ref[...] + lnb_ref[...]
    ki_ref[...] = jnp.where(live, ki, 0.0).astype(BF16)
    wit_ref[0] = (p_ref[:, DSA_WI_OFF:DSA_WI_OFF + LANES] * (IDX_HEADS ** -0.5 * IDX_HD ** -0.5)).T


def _dsa_in_kernel(x_ref, g_ref, w_ref, qg_ref, kg_ref, lng_ref, lnb_ref,
                   q_ref, kt_ref, v_ref, qit_ref, ki_ref, wit_ref, proj_ref):
    _norm_matmul_kernel(x_ref, g_ref, w_ref, proj_ref)
    _dsa_prep_kernel(proj_ref, qg_ref, kg_ref, lng_ref, lnb_ref, q_ref, kt_ref, v_ref, qit_ref, ki_ref, wit_ref)


def _dsa_in(h2d, gain, w_bf16, q_gain, k_gain, ln_g, ln_b):
    T, D = h2d.shape
    W = w_bf16.shape[1]
    tm = KEY_TILE
    assert T % tm == 0 and ATT_TILE == KEY_TILE
    lng = jnp.zeros((1, LANES), F32).at[0, 0:IDX_HD].set(ln_g.astype(F32))
    lnb = jnp.zeros((1, LANES), F32).at[0, 0:IDX_HD].set(ln_b.astype(F32))
    row = lambda i: (i, 0)
    fix = lambda i: (0, 0)
    blk = lambda i: (i, 0, 0)
    return pl.pallas_call(
        _dsa_in_kernel,
        grid=(T // tm,),
        in_specs=[pl.BlockSpec((tm, D), row), pl.BlockSpec((1, D), fix), pl.BlockSpec((D, W), fix),
                  pl.BlockSpec((1, ATT_HD), fix), pl.BlockSpec((1, ATT_HD), fix),
                  pl.BlockSpec((1, LANES), fix), pl.BlockSpec((1, LANES), fix)],
        out_specs=[pl.BlockSpec((tm, ATT_Q_DIM), row), pl.BlockSpec((1, ATT_KV_DIM, tm), blk),
                   pl.BlockSpec((tm, ATT_KV_DIM), row), pl.BlockSpec((1, IDX_HEADS * LANES, tm), blk),
                   pl.BlockSpec((tm, LANES), row), pl.BlockSpec((1, LANES, tm), blk)],
        out_shape=[jax.ShapeDtypeStruct((T, ATT_Q_DIM), BF16), jax.ShapeDtypeStruct((T // tm, ATT_KV_DIM, tm), BF16),
                   jax.ShapeDtypeStruct((T, ATT_KV_DIM), BF16),
                   jax.ShapeDtypeStruct((T // tm, IDX_HEADS * LANES, tm), BF16),
                   jax.ShapeDtypeStruct((T, LANES), BF16), jax.ShapeDtypeStruct((T // tm, LANES, tm), F32)],
        scratch_shapes=[pltpu.VMEM((tm, W), F32)],
        compiler_params=_compiler_params(("parallel",)),
        name="dsa_in",
    )(h2d, gain.reshape(1, D), w_bf16, q_gain.reshape(1, ATT_HD), k_gain.reshape(1, ATT_HD), lng, lnb)


def _dsa_kernel(q_ref, qit_ref, wit_ref, kt_ref, v_ref, ki_ref, bias_ref, bmax_ref, o_ref,
                sc_ref, mask_ref, qg_ref, m_ref, l_ref, acc_ref, knorm_ref, *, n_select, live_rows):
    i = pl.program_id(1)
    TQ = ATT_TILE
    TK = KEY_TILE
    HALVES = TK // LANES

    @pl.when(i == 0)
    def _():
        def body(kt, mx):
            kk = kt_ref[0, kt].astype(F32)
            return tuple(jnp.maximum(mx[n], jnp.sum(jnp.square(kk[n * ATT_HD:(n + 1) * ATT_HD, :]),
                                                    axis=0, keepdims=True)) for n in range(ATT_KV_HEADS))

        mx = lax.fori_loop(0, kt_ref.shape[1], body,
                           tuple(jnp.zeros((1, TK), F32) for _ in range(ATT_KV_HEADS)))
        for n in range(ATT_KV_HEADS):
            knorm_ref[n] = jnp.broadcast_to(jnp.sqrt(jnp.max(mx[n], axis=-1, keepdims=True)), (1, LANES))

    @pl.when(i * TQ >= live_rows)
    def _():
        o_ref[...] = jnp.zeros_like(o_ref)

    @pl.when(i * TQ < live_rows)
    def _():
        _dsa_tile(i, q_ref, qit_ref, wit_ref, kt_ref, v_ref, ki_ref, bias_ref, bmax_ref, o_ref,
                  sc_ref, mask_ref, qg_ref, m_ref, l_ref, acc_ref, knorm_ref, n_select, TQ, TK, HALVES)


def _dsa_tile(i, q_ref, qit_ref, wit_ref, kt_ref, v_ref, ki_ref, bias_ref, bmax_ref, o_ref,
              sc_ref, mask_ref, qg_ref, m_ref, l_ref, acc_ref, knorm_ref, n_select, TQ, TK, HALVES):
    assert TQ == TK
    nkt = i + 1
    fmin = float(jnp.finfo(F32).min)
    kf = float(n_select)

    chunk_shift = int(math.log2(CHUNK))
    q_pos = i * TQ + lax.broadcasted_iota(jnp.int32, (1, TQ), 1)
    q_chunk = jnp.maximum(lax.shift_right_logical(q_pos, chunk_shift), FRONT_PAD // CHUNK)
    k_row = lax.broadcasted_iota(jnp.int32, (TK, 1), 0)

    def halves(x):
        return [x[:, j * LANES:(j + 1) * LANES] for j in range(HALVES)]

    def fold(x, op):
        parts = [x[r * SUBLANES:(r + 1) * SUBLANES, :] for r in range(TK // SUBLANES)]
        while len(parts) > 1:
            parts = [op(parts[j], parts[j + 1]) for j in range(0, len(parts), 2)]
        return parts[0]

    def score_tiles(kts):
        for kt in kts:
            keys = ki_ref[0, pl.ds(pl.multiple_of(kt * TK, TK), TK), :]
            acc = None
            for h in range(IDX_HEADS):
                d = jnp.dot(keys, qit_ref[0, 0, h * LANES:(h + 1) * LANES, :], preferred_element_type=F32)
                term = wit_ref[0, 0, h:h + 1, :] * jnp.maximum(d, 0.0)
                acc = term if acc is None else acc + term
            sc_ref[kt] = acc
            mask_ref[kt] = acc.T

    def grouped_loop(n, fn, group):
        def body(j, carry):
            fn([group * j + r for r in range(group)])
            return carry

        lax.fori_loop(0, lax.shift_right_logical(n, int(math.log2(group))), body, 0)
        size = group // 2
        while size >= 1:
            first = lax.bitwise_and(n, ~(2 * size - 1))

            @pl.when(lax.bitwise_and(n, size) != 0)
            def _(size=size, first=first):
                fn([first + r for r in range(size)])

            size //= 2

    grouped_loop(nkt, score_tiles, INDEXER_GROUP)

    def mask_inadmissible(kt):
        k_pos = kt * TK + k_row
        adm = (lax.shift_right_logical(k_pos, chunk_shift) <= q_chunk) & (k_pos >= FRONT_PAD)
        sc_ref[kt] = jnp.where(adm, sc_ref[kt], NEG_INF)

    mask_inadmissible(0)

    @pl.when(nkt > 1)
    def _():
        mask_inadmissible(nkt - 1)

    n_adm = CHUNK * (q_chunk + 1) - FRONT_PAD
    need = n_adm > n_select

    def count_where(pred):
        def one(kt):
            return fold(jnp.where(pred(sc_ref[kt], kt), 1.0, 0.0), jnp.add)

        def two(j, acc):
            return acc + (one(2 * j) + one(2 * j + 1))

        acc = lax.fori_loop(0, lax.shift_right_logical(nkt, 1), two, jnp.zeros((SUBLANES, TQ), F32))
        acc = lax.cond(lax.bitwise_and(nkt, 1) == 1, lambda a: a + one(nkt - 1), lambda a: a, acc)
        return jnp.sum(acc, axis=0, keepdims=True)

    def count_ge(t):
        return count_where(lambda s, kt: s >= t)

    def minmax_body(kt, carry):
        mn, mx = carry
        s = sc_ref[kt]
        mn = jnp.minimum(mn, fold(jnp.where(s == NEG_INF, jnp.inf, s), jnp.minimum))
        mx = jnp.maximum(mx, fold(s, jnp.maximum))
        return mn, mx

    mn_l, mx_l = lax.fori_loop(0, nkt, minmax_body,
                               (jnp.full((SUBLANES, TQ), jnp.inf, F32), jnp.full((SUBLANES, TQ), NEG_INF, F32)))
    s_min = jnp.min(mn_l, axis=0, keepdims=True)
    s_max = jnp.max(mx_l, axis=0, keepdims=True)
    lo0 = s_min
    hi0 = s_max + (s_max - s_min) + 1.0
    cnt0 = n_adm.astype(F32)
    active0 = jnp.where(need, 1.0, 0.0)

    def any_active(active):
        return jnp.max(active) > 0.0

    def bis_step(lo, hi, cnt, active):
        mid = lo + (hi - lo) * 0.5
        c = count_ge(mid)
        ge = c >= kf
        moving = (mid > lo) & (mid < hi) & (active > 0.0)
        lo_n = jnp.where(moving & ge, mid, lo)
        hi_n = jnp.where(moving & (~ge), mid, hi)
        cnt_n = jnp.where(moving & ge, c, cnt)
        active_n = jnp.where(moving & (cnt_n != kf), 1.0, 0.0)
        return lo_n, hi_n, cnt_n, active_n

    def bis_body(st):
        it, rest = st[0], st[1:]
        for _ in range(BISECT_UNROLL):
            rest = bis_step(*rest)
        return (it + BISECT_UNROLL,) + rest

    _, lo, hi, cnt, _ = lax.while_loop(
        lambda st: jnp.logical_and(st[0] < BISECT_STEPS, any_active(st[4])), bis_body,
        (jnp.int32(0), lo0, hi0, cnt0, active0))

    def snap_body(st):
        lo, hi, cnt, active = st

        def body(kt, mx):
            s = sc_ref[kt]
            return jnp.maximum(mx, fold(jnp.where(s < hi, s, NEG_INF), jnp.maximum))

        v1 = jnp.max(lax.fori_loop(0, nkt, body, jnp.full((SUBLANES, TQ), NEG_INF, F32)),
                     axis=0, keepdims=True)
        c1 = count_ge(v1)
        on = active > 0.0
        hit = on & (c1 >= kf)
        lo_n = jnp.where(hit, v1, lo)
        cnt_n = jnp.where(hit, c1, cnt)
        hi_n = jnp.where(on & (~hit), v1, hi)
        return lo_n, hi_n, cnt_n, jnp.where(on & (~hit), 1.0, 0.0)

    unsettled = jnp.where(need & (cnt != kf), 1.0, 0.0)
    lo, hi, cnt, _ = lax.while_loop(lambda st: any_active(st[3]), snap_body, (lo, hi, cnt, unsettled))

    thr = jnp.where(need, lo, fmin)
    tied = need & (cnt > kf)
    big_pos = jnp.int32(2 ** 30)

    def tie_break():
        want = kf - count_where(lambda s, kt: s > thr)

        def body(_, st):
            jl, jh = st
            jm = lax.shift_right_arithmetic(jl + jh, 1)
            c = count_where(lambda s, kt: (s == thr) & (kt * TK + k_row <= jm))
            ok = c >= want
            return jnp.where(ok, jl, jm), jnp.where(ok, jm, jh)

        n_keys = sc_ref.shape[0] * TK
        steps = int(math.ceil(math.log2(n_keys + 1))) + 1
        jl0 = jnp.full((1, TQ), -1, jnp.int32)
        jh0 = jnp.full((1, TQ), n_keys, jnp.int32)
        _, jh = lax.fori_loop(0, steps, body, (jl0, jh0))
        return jnp.where(tied, jh, big_pos)

    pos_cap = lax.cond(jnp.max(jnp.where(tied, 1.0, 0.0)) > 0.0, tie_break,
                       lambda: jnp.full((1, TQ), big_pos, jnp.int32))

    shift_w = None
    for n in range(ATT_KV_HEADS):
        for g in range(ATT_GROUP):
            h = n * ATT_GROUP + g
            qg_ref[n, g * TQ:(g + 1) * TQ, :] = q_ref[0, :, h * ATT_HD:(h + 1) * ATT_HD]
            qh = q_ref[0, :, h * ATT_HD:(h + 1) * ATT_HD].astype(F32)
            q_len = jnp.sqrt(jnp.sum(qh * qh, axis=-1, keepdims=True))
            bound = q_len * knorm_ref[n] * SHIFT_SLACK + (bmax_ref[h] + SHIFT_SLACK - 1.0)
            shift_w = bound if shift_w is None else jnp.maximum(shift_w, bound)

    def lane_wide(row):
        return jnp.broadcast_to(row, (LANES, TQ)).T

    thr_w = lane_wide(thr)
    cap_w = lane_wide(pos_cap)
    chunk_w = lane_wide(q_chunk)
    k_lane = lax.broadcasted_iota(jnp.int32, (1, LANES), 1)

    def to_mask(kt, edge_tile):
        out = []
        for j, s in enumerate(halves(mask_ref[kt])):
            k_pos = kt * TK + j * LANES + k_lane
            sel = (s > thr_w) | ((s == thr_w) & (k_pos <= cap_w))
            if edge_tile:
                sel = sel & (lax.shift_right_logical(k_pos, chunk_shift) <= chunk_w) & (k_pos >= FRONT_PAD)
            out.append(jnp.where(sel, -shift_w, NEG_INF))
        mask_ref[kt] = jnp.concatenate(out, axis=1)

    def inner_mask(kt, carry):
        to_mask(kt, False)
        return carry

    lax.fori_loop(1, nkt - 1, inner_mask, 0)
    to_mask(0, True)

    @pl.when(nkt > 1)
    def _():
        to_mask(nkt - 1, True)

    l_ref[...] = jnp.zeros_like(l_ref)
    acc_ref[...] = jnp.zeros_like(acc_ref)

    kv_heads = range(ATT_KV_HEADS)

    def raw_logits(kts):
        return [[jnp.dot(qg_ref[n], kt_ref[0, kt, n * ATT_HD:(n + 1) * ATT_HD, :], preferred_element_type=F32)
                 for n in kv_heads] for kt in kts]

    def masked_logits(s, mask, n, g, b0):
        sg = s[g * TQ:(g + 1) * TQ, :] + mask
        if b0 is not None:
            h = n * ATT_GROUP + g
            sg = sg + jnp.concatenate([bias_ref[h, b0 + j] for j in range(HALVES)], axis=1)
        return sg

    def plain_mask(kt):
        return jnp.where(mask_ref[kt] == NEG_INF, NEG_INF, 0.0)

    def max_pass(kts, b0):
        raw = raw_logits(kts)
        masks = [plain_mask(kt) for kt in kts]
        for n in kv_heads:
            for g in range(ATT_GROUP):
                rows = slice(g * TQ, (g + 1) * TQ)
                m = m_ref[n, rows, :]
                for t in range(len(kts)):
                    for part in halves(masked_logits(raw[t][n], masks[t], n, g, b0)):
                        m = jnp.maximum(m, part)
                m_ref[n, rows, :] = m

    def exp_pass(kts, b0, exact_max):
        raw = raw_logits(kts)
        masks = [plain_mask(kt) if exact_max else mask_ref[kt] for kt in kts]
        for n in kv_heads:
            ps = [[] for _ in kts]
            for g in range(ATT_GROUP):
                rows = slice(g * TQ, (g + 1) * TQ)
                m = m_ref[n, rows, :] if exact_max else 0.0
                l_new = l_ref[n, rows, :]
                for t in range(len(kts)):
                    parts = [jnp.exp(part - m) if exact_max else jnp.exp(part)
                             for part in halves(masked_logits(raw[t][n], masks[t], n, g, b0))]
                    l_new = l_new + functools.reduce(lambda a, b: a + b, parts)
                    ps[t].append(jnp.concatenate(parts, axis=1).astype(BF16))
                l_ref[n, rows, :] = l_new
            pv = None
            for t, kt in enumerate(kts):
                vv = v_ref[0, pl.ds(pl.multiple_of(kt * TK, TK), TK), n * ATT_HD:(n + 1) * ATT_HD]
                d = jnp.dot(jnp.concatenate(ps[t], axis=0), vv, preferred_element_type=F32)
                pv = d if pv is None else pv + d
            acc_ref[n] += pv

    def run_pass(tile_fn):
        grouped_loop(jnp.maximum(i - 1, 0), lambda kts: tile_fn(kts, None), ATTEND_GROUP)

        @pl.when(i >= 1)
        def _():
            tile_fn([i - 1], 0)

        tile_fn([i], HALVES)

    run_pass(functools.partial(exp_pass, exact_max=False))

    def smallest_row_sum():
        return functools.reduce(jnp.minimum, [jnp.min(jnp.sum(l_ref[n], axis=-1, keepdims=True))
                                              for n in range(ATT_KV_HEADS)])

    @pl.when(jnp.logical_not(smallest_row_sum() > ROW_SUM_FLOOR))
    def _():
        m_ref[...] = jnp.full(m_ref.shape, -1e30, F32)
        run_pass(max_pass)
        for n in range(ATT_KV_HEADS):
            m_ref[n] = jnp.broadcast_to(jnp.max(m_ref[n], axis=-1, keepdims=True), m_ref.shape[1:])
        l_ref[...] = jnp.zeros_like(l_ref)
        acc_ref[...] = jnp.zeros_like(acc_ref)
        run_pass(functools.partial(exp_pass, exact_max=True))

    for n in range(ATT_KV_HEADS):
        inv_l = 1.0 / jnp.sum(l_ref[n], axis=-1, keepdims=True)
        for g in range(ATT_GROUP):
            h = n * ATT_GROUP + g
            rows = slice(g * TQ, (g + 1) * TQ)
            o_ref[0, :, h * ATT_HD:(h + 1) * ATT_HD] = (acc_ref[n, rows, :] * inv_l[rows, :]).astype(o_ref.dtype)


def _t5_bucket(rel):
    nb = REL_BUCKETS // 2
    max_exact = nb // 2
    ret = jnp.where(rel > 0, nb, 0)
    n = jnp.abs(rel)
    nf = jnp.maximum(n, 1).astype(jnp.float32)
    large = max_exact + (jnp.log(nf / max_exact) / math.log(REL_MAX_DIST / max_exact)
                         * (nb - max_exact)).astype(jnp.int32)
    large = jnp.minimum(large, nb - 1)
    return ret + jnp.where(n < max_exact, n, large)


def _dsa_core(q, kt, v, qit, ki, wit, rel_bias, n_select, live_rows):
    B, LP, _ = q.shape
    nq = LP // ATT_TILE
    nkt = LP // KEY_TILE
    a = jnp.arange(ATT_TILE)[:, None]
    xx = jnp.arange(2 * KEY_TILE)[None, :]
    rel = (xx - KEY_TILE) - a
    far = rel_bias.astype(F32)[REL_BUCKETS // 2 - 1]
    onehot = jax.nn.one_hot(_t5_bucket(rel), REL_BUCKETS, dtype=F32)
    table = jnp.einsum("abk,kh->abh", onehot, rel_bias.astype(F32),
                       precision=lax.Precision.HIGHEST) - far
    bias_near = jnp.transpose(table.reshape(ATT_TILE, 2 * KEY_TILE // LANES, LANES, ATT_HEADS), (3, 1, 0, 2))
    bias_max = jnp.broadcast_to(jnp.maximum(jnp.max(table, axis=(0, 1)), 0.0)[:, None, None],
                                (ATT_HEADS, 1, LANES))
    kern = functools.partial(_dsa_kernel, n_select=n_select, live_rows=live_rows)
    tile = lambda b, i: (b, i, 0)
    tile4 = lambda b, i: (b, i, 0, 0)
    whole3 = lambda b, i: (b, 0, 0)
    whole4 = lambda b, i: (b, 0, 0, 0)
    once = pl.Buffered(1)
    return pl.pallas_call(
        kern,
        grid=(B, nq),
        in_specs=[pl.BlockSpec((1, ATT_TILE, ATT_Q_DIM), tile),
                  pl.BlockSpec((1, 1, IDX_HEADS * LANES, ATT_TILE), tile4),
                  pl.BlockSpec((1, 1, LANES, ATT_TILE), tile4),
                  pl.BlockSpec((1, nkt, ATT_KV_DIM, KEY_TILE), whole4, pipeline_mode=once),
                  pl.BlockSpec((1, LP, ATT_KV_DIM), whole3, pipeline_mode=once),
                  pl.BlockSpec((1, LP, LANES), whole3, pipeline_mode=once),
                  pl.BlockSpec((ATT_HEADS, 2 * KEY_TILE // LANES, ATT_TILE, LANES), lambda b, i: (0, 0, 0, 0),
                               pipeline_mode=once),
                  pl.BlockSpec((ATT_HEADS, 1, LANES), lambda b, i: (0, 0, 0))],
        out_specs=pl.BlockSpec((1, ATT_TILE, ATT_Q_DIM), tile),
        out_shape=jax.ShapeDtypeStruct((B, LP, ATT_Q_DIM), BF16),
        scratch_shapes=[pltpu.VMEM((nkt, KEY_TILE, ATT_TILE), F32),
                        pltpu.VMEM((nkt, ATT_TILE, KEY_TILE), F32),
                        pltpu.VMEM((ATT_KV_HEADS, ATT_GROUP * ATT_TILE, ATT_HD), BF16),
                        pltpu.VMEM((ATT_KV_HEADS, ATT_GROUP * ATT_TILE, LANES), F32),
                        pltpu.VMEM((ATT_KV_HEADS, ATT_GROUP * ATT_TILE, LANES), F32),
                        pltpu.VMEM((ATT_KV_HEADS, ATT_GROUP * ATT_TILE, ATT_HD), F32),
                        pltpu.VMEM((ATT_KV_HEADS, 1, LANES), F32)],
        compiler_params=_compiler_params(("arbitrary", "arbitrary")),
        name="dsa_core",
    )(q, qit, wit, kt, v, ki, bias_near, bias_max)


def _gdn_in_weight(w_in):
    D = w_in.shape[0]
    w = jnp.zeros((D, GDN_PROJ_W), F32).at[:, 0:w_in.shape[1]].set(w_in.astype(F32))
    return w.astype(BF16)


def _dsa_in_weight(w_in):
    D = w_in.shape[0]
    w_in = w_in.astype(BF16)
    zeros = lambda n: jnp.zeros((D, n), BF16)
    src = ATT_Q_DIM + 2 * ATT_KV_DIM
    parts = [w_in[:, 0:src]]
    for h in range(IDX_HEADS):
        parts += [w_in[:, src + h * IDX_HD:src + (h + 1) * IDX_HD], zeros(LANES - IDX_HD)]
    src += IDX_HEADS * IDX_HD
    parts += [w_in[:, src:src + IDX_HD], zeros(LANES - IDX_HD),
              w_in[:, src + IDX_HD:src + IDX_HD + IDX_HEADS], zeros(LANES - IDX_HEADS)]
    w = jnp.concatenate(parts, axis=1)
    assert w.shape[1] == DSA_PROJ_W
    return w


def kernel(x, meta_tokens, norm_mix, norm_ffn, rel_bias, gdn_w_in, gdn_conv, gdn_a_log, gdn_dt_bias,
           gdn_o_norm, gdn_w_out, dsa_w_in, dsa_q_norm, dsa_k_norm, dsa_idx_ln_g, dsa_idx_ln_b, dsa_w_out,
           moe_w_group, moe_b_group, moe_w_expert, moe_b_expert, moe_w_gate_up, moe_w_down):
    B, S, D = x.shape
    depth = norm_mix.shape[0]
    n_select = min(TOPK_MAX, S // 4)
    off = FRONT_PAD + N_META
    LP = -(-(off + S) // ROW_ALIGN) * ROW_ALIGN
    assert LP == off + S, "frames must fill whole attention tiles"
    meta = jnp.broadcast_to(meta_tokens.astype(x.dtype)[None], (B, N_META, D))
    head = jnp.concatenate([jnp.zeros((B, FRONT_PAD, D), x.dtype), meta], axis=1)
    wgu_all = moe_w_gate_up.astype(BF16).reshape(depth * N_EXPERTS, D, 2 * EXPERT_FF)
    wd_all = moe_w_down.astype(BF16).reshape(depth * N_EXPERTS, EXPERT_FF, D)
    h = None
    for i in range(depth):
        j = i // 2
        if i % 2 == 0:
            gdn_args = (norm_mix[i], _gdn_in_weight(gdn_w_in[j]), gdn_conv[j].astype(F32), gdn_a_log[j],
                        gdn_dt_bias[j], gdn_o_norm[j])
            if i == 0:
                o, h = _gdn_core(x, *gdn_args, head=head)
                h = h.reshape(B * LP, D)
            else:
                o = _gdn_core(h.reshape(B, LP, D), *gdn_args)
            h = _matmul_residual(o.reshape(B * LP, GDN_V_DIM), gdn_w_out[j].astype(BF16), h)
        else:
            q, kt, v, qit, ki, wit = _dsa_in(h, norm_mix[i], _dsa_in_weight(dsa_w_in[j]), dsa_q_norm[j],
                                             dsa_k_norm[j], dsa_idx_ln_g[j], dsa_idx_ln_b[j])
            r3 = lambda t: t.reshape(B, LP, t.shape[-1])
            r4 = lambda t: t.reshape(B, LP // KEY_TILE, t.shape[-2], KEY_TILE)
            o = _dsa_core(r3(q), r4(kt), r3(v), r4(qit), r3(ki), r4(wit), rel_bias, n_select, off + S)
            h = _matmul_residual(o.reshape(B * LP, ATT_Q_DIM), dsa_w_out[j].astype(BF16), h)
        h = _moe(h, norm_ffn[i], moe_w_group[i], moe_b_group[i], moe_w_expert[i], moe_b_expert[i],
                 wgu_all, wd_all, i, frames=(LP, S, off) if i == depth - 1 else None)
    return h.reshape(B, S, D)
```

```python
import functools
import math

import jax
import jax.numpy as jnp
from jax import lax
from jax.experimental import pallas as pl
from jax.experimental.pallas import tpu as pltpu

F32 = jnp.float32
BF16 = jnp.bfloat16

LANES = 128
SUBLANES = 8
VMEM_LIMIT_BYTES = 56 * 1024 * 1024

CHUNK = 64
N_META = 16
META_PAD = (-N_META) % CHUNK
RMS_EPS = 1e-6
L2_EPS = 1e-6

GDN_HEADS = 8
GDN_DK = 128
GDN_DV = 128
GDN_CONV = 4
GDN_QK_DIM = GDN_HEADS * GDN_DK
GDN_V_DIM = GDN_HEADS * GDN_DV
GDN_CONV_DIM = 2 * GDN_QK_DIM + GDN_V_DIM
GDN_GATE_OFF = GDN_CONV_DIM + GDN_V_DIM
GDN_PROJ_W = GDN_GATE_OFF + LANES

ATT_HEADS = 8
ATT_KV_HEADS = 2
ATT_GROUP = ATT_HEADS // ATT_KV_HEADS
ATT_HD = 128
ATT_Q_DIM = ATT_HEADS * ATT_HD
ATT_KV_DIM = ATT_KV_HEADS * ATT_HD
V_WIDE = ATT_HD + LANES
IDX_HEADS = 8
IDX_HD = 64
TOPK_MAX = 256
REL_BUCKETS = 32
REL_MAX_DIST = 128

DSA_Q_OFF = 0
DSA_K_OFF = ATT_Q_DIM
DSA_V_OFF = DSA_K_OFF + ATT_KV_DIM
DSA_QI_OFF = DSA_V_OFF + ATT_KV_DIM
DSA_KI_OFF = DSA_QI_OFF + IDX_HEADS * LANES
DSA_WI_OFF = DSA_KI_OFF + LANES
DSA_PROJ_W = DSA_WI_OFF + LANES

N_GROUPS = 4
EXPERTS_PER_GROUP = 8
N_EXPERTS = N_GROUPS * EXPERTS_PER_GROUP
EXPERT_FF = 256

GDN_ROW_TILES = (256, 128, 64)
MOE_ROUTER_TILES = (512, 256, 128)
MOE_ROW_TILES = (512, 256, 128)

ATT_TILE = 256
KEY_TILE = 256
ROW_ALIGN = KEY_TILE
INDEXER_GROUP = 4
ATTEND_GROUP = 2
FRONT_PAD = ROW_ALIGN - N_META
assert FRONT_PAD % CHUNK == META_PAD
BISECT_UNROLL = 4
BISECT_STEPS = 20
SHIFT_SLACK = 1.001
ROW_SUM_FLOOR = 1e-30
NEG_INF = float("-inf")


def _pick_tile(n, candidates):
    for c in candidates:
        if n % c == 0:
            return c
    raise ValueError(f"no tile for {n}")


def _compiler_params(semantics):
    return pltpu.CompilerParams(dimension_semantics=semantics, vmem_limit_bytes=VMEM_LIMIT_BYTES)


def _silu(x):
    return x * jax.nn.sigmoid(x)


def _norm_matmul_kernel(x_ref, g_ref, w_ref, o_ref):
    x = x_ref[...]
    y = x * lax.rsqrt(jnp.mean(x * x, axis=-1, keepdims=True) + RMS_EPS) * g_ref[...]
    o_ref[...] = jnp.dot(y.astype(BF16), w_ref[...], preferred_element_type=F32)


def _matmul_residual_kernel(a_ref, w_ref, r_ref, o_ref):
    o_ref[...] = r_ref[...] + jnp.dot(a_ref[...], w_ref[...], preferred_element_type=F32)


def _matmul_residual(a_bf16, w_bf16, res):
    T, K = a_bf16.shape
    N = w_bf16.shape[1]
    tm = _pick_tile(T, (512, 256, 128))
    return pl.pallas_call(
        _matmul_residual_kernel,
        grid=(T // tm,),
        in_specs=[pl.BlockSpec((tm, K), lambda i: (i, 0)),
                  pl.BlockSpec((K, N), lambda i: (0, 0)),
                  pl.BlockSpec((tm, N), lambda i: (i, 0))],
        out_specs=pl.BlockSpec((tm, N), lambda i: (i, 0)),
        out_shape=jax.ShapeDtypeStruct((T, N), F32),
        compiler_params=_compiler_params(("parallel",)),
        name="matmul_residual",
    )(a_bf16, w_bf16, res)


def _nt_dot(a, b):
    return lax.dot_general(a, b, (((1,), (1,)), ((), ())), preferred_element_type=F32)


def _tn_dot(a, b):
    return lax.dot_general(a, b, (((0,), (0,)), ((), ())), preferred_element_type=F32)


def _gdn_kernel(x_ref, g_ref, w_ref, conv_ref, gate_ref, ogain_ref, o_ref,
                state_ref, xbuf_ref, qkv_ref, proj_ref):
    t = pl.program_id(0)
    nb, rows = x_ref.shape[0], x_ref.shape[1]

    @pl.when(t == 0)
    def _():
        state_ref[...] = jnp.zeros_like(state_ref)
        xbuf_ref[:, 0:SUBLANES, :] = jnp.zeros((nb, SUBLANES, GDN_CONV_DIM), F32)

    for b in range(nb):
        _norm_matmul_kernel(x_ref.at[b], g_ref, w_ref, proj_ref.at[b])

    def chunk(cc, carry):
        r0 = pl.multiple_of(cc * CHUNK, CHUNK)
        _gdn_chunk(t * rows + r0, r0, nb, proj_ref, conv_ref, gate_ref, ogain_ref, o_ref,
                   state_ref, xbuf_ref, qkv_ref)
        return carry

    lax.fori_loop(0, rows // CHUNK, chunk, 0)


def _gdn_chunk(row0, r0, nb, proj_ref, conv_ref, gate_ref, ogain_ref, o_ref, state_ref, xbuf_ref, qkv_ref):
    C = CHUNK
    row = row0 + lax.broadcasted_iota(jnp.int32, (C, 1), 0)
    live = row >= FRONT_PAD
    r_i = lax.broadcasted_iota(jnp.int32, (C, C), 0)
    c_i = lax.broadcasted_iota(jnp.int32, (C, C), 1)
    causal = c_i <= r_i
    strict = c_i < r_i
    neg_rate = -jnp.exp(gate_ref[0:1, :])
    betas, gcs, gc_ts, egs, eg_lasts, eks = [], [], [], [], [], []
    for b in range(nb):
        u = jnp.where(live, proj_ref[b, pl.ds(r0, C), 0:GDN_CONV_DIM], 0.0)
        xbuf_ref[b, SUBLANES:SUBLANES + C, :] = u
        conv = conv_ref[0:1, :] * xbuf_ref[b, SUBLANES - 3:SUBLANES - 3 + C, :]
        for j in range(1, GDN_CONV):
            conv = conv + conv_ref[j:j + 1, :] * xbuf_ref[b, SUBLANES - 3 + j:SUBLANES - 3 + j + C, :]
        xbuf_ref[b, 0:SUBLANES, :] = u[C - SUBLANES:C, :]
        qkv_ref[b] = _silu(conv)

        gates = jnp.where(live, proj_ref[b, pl.ds(r0, C), GDN_GATE_OFF:GDN_GATE_OFF + LANES], 0.0)
        g = neg_rate * jax.nn.softplus(gates + gate_ref[1:2, :])
        gc = jnp.dot(causal.astype(F32), g, preferred_element_type=F32,
                     precision=lax.Precision.HIGHEST)
        g_last = gc[C - 1:C, :]
        betas.append(jax.nn.sigmoid(gates))
        gcs.append(gc)
        gc_ts.append(gc.T)
        egs.append(jnp.exp(gc))
        eg_lasts.append(jnp.exp(g_last))
        eks.append(jnp.exp(g_last - gc))

    chains = [(b, h) for b in range(nb) for h in range(GDN_HEADS)]
    heads = range(len(chains))
    dot = functools.partial(jnp.dot, preferred_element_type=F32)
    col = lambda a, h: a[:, GDN_HEADS + h:GDN_HEADS + h + 1]
    qs, ks, k16s, xs, decays = [], [], [], [], []
    for b, h in chains:
        lo = h * GDN_DK
        q = qkv_ref[b, :, lo:lo + GDN_DK]
        k = qkv_ref[b, :, GDN_QK_DIM + lo:GDN_QK_DIM + lo + GDN_DK]
        v = qkv_ref[b, :, 2 * GDN_QK_DIM + h * GDN_DV:2 * GDN_QK_DIM + (h + 1) * GDN_DV]
        q = q * lax.rsqrt(jnp.sum(q * q, axis=-1, keepdims=True) + L2_EPS) * (GDN_DK ** -0.5)
        k = k * lax.rsqrt(jnp.sum(k * k, axis=-1, keepdims=True) + L2_EPS)
        kb = k * betas[b][:, h:h + 1]
        vb = v * betas[b][:, h:h + 1]
        qs.append(q)
        ks.append(k)
        k16s.append(k.astype(BF16))
        xs.append((kb, jnp.concatenate([vb, kb * col(egs[b], h)], axis=1)))
        decays.append(jnp.exp(jnp.where(causal, col(gcs[b], h) - gc_ts[b][GDN_HEADS + h:GDN_HEADS + h + 1, :],
                                        NEG_INF)))
    kk = [_nt_dot(xs[h][0].astype(BF16), k16s[h]) for h in heads]
    qk = [_nt_dot(qs[h].astype(BF16), k16s[h]) for h in heads]
    ns = [-jnp.where(strict, kk[h] * decays[h], 0.0) for h in heads]
    xs = [x for _, x in xs]
    for it in range(6):
        n16 = [n.astype(BF16) for n in ns]
        xs = [xs[h] + dot(n16[h], xs[h].astype(BF16)) for h in heads]
        if it < 5:
            ns = [dot(n16[h], n16[h]) for h in heads]
    qk = [jnp.where(causal, qk[h] * decays[h], 0.0).astype(BF16) for h in heads]
    states = [state_ref[c] for c in heads]
    s16 = [s.astype(BF16) for s in states]
    ws = [dot(xs[c][:, GDN_DV:2 * GDN_DV].astype(BF16), s16[c]) for c in heads]
    qs_s = [dot((qs[c] * col(egs[b], h)).astype(BF16), s16[c]) for c, (b, h) in enumerate(chains)]
    v_new = [(xs[c][:, 0:GDN_DV] - ws[c]).astype(BF16) for c in heads]
    os_ = [qs_s[c] + dot(qk[c], v_new[c]) for c in heads]
    kv = [_tn_dot((ks[c] * col(eks[b], h)).astype(BF16), v_new[c]) for c, (b, h) in enumerate(chains)]
    for c, (b, h) in enumerate(chains):
        state_ref[c] = states[c] * col(eg_lasts[b], h) + kv[c]
        z = proj_ref[b, pl.ds(r0, C), GDN_CONV_DIM + h * GDN_DV:GDN_CONV_DIM + (h + 1) * GDN_DV]
        o = os_[c]
        on = o * lax.rsqrt(jnp.mean(o * o, axis=-1, keepdims=True) + RMS_EPS) * ogain_ref[...]
        o_ref[b, pl.ds(r0, C), h * GDN_DV:(h + 1) * GDN_DV] = (on * _silu(z)).astype(o_ref.dtype)


def _gdn_first_kernel(x_ref, head_ref, g_ref, w_ref, conv_ref, gate_ref, ogain_ref, o_ref, h_ref, *scratch):
    t = pl.program_id(0)

    @pl.when(t == 0)
    def _():
        h_ref[...] = head_ref[...]

    @pl.when(t > 0)
    def _():
        h_ref[...] = x_ref[...]

    _gdn_kernel(h_ref, g_ref, w_ref, conv_ref, gate_ref, ogain_ref, o_ref, *scratch)


def _gdn_core(src, gain, w_bf16, conv_w, a_log, dt_bias, o_gain, head=None):
    W = w_bf16.shape[1]
    gate = jnp.zeros((2, LANES), F32)
    gate = gate.at[0, GDN_HEADS:2 * GDN_HEADS].set(a_log.astype(F32))
    gate = gate.at[1, GDN_HEADS:2 * GDN_HEADS].set(dt_bias.astype(F32))
    fix = lambda t: (0, 0)
    if head is not None:
        B, S, D = src.shape
        rows = head.shape[1]
        LP = rows + S
        assert S % rows == 0
        scratch = [pltpu.VMEM((B * GDN_HEADS, GDN_DK, GDN_DV), F32),
                   pltpu.VMEM((B, SUBLANES + CHUNK, GDN_CONV_DIM), F32),
                   pltpu.VMEM((B, CHUNK, GDN_CONV_DIM), F32),
                   pltpu.VMEM((B, rows, W), F32)]
        return pl.pallas_call(
            _gdn_first_kernel,
            grid=(LP // rows,),
            in_specs=[pl.BlockSpec((B, rows, D), lambda t: (0, jnp.maximum(t - 1, 0), 0)),
                      pl.BlockSpec((B, rows, D), lambda t: (0, 0, 0), pipeline_mode=pl.Buffered(1)),
                      pl.BlockSpec((1, D), fix),
                      pl.BlockSpec((D, W), fix, pipeline_mode=pl.Buffered(1)),
                      pl.BlockSpec((GDN_CONV, GDN_CONV_DIM), fix),
                      pl.BlockSpec((2, LANES), fix),
                      pl.BlockSpec((1, GDN_DV), fix)],
            out_specs=[pl.BlockSpec((B, rows, GDN_V_DIM), lambda t: (0, t, 0)),
                       pl.BlockSpec((B, rows, D), lambda t: (0, t, 0))],
            out_shape=[jax.ShapeDtypeStruct((B, LP, GDN_V_DIM), BF16),
                       jax.ShapeDtypeStruct((B, LP, D), F32)],
            scratch_shapes=scratch,
            compiler_params=_compiler_params(("arbitrary",)),
            name="gdn_core",
        )(src, head, gain.reshape(1, D), w_bf16, conv_w, gate, o_gain.reshape(1, GDN_DV))
    h3d = src
    B, LP, D = h3d.shape
    rows = _pick_tile(LP, GDN_ROW_TILES)
    return pl.pallas_call(
        _gdn_kernel,
        grid=(LP // rows,),
        in_specs=[pl.BlockSpec((B, rows, D), lambda t: (0, t, 0)),
                  pl.BlockSpec((1, D), fix),
                  pl.BlockSpec((D, W), fix, pipeline_mode=pl.Buffered(1)),
                  pl.BlockSpec((GDN_CONV, GDN_CONV_DIM), fix),
                  pl.BlockSpec((2, LANES), fix),
                  pl.BlockSpec((1, GDN_DV), fix)],
        out_specs=pl.BlockSpec((B, rows, GDN_V_DIM), lambda t: (0, t, 0)),
        out_shape=jax.ShapeDtypeStruct((B, LP, GDN_V_DIM), BF16),
        scratch_shapes=[pltpu.VMEM((B * GDN_HEADS, GDN_DK, GDN_DV), F32),
                        pltpu.VMEM((B, SUBLANES + CHUNK, GDN_CONV_DIM), F32),
                        pltpu.VMEM((B, CHUNK, GDN_CONV_DIM), F32),
                        pltpu.VMEM((B, rows, W), F32)],
        compiler_params=_compiler_params(("arbitrary",)),
        name="gdn_core",
    )(h3d, gain.reshape(1, D), w_bf16, conv_w, gate, o_gain.reshape(1, GDN_DV))


def _moe_router_kernel(h_ref, gain_ref, wr_ref, br_ref, hx_ref, gid_ref, rank_ref, cnt_ref, carry_ref):
    i = pl.program_id(0)
    TM = h_ref.shape[0]
    lane = lax.broadcasted_iota(jnp.int32, (1, LANES), 1)

    @pl.when(i == 0)
    def _():
        carry_ref[...] = jnp.zeros_like(carry_ref)

    x = h_ref[...]
    xn = x * lax.rsqrt(jnp.mean(x * x, axis=-1, keepdims=True) + RMS_EPS) * gain_ref[...]
    logits = jnp.dot(xn, wr_ref[...], preferred_element_type=F32,
                     precision=lax.Precision.HIGHEST) + br_ref[...]
    e_log = logits
    g_lane = lane - N_EXPERTS
    g_log = jnp.where((g_lane >= 0) & (g_lane < N_GROUPS), logits, NEG_INF)
    g_max = jnp.max(g_log, axis=-1, keepdims=True)
    g_sel = jnp.min(jnp.where(g_log == g_max, g_lane, LANES), axis=-1, keepdims=True)
    g_sel = jnp.minimum(g_sel, N_GROUPS - 1)
    g_w = 1.0 / jnp.sum(jnp.exp(g_log - g_max), axis=-1, keepdims=True)
    in_grp = (lane >= g_sel * EXPERTS_PER_GROUP) & (lane < (g_sel + 1) * EXPERTS_PER_GROUP)
    l0 = jnp.where(in_grp, e_log, NEG_INF)
    m1 = jnp.max(l0, axis=-1, keepdims=True)
    i1 = jnp.min(jnp.where(l0 == m1, lane, LANES), axis=-1, keepdims=True)
    l1 = jnp.where(lane == i1, NEG_INF, l0)
    m2 = jnp.max(l1, axis=-1, keepdims=True)
    i2 = jnp.min(jnp.where(l1 == m2, lane, LANES), axis=-1, keepdims=True)
    p2 = jnp.exp(m2 - m1)
    w1 = g_w / (1.0 + p2)
    w2 = g_w * p2 / (1.0 + p2)
    j1 = i1 - g_sel * EXPERTS_PER_GROUP
    j2 = i2 - g_sel * EXPERTS_PER_GROUP
    hx_ref[:, 0:xn.shape[1]] = xn
    hx_ref[:, xn.shape[1]:] = jnp.where(lane == j1, w1, 0.0) + jnp.where(lane == j2, w2, 0.0)

    onehot = jnp.where(lane == g_sel, 1.0, 0.0)
    r_i = lax.broadcasted_iota(jnp.int32, (TM, TM), 0)
    c_i = lax.broadcasted_iota(jnp.int32, (TM, TM), 1)
    before = jnp.where(c_i < r_i, 1.0, 0.0).astype(BF16)
    prior = jnp.dot(before, onehot.astype(BF16), preferred_element_type=F32) + carry_ref[...]
    rank = jnp.sum(jnp.where(lane == g_sel, prior, 0.0), axis=-1, keepdims=True)
    carry_ref[...] += jnp.sum(onehot, axis=0, keepdims=True)
    cnt_ref[...] = carry_ref[...]
    gid_ref[0] = jnp.broadcast_to(g_sel.astype(F32), (TM, LANES)).T[0:1, :].astype(jnp.int32)
    rank_ref[0] = jnp.broadcast_to(rank, (TM, LANES)).T[0:1, :].astype(jnp.int32)


def _moe_dispatch_kernel(seg_ref, pos_ref, hx_ref, xs_ref, zero_ref, sem, zero_sem):
    TM = hx_ref.shape[0]

    @pl.when(pl.program_id(0) == 0)
    def _():
        zero_ref[...] = jnp.zeros_like(zero_ref)

        def zero_tile(t):
            fill = pltpu.make_async_copy(zero_ref, xs_ref.at[pl.ds(t * TM, TM), :], zero_sem)
            fill.start()
            fill.wait()

        for g in range(N_GROUPS):
            first = seg_ref[g - 1] if g else 0
            pl.when(seg_ref[g] > first)(functools.partial(zero_tile, seg_ref[g] - 1))
            tail = seg_ref[N_GROUPS - 1] + g
            pl.when(tail < xs_ref.shape[0] // TM)(functools.partial(zero_tile, tail))

    def start(a, carry):
        for b in range(SUBLANES):
            r = a * SUBLANES + b
            pltpu.make_async_copy(hx_ref.at[pl.ds(r, 1), :], xs_ref.at[pl.ds(pos_ref[0, 0, r], 1), :],
                                  sem).start()
        return carry

    lax.fori_loop(0, TM // SUBLANES, start, 0)
    pltpu.make_async_copy(hx_ref, xs_ref.at[pl.ds(0, TM), :], sem).wait()


def _moe_expert_kernel(grp_ref, xs_ref, wgu_ref, wd_ref, ys_ref, xn_ref, acc_ref):
    j = pl.program_id(0)
    e = pl.program_id(1)
    D = xn_ref.shape[1]
    lane = lax.broadcasted_iota(jnp.int32, (1, LANES), 1)

    @pl.when(grp_ref[j] >= 0)
    def _():
        @pl.when(e == 0)
        def _():
            xn_ref[...] = xs_ref[:, 0:D].astype(BF16)
            acc_ref[...] = jnp.zeros_like(acc_ref)

        hh = jnp.dot(xn_ref[...], wgu_ref[0], preferred_element_type=F32)
        cw = jnp.sum(jnp.where(lane == e, xs_ref[:, D:], 0.0), axis=-1, keepdims=True)
        act = _silu(hh[:, 0:EXPERT_FF]) * hh[:, EXPERT_FF:2 * EXPERT_FF] * cw
        acc_ref[...] += jnp.dot(act.astype(BF16), wd_ref[0], preferred_element_type=F32)

    last = e == EXPERTS_PER_GROUP - 1

    @pl.when(last & (grp_ref[j] >= 0))
    def _():
        ys_ref[...] = acc_ref[...]

    @pl.when(last & (grp_ref[j] < 0))
    def _():
        ys_ref[...] = jnp.zeros_like(ys_ref)


def _moe_combine_kernel(pos_ref, h_ref, ys_ref, o_ref, stage_ref, sem):
    TM = h_ref.shape[0]

    def start(a, carry):
        for b in range(SUBLANES):
            r = a * SUBLANES + b
            pltpu.make_async_copy(ys_ref.at[pl.ds(pos_ref[0, 0, r], 1), :], stage_ref.at[pl.ds(r, 1), :],
                                  sem).start()
        return carry

    lax.fori_loop(0, TM // SUBLANES, start, 0)
    pltpu.make_async_copy(ys_ref.at[pl.ds(0, TM), :], stage_ref, sem).wait()
    o_ref[...] = h_ref[...] + stage_ref[...]


def _moe(h2d, gain, w_group, b_group, w_expert, b_expert, wgu, wd, layer, frames=None):
    T, D = h2d.shape
    tr = _pick_tile(T, MOE_ROUTER_TILES)
    tm = _pick_tile(T, MOE_ROW_TILES)
    n_tiles = T // tm + N_GROUPS
    XW = D + LANES
    assert N_EXPERTS + N_GROUPS <= LANES
    wr = jnp.zeros((D, LANES), F32)
    wr = wr.at[:, 0:N_EXPERTS].set(w_expert.astype(F32))
    wr = wr.at[:, N_EXPERTS:N_EXPERTS + N_GROUPS].set(w_group.astype(F32))
    br = jnp.zeros((1, LANES), F32)
    br = br.at[0, 0:N_EXPERTS].set(b_expert.astype(F32))
    br = br.at[0, N_EXPERTS:N_EXPERTS + N_GROUPS].set(b_group.astype(F32))

    hx, gid, rank, counts = pl.pallas_call(
        _moe_router_kernel,
        grid=(T // tr,),
        in_specs=[pl.BlockSpec((tr, D), lambda i: (i, 0)),
                  pl.BlockSpec((1, D), lambda i: (0, 0)),
                  pl.BlockSpec((D, LANES), lambda i: (0, 0)),
                  pl.BlockSpec((1, LANES), lambda i: (0, 0))],
        out_specs=[pl.BlockSpec((tr, XW), lambda i: (i, 0)),
                   pl.BlockSpec((1, 1, tr), lambda i: (i, 0, 0)),
                   pl.BlockSpec((1, 1, tr), lambda i: (i, 0, 0)),
                   pl.BlockSpec((1, LANES), lambda i: (0, 0))],
        out_shape=[jax.ShapeDtypeStruct((T, XW), F32),
                   jax.ShapeDtypeStruct((T // tr, 1, tr), jnp.int32),
                   jax.ShapeDtypeStruct((T // tr, 1, tr), jnp.int32),
                   jax.ShapeDtypeStruct((1, LANES), F32)],
        scratch_shapes=[pltpu.VMEM((1, LANES), F32)],
        compiler_params=_compiler_params(("arbitrary",)),
        name="moe_router",
    )(h2d, gain.reshape(1, D), wr, br)

    seg_tiles = -(-counts[0, 0:N_GROUPS].astype(jnp.int32) // tm)
    seg_end = jnp.cumsum(seg_tiles)
    seg_start = seg_end - seg_tiles
    pos = (seg_start * tm)[gid.reshape(T)] + rank.reshape(T)
    tile_ids = jnp.arange(n_tiles, dtype=jnp.int32)
    tile_group = jnp.sum((tile_ids[:, None] >= seg_end[None, :]).astype(jnp.int32), axis=1)
    tile_group = jnp.where(tile_ids < seg_end[N_GROUPS - 1], tile_group, -1)
    pos3 = pos.reshape(T // tm, 1, tm)

    xs = pl.pallas_call(
        _moe_dispatch_kernel,
        grid_spec=pltpu.PrefetchScalarGridSpec(
            num_scalar_prefetch=1,
            grid=(T // tm,),
            in_specs=[pl.BlockSpec((1, 1, tm), lambda i, seg: (i, 0, 0), memory_space=pltpu.SMEM),
                      pl.BlockSpec((tm, XW), lambda i, seg: (i, 0))],
            out_specs=pl.BlockSpec(memory_space=pl.ANY),
            scratch_shapes=[pltpu.VMEM((tm, XW), F32), pltpu.SemaphoreType.DMA(()),
                            pltpu.SemaphoreType.DMA(())]),
        out_shape=jax.ShapeDtypeStruct((n_tiles * tm, XW), F32),
        compiler_params=_compiler_params(("arbitrary",)),
        name="moe_dispatch",
    )(seg_end, pos3, hx)

    def expert_index(j, e, grp):
        return (layer * N_EXPERTS + jnp.maximum(grp[j], 0) * EXPERTS_PER_GROUP + e, 0, 0)

    ys = pl.pallas_call(
        _moe_expert_kernel,
        grid_spec=pltpu.PrefetchScalarGridSpec(
            num_scalar_prefetch=1,
            grid=(n_tiles, EXPERTS_PER_GROUP),
            in_specs=[pl.BlockSpec((tm, XW), lambda j, e, grp: (j, 0)),
                      pl.BlockSpec((1, D, 2 * EXPERT_FF), expert_index),
                      pl.BlockSpec((1, EXPERT_FF, D), expert_index)],
            out_specs=pl.BlockSpec((tm, D), lambda j, e, grp: (j, 0)),
            scratch_shapes=[pltpu.VMEM((tm, D), BF16), pltpu.VMEM((tm, D), F32)]),
        out_shape=jax.ShapeDtypeStruct((n_tiles * tm, D), F32),
        compiler_params=_compiler_params(("parallel", "arbitrary")),
        name="moe_experts",
    )(tile_group, xs, wgu, wd)

    if frames is None:
        tc, out_rows, out_index = tm, T, lambda i: (i, 0)
    else:
        batch_rows, n_frames, first = frames
        tc = ROW_ALIGN
        assert batch_rows % tc == 0 and n_frames % tc == 0 and first % tc == 0 and first >= tc
        out_rows = (T // batch_rows) * n_frames

        def out_index(i):
            b, t = i // (batch_rows // tc), i % (batch_rows // tc)
            return (b * (n_frames // tc) + jnp.maximum(t - first // tc, 0), 0)

    return pl.pallas_call(
        _moe_combine_kernel,
        grid=(T // tc,),
        in_specs=[pl.BlockSpec((1, 1, tc), lambda i: (i, 0, 0), memory_space=pltpu.SMEM),
                  pl.BlockSpec((tc, D), lambda i: (i, 0)),
                  pl.BlockSpec(memory_space=pl.ANY)],
        out_specs=pl.BlockSpec((tc, D), out_index),
        out_shape=jax.ShapeDtypeStruct((out_rows, D), F32),
        scratch_shapes=[pltpu.VMEM((tc, D), F32), pltpu.SemaphoreType.DMA(())],
        compiler_params=_compiler_params(("arbitrary",)),
        name="moe_combine",
    )(pos.reshape(T // tc, 1, tc), h2d, ys)


def _dsa_prep_kernel(p_ref, qg_ref, kg_ref, lng_ref, lnb_ref, q_ref, kt_ref, v_ref, qit_ref, ki_ref, wit_ref):
    lane = lax.broadcasted_iota(jnp.int32, (1, LANES), 1)
    for h in range(ATT_HEADS):
        x = p_ref[:, DSA_Q_OFF + h * ATT_HD:DSA_Q_OFF + (h + 1) * ATT_HD]
        y = x * lax.rsqrt(jnp.mean(x * x, axis=-1, keepdims=True) + RMS_EPS) * qg_ref[...]
        q_ref[:, h * ATT_HD:(h + 1) * ATT_HD] = (y * (ATT_HD ** -0.5)).astype(BF16)
    for n in range(ATT_KV_HEADS):
        x = p_ref[:, DSA_K_OFF + n * ATT_HD:DSA_K_OFF + (n + 1) * ATT_HD]
        y = x * lax.rsqrt(jnp.mean(x * x, axis=-1, keepdims=True) + RMS_EPS) * kg_ref[...]
        kt_ref[0, n * ATT_HD:(n + 1) * ATT_HD, :] = y.T.astype(BF16)
    ones_col = jnp.where(lane == 0, 1.0, 0.0).astype(BF16)
    for n in range(ATT_KV_HEADS):
        v_ref[:, n * V_WIDE:n * V_WIDE + ATT_HD] = p_ref[:, DSA_V_OFF + n * ATT_HD:DSA_V_OFF + (n + 1) * ATT_HD].astype(BF16)
        v_ref[:, n * V_WIDE + ATT_HD:(n + 1) * V_WIDE] = jnp.broadcast_to(ones_col, (v_ref.shape[0], LANES))
    qit_ref[0] = p_ref[:, DSA_QI_OFF:DSA_QI_OFF + IDX_HEADS * LANES].T.astype(BF16)
    x = p_ref[:, DSA_KI_OFF:DSA_KI_OFF + LANES]
    live = lane < IDX_HD
    mu = jnp.sum(x, axis=-1, keepdims=True) * (1.0 / IDX_HD)
    xc = jnp.where(live, x - mu, 0.0)
    var = jnp.sum(xc * xc, axis=-1, keepdims=True) * (1.0 / IDX_HD)
    ki = xc * lax.rsqrt(var + RMS_EPS) * lng_ref[...] + lnb_ref[...]
    ki_ref[...] = jnp.where(live, ki, 0.0).astype(BF16)
    wit_ref[0] = (p_ref[:, DSA_WI_OFF:DSA_WI_OFF + LANES] * (IDX_HEADS ** -0.5 * IDX_HD ** -0.5)).T


def _dsa_in_kernel(x_ref, g_ref, w_ref, qg_ref, kg_ref, lng_ref, lnb_ref,
                   q_ref, kt_ref, v_ref, qit_ref, ki_ref, wit_ref, proj_ref):
    _norm_matmul_kernel(x_ref, g_ref, w_ref, proj_ref)
    _dsa_prep_kernel(proj_ref, qg_ref, kg_ref, lng_ref, lnb_ref, q_ref, kt_ref, v_ref, qit_ref, ki_ref, wit_ref)


def _dsa_in(h2d, gain, w_bf16, q_gain, k_gain, ln_g, ln_b):
    T, D = h2d.shape
    W = w_bf16.shape[1]
    tm = KEY_TILE
    assert T % tm == 0 and ATT_TILE == KEY_TILE
    lng = jnp.zeros((1, LANES), F32).at[0, 0:IDX_HD].set(ln_g.astype(F32))
    lnb = jnp.zeros((1, LANES), F32).at[0, 0:IDX_HD].set(ln_b.astype(F32))
    row = lambda i: (i, 0)
    fix = lambda i: (0, 0)
    blk = lambda i: (i, 0, 0)
    return pl.pallas_call(
        _dsa_in_kernel,
        grid=(T // tm,),
        in_specs=[pl.BlockSpec((tm, D), row), pl.BlockSpec((1, D), fix), pl.BlockSpec((D, W), fix),
                  pl.BlockSpec((1, ATT_HD), fix), pl.BlockSpec((1, ATT_HD), fix),
                  pl.BlockSpec((1, LANES), fix), pl.BlockSpec((1, LANES), fix)],
        out_specs=[pl.BlockSpec((tm, ATT_Q_DIM), row), pl.BlockSpec((1, ATT_KV_DIM, tm), blk),
                   pl.BlockSpec((tm, ATT_KV_HEADS * V_WIDE), row), pl.BlockSpec((1, IDX_HEADS * LANES, tm), blk),
                   pl.BlockSpec((tm, LANES), row), pl.BlockSpec((1, LANES, tm), blk)],
        out_shape=[jax.ShapeDtypeStruct((T, ATT_Q_DIM), BF16), jax.ShapeDtypeStruct((T // tm, ATT_KV_DIM, tm), BF16),
                   jax.ShapeDtypeStruct((T, ATT_KV_HEADS * V_WIDE), BF16),
                   jax.ShapeDtypeStruct((T // tm, IDX_HEADS * LANES, tm), BF16),
                   jax.ShapeDtypeStruct((T, LANES), BF16), jax.ShapeDtypeStruct((T // tm, LANES, tm), F32)],
        scratch_shapes=[pltpu.VMEM((tm, W), F32)],
        compiler_params=_compiler_params(("parallel",)),
        name="dsa_in",
    )(h2d, gain.reshape(1, D), w_bf16, q_gain.reshape(1, ATT_HD), k_gain.reshape(1, ATT_HD), lng, lnb)


def _dsa_kernel(q_ref, qit_ref, wit_ref, kt_ref, v_ref, ki_ref, bias_ref, bmax_ref, o_ref,
                sc_ref, mask_ref, qg_ref, m_ref, acc_ref, knorm_ref, *, n_select, live_rows):
    i = pl.program_id(1)
    TQ = ATT_TILE
    TK = KEY_TILE
    HALVES = TK // LANES

    @pl.when(i == 0)
    def _():
        def body(kt, mx):
            kk = kt_ref[0, kt].astype(F32)
            return tuple(jnp.maximum(mx[n], jnp.sum(jnp.square(kk[n * ATT_HD:(n + 1) * ATT_HD, :]),
                                                    axis=0, keepdims=True)) for n in range(ATT_KV_HEADS))

        mx = lax.fori_loop(0, kt_ref.shape[1], body,
                           tuple(jnp.zeros((1, TK), F32) for _ in range(ATT_KV_HEADS)))
        for n in range(ATT_KV_HEADS):
            knorm_ref[n] = jnp.broadcast_to(jnp.sqrt(jnp.max(mx[n], axis=-1, keepdims=True)), (1, LANES))

    @pl.when(i * TQ >= live_rows)
    def _():
        o_ref[...] = jnp.zeros_like(o_ref)

    @pl.when(i * TQ < live_rows)
    def _():
        _dsa_tile(i, q_ref, qit_ref, wit_ref, kt_ref, v_ref, ki_ref, bias_ref, bmax_ref, o_ref,
                  sc_ref, mask_ref, qg_ref, m_ref, acc_ref, knorm_ref, n_select, TQ, TK, HALVES)


def _dsa_tile(i, q_ref, qit_ref, wit_ref, kt_ref, v_ref, ki_ref, bias_ref, bmax_ref, o_ref,
              sc_ref, mask_ref, qg_ref, m_ref, acc_ref, knorm_ref, n_select, TQ, TK, HALVES):
    assert TQ == TK
    nkt = i + 1
    fmin = float(jnp.finfo(F32).min)
    kf = float(n_select)

    chunk_shift = int(math.log2(CHUNK))
    q_pos = i * TQ + lax.broadcasted_iota(jnp.int32, (1, TQ), 1)
    q_chunk = jnp.maximum(lax.shift_right_logical(q_pos, chunk_shift), FRONT_PAD // CHUNK)
    k_row = lax.broadcasted_iota(jnp.int32, (TK, 1), 0)

    def halves(x):
        return [x[:, j * LANES:(j + 1) * LANES] for j in range(HALVES)]

    def fold(x, op):
        parts = [x[r * SUBLANES:(r + 1) * SUBLANES, :] for r in range(TK // SUBLANES)]
        while len(parts) > 1:
            parts = [op(parts[j], parts[j + 1]) for j in range(0, len(parts), 2)]
        return parts[0]

    def score_tiles(kts):
        for kt in kts:
            keys = ki_ref[0, pl.ds(pl.multiple_of(kt * TK, TK), TK), :]
            acc = None
            for h in range(IDX_HEADS):
                d = jnp.dot(keys, qit_ref[0, 0, h * LANES:(h + 1) * LANES, :], preferred_element_type=F32)
                term = wit_ref[0, 0, h:h + 1, :] * jnp.maximum(d, 0.0)
                acc = term if acc is None else acc + term
            sc_ref[kt] = acc
            mask_ref[kt] = acc.T

    def grouped_loop(n, fn, group):
        def body(j, carry):
            fn([group * j + r for r in range(group)])
            return carry

        lax.fori_loop(0, lax.shift_right_logical(n, int(math.log2(group))), body, 0)
        size = group // 2
        while size >= 1:
            first = lax.bitwise_and(n, ~(2 * size - 1))

            @pl.when(lax.bitwise_and(n, size) != 0)
            def _(size=size, first=first):
                fn([first + r for r in range(size)])

            size //= 2

    grouped_loop(nkt, score_tiles, INDEXER_GROUP)

    def mask_inadmissible(kt):
        k_pos = kt * TK + k_row
        adm = (lax.shift_right_logical(k_pos, chunk_shift) <= q_chunk) & (k_pos >= FRONT_PAD)
        sc_ref[kt] = jnp.where(adm, sc_ref[kt], NEG_INF)

    mask_inadmissible(0)

    @pl.when(nkt > 1)
    def _():
        mask_inadmissible(nkt - 1)

    n_adm = CHUNK * (q_chunk + 1) - FRONT_PAD
    need = n_adm > n_select

    def count_where(pred):
        def one(kt):
            return fold(jnp.where(pred(sc_ref[kt], kt), 1.0, 0.0), jnp.add)

        def two(j, acc):
            return acc + (one(2 * j) + one(2 * j + 1))

        acc = lax.fori_loop(0, lax.shift_right_logical(nkt, 1), two, jnp.zeros((SUBLANES, TQ), F32))
        acc = lax.cond(lax.bitwise_and(nkt, 1) == 1, lambda a: a + one(nkt - 1), lambda a: a, acc)
        return jnp.sum(acc, axis=0, keepdims=True)

    def count_ge(t):
        return count_where(lambda s, kt: s >= t)

    def minmax_body(kt, carry):
        mn, mx = carry
        s = sc_ref[kt]
        mn = jnp.minimum(mn, fold(jnp.where(s == NEG_INF, jnp.inf, s), jnp.minimum))
        mx = jnp.maximum(mx, fold(s, jnp.maximum))
        return mn, mx

    mn_l, mx_l = lax.fori_loop(0, nkt, minmax_body,
                               (jnp.full((SUBLANES, TQ), jnp.inf, F32), jnp.full((SUBLANES, TQ), NEG_INF, F32)))
    s_min = jnp.min(mn_l, axis=0, keepdims=True)
    s_max = jnp.max(mx_l, axis=0, keepdims=True)
    lo0 = s_min
    hi0 = s_max + (s_max - s_min) + 1.0
    cnt0 = n_adm.astype(F32)
    active0 = jnp.where(need, 1.0, 0.0)

    def any_active(active):
        return jnp.max(active) > 0.0

    def bis_step(lo, hi, cnt, active):
        mid = lo + (hi - lo) * 0.5
        c = count_ge(mid)
        ge = c >= kf
        moving = (mid > lo) & (mid < hi) & (active > 0.0)
        lo_n = jnp.where(moving & ge, mid, lo)
        hi_n = jnp.where(moving & (~ge), mid, hi)
        cnt_n = jnp.where(moving & ge, c, cnt)
        active_n = jnp.where(moving & (cnt_n != kf), 1.0, 0.0)
        return lo_n, hi_n, cnt_n, active_n

    def bis_body(st):
        it, rest = st[0], st[1:]
        for _ in range(BISECT_UNROLL):
            rest = bis_step(*rest)
        return (it + BISECT_UNROLL,) + rest

    _, lo, hi, cnt, _ = lax.while_loop(
        lambda st: jnp.logical_and(st[0] < BISECT_STEPS, any_active(st[4])), bis_body,
        (jnp.int32(0), lo0, hi0, cnt0, active0))

    def snap_body(st):
        lo, hi, cnt, active = st

        def body(kt, mx):
            s = sc_ref[kt]
            return jnp.maximum(mx, fold(jnp.where(s < hi, s, NEG_INF), jnp.maximum))

        v1 = jnp.max(lax.fori_loop(0, nkt, body, jnp.full((SUBLANES, TQ), NEG_INF, F32)),
                     axis=0, keepdims=True)
        c1 = count_ge(v1)
        on = active > 0.0
        hit = on & (c1 >= kf)
        lo_n = jnp.where(hit, v1, lo)
        cnt_n = jnp.where(hit, c1, cnt)
        hi_n = jnp.where(on & (~hit), v1, hi)
        return lo_n, hi_n, cnt_n, jnp.where(on & (~hit), 1.0, 0.0)

    unsettled = jnp.where(need & (cnt != kf), 1.0, 0.0)
    lo, hi, cnt, _ = lax.while_loop(lambda st: any_active(st[3]), snap_body, (lo, hi, cnt, unsettled))

    thr = jnp.where(need, lo, fmin)
    tied = need & (cnt > kf)
    big_pos = jnp.int32(2 ** 30)

    def tie_break():
        want = kf - count_where(lambda s, kt: s > thr)

        def body(_, st):
            jl, jh = st
            jm = lax.shift_right_arithmetic(jl + jh, 1)
            c = count_where(lambda s, kt: (s == thr) & (kt * TK + k_row <= jm))
            ok = c >= want
            return jnp.where(ok, jl, jm), jnp.where(ok, jm, jh)

        n_keys = sc_ref.shape[0] * TK
        steps = int(math.ceil(math.log2(n_keys + 1))) + 1
        jl0 = jnp.full((1, TQ), -1, jnp.int32)
        jh0 = jnp.full((1, TQ), n_keys, jnp.int32)
        _, jh = lax.fori_loop(0, steps, body, (jl0, jh0))
        return jnp.where(tied, jh, big_pos)

    pos_cap = lax.cond(jnp.max(jnp.where(tied, 1.0, 0.0)) > 0.0, tie_break,
                       lambda: jnp.full((1, TQ), big_pos, jnp.int32))

    shift_w = None
    for n in range(ATT_KV_HEADS):
        for g in range(ATT_GROUP):
            h = n * ATT_GROUP + g
            qg_ref[n, g * TQ:(g + 1) * TQ, :] = q_ref[0, :, h * ATT_HD:(h + 1) * ATT_HD]
            qh = q_ref[0, :, h * ATT_HD:(h + 1) * ATT_HD].astype(F32)
            q_len = jnp.sqrt(jnp.sum(qh * qh, axis=-1, keepdims=True))
            bound = q_len * knorm_ref[n] * SHIFT_SLACK + (bmax_ref[h] + SHIFT_SLACK - 1.0)
            shift_w = bound if shift_w is None else jnp.maximum(shift_w, bound)

    def lane_wide(row):
        return jnp.broadcast_to(row, (LANES, TQ)).T

    thr_w = lane_wide(thr)
    cap_w = lane_wide(pos_cap)
    chunk_w = lane_wide(q_chunk)
    k_lane = lax.broadcasted_iota(jnp.int32, (1, LANES), 1)

    def to_mask(kt, edge_tile):
        out = []
        for j, s in enumerate(halves(mask_ref[kt])):
            k_pos = kt * TK + j * LANES + k_lane
            sel = (s > thr_w) | ((s == thr_w) & (k_pos <= cap_w))
            if edge_tile:
                sel = sel & (lax.shift_right_logical(k_pos, chunk_shift) <= chunk_w) & (k_pos >= FRONT_PAD)
            out.append(jnp.where(sel, -shift_w, NEG_INF))
        mask_ref[kt] = jnp.concatenate(out, axis=1)

    def inner_mask(kt, carry):
        to_mask(kt, False)
        return carry

    lax.fori_loop(1, nkt - 1, inner_mask, 0)
    to_mask(0, True)

    @pl.when(nkt > 1)
    def _():
        to_mask(nkt - 1, True)

    acc_ref[...] = jnp.zeros_like(acc_ref)

    kv_heads = range(ATT_KV_HEADS)

    def raw_logits(kts):
        return [[jnp.dot(qg_ref[n], kt_ref[0, kt, n * ATT_HD:(n + 1) * ATT_HD, :], preferred_element_type=F32)
                 for n in kv_heads] for kt in kts]

    def masked_logits(s, mask, n, g, b0):
        sg = s[g * TQ:(g + 1) * TQ, :] + mask
        if b0 is not None:
            h = n * ATT_GROUP + g
            sg = sg + jnp.concatenate([bias_ref[h, b0 + j] for j in range(HALVES)], axis=1)
        return sg

    def plain_mask(kt):
        return jnp.where(mask_ref[kt] == NEG_INF, NEG_INF, 0.0)

    def max_pass(kts, b0):
        raw = raw_logits(kts)
        masks = [plain_mask(kt) for kt in kts]
        for n in kv_heads:
            for g in range(ATT_GROUP):
                rows = slice(g * TQ, (g + 1) * TQ)
                m = m_ref[n, rows, :]
                for t in range(len(kts)):
                    for part in halves(masked_logits(raw[t][n], masks[t], n, g, b0)):
                        m = jnp.maximum(m, part)
                m_ref[n, rows, :] = m

    def exp_pass(kts, b0, exact_max):
        raw = raw_logits(kts)
        masks = [plain_mask(kt) if exact_max else mask_ref[kt] for kt in kts]
        for n in kv_heads:
            ps = [[] for _ in kts]
            for g in range(ATT_GROUP):
                rows = slice(g * TQ, (g + 1) * TQ)
                m = m_ref[n, rows, :] if exact_max else 0.0
                for t in range(len(kts)):
                    parts = [jnp.exp(part - m) if exact_max else jnp.exp(part)
                             for part in halves(masked_logits(raw[t][n], masks[t], n, g, b0))]
                    ps[t].append(jnp.concatenate(parts, axis=1).astype(BF16))
            pv = None
            for t, kt in enumerate(kts):
                vv = v_ref[0, pl.ds(pl.multiple_of(kt * TK, TK), TK), n * V_WIDE:(n + 1) * V_WIDE]
                d = jnp.dot(jnp.concatenate(ps[t], axis=0), vv, preferred_element_type=F32)
                pv = d if pv is None else pv + d
            acc_ref[n] += pv

    def run_pass(tile_fn):
        grouped_loop(jnp.maximum(i - 1, 0), lambda kts: tile_fn(kts, None), ATTEND_GROUP)

        @pl.when(i >= 1)
        def _():
            tile_fn([i - 1], 0)

        tile_fn([i], HALVES)

    run_pass(functools.partial(exp_pass, exact_max=False))

    def row_sums(n):
        return acc_ref[n, :, ATT_HD:ATT_HD + 1]

    def smallest_row_sum():
        return functools.reduce(jnp.minimum, [jnp.min(row_sums(n)) for n in range(ATT_KV_HEADS)])

    @pl.when(jnp.logical_not(smallest_row_sum() > ROW_SUM_FLOOR))
    def _():
        m_ref[...] = jnp.full(m_ref.shape, -1e30, F32)
        run_pass(max_pass)
        for n in range(ATT_KV_HEADS):
            m_ref[n] = jnp.broadcast_to(jnp.max(m_ref[n], axis=-1, keepdims=True), m_ref.shape[1:])
        acc_ref[...] = jnp.zeros_like(acc_ref)
        run_pass(functools.partial(exp_pass, exact_max=True))

    for n in range(ATT_KV_HEADS):
        inv_l = 1.0 / row_sums(n)
        for g in range(ATT_GROUP):
            h = n * ATT_GROUP + g
            rows = slice(g * TQ, (g + 1) * TQ)
            o_ref[0, :, h * ATT_HD:(h + 1) * ATT_HD] = (acc_ref[n, rows, 0:ATT_HD] * inv_l[rows, :]).astype(o_ref.dtype)


def _t5_bucket(rel):
    nb = REL_BUCKETS // 2
    max_exact = nb // 2
    ret = jnp.where(rel > 0, nb, 0)
    n = jnp.abs(rel)
    nf = jnp.maximum(n, 1).astype(jnp.float32)
    large = max_exact + (jnp.log(nf / max_exact) / math.log(REL_MAX_DIST / max_exact)
                         * (nb - max_exact)).astype(jnp.int32)
    large = jnp.minimum(large, nb - 1)
    return ret + jnp.where(n < max_exact, n, large)


def _dsa_core(q, kt, v, qit, ki, wit, rel_bias, n_select, live_rows):
    B, LP, _ = q.shape
    nq = LP // ATT_TILE
    nkt = LP // KEY_TILE
    a = jnp.arange(ATT_TILE)[:, None]
    xx = jnp.arange(2 * KEY_TILE)[None, :]
    rel = (xx - KEY_TILE) - a
    far = rel_bias.astype(F32)[REL_BUCKETS // 2 - 1]
    onehot = jax.nn.one_hot(_t5_bucket(rel), REL_BUCKETS, dtype=F32)
    table = jnp.einsum("abk,kh->abh", onehot, rel_bias.astype(F32),
                       precision=lax.Precision.HIGHEST) - far
    bias_near = jnp.transpose(table.reshape(ATT_TILE, 2 * KEY_TILE // LANES, LANES, ATT_HEADS), (3, 1, 0, 2))
    bias_max = jnp.broadcast_to(jnp.maximum(jnp.max(table, axis=(0, 1)), 0.0)[:, None, None],
                                (ATT_HEADS, 1, LANES))
    kern = functools.partial(_dsa_kernel, n_select=n_select, live_rows=live_rows)
    tile = lambda b, i: (b, i, 0)
    tile4 = lambda b, i: (b, i, 0, 0)
    whole3 = lambda b, i: (b, 0, 0)
    whole4 = lambda b, i: (b, 0, 0, 0)
    once = pl.Buffered(1)
    return pl.pallas_call(
        kern,
        grid=(B, nq),
        in_specs=[pl.BlockSpec((1, ATT_TILE, ATT_Q_DIM), tile),
                  pl.BlockSpec((1, 1, IDX_HEADS * LANES, ATT_TILE), tile4),
                  pl.BlockSpec((1, 1, LANES, ATT_TILE), tile4),
                  pl.BlockSpec((1, nkt, ATT_KV_DIM, KEY_TILE), whole4, pipeline_mode=once),
                  pl.BlockSpec((1, LP, ATT_KV_HEADS * V_WIDE), whole3, pipeline_mode=once),
                  pl.BlockSpec((1, LP, LANES), whole3, pipeline_mode=once),
                  pl.BlockSpec((ATT_HEADS, 2 * KEY_TILE // LANES, ATT_TILE, LANES), lambda b, i: (0, 0, 0, 0),
                               pipeline_mode=once),
                  pl.BlockSpec((ATT_HEADS, 1, LANES), lambda b, i: (0, 0, 0))],
        out_specs=pl.BlockSpec((1, ATT_TILE, ATT_Q_DIM), tile),
        out_shape=jax.ShapeDtypeStruct((B, LP, ATT_Q_DIM), BF16),
        scratch_shapes=[pltpu.VMEM((nkt, KEY_TILE, ATT_TILE), F32),
                        pltpu.VMEM((nkt, ATT_TILE, KEY_TILE), F32),
                        pltpu.VMEM((ATT_KV_HEADS, ATT_GROUP * ATT_TILE, ATT_HD), BF16),
                        pltpu.VMEM((ATT_KV_HEADS, ATT_GROUP * ATT_TILE, LANES), F32),
                        pltpu.VMEM((ATT_KV_HEADS, ATT_GROUP * ATT_TILE, V_WIDE), F32),
                        pltpu.VMEM((ATT_KV_HEADS, 1, LANES), F32)],
        compiler_params=_compiler_params(("arbitrary", "arbitrary")),
        name="dsa_core",
    )(q, qit, wit, kt, v, ki, bias_near, bias_max)


def _gdn_in_weight(w_in):
    D = w_in.shape[0]
    w = jnp.zeros((D, GDN_PROJ_W), F32).at[:, 0:w_in.shape[1]].set(w_in.astype(F32))
    return w.astype(BF16)


def _dsa_in_weight(w_in):
    D = w_in.shape[0]
    w_in = w_in.astype(BF16)
    zeros = lambda n: jnp.zeros((D, n), BF16)
    src = ATT_Q_DIM + 2 * ATT_KV_DIM
    parts = [w_in[:, 0:src]]
    for h in range(IDX_HEADS):
        parts += [w_in[:, src + h * IDX_HD:src + (h + 1) * IDX_HD], zeros(LANES - IDX_HD)]
    src += IDX_HEADS * IDX_HD
    parts += [w_in[:, src:src + IDX_HD], zeros(LANES - IDX_HD),
              w_in[:, src + IDX_HD:src + IDX_HD + IDX_HEADS], zeros(LANES - IDX_HEADS)]
    w = jnp.concatenate(parts, axis=1)
    assert w.shape[1] == DSA_PROJ_W
    return w


def kernel(x, meta_tokens, norm_mix, norm_ffn, rel_bias, gdn_w_in, gdn_conv, gdn_a_log, gdn_dt_bias,
           gdn_o_norm, gdn_w_out, dsa_w_in, dsa_q_norm, dsa_k_norm, dsa_idx_ln_g, dsa_idx_ln_b, dsa_w_out,
           moe_w_group, moe_b_group, moe_w_expert, moe_b_expert, moe_w_gate_up, moe_w_down):
    B, S, D = x.shape
    depth = norm_mix.shape[0]
    n_select = min(TOPK_MAX, S // 4)
    off = FRONT_PAD + N_META
    LP = -(-(off + S) // ROW_ALIGN) * ROW_ALIGN
    assert LP == off + S, "frames must fill whole attention tiles"
    meta = jnp.broadcast_to(meta_tokens.astype(x.dtype)[None], (B, N_META, D))
    head = jnp.concatenate([jnp.zeros((B, FRONT_PAD, D), x.dtype), meta], axis=1)
    wgu_all = moe_w_gate_up.astype(BF16).reshape(depth * N_EXPERTS, D, 2 * EXPERT_FF)
    wd_all = moe_w_down.astype(BF16).reshape(depth * N_EXPERTS, EXPERT_FF, D)
    h = None
    for i in range(depth):
        j = i // 2
        if i % 2 == 0:
            gdn_args = (norm_mix[i], _gdn_in_weight(gdn_w_in[j]), gdn_conv[j].astype(F32), gdn_a_log[j],
                        gdn_dt_bias[j], gdn_o_norm[j])
            if i == 0:
                o, h = _gdn_core(x, *gdn_args, head=head)
                h = h.reshape(B * LP, D)
            else:
                o = _gdn_core(h.reshape(B, LP, D), *gdn_args)
            h = _matmul_residual(o.reshape(B * LP, GDN_V_DIM), gdn_w_out[j].astype(BF16), h)
        else:
            q, kt, v, qit, ki, wit = _dsa_in(h, norm_mix[i], _dsa_in_weight(dsa_w_in[j]), dsa_q_norm[j],
                                             dsa_k_norm[j], dsa_idx_ln_g[j], dsa_idx_ln_b[j])
            r3 = lambda t: t.reshape(B, LP, t.shape[-1])
            r4 = lambda t: t.reshape(B, LP // KEY_TILE, t.shape[-2], KEY_TILE)
            o = _dsa_core(r3(q), r4(kt), r3(v), r4(qit), r3(ki), r4(wit), rel_bias, n_select, off + S)
            h = _matmul_residual(o.reshape(B * LP, ATT_Q_DIM), dsa_w_out[j].astype(BF16), h)
        h = _moe(h, norm_ffn[i], moe_w_group[i], moe_b_group[i], moe_w_expert[i], moe_b_expert[i],
                 wgu_all, wd_all, i, frames=(LP, S, off) if i == depth - 1 else None)
    return h.reshape(B, S, D)
```

```python
import functools
import math

import jax
import jax.numpy as jnp
from jax import lax
from jax.experimental import pallas as pl
from jax.experimental.pallas import tpu as pltpu

F32 = jnp.float32
BF16 = jnp.bfloat16

LANES = 128
SUBLANES = 8
VMEM_LIMIT_BYTES = 56 * 1024 * 1024

CHUNK = 64
N_META = 16
META_PAD = (-N_META) % CHUNK
RMS_EPS = 1e-6
L2_EPS = 1e-6

GDN_HEADS = 8
GDN_DK = 128
GDN_DV = 128
GDN_CONV = 4
GDN_QK_DIM = GDN_HEADS * GDN_DK
GDN_V_DIM = GDN_HEADS * GDN_DV
GDN_CONV_DIM = 2 * GDN_QK_DIM + GDN_V_DIM
GDN_GATE_OFF = GDN_CONV_DIM + GDN_V_DIM
GDN_PROJ_W = GDN_GATE_OFF + LANES

ATT_HEADS = 8
ATT_KV_HEADS = 2
ATT_GROUP = ATT_HEADS // ATT_KV_HEADS
ATT_HD = 128
ATT_Q_DIM = ATT_HEADS * ATT_HD
ATT_KV_DIM = ATT_KV_HEADS * ATT_HD
V_WIDE = ATT_HD + LANES
IDX_HEADS = 8
IDX_HD = 64
TOPK_MAX = 256
REL_BUCKETS = 32
REL_MAX_DIST = 128

DSA_Q_OFF = 0
DSA_K_OFF = ATT_Q_DIM
DSA_V_OFF = DSA_K_OFF + ATT_KV_DIM
DSA_QI_OFF = DSA_V_OFF + ATT_KV_DIM
DSA_KI_OFF = DSA_QI_OFF + IDX_HEADS * LANES
DSA_WI_OFF = DSA_KI_OFF + LANES
DSA_PROJ_W = DSA_WI_OFF + LANES

N_GROUPS = 4
EXPERTS_PER_GROUP = 8
N_EXPERTS = N_GROUPS * EXPERTS_PER_GROUP
EXPERT_FF = 256

GDN_ROW_TILES = (256, 128, 64)
MOE_ROUTER_TILES = (512, 256, 128)
MOE_ROW_TILES = (512, 256, 128)

ATT_TILE = 256
KEY_TILE = 256
ROW_ALIGN = KEY_TILE
INDEXER_GROUP = 4
ATTEND_GROUP = 4
FRONT_PAD = ROW_ALIGN - N_META
assert FRONT_PAD % CHUNK == META_PAD
BISECT_UNROLL = 4
BISECT_STEPS = 20
SHIFT_SLACK = 1.001
ROW_SUM_FLOOR = 1e-30
NEG_INF = float("-inf")


def _pick_tile(n, candidates):
    for c in candidates:
        if n % c == 0:
            return c
    raise ValueError(f"no tile for {n}")


def _compiler_params(semantics):
    return pltpu.CompilerParams(dimension_semantics=semantics, vmem_limit_bytes=VMEM_LIMIT_BYTES)


def _silu(x):
    return x * jax.nn.sigmoid(x)


def _norm_matmul_kernel(x_ref, g_ref, w_ref, o_ref):
    x = x_ref[...]
    y = x * lax.rsqrt(jnp.mean(x * x, axis=-1, keepdims=True) + RMS_EPS) * g_ref[...]
    o_ref[...] = jnp.dot(y.astype(BF16), w_ref[...], preferred_element_type=F32)


def _matmul_residual_kernel(a_ref, w_ref, r_ref, o_ref):
    o_ref[...] = r_ref[...] + jnp.dot(a_ref[...], w_ref[...], preferred_element_type=F32)


def _matmul_residual(a_bf16, w_bf16, res):
    T, K = a_bf16.shape
    N = w_bf16.shape[1]
    tm = _pick_tile(T, (512, 256, 128))
    return pl.pallas_call(
        _matmul_residual_kernel,
        grid=(T // tm,),
        in_specs=[pl.BlockSpec((tm, K), lambda i: (i, 0)),
                  pl.BlockSpec((K, N), lambda i: (0, 0)),
                  pl.BlockSpec((tm, N), lambda i: (i, 0))],
        out_specs=pl.BlockSpec((tm, N), lambda i: (i, 0)),
        out_shape=jax.ShapeDtypeStruct((T, N), F32),
        compiler_params=_compiler_params(("parallel",)),
        name="matmul_residual",
    )(a_bf16, w_bf16, res)


def _nt_dot(a, b):
    return lax.dot_general(a, b, (((1,), (1,)), ((), ())), preferred_element_type=F32)


def _tn_dot(a, b):
    return lax.dot_general(a, b, (((0,), (0,)), ((), ())), preferred_element_type=F32)


def _gdn_kernel(x_ref, g_ref, w_ref, conv_ref, gate_ref, ogain_ref, o_ref,
                state_ref, xbuf_ref, qkv_ref, proj_ref):
    t = pl.program_id(0)
    nb, rows = x_ref.shape[0], x_ref.shape[1]

    @pl.when(t == 0)
    def _():
        state_ref[...] = jnp.zeros_like(state_ref)
        xbuf_ref[:, 0:SUBLANES, :] = jnp.zeros((nb, SUBLANES, GDN_CONV_DIM), F32)

    for b in range(nb):
        _norm_matmul_kernel(x_ref.at[b], g_ref, w_ref, proj_ref.at[b])

    def chunk(cc, carry):
        r0 = pl.multiple_of(cc * CHUNK, CHUNK)
        _gdn_chunk(t * rows + r0, r0, nb, proj_ref, conv_ref, gate_ref, ogain_ref, o_ref,
                   state_ref, xbuf_ref, qkv_ref)
        return carry

    lax.fori_loop(0, rows // CHUNK, chunk, 0)


def _gdn_chunk(row0, r0, nb, proj_ref, conv_ref, gate_ref, ogain_ref, o_ref, state_ref, xbuf_ref, qkv_ref):
    C = CHUNK
    row = row0 + lax.broadcasted_iota(jnp.int32, (C, 1), 0)
    live = row >= FRONT_PAD
    r_i = lax.broadcasted_iota(jnp.int32, (C, C), 0)
    c_i = lax.broadcasted_iota(jnp.int32, (C, C), 1)
    causal = c_i <= r_i
    strict = c_i < r_i
    neg_rate = -jnp.exp(gate_ref[0:1, :])
    betas, gcs, gc_ts, egs, eg_lasts, eks = [], [], [], [], [], []
    for b in range(nb):
        u = jnp.where(live, proj_ref[b, pl.ds(r0, C), 0:GDN_CONV_DIM], 0.0)
        xbuf_ref[b, SUBLANES:SUBLANES + C, :] = u
        conv = conv_ref[0:1, :] * xbuf_ref[b, SUBLANES - 3:SUBLANES - 3 + C, :]
        for j in range(1, GDN_CONV):
            conv = conv + conv_ref[j:j + 1, :] * xbuf_ref[b, SUBLANES - 3 + j:SUBLANES - 3 + j + C, :]
        xbuf_ref[b, 0:SUBLANES, :] = u[C - SUBLANES:C, :]
        qkv_ref[b] = _silu(conv)

        gates = jnp.where(live, proj_ref[b, pl.ds(r0, C), GDN_GATE_OFF:GDN_GATE_OFF + LANES], 0.0)
        g = neg_rate * jax.nn.softplus(gates + gate_ref[1:2, :])
        gc = jnp.dot(causal.astype(F32), g, preferred_element_type=F32,
                     precision=lax.Precision.HIGHEST)
        g_last = gc[C - 1:C, :]
        betas.append(jax.nn.sigmoid(gates))
        gcs.append(gc)
        gc_ts.append(gc.T)
        egs.append(jnp.exp(gc))
        eg_lasts.append(jnp.exp(g_last))
        eks.append(jnp.exp(g_last - gc))

    chains = [(b, h) for b in range(nb) for h in range(GDN_HEADS)]
    heads = range(len(chains))
    dot = functools.partial(jnp.dot, preferred_element_type=F32)
    col = lambda a, h: a[:, GDN_HEADS + h:GDN_HEADS + h + 1]
    qs, ks, k16s, xs, decays = [], [], [], [], []
    for b, h in chains:
        lo = h * GDN_DK
        q = qkv_ref[b, :, lo:lo + GDN_DK]
        k = qkv_ref[b, :, GDN_QK_DIM + lo:GDN_QK_DIM + lo + GDN_DK]
        v = qkv_ref[b, :, 2 * GDN_QK_DIM + h * GDN_DV:2 * GDN_QK_DIM + (h + 1) * GDN_DV]
        q = q * lax.rsqrt(jnp.sum(q * q, axis=-1, keepdims=True) + L2_EPS) * (GDN_DK ** -0.5)
        k = k * lax.rsqrt(jnp.sum(k * k, axis=-1, keepdims=True) + L2_EPS)
        kb = k * betas[b][:, h:h + 1]
        vb = v * betas[b][:, h:h + 1]
        qs.append(q)
        ks.append(k)
        k16s.append(k.astype(BF16))
        xs.append((kb, jnp.concatenate([vb, kb * col(egs[b], h)], axis=1)))
        decays.append(jnp.exp(jnp.where(causal, col(gcs[b], h) - gc_ts[b][GDN_HEADS + h:GDN_HEADS + h + 1, :],
                                        NEG_INF)))
    kk = [_nt_dot(xs[h][0].astype(BF16), k16s[h]) for h in heads]
    qk = [_nt_dot(qs[h].astype(BF16), k16s[h]) for h in heads]
    ns = [-jnp.where(strict, kk[h] * decays[h], 0.0) for h in heads]
    xs = [x for _, x in xs]
    for it in range(6):
        n16 = [n.astype(BF16) for n in ns]
        xs = [xs[h] + dot(n16[h], xs[h].astype(BF16)) for h in heads]
        if it < 5:
            ns = [dot(n16[h], n16[h]) for h in heads]
    qk = [jnp.where(causal, qk[h] * decays[h], 0.0).astype(BF16) for h in heads]
    states = [state_ref[c] for c in heads]
    s16 = [s.astype(BF16) for s in states]
    ws = [dot(xs[c][:, GDN_DV:2 * GDN_DV].astype(BF16), s16[c]) for c in heads]
    qs_s = [dot((qs[c] * col(egs[b], h)).astype(BF16), s16[c]) for c, (b, h) in enumerate(chains)]
    v_new = [(xs[c][:, 0:GDN_DV] - ws[c]).astype(BF16) for c in heads]
    os_ = [qs_s[c] + dot(qk[c], v_new[c]) for c in heads]
    kv = [_tn_dot((ks[c] * col(eks[b], h)).astype(BF16), v_new[c]) for c, (b, h) in enumerate(chains)]
    for c, (b, h) in enumerate(chains):
        state_ref[c] = states[c] * col(eg_lasts[b], h) + kv[c]
        z = proj_ref[b, pl.ds(r0, C), GDN_CONV_DIM + h * GDN_DV:GDN_CONV_DIM + (h + 1) * GDN_DV]
        o = os_[c]
        on = o * lax.rsqrt(jnp.mean(o * o, axis=-1, keepdims=True) + RMS_EPS) * ogain_ref[...]
        o_ref[b, pl.ds(r0, C), h * GDN_DV:(h + 1) * GDN_DV] = (on * _silu(z)).astype(o_ref.dtype)


def _gdn_first_kernel(x_ref, head_ref, g_ref, w_ref, conv_ref, gate_ref, ogain_ref, o_ref, h_ref, *scratch):
    t = pl.program_id(0)

    @pl.when(t == 0)
    def _():
        h_ref[...] = head_ref[...]

    @pl.when(t > 0)
    def _():
        h_ref[...] = x_ref[...]

    _gdn_kernel(h_ref, g_ref, w_ref, conv_ref, gate_ref, ogain_ref, o_ref, *scratch)


def _gdn_core(src, gain, w_bf16, conv_w, a_log, dt_bias, o_gain, head=None):
    W = w_bf16.shape[1]
    gate = jnp.zeros((2, LANES), F32)
    gate = gate.at[0, GDN_HEADS:2 * GDN_HEADS].set(a_log.astype(F32))
    gate = gate.at[1, GDN_HEADS:2 * GDN_HEADS].set(dt_bias.astype(F32))
    fix = lambda t: (0, 0)
    if head is not None:
        B, S, D = src.shape
        rows = head.shape[1]
        LP = rows + S
        assert S % rows == 0
        scratch = [pltpu.VMEM((B * GDN_HEADS, GDN_DK, GDN_DV), F32),
                   pltpu.VMEM((B, SUBLANES + CHUNK, GDN_CONV_DIM), F32),
                   pltpu.VMEM((B, CHUNK, GDN_CONV_DIM), F32),
                   pltpu.VMEM((B, rows, W), F32)]
        return pl.pallas_call(
            _gdn_first_kernel,
            grid=(LP // rows,),
            in_specs=[pl.BlockSpec((B, rows, D), lambda t: (0, jnp.maximum(t - 1, 0), 0)),
                      pl.BlockSpec((B, rows, D), lambda t: (0, 0, 0), pipeline_mode=pl.Buffered(1)),
                      pl.BlockSpec((1, D), fix),
                      pl.BlockSpec((D, W), fix, pipeline_mode=pl.Buffered(1)),
                      pl.BlockSpec((GDN_CONV, GDN_CONV_DIM), fix),
                      pl.BlockSpec((2, LANES), fix),
                      pl.BlockSpec((1, GDN_DV), fix)],
            out_specs=[pl.BlockSpec((B, rows, GDN_V_DIM), lambda t: (0, t, 0)),
                       pl.BlockSpec((B, rows, D), lambda t: (0, t, 0))],
            out_shape=[jax.ShapeDtypeStruct((B, LP, GDN_V_DIM), BF16),
                       jax.ShapeDtypeStruct((B, LP, D), F32)],
            scratch_shapes=scratch,
            compiler_params=_compiler_params(("arbitrary",)),
            name="gdn_core",
        )(src, head, gain.reshape(1, D), w_bf16, conv_w, gate, o_gain.reshape(1, GDN_DV))
    h3d = src
    B, LP, D = h3d.shape
    rows = _pick_tile(LP, GDN_ROW_TILES)
    return pl.pallas_call(
        _gdn_kernel,
        grid=(LP // rows,),
        in_specs=[pl.BlockSpec((B, rows, D), lambda t: (0, t, 0)),
                  pl.BlockSpec((1, D), fix),
                  pl.BlockSpec((D, W), fix, pipeline_mode=pl.Buffered(1)),
                  pl.BlockSpec((GDN_CONV, GDN_CONV_DIM), fix),
                  pl.BlockSpec((2, LANES), fix),
                  pl.BlockSpec((1, GDN_DV), fix)],
        out_specs=pl.BlockSpec((B, rows, GDN_V_DIM), lambda t: (0, t, 0)),
        out_shape=jax.ShapeDtypeStruct((B, LP, GDN_V_DIM), BF16),
        scratch_shapes=[pltpu.VMEM((B * GDN_HEADS, GDN_DK, GDN_DV), F32),
                        pltpu.VMEM((B, SUBLANES + CHUNK, GDN_CONV_DIM), F32),
                        pltpu.VMEM((B, CHUNK, GDN_CONV_DIM), F32),
                        pltpu.VMEM((B, rows, W), F32)],
        compiler_params=_compiler_params(("arbitrary",)),
        name="gdn_core",
    )(h3d, gain.reshape(1, D), w_bf16, conv_w, gate, o_gain.reshape(1, GDN_DV))


def _moe_router_kernel(h_ref, gain_ref, wr_ref, br_ref, hx_ref, gid_ref, rank_ref, cnt_ref, carry_ref):
    i = pl.program_id(0)
    TM = h_ref.shape[0]
    lane = lax.broadcasted_iota(jnp.int32, (1, LANES), 1)

    @pl.when(i == 0)
    def _():
        carry_ref[...] = jnp.zeros_like(carry_ref)

    x = h_ref[...]
    xn = x * lax.rsqrt(jnp.mean(x * x, axis=-1, keepdims=True) + RMS_EPS) * gain_ref[...]
    logits = jnp.dot(xn, wr_ref[...], preferred_element_type=F32,
                     precision=lax.Precision.HIGHEST) + br_ref[...]
    e_log = logits
    g_lane = lane - N_EXPERTS
    g_log = jnp.where((g_lane >= 0) & (g_lane < N_GROUPS), logits, NEG_INF)
    g_max = jnp.max(g_log, axis=-1, keepdims=True)
    g_sel = jnp.min(jnp.where(g_log == g_max, g_lane, LANES), axis=-1, keepdims=True)
    g_sel = jnp.minimum(g_sel, N_GROUPS - 1)
    g_w = 1.0 / jnp.sum(jnp.exp(g_log - g_max), axis=-1, keepdims=True)
    in_grp = (lane >= g_sel * EXPERTS_PER_GROUP) & (lane < (g_sel + 1) * EXPERTS_PER_GROUP)
    l0 = jnp.where(in_grp, e_log, NEG_INF)
    m1 = jnp.max(l0, axis=-1, keepdims=True)
    i1 = jnp.min(jnp.where(l0 == m1, lane, LANES), axis=-1, keepdims=True)
    l1 = jnp.where(lane == i1, NEG_INF, l0)
    m2 = jnp.max(l1, axis=-1, keepdims=True)
    i2 = jnp.min(jnp.where(l1 == m2, lane, LANES), axis=-1, keepdims=True)
    p2 = jnp.exp(m2 - m1)
    w1 = g_w / (1.0 + p2)
    w2 = g_w * p2 / (1.0 + p2)
    j1 = i1 - g_sel * EXPERTS_PER_GROUP
    j2 = i2 - g_sel * EXPERTS_PER_GROUP
    hx_ref[:, 0:xn.shape[1]] = xn
    hx_ref[:, xn.shape[1]:] = jnp.where(lane == j1, w1, 0.0) + jnp.where(lane == j2, w2, 0.0)

    onehot = jnp.where(lane == g_sel, 1.0, 0.0)
    r_i = lax.broadcasted_iota(jnp.int32, (TM, TM), 0)
    c_i = lax.broadcasted_iota(jnp.int32, (TM, TM), 1)
    before = jnp.where(c_i < r_i, 1.0, 0.0).astype(BF16)
    prior = jnp.dot(before, onehot.astype(BF16), preferred_element_type=F32) + carry_ref[...]
    rank = jnp.sum(jnp.where(lane == g_sel, prior, 0.0), axis=-1, keepdims=True)
    carry_ref[...] += jnp.sum(onehot, axis=0, keepdims=True)
    cnt_ref[...] = carry_ref[...]
    gid_ref[0] = jnp.broadcast_to(g_sel.astype(F32), (TM, LANES)).T[0:1, :].astype(jnp.int32)
    rank_ref[0] = jnp.broadcast_to(rank, (TM, LANES)).T[0:1, :].astype(jnp.int32)


def _moe_dispatch_kernel(seg_ref, pos_ref, hx_ref, xs_ref, zero_ref, sem, zero_sem):
    TM = hx_ref.shape[0]

    @pl.when(pl.program_id(0) == 0)
    def _():
        zero_ref[...] = jnp.zeros_like(zero_ref)

        def zero_tile(t):
            fill = pltpu.make_async_copy(zero_ref, xs_ref.at[pl.ds(t * TM, TM), :], zero_sem)
            fill.start()
            fill.wait()

        for g in range(N_GROUPS):
            first = seg_ref[g - 1] if g else 0
            pl.when(seg_ref[g] > first)(functools.partial(zero_tile, seg_ref[g] - 1))
            tail = seg_ref[N_GROUPS - 1] + g
            pl.when(tail < xs_ref.shape[0] // TM)(functools.partial(zero_tile, tail))

    def start(a, carry):
        for b in range(SUBLANES):
            r = a * SUBLANES + b
            pltpu.make_async_copy(hx_ref.at[pl.ds(r, 1), :], xs_ref.at[pl.ds(pos_ref[0, 0, r], 1), :],
                                  sem).start()
        return carry

    lax.fori_loop(0, TM // SUBLANES, start, 0)
    pltpu.make_async_copy(hx_ref, xs_ref.at[pl.ds(0, TM), :], sem).wait()


def _moe_expert_kernel(grp_ref, xs_ref, wgu_ref, wd_ref, ys_ref, xn_ref, acc_ref):
    j = pl.program_id(0)
    e = pl.program_id(1)
    D = xn_ref.shape[1]
    lane = lax.broadcasted_iota(jnp.int32, (1, LANES), 1)

    @pl.when(grp_ref[j] >= 0)
    def _():
        @pl.when(e == 0)
        def _():
            xn_ref[...] = xs_ref[:, 0:D].astype(BF16)
            acc_ref[...] = jnp.zeros_like(acc_ref)

        hh = jnp.dot(xn_ref[...], wgu_ref[0], preferred_element_type=F32)
        cw = jnp.sum(jnp.where(lane == e, xs_ref[:, D:], 0.0), axis=-1, keepdims=True)
        act = _silu(hh[:, 0:EXPERT_FF]) * hh[:, EXPERT_FF:2 * EXPERT_FF] * cw
        acc_ref[...] += jnp.dot(act.astype(BF16), wd_ref[0], preferred_element_type=F32)

    last = e == EXPERTS_PER_GROUP - 1

    @pl.when(last & (grp_ref[j] >= 0))
    def _():
        ys_ref[...] = acc_ref[...]

    @pl.when(last & (grp_ref[j] < 0))
    def _():
        ys_ref[...] = jnp.zeros_like(ys_ref)


def _moe_combine_kernel(pos_ref, h_ref, ys_ref, o_ref, stage_ref, sem):
    TM = h_ref.shape[0]

    def start(a, carry):
        for b in range(SUBLANES):
            r = a * SUBLANES + b
            pltpu.make_async_copy(ys_ref.at[pl.ds(pos_ref[0, 0, r], 1), :], stage_ref.at[pl.ds(r, 1), :],
                                  sem).start()
        return carry

    lax.fori_loop(0, TM // SUBLANES, start, 0)
    pltpu.make_async_copy(ys_ref.at[pl.ds(0, TM), :], stage_ref, sem).wait()
    o_ref[...] = h_ref[...] + stage_ref[...]


def _moe(h2d, gain, w_group, b_group, w_expert, b_expert, wgu, wd, layer, frames=None):
    T, D = h2d.shape
    tr = _pick_tile(T, MOE_ROUTER_TILES)
    tm = _pick_tile(T, MOE_ROW_TILES)
    n_tiles = T // tm + N_GROUPS
    XW = D + LANES
    assert N_EXPERTS + N_GROUPS <= LANES
    wr = jnp.zeros((D, LANES), F32)
    wr = wr.at[:, 0:N_EXPERTS].set(w_expert.astype(F32))
    wr = wr.at[:, N_EXPERTS:N_EXPERTS + N_GROUPS].set(w_group.astype(F32))
    br = jnp.zeros((1, LANES), F32)
    br = br.at[0, 0:N_EXPERTS].set(b_expert.astype(F32))
    br = br.at[0, N_EXPERTS:N_EXPERTS + N_GROUPS].set(b_group.astype(F32))

    hx, gid, rank, counts = pl.pallas_call(
        _moe_router_kernel,
        grid=(T // tr,),
        in_specs=[pl.BlockSpec((tr, D), lambda i: (i, 0)),
                  pl.BlockSpec((1, D), lambda i: (0, 0)),
                  pl.BlockSpec((D, LANES), lambda i: (0, 0)),
                  pl.BlockSpec((1, LANES), lambda i: (0, 0))],
        out_specs=[pl.BlockSpec((tr, XW), lambda i: (i, 0)),
                   pl.BlockSpec((1, 1, tr), lambda i: (i, 0, 0)),
                   pl.BlockSpec((1, 1, tr), lambda i: (i, 0, 0)),
                   pl.BlockSpec((1, LANES), lambda i: (0, 0))],
        out_shape=[jax.ShapeDtypeStruct((T, XW), F32),
                   jax.ShapeDtypeStruct((T // tr, 1, tr), jnp.int32),
                   jax.ShapeDtypeStruct((T // tr, 1, tr), jnp.int32),
                   jax.ShapeDtypeStruct((1, LANES), F32)],
        scratch_shapes=[pltpu.VMEM((1, LANES), F32)],
        compiler_params=_compiler_params(("arbitrary",)),
        name="moe_router",
    )(h2d, gain.reshape(1, D), wr, br)

    seg_tiles = -(-counts[0, 0:N_GROUPS].astype(jnp.int32) // tm)
    seg_end = jnp.cumsum(seg_tiles)
    seg_start = seg_end - seg_tiles
    pos = (seg_start * tm)[gid.reshape(T)] + rank.reshape(T)
    tile_ids = jnp.arange(n_tiles, dtype=jnp.int32)
    tile_group = jnp.sum((tile_ids[:, None] >= seg_end[None, :]).astype(jnp.int32), axis=1)
    tile_group = jnp.where(tile_ids < seg_end[N_GROUPS - 1], tile_group, -1)
    pos3 = pos.reshape(T // tm, 1, tm)

    xs = pl.pallas_call(
        _moe_dispatch_kernel,
        grid_spec=pltpu.PrefetchScalarGridSpec(
            num_scalar_prefetch=1,
            grid=(T // tm,),
            in_specs=[pl.BlockSpec((1, 1, tm), lambda i, seg: (i, 0, 0), memory_space=pltpu.SMEM),
                      pl.BlockSpec((tm, XW), lambda i, seg: (i, 0))],
            out_specs=pl.BlockSpec(memory_space=pl.ANY),
            scratch_shapes=[pltpu.VMEM((tm, XW), F32), pltpu.SemaphoreType.DMA(()),
                            pltpu.SemaphoreType.DMA(())]),
        out_shape=jax.ShapeDtypeStruct((n_tiles * tm, XW), F32),
        compiler_params=_compiler_params(("arbitrary",)),
        name="moe_dispatch",
    )(seg_end, pos3, hx)

    def expert_index(j, e, grp):
        return (layer * N_EXPERTS + jnp.maximum(grp[j], 0) * EXPERTS_PER_GROUP + e, 0, 0)

    ys = pl.pallas_call(
        _moe_expert_kernel,
        grid_spec=pltpu.PrefetchScalarGridSpec(
            num_scalar_prefetch=1,
            grid=(n_tiles, EXPERTS_PER_GROUP),
            in_specs=[pl.BlockSpec((tm, XW), lambda j, e, grp: (j, 0)),
                      pl.BlockSpec((1, D, 2 * EXPERT_FF), expert_index),
                      pl.BlockSpec((1, EXPERT_FF, D), expert_index)],
            out_specs=pl.BlockSpec((tm, D), lambda j, e, grp: (j, 0)),
            scratch_shapes=[pltpu.VMEM((tm, D), BF16), pltpu.VMEM((tm, D), F32)]),
        out_shape=jax.ShapeDtypeStruct((n_tiles * tm, D), F32),
        compiler_params=_compiler_params(("parallel", "arbitrary")),
        name="moe_experts",
    )(tile_group, xs, wgu, wd)

    if frames is None:
        tc, out_rows, out_index = tm, T, lambda i: (i, 0)
    else:
        batch_rows, n_frames, first = frames
        tc = ROW_ALIGN
        assert batch_rows % tc == 0 and n_frames % tc == 0 and first % tc == 0 and first >= tc
        out_rows = (T // batch_rows) * n_frames

        def out_index(i):
            b, t = i // (batch_rows // tc), i % (batch_rows // tc)
            return (b * (n_frames // tc) + jnp.maximum(t - first // tc, 0), 0)

    return pl.pallas_call(
        _moe_combine_kernel,
        grid=(T // tc,),
        in_specs=[pl.BlockSpec((1, 1, tc), lambda i: (i, 0, 0), memory_space=pltpu.SMEM),
                  pl.BlockSpec((tc, D), lambda i: (i, 0)),
                  pl.BlockSpec(memory_space=pl.ANY)],
        out_specs=pl.BlockSpec((tc, D), out_index),
        out_shape=jax.ShapeDtypeStruct((out_rows, D), F32),
        scratch_shapes=[pltpu.VMEM((tc, D), F32), pltpu.SemaphoreType.DMA(())],
        compiler_params=_compiler_params(("arbitrary",)),
        name="moe_combine",
    )(pos.reshape(T // tc, 1, tc), h2d, ys)


def _dsa_prep_kernel(p_ref, qg_ref, kg_ref, lng_ref, lnb_ref, q_ref, kt_ref, v_ref, qit_ref, ki_ref, wit_ref):
    lane = lax.broadcasted_iota(jnp.int32, (1, LANES), 1)
    for h in range(ATT_HEADS):
        x = p_ref[:, DSA_Q_OFF + h * ATT_HD:DSA_Q_OFF + (h + 1) * ATT_HD]
        y = x * lax.rsqrt(jnp.mean(x * x, axis=-1, keepdims=True) + RMS_EPS) * qg_ref[...]
        q_ref[:, h * ATT_HD:(h + 1) * ATT_HD] = (y * (ATT_HD ** -0.5)).astype(BF16)
    for n in range(ATT_KV_HEADS):
        x = p_ref[:, DSA_K_OFF + n * ATT_HD:DSA_K_OFF + (n + 1) * ATT_HD]
        y = x * lax.rsqrt(jnp.mean(x * x, axis=-1, keepdims=True) + RMS_EPS) * kg_ref[...]
        kt_ref[0, n * ATT_HD:(n + 1) * ATT_HD, :] = y.T.astype(BF16)
    ones_col = jnp.where(lane == 0, 1.0, 0.0).astype(BF16)
    for n in range(ATT_KV_HEADS):
        v_ref[:, n * V_WIDE:n * V_WIDE + ATT_HD] = p_ref[:, DSA_V_OFF + n * ATT_HD:DSA_V_OFF + (n + 1) * ATT_HD].astype(BF16)
        v_ref[:, n * V_WIDE + ATT_HD:(n + 1) * V_WIDE] = jnp.broadcast_to(ones_col, (v_ref.shape[0], LANES))
    qit_ref[0] = p_ref[:, DSA_QI_OFF:DSA_QI_OFF + IDX_HEADS * LANES].T.astype(BF16)
    x = p_ref[:, DSA_KI_OFF:DSA_KI_OFF + LANES]
    live = lane < IDX_HD
    mu = jnp.sum(x, axis=-1, keepdims=True) * (1.0 / IDX_HD)
    xc = jnp.where(live, x - mu, 0.0)
    var = jnp.sum(xc * xc, axis=-1, keepdims=True) * (1.0 / IDX_HD)
    ki = xc * lax.rsqrt(var + RMS_EPS) * lng_ref[...] + lnb_ref[...]
    ki_ref[...] = jnp.where(live, ki, 0.0).astype(BF16)
    wit_ref[0] = (p_ref[:, DSA_WI_OFF:DSA_WI_OFF + LANES] * (IDX_HEADS ** -0.5 * IDX_HD ** -0.5)).T


def _dsa_in_kernel(x_ref, g_ref, w_ref, qg_ref, kg_ref, lng_ref, lnb_ref,
                   q_ref, kt_ref, v_ref, qit_ref, ki_ref, wit_ref, proj_ref):
    _norm_matmul_kernel(x_ref, g_ref, w_ref, proj_ref)
    _dsa_prep_kernel(proj_ref, qg_ref, kg_ref, lng_ref, lnb_ref, q_ref, kt_ref, v_ref, qit_ref, ki_ref, wit_ref)


def _dsa_in(h2d, gain, w_bf16, q_gain, k_gain, ln_g, ln_b):
    T, D = h2d.shape
    W = w_bf16.shape[1]
    tm = KEY_TILE
    assert T % tm == 0 and ATT_TILE == KEY_TILE
    lng = jnp.zeros((1, LANES), F32).at[0, 0:IDX_HD].set(ln_g.astype(F32))
    lnb = jnp.zeros((1, LANES), F32).at[0, 0:IDX_HD].set(ln_b.astype(F32))
    row = lambda i: (i, 0)
    fix = lambda i: (0, 0)
    blk = lambda i: (i, 0, 0)
    return pl.pallas_call(
        _dsa_in_kernel,
        grid=(T // tm,),
        in_specs=[pl.BlockSpec((tm, D), row), pl.BlockSpec((1, D), fix), pl.BlockSpec((D, W), fix),
                  pl.BlockSpec((1, ATT_HD), fix), pl.BlockSpec((1, ATT_HD), fix),
                  pl.BlockSpec((1, LANES), fix), pl.BlockSpec((1, LANES), fix)],
        out_specs=[pl.BlockSpec((tm, ATT_Q_DIM), row), pl.BlockSpec((1, ATT_KV_DIM, tm), blk),
                   pl.BlockSpec((tm, ATT_KV_HEADS * V_WIDE), row), pl.BlockSpec((1, IDX_HEADS * LANES, tm), blk),
                   pl.BlockSpec((tm, LANES), row), pl.BlockSpec((1, LANES, tm), blk)],
        out_shape=[jax.ShapeDtypeStruct((T, ATT_Q_DIM), BF16), jax.ShapeDtypeStruct((T // tm, ATT_KV_DIM, tm), BF16),
                   jax.ShapeDtypeStruct((T, ATT_KV_HEADS * V_WIDE), BF16),
                   jax.ShapeDtypeStruct((T // tm, IDX_HEADS * LANES, tm), BF16),
                   jax.ShapeDtypeStruct((T, LANES), BF16), jax.ShapeDtypeStruct((T // tm, LANES, tm), F32)],
        scratch_shapes=[pltpu.VMEM((tm, W), F32)],
        compiler_params=_compiler_params(("parallel",)),
        name="dsa_in",
    )(h2d, gain.reshape(1, D), w_bf16, q_gain.reshape(1, ATT_HD), k_gain.reshape(1, ATT_HD), lng, lnb)


def _dsa_kernel(q_ref, qit_ref, wit_ref, kt_ref, v_ref, ki_ref, bias_ref, bmax_ref, o_ref,
                sc_ref, mask_ref, qg_ref, m_ref, acc_ref, knorm_ref, *, n_select, live_rows):
    i = pl.program_id(1)
    TQ = ATT_TILE
    TK = KEY_TILE
    HALVES = TK // LANES

    @pl.when(i == 0)
    def _():
        def body(kt, mx):
            kk = kt_ref[0, kt].astype(F32)
            return tuple(jnp.maximum(mx[n], jnp.sum(jnp.square(kk[n * ATT_HD:(n + 1) * ATT_HD, :]),
                                                    axis=0, keepdims=True)) for n in range(ATT_KV_HEADS))

        mx = lax.fori_loop(0, kt_ref.shape[1], body,
                           tuple(jnp.zeros((1, TK), F32) for _ in range(ATT_KV_HEADS)))
        for n in range(ATT_KV_HEADS):
            knorm_ref[n] = jnp.broadcast_to(jnp.sqrt(jnp.max(mx[n], axis=-1, keepdims=True)), (1, LANES))

    @pl.when(i * TQ >= live_rows)
    def _():
        o_ref[...] = jnp.zeros_like(o_ref)

    @pl.when(i * TQ < live_rows)
    def _():
        _dsa_tile(i, q_ref, qit_ref, wit_ref, kt_ref, v_ref, ki_ref, bias_ref, bmax_ref, o_ref,
                  sc_ref, mask_ref, qg_ref, m_ref, acc_ref, knorm_ref, n_select, TQ, TK, HALVES)


def _dsa_tile(i, q_ref, qit_ref, wit_ref, kt_ref, v_ref, ki_ref, bias_ref, bmax_ref, o_ref,
              sc_ref, mask_ref, qg_ref, m_ref, acc_ref, knorm_ref, n_select, TQ, TK, HALVES):
    assert TQ == TK
    nkt = i + 1
    fmin = float(jnp.finfo(F32).min)
    kf = float(n_select)

    chunk_shift = int(math.log2(CHUNK))
    q_pos = i * TQ + lax.broadcasted_iota(jnp.int32, (1, TQ), 1)
    q_chunk = jnp.maximum(lax.shift_right_logical(q_pos, chunk_shift), FRONT_PAD // CHUNK)
    k_row = lax.broadcasted_iota(jnp.int32, (TK, 1), 0)

    def halves(x):
        return [x[:, j * LANES:(j + 1) * LANES] for j in range(HALVES)]

    def fold(x, op):
        parts = [x[r * SUBLANES:(r + 1) * SUBLANES, :] for r in range(TK // SUBLANES)]
        while len(parts) > 1:
            parts = [op(parts[j], parts[j + 1]) for j in range(0, len(parts), 2)]
        return parts[0]

    def score_tiles(kts):
        for kt in kts:
            keys = ki_ref[0, pl.ds(pl.multiple_of(kt * TK, TK), TK), :]
            acc = None
            for h in range(IDX_HEADS):
                d = jnp.dot(keys, qit_ref[0, 0, h * LANES:(h + 1) * LANES, :], preferred_element_type=F32)
                term = wit_ref[0, 0, h:h + 1, :] * jnp.maximum(d, 0.0)
                acc = term if acc is None else acc + term
            sc_ref[kt] = acc
            mask_ref[kt] = acc.T

    def grouped_loop(n, fn, group):
        def body(j, carry):
            fn([group * j + r for r in range(group)])
            return carry

        lax.fori_loop(0, lax.shift_right_logical(n, int(math.log2(group))), body, 0)
        size = group // 2
        while size >= 1:
            first = lax.bitwise_and(n, ~(2 * size - 1))

            @pl.when(lax.bitwise_and(n, size) != 0)
            def _(size=size, first=first):
                fn([first + r for r in range(size)])

            size //= 2

    grouped_loop(nkt, score_tiles, INDEXER_GROUP)

    def mask_inadmissible(kt):
        k_pos = kt * TK + k_row
        adm = (lax.shift_right_logical(k_pos, chunk_shift) <= q_chunk) & (k_pos >= FRONT_PAD)
        sc_ref[kt] = jnp.where(adm, sc_ref[kt], NEG_INF)

    mask_inadmissible(0)

    @pl.when(nkt > 1)
    def _():
        mask_inadmissible(nkt - 1)

    n_adm = CHUNK * (q_chunk + 1) - FRONT_PAD
    need = n_adm > n_select

    def count_where(pred):
        def one(kt):
            return fold(jnp.where(pred(sc_ref[kt], kt), 1.0, 0.0), jnp.add)

        def two(j, acc):
            return acc + (one(2 * j) + one(2 * j + 1))

        acc = lax.fori_loop(0, lax.shift_right_logical(nkt, 1), two, jnp.zeros((SUBLANES, TQ), F32))
        acc = lax.cond(lax.bitwise_and(nkt, 1) == 1, lambda a: a + one(nkt - 1), lambda a: a, acc)
        return jnp.sum(acc, axis=0, keepdims=True)

    def count_ge(t):
        return count_where(lambda s, kt: s >= t)

    def minmax_body(kt, carry):
        mn, mx = carry
        s = sc_ref[kt]
        mn = jnp.minimum(mn, fold(jnp.where(s == NEG_INF, jnp.inf, s), jnp.minimum))
        mx = jnp.maximum(mx, fold(s, jnp.maximum))
        return mn, mx

    mn_l, mx_l = lax.fori_loop(0, nkt, minmax_body,
                               (jnp.full((SUBLANES, TQ), jnp.inf, F32), jnp.full((SUBLANES, TQ), NEG_INF, F32)))
    s_min = jnp.min(mn_l, axis=0, keepdims=True)
    s_max = jnp.max(mx_l, axis=0, keepdims=True)
    lo0 = s_min
    hi0 = s_max + (s_max - s_min) + 1.0
    cnt0 = n_adm.astype(F32)
    active0 = jnp.where(need, 1.0, 0.0)

    def any_active(active):
        return jnp.max(active) > 0.0

    def bis_step(lo, hi, cnt, active):
        mid = lo + (hi - lo) * 0.5
        c = count_ge(mid)
        ge = c >= kf
        moving = (mid > lo) & (mid < hi) & (active > 0.0)
        lo_n = jnp.where(moving & ge, mid, lo)
        hi_n = jnp.where(moving & (~ge), mid, hi)
        cnt_n = jnp.where(moving & ge, c, cnt)
        active_n = jnp.where(moving & (cnt_n != kf), 1.0, 0.0)
        return lo_n, hi_n, cnt_n, active_n

    def bis_body(st):
        it, rest = st[0], st[1:]
        for _ in range(BISECT_UNROLL):
            rest = bis_step(*rest)
        return (it + BISECT_UNROLL,) + rest

    _, lo, hi, cnt, _ = lax.while_loop(
        lambda st: jnp.logical_and(st[0] < BISECT_STEPS, any_active(st[4])), bis_body,
        (jnp.int32(0), lo0, hi0, cnt0, active0))

    def snap_body(st):
        lo, hi, cnt, active = st

        def body(kt, mx):
            s = sc_ref[kt]
            return jnp.maximum(mx, fold(jnp.where(s < hi, s, NEG_INF), jnp.maximum))

        v1 = jnp.max(lax.fori_loop(0, nkt, body, jnp.full((SUBLANES, TQ), NEG_INF, F32)),
                     axis=0, keepdims=True)
        c1 = count_ge(v1)
        on = active > 0.0
        hit = on & (c1 >= kf)
        lo_n = jnp.where(hit, v1, lo)
        cnt_n = jnp.where(hit, c1, cnt)
        hi_n = jnp.where(on & (~hit), v1, hi)
        return lo_n, hi_n, cnt_n, jnp.where(on & (~hit), 1.0, 0.0)

    unsettled = jnp.where(need & (cnt != kf), 1.0, 0.0)
    lo, hi, cnt, _ = lax.while_loop(lambda st: any_active(st[3]), snap_body, (lo, hi, cnt, unsettled))

    thr = jnp.where(need, lo, fmin)
    tied = need & (cnt > kf)
    big_pos = jnp.int32(2 ** 30)

    def tie_break():
        want = kf - count_where(lambda s, kt: s > thr)

        def body(_, st):
            jl, jh = st
            jm = lax.shift_right_arithmetic(jl + jh, 1)
            c = count_where(lambda s, kt: (s == thr) & (kt * TK + k_row <= jm))
            ok = c >= want
            return jnp.where(ok, jl, jm), jnp.where(ok, jm, jh)

        n_keys = sc_ref.shape[0] * TK
        steps = int(math.ceil(math.log2(n_keys + 1))) + 1
        jl0 = jnp.full((1, TQ), -1, jnp.int32)
        jh0 = jnp.full((1, TQ), n_keys, jnp.int32)
        _, jh = lax.fori_loop(0, steps, body, (jl0, jh0))
        return jnp.where(tied, jh, big_pos)

    pos_cap = lax.cond(jnp.max(jnp.where(tied, 1.0, 0.0)) > 0.0, tie_break,
                       lambda: jnp.full((1, TQ), big_pos, jnp.int32))

    shift_w = None
    for n in range(ATT_KV_HEADS):
        for g in range(ATT_GROUP):
            h = n * ATT_GROUP + g
            qg_ref[n, g * TQ:(g + 1) * TQ, :] = q_ref[0, :, h * ATT_HD:(h + 1) * ATT_HD]
            qh = q_ref[0, :, h * ATT_HD:(h + 1) * ATT_HD].astype(F32)
            q_len = jnp.sqrt(jnp.sum(qh * qh, axis=-1, keepdims=True))
            bound = q_len * knorm_ref[n] * SHIFT_SLACK + (bmax_ref[h] + SHIFT_SLACK - 1.0)
            shift_w = bound if shift_w is None else jnp.maximum(shift_w, bound)

    def lane_wide(row):
        return jnp.broadcast_to(row, (LANES, TQ)).T

    thr_w = lane_wide(thr)
    cap_w = lane_wide(pos_cap)
    chunk_w = lane_wide(q_chunk)
    k_lane = lax.broadcasted_iota(jnp.int32, (1, LANES), 1)

    def to_mask(kt, edge_tile):
        out = []
        for j, s in enumerate(halves(mask_ref[kt])):
            k_pos = kt * TK + j * LANES + k_lane
            sel = (s > thr_w) | ((s == thr_w) & (k_pos <= cap_w))
            if edge_tile:
                sel = sel & (lax.shift_right_logical(k_pos, chunk_shift) <= chunk_w) & (k_pos >= FRONT_PAD)
            out.append(jnp.where(sel, -shift_w, NEG_INF))
        mask_ref[kt] = jnp.concatenate(out, axis=1)

    def inner_mask(kt, carry):
        to_mask(kt, False)
        return carry

    lax.fori_loop(1, nkt - 1, inner_mask, 0)
    to_mask(0, True)

    @pl.when(nkt > 1)
    def _():
        to_mask(nkt - 1, True)

    acc_ref[...] = jnp.zeros_like(acc_ref)

    kv_heads = range(ATT_KV_HEADS)

    def raw_logits(kts):
        return [[jnp.dot(qg_ref[n], kt_ref[0, kt, n * ATT_HD:(n + 1) * ATT_HD, :], preferred_element_type=F32)
                 for n in kv_heads] for kt in kts]

    def masked_logits(s, mask, n, g, b0):
        sg = s[g * TQ:(g + 1) * TQ, :] + mask
        if b0 is not None:
            h = n * ATT_GROUP + g
            sg = sg + jnp.concatenate([bias_ref[h, b0 + j] for j in range(HALVES)], axis=1)
        return sg

    def plain_mask(kt):
        return jnp.where(mask_ref[kt] == NEG_INF, NEG_INF, 0.0)

    def max_pass(kts, b0):
        raw = raw_logits(kts)
        masks = [plain_mask(kt) for kt in kts]
        for n in kv_heads:
            for g in range(ATT_GROUP):
                rows = slice(g * TQ, (g + 1) * TQ)
                m = m_ref[n, rows, :]
                for t in range(len(kts)):
                    for part in halves(masked_logits(raw[t][n], masks[t], n, g, b0)):
                        m = jnp.maximum(m, part)
                m_ref[n, rows, :] = m

    def exp_pass(kts, b0, exact_max):
        raw = raw_logits(kts)
        masks = [plain_mask(kt) if exact_max else mask_ref[kt] for kt in kts]
        for n in kv_heads:
            ps = [[] for _ in kts]
            for g in range(ATT_GROUP):
                rows = slice(g * TQ, (g + 1) * TQ)
                m = m_ref[n, rows, :] if exact_max else 0.0
                for t in range(len(kts)):
                    parts = [jnp.exp(part - m) if exact_max else jnp.exp(part)
                             for part in halves(masked_logits(raw[t][n], masks[t], n, g, b0))]
                    ps[t].append(jnp.concatenate(parts, axis=1).astype(BF16))
            pv = None
            for t, kt in enumerate(kts):
                vv = v_ref[0, pl.ds(pl.multiple_of(kt * TK, TK), TK), n * V_WIDE:(n + 1) * V_WIDE]
                d = jnp.dot(jnp.concatenate(ps[t], axis=0), vv, preferred_element_type=F32)
                pv = d if pv is None else pv + d
            acc_ref[n] += pv

    def run_pass(tile_fn, group=2):
        grouped_loop(jnp.maximum(i - 1, 0), lambda kts: tile_fn(kts, None), group)

        @pl.when(i >= 1)
        def _():
            tile_fn([i - 1], 0)

        tile_fn([i], HALVES)

    run_pass(functools.partial(exp_pass, exact_max=False), ATTEND_GROUP)

    def row_sums(n):
        return acc_ref[n, :, ATT_HD:ATT_HD + 1]

    def smallest_row_sum():
        return functools.reduce(jnp.minimum, [jnp.min(row_sums(n)) for n in range(ATT_KV_HEADS)])

    @pl.when(jnp.logical_not(smallest_row_sum() > ROW_SUM_FLOOR))
    def _():
        m_ref[...] = jnp.full(m_ref.shape, -1e30, F32)
        run_pass(max_pass)
        for n in range(ATT_KV_HEADS):
            m_ref[n] = jnp.broadcast_to(jnp.max(m_ref[n], axis=-1, keepdims=True), m_ref.shape[1:])
        acc_ref[...] = jnp.zeros_like(acc_ref)
        run_pass(functools.partial(exp_pass, exact_max=True))

    for n in range(ATT_KV_HEADS):
        inv_l = 1.0 / row_sums(n)
        for g in range(ATT_GROUP):
            h = n * ATT_GROUP + g
            rows = slice(g * TQ, (g + 1) * TQ)
            o_ref[0, :, h * ATT_HD:(h + 1) * ATT_HD] = (acc_ref[n, rows, 0:ATT_HD] * inv_l[rows, :]).astype(o_ref.dtype)


def _t5_bucket(rel):
    nb = REL_BUCKETS // 2
    max_exact = nb // 2
    ret = jnp.where(rel > 0, nb, 0)
    n = jnp.abs(rel)
    nf = jnp.maximum(n, 1).astype(jnp.float32)
    large = max_exact + (jnp.log(nf / max_exact) / math.log(REL_MAX_DIST / max_exact)
                         * (nb - max_exact)).astype(jnp.int32)
    large = jnp.minimum(large, nb - 1)
    return ret + jnp.where(n < max_exact, n, large)


def _dsa_core(q, kt, v, qit, ki, wit, rel_bias, n_select, live_rows):
    B, LP, _ = q.shape
    nq = LP // ATT_TILE
    nkt = LP // KEY_TILE
    a = jnp.arange(ATT_TILE)[:, None]
    xx = jnp.arange(2 * KEY_TILE)[None, :]
    rel = (xx - KEY_TILE) - a
    far = rel_bias.astype(F32)[REL_BUCKETS // 2 - 1]
    onehot = jax.nn.one_hot(_t5_bucket(rel), REL_BUCKETS, dtype=F32)
    table = jnp.einsum("abk,kh->abh", onehot, rel_bias.astype(F32),
                       precision=lax.Precision.HIGHEST) - far
    bias_near = jnp.transpose(table.reshape(ATT_TILE, 2 * KEY_TILE // LANES, LANES, ATT_HEADS), (3, 1, 0, 2))
    bias_max = jnp.broadcast_to(jnp.maximum(jnp.max(table, axis=(0, 1)), 0.0)[:, None, None],
                                (ATT_HEADS, 1, LANES))
    kern = functools.partial(_dsa_kernel, n_select=n_select, live_rows=live_rows)
    tile = lambda b, i: (b, i, 0)
    tile4 = lambda b, i: (b, i, 0, 0)
    whole3 = lambda b, i: (b, 0, 0)
    whole4 = lambda b, i: (b, 0, 0, 0)
    once = pl.Buffered(1)
    return pl.pallas_call(
        kern,
        grid=(B, nq),
        in_specs=[pl.BlockSpec((1, ATT_TILE, ATT_Q_DIM), tile),
                  pl.BlockSpec((1, 1, IDX_HEADS * LANES, ATT_TILE), tile4),
                  pl.BlockSpec((1, 1, LANES, ATT_TILE), tile4),
                  pl.BlockSpec((1, nkt, ATT_KV_DIM, KEY_TILE), whole4, pipeline_mode=once),
                  pl.BlockSpec((1, LP, ATT_KV_HEADS * V_WIDE), whole3, pipeline_mode=once),
                  pl.BlockSpec((1, LP, LANES), whole3, pipeline_mode=once),
                  pl.BlockSpec((ATT_HEADS, 2 * KEY_TILE // LANES, ATT_TILE, LANES), lambda b, i: (0, 0, 0, 0),
                               pipeline_mode=once),
                  pl.BlockSpec((ATT_HEADS, 1, LANES), lambda b, i: (0, 0, 0))],
        out_specs=pl.BlockSpec((1, ATT_TILE, ATT_Q_DIM), tile),
        out_shape=jax.ShapeDtypeStruct((B, LP, ATT_Q_DIM), BF16),
        scratch_shapes=[pltpu.VMEM((nkt, KEY_TILE, ATT_TILE), F32),
                        pltpu.VMEM((nkt, ATT_TILE, KEY_TILE), F32),
                        pltpu.VMEM((ATT_KV_HEADS, ATT_GROUP * ATT_TILE, ATT_HD), BF16),
                        pltpu.VMEM((ATT_KV_HEADS, ATT_GROUP * ATT_TILE, LANES), F32),
                        pltpu.VMEM((ATT_KV_HEADS, ATT_GROUP * ATT_TILE, V_WIDE), F32),
                        pltpu.VMEM((ATT_KV_HEADS, 1, LANES), F32)],
        compiler_params=_compiler_params(("arbitrary", "arbitrary")),
        name="dsa_core",
    )(q, qit, wit, kt, v, ki, bias_near, bias_max)


def _gdn_in_weight(w_in):
    D = w_in.shape[0]
    w = jnp.zeros((D, GDN_PROJ_W), F32).at[:, 0:w_in.shape[1]].set(w_in.astype(F32))
    return w.astype(BF16)


def _dsa_in_weight(w_in):
    D = w_in.shape[0]
    w_in = w_in.astype(BF16)
    zeros = lambda n: jnp.zeros((D, n), BF16)
    src = ATT_Q_DIM + 2 * ATT_KV_DIM
    parts = [w_in[:, 0:src]]
    for h in range(IDX_HEADS):
        parts += [w_in[:, src + h * IDX_HD:src + (h + 1) * IDX_HD], zeros(LANES - IDX_HD)]
    src += IDX_HEADS * IDX_HD
    parts += [w_in[:, src:src + IDX_HD], zeros(LANES - IDX_HD),
              w_in[:, src + IDX_HD:src + IDX_HD + IDX_HEADS], zeros(LANES - IDX_HEADS)]
    w = jnp.concatenate(parts, axis=1)
    assert w.shape[1] == DSA_PROJ_W
    return w


def kernel(x, meta_tokens, norm_mix, norm_ffn, rel_bias, gdn_w_in, gdn_conv, gdn_a_log, gdn_dt_bias,
           gdn_o_norm, gdn_w_out, dsa_w_in, dsa_q_norm, dsa_k_norm, dsa_idx_ln_g, dsa_idx_ln_b, dsa_w_out,
           moe_w_group, moe_b_group, moe_w_expert, moe_b_expert, moe_w_gate_up, moe_w_down):
    B, S, D = x.shape
    depth = norm_mix.shape[0]
    n_select = min(TOPK_MAX, S // 4)
    off = FRONT_PAD + N_META
    LP = -(-(off + S) // ROW_ALIGN) * ROW_ALIGN
    assert LP == off + S, "frames must fill whole attention tiles"
    meta = jnp.broadcast_to(meta_tokens.astype(x.dtype)[None], (B, N_META, D))
    head = jnp.concatenate([jnp.zeros((B, FRONT_PAD, D), x.dtype), meta], axis=1)
    wgu_all = moe_w_gate_up.astype(BF16).reshape(depth * N_EXPERTS, D, 2 * EXPERT_FF)
    wd_all = moe_w_down.astype(BF16).reshape(depth * N_EXPERTS, EXPERT_FF, D)
    h = None
    for i in range(depth):
        j = i // 2
        if i % 2 == 0:
            gdn_args = (norm_mix[i], _gdn_in_weight(gdn_w_in[j]), gdn_conv[j].astype(F32), gdn_a_log[j],
                        gdn_dt_bias[j], gdn_o_norm[j])
            if i == 0:
                o, h = _gdn_core(x, *gdn_args, head=head)
                h = h.reshape(B * LP, D)
            else:
                o = _gdn_core(h.reshape(B, LP, D), *gdn_args)
            h = _matmul_residual(o.reshape(B * LP, GDN_V_DIM), gdn_w_out[j].astype(BF16), h)
        else:
            q, kt, v, qit, ki, wit = _dsa_in(h, norm_mix[i], _dsa_in_weight(dsa_w_in[j]), dsa_q_norm[j],
                                             dsa_k_norm[j], dsa_idx_ln_g[j], dsa_idx_ln_b[j])
            r3 = lambda t: t.reshape(B, LP, t.shape[-1])
            r4 = lambda t: t.reshape(B, LP // KEY_TILE, t.shape[-2], KEY_TILE)
            o = _dsa_core(r3(q), r4(kt), r3(v), r4(qit), r3(ki), r4(wit), rel_bias, n_select, off + S)
            h = _matmul_residual(o.reshape(B * LP, ATT_Q_DIM), dsa_w_out[j].astype(BF16), h)
        h = _moe(h, norm_ffn[i], moe_w_group[i], moe_b_group[i], moe_w_expert[i], moe_b_expert[i],
                 wgu_all, wd_all, i, frames=(LP, S, off) if i == depth - 1 else None)
    return h.reshape(B, S, D)
```

```python
import functools
import math

import jax
import jax.numpy as jnp
from jax import lax
from jax.experimental import pallas as pl
from jax.experimental.pallas import tpu as pltpu

F32 = jnp.float32
BF16 = jnp.bfloat16

LANES = 128
SUBLANES = 8
VMEM_LIMIT_BYTES = 56 * 1024 * 1024

CHUNK = 64
N_META = 16
META_PAD = (-N_META) % CHUNK
RMS_EPS = 1e-6
L2_EPS = 1e-6

GDN_HEADS = 8
GDN_DK = 128
GDN_DV = 128
GDN_CONV = 4
GDN_QK_DIM = GDN_HEADS * GDN_DK
GDN_V_DIM = GDN_HEADS * GDN_DV
GDN_CONV_DIM = 2 * GDN_QK_DIM + GDN_V_DIM
GDN_GATE_OFF = GDN_CONV_DIM + GDN_V_DIM
GDN_PROJ_W = GDN_GATE_OFF + LANES

ATT_HEADS = 8
ATT_KV_HEADS = 2
ATT_GROUP = ATT_HEADS // ATT_KV_HEADS
ATT_HD = 128
ATT_Q_DIM = ATT_HEADS * ATT_HD
ATT_KV_DIM = ATT_KV_HEADS * ATT_HD
V_WIDE = ATT_HD + LANES
IDX_HEADS = 8
IDX_HD = 64
TOPK_MAX = 256
REL_BUCKETS = 32
REL_MAX_DIST = 128

DSA_Q_OFF = 0
DSA_K_OFF = ATT_Q_DIM
DSA_V_OFF = DSA_K_OFF + ATT_KV_DIM
DSA_QI_OFF = DSA_V_OFF + ATT_KV_DIM
DSA_KI_OFF = DSA_QI_OFF + IDX_HEADS * LANES
DSA_WI_OFF = DSA_KI_OFF + LANES
DSA_PROJ_W = DSA_WI_OFF + LANES

N_GROUPS = 4
EXPERTS_PER_GROUP = 8
N_EXPERTS = N_GROUPS * EXPERTS_PER_GROUP
EXPERT_FF = 256

GDN_ROW_TILES = (256, 128, 64)
MOE_ROUTER_TILES = (512, 256, 128)
MOE_ROW_TILES = (512, 256, 128)

ATT_TILE = 256
KEY_TILE = 256
ROW_ALIGN = KEY_TILE
INDEXER_GROUP = 8
ATTEND_GROUP = 4
FRONT_PAD = ROW_ALIGN - N_META
assert FRONT_PAD % CHUNK == META_PAD
BISECT_UNROLL = 4
BISECT_STEPS = 20
SHIFT_SLACK = 1.001
ROW_SUM_FLOOR = 1e-30
NEG_INF = float("-inf")


def _pick_tile(n, candidates):
    for c in candidates:
        if n % c == 0:
            return c
    raise ValueError(f"no tile for {n}")


def _compiler_params(semantics):
    return pltpu.CompilerParams(dimension_semantics=semantics, vmem_limit_bytes=VMEM_LIMIT_BYTES)


def _silu(x):
    return x * jax.nn.sigmoid(x)


def _norm_matmul_kernel(x_ref, g_ref, w_ref, o_ref):
    x = x_ref[...]
    y = x * lax.rsqrt(jnp.mean(x * x, axis=-1, keepdims=True) + RMS_EPS) * g_ref[...]
    o_ref[...] = jnp.dot(y.astype(BF16), w_ref[...], preferred_element_type=F32)


def _matmul_residual_kernel(a_ref, w_ref, r_ref, o_ref):
    o_ref[...] = r_ref[...] + jnp.dot(a_ref[...], w_ref[...], preferred_element_type=F32)


def _matmul_residual(a_bf16, w_bf16, res):
    T, K = a_bf16.shape
    N = w_bf16.shape[1]
    tm = _pick_tile(T, (512, 256, 128))
    return pl.pallas_call(
        _matmul_residual_kernel,
        grid=(T // tm,),
        in_specs=[pl.BlockSpec((tm, K), lambda i: (i, 0)),
                  pl.BlockSpec((K, N), lambda i: (0, 0)),
                  pl.BlockSpec((tm, N), lambda i: (i, 0))],
        out_specs=pl.BlockSpec((tm, N), lambda i: (i, 0)),
        out_shape=jax.ShapeDtypeStruct((T, N), F32),
        compiler_params=_compiler_params(("parallel",)),
        name="matmul_residual",
    )(a_bf16, w_bf16, res)


def _nt_dot(a, b):
    return lax.dot_general(a, b, (((1,), (1,)), ((), ())), preferred_element_type=F32)


def _tn_dot(a, b):
    return lax.dot_general(a, b, (((0,), (0,)), ((), ())), preferred_element_type=F32)


def _gdn_kernel(x_ref, g_ref, w_ref, conv_ref, gate_ref, ogain_ref, o_ref,
                state_ref, xbuf_ref, qkv_ref, proj_ref):
    t = pl.program_id(0)
    nb, rows = x_ref.shape[0], x_ref.shape[1]

    @pl.when(t == 0)
    def _():
        state_ref[...] = jnp.zeros_like(state_ref)
        xbuf_ref[:, 0:SUBLANES, :] = jnp.zeros((nb, SUBLANES, GDN_CONV_DIM), F32)

    for b in range(nb):
        _norm_matmul_kernel(x_ref.at[b], g_ref, w_ref, proj_ref.at[b])

    def chunk(cc, carry):
        r0 = pl.multiple_of(cc * CHUNK, CHUNK)
        _gdn_chunk(t * rows + r0, r0, nb, proj_ref, conv_ref, gate_ref, ogain_ref, o_ref,
                   state_ref, xbuf_ref, qkv_ref)
        return carry

    lax.fori_loop(0, rows // CHUNK, chunk, 0)


def _gdn_chunk(row0, r0, nb, proj_ref, conv_ref, gate_ref, ogain_ref, o_ref, state_ref, xbuf_ref, qkv_ref):
    C = CHUNK
    row = row0 + lax.broadcasted_iota(jnp.int32, (C, 1), 0)
    live = row >= FRONT_PAD
    r_i = lax.broadcasted_iota(jnp.int32, (C, C), 0)
    c_i = lax.broadcasted_iota(jnp.int32, (C, C), 1)
    causal = c_i <= r_i
    strict = c_i < r_i
    neg_rate = -jnp.exp(gate_ref[0:1, :])
    betas, gcs, gc_ts, egs, eg_lasts, eks = [], [], [], [], [], []
    for b in range(nb):
        u = jnp.where(live, proj_ref[b, pl.ds(r0, C), 0:GDN_CONV_DIM], 0.0)
        xbuf_ref[b, SUBLANES:SUBLANES + C, :] = u
        conv = conv_ref[0:1, :] * xbuf_ref[b, SUBLANES - 3:SUBLANES - 3 + C, :]
        for j in range(1, GDN_CONV):
            conv = conv + conv_ref[j:j + 1, :] * xbuf_ref[b, SUBLANES - 3 + j:SUBLANES - 3 + j + C, :]
        xbuf_ref[b, 0:SUBLANES, :] = u[C - SUBLANES:C, :]
        qkv_ref[b] = _silu(conv)

        gates = jnp.where(live, proj_ref[b, pl.ds(r0, C), GDN_GATE_OFF:GDN_GATE_OFF + LANES], 0.0)
        g = neg_rate * jax.nn.softplus(gates + gate_ref[1:2, :])
        gc = jnp.dot(causal.astype(F32), g, preferred_element_type=F32,
                     precision=lax.Precision.HIGHEST)
        g_last = gc[C - 1:C, :]
        betas.append(jax.nn.sigmoid(gates))
        gcs.append(gc)
        gc_ts.append(gc.T)
        egs.append(jnp.exp(gc))
        eg_lasts.append(jnp.exp(g_last))
        eks.append(jnp.exp(g_last - gc))

    chains = [(b, h) for b in range(nb) for h in range(GDN_HEADS)]
    heads = range(len(chains))
    dot = functools.partial(jnp.dot, preferred_element_type=F32)
    col = lambda a, h: a[:, GDN_HEADS + h:GDN_HEADS + h + 1]
    qs, ks, k16s, xs, decays = [], [], [], [], []
    for b, h in chains:
        lo = h * GDN_DK
        q = qkv_ref[b, :, lo:lo + GDN_DK]
        k = qkv_ref[b, :, GDN_QK_DIM + lo:GDN_QK_DIM + lo + GDN_DK]
        v = qkv_ref[b, :, 2 * GDN_QK_DIM + h * GDN_DV:2 * GDN_QK_DIM + (h + 1) * GDN_DV]
        q = q * lax.rsqrt(jnp.sum(q * q, axis=-1, keepdims=True) + L2_EPS) * (GDN_DK ** -0.5)
        k = k * lax.rsqrt(jnp.sum(k * k, axis=-1, keepdims=True) + L2_EPS)
        kb = k * betas[b][:, h:h + 1]
        vb = v * betas[b][:, h:h + 1]
        qs.append(q)
        ks.append(k)
        k16s.append(k.astype(BF16))
        xs.append((kb, jnp.concatenate([vb, kb * col(egs[b], h)], axis=1)))
        decays.append(jnp.exp(jnp.where(causal, col(gcs[b], h) - gc_ts[b][GDN_HEADS + h:GDN_HEADS + h + 1, :],
                                        NEG_INF)))
    kk = [_nt_dot(xs[h][0].astype(BF16), k16s[h]) for h in heads]
    qk = [_nt_dot(qs[h].astype(BF16), k16s[h]) for h in heads]
    ns = [-jnp.where(strict, kk[h] * decays[h], 0.0) for h in heads]
    xs = [x for _, x in xs]
    for it in range(6):
        n16 = [n.astype(BF16) for n in ns]
        xs = [xs[h] + dot(n16[h], xs[h].astype(BF16)) for h in heads]
        if it < 5:
            ns = [dot(n16[h], n16[h]) for h in heads]
    qk = [jnp.where(causal, qk[h] * decays[h], 0.0).astype(BF16) for h in heads]
    states = [state_ref[c] for c in heads]
    s16 = [s.astype(BF16) for s in states]
    ws = [dot(xs[c][:, GDN_DV:2 * GDN_DV].astype(BF16), s16[c]) for c in heads]
    qs_s = [dot((qs[c] * col(egs[b], h)).astype(BF16), s16[c]) for c, (b, h) in enumerate(chains)]
    v_new = [(xs[c][:, 0:GDN_DV] - ws[c]).astype(BF16) for c in heads]
    os_ = [qs_s[c] + dot(qk[c], v_new[c]) for c in heads]
    kv = [_tn_dot((ks[c] * col(eks[b], h)).astype(BF16), v_new[c]) for c, (b, h) in enumerate(chains)]
    for c, (b, h) in enumerate(chains):
        state_ref[c] = states[c] * col(eg_lasts[b], h) + kv[c]
        z = proj_ref[b, pl.ds(r0, C), GDN_CONV_DIM + h * GDN_DV:GDN_CONV_DIM + (h + 1) * GDN_DV]
        o = os_[c]
        on = o * lax.rsqrt(jnp.mean(o * o, axis=-1, keepdims=True) + RMS_EPS) * ogain_ref[...]
        o_ref[b, pl.ds(r0, C), h * GDN_DV:(h + 1) * GDN_DV] = (on * _silu(z)).astype(o_ref.dtype)


def _gdn_first_kernel(x_ref, head_ref, g_ref, w_ref, conv_ref, gate_ref, ogain_ref, o_ref, h_ref, *scratch):
    t = pl.program_id(0)

    @pl.when(t == 0)
    def _():
        h_ref[...] = head_ref[...]

    @pl.when(t > 0)
    def _():
        h_ref[...] = x_ref[...]

    _gdn_kernel(h_ref, g_ref, w_ref, conv_ref, gate_ref, ogain_ref, o_ref, *scratch)


def _gdn_core(src, gain, w_bf16, conv_w, a_log, dt_bias, o_gain, head=None):
    W = w_bf16.shape[1]
    gate = jnp.zeros((2, LANES), F32)
    gate = gate.at[0, GDN_HEADS:2 * GDN_HEADS].set(a_log.astype(F32))
    gate = gate.at[1, GDN_HEADS:2 * GDN_HEADS].set(dt_bias.astype(F32))
    fix = lambda t: (0, 0)
    if head is not None:
        B, S, D = src.shape
        rows = head.shape[1]
        LP = rows + S
        assert S % rows == 0
        scratch = [pltpu.VMEM((B * GDN_HEADS, GDN_DK, GDN_DV), F32),
                   pltpu.VMEM((B, SUBLANES + CHUNK, GDN_CONV_DIM), F32),
                   pltpu.VMEM((B, CHUNK, GDN_CONV_DIM), F32),
                   pltpu.VMEM((B, rows, W), F32)]
        return pl.pallas_call(
            _gdn_first_kernel,
            grid=(LP // rows,),
            in_specs=[pl.BlockSpec((B, rows, D), lambda t: (0, jnp.maximum(t - 1, 0), 0)),
                      pl.BlockSpec((B, rows, D), lambda t: (0, 0, 0), pipeline_mode=pl.Buffered(1)),
                      pl.BlockSpec((1, D), fix),
                      pl.BlockSpec((D, W), fix, pipeline_mode=pl.Buffered(1)),
                      pl.BlockSpec((GDN_CONV, GDN_CONV_DIM), fix),
                      pl.BlockSpec((2, LANES), fix),
                      pl.BlockSpec((1, GDN_DV), fix)],
            out_specs=[pl.BlockSpec((B, rows, GDN_V_DIM), lambda t: (0, t, 0)),
                       pl.BlockSpec((B, rows, D), lambda t: (0, t, 0))],
            out_shape=[jax.ShapeDtypeStruct((B, LP, GDN_V_DIM), BF16),
                       jax.ShapeDtypeStruct((B, LP, D), F32)],
            scratch_shapes=scratch,
            compiler_params=_compiler_params(("arbitrary",)),
            name="gdn_core",
        )(src, head, gain.reshape(1, D), w_bf16, conv_w, gate, o_gain.reshape(1, GDN_DV))
    h3d = src
    B, LP, D = h3d.shape
    rows = _pick_tile(LP, GDN_ROW_TILES)
    return pl.pallas_call(
        _gdn_kernel,
        grid=(LP // rows,),
        in_specs=[pl.BlockSpec((B, rows, D), lambda t: (0, t, 0)),
                  pl.BlockSpec((1, D), fix),
                  pl.BlockSpec((D, W), fix, pipeline_mode=pl.Buffered(1)),
                  pl.BlockSpec((GDN_CONV, GDN_CONV_DIM), fix),
                  pl.BlockSpec((2, LANES), fix),
                  pl.BlockSpec((1, GDN_DV), fix)],
        out_specs=pl.BlockSpec((B, rows, GDN_V_DIM), lambda t: (0, t, 0)),
        out_shape=jax.ShapeDtypeStruct((B, LP, GDN_V_DIM), BF16),
        scratch_shapes=[pltpu.VMEM((B * GDN_HEADS, GDN_DK, GDN_DV), F32),
                        pltpu.VMEM((B, SUBLANES + CHUNK, GDN_CONV_DIM), F32),
                        pltpu.VMEM((B, CHUNK, GDN_CONV_DIM), F32),
                        pltpu.VMEM((B, rows, W), F32)],
        compiler_params=_compiler_params(("arbitrary",)),
        name="gdn_core",
    )(h3d, gain.reshape(1, D), w_bf16, conv_w, gate, o_gain.reshape(1, GDN_DV))


def _moe_router_kernel(h_ref, gain_ref, wr_ref, br_ref, hx_ref, gid_ref, rank_ref, cnt_ref, carry_ref):
    i = pl.program_id(0)
    TM = h_ref.shape[0]
    lane = lax.broadcasted_iota(jnp.int32, (1, LANES), 1)

    @pl.when(i == 0)
    def _():
        carry_ref[...] = jnp.zeros_like(carry_ref)

    x = h_ref[...]
    xn = x * lax.rsqrt(jnp.mean(x * x, axis=-1, keepdims=True) + RMS_EPS) * gain_ref[...]
    logits = jnp.dot(xn, wr_ref[...], preferred_element_type=F32,
                     precision=lax.Precision.HIGHEST) + br_ref[...]
    e_log = logits
    g_lane = lane - N_EXPERTS
    g_log = jnp.where((g_lane >= 0) & (g_lane < N_GROUPS), logits, NEG_INF)
    g_max = jnp.max(g_log, axis=-1, keepdims=True)
    g_sel = jnp.min(jnp.where(g_log == g_max, g_lane, LANES), axis=-1, keepdims=True)
    g_sel = jnp.minimum(g_sel, N_GROUPS - 1)
    g_w = 1.0 / jnp.sum(jnp.exp(g_log - g_max), axis=-1, keepdims=True)
    in_grp = (lane >= g_sel * EXPERTS_PER_GROUP) & (lane < (g_sel + 1) * EXPERTS_PER_GROUP)
    l0 = jnp.where(in_grp, e_log, NEG_INF)
    m1 = jnp.max(l0, axis=-1, keepdims=True)
    i1 = jnp.min(jnp.where(l0 == m1, lane, LANES), axis=-1, keepdims=True)
    l1 = jnp.where(lane == i1, NEG_INF, l0)
    m2 = jnp.max(l1, axis=-1, keepdims=True)
    i2 = jnp.min(jnp.where(l1 == m2, lane, LANES), axis=-1, keepdims=True)
    p2 = jnp.exp(m2 - m1)
    w1 = g_w / (1.0 + p2)
    w2 = g_w * p2 / (1.0 + p2)
    j1 = i1 - g_sel * EXPERTS_PER_GROUP
    j2 = i2 - g_sel * EXPERTS_PER_GROUP
    hx_ref[:, 0:xn.shape[1]] = xn
    hx_ref[:, xn.shape[1]:] = jnp.where(lane == j1, w1, 0.0) + jnp.where(lane == j2, w2, 0.0)

    onehot = jnp.where(lane == g_sel, 1.0, 0.0)
    r_i = lax.broadcasted_iota(jnp.int32, (TM, TM), 0)
    c_i = lax.broadcasted_iota(jnp.int32, (TM, TM), 1)
    before = jnp.where(c_i < r_i, 1.0, 0.0).astype(BF16)
    prior = jnp.dot(before, onehot.astype(BF16), preferred_element_type=F32) + carry_ref[...]
    rank = jnp.sum(jnp.where(lane == g_sel, prior, 0.0), axis=-1, keepdims=True)
    carry_ref[...] += jnp.sum(onehot, axis=0, keepdims=True)
    cnt_ref[...] = carry_ref[...]
    gid_ref[0] = jnp.broadcast_to(g_sel.astype(F32), (TM, LANES)).T[0:1, :].astype(jnp.int32)
    rank_ref[0] = jnp.broadcast_to(rank, (TM, LANES)).T[0:1, :].astype(jnp.int32)


def _moe_dispatch_kernel(seg_ref, pos_ref, hx_ref, xs_ref, zero_ref, sem, zero_sem):
    TM = hx_ref.shape[0]

    @pl.when(pl.program_id(0) == 0)
    def _():
        zero_ref[...] = jnp.zeros_like(zero_ref)

        def zero_tile(t):
            fill = pltpu.make_async_copy(zero_ref, xs_ref.at[pl.ds(t * TM, TM), :], zero_sem)
            fill.start()
            fill.wait()

        for g in range(N_GROUPS):
            first = seg_ref[g - 1] if g else 0
            pl.when(seg_ref[g] > first)(functools.partial(zero_tile, seg_ref[g] - 1))
            tail = seg_ref[N_GROUPS - 1] + g
            pl.when(tail < xs_ref.shape[0] // TM)(functools.partial(zero_tile, tail))

    def start(a, carry):
        for b in range(SUBLANES):
            r = a * SUBLANES + b
            pltpu.make_async_copy(hx_ref.at[pl.ds(r, 1), :], xs_ref.at[pl.ds(pos_ref[0, 0, r], 1), :],
                                  sem).start()
        return carry

    lax.fori_loop(0, TM // SUBLANES, start, 0)
    pltpu.make_async_copy(hx_ref, xs_ref.at[pl.ds(0, TM), :], sem).wait()


def _moe_expert_kernel(grp_ref, xs_ref, wgu_ref, wd_ref, ys_ref, xn_ref, acc_ref):
    j = pl.program_id(0)
    e = pl.program_id(1)
    D = xn_ref.shape[1]
    lane = lax.broadcasted_iota(jnp.int32, (1, LANES), 1)

    @pl.when(grp_ref[j] >= 0)
    def _():
        @pl.when(e == 0)
        def _():
            xn_ref[...] = xs_ref[:, 0:D].astype(BF16)
            acc_ref[...] = jnp.zeros_like(acc_ref)

        hh = jnp.dot(xn_ref[...], wgu_ref[0], preferred_element_type=F32)
        cw = jnp.sum(jnp.where(lane == e, xs_ref[:, D:], 0.0), axis=-1, keepdims=True)
        act = _silu(hh[:, 0:EXPERT_FF]) * hh[:, EXPERT_FF:2 * EXPERT_FF] * cw
        acc_ref[...] += jnp.dot(act.astype(BF16), wd_ref[0], preferred_element_type=F32)

    last = e == EXPERTS_PER_GROUP - 1

    @pl.when(last & (grp_ref[j] >= 0))
    def _():
        ys_ref[...] = acc_ref[...]

    @pl.when(last & (grp_ref[j] < 0))
    def _():
        ys_ref[...] = jnp.zeros_like(ys_ref)


def _moe_combine_kernel(pos_ref, h_ref, ys_ref, o_ref, stage_ref, sem):
    TM = h_ref.shape[0]

    def start(a, carry):
        for b in range(SUBLANES):
            r = a * SUBLANES + b
            pltpu.make_async_copy(ys_ref.at[pl.ds(pos_ref[0, 0, r], 1), :], stage_ref.at[pl.ds(r, 1), :],
                                  sem).start()
        return carry

    lax.fori_loop(0, TM // SUBLANES, start, 0)
    pltpu.make_async_copy(ys_ref.at[pl.ds(0, TM), :], stage_ref, sem).wait()
    o_ref[...] = h_ref[...] + stage_ref[...]


def _moe(h2d, gain, w_group, b_group, w_expert, b_expert, wgu, wd, layer, frames=None):
    T, D = h2d.shape
    tr = _pick_tile(T, MOE_ROUTER_TILES)
    tm = _pick_tile(T, MOE_ROW_TILES)
    n_tiles = T // tm + N_GROUPS
    XW = D + LANES
    assert N_EXPERTS + N_GROUPS <= LANES
    wr = jnp.zeros((D, LANES), F32)
    wr = wr.at[:, 0:N_EXPERTS].set(w_expert.astype(F32))
    wr = wr.at[:, N_EXPERTS:N_EXPERTS + N_GROUPS].set(w_group.astype(F32))
    br = jnp.zeros((1, LANES), F32)
    br = br.at[0, 0:N_EXPERTS].set(b_expert.astype(F32))
    br = br.at[0, N_EXPERTS:N_EXPERTS + N_GROUPS].set(b_group.astype(F32))

    hx, gid, rank, counts = pl.pallas_call(
        _moe_router_kernel,
        grid=(T // tr,),
        in_specs=[pl.BlockSpec((tr, D), lambda i: (i, 0)),
                  pl.BlockSpec((1, D), lambda i: (0, 0)),
                  pl.BlockSpec((D, LANES), lambda i: (0, 0)),
                  pl.BlockSpec((1, LANES), lambda i: (0, 0))],
        out_specs=[pl.BlockSpec((tr, XW), lambda i: (i, 0)),
                   pl.BlockSpec((1, 1, tr), lambda i: (i, 0, 0)),
                   pl.BlockSpec((1, 1, tr), lambda i: (i, 0, 0)),
                   pl.BlockSpec((1, LANES), lambda i: (0, 0))],
        out_shape=[jax.ShapeDtypeStruct((T, XW), F32),
                   jax.ShapeDtypeStruct((T // tr, 1, tr), jnp.int32),
                   jax.ShapeDtypeStruct((T // tr, 1, tr), jnp.int32),
                   jax.ShapeDtypeStruct((1, LANES), F32)],
        scratch_shapes=[pltpu.VMEM((1, LANES), F32)],
        compiler_params=_compiler_params(("arbitrary",)),
        name="moe_router",
    )(h2d, gain.reshape(1, D), wr, br)

    seg_tiles = -(-counts[0, 0:N_GROUPS].astype(jnp.int32) // tm)
    seg_end = jnp.cumsum(seg_tiles)
    seg_start = seg_end - seg_tiles
    pos = (seg_start * tm)[gid.reshape(T)] + rank.reshape(T)
    tile_ids = jnp.arange(n_tiles, dtype=jnp.int32)
    tile_group = jnp.sum((tile_ids[:, None] >= seg_end[None, :]).astype(jnp.int32), axis=1)
    tile_group = jnp.where(tile_ids < seg_end[N_GROUPS - 1], tile_group, -1)
    pos3 = pos.reshape(T // tm, 1, tm)

    xs = pl.pallas_call(
        _moe_dispatch_kernel,
        grid_spec=pltpu.PrefetchScalarGridSpec(
            num_scalar_prefetch=1,
            grid=(T // tm,),
            in_specs=[pl.BlockSpec((1, 1, tm), lambda i, seg: (i, 0, 0), memory_space=pltpu.SMEM),
                      pl.BlockSpec((tm, XW), lambda i, seg: (i, 0))],
            out_specs=pl.BlockSpec(memory_space=pl.ANY),
            scratch_shapes=[pltpu.VMEM((tm, XW), F32), pltpu.SemaphoreType.DMA(()),
                            pltpu.SemaphoreType.DMA(())]),
        out_shape=jax.ShapeDtypeStruct((n_tiles * tm, XW), F32),
        compiler_params=_compiler_params(("arbitrary",)),
        name="moe_dispatch",
    )(seg_end, pos3, hx)

    def expert_index(j, e, grp):
        return (layer * N_EXPERTS + jnp.maximum(grp[j], 0) * EXPERTS_PER_GROUP + e, 0, 0)

    ys = pl.pallas_call(
        _moe_expert_kernel,
        grid_spec=pltpu.PrefetchScalarGridSpec(
            num_scalar_prefetch=1,
            grid=(n_tiles, EXPERTS_PER_GROUP),
            in_specs=[pl.BlockSpec((tm, XW), lambda j, e, grp: (j, 0)),
                      pl.BlockSpec((1, D, 2 * EXPERT_FF), expert_index),
                      pl.BlockSpec((1, EXPERT_FF, D), expert_index)],
            out_specs=pl.BlockSpec((tm, D), lambda j, e, grp: (j, 0)),
            scratch_shapes=[pltpu.VMEM((tm, D), BF16), pltpu.VMEM((tm, D), F32)]),
        out_shape=jax.ShapeDtypeStruct((n_tiles * tm, D), F32),
        compiler_params=_compiler_params(("parallel", "arbitrary")),
        name="moe_experts",
    )(tile_group, xs, wgu, wd)

    if frames is None:
        tc, out_rows, out_index = tm, T, lambda i: (i, 0)
    else:
        batch_rows, n_frames, first = frames
        tc = ROW_ALIGN
        assert batch_rows % tc == 0 and n_frames % tc == 0 and first % tc == 0 and first >= tc
        out_rows = (T // batch_rows) * n_frames

        def out_index(i):
            b, t = i // (batch_rows // tc), i % (batch_rows // tc)
            return (b * (n_frames // tc) + jnp.maximum(t - first // tc, 0), 0)

    return pl.pallas_call(
        _moe_combine_kernel,
        grid=(T // tc,),
        in_specs=[pl.BlockSpec((1, 1, tc), lambda i: (i, 0, 0), memory_space=pltpu.SMEM),
                  pl.BlockSpec((tc, D), lambda i: (i, 0)),
                  pl.BlockSpec(memory_space=pl.ANY)],
        out_specs=pl.BlockSpec((tc, D), out_index),
        out_shape=jax.ShapeDtypeStruct((out_rows, D), F32),
        scratch_shapes=[pltpu.VMEM((tc, D), F32), pltpu.SemaphoreType.DMA(())],
        compiler_params=_compiler_params(("arbitrary",)),
        name="moe_combine",
    )(pos.reshape(T // tc, 1, tc), h2d, ys)


def _dsa_prep_kernel(p_ref, qg_ref, kg_ref, lng_ref, lnb_ref, q_ref, kt_ref, v_ref, qit_ref, ki_ref, wit_ref):
    lane = lax.broadcasted_iota(jnp.int32, (1, LANES), 1)
    for h in range(ATT_HEADS):
        x = p_ref[:, DSA_Q_OFF + h * ATT_HD:DSA_Q_OFF + (h + 1) * ATT_HD]
        y = x * lax.rsqrt(jnp.mean(x * x, axis=-1, keepdims=True) + RMS_EPS) * qg_ref[...]
        q_ref[:, h * ATT_HD:(h + 1) * ATT_HD] = (y * (ATT_HD ** -0.5)).astype(BF16)
    for n in range(ATT_KV_HEADS):
        x = p_ref[:, DSA_K_OFF + n * ATT_HD:DSA_K_OFF + (n + 1) * ATT_HD]
        y = x * lax.rsqrt(jnp.mean(x * x, axis=-1, keepdims=True) + RMS_EPS) * kg_ref[...]
        kt_ref[0, n * ATT_HD:(n + 1) * ATT_HD, :] = y.T.astype(BF16)
    ones_col = jnp.where(lane == 0, 1.0, 0.0).astype(BF16)
    for n in range(ATT_KV_HEADS):
        v_ref[:, n * V_WIDE:n * V_WIDE + ATT_HD] = p_ref[:, DSA_V_OFF + n * ATT_HD:DSA_V_OFF + (n + 1) * ATT_HD].astype(BF16)
        v_ref[:, n * V_WIDE + ATT_HD:(n + 1) * V_WIDE] = jnp.broadcast_to(ones_col, (v_ref.shape[0], LANES))
    qit_ref[0] = p_ref[:, DSA_QI_OFF:DSA_QI_OFF + IDX_HEADS * LANES].T.astype(BF16)
    x = p_ref[:, DSA_KI_OFF:DSA_KI_OFF + LANES]
    live = lane < IDX_HD
    mu = jnp.sum(x, axis=-1, keepdims=True) * (1.0 / IDX_HD)
    xc = jnp.where(live, x - mu, 0.0)
    var = jnp.sum(xc * xc, axis=-1, keepdims=True) * (1.0 / IDX_HD)
    ki = xc * lax.rsqrt(var + RMS_EPS) * lng_ref[...] + lnb_ref[...]
    ki_ref[...] = jnp.where(live, ki, 0.0).astype(BF16)
    wit_ref[0] = (p_ref[:, DSA_WI_OFF:DSA_WI_OFF + LANES] * (IDX_HEADS ** -0.5 * IDX_HD ** -0.5)).T


def _dsa_in_kernel(x_ref, g_ref, w_ref, qg_ref, kg_ref, lng_ref, lnb_ref,
                   q_ref, kt_ref, v_ref, qit_ref, ki_ref, wit_ref, proj_ref):
    _norm_matmul_kernel(x_ref, g_ref, w_ref, proj_ref)
    _dsa_prep_kernel(proj_ref, qg_ref, kg_ref, lng_ref, lnb_ref, q_ref, kt_ref, v_ref, qit_ref, ki_ref, wit_ref)


def _dsa_in(h2d, gain, w_bf16, q_gain, k_gain, ln_g, ln_b):
    T, D = h2d.shape
    W = w_bf16.shape[1]
    tm = KEY_TILE
    assert T % tm == 0 and ATT_TILE == KEY_TILE
    lng = jnp.zeros((1, LANES), F32).at[0, 0:IDX_HD].set(ln_g.astype(F32))
    lnb = jnp.zeros((1, LANES), F32).at[0, 0:IDX_HD].set(ln_b.astype(F32))
    row = lambda i: (i, 0)
    fix = lambda i: (0, 0)
    blk = lambda i: (i, 0, 0)
    return pl.pallas_call(
        _dsa_in_kernel,
        grid=(T // tm,),
        in_specs=[pl.BlockSpec((tm, D), row), pl.BlockSpec((1, D), fix), pl.BlockSpec((D, W), fix),
                  pl.BlockSpec((1, ATT_HD), fix), pl.BlockSpec((1, ATT_HD), fix),
                  pl.BlockSpec((1, LANES), fix), pl.BlockSpec((1, LANES), fix)],
        out_specs=[pl.BlockSpec((tm, ATT_Q_DIM), row), pl.BlockSpec((1, ATT_KV_DIM, tm), blk),
                   pl.BlockSpec((tm, ATT_KV_HEADS * V_WIDE), row), pl.BlockSpec((1, IDX_HEADS * LANES, tm), blk),
                   pl.BlockSpec((tm, LANES), row), pl.BlockSpec((1, LANES, tm), blk)],
        out_shape=[jax.ShapeDtypeStruct((T, ATT_Q_DIM), BF16), jax.ShapeDtypeStruct((T // tm, ATT_KV_DIM, tm), BF16),
                   jax.ShapeDtypeStruct((T, ATT_KV_HEADS * V_WIDE), BF16),
                   jax.ShapeDtypeStruct((T // tm, IDX_HEADS * LANES, tm), BF16),
                   jax.ShapeDtypeStruct((T, LANES), BF16), jax.ShapeDtypeStruct((T // tm, LANES, tm), F32)],
        scratch_shapes=[pltpu.VMEM((tm, W), F32)],
        compiler_params=_compiler_params(("parallel",)),
        name="dsa_in",
    )(h2d, gain.reshape(1, D), w_bf16, q_gain.reshape(1, ATT_HD), k_gain.reshape(1, ATT_HD), lng, lnb)


def _dsa_kernel(q_ref, qit_ref, wit_ref, kt_ref, v_ref, ki_ref, bias_ref, bmax_ref, o_ref,
                sc_ref, mask_ref, qg_ref, m_ref, acc_ref, knorm_ref, *, n_select, live_rows):
    i = pl.program_id(1)
    TQ = ATT_TILE
    TK = KEY_TILE
    HALVES = TK // LANES

    @pl.when(i == 0)
    def _():
        def body(kt, mx):
            kk = kt_ref[0, kt].astype(F32)
            return tuple(jnp.maximum(mx[n], jnp.sum(jnp.square(kk[n * ATT_HD:(n + 1) * ATT_HD, :]),
                                                    axis=0, keepdims=True)) for n in range(ATT_KV_HEADS))

        mx = lax.fori_loop(0, kt_ref.shape[1], body,
                           tuple(jnp.zeros((1, TK), F32) for _ in range(ATT_KV_HEADS)))
        for n in range(ATT_KV_HEADS):
            knorm_ref[n] = jnp.broadcast_to(jnp.sqrt(jnp.max(mx[n], axis=-1, keepdims=True)), (1, LANES))

    @pl.when(i * TQ >= live_rows)
    def _():
        o_ref[...] = jnp.zeros_like(o_ref)

    @pl.when(i * TQ < live_rows)
    def _():
        _dsa_tile(i, q_ref, qit_ref, wit_ref, kt_ref, v_ref, ki_ref, bias_ref, bmax_ref, o_ref,
                  sc_ref, mask_ref, qg_ref, m_ref, acc_ref, knorm_ref, n_select, TQ, TK, HALVES)


def _dsa_tile(i, q_ref, qit_ref, wit_ref, kt_ref, v_ref, ki_ref, bias_ref, bmax_ref, o_ref,
              sc_ref, mask_ref, qg_ref, m_ref, acc_ref, knorm_ref, n_select, TQ, TK, HALVES):
    assert TQ == TK
    nkt = i + 1
    fmin = float(jnp.finfo(F32).min)
    kf = float(n_select)

    chunk_shift = int(math.log2(CHUNK))
    q_pos = i * TQ + lax.broadcasted_iota(jnp.int32, (1, TQ), 1)
    q_chunk = jnp.maximum(lax.shift_right_logical(q_pos, chunk_shift), FRONT_PAD // CHUNK)
    k_row = lax.broadcasted_iota(jnp.int32, (TK, 1), 0)

    def halves(x):
        return [x[:, j * LANES:(j + 1) * LANES] for j in range(HALVES)]

    def fold(x, op):
        parts = [x[r * SUBLANES:(r + 1) * SUBLANES, :] for r in range(TK // SUBLANES)]
        while len(parts) > 1:
            parts = [op(parts[j], parts[j + 1]) for j in range(0, len(parts), 2)]
        return parts[0]

    def score_tiles(kts):
        for kt in kts:
            keys = ki_ref[0, pl.ds(pl.multiple_of(kt * TK, TK), TK), :]
            acc = None
            for h in range(IDX_HEADS):
                d = jnp.dot(keys, qit_ref[0, 0, h * LANES:(h + 1) * LANES, :], preferred_element_type=F32)
                term = wit_ref[0, 0, h:h + 1, :] * jnp.maximum(d, 0.0)
                acc = term if acc is None else acc + term
            sc_ref[kt] = acc
            mask_ref[kt] = acc.T

    def grouped_loop(n, fn, group):
        def body(j, carry):
            fn([group * j + r for r in range(group)])
            return carry

        lax.fori_loop(0, lax.shift_right_logical(n, int(math.log2(group))), body, 0)
        size = group // 2
        while size >= 1:
            first = lax.bitwise_and(n, ~(2 * size - 1))

            @pl.when(lax.bitwise_and(n, size) != 0)
            def _(size=size, first=first):
                fn([first + r for r in range(size)])

            size //= 2

    grouped_loop(nkt, score_tiles, INDEXER_GROUP)

    def mask_inadmissible(kt):
        k_pos = kt * TK + k_row
        adm = (lax.shift_right_logical(k_pos, chunk_shift) <= q_chunk) & (k_pos >= FRONT_PAD)
        sc_ref[kt] = jnp.where(adm, sc_ref[kt], NEG_INF)

    mask_inadmissible(0)

    @pl.when(nkt > 1)
    def _():
        mask_inadmissible(nkt - 1)

    n_adm = CHUNK * (q_chunk + 1) - FRONT_PAD
    need = n_adm > n_select

    def count_where(pred):
        def one(kt):
            return fold(jnp.where(pred(sc_ref[kt], kt), 1.0, 0.0), jnp.add)

        def two(j, acc):
            return acc + (one(2 * j) + one(2 * j + 1))

        acc = lax.fori_loop(0, lax.shift_right_logical(nkt, 1), two, jnp.zeros((SUBLANES, TQ), F32))
        acc = lax.cond(lax.bitwise_and(nkt, 1) == 1, lambda a: a + one(nkt - 1), lambda a: a, acc)
        return jnp.sum(acc, axis=0, keepdims=True)

    def count_ge(t):
        return count_where(lambda s, kt: s >= t)

    def minmax_body(kt, carry):
        mn, mx = carry
        s = sc_ref[kt]
        mn = jnp.minimum(mn, fold(jnp.where(s == NEG_INF, jnp.inf, s), jnp.minimum))
        mx = jnp.maximum(mx, fold(s, jnp.maximum))
        return mn, mx

    mn_l, mx_l = lax.fori_loop(0, nkt, minmax_body,
                               (jnp.full((SUBLANES, TQ), jnp.inf, F32), jnp.full((SUBLANES, TQ), NEG_INF, F32)))
    s_min = jnp.min(mn_l, axis=0, keepdims=True)
    s_max = jnp.max(mx_l, axis=0, keepdims=True)
    lo0 = s_min
    hi0 = s_max + (s_max - s_min) + 1.0
    cnt0 = n_adm.astype(F32)
    active0 = jnp.where(need, 1.0, 0.0)

    def any_active(active):
        return jnp.max(active) > 0.0

    def bis_step(lo, hi, cnt, active):
        mid = lo + (hi - lo) * 0.5
        c = count_ge(mid)
        ge = c >= kf
        moving = (mid > lo) & (mid < hi) & (active > 0.0)
        lo_n = jnp.where(moving & ge, mid, lo)
        hi_n = jnp.where(moving & (~ge), mid, hi)
        cnt_n = jnp.where(moving & ge, c, cnt)
        active_n = jnp.where(moving & (cnt_n != kf), 1.0, 0.0)
        return lo_n, hi_n, cnt_n, active_n

    def bis_body(st):
        it, rest = st[0], st[1:]
        for _ in range(BISECT_UNROLL):
            rest = bis_step(*rest)
        return (it + BISECT_UNROLL,) + rest

    _, lo, hi, cnt, _ = lax.while_loop(
        lambda st: jnp.logical_and(st[0] < BISECT_STEPS, any_active(st[4])), bis_body,
        (jnp.int32(0), lo0, hi0, cnt0, active0))

    def snap_body(st):
        lo, hi, cnt, active = st

        def body(kt, mx):
            s = sc_ref[kt]
            return jnp.maximum(mx, fold(jnp.where(s < hi, s, NEG_INF), jnp.maximum))

        v1 = jnp.max(lax.fori_loop(0, nkt, body, jnp.full((SUBLANES, TQ), NEG_INF, F32)),
                     axis=0, keepdims=True)
        c1 = count_ge(v1)
        on = active > 0.0
        hit = on & (c1 >= kf)
        lo_n = jnp.where(hit, v1, lo)
        cnt_n = jnp.where(hit, c1, cnt)
        hi_n = jnp.where(on & (~hit), v1, hi)
        return lo_n, hi_n, cnt_n, jnp.where(on & (~hit), 1.0, 0.0)

    unsettled = jnp.where(need & (cnt != kf), 1.0, 0.0)
    lo, hi, cnt, _ = lax.while_loop(lambda st: any_active(st[3]), snap_body, (lo, hi, cnt, unsettled))

    thr = jnp.where(need, lo, fmin)
    tied = need & (cnt > kf)
    big_pos = jnp.int32(2 ** 30)

    def tie_break():
        want = kf - count_where(lambda s, kt: s > thr)

        def body(_, st):
            jl, jh = st
            jm = lax.shift_right_arithmetic(jl + jh, 1)
            c = count_where(lambda s, kt: (s == thr) & (kt * TK + k_row <= jm))
            ok = c >= want
            return jnp.where(ok, jl, jm), jnp.where(ok, jm, jh)

        n_keys = sc_ref.shape[0] * TK
        steps = int(math.ceil(math.log2(n_keys + 1))) + 1
        jl0 = jnp.full((1, TQ), -1, jnp.int32)
        jh0 = jnp.full((1, TQ), n_keys, jnp.int32)
        _, jh = lax.fori_loop(0, steps, body, (jl0, jh0))
        return jnp.where(tied, jh, big_pos)

    pos_cap = lax.cond(jnp.max(jnp.where(tied, 1.0, 0.0)) > 0.0, tie_break,
                       lambda: jnp.full((1, TQ), big_pos, jnp.int32))

    shift_w = None
    for n in range(ATT_KV_HEADS):
        for g in range(ATT_GROUP):
            h = n * ATT_GROUP + g
            qg_ref[n, g * TQ:(g + 1) * TQ, :] = q_ref[0, :, h * ATT_HD:(h + 1) * ATT_HD]
            qh = q_ref[0, :, h * ATT_HD:(h + 1) * ATT_HD].astype(F32)
            q_len = jnp.sqrt(jnp.sum(qh * qh, axis=-1, keepdims=True))
            bound = q_len * knorm_ref[n] * SHIFT_SLACK + (bmax_ref[h] + SHIFT_SLACK - 1.0)
            shift_w = bound if shift_w is None else jnp.maximum(shift_w, bound)

    def lane_wide(row):
        return jnp.broadcast_to(row, (LANES, TQ)).T

    thr_w = lane_wide(thr)
    cap_w = lane_wide(pos_cap)
    chunk_w = lane_wide(q_chunk)
    k_lane = lax.broadcasted_iota(jnp.int32, (1, LANES), 1)

    def to_mask(kt, edge_tile):
        out = []
        for j, s in enumerate(halves(mask_ref[kt])):
            k_pos = kt * TK + j * LANES + k_lane
            sel = (s > thr_w) | ((s == thr_w) & (k_pos <= cap_w))
            if edge_tile:
                sel = sel & (lax.shift_right_logical(k_pos, chunk_shift) <= chunk_w) & (k_pos >= FRONT_PAD)
            out.append(jnp.where(sel, -shift_w, NEG_INF))
        mask_ref[kt] = jnp.concatenate(out, axis=1)

    def inner_mask(kt, carry):
        to_mask(kt, False)
        return carry

    lax.fori_loop(1, nkt - 1, inner_mask, 0)
    to_mask(0, True)

    @pl.when(nkt > 1)
    def _():
        to_mask(nkt - 1, True)

    acc_ref[...] = jnp.zeros_like(acc_ref)

    kv_heads = range(ATT_KV_HEADS)

    def raw_logits(kts):
        return [[jnp.dot(qg_ref[n], kt_ref[0, kt, n * ATT_HD:(n + 1) * ATT_HD, :], preferred_element_type=F32)
                 for n in kv_heads] for kt in kts]

    def masked_logits(s, mask, n, g, b0):
        sg = s[g * TQ:(g + 1) * TQ, :] + mask
        if b0 is not None:
            h = n * ATT_GROUP + g
            sg = sg + jnp.concatenate([bias_ref[h, b0 + j] for j in range(HALVES)], axis=1)
        return sg

    def plain_mask(kt):
        return jnp.where(mask_ref[kt] == NEG_INF, NEG_INF, 0.0)

    def max_pass(kts, b0):
        raw = raw_logits(kts)
        masks = [plain_mask(kt) for kt in kts]
        for n in kv_heads:
            for g in range(ATT_GROUP):
                rows = slice(g * TQ, (g + 1) * TQ)
                m = m_ref[n, rows, :]
                for t in range(len(kts)):
                    for part in halves(masked_logits(raw[t][n], masks[t], n, g, b0)):
                        m = jnp.maximum(m, part)
                m_ref[n, rows, :] = m

    def exp_pass(kts, b0, exact_max):
        raw = raw_logits(kts)
        masks = [plain_mask(kt) if exact_max else mask_ref[kt] for kt in kts]
        for n in kv_heads:
            ps = [[] for _ in kts]
            for g in range(ATT_GROUP):
                rows = slice(g * TQ, (g + 1) * TQ)
                m = m_ref[n, rows, :] if exact_max else 0.0
                for t in range(len(kts)):
                    parts = [jnp.exp(part - m) if exact_max else jnp.exp(part)
                             for part in halves(masked_logits(raw[t][n], masks[t], n, g, b0))]
                    ps[t].append(jnp.concatenate(parts, axis=1).astype(BF16))
            pv = None
            for t, kt in enumerate(kts):
                vv = v_ref[0, pl.ds(pl.multiple_of(kt * TK, TK), TK), n * V_WIDE:(n + 1) * V_WIDE]
                d = jnp.dot(jnp.concatenate(ps[t], axis=0), vv, preferred_element_type=F32)
                pv = d if pv is None else pv + d
            acc_ref[n] += pv

    def run_pass(tile_fn, group=2):
        grouped_loop(jnp.maximum(i - 1, 0), lambda kts: tile_fn(kts, None), group)

        @pl.when(i >= 1)
        def _():
            tile_fn([i - 1], 0)

        tile_fn([i], HALVES)

    run_pass(functools.partial(exp_pass, exact_max=False), ATTEND_GROUP)

    def row_sums(n):
        return acc_ref[n, :, ATT_HD:ATT_HD + 1]

    def smallest_row_sum():
        return functools.reduce(jnp.minimum, [jnp.min(row_sums(n)) for n in range(ATT_KV_HEADS)])

    @pl.when(jnp.logical_not(smallest_row_sum() > ROW_SUM_FLOOR))
    def _():
        m_ref[...] = jnp.full(m_ref.shape, -1e30, F32)
        run_pass(max_pass)
        for n in range(ATT_KV_HEADS):
            m_ref[n] = jnp.broadcast_to(jnp.max(m_ref[n], axis=-1, keepdims=True), m_ref.shape[1:])
        acc_ref[...] = jnp.zeros_like(acc_ref)
        run_pass(functools.partial(exp_pass, exact_max=True))

    for n in range(ATT_KV_HEADS):
        inv_l = 1.0 / row_sums(n)
        for g in range(ATT_GROUP):
            h = n * ATT_GROUP + g
            rows = slice(g * TQ, (g + 1) * TQ)
            o_ref[0, :, h * ATT_HD:(h + 1) * ATT_HD] = (acc_ref[n, rows, 0:ATT_HD] * inv_l[rows, :]).astype(o_ref.dtype)


def _t5_bucket(rel):
    nb = REL_BUCKETS // 2
    max_exact = nb // 2
    ret = jnp.where(rel > 0, nb, 0)
    n = jnp.abs(rel)
    nf = jnp.maximum(n, 1).astype(jnp.float32)
    large = max_exact + (jnp.log(nf / max_exact) / math.log(REL_MAX_DIST / max_exact)
                         * (nb - max_exact)).astype(jnp.int32)
    large = jnp.minimum(large, nb - 1)
    return ret + jnp.where(n < max_exact, n, large)


def _dsa_core(q, kt, v, qit, ki, wit, rel_bias, n_select, live_rows):
    B, LP, _ = q.shape
    nq = LP // ATT_TILE
    nkt = LP // KEY_TILE
    a = jnp.arange(ATT_TILE)[:, None]
    xx = jnp.arange(2 * KEY_TILE)[None, :]
    rel = (xx - KEY_TILE) - a
    far = rel_bias.astype(F32)[REL_BUCKETS // 2 - 1]
    onehot = jax.nn.one_hot(_t5_bucket(rel), REL_BUCKETS, dtype=F32)
    table = jnp.einsum("abk,kh->abh", onehot, rel_bias.astype(F32),
                       precision=lax.Precision.HIGHEST) - far
    bias_near = jnp.transpose(table.reshape(ATT_TILE, 2 * KEY_TILE // LANES, LANES, ATT_HEADS), (3, 1, 0, 2))
    bias_max = jnp.broadcast_to(jnp.maximum(jnp.max(table, axis=(0, 1)), 0.0)[:, None, None],
                                (ATT_HEADS, 1, LANES))
    kern = functools.partial(_dsa_kernel, n_select=n_select, live_rows=live_rows)
    tile = lambda b, i: (b, i, 0)
    tile4 = lambda b, i: (b, i, 0, 0)
    whole3 = lambda b, i: (b, 0, 0)
    whole4 = lambda b, i: (b, 0, 0, 0)
    once = pl.Buffered(1)
    return pl.pallas_call(
        kern,
        grid=(B, nq),
        in_specs=[pl.BlockSpec((1, ATT_TILE, ATT_Q_DIM), tile),
                  pl.BlockSpec((1, 1, IDX_HEADS * LANES, ATT_TILE), tile4),
                  pl.BlockSpec((1, 1, LANES, ATT_TILE), tile4),
                  pl.BlockSpec((1, nkt, ATT_KV_DIM, KEY_TILE), whole4, pipeline_mode=once),
                  pl.BlockSpec((1, LP, ATT_KV_HEADS * V_WIDE), whole3, pipeline_mode=once),
                  pl.BlockSpec((1, LP, LANES), whole3, pipeline_mode=once),
                  pl.BlockSpec((ATT_HEADS, 2 * KEY_TILE // LANES, ATT_TILE, LANES), lambda b, i: (0, 0, 0, 0),
                               pipeline_mode=once),
                  pl.BlockSpec((ATT_HEADS, 1, LANES), lambda b, i: (0, 0, 0))],
        out_specs=pl.BlockSpec((1, ATT_TILE, ATT_Q_DIM), tile),
        out_shape=jax.ShapeDtypeStruct((B, LP, ATT_Q_DIM), BF16),
        scratch_shapes=[pltpu.VMEM((nkt, KEY_TILE, ATT_TILE), F32),
                        pltpu.VMEM((nkt, ATT_TILE, KEY_TILE), F32),
                        pltpu.VMEM((ATT_KV_HEADS, ATT_GROUP * ATT_TILE, ATT_HD), BF16),
                        pltpu.VMEM((ATT_KV_HEADS, ATT_GROUP * ATT_TILE, LANES), F32),
                        pltpu.VMEM((ATT_KV_HEADS, ATT_GROUP * ATT_TILE, V_WIDE), F32),
                        pltpu.VMEM((ATT_KV_HEADS, 1, LANES), F32)],
        compiler_params=_compiler_params(("arbitrary", "arbitrary")),
        name="dsa_core",
    )(q, qit, wit, kt, v, ki, bias_near, bias_max)


def _gdn_in_weight(w_in):
    D = w_in.shape[0]
    w = jnp.zeros((D, GDN_PROJ_W), F32).at[:, 0:w_in.shape[1]].set(w_in.astype(F32))
    return w.astype(BF16)


def _dsa_in_weight(w_in):
    D = w_in.shape[0]
    w_in = w_in.astype(BF16)
    zeros = lambda n: jnp.zeros((D, n), BF16)
    src = ATT_Q_DIM + 2 * ATT_KV_DIM
    parts = [w_in[:, 0:src]]
    for h in range(IDX_HEADS):
        parts += [w_in[:, src + h * IDX_HD:src + (h + 1) * IDX_HD], zeros(LANES - IDX_HD)]
    src += IDX_HEADS * IDX_HD
    parts += [w_in[:, src:src + IDX_HD], zeros(LANES - IDX_HD),
              w_in[:, src + IDX_HD:src + IDX_HD + IDX_HEADS], zeros(LANES - IDX_HEADS)]
    w = jnp.concatenate(parts, axis=1)
    assert w.shape[1] == DSA_PROJ_W
    return w


def kernel(x, meta_tokens, norm_mix, norm_ffn, rel_bias, gdn_w_in, gdn_conv, gdn_a_log, gdn_dt_bias,
           gdn_o_norm, gdn_w_out, dsa_w_in, dsa_q_norm, dsa_k_norm, dsa_idx_ln_g, dsa_idx_ln_b, dsa_w_out,
           moe_w_group, moe_b_group, moe_w_expert, moe_b_expert, moe_w_gate_up, moe_w_down):
    B, S, D = x.shape
    depth = norm_mix.shape[0]
    n_select = min(TOPK_MAX, S // 4)
    off = FRONT_PAD + N_META
    LP = -(-(off + S) // ROW_ALIGN) * ROW_ALIGN
    assert LP == off + S, "frames must fill whole attention tiles"
    meta = jnp.broadcast_to(meta_tokens.astype(x.dtype)[None], (B, N_META, D))
    head = jnp.concatenate([jnp.zeros((B, FRONT_PAD, D), x.dtype), meta], axis=1)
    wgu_all = moe_w_gate_up.astype(BF16).reshape(depth * N_EXPERTS, D, 2 * EXPERT_FF)
    wd_all = moe_w_down.astype(BF16).reshape(depth * N_EXPERTS, EXPERT_FF, D)
    h = None
    for i in range(depth):
        j = i // 2
        if i % 2 == 0:
            gdn_args = (norm_mix[i], _gdn_in_weight(gdn_w_in[j]), gdn_conv[j].astype(F32), gdn_a_log[j],
                        gdn_dt_bias[j], gdn_o_norm[j])
            if i == 0:
                o, h = _gdn_core(x, *gdn_args, head=head)
                h = h.reshape(B * LP, D)
            else:
                o = _gdn_core(h.reshape(B, LP, D), *gdn_args)
            h = _matmul_residual(o.reshape(B * LP, GDN_V_DIM), gdn_w_out[j].astype(BF16), h)
        else:
            q, kt, v, qit, ki, wit = _dsa_in(h, norm_mix[i], _dsa_in_weight(dsa_w_in[j]), dsa_q_norm[j],
                                             dsa_k_norm[j], dsa_idx_ln_g[j], dsa_idx_ln_b[j])
            r3 = lambda t: t.reshape(B, LP, t.shape[-1])
            r4 = lambda t: t.reshape(B, LP // KEY_TILE, t.shape[-2], KEY_TILE)
            o = _dsa_core(r3(q), r4(kt), r3(v), r4(qit), r3(ki), r4(wit), rel_bias, n_select, off + S)
            h = _matmul_residual(o.reshape(B * LP, ATT_Q_DIM), dsa_w_out[j].astype(BF16), h)
        h = _moe(h, norm_ffn[i], moe_w_group[i], moe_b_group[i], moe_w_expert[i], moe_b_expert[i],
                 wgu_all, wd_all, i, frames=(LP, S, off) if i == depth - 1 else None)
    return h.reshape(B, S, D)
```
